```python
import math
import jax, jax.numpy as jnp
from jax import lax
import numpy as np

D_MODEL = 2048
BATCH = 2
SEQ = 4096
DEPTH = 1

MLA_HEADS = 8
MLA_Q_RANK = 512
MLA_KV_RANK = 256
MLA_NOPE_DIM = 128
MLA_ROPE_DIM = 64
MLA_V_DIM = 128
ROPE_THETA = 10000.0
MOBA_HEADS = 8
MOBA_HEAD_DIM = 128
MOBA_BLOCK = 256
MOBA_TOPK = 3
Q_BLOCK = 128
T5_BUCKETS = 32
T5_MAX_DISTANCE = 128
MOE_GROUPS = 4
MOE_EXPERTS_PER_GROUP = 8
MOE_N_EXPERTS = MOE_GROUPS * MOE_EXPERTS_PER_GROUP
MOE_TOPK = 2
MOE_D_FF = 512
LN_EPS = 1e-5
RMS_EPS = 1e-6
DEEPNORM_ALPHA = (2.0 * DEPTH) ** 0.25
DEEPNORM_BETA = (8.0 * DEPTH) ** -0.25
MLA_WIDTH = MLA_HEADS * MLA_V_DIM
MOBA_WIDTH = MOBA_HEADS * MOBA_HEAD_DIM
MIX_WIDTH = MLA_WIDTH + MOBA_WIDTH
IN_SPLITS = [MLA_Q_RANK,
             MLA_Q_RANK + MLA_KV_RANK,
             MLA_Q_RANK + MLA_KV_RANK + MLA_ROPE_DIM,
             MLA_Q_RANK + MLA_KV_RANK + MLA_ROPE_DIM + MOBA_WIDTH,
             MLA_Q_RANK + MLA_KV_RANK + MLA_ROPE_DIM + 2 * MOBA_WIDTH]
IN_COLS = MLA_Q_RANK + MLA_KV_RANK + MLA_ROPE_DIM + 3 * MOBA_WIDTH

kernel_name = "hymba_mla_moba_hiermoe_deepnorm_adaln"


def layer_norm_noaffine(x):
    xf = x.astype(jnp.float32)
    mu = xf.mean(-1, keepdims=True)
    var = jnp.square(xf - mu).mean(-1, keepdims=True)
    return (xf - mu) * lax.rsqrt(var + LN_EPS)


def layer_norm(x, g, b):
    return (layer_norm_noaffine(x) * g + b).astype(x.dtype)


def rms_norm(x, g):
    xf = x.astype(jnp.float32)
    return (xf * lax.rsqrt(jnp.square(xf).mean(-1, keepdims=True) + RMS_EPS) * g).astype(x.dtype)


def rope_tables(S):
    inv = 1.0 / (ROPE_THETA ** (jnp.arange(0, MLA_ROPE_DIM, 2, dtype=jnp.float32) / MLA_ROPE_DIM))
    ang = jnp.arange(S, dtype=jnp.float32)[:, None] * inv[None, :]
    return jnp.cos(ang), jnp.sin(ang)


def apply_rope(x, cos, sin):
    x1, x2 = jnp.split(x.astype(jnp.float32), 2, axis=-1)
    return jnp.concatenate([x1 * cos - x2 * sin, x1 * sin + x2 * cos], axis=-1).astype(x.dtype)


def t5_bucket(rel):
    n = jnp.maximum(rel, 0)
    max_exact = T5_BUCKETS // 2
    nf = jnp.maximum(n, 1).astype(jnp.float32)
    large = max_exact + (jnp.log(nf / max_exact) / math.log(T5_MAX_DISTANCE / max_exact)
                         * (T5_BUCKETS - max_exact)).astype(jnp.int32)
    large = jnp.minimum(large, T5_BUCKETS - 1)
    return jnp.where(n < max_exact, n, large)


def mla_attention(q_nope, q_rope, k_nope, k_rope, v):
    B, S, H, Dn = q_nope.shape
    nqb = S // Q_BLOCK
    scale = 1.0 / math.sqrt(MLA_NOPE_DIM + MLA_ROPE_DIM)
    qn = q_nope.reshape(B, nqb, Q_BLOCK, H, Dn).transpose(1, 0, 2, 3, 4)
    qr = q_rope.reshape(B, nqb, Q_BLOCK, H, MLA_ROPE_DIM).transpose(1, 0, 2, 3, 4)
    kpos = jnp.arange(S)

    def attend_block(args):
        qn_b, qr_b, i = args
        s = (jnp.einsum('bqhd,bkhd->bhqk', qn_b, k_nope)
             + jnp.einsum('bqhd,bkd->bhqk', qr_b, k_rope)).astype(jnp.float32) * scale
        qpos = i * Q_BLOCK + jnp.arange(Q_BLOCK)
        s = jnp.where(kpos[None, :] <= qpos[:, None], s, -jnp.inf)
        p = jax.nn.softmax(s, axis=-1).astype(v.dtype)
        return jnp.einsum('bhqk,bkhd->bqhd', p, v)

    out = lax.map(attend_block, (qn, qr, jnp.arange(nqb)))
    return out.transpose(1, 0, 2, 3, 4).reshape(B, S, H * MLA_V_DIM)


def moba_attention(q, k, v, t5_table):
    B, S, H, Dh = q.shape
    nb = max(-(-S // MOBA_BLOCK), MOBA_TOPK)
    s_pad = nb * MOBA_BLOCK
    nqb = S // Q_BLOCK
    scale = 1.0 / math.sqrt(Dh)
    qf = q.transpose(0, 2, 1, 3).reshape(B * H, S, Dh)
    kf = k.transpose(0, 2, 1, 3).reshape(B * H, S, Dh)
    vf = v.transpose(0, 2, 1, 3).reshape(B * H, S, Dh)
    pad = ((0, 0), (0, s_pad - S), (0, 0))
    kb = jnp.pad(kf, pad).reshape(B * H, nb, MOBA_BLOCK, Dh)
    vb = jnp.pad(vf, pad).reshape(B * H, nb, MOBA_BLOCK, Dh)
    k_mean = kb.astype(jnp.float32).mean(axis=2)
    gate = jnp.einsum('zsd,znd->zsn', qf.astype(jnp.float32), k_mean)
    q_blk = jnp.arange(S) // MOBA_BLOCK
    fully_past = jnp.arange(nb)[None, :] < q_blk[:, None]
    gate = jnp.where(fully_past[None], gate, -jnp.inf)
    _, top_idx = lax.top_k(gate, MOBA_TOPK)
    sel = jnp.concatenate(
        [top_idx, jnp.broadcast_to(q_blk[None, :, None], (B * H, S, 1))], axis=-1)
    n_items = B * H * nqb
    xs = (qf.reshape(n_items, Q_BLOCK, Dh),
          sel.reshape(n_items, Q_BLOCK, MOBA_TOPK + 1),
          jnp.repeat(jnp.arange(B * H), nqb),
          jnp.tile(jnp.arange(nqb), B * H))
    offs = jnp.arange(MOBA_BLOCK)
    slot_rank = jnp.arange(MOBA_TOPK + 1)

    def attend_block(args):
        q_b, sel_b, bh, i = args
        kg = kb[bh, sel_b].reshape(Q_BLOCK, (MOBA_TOPK + 1) * MOBA_BLOCK, Dh)
        vg = vb[bh, sel_b].reshape(Q_BLOCK, (MOBA_TOPK + 1) * MOBA_BLOCK, Dh)
        qpos = i * Q_BLOCK + jnp.arange(Q_BLOCK)
        kpos = (sel_b[:, :, None] * MOBA_BLOCK + offs).reshape(Q_BLOCK, -1)
        slot_ok = (slot_rank[None, :] < (qpos // MOBA_BLOCK)[:, None]) | (slot_rank[None, :] == MOBA_TOPK)
        mask = jnp.repeat(slot_ok, MOBA_BLOCK, axis=1) & (kpos <= qpos[:, None])
        bias = t5_table[t5_bucket(qpos[:, None] - kpos), bh % H].astype(jnp.float32)
        s = jnp.einsum('qd,qkd->qk', q_b, kg).astype(jnp.float32) * scale + bias
        s = jnp.where(mask, s, -jnp.inf)
        p = jax.nn.softmax(s, axis=-1).astype(vg.dtype)
        return jnp.einsum('qk,qkd->qd', p, vg)

    out = lax.map(attend_block, xs)
    return out.reshape(B, H, S, Dh).transpose(0, 2, 1, 3).reshape(B, S, H * Dh)


def mixer(h, w_in, q_norm_g, w_uq, kv_norm_g, w_ukv, w_out, t5_table, cos, sin):
    B, S, _ = h.shape
    proj = h @ w_in
    c_q, c_kv, k_rope, q_mo, k_mo, v_mo = jnp.split(proj, IN_SPLITS, axis=-1)
    q = (rms_norm(c_q, q_norm_g) @ w_uq).reshape(B, S, MLA_HEADS, MLA_NOPE_DIM + MLA_ROPE_DIM)
    q_nope, q_rope = q[..., :MLA_NOPE_DIM], q[..., MLA_NOPE_DIM:]
    kv = (rms_norm(c_kv, kv_norm_g) @ w_ukv).reshape(B, S, MLA_HEADS, MLA_NOPE_DIM + MLA_V_DIM)
    k_nope, v = kv[..., :MLA_NOPE_DIM], kv[..., MLA_NOPE_DIM:]
    q_rope = apply_rope(q_rope, cos[None, :, None, :], sin[None, :, None, :])
    k_rope = apply_rope(k_rope, cos[None], sin[None])
    o_mla = mla_attention(q_nope, q_rope, k_nope, k_rope, v)
    shp = (B, S, MOBA_HEADS, MOBA_HEAD_DIM)
    o_moba = moba_attention(q_mo.reshape(shp), k_mo.reshape(shp), v_mo.reshape(shp), t5_table)
    return jnp.concatenate([o_mla, o_moba], axis=-1) @ w_out


def hier_moe(h, w_rg, b_rg, w_re, b_re, w1, w3, w2):
    B, S, D = h.shape
    T = B * S
    hf = h.reshape(T, D)
    g_prob = jax.nn.softmax((hf @ w_rg + b_rg).astype(jnp.float32), axis=-1)
    g_p, g_idx = lax.top_k(g_prob, 1)
    e_logits = (hf @ w_re + b_re).astype(jnp.float32).reshape(T, MOE_GROUPS, MOE_EXPERTS_PER_GROUP)
    idx = jnp.broadcast_to(g_idx[:, :, None], (T, 1, MOE_EXPERTS_PER_GROUP))
    e_sel = jnp.take_along_axis(e_logits, idx, axis=1)[:, 0]
    e_p, e_idx = lax.top_k(jax.nn.softmax(e_sel, axis=-1), MOE_TOPK)
    e_p = e_p / e_p.sum(-1, keepdims=True)
    weights = g_p * e_p
    expert_id = g_idx * MOE_EXPERTS_PER_GROUP + e_idx
    gates = jnp.zeros((T, MOE_N_EXPERTS), jnp.float32).at[
        jnp.arange(T)[:, None], expert_id].add(weights).astype(h.dtype)
    hid = jax.nn.silu(jnp.einsum('td,edf->tef', hf, w1)) * jnp.einsum('td,edf->tef', hf, w3)
    y = jnp.einsum('tef,efd->td', hid * gates[:, :, None], w2)
    return y.reshape(B, S, D)


def setup_inputs(seed: int = 0) -> dict:
    key = jax.random.key(seed)
    ks = jax.random.split(key, 24)

    def nrm(k, shape, scale):
        return jax.random.normal(k, shape, jnp.float32) * scale

    D = D_MODEL
    return {
        "x": nrm(ks[0], (BATCH, SEQ, D), 1.0),
        "c": nrm(ks[1], (BATCH, D), 1.0),
        "w_ada": nrm(ks[2], (DEPTH, D, 6 * D), 0.5 * D ** -0.5),
        "b_ada": nrm(ks[3], (DEPTH, 6 * D), 0.01),
        "w_in": nrm(ks[4], (DEPTH, D, IN_COLS), D ** -0.5),
        "q_norm_g": 1.0 + nrm(ks[5], (DEPTH, MLA_Q_RANK), 0.01),
        "w_uq": nrm(ks[6], (DEPTH, MLA_Q_RANK, MLA_HEADS * (MLA_NOPE_DIM + MLA_ROPE_DIM)), MLA_Q_RANK ** -0.5),
        "kv_norm_g": 1.0 + nrm(ks[7], (DEPTH, MLA_KV_RANK), 0.01),
        "w_ukv": nrm(ks[8], (DEPTH, MLA_KV_RANK, MLA_HEADS * (MLA_NOPE_DIM + MLA_V_DIM)), MLA_KV_RANK ** -0.5),
        "w_out": nrm(ks[9], (DEPTH, MIX_WIDTH, D), DEEPNORM_BETA * MIX_WIDTH ** -0.5),
        "t5_table": nrm(ks[10], (T5_BUCKETS, MOBA_HEADS), 0.5),
        "ln1_g": 1.0 + nrm(ks[11], (DEPTH, D), 0.01),
        "ln1_b": nrm(ks[12], (DEPTH, D), 0.01),
        "w_router_group": nrm(ks[13], (DEPTH, D, MOE_GROUPS), D ** -0.5),
        "b_router_group": nrm(ks[14], (DEPTH, MOE_GROUPS), 0.01),
        "w_router_expert": nrm(ks[15], (DEPTH, D, MOE_N_EXPERTS), D ** -0.5),
        "b_router_expert": nrm(ks[16], (DEPTH, MOE_N_EXPERTS), 0.01),
        "w1": nrm(ks[17], (DEPTH, MOE_N_EXPERTS, D, MOE_D_FF), D ** -0.5),
        "w3": nrm(ks[18], (DEPTH, MOE_N_EXPERTS, D, MOE_D_FF), D ** -0.5),
        "w2": nrm(ks[19], (DEPTH, MOE_N_EXPERTS, MOE_D_FF, D), DEEPNORM_BETA * MOE_D_FF ** -0.5),
        "ln2_g": 1.0 + nrm(ks[20], (DEPTH, D), 0.01),
        "ln2_b": nrm(ks[21], (DEPTH, D), 0.01),
    }


def reference(x, c, w_ada, b_ada, w_in, q_norm_g, w_uq, kv_norm_g, w_ukv, w_out, t5_table,
              ln1_g, ln1_b, w_router_group, b_router_group, w_router_expert, b_router_expert,
              w1, w3, w2, ln2_g, ln2_b):
    S = x.shape[1]
    cos, sin = rope_tables(S)
    c_act = jax.nn.silu(c)
    for l in range(DEPTH):
        mod = c_act @ w_ada[l] + b_ada[l]
        sh1, sc1, g1, sh2, sc2, g2 = [m[:, None, :] for m in jnp.split(mod, 6, axis=-1)]
        h = (layer_norm_noaffine(x) * (1.0 + sc1) + sh1).astype(x.dtype)
        y = mixer(h, w_in[l], q_norm_g[l], w_uq[l], kv_norm_g[l], w_ukv[l], w_out[l], t5_table, cos, sin)
        x = layer_norm(DEEPNORM_ALPHA * x + g1 * y, ln1_g[l], ln1_b[l])
        h = (layer_norm_noaffine(x) * (1.0 + sc2) + sh2).astype(x.dtype)
        y = hier_moe(h, w_router_group[l], b_router_group[l], w_router_expert[l], b_router_expert[l],
                     w1[l], w3[l], w2[l])
        x = layer_norm(DEEPNORM_ALPHA * x + g2 * y, ln2_g[l], ln2_b[l])
    return x
```

```python
import functools
import math

import jax
import jax.numpy as jnp
from jax import lax
from jax.experimental import pallas as pl
from jax.experimental.pallas import tpu as pltpu

D_MODEL = 2048
MLA_HEADS = 8
MLA_Q_RANK = 512
MLA_KV_RANK = 256
MLA_NOPE_DIM = 128
MLA_ROPE_DIM = 64
MLA_V_DIM = 128
ROPE_THETA = 10000.0
MOBA_HEADS = 8
MOBA_HEAD_DIM = 128
MOBA_BLOCK = 256
MOBA_TOPK = 3
T5_BUCKETS = 32
T5_MAX_DISTANCE = 128
MOE_GROUPS = 4
MOE_EXPERTS_PER_GROUP = 8
MOE_N_EXPERTS = MOE_GROUPS * MOE_EXPERTS_PER_GROUP
MOE_D_FF = 512
LN_EPS = 1e-5
RMS_EPS = 1e-6
MOBA_WIDTH = MOBA_HEADS * MOBA_HEAD_DIM

LANES = 128
QK_PAD = 256
PART_A = 1024
NEG = -1e30
VMEM_LIMIT = 56 * 1024 * 1024

F32 = jnp.float32
BF16 = jnp.bfloat16


def _cparams(sem):
    return pltpu.CompilerParams(dimension_semantics=sem, vmem_limit_bytes=VMEM_LIMIT)


def _ln(x):
    mu = jnp.mean(x, axis=-1, keepdims=True)
    xc = x - mu
    var = jnp.mean(xc * xc, axis=-1, keepdims=True)
    return xc * lax.rsqrt(var + LN_EPS)


def _nt_dot(a, b):
    return lax.dot_general(a, b, (((1,), (1,)), ((), ())), preferred_element_type=F32)


def _ada_kernel(ct_ref, w_ref, b_ref, o_ref, *, batch):
    ct = ct_ref[...]
    ca = ct / (1.0 + jnp.exp(-ct))
    w = w_ref[...]
    rows = [jnp.sum(w * ca[:, b:b + 1], axis=0, keepdims=True) for b in range(batch)]
    o_ref[...] = jnp.concatenate(rows, axis=0) + b_ref[...]


def _ada_mod(c, w_ada, b_ada):
    batch, d = c.shape
    n = w_ada.shape[1]
    tn = 512
    ct = jnp.zeros((d, LANES), F32).at[:, :batch].set(c.T)
    return pl.pallas_call(
        functools.partial(_ada_kernel, batch=batch),
        grid=(n // tn,),
        in_specs=[pl.BlockSpec((d, LANES), lambda j: (0, 0)),
                  pl.BlockSpec((d, tn), lambda j: (0, j)),
                  pl.BlockSpec((1, tn), lambda j: (0, j))],
        out_specs=pl.BlockSpec((batch, tn), lambda j: (0, j)),
        out_shape=jax.ShapeDtypeStruct((batch, n), F32),
        compiler_params=_cparams(("arbitrary",)),
        name="ada_mod",
    )(ct, w_ada, b_ada.reshape(1, n))


def _inproj_kernel(x_ref, mod_ref, w_ref, a_ref, b_ref, h_scr):
    j = pl.program_id(1)

    @pl.when(j == 0)
    def _():
        h = _ln(x_ref[...]) * (1.0 + mod_ref[0, 1:2, :]) + mod_ref[0, 0:1, :]
        h_scr[...] = h.astype(BF16)
        a_ref[...] = jnp.dot(h_scr[...], w_ref[...], preferred_element_type=F32)

    @pl.when(j > 0)
    def _():
        b_ref[...] = jnp.dot(h_scr[...], w_ref[...], preferred_element_type=F32).astype(BF16)


def _in_proj(xf, mod3, w4, seq):
    t, d = xf.shape
    n = w4.shape[1]
    tm, tn = 512, PART_A
    per_b = seq // tm
    return pl.pallas_call(
        _inproj_kernel,
        grid=(t // tm, n // tn),
        in_specs=[pl.BlockSpec((tm, d), lambda i, j: (i, 0)),
                  pl.BlockSpec((1, 6, d), lambda i, j: (i // per_b, 0, 0)),
                  pl.BlockSpec((d, tn), lambda i, j: (0, j))],
        out_specs=[pl.BlockSpec((tm, tn), lambda i, j: (i, 0)),
                   pl.BlockSpec((tm, tn), lambda i, j: (i, jnp.maximum(j - 1, 0)))],
        out_shape=[jax.ShapeDtypeStruct((t, PART_A), F32),
                   jax.ShapeDtypeStruct((t, n - PART_A), BF16)],
        scratch_shapes=[pltpu.VMEM((tm, d), BF16)],
        compiler_params=_cparams(("arbitrary", "arbitrary")),
        name="in_proj",
    )(xf, mod3, w4)


def _qkv_kernel(a_ref, wq_ref, wkv_ref, gq_ref, gkv_ref, cos_ref, sin_ref, q_ref, k_ref, v_ref):
    a = a_ref[...]
    cq = a[:, :MLA_Q_RANK]
    ckv = a[:, MLA_Q_RANK:MLA_Q_RANK + MLA_KV_RANK]
    kr = a[:, 768:896]
    krs = a[:, 896:1024]
    cqn = (cq * lax.rsqrt(jnp.mean(cq * cq, axis=-1, keepdims=True) + RMS_EPS) * gq_ref[...]).astype(BF16)
    ckvn = (ckv * lax.rsqrt(jnp.mean(ckv * ckv, axis=-1, keepdims=True) + RMS_EPS) * gkv_ref[...]).astype(BF16)
    q3 = jnp.dot(cqn, wq_ref[...], preferred_element_type=F32)
    kv = jnp.dot(ckvn, wkv_ref[...], preferred_element_type=F32)
    cos = cos_ref[...]
    sin = sin_ref[...]
    krr = (kr * cos + krs * sin).astype(BF16)
    for h in range(MLA_HEADS):
        q0 = h * 3 * LANES
        c0 = h * QK_PAD
        q_ref[:, c0:c0 + LANES] = q3[:, q0:q0 + LANES].astype(BF16)
        q_ref[:, c0 + LANES:c0 + QK_PAD] = (
            q3[:, q0 + LANES:q0 + 2 * LANES] * cos + q3[:, q0 + 2 * LANES:q0 + 3 * LANES] * sin).astype(BF16)
        k_ref[:, c0:c0 + LANES] = kv[:, c0:c0 + LANES].astype(BF16)
        k_ref[:, c0 + LANES:c0 + QK_PAD] = krr
        v_ref[:, h * LANES:(h + 1) * LANES] = kv[:, c0 + LANES:c0 + QK_PAD].astype(BF16)


def _qkv(part_a, wq3, wkv, gq, gkv, cos_t, sin_t, seq):
    t = part_a.shape[0]
    tm = 512
    per_b = seq // tm
    hq = MLA_HEADS * QK_PAD
    return pl.pallas_call(
        _qkv_kernel,
        grid=(t // tm,),
        in_specs=[pl.BlockSpec((tm, PART_A), lambda i: (i, 0)),
                  pl.BlockSpec(wq3.shape, lambda i: (0, 0)),
                  pl.BlockSpec(wkv.shape, lambda i: (0, 0)),
                  pl.BlockSpec((1, MLA_Q_RANK), lambda i: (0, 0)),
                  pl.BlockSpec((1, MLA_KV_RANK), lambda i: (0, 0)),
                  pl.BlockSpec((tm, LANES), lambda i: (i % per_b, 0)),
                  pl.BlockSpec((tm, LANES), lambda i: (i % per_b, 0))],
        out_specs=[pl.BlockSpec((tm, hq), lambda i: (i, 0)),
                   pl.BlockSpec((tm, hq), lambda i: (i, 0)),
                   pl.BlockSpec((tm, MLA_HEADS * MLA_V_DIM), lambda i: (i, 0))],
        out_shape=[jax.ShapeDtypeStruct((t, hq), BF16),
                   jax.ShapeDtypeStruct((t, hq), BF16),
                   jax.ShapeDtypeStruct((t, MLA_HEADS * MLA_V_DIM), BF16)],
        compiler_params=_cparams(("arbitrary",)),
        name="qkv",
    )(part_a, wq3, wkv, gq, gkv, cos_t, sin_t)


def _softmax_step(carry, s, v):
    m, l, acc = carry
    m_new = jnp.maximum(m, jnp.max(s, axis=-1, keepdims=True))
    alpha = jnp.exp(m - m_new)
    p = jnp.exp(s - m_new)
    l_new = alpha * l + jnp.sum(p, axis=-1, keepdims=True)
    acc_new = alpha * acc + jnp.dot(p.astype(BF16), v, preferred_element_type=F32)
    return m_new, l_new, acc_new


def _softmax_first(s, v):
    m = jnp.max(s, axis=-1, keepdims=True)
    p = jnp.exp(s - m)
    return m, jnp.sum(p, axis=-1, keepdims=True), jnp.dot(p.astype(BF16), v, preferred_element_type=F32)


def _mla_kernel(q_ref, k_ref, v_ref, o_ref, *, tile, scale):
    qi = pl.program_id(2)
    q = q_ref[...]
    row = lax.broadcasted_iota(jnp.int32, (tile, tile), 0)
    col = lax.broadcasted_iota(jnp.int32, (tile, tile), 1)
    d0 = pl.multiple_of(qi * tile, tile)
    s = _nt_dot(q, k_ref[pl.ds(d0, tile), :]) * scale
    s = jnp.where(row >= col, s, NEG)
    carry = _softmax_first(s, v_ref[pl.ds(d0, tile), :])

    def body(n, carry):
        k0 = pl.multiple_of(n * tile, tile)
        s = _nt_dot(q, k_ref[pl.ds(k0, tile), :]) * scale
        return _softmax_step(carry, s, v_ref[pl.ds(k0, tile), :])

    m, l, acc = lax.fori_loop(0, qi, body, carry)
    o_ref[...] = (acc / l).astype(o_ref.dtype)


def _mla_attention(q, k, v, batch, seq):
    tile = 512
    nq = seq // tile
    scale = 1.0 / math.sqrt(MLA_NOPE_DIM + MLA_ROPE_DIM)
    return pl.pallas_call(
        functools.partial(_mla_kernel, tile=tile, scale=scale),
        grid=(batch, MLA_HEADS, nq),
        in_specs=[pl.BlockSpec((tile, QK_PAD), lambda b, h, i: (b * nq + i, h)),
                  pl.BlockSpec((seq, QK_PAD), lambda b, h, i: (b, h)),
                  pl.BlockSpec((seq, MLA_V_DIM), lambda b, h, i: (b, h))],
        out_specs=pl.BlockSpec((tile, MLA_V_DIM), lambda b, h, i: (b * nq + i, h)),
        out_shape=jax.ShapeDtypeStruct((batch * seq, MLA_HEADS * MLA_V_DIM), BF16),
        compiler_params=_cparams(("arbitrary", "arbitrary", "arbitrary")),
        name="mla_attn",
    )(q, k, v)


def _moba_select_kernel(q_ref, k_ref, sel_ref, *, seq, nb):
    kf = k_ref[...].astype(F32)
    km = jnp.sum(kf.reshape(nb, MOBA_BLOCK, MOBA_HEAD_DIM), axis=1) * (1.0 / MOBA_BLOCK)
    km = jnp.concatenate([km, jnp.zeros((LANES - nb, MOBA_HEAD_DIM), F32)], axis=0)
    km_hi = km.astype(BF16)
    km_lo = (km - km_hi.astype(F32)).astype(BF16)
    q = q_ref[...]
    gate = _nt_dot(q, km_hi) + _nt_dot(q, km_lo)
    blk = lax.broadcasted_iota(jnp.int32, (seq, LANES), 1)
    qblk = jnp.right_shift(lax.broadcasted_iota(jnp.int32, (seq, LANES), 0), MOBA_BLOCK.bit_length() - 1)
    g = jnp.where(blk < qblk, gate, NEG)
    chosen = jnp.zeros((seq, LANES), jnp.bool_)
    for _ in range(MOBA_TOPK):
        mx = jnp.max(g, axis=-1, keepdims=True)
        first = jnp.min(jnp.where(g == mx, blk, LANES), axis=-1, keepdims=True)
        pick = (blk == first) & (mx > 0.5 * NEG)
        chosen = chosen | pick
        g = jnp.where(pick, NEG, g)
    sel_ref[0, 0] = jnp.where(chosen, 0.0, NEG)[:, :nb]


def _moba_select(qkv_mo, batch, seq):
    nb = seq // MOBA_BLOCK
    return pl.pallas_call(
        functools.partial(_moba_select_kernel, seq=seq, nb=nb),
        grid=(batch, MOBA_HEADS),
        in_specs=[pl.BlockSpec((seq, MOBA_HEAD_DIM), lambda b, h: (b, h)),
                  pl.BlockSpec((seq, MOBA_HEAD_DIM), lambda b, h: (b, MOBA_HEADS + h))],
        out_specs=pl.BlockSpec((1, 1, seq, nb), lambda b, h: (b, h, 0, 0)),
        out_shape=jax.ShapeDtypeStruct((batch, MOBA_HEADS, seq, nb), F32),
        compiler_params=_cparams(("arbitrary", "arbitrary")),
        name="moba_select",
    )(qkv_mo, qkv_mo)


def _t5_kernel(tab_ref, o_ref):
    h = pl.program_id(0)
    r = lax.broadcasted_iota(jnp.int32, (MOBA_BLOCK, MOBA_BLOCK), 0)
    c = lax.broadcasted_iota(jnp.int32, (MOBA_BLOCK, MOBA_BLOCK), 1)
    max_exact = T5_BUCKETS // 2
    far = tab_ref[T5_BUCKETS - 1, h]
    for d in range(2):
        n = jnp.maximum(d * MOBA_BLOCK + r - c, 0)
        nf = jnp.maximum(n, 1).astype(F32)
        large = max_exact + (jnp.log(nf / max_exact) / math.log(T5_MAX_DISTANCE / max_exact)
                             * (T5_BUCKETS - max_exact)).astype(jnp.int32)
        large = jnp.minimum(large, T5_BUCKETS - 1)
        bucket = jnp.where(n < max_exact, n, large)
        bias = jnp.zeros((MOBA_BLOCK, MOBA_BLOCK), F32)
        for j in range(T5_BUCKETS):
            bias = jnp.where(bucket == j, tab_ref[j, h], bias)
        o_ref[0, d] = bias - far


def _t5_tiles(t5_table):
    assert MOBA_BLOCK >= T5_MAX_DISTANCE
    return pl.pallas_call(
        _t5_kernel,
        grid=(MOBA_HEADS,),
        in_specs=[pl.BlockSpec(memory_space=pltpu.SMEM)],
        out_specs=pl.BlockSpec((1, 2, MOBA_BLOCK, MOBA_BLOCK), lambda h: (h, 0, 0, 0)),
        out_shape=jax.ShapeDtypeStruct((MOBA_HEADS, 2, MOBA_BLOCK, MOBA_BLOCK), F32),
        compiler_params=_cparams(("arbitrary",)),
        name="t5_tiles",
    )(t5_table)


def _moba_kernel(q_ref, k_ref, v_ref, sel_ref, bias_ref, o_ref, *, nb, scale):
    tile = MOBA_BLOCK
    qi = pl.program_id(2)
    q = q_ref[...]
    sel = sel_ref[0, 0]
    blk = lax.broadcasted_iota(jnp.int32, (tile, nb), 1)
    row = lax.broadcasted_iota(jnp.int32, (tile, tile), 0)
    col = lax.broadcasted_iota(jnp.int32, (tile, tile), 1)

    def sel_col(n):
        return jnp.sum(jnp.where(blk == n, sel, 0.0), axis=-1, keepdims=True)

    d0 = pl.multiple_of(qi * tile, tile)
    s = _nt_dot(q, k_ref[pl.ds(d0, tile), :]) * scale + bias_ref[0, 0]
    s = jnp.where(row >= col, s, NEG)
    carry = _softmax_first(s, v_ref[pl.ds(d0, tile), :])
    pn = jnp.maximum(qi - 1, 0)
    p0 = pl.multiple_of(pn * tile, tile)
    s = _nt_dot(q, k_ref[pl.ds(p0, tile), :]) * scale + bias_ref[0, 1] + sel_col(pn)
    carry = _softmax_step(carry, s, v_ref[pl.ds(p0, tile), :])

    def body(n, carry):
        k0 = pl.multiple_of(n * tile, tile)
        s = _nt_dot(q, k_ref[pl.ds(k0, tile), :]) * scale + sel_col(n)
        return _softmax_step(carry, s, v_ref[pl.ds(k0, tile), :])

    m, l, acc = lax.fori_loop(0, jnp.maximum(qi - 1, 0), body, carry)
    o_ref[...] = (acc / l).astype(o_ref.dtype)


def _moba_attention(qkv_mo, sel, bias, batch, seq):
    tile = MOBA_BLOCK
    nq = seq // tile
    nb = seq // MOBA_BLOCK
    scale = 1.0 / math.sqrt(MOBA_HEAD_DIM)
    dh = MOBA_HEAD_DIM
    return pl.pallas_call(
        functools.partial(_moba_kernel, nb=nb, scale=scale),
        grid=(batch, MOBA_HEADS, nq),
        in_specs=[pl.BlockSpec((tile, dh), lambda b, h, i: (b * nq + i, h)),
                  pl.BlockSpec((seq, dh), lambda b, h, i: (b, MOBA_HEADS + h)),
                  pl.BlockSpec((seq, dh), lambda b, h, i: (b, 2 * MOBA_HEADS + h)),
                  pl.BlockSpec((1, 1, tile, nb), lambda b, h, i: (b, h, i, 0)),
                  pl.BlockSpec((1, 2, tile, tile), lambda b, h, i: (h, 0, 0, 0))],
        out_specs=pl.BlockSpec((tile, dh), lambda b, h, i: (b * nq + i, h)),
        out_shape=jax.ShapeDtypeStruct((batch * seq, MOBA_WIDTH), BF16),
        compiler_params=_cparams(("arbitrary", "arbitrary", "arbitrary")),
        name="moba_attn",
    )(qkv_mo, qkv_mo, qkv_mo, sel, bias)


def _outproj_kernel(oa_ref, ob_ref, wa_ref, wb_ref, x_ref, mod_ref, g_ref, b_ref, wr_ref,
                    x1_ref, h2_ref, lg_ref, *, alpha):
    y = (jnp.dot(oa_ref[...], wa_ref[...], preferred_element_type=F32)
         + jnp.dot(ob_ref[...], wb_ref[...], preferred_element_type=F32))
    z = alpha * x_ref[...] + mod_ref[0, 2:3, :] * y
    x1 = _ln(z) * g_ref[...] + b_ref[...]
    x1_ref[...] = x1
    h2 = _ln(x1) * (1.0 + mod_ref[0, 4:5, :]) + mod_ref[0, 3:4, :]
    h2_ref[...] = h2
    h_hi = h2.astype(BF16)
    h_lo = (h2 - h_hi.astype(F32)).astype(BF16)
    zz = (jnp.dot(h_hi, wr_ref[...], preferred_element_type=F32)
          + jnp.dot(h_lo, wr_ref[...], preferred_element_type=F32))
    lg_ref[...] = zz + pltpu.roll(zz, LANES // 2, 1)


def _out_proj(o_mla, o_moba, wa, wb, xf, mod3, ln_g, ln_b, wr, seq, alpha):
    t, d = xf.shape
    tm = 256
    per_b = seq // tm
    ka, kb = wa.shape[0], wb.shape[0]
    return pl.pallas_call(
        functools.partial(_outproj_kernel, alpha=alpha),
        grid=(t // tm,),
        in_specs=[pl.BlockSpec((tm, ka), lambda i: (i, 0)),
                  pl.BlockSpec((tm, kb), lambda i: (i, 0)),
                  pl.BlockSpec((ka, d), lambda i: (0, 0)),
                  pl.BlockSpec((kb, d), lambda i: (0, 0)),
                  pl.BlockSpec((tm, d), lambda i: (i, 0)),
                  pl.BlockSpec((1, 6, d), lambda i: (i // per_b, 0, 0)),
                  pl.BlockSpec((1, d), lambda i: (0, 0)),
                  pl.BlockSpec((1, d), lambda i: (0, 0)),
                  pl.BlockSpec((d, LANES), lambda i: (0, 0))],
        out_specs=[pl.BlockSpec((tm, d), lambda i: (i, 0)),
                   pl.BlockSpec((tm, d), lambda i: (i, 0)),
                   pl.BlockSpec((tm, LANES), lambda i: (i, 0))],
        out_shape=[jax.ShapeDtypeStruct((t, d), F32),
                   jax.ShapeDtypeStruct((t, d), F32),
                   jax.ShapeDtypeStruct((t, LANES), F32)],
        compiler_params=_cparams(("arbitrary",)),
        name="out_proj",
    )(o_mla, o_moba, wa, wb, xf, mod3, ln_g, ln_b, wr)


def _route_kernel(lg_ref, br_ref, info_ref, cnt_ref, run_scr, *, tm):
    i = pl.program_id(0)

    @pl.when(i == 0)
    def _():
        run_scr[...] = jnp.zeros_like(run_scr)

    lg = lg_ref[...] + br_ref[...]
    lane = lax.broadcasted_iota(jnp.int32, (tm, LANES), 1)
    e_lo, e_hi = MOE_GROUPS, MOE_GROUPS + MOE_N_EXPERTS
    is_g = lane < e_lo
    gl = jnp.where(is_g, lg, NEG)
    gmax = jnp.max(gl, axis=-1, keepdims=True)
    gidx = jnp.min(jnp.where(gl == gmax, lane, LANES), axis=-1, keepdims=True)
    g_p = 1.0 / jnp.sum(jnp.where(is_g, jnp.exp(gl - gmax), 0.0), axis=-1, keepdims=True)
    grp_of_lane = jnp.right_shift(lane - e_lo, MOE_EXPERTS_PER_GROUP.bit_length() - 1)
    in_grp = (lane >= e_lo) & (lane < e_hi) & (grp_of_lane == gidx)
    el = jnp.where(in_grp, lg, NEG)
    m1 = jnp.max(el, axis=-1, keepdims=True)
    l1 = jnp.min(jnp.where(el == m1, lane, LANES), axis=-1, keepdims=True)
    el2 = jnp.where(lane == l1, NEG, el)
    m2 = jnp.max(el2, axis=-1, keepdims=True)
    l2 = jnp.min(jnp.where(el2 == m2, lane, LANES), axis=-1, keepdims=True)
    zsum = jnp.sum(jnp.where(in_grp, jnp.exp(el - m1), 0.0), axis=-1, keepdims=True)
    p1 = 1.0 / zsum
    p2 = jnp.exp(m2 - m1) / zsum
    wa = g_p * (p1 / (p1 + p2))
    wb = g_p * (p2 / (p1 + p2))
    hot_a = lane == l1
    hot_b = lane == l2
    onehot = jnp.where(hot_a | hot_b, 1.0, 0.0)
    r = lax.broadcasted_iota(jnp.int32, (tm, tm), 0)
    c = lax.broadcasted_iota(jnp.int32, (tm, tm), 1)
    lower = jnp.where(c < r, 1.0, 0.0).astype(BF16)
    before = jnp.dot(lower, onehot.astype(BF16), preferred_element_type=F32) + run_scr[...]
    rank_a = jnp.sum(jnp.where(hot_a, before, 0.0), axis=-1, keepdims=True)
    rank_b = jnp.sum(jnp.where(hot_b, before, 0.0), axis=-1, keepdims=True)
    run_scr[...] += jnp.sum(onehot, axis=0, keepdims=True)
    info = jnp.zeros((tm, LANES), F32)
    for k, val in enumerate([(l1 - e_lo).astype(F32), (l2 - e_lo).astype(F32), wa, wb, rank_a, rank_b]):
        info = jnp.where(lane == k, val, info)
    info_ref[...] = info
    cnt_ref[...] = run_scr[...]


def _route(logits, br):
    t = logits.shape[0]
    tm = 512
    return pl.pallas_call(
        functools.partial(_route_kernel, tm=tm),
        grid=(t // tm,),
        in_specs=[pl.BlockSpec((tm, LANES), lambda i: (i, 0)),
                  pl.BlockSpec((1, LANES), lambda i: (0, 0))],
        out_specs=[pl.BlockSpec((tm, LANES), lambda i: (i, 0)),
                   pl.BlockSpec((1, LANES), lambda i: (0, 0))],
        out_shape=[jax.ShapeDtypeStruct((t, LANES), F32),
                   jax.ShapeDtypeStruct((1, LANES), F32)],
        scratch_shapes=[pltpu.VMEM((1, LANES), F32)],
        compiler_params=_cparams(("arbitrary",)),
        name="route",
    )(logits, br)


def _pos_kernel(info_ref, start_ref, pos_ref):
    info = info_ref[...]
    tm = info.shape[0]
    lane = lax.broadcasted_iota(jnp.int32, (tm, LANES), 1)
    start = start_ref[...]
    cols = []
    for k in range(2):
        e = jnp.sum(jnp.where(lane == k, info, 0.0), axis=-1, keepdims=True).astype(jnp.int32)
        rank = jnp.sum(jnp.where(lane == 4 + k, info, 0.0), axis=-1, keepdims=True)
        base = jnp.sum(jnp.where(lane == e + MOE_GROUPS, start, 0.0), axis=-1, keepdims=True)
        cols.append(base + rank)
    pos_ref[...] = jnp.where(lane == 0, cols[0], jnp.where(lane == 1, cols[1], 0.0)).astype(jnp.int32)


def _positions(info, start_lanes):
    t = info.shape[0]
    tm = 1024
    return pl.pallas_call(
        _pos_kernel,
        grid=(t // tm,),
        in_specs=[pl.BlockSpec((tm, LANES), lambda i: (i, 0)),
                  pl.BlockSpec((1, LANES), lambda i: (0, 0))],
        out_specs=pl.BlockSpec((tm, LANES), lambda i: (i, 0)),
        out_shape=jax.ShapeDtypeStruct((t, LANES), jnp.int32),
        compiler_params=_cparams(("arbitrary",)),
        name="positions",
    )(info, start_lanes)


def _row_copy(src_ref, src_row, dst_ref, dst_row, sem):
    return pltpu.make_async_copy(src_ref.at[pl.ds(src_row, 1)], dst_ref.at[pl.ds(dst_row, 1)], sem)


def _dispatch_kernel(pa_ref, pb_ref, h_ref, xs_in_ref, xs_ref, sem, *, tm):
    del xs_in_ref
    base = pl.program_id(0) * tm

    def issue(t, _):
        _row_copy(h_ref, t, xs_ref, pa_ref[base + t], sem).start()
        _row_copy(h_ref, t, xs_ref, pb_ref[base + t], sem).start()
        return 0

    lax.fori_loop(0, tm, issue, 0)

    def drain(t, _):
        _row_copy(h_ref, 0, xs_ref, 0, sem).wait()
        _row_copy(h_ref, 0, xs_ref, 0, sem).wait()
        return 0

    lax.fori_loop(0, tm, drain, 0)


def _dispatch(pos_a, pos_b, h2, rows):
    t, d = h2.shape
    tm = 256
    xs0 = jnp.zeros((rows, d), F32)
    grid_spec = pltpu.PrefetchScalarGridSpec(
        num_scalar_prefetch=2,
        grid=(t // tm,),
        in_specs=[pl.BlockSpec((tm, d), lambda i, pa, pb: (i, 0)),
                  pl.BlockSpec(memory_space=pl.ANY)],
        out_specs=pl.BlockSpec(memory_space=pl.ANY),
        scratch_shapes=[pltpu.SemaphoreType.DMA(())],
    )
    return pl.pallas_call(
        functools.partial(_dispatch_kernel, tm=tm),
        grid_spec=grid_spec,
        out_shape=jax.ShapeDtypeStruct((rows, d), F32),
        input_output_aliases={3: 0},
        compiler_params=_cparams(("arbitrary",)),
        name="dispatch",
    )(pos_a, pos_b, h2, xs0)


def _experts_kernel(texp_ref, tidx_ref, used_ref, x_ref, w1_ref, w3_ref, w2_ref, o_ref, w1_scr, w3_scr, w2_scr):
    i = pl.program_id(0)
    prev = texp_ref[jnp.maximum(i - 1, 0)]

    @pl.when((i == 0) | (texp_ref[i] != prev))
    def _():
        w1_scr[...] = w1_ref[0].astype(BF16)
        w3_scr[...] = w3_ref[0].astype(BF16)
        w2_scr[...] = w2_ref[0].astype(BF16)

    @pl.when(i < used_ref[0])
    def _():
        x = x_ref[...].astype(BF16)
        a = jnp.dot(x, w1_scr[...], preferred_element_type=F32)
        b = jnp.dot(x, w3_scr[...], preferred_element_type=F32)
        hid = (a / (1.0 + jnp.exp(-a))) * b
        o_ref[...] = jnp.dot(hid.astype(BF16), w2_scr[...], preferred_element_type=F32)

    @pl.when(i >= used_ref[0])
    def _():
        o_ref[...] = jnp.zeros_like(o_ref)


def _experts(texp, tidx, used, xs, w1, w3, w2, tr):
    rows, d = xs.shape
    nt = rows // tr
    f = w1.shape[2]
    grid_spec = pltpu.PrefetchScalarGridSpec(
        num_scalar_prefetch=3,
        grid=(nt,),
        in_specs=[pl.BlockSpec((tr, d), lambda i, te, ti, u: (ti[i], 0)),
                  pl.BlockSpec((1, d, f), lambda i, te, ti, u: (te[i], 0, 0)),
                  pl.BlockSpec((1, d, f), lambda i, te, ti, u: (te[i], 0, 0)),
                  pl.BlockSpec((1, f, d), lambda i, te, ti, u: (te[i], 0, 0))],
        out_specs=pl.BlockSpec((tr, d), lambda i, te, ti, u: (i, 0)),
        scratch_shapes=[pltpu.VMEM((d, f), BF16), pltpu.VMEM((d, f), BF16), pltpu.VMEM((f, d), BF16)],
    )
    return pl.pallas_call(
        _experts_kernel,
        grid_spec=grid_spec,
        out_shape=jax.ShapeDtypeStruct((rows, d), F32),
        compiler_params=_cparams(("arbitrary",)),
        name="experts",
    )(texp, tidx, used, xs, w1, w3, w2)


def _combine_kernel(pa_ref, pb_ref, ys_ref, x1_ref, info_ref, mod_ref, g_ref, b_ref, o_ref,
                    buf_a, buf_b, sem, *, tm, alpha):
    base = pl.program_id(0) * tm

    def issue(t, _):
        _row_copy(ys_ref, pa_ref[base + t], buf_a, t, sem).start()
        _row_copy(ys_ref, pb_ref[base + t], buf_b, t, sem).start()
        return 0

    lax.fori_loop(0, tm, issue, 0)

    def drain(t, _):
        _row_copy(ys_ref, 0, buf_a, 0, sem).wait()
        _row_copy(ys_ref, 0, buf_b, 0, sem).wait()
        return 0

    lax.fori_loop(0, tm, drain, 0)
    info = info_ref[...]
    y = info[:, 2:3] * buf_a[...] + info[:, 3:4] * buf_b[...]
    z = alpha * x1_ref[...] + mod_ref[0, 5:6, :] * y
    o_ref[...] = _ln(z) * g_ref[...] + b_ref[...]


def _combine(pos_a, pos_b, ys, x1, info, mod3, ln_g, ln_b, seq, alpha):
    t, d = x1.shape
    tm = 256
    per_b = seq // tm
    grid_spec = pltpu.PrefetchScalarGridSpec(
        num_scalar_prefetch=2,
        grid=(t // tm,),
        in_specs=[pl.BlockSpec(memory_space=pl.ANY),
                  pl.BlockSpec((tm, d), lambda i, pa, pb: (i, 0)),
                  pl.BlockSpec((tm, LANES), lambda i, pa, pb: (i, 0)),
                  pl.BlockSpec((1, 6, d), lambda i, pa, pb: (i // per_b, 0, 0)),
                  pl.BlockSpec((1, d), lambda i, pa, pb: (0, 0)),
                  pl.BlockSpec((1, d), lambda i, pa, pb: (0, 0))],
        out_specs=pl.BlockSpec((tm, d), lambda i, pa, pb: (i, 0)),
        scratch_shapes=[pltpu.VMEM((tm, d), F32), pltpu.VMEM((tm, d), F32), pltpu.SemaphoreType.DMA(())],
    )
    return pl.pallas_call(
        functools.partial(_combine_kernel, tm=tm, alpha=alpha),
        grid_spec=grid_spec,
        out_shape=jax.ShapeDtypeStruct((t, d), F32),
        compiler_params=_cparams(("arbitrary",)),
        name="combine",
    )(pos_a, pos_b, ys, x1, info, mod3, ln_g, ln_b)


def _prep_w_in(w):
    cq = w[:, :MLA_Q_RANK]
    ckv = w[:, MLA_Q_RANK:MLA_Q_RANK + MLA_KV_RANK]
    r0 = MLA_Q_RANK + MLA_KV_RANK
    kr = w[:, r0:r0 + MLA_ROPE_DIM]
    mo = w[:, r0 + MLA_ROPE_DIM:]
    half = MLA_ROPE_DIM // 2
    z = jnp.zeros((w.shape[0], LANES - MLA_ROPE_DIM), w.dtype)
    return jnp.concatenate([cq, ckv, kr, z, kr[:, half:], kr[:, :half], z, mo], axis=1).astype(BF16)


def _prep_w_uq(w):
    r = w.shape[0]
    w = w.reshape(r, MLA_HEADS, MLA_NOPE_DIM + MLA_ROPE_DIM)
    half = MLA_ROPE_DIM // 2
    nope = w[:, :, :MLA_NOPE_DIM]
    x1 = w[:, :, MLA_NOPE_DIM:MLA_NOPE_DIM + half]
    x2 = w[:, :, MLA_NOPE_DIM + half:]
    z = jnp.zeros((r, MLA_HEADS, LANES - MLA_ROPE_DIM), w.dtype)
    return jnp.concatenate([nope, x1, x2, z, x2, x1, z], axis=2).reshape(r, MLA_HEADS * 3 * LANES).astype(BF16)


def _rope_lanes(seq):
    inv = 1.0 / (ROPE_THETA ** (jnp.arange(0, MLA_ROPE_DIM, 2, dtype=F32) / MLA_ROPE_DIM))
    ang = jnp.arange(seq, dtype=F32)[:, None] * inv[None, :]
    cos, sin = jnp.cos(ang), jnp.sin(ang)
    z = jnp.zeros((seq, LANES - MLA_ROPE_DIM), F32)
    return jnp.concatenate([cos, cos, z], axis=1), jnp.concatenate([-sin, sin, z], axis=1)


def _prep_router(w_rg, b_rg, w_re, b_re):
    d = w_rg.shape[0]
    w = jnp.concatenate([w_rg, w_re], axis=1)
    n = w.shape[1]
    hi = w.astype(BF16)
    lo = (w - hi.astype(F32)).astype(BF16)
    z = jnp.zeros((d, LANES // 2 - n), BF16)
    wr = jnp.concatenate([hi, z, lo, z], axis=1)
    br = jnp.zeros((1, LANES), F32).at[0, :n].set(jnp.concatenate([b_rg, b_re]))
    return wr, br


def _layer(xf, mod3, batch, seq, depth_alpha, w_in, q_norm_g, w_uq, kv_norm_g, w_ukv, w_out, bias_tiles,
           cos_t, sin_t, ln1_g, ln1_b, w_rg, b_rg, w_re, b_re, w1, w3, w2, ln2_g, ln2_b):
    t, d = xf.shape
    part_a, qkv_mo = _in_proj(xf, mod3, _prep_w_in(w_in), seq)
    q, k, v = _qkv(part_a, _prep_w_uq(w_uq), w_ukv.astype(BF16), q_norm_g.reshape(1, -1),
                   kv_norm_g.reshape(1, -1), cos_t, sin_t, seq)
    o_mla = _mla_attention(q, k, v, batch, seq)
    sel = _moba_select(qkv_mo, batch, seq)
    o_moba = _moba_attention(qkv_mo, sel, bias_tiles, batch, seq)
    wo = w_out.astype(BF16)
    wr, br = _prep_router(w_rg, b_rg, w_re, b_re)
    n_mla = MLA_HEADS * MLA_V_DIM
    x1, h2, logits = _out_proj(o_mla, o_moba, wo[:n_mla], wo[n_mla:], xf, mod3, ln1_g.reshape(1, d),
                               ln1_b.reshape(1, d), wr, seq, depth_alpha)
    info, counts = _route(logits, br)
    tr = 256
    nt = (2 * t) // tr + MOE_N_EXPERTS
    cnt = counts[0, MOE_GROUPS:MOE_GROUPS + MOE_N_EXPERTS].astype(jnp.int32)
    ntile = (cnt + tr - 1) // tr
    tile_end = jnp.cumsum(ntile)
    tile_start = tile_end - ntile
    used = tile_end[-1]
    start_lanes = jnp.zeros((1, LANES), F32).at[0, MOE_GROUPS:MOE_GROUPS + MOE_N_EXPERTS].set(
        (tile_start * tr).astype(F32))
    pos = _positions(info, start_lanes)
    pos_a, pos_b = pos[:, 0], pos[:, 1]
    tidx = jnp.minimum(jnp.arange(nt, dtype=jnp.int32), used - 1)
    texp = jnp.sum(tidx[:, None] >= tile_end[None, :], axis=1).astype(jnp.int32)
    xs = _dispatch(pos_a, pos_b, h2, nt * tr)
    ys = _experts(texp, tidx, used.reshape(1).astype(jnp.int32), xs, w1, w3, w2, tr)
    return _combine(pos_a, pos_b, ys, x1, info, mod3, ln2_g.reshape(1, d), ln2_b.reshape(1, d), seq, depth_alpha)


def kernel(x, c, w_ada, b_ada, w_in, q_norm_g, w_uq, kv_norm_g, w_ukv, w_out, t5_table, ln1_g, ln1_b,
           w_router_group, b_router_group, w_router_expert, b_router_expert, w1, w3, w2, ln2_g, ln2_b):
    batch, seq, d = x.shape
    depth = w_ada.shape[0]
    alpha = (2.0 * depth) ** 0.25
    cos_t, sin_t = _rope_lanes(seq)
    bias_tiles = _t5_tiles(t5_table)
    xf = x.reshape(batch * seq, d)
    for l in range(depth):
        mod3 = _ada_mod(c, w_ada[l], b_ada[l]).reshape(batch, 6, d)
        xf = _layer(xf, mod3, batch, seq, alpha, w_in[l], q_norm_g[l], w_uq[l], kv_norm_g[l], w_ukv[l],
                    w_out[l], bias_tiles, cos_t, sin_t, ln1_g[l], ln1_b[l], w_router_group[l],
                    b_router_group[l], w_router_expert[l], b_router_expert[l], w1[l], w3[l], w2[l],
                    ln2_g[l], ln2_b[l])
    return xf.reshape(batch, seq, d)
```

```python
import functools
import math

import jax
import jax.numpy as jnp
from jax import lax
from jax.experimental import pallas as pl
from jax.experimental.pallas import tpu as pltpu

D_MODEL = 2048
MLA_HEADS = 8
MLA_Q_RANK = 512
MLA_KV_RANK = 256
MLA_NOPE_DIM = 128
MLA_ROPE_DIM = 64
MLA_V_DIM = 128
ROPE_THETA = 10000.0
MOBA_HEADS = 8
MOBA_HEAD_DIM = 128
MOBA_BLOCK = 256
MOBA_TOPK = 3
T5_BUCKETS = 32
T5_MAX_DISTANCE = 128
MOE_GROUPS = 4
MOE_EXPERTS_PER_GROUP = 8
MOE_N_EXPERTS = MOE_GROUPS * MOE_EXPERTS_PER_GROUP
MOE_D_FF = 512
LN_EPS = 1e-5
RMS_EPS = 1e-6
MOBA_WIDTH = MOBA_HEADS * MOBA_HEAD_DIM

LANES = 128
QK_PAD = 256
PART_A = 1024
NEG = -1e30
VMEM_LIMIT = 56 * 1024 * 1024

F32 = jnp.float32
BF16 = jnp.bfloat16


def _cparams(sem):
    return pltpu.CompilerParams(dimension_semantics=sem, vmem_limit_bytes=VMEM_LIMIT)


def _ln(x):
    mu = jnp.mean(x, axis=-1, keepdims=True)
    xc = x - mu
    var = jnp.mean(xc * xc, axis=-1, keepdims=True)
    return xc * lax.rsqrt(var + LN_EPS)


def _nt_dot(a, b):
    return lax.dot_general(a, b, (((1,), (1,)), ((), ())), preferred_element_type=F32)


def _ada_kernel(ct_ref, w_ref, b_ref, o_ref, *, batch):
    ct = ct_ref[...]
    ca = ct / (1.0 + jnp.exp(-ct))
    w = w_ref[...]
    rows = [jnp.sum(w * ca[:, b:b + 1], axis=0, keepdims=True) for b in range(batch)]
    o_ref[...] = jnp.concatenate(rows, axis=0) + b_ref[...]


def _ada_mod(c, w_ada, b_ada):
    batch, d = c.shape
    n = w_ada.shape[1]
    tn = 512
    ct = jnp.zeros((d, LANES), F32).at[:, :batch].set(c.T)
    return pl.pallas_call(
        functools.partial(_ada_kernel, batch=batch),
        grid=(n // tn,),
        in_specs=[pl.BlockSpec((d, LANES), lambda j: (0, 0)),
                  pl.BlockSpec((d, tn), lambda j: (0, j)),
                  pl.BlockSpec((1, tn), lambda j: (0, j))],
        out_specs=pl.BlockSpec((batch, tn), lambda j: (0, j)),
        out_shape=jax.ShapeDtypeStruct((batch, n), F32),
        compiler_params=_cparams(("arbitrary",)),
        name="ada_mod",
    )(ct, w_ada, b_ada.reshape(1, n))


def _inproj_kernel(x_ref, mod_ref, w_ref, a_ref, b_ref, h_scr):
    j = pl.program_id(1)

    @pl.when(j == 0)
    def _():
        h = _ln(x_ref[...]) * (1.0 + mod_ref[0, 1:2, :]) + mod_ref[0, 0:1, :]
        h_scr[...] = h.astype(BF16)
        a_ref[...] = jnp.dot(h_scr[...], w_ref[...], preferred_element_type=F32)

    @pl.when(j > 0)
    def _():
        b_ref[...] = jnp.dot(h_scr[...], w_ref[...], preferred_element_type=F32).astype(BF16)


def _in_proj(xf, mod3, w4, seq):
    t, d = xf.shape
    n = w4.shape[1]
    tm, tn = 512, PART_A
    per_b = seq // tm
    return pl.pallas_call(
        _inproj_kernel,
        grid=(t // tm, n // tn),
        in_specs=[pl.BlockSpec((tm, d), lambda i, j: (i, 0)),
                  pl.BlockSpec((1, 6, d), lambda i, j: (i // per_b, 0, 0)),
                  pl.BlockSpec((d, tn), lambda i, j: (0, j))],
        out_specs=[pl.BlockSpec((tm, tn), lambda i, j: (i, 0)),
                   pl.BlockSpec((tm, tn), lambda i, j: (i, jnp.maximum(j - 1, 0)))],
        out_shape=[jax.ShapeDtypeStruct((t, PART_A), F32),
                   jax.ShapeDtypeStruct((t, n - PART_A), BF16)],
        scratch_shapes=[pltpu.VMEM((tm, d), BF16)],
        compiler_params=_cparams(("arbitrary", "arbitrary")),
        name="in_proj",
    )(xf, mod3, w4)


def _qkv_kernel(a_ref, wq_ref, wkv_ref, gq_ref, gkv_ref, cos_ref, sin_ref, q_ref, k_ref, v_ref):
    a = a_ref[...]
    cq = a[:, :MLA_Q_RANK]
    ckv = a[:, MLA_Q_RANK:MLA_Q_RANK + MLA_KV_RANK]
    kr = a[:, 768:896]
    krs = a[:, 896:1024]
    cqn = (cq * lax.rsqrt(jnp.mean(cq * cq, axis=-1, keepdims=True) + RMS_EPS) * gq_ref[...]).astype(BF16)
    ckvn = (ckv * lax.rsqrt(jnp.mean(ckv * ckv, axis=-1, keepdims=True) + RMS_EPS) * gkv_ref[...]).astype(BF16)
    q3 = jnp.dot(cqn, wq_ref[...], preferred_element_type=F32)
    kv = jnp.dot(ckvn, wkv_ref[...], preferred_element_type=F32)
    cos = cos_ref[...]
    sin = sin_ref[...]
    krr = (kr * cos + krs * sin).astype(BF16)
    for h in range(MLA_HEADS):
        q0 = h * 3 * LANES
        c0 = h * QK_PAD
        q_ref[:, c0:c0 + LANES] = q3[:, q0:q0 + LANES].astype(BF16)
        q_ref[:, c0 + LANES:c0 + QK_PAD] = (
            q3[:, q0 + LANES:q0 + 2 * LANES] * cos + q3[:, q0 + 2 * LANES:q0 + 3 * LANES] * sin).astype(BF16)
        k_ref[:, c0:c0 + LANES] = kv[:, c0:c0 + LANES].astype(BF16)
        k_ref[:, c0 + LANES:c0 + QK_PAD] = krr
        v_ref[:, h * LANES:(h + 1) * LANES] = kv[:, c0 + LANES:c0 + QK_PAD].astype(BF16)


def _qkv(part_a, wq3, wkv, gq, gkv, cos_t, sin_t, seq):
    t = part_a.shape[0]
    tm = 512
    per_b = seq // tm
    hq = MLA_HEADS * QK_PAD
    return pl.pallas_call(
        _qkv_kernel,
        grid=(t // tm,),
        in_specs=[pl.BlockSpec((tm, PART_A), lambda i: (i, 0)),
                  pl.BlockSpec(wq3.shape, lambda i: (0, 0)),
                  pl.BlockSpec(wkv.shape, lambda i: (0, 0)),
                  pl.BlockSpec((1, MLA_Q_RANK), lambda i: (0, 0)),
                  pl.BlockSpec((1, MLA_KV_RANK), lambda i: (0, 0)),
                  pl.BlockSpec((tm, LANES), lambda i: (i % per_b, 0)),
                  pl.BlockSpec((tm, LANES), lambda i: (i % per_b, 0))],
        out_specs=[pl.BlockSpec((tm, hq), lambda i: (i, 0)),
                   pl.BlockSpec((tm, hq), lambda i: (i, 0)),
                   pl.BlockSpec((tm, MLA_HEADS * MLA_V_DIM), lambda i: (i, 0))],
        out_shape=[jax.ShapeDtypeStruct((t, hq), BF16),
                   jax.ShapeDtypeStruct((t, hq), BF16),
                   jax.ShapeDtypeStruct((t, MLA_HEADS * MLA_V_DIM), BF16)],
        compiler_params=_cparams(("arbitrary",)),
        name="qkv",
    )(part_a, wq3, wkv, gq, gkv, cos_t, sin_t)


ATTN_TILE = 512


def _attn_kernel(*refs, c, masked):
    if masked:
        q_ref, k_ref, v_ref, bias_ref, o_ref, s_a, s_b, m_scr, l_scr, acc_scr = refs
    else:
        q_ref, k_ref, v_ref, o_ref, s_a, s_b, m_scr, l_scr, acc_scr = refs
    tile = ATTN_TILE
    qi = pl.program_id(2)

    def put_scores(buf, t):
        k = k_ref[pl.ds(pl.multiple_of(t * tile, tile), tile), :]
        buf[...] = _nt_dot(q_ref[...], k)

    def tile_step(buf, t, kind, nxt=None):
        if nxt is not None:
            put_scores(*nxt)
        s = buf[...]
        if masked:
            if kind == "prev":
                s = s + bias_ref[0, 1] + jnp.where(qi == 0, NEG, 0.0)
            elif kind == "diag":
                s = s + bias_ref[0, 0]
        elif kind == "diag":
            row = lax.broadcasted_iota(jnp.int32, (tile, tile), 0)
            col = lax.broadcasted_iota(jnp.int32, (tile, tile), 1)
            s = jnp.where(row >= col, s, NEG)
        m_old = m_scr[...]
        m_new = jnp.maximum(m_old, jnp.broadcast_to(jnp.max(s, axis=-1, keepdims=True), m_old.shape))
        alpha = jnp.exp2((m_old - m_new) * c)
        p = jnp.concatenate([jnp.exp2((s[:, j * LANES:(j + 1) * LANES] - m_new) * c)
                             for j in range(tile // LANES)], axis=1)
        v = v_ref[pl.ds(pl.multiple_of(t * tile, tile), tile), :]
        row_sum = jnp.broadcast_to(jnp.sum(p, axis=-1, keepdims=True), m_old.shape)
        l_scr[...] = alpha * l_scr[...] + row_sum
        acc_scr[...] = alpha * acc_scr[...] + jnp.dot(p.astype(BF16), v, preferred_element_type=F32)
        m_scr[...] = m_new

    m_scr[...] = jnp.full(m_scr.shape, NEG, F32)
    l_scr[...] = jnp.zeros(l_scr.shape, F32)
    acc_scr[...] = jnp.zeros(acc_scr.shape, F32)
    put_scores(s_a, 0)

    n_far = jnp.maximum(qi - 1, 0) if masked else qi
    pn = jnp.maximum(qi - 1, 0)

    def pair(j, _):
        t = 2 * j
        tile_step(s_a, t, "far", (s_b, t + 1))
        tile_step(s_b, t + 1, "far", (s_a, t + 2))
        return 0

    lax.fori_loop(0, n_far // 2, pair, 0)
    t0 = 2 * (n_far // 2)

    def tail(first, second):
        if masked:
            tile_step(first, pn, "prev", (second, qi))
            tile_step(second, qi, "diag")
        else:
            tile_step(first, qi, "diag")

    @pl.when(n_far % 2 == 1)
    def _():
        tile_step(s_a, t0, "far", (s_b, t0 + 1))
        tail(s_b, s_a)

    @pl.when(n_far % 2 == 0)
    def _():
        tail(s_a, s_b)

    o_ref[...] = (acc_scr[...] / l_scr[...]).astype(o_ref.dtype)


def _attn_scratch(dv):
    tile = ATTN_TILE
    return [pltpu.VMEM((tile, tile), F32), pltpu.VMEM((tile, tile), F32),
            pltpu.VMEM((tile, LANES), F32), pltpu.VMEM((tile, LANES), F32), pltpu.VMEM((tile, dv), F32)]


def _mla_attention(q, k, v, batch, seq):
    tile = ATTN_TILE
    nq = seq // tile
    c = math.log2(math.e) / math.sqrt(MLA_NOPE_DIM + MLA_ROPE_DIM)
    return pl.pallas_call(
        functools.partial(_attn_kernel, c=c, masked=False),
        grid=(batch, MLA_HEADS, nq),
        in_specs=[pl.BlockSpec((tile, QK_PAD), lambda b, h, i: (b * nq + i, h)),
                  pl.BlockSpec((seq, QK_PAD), lambda b, h, i: (b, h)),
                  pl.BlockSpec((seq, MLA_V_DIM), lambda b, h, i: (b, h))],
        out_specs=pl.BlockSpec((tile, MLA_V_DIM), lambda b, h, i: (b * nq + i, h)),
        out_shape=jax.ShapeDtypeStruct((batch * seq, MLA_HEADS * MLA_V_DIM), BF16),
        scratch_shapes=_attn_scratch(MLA_V_DIM),
        compiler_params=_cparams(("arbitrary", "arbitrary", "arbitrary")),
        name="mla_attn",
    )(q, k, v)


def _moba_select_kernel(q_ref, k_ref, qa_ref, ka_ref, *, seq, nb):
    kf = k_ref[...].astype(F32)
    km = jnp.sum(kf.reshape(nb, MOBA_BLOCK, MOBA_HEAD_DIM), axis=1) * (1.0 / MOBA_BLOCK)
    km = jnp.concatenate([km, jnp.zeros((LANES - nb, MOBA_HEAD_DIM), F32)], axis=0)
    km_hi = km.astype(BF16)
    km_lo = (km - km_hi.astype(F32)).astype(BF16)
    q = q_ref[...]
    gate = _nt_dot(q, km_hi) + _nt_dot(q, km_lo)
    blk = lax.broadcasted_iota(jnp.int32, (seq, LANES), 1)
    qblk = jnp.right_shift(lax.broadcasted_iota(jnp.int32, (seq, LANES), 0), MOBA_BLOCK.bit_length() - 1)
    g = jnp.where(blk < qblk, gate, NEG)
    visible = blk == qblk
    for _ in range(MOBA_TOPK):
        mx = jnp.max(g, axis=-1, keepdims=True)
        first = jnp.min(jnp.where(g == mx, blk, LANES), axis=-1, keepdims=True)
        pick = (blk == first) & (mx > 0.5 * NEG)
        visible = visible | pick
        g = jnp.where(pick, NEG, g)
    in_use = blk < nb
    qa_ref[:, :MOBA_HEAD_DIM] = q
    qa_ref[:, MOBA_HEAD_DIM:] = jnp.where(visible | ~in_use, 0.0, NEG).astype(BF16)
    ka_ref[:, :MOBA_HEAD_DIM] = k_ref[...]
    ka_ref[:, MOBA_HEAD_DIM:] = jnp.where(blk == qblk, 1.0, 0.0).astype(BF16)


def _moba_select(qkv_mo, batch, seq):
    nb = seq // MOBA_BLOCK
    assert MOBA_HEAD_DIM == LANES and nb <= QK_PAD - MOBA_HEAD_DIM
    wide = jax.ShapeDtypeStruct((batch * seq, MOBA_HEADS * QK_PAD), BF16)
    return pl.pallas_call(
        functools.partial(_moba_select_kernel, seq=seq, nb=nb),
        grid=(batch, MOBA_HEADS),
        in_specs=[pl.BlockSpec((seq, MOBA_HEAD_DIM), lambda b, h: (b, h)),
                  pl.BlockSpec((seq, MOBA_HEAD_DIM), lambda b, h: (b, MOBA_HEADS + h))],
        out_specs=[pl.BlockSpec((seq, QK_PAD), lambda b, h: (b, h)),
                   pl.BlockSpec((seq, QK_PAD), lambda b, h: (b, h))],
        out_shape=[wide, wide],
        compiler_params=_cparams(("arbitrary", "arbitrary")),
        name="moba_select",
    )(qkv_mo, qkv_mo)


def _t5_kernel(tab_ref, o_ref, *, inv_scale):
    h = pl.program_id(0)
    tile = ATTN_TILE
    r = lax.broadcasted_iota(jnp.int32, (tile, tile), 0)
    c = lax.broadcasted_iota(jnp.int32, (tile, tile), 1)
    max_exact = T5_BUCKETS // 2
    far = tab_ref[T5_BUCKETS - 1, h]
    for d in range(2):
        rel = d * tile + r - c
        n = jnp.maximum(rel, 0)
        nf = jnp.maximum(n, 1).astype(F32)
        large = max_exact + (jnp.log(nf / max_exact) / math.log(T5_MAX_DISTANCE / max_exact)
                             * (T5_BUCKETS - max_exact)).astype(jnp.int32)
        large = jnp.minimum(large, T5_BUCKETS - 1)
        bucket = jnp.where(n < max_exact, n, large)
        bias = jnp.zeros((tile, tile), F32)
        for j in range(T5_BUCKETS):
            bias = jnp.where(bucket == j, tab_ref[j, h], bias)
        o_ref[0, d] = jnp.where(rel >= 0, (bias - far) * inv_scale, NEG)


def _t5_tiles(t5_table):
    assert ATTN_TILE >= T5_MAX_DISTANCE
    tile = ATTN_TILE
    return pl.pallas_call(
        functools.partial(_t5_kernel, inv_scale=math.sqrt(MOBA_HEAD_DIM)),
        grid=(MOBA_HEADS,),
        in_specs=[pl.BlockSpec(memory_space=pltpu.SMEM)],
        out_specs=pl.BlockSpec((1, 2, tile, tile), lambda h: (h, 0, 0, 0)),
        out_shape=jax.ShapeDtypeStruct((MOBA_HEADS, 2, tile, tile), F32),
        compiler_params=_cparams(("arbitrary",)),
        name="t5_tiles",
    )(t5_table)


def _moba_attention(q_wide, k_wide, qkv_mo, bias, batch, seq):
    tile = ATTN_TILE
    nq = seq // tile
    c = math.log2(math.e) / math.sqrt(MOBA_HEAD_DIM)
    dh = MOBA_HEAD_DIM
    return pl.pallas_call(
        functools.partial(_attn_kernel, c=c, masked=True),
        grid=(batch, MOBA_HEADS, nq),
        in_specs=[pl.BlockSpec((tile, QK_PAD), lambda b, h, i: (b * nq + i, h)),
                  pl.BlockSpec((seq, QK_PAD), lambda b, h, i: (b, h)),
                  pl.BlockSpec((seq, dh), lambda b, h, i: (b, 2 * MOBA_HEADS + h)),
                  pl.BlockSpec((1, 2, tile, tile), lambda b, h, i: (h, 0, 0, 0))],
        out_specs=pl.BlockSpec((tile, dh), lambda b, h, i: (b * nq + i, h)),
        out_shape=jax.ShapeDtypeStruct((batch * seq, MOBA_WIDTH), BF16),
        scratch_shapes=_attn_scratch(MOBA_HEAD_DIM),
        compiler_params=_cparams(("arbitrary", "arbitrary", "arbitrary")),
        name="moba_attn",
    )(q_wide, k_wide, qkv_mo, bias)


def _outproj_kernel(oa_ref, ob_ref, wa_ref, wb_ref, x_ref, mod_ref, g_ref, b_ref, wr_ref,
                    x1_ref, h2_ref, lg_ref, *, alpha):
    y = (jnp.dot(oa_ref[...], wa_ref[...], preferred_element_type=F32)
         + jnp.dot(ob_ref[...], wb_ref[...], preferred_element_type=F32))
    z = alpha * x_ref[...] + mod_ref[0, 2:3, :] * y
    x1 = _ln(z) * g_ref[...] + b_ref[...]
    x1_ref[...] = x1
    h2 = _ln(x1) * (1.0 + mod_ref[0, 4:5, :]) + mod_ref[0, 3:4, :]
    h2_ref[...] = h2
    h_hi = h2.astype(BF16)
    h_lo = (h2 - h_hi.astype(F32)).astype(BF16)
    zz = (jnp.dot(h_hi, wr_ref[...], preferred_element_type=F32)
          + jnp.dot(h_lo, wr_ref[...], preferred_element_type=F32))
    lg_ref[...] = zz + pltpu.roll(zz, LANES // 2, 1)


def _out_proj(o_mla, o_moba, wa, wb, xf, mod3, ln_g, ln_b, wr, seq, alpha):
    t, d = xf.shape
    tm = 256
    per_b = seq // tm
    ka, kb = wa.shape[0], wb.shape[0]
    return pl.pallas_call(
        functools.partial(_outproj_kernel, alpha=alpha),
        grid=(t // tm,),
        in_specs=[pl.BlockSpec((tm, ka), lambda i: (i, 0)),
                  pl.BlockSpec((tm, kb), lambda i: (i, 0)),
                  pl.BlockSpec((ka, d), lambda i: (0, 0)),
                  pl.BlockSpec((kb, d), lambda i: (0, 0)),
                  pl.BlockSpec((tm, d), lambda i: (i, 0)),
                  pl.BlockSpec((1, 6, d), lambda i: (i // per_b, 0, 0)),
                  pl.BlockSpec((1, d), lambda i: (0, 0)),
                  pl.BlockSpec((1, d), lambda i: (0, 0)),
                  pl.BlockSpec((d, LANES), lambda i: (0, 0))],
        out_specs=[pl.BlockSpec((tm, d), lambda i: (i, 0)),
                   pl.BlockSpec((tm, d), lambda i: (i, 0)),
                   pl.BlockSpec((tm, LANES), lambda i: (i, 0))],
        out_shape=[jax.ShapeDtypeStruct((t, d), F32),
                   jax.ShapeDtypeStruct((t, d), F32),
                   jax.ShapeDtypeStruct((t, LANES), F32)],
        compiler_params=_cparams(("arbitrary",)),
        name="out_proj",
    )(o_mla, o_moba, wa, wb, xf, mod3, ln_g, ln_b, wr)


def _route_kernel(lg_ref, br_ref, info_ref, cnt_ref, run_scr, *, tm):
    i = pl.program_id(0)

    @pl.when(i == 0)
    def _():
        run_scr[...] = jnp.zeros_like(run_scr)

    lg = lg_ref[...] + br_ref[...]
    lane = lax.broadcasted_iota(jnp.int32, (tm, LANES), 1)
    e_lo, e_hi = MOE_GROUPS, MOE_GROUPS + MOE_N_EXPERTS
    is_g = lane < e_lo
    gl = jnp.where(is_g, lg, NEG)
    gmax = jnp.max(gl, axis=-1, keepdims=True)
    gidx = jnp.min(jnp.where(gl == gmax, lane, LANES), axis=-1, keepdims=True)
    g_p = 1.0 / jnp.sum(jnp.where(is_g, jnp.exp(gl - gmax), 0.0), axis=-1, keepdims=True)
    grp_of_lane = jnp.right_shift(lane - e_lo, MOE_EXPERTS_PER_GROUP.bit_length() - 1)
    in_grp = (lane >= e_lo) & (lane < e_hi) & (grp_of_lane == gidx)
    el = jnp.where(in_grp, lg, NEG)
    m1 = jnp.max(el, axis=-1, keepdims=True)
    l1 = jnp.min(jnp.where(el == m1, lane, LANES), axis=-1, keepdims=True)
    el2 = jnp.where(lane == l1, NEG, el)
    m2 = jnp.max(el2, axis=-1, keepdims=True)
    l2 = jnp.min(jnp.where(el2 == m2, lane, LANES), axis=-1, keepdims=True)
    zsum = jnp.sum(jnp.where(in_grp, jnp.exp(el - m1), 0.0), axis=-1, keepdims=True)
    p1 = 1.0 / zsum
    p2 = jnp.exp(m2 - m1) / zsum
    wa = g_p * (p1 / (p1 + p2))
    wb = g_p * (p2 / (p1 + p2))
    hot_a = lane == l1
    hot_b = lane == l2
    onehot = jnp.where(hot_a | hot_b, 1.0, 0.0)
    r = lax.broadcasted_iota(jnp.int32, (tm, tm), 0)
    c = lax.broadcasted_iota(jnp.int32, (tm, tm), 1)
    lower = jnp.where(c < r, 1.0, 0.0).astype(BF16)
    before = jnp.dot(lower, onehot.astype(BF16), preferred_element_type=F32) + run_scr[...]
    rank_a = jnp.sum(jnp.where(hot_a, before, 0.0), axis=-1, keepdims=True)
    rank_b = jnp.sum(jnp.where(hot_b, before, 0.0), axis=-1, keepdims=True)
    run_scr[...] += jnp.sum(onehot, axis=0, keepdims=True)
    info = jnp.zeros((tm, LANES), F32)
    for k, val in enumerate([(l1 - e_lo).astype(F32), (l2 - e_lo).astype(F32), wa, wb, rank_a, rank_b]):
        info = jnp.where(lane == k, val, info)
    info_ref[...] = info
    cnt_ref[...] = run_scr[...]


def _route(logits, br):
    t = logits.shape[0]
    tm = 512
    return pl.pallas_call(
        functools.partial(_route_kernel, tm=tm),
        grid=(t // tm,),
        in_specs=[pl.BlockSpec((tm, LANES), lambda i: (i, 0)),
                  pl.BlockSpec((1, LANES), lambda i: (0, 0))],
        out_specs=[pl.BlockSpec((tm, LANES), lambda i: (i, 0)),
                   pl.BlockSpec((1, LANES), lambda i: (0, 0))],
        out_shape=[jax.ShapeDtypeStruct((t, LANES), F32),
                   jax.ShapeDtypeStruct((1, LANES), F32)],
        scratch_shapes=[pltpu.VMEM((1, LANES), F32)],
        compiler_params=_cparams(("arbitrary",)),
        name="route",
    )(logits, br)


def _pos_kernel(info_ref, start_ref, pos_ref):
    info = info_ref[...]
    tm = info.shape[0]
    lane = lax.broadcasted_iota(jnp.int32, (tm, LANES), 1)
    start = start_ref[...]
    cols = []
    for k in range(2):
        e = jnp.sum(jnp.where(lane == k, info, 0.0), axis=-1, keepdims=True).astype(jnp.int32)
        rank = jnp.sum(jnp.where(lane == 4 + k, info, 0.0), axis=-1, keepdims=True)
        base = jnp.sum(jnp.where(lane == e + MOE_GROUPS, start, 0.0), axis=-1, keepdims=True)
        cols.append(base + rank)
    pos_ref[...] = jnp.where(lane == 0, cols[0], jnp.where(lane == 1, cols[1], 0.0)).astype(jnp.int32)


def _positions(info, start_lanes):
    t = info.shape[0]
    tm = 1024
    return pl.pallas_call(
        _pos_kernel,
        grid=(t // tm,),
        in_specs=[pl.BlockSpec((tm, LANES), lambda i: (i, 0)),
                  pl.BlockSpec((1, LANES), lambda i: (0, 0))],
        out_specs=pl.BlockSpec((tm, LANES), lambda i: (i, 0)),
        out_shape=jax.ShapeDtypeStruct((t, LANES), jnp.int32),
        compiler_params=_cparams(("arbitrary",)),
        name="positions",
    )(info, start_lanes)


def _row_copy(src_ref, src_row, dst_ref, dst_row, sem):
    return pltpu.make_async_copy(src_ref.at[pl.ds(src_row, 1)], dst_ref.at[pl.ds(dst_row, 1)], sem)


def _dispatch_kernel(pa_ref, pb_ref, h_ref, xs_in_ref, xs_ref, sem, *, tm):
    del xs_in_ref
    base = pl.program_id(0) * tm

    def issue(t, _):
        _row_copy(h_ref, t, xs_ref, pa_ref[base + t], sem).start()
        _row_copy(h_ref, t, xs_ref, pb_ref[base + t], sem).start()
        return 0

    lax.fori_loop(0, tm, issue, 0)

    def drain(t, _):
        _row_copy(h_ref, 0, xs_ref, 0, sem).wait()
        _row_copy(h_ref, 0, xs_ref, 0, sem).wait()
        return 0

    lax.fori_loop(0, tm, drain, 0)


def _dispatch(pos_a, pos_b, h2, rows):
    t, d = h2.shape
    tm = 256
    xs0 = jnp.zeros((rows, d), F32)
    grid_spec = pltpu.PrefetchScalarGridSpec(
        num_scalar_prefetch=2,
        grid=(t // tm,),
        in_specs=[pl.BlockSpec((tm, d), lambda i, pa, pb: (i, 0)),
                  pl.BlockSpec(memory_space=pl.ANY)],
        out_specs=pl.BlockSpec(memory_space=pl.ANY),
        scratch_shapes=[pltpu.SemaphoreType.DMA(())],
    )
    return pl.pallas_call(
        functools.partial(_dispatch_kernel, tm=tm),
        grid_spec=grid_spec,
        out_shape=jax.ShapeDtypeStruct((rows, d), F32),
        input_output_aliases={3: 0},
        compiler_params=_cparams(("arbitrary",)),
        name="dispatch",
    )(pos_a, pos_b, h2, xs0)


def _experts_kernel(texp_ref, tidx_ref, used_ref, x_ref, w1_ref, w3_ref, w2_ref, o_ref, w1_scr, w3_scr, w2_scr):
    i = pl.program_id(0)
    prev = texp_ref[jnp.maximum(i - 1, 0)]

    @pl.when((i == 0) | (texp_ref[i] != prev))
    def _():
        w1_scr[...] = w1_ref[0].astype(BF16)
        w3_scr[...] = w3_ref[0].astype(BF16)
        w2_scr[...] = w2_ref[0].astype(BF16)

    @pl.when(i < used_ref[0])
    def _():
        x = x_ref[...].astype(BF16)
        a = jnp.dot(x, w1_scr[...], preferred_element_type=F32)
        b = jnp.dot(x, w3_scr[...], preferred_element_type=F32)
        hid = (a / (1.0 + jnp.exp(-a))) * b
        o_ref[...] = jnp.dot(hid.astype(BF16), w2_scr[...], preferred_element_type=F32)

    @pl.when(i >= used_ref[0])
    def _():
        o_ref[...] = jnp.zeros_like(o_ref)


def _experts(texp, tidx, used, xs, w1, w3, w2, tr):
    rows, d = xs.shape
    nt = rows // tr
    f = w1.shape[2]
    grid_spec = pltpu.PrefetchScalarGridSpec(
        num_scalar_prefetch=3,
        grid=(nt,),
        in_specs=[pl.BlockSpec((tr, d), lambda i, te, ti, u: (ti[i], 0)),
                  pl.BlockSpec((1, d, f), lambda i, te, ti, u: (te[i], 0, 0)),
                  pl.BlockSpec((1, d, f), lambda i, te, ti, u: (te[i], 0, 0)),
                  pl.BlockSpec((1, f, d), lambda i, te, ti, u: (te[i], 0, 0))],
        out_specs=pl.BlockSpec((tr, d), lambda i, te, ti, u: (i, 0)),
        scratch_shapes=[pltpu.VMEM((d, f), BF16), pltpu.VMEM((d, f), BF16), pltpu.VMEM((f, d), BF16)],
    )
    return pl.pallas_call(
        _experts_kernel,
        grid_spec=grid_spec,
        out_shape=jax.ShapeDtypeStruct((rows, d), F32),
        compiler_params=_cparams(("arbitrary",)),
        name="experts",
    )(texp, tidx, used, xs, w1, w3, w2)


def _combine_kernel(pa_ref, pb_ref, ys_ref, x1_ref, info_ref, mod_ref, g_ref, b_ref, o_ref,
                    buf_a, buf_b, sem, *, tm, alpha):
    base = pl.program_id(0) * tm

    def issue(t, _):
        _row_copy(ys_ref, pa_ref[base + t], buf_a, t, sem).start()
        _row_copy(ys_ref, pb_ref[base + t], buf_b, t, sem).start()
        return 0

    lax.fori_loop(0, tm, issue, 0)

    def drain(t, _):
        _row_copy(ys_ref, 0, buf_a, 0, sem).wait()
        _row_copy(ys_ref, 0, buf_b, 0, sem).wait()
        return 0

    lax.fori_loop(0, tm, drain, 0)
    info = info_ref[...]
    y = info[:, 2:3] * buf_a[...] + info[:, 3:4] * buf_b[...]
    z = alpha * x1_ref[...] + mod_ref[0, 5:6, :] * y
    o_ref[...] = _ln(z) * g_ref[...] + b_ref[...]


def _combine(pos_a, pos_b, ys, x1, info, mod3, ln_g, ln_b, seq, alpha):
    t, d = x1.shape
    tm = 256
    per_b = seq // tm
    grid_spec = pltpu.PrefetchScalarGridSpec(
        num_scalar_prefetch=2,
        grid=(t // tm,),
        in_specs=[pl.BlockSpec(memory_space=pl.ANY),
                  pl.BlockSpec((tm, d), lambda i, pa, pb: (i, 0)),
                  pl.BlockSpec((tm, LANES), lambda i, pa, pb: (i, 0)),
                  pl.BlockSpec((1, 6, d), lambda i, pa, pb: (i // per_b, 0, 0)),
                  pl.BlockSpec((1, d), lambda i, pa, pb: (0, 0)),
                  pl.BlockSpec((1, d), lambda i, pa, pb: (0, 0))],
        out_specs=pl.BlockSpec((tm, d), lambda i, pa, pb: (i, 0)),
        scratch_shapes=[pltpu.VMEM((tm, d), F32), pltpu.VMEM((tm, d), F32), pltpu.SemaphoreType.DMA(())],
    )
    return pl.pallas_call(
        functools.partial(_combine_kernel, tm=tm, alpha=alpha),
        grid_spec=grid_spec,
        out_shape=jax.ShapeDtypeStruct((t, d), F32),
        compiler_params=_cparams(("arbitrary",)),
        name="combine",
    )(pos_a, pos_b, ys, x1, info, mod3, ln_g, ln_b)


def _prep_w_in(w):
    cq = w[:, :MLA_Q_RANK]
    ckv = w[:, MLA_Q_RANK:MLA_Q_RANK + MLA_KV_RANK]
    r0 = MLA_Q_RANK + MLA_KV_RANK
    kr = w[:, r0:r0 + MLA_ROPE_DIM]
    mo = w[:, r0 + MLA_ROPE_DIM:]
    half = MLA_ROPE_DIM // 2
    z = jnp.zeros((w.shape[0], LANES - MLA_ROPE_DIM), w.dtype)
    return jnp.concatenate([cq, ckv, kr, z, kr[:, half:], kr[:, :half], z, mo], axis=1).astype(BF16)


def _prep_w_uq(w):
    r = w.shape[0]
    w = w.reshape(r, MLA_HEADS, MLA_NOPE_DIM + MLA_ROPE_DIM)
    half = MLA_ROPE_DIM // 2
    nope = w[:, :, :MLA_NOPE_DIM]
    x1 = w[:, :, MLA_NOPE_DIM:MLA_NOPE_DIM + half]
    x2 = w[:, :, MLA_NOPE_DIM + half:]
    z = jnp.zeros((r, MLA_HEADS, LANES - MLA_ROPE_DIM), w.dtype)
    return jnp.concatenate([nope, x1, x2, z, x2, x1, z], axis=2).reshape(r, MLA_HEADS * 3 * LANES).astype(BF16)


def _rope_lanes(seq):
    inv = 1.0 / (ROPE_THETA ** (jnp.arange(0, MLA_ROPE_DIM, 2, dtype=F32) / MLA_ROPE_DIM))
    ang = jnp.arange(seq, dtype=F32)[:, None] * inv[None, :]
    cos, sin = jnp.cos(ang), jnp.sin(ang)
    z = jnp.zeros((seq, LANES - MLA_ROPE_DIM), F32)
    return jnp.concatenate([cos, cos, z], axis=1), jnp.concatenate([-sin, sin, z], axis=1)


def _prep_router(w_rg, b_rg, w_re, b_re):
    d = w_rg.shape[0]
    w = jnp.concatenate([w_rg, w_re], axis=1)
    n = w.shape[1]
    hi = w.astype(BF16)
    lo = (w - hi.astype(F32)).astype(BF16)
    z = jnp.zeros((d, LANES // 2 - n), BF16)
    wr = jnp.concatenate([hi, z, lo, z], axis=1)
    br = jnp.zeros((1, LANES), F32).at[0, :n].set(jnp.concatenate([b_rg, b_re]))
    return wr, br


def _layer(xf, mod3, batch, seq, depth_alpha, w_in, q_norm_g, w_uq, kv_norm_g, w_ukv, w_out, bias_tiles,
           cos_t, sin_t, ln1_g, ln1_b, w_rg, b_rg, w_re, b_re, w1, w3, w2, ln2_g, ln2_b):
    t, d = xf.shape
    part_a, qkv_mo = _in_proj(xf, mod3, _prep_w_in(w_in), seq)
    q, k, v = _qkv(part_a, _prep_w_uq(w_uq), w_ukv.astype(BF16), q_norm_g.reshape(1, -1),
                   kv_norm_g.reshape(1, -1), cos_t, sin_t, seq)
    o_mla = _mla_attention(q, k, v, batch, seq)
    q_wide, k_wide = _moba_select(qkv_mo, batch, seq)
    o_moba = _moba_attention(q_wide, k_wide, qkv_mo, bias_tiles, batch, seq)
    wo = w_out.astype(BF16)
    wr, br = _prep_router(w_rg, b_rg, w_re, b_re)
    n_mla = MLA_HEADS * MLA_V_DIM
    x1, h2, logits = _out_proj(o_mla, o_moba, wo[:n_mla], wo[n_mla:], xf, mod3, ln1_g.reshape(1, d),
                               ln1_b.reshape(1, d), wr, seq, depth_alpha)
    info, counts = _route(logits, br)
    tr = 256
    nt = (2 * t) // tr + MOE_N_EXPERTS
    cnt = counts[0, MOE_GROUPS:MOE_GROUPS + MOE_N_EXPERTS].astype(jnp.int32)
    ntile = (cnt + tr - 1) // tr
    tile_end = jnp.cumsum(ntile)
    tile_start = tile_end - ntile
    used = tile_end[-1]
    start_lanes = jnp.zeros((1, LANES), F32).at[0, MOE_GROUPS:MOE_GROUPS + MOE_N_EXPERTS].set(
        (tile_start * tr).astype(F32))
    pos = _positions(info, start_lanes)
    pos_a, pos_b = pos[:, 0], pos[:, 1]
    tidx = jnp.minimum(jnp.arange(nt, dtype=jnp.int32), used - 1)
    texp = jnp.sum(tidx[:, None] >= tile_end[None, :], axis=1).astype(jnp.int32)
    xs = _dispatch(pos_a, pos_b, h2, nt * tr)
    ys = _experts(texp, tidx, used.reshape(1).astype(jnp.int32), xs, w1, w3, w2, tr)
    return _combine(pos_a, pos_b, ys, x1, info, mod3, ln2_g.reshape(1, d), ln2_b.reshape(1, d), seq, depth_alpha)


def kernel(x, c, w_ada, b_ada, w_in, q_norm_g, w_uq, kv_norm_g, w_ukv, w_out, t5_table, ln1_g, ln1_b,
           w_router_group, b_router_group, w_router_expert, b_router_expert, w1, w3, w2, ln2_g, ln2_b):
    batch, seq, d = x.shape
    depth = w_ada.shape[0]
    alpha = (2.0 * depth) ** 0.25
    cos_t, sin_t = _rope_lanes(seq)
    bias_tiles = _t5_tiles(t5_table)
    xf = x.reshape(batch * seq, d)
    for l in range(depth):
        mod3 = _ada_mod(c, w_ada[l], b_ada[l]).reshape(batch, 6, d)
        xf = _layer(xf, mod3, batch, seq, alpha, w_in[l], q_norm_g[l], w_uq[l], kv_norm_g[l], w_ukv[l],
                    w_out[l], bias_tiles, cos_t, sin_t, ln1_g[l], ln1_b[l], w_router_group[l],
                    b_router_group[l], w_router_expert[l], b_router_expert[l], w1[l], w3[l], w2[l],
                    ln2_g[l], ln2_b[l])
    return xf.reshape(batch, seq, d)
```

```python
import functools
import math

import jax
import jax.numpy as jnp
from jax import lax
from jax.experimental import pallas as pl
from jax.experimental.pallas import tpu as pltpu

D_MODEL = 2048
MLA_HEADS = 8
MLA_Q_RANK = 512
MLA_KV_RANK = 256
MLA_NOPE_DIM = 128
MLA_ROPE_DIM = 64
MLA_V_DIM = 128
ROPE_THETA = 10000.0
MOBA_HEADS = 8
MOBA_HEAD_DIM = 128
MOBA_BLOCK = 256
MOBA_TOPK = 3
T5_BUCKETS = 32
T5_MAX_DISTANCE = 128
MOE_GROUPS = 4
MOE_EXPERTS_PER_GROUP = 8
MOE_N_EXPERTS = MOE_GROUPS * MOE_EXPERTS_PER_GROUP
MOE_D_FF = 512
LN_EPS = 1e-5
RMS_EPS = 1e-6
MOBA_WIDTH = MOBA_HEADS * MOBA_HEAD_DIM

LANES = 128
QK_PAD = 256
PART_A = 1024
NEG = -1e30
VMEM_LIMIT = 56 * 1024 * 1024

F32 = jnp.float32
BF16 = jnp.bfloat16


def _cparams(sem):
    return pltpu.CompilerParams(dimension_semantics=sem, vmem_limit_bytes=VMEM_LIMIT)


def _ln(x):
    mu = jnp.mean(x, axis=-1, keepdims=True)
    xc = x - mu
    var = jnp.mean(xc * xc, axis=-1, keepdims=True)
    return xc * lax.rsqrt(var + LN_EPS)


def _nt_dot(a, b):
    return lax.dot_general(a, b, (((1,), (1,)), ((), ())), preferred_element_type=F32)


def _ada_kernel(ct_ref, w_ref, b_ref, o_ref, *, batch):
    ct = ct_ref[...]
    ca = ct / (1.0 + jnp.exp(-ct))
    w = w_ref[...]
    rows = [jnp.sum(w * ca[:, b:b + 1], axis=0, keepdims=True) for b in range(batch)]
    o_ref[...] = jnp.concatenate(rows, axis=0) + b_ref[...]


def _ada_mod(c, w_ada, b_ada):
    batch, d = c.shape
    n = w_ada.shape[1]
    tn = 512
    ct = jnp.zeros((d, LANES), F32).at[:, :batch].set(c.T)
    return pl.pallas_call(
        functools.partial(_ada_kernel, batch=batch),
        grid=(n // tn,),
        in_specs=[pl.BlockSpec((d, LANES), lambda j: (0, 0)),
                  pl.BlockSpec((d, tn), lambda j: (0, j)),
                  pl.BlockSpec((1, tn), lambda j: (0, j))],
        out_specs=pl.BlockSpec((batch, tn), lambda j: (0, j)),
        out_shape=jax.ShapeDtypeStruct((batch, n), F32),
        compiler_params=_cparams(("arbitrary",)),
        name="ada_mod",
    )(ct, w_ada, b_ada.reshape(1, n))


def _inproj_kernel(x_ref, mod_ref, w_ref, a_ref, b_ref, h_scr):
    j = pl.program_id(1)

    @pl.when(j == 0)
    def _():
        h = _ln(x_ref[...]) * (1.0 + mod_ref[0, 1:2, :]) + mod_ref[0, 0:1, :]
        h_scr[...] = h.astype(BF16)
        a_ref[...] = jnp.dot(h_scr[...], w_ref[...], preferred_element_type=F32)

    @pl.when(j > 0)
    def _():
        b_ref[...] = jnp.dot(h_scr[...], w_ref[...], preferred_element_type=F32).astype(BF16)


def _in_proj(xf, mod3, w4, seq):
    t, d = xf.shape
    n = w4.shape[1]
    tm, tn = 512, PART_A
    per_b = seq // tm
    return pl.pallas_call(
        _inproj_kernel,
        grid=(t // tm, n // tn),
        in_specs=[pl.BlockSpec((tm, d), lambda i, j: (i, 0)),
                  pl.BlockSpec((1, 6, d), lambda i, j: (i // per_b, 0, 0)),
                  pl.BlockSpec((d, tn), lambda i, j: (0, j))],
        out_specs=[pl.BlockSpec((tm, tn), lambda i, j: (i, 0)),
                   pl.BlockSpec((tm, tn), lambda i, j: (i, jnp.maximum(j - 1, 0)))],
        out_shape=[jax.ShapeDtypeStruct((t, PART_A), F32),
                   jax.ShapeDtypeStruct((t, n - PART_A), BF16)],
        scratch_shapes=[pltpu.VMEM((tm, d), BF16)],
        compiler_params=_cparams(("arbitrary", "arbitrary")),
        name="in_proj",
    )(xf, mod3, w4)


def _qkv_kernel(a_ref, wq_ref, wkv_ref, gq_ref, gkv_ref, cos_ref, sin_ref, q_ref, k_ref, v_ref):
    a = a_ref[...]
    cq = a[:, :MLA_Q_RANK]
    ckv = a[:, MLA_Q_RANK:MLA_Q_RANK + MLA_KV_RANK]
    kr = a[:, 768:896]
    krs = a[:, 896:1024]
    cqn = (cq * lax.rsqrt(jnp.mean(cq * cq, axis=-1, keepdims=True) + RMS_EPS) * gq_ref[...]).astype(BF16)
    ckvn = (ckv * lax.rsqrt(jnp.mean(ckv * ckv, axis=-1, keepdims=True) + RMS_EPS) * gkv_ref[...]).astype(BF16)
    q3 = jnp.dot(cqn, wq_ref[...], preferred_element_type=F32)
    kv = jnp.dot(ckvn, wkv_ref[...], preferred_element_type=F32)
    cos = cos_ref[...]
    sin = sin_ref[...]
    krr = (kr * cos + krs * sin).astype(BF16)
    for h in range(MLA_HEADS):
        q0 = h * 3 * LANES
        c0 = h * QK_PAD
        q_ref[:, c0:c0 + LANES] = q3[:, q0:q0 + LANES].astype(BF16)
        q_ref[:, c0 + LANES:c0 + QK_PAD] = (
            q3[:, q0 + LANES:q0 + 2 * LANES] * cos + q3[:, q0 + 2 * LANES:q0 + 3 * LANES] * sin).astype(BF16)
        k_ref[:, c0:c0 + LANES] = kv[:, c0:c0 + LANES].astype(BF16)
        k_ref[:, c0 + LANES:c0 + QK_PAD] = krr
        v_ref[:, h * LANES:(h + 1) * LANES] = kv[:, c0 + LANES:c0 + QK_PAD].astype(BF16)


def _qkv(part_a, wq3, wkv, gq, gkv, cos_t, sin_t, seq):
    t = part_a.shape[0]
    tm = 512
    per_b = seq // tm
    hq = MLA_HEADS * QK_PAD
    return pl.pallas_call(
        _qkv_kernel,
        grid=(t // tm,),
        in_specs=[pl.BlockSpec((tm, PART_A), lambda i: (i, 0)),
                  pl.BlockSpec(wq3.shape, lambda i: (0, 0)),
                  pl.BlockSpec(wkv.shape, lambda i: (0, 0)),
                  pl.BlockSpec((1, MLA_Q_RANK), lambda i: (0, 0)),
                  pl.BlockSpec((1, MLA_KV_RANK), lambda i: (0, 0)),
                  pl.BlockSpec((tm, LANES), lambda i: (i % per_b, 0)),
                  pl.BlockSpec((tm, LANES), lambda i: (i % per_b, 0))],
        out_specs=[pl.BlockSpec((tm, hq), lambda i: (i, 0)),
                   pl.BlockSpec((tm, hq), lambda i: (i, 0)),
                   pl.BlockSpec((tm, MLA_HEADS * MLA_V_DIM), lambda i: (i, 0))],
        out_shape=[jax.ShapeDtypeStruct((t, hq), BF16),
                   jax.ShapeDtypeStruct((t, hq), BF16),
                   jax.ShapeDtypeStruct((t, MLA_HEADS * MLA_V_DIM), BF16)],
        compiler_params=_cparams(("arbitrary",)),
        name="qkv",
    )(part_a, wq3, wkv, gq, gkv, cos_t, sin_t)


ATTN_TILE = 512


def _attn_kernel(*refs, c, masked):
    if masked:
        q_ref, k_ref, v_ref, bias_ref, o_ref, s_a, s_b, m_scr, l_scr, acc_scr = refs
    else:
        q_ref, k_ref, v_ref, o_ref, s_a, s_b, m_scr, l_scr, acc_scr = refs
    tile = ATTN_TILE
    qi = pl.program_id(2)

    def put_scores(buf, t):
        k = k_ref[pl.ds(pl.multiple_of(t * tile, tile), tile), :]
        buf[...] = _nt_dot(q_ref[...], k)

    def tile_step(buf, t, kind, nxt=None):
        if nxt is not None:
            put_scores(*nxt)
        s = buf[...]
        if masked:
            if kind == "prev":
                s = s + bias_ref[0, 1] + jnp.where(qi == 0, NEG, 0.0)
            elif kind == "diag":
                s = s + bias_ref[0, 0]
        elif kind == "diag":
            row = lax.broadcasted_iota(jnp.int32, (tile, tile), 0)
            col = lax.broadcasted_iota(jnp.int32, (tile, tile), 1)
            s = jnp.where(row >= col, s, NEG)
        m_old = m_scr[...]
        m_new = jnp.maximum(m_old, jnp.broadcast_to(jnp.max(s, axis=-1, keepdims=True), m_old.shape))
        alpha = jnp.exp2((m_old - m_new) * c)
        p = jnp.concatenate([jnp.exp2((s[:, j * LANES:(j + 1) * LANES] - m_new) * c)
                             for j in range(tile // LANES)], axis=1)
        v = v_ref[pl.ds(pl.multiple_of(t * tile, tile), tile), :]
        row_sum = jnp.broadcast_to(jnp.sum(p, axis=-1, keepdims=True), m_old.shape)
        l_scr[...] = alpha * l_scr[...] + row_sum
        acc_scr[...] = alpha * acc_scr[...] + jnp.dot(p.astype(BF16), v, preferred_element_type=F32)
        m_scr[...] = m_new

    m_scr[...] = jnp.full(m_scr.shape, NEG, F32)
    l_scr[...] = jnp.zeros(l_scr.shape, F32)
    acc_scr[...] = jnp.zeros(acc_scr.shape, F32)
    put_scores(s_a, 0)

    n_far = jnp.maximum(qi - 1, 0) if masked else qi
    pn = jnp.maximum(qi - 1, 0)

    def pair(j, _):
        t = 2 * j
        tile_step(s_a, t, "far", (s_b, t + 1))
        tile_step(s_b, t + 1, "far", (s_a, t + 2))
        return 0

    lax.fori_loop(0, n_far // 2, pair, 0)
    t0 = 2 * (n_far // 2)

    def tail(first, second):
        if masked:
            tile_step(first, pn, "prev", (second, qi))
            tile_step(second, qi, "diag")
        else:
            tile_step(first, qi, "diag")

    @pl.when(n_far % 2 == 1)
    def _():
        tile_step(s_a, t0, "far", (s_b, t0 + 1))
        tail(s_b, s_a)

    @pl.when(n_far % 2 == 0)
    def _():
        tail(s_a, s_b)

    o_ref[...] = (acc_scr[...] / l_scr[...]).astype(o_ref.dtype)


def _attn_scratch(dv):
    tile = ATTN_TILE
    return [pltpu.VMEM((tile, tile), F32), pltpu.VMEM((tile, tile), F32),
            pltpu.VMEM((tile, LANES), F32), pltpu.VMEM((tile, LANES), F32), pltpu.VMEM((tile, dv), F32)]


def _mla_attention(q, k, v, batch, seq):
    tile = ATTN_TILE
    nq = seq // tile
    c = math.log2(math.e) / math.sqrt(MLA_NOPE_DIM + MLA_ROPE_DIM)
    return pl.pallas_call(
        functools.partial(_attn_kernel, c=c, masked=False),
        grid=(batch, MLA_HEADS, nq),
        in_specs=[pl.BlockSpec((tile, QK_PAD), lambda b, h, i: (b * nq + i, h)),
                  pl.BlockSpec((seq, QK_PAD), lambda b, h, i: (b, h)),
                  pl.BlockSpec((seq, MLA_V_DIM), lambda b, h, i: (b, h))],
        out_specs=pl.BlockSpec((tile, MLA_V_DIM), lambda b, h, i: (b * nq + i, h)),
        out_shape=jax.ShapeDtypeStruct((batch * seq, MLA_HEADS * MLA_V_DIM), BF16),
        scratch_shapes=_attn_scratch(MLA_V_DIM),
        compiler_params=_cparams(("arbitrary", "arbitrary", "arbitrary")),
        name="mla_attn",
    )(q, k, v)


def _moba_select_kernel(q_ref, k_ref, qa_ref, ka_ref, *, seq, nb):
    kf = k_ref[...].astype(F32)
    km = jnp.sum(kf.reshape(nb, MOBA_BLOCK, MOBA_HEAD_DIM), axis=1) * (1.0 / MOBA_BLOCK)
    km = jnp.concatenate([km, jnp.zeros((LANES - nb, MOBA_HEAD_DIM), F32)], axis=0)
    km_hi = km.astype(BF16)
    km_lo = (km - km_hi.astype(F32)).astype(BF16)
    q = q_ref[...]
    gate = _nt_dot(q, km_hi) + _nt_dot(q, km_lo)
    blk = lax.broadcasted_iota(jnp.int32, (seq, LANES), 1)
    qblk = jnp.right_shift(lax.broadcasted_iota(jnp.int32, (seq, LANES), 0), MOBA_BLOCK.bit_length() - 1)
    g = jnp.where(blk < qblk, gate, NEG)
    visible = blk == qblk
    for _ in range(MOBA_TOPK):
        mx = jnp.max(g, axis=-1, keepdims=True)
        first = jnp.min(jnp.where(g == mx, blk, LANES), axis=-1, keepdims=True)
        pick = (blk == first) & (mx > 0.5 * NEG)
        visible = visible | pick
        g = jnp.where(pick, NEG, g)
    in_use = blk < nb
    qa_ref[:, :MOBA_HEAD_DIM] = q
    qa_ref[:, MOBA_HEAD_DIM:] = jnp.where(visible | ~in_use, 0.0, NEG).astype(BF16)
    ka_ref[:, :MOBA_HEAD_DIM] = k_ref[...]
    ka_ref[:, MOBA_HEAD_DIM:] = jnp.where(blk == qblk, 1.0, 0.0).astype(BF16)


def _moba_select(qkv_mo, batch, seq):
    nb = seq // MOBA_BLOCK
    assert MOBA_HEAD_DIM == LANES and nb <= QK_PAD - MOBA_HEAD_DIM
    wide = jax.ShapeDtypeStruct((batch * seq, MOBA_HEADS * QK_PAD), BF16)
    return pl.pallas_call(
        functools.partial(_moba_select_kernel, seq=seq, nb=nb),
        grid=(batch, MOBA_HEADS),
        in_specs=[pl.BlockSpec((seq, MOBA_HEAD_DIM), lambda b, h: (b, h)),
                  pl.BlockSpec((seq, MOBA_HEAD_DIM), lambda b, h: (b, MOBA_HEADS + h))],
        out_specs=[pl.BlockSpec((seq, QK_PAD), lambda b, h: (b, h)),
                   pl.BlockSpec((seq, QK_PAD), lambda b, h: (b, h))],
        out_shape=[wide, wide],
        compiler_params=_cparams(("arbitrary", "arbitrary")),
        name="moba_select",
    )(qkv_mo, qkv_mo)


def _t5_kernel(tab_ref, o_ref, *, inv_scale):
    h = pl.program_id(0)
    r = lax.broadcasted_iota(jnp.int32, (LANES, LANES), 0)
    c = lax.broadcasted_iota(jnp.int32, (LANES, LANES), 1)
    max_exact = T5_BUCKETS // 2
    far = tab_ref[T5_BUCKETS - 1, h]

    def block(offset):
        rel = offset + r - c
        n = jnp.maximum(rel, 0)
        nf = jnp.maximum(n, 1).astype(F32)
        large = max_exact + (jnp.log(nf / max_exact) / math.log(T5_MAX_DISTANCE / max_exact)
                             * (T5_BUCKETS - max_exact)).astype(jnp.int32)
        large = jnp.minimum(large, T5_BUCKETS - 1)
        bucket = jnp.where(n < max_exact, n, large)
        bias = jnp.zeros((LANES, LANES), F32)
        for j in range(T5_BUCKETS):
            bias = jnp.where(bucket == j, tab_ref[j, h], bias)
        return jnp.where(rel >= 0, (bias - far) * inv_scale, NEG)

    near = {0: block(0), 1: block(LANES)}
    nblk = ATTN_TILE // LANES
    for d in range(2):
        for i in range(nblk):
            for j in range(nblk):
                k = d * nblk + i - j
                if k < 0:
                    val = jnp.full((LANES, LANES), NEG, F32)
                else:
                    val = near.get(k, jnp.zeros((LANES, LANES), F32))
                o_ref[0, d, i * LANES:(i + 1) * LANES, j * LANES:(j + 1) * LANES] = val


def _t5_tiles(t5_table):
    assert LANES >= T5_MAX_DISTANCE
    tile = ATTN_TILE
    return pl.pallas_call(
        functools.partial(_t5_kernel, inv_scale=math.sqrt(MOBA_HEAD_DIM)),
        grid=(MOBA_HEADS,),
        in_specs=[pl.BlockSpec(memory_space=pltpu.SMEM)],
        out_specs=pl.BlockSpec((1, 2, tile, tile), lambda h: (h, 0, 0, 0)),
        out_shape=jax.ShapeDtypeStruct((MOBA_HEADS, 2, tile, tile), F32),
        compiler_params=_cparams(("arbitrary",)),
        name="t5_tiles",
    )(t5_table)


def _moba_attention(q_wide, k_wide, qkv_mo, bias, batch, seq):
    tile = ATTN_TILE
    nq = seq // tile
    c = math.log2(math.e) / math.sqrt(MOBA_HEAD_DIM)
    dh = MOBA_HEAD_DIM
    return pl.pallas_call(
        functools.partial(_attn_kernel, c=c, masked=True),
        grid=(batch, MOBA_HEADS, nq),
        in_specs=[pl.BlockSpec((tile, QK_PAD), lambda b, h, i: (b * nq + i, h)),
                  pl.BlockSpec((seq, QK_PAD), lambda b, h, i: (b, h)),
                  pl.BlockSpec((seq, dh), lambda b, h, i: (b, 2 * MOBA_HEADS + h)),
                  pl.BlockSpec((1, 2, tile, tile), lambda b, h, i: (h, 0, 0, 0))],
        out_specs=pl.BlockSpec((tile, dh), lambda b, h, i: (b * nq + i, h)),
        out_shape=jax.ShapeDtypeStruct((batch * seq, MOBA_WIDTH), BF16),
        scratch_shapes=_attn_scratch(MOBA_HEAD_DIM),
        compiler_params=_cparams(("arbitrary", "arbitrary", "arbitrary")),
        name="moba_attn",
    )(q_wide, k_wide, qkv_mo, bias)


def _outproj_kernel(oa_ref, ob_ref, wa_ref, wb_ref, x_ref, mod_ref, g_ref, b_ref, wr_ref,
                    x1_ref, h2_ref, lg_ref, *, alpha):
    y = (jnp.dot(oa_ref[...], wa_ref[...], preferred_element_type=F32)
         + jnp.dot(ob_ref[...], wb_ref[...], preferred_element_type=F32))
    z = alpha * x_ref[...] + mod_ref[0, 2:3, :] * y
    x1 = _ln(z) * g_ref[...] + b_ref[...]
    x1_ref[...] = x1
    h2 = _ln(x1) * (1.0 + mod_ref[0, 4:5, :]) + mod_ref[0, 3:4, :]
    h2_ref[...] = h2
    h_hi = h2.astype(BF16)
    h_lo = (h2 - h_hi.astype(F32)).astype(BF16)
    zz = (jnp.dot(h_hi, wr_ref[...], preferred_element_type=F32)
          + jnp.dot(h_lo, wr_ref[...], preferred_element_type=F32))
    lg_ref[...] = zz + pltpu.roll(zz, LANES // 2, 1)


def _out_proj(o_mla, o_moba, wa, wb, xf, mod3, ln_g, ln_b, wr, seq, alpha):
    t, d = xf.shape
    tm = 512
    per_b = seq // tm
    ka, kb = wa.shape[0], wb.shape[0]
    once = pl.Buffered(1)
    return pl.pallas_call(
        functools.partial(_outproj_kernel, alpha=alpha),
        grid=(t // tm,),
        in_specs=[pl.BlockSpec((tm, ka), lambda i: (i, 0)),
                  pl.BlockSpec((tm, kb), lambda i: (i, 0)),
                  pl.BlockSpec((ka, d), lambda i: (0, 0), pipeline_mode=once),
                  pl.BlockSpec((kb, d), lambda i: (0, 0), pipeline_mode=once),
                  pl.BlockSpec((tm, d), lambda i: (i, 0)),
                  pl.BlockSpec((1, 6, d), lambda i: (i // per_b, 0, 0)),
                  pl.BlockSpec((1, d), lambda i: (0, 0)),
                  pl.BlockSpec((1, d), lambda i: (0, 0)),
                  pl.BlockSpec((d, LANES), lambda i: (0, 0), pipeline_mode=once)],
        out_specs=[pl.BlockSpec((tm, d), lambda i: (i, 0)),
                   pl.BlockSpec((tm, d), lambda i: (i, 0)),
                   pl.BlockSpec((tm, LANES), lambda i: (i, 0))],
        out_shape=[jax.ShapeDtypeStruct((t, d), F32),
                   jax.ShapeDtypeStruct((t, d), F32),
                   jax.ShapeDtypeStruct((t, LANES), F32)],
        compiler_params=_cparams(("arbitrary",)),
        name="out_proj",
    )(o_mla, o_moba, wa, wb, xf, mod3, ln_g, ln_b, wr)


def _route_kernel(lg_ref, br_ref, info_ref, cnt_ref, run_scr, *, tm):
    i = pl.program_id(0)

    @pl.when(i == 0)
    def _():
        run_scr[...] = jnp.zeros_like(run_scr)

    lg = lg_ref[...] + br_ref[...]
    lane = lax.broadcasted_iota(jnp.int32, (tm, LANES), 1)
    e_lo, e_hi = MOE_GROUPS, MOE_GROUPS + MOE_N_EXPERTS
    is_g = lane < e_lo
    gl = jnp.where(is_g, lg, NEG)
    gmax = jnp.max(gl, axis=-1, keepdims=True)
    gidx = jnp.min(jnp.where(gl == gmax, lane, LANES), axis=-1, keepdims=True)
    g_p = 1.0 / jnp.sum(jnp.where(is_g, jnp.exp(gl - gmax), 0.0), axis=-1, keepdims=True)
    grp_of_lane = jnp.right_shift(lane - e_lo, MOE_EXPERTS_PER_GROUP.bit_length() - 1)
    in_grp = (lane >= e_lo) & (lane < e_hi) & (grp_of_lane == gidx)
    el = jnp.where(in_grp, lg, NEG)
    m1 = jnp.max(el, axis=-1, keepdims=True)
    l1 = jnp.min(jnp.where(el == m1, lane, LANES), axis=-1, keepdims=True)
    el2 = jnp.where(lane == l1, NEG, el)
    m2 = jnp.max(el2, axis=-1, keepdims=True)
    l2 = jnp.min(jnp.where(el2 == m2, lane, LANES), axis=-1, keepdims=True)
    zsum = jnp.sum(jnp.where(in_grp, jnp.exp(el - m1), 0.0), axis=-1, keepdims=True)
    p1 = 1.0 / zsum
    p2 = jnp.exp(m2 - m1) / zsum
    wa = g_p * (p1 / (p1 + p2))
    wb = g_p * (p2 / (p1 + p2))
    hot_a = lane == l1
    hot_b = lane == l2
    onehot = jnp.where(hot_a | hot_b, 1.0, 0.0)
    r = lax.broadcasted_iota(jnp.int32, (tm, tm), 0)
    c = lax.broadcasted_iota(jnp.int32, (tm, tm), 1)
    lower = jnp.where(c < r, 1.0, 0.0).astype(BF16)
    before = jnp.dot(lower, onehot.astype(BF16), preferred_element_type=F32) + run_scr[...]
    rank_a = jnp.sum(jnp.where(hot_a, before, 0.0), axis=-1, keepdims=True)
    rank_b = jnp.sum(jnp.where(hot_b, before, 0.0), axis=-1, keepdims=True)
    run_scr[...] += jnp.sum(onehot, axis=0, keepdims=True)
    info = jnp.zeros((tm, LANES), F32)
    for k, val in enumerate([(l1 - e_lo).astype(F32), (l2 - e_lo).astype(F32), wa, wb, rank_a, rank_b]):
        info = jnp.where(lane == k, val, info)
    info_ref[...] = info
    cnt_ref[...] = run_scr[...]


def _route(logits, br):
    t = logits.shape[0]
    tm = 512
    return pl.pallas_call(
        functools.partial(_route_kernel, tm=tm),
        grid=(t // tm,),
        in_specs=[pl.BlockSpec((tm, LANES), lambda i: (i, 0)),
                  pl.BlockSpec((1, LANES), lambda i: (0, 0))],
        out_specs=[pl.BlockSpec((tm, LANES), lambda i: (i, 0)),
                   pl.BlockSpec((1, LANES), lambda i: (0, 0))],
        out_shape=[jax.ShapeDtypeStruct((t, LANES), F32),
                   jax.ShapeDtypeStruct((1, LANES), F32)],
        scratch_shapes=[pltpu.VMEM((1, LANES), F32)],
        compiler_params=_cparams(("arbitrary",)),
        name="route",
    )(logits, br)


def _pos_kernel(info_ref, start_ref, pos_ref):
    info = info_ref[...]
    tm = info.shape[0]
    lane = lax.broadcasted_iota(jnp.int32, (tm, LANES), 1)
    start = start_ref[...]
    cols = []
    for k in range(2):
        e = jnp.sum(jnp.where(lane == k, info, 0.0), axis=-1, keepdims=True).astype(jnp.int32)
        rank = jnp.sum(jnp.where(lane == 4 + k, info, 0.0), axis=-1, keepdims=True)
        base = jnp.sum(jnp.where(lane == e + MOE_GROUPS, start, 0.0), axis=-1, keepdims=True)
        cols.append(base + rank)
    pos_ref[...] = jnp.where(lane == 0, cols[0], jnp.where(lane == 1, cols[1], 0.0)).astype(jnp.int32)


def _positions(info, start_lanes):
    t = info.shape[0]
    tm = 1024
    return pl.pallas_call(
        _pos_kernel,
        grid=(t // tm,),
        in_specs=[pl.BlockSpec((tm, LANES), lambda i: (i, 0)),
                  pl.BlockSpec((1, LANES), lambda i: (0, 0))],
        out_specs=pl.BlockSpec((tm, LANES), lambda i: (i, 0)),
        out_shape=jax.ShapeDtypeStruct((t, LANES), jnp.int32),
        compiler_params=_cparams(("arbitrary",)),
        name="positions",
    )(info, start_lanes)


def _row_copy(src_ref, src_row, dst_ref, dst_row, sem):
    return pltpu.make_async_copy(src_ref.at[pl.ds(src_row, 1)], dst_ref.at[pl.ds(dst_row, 1)], sem)


def _dispatch_kernel(pa_ref, pb_ref, h_ref, xs_in_ref, xs_ref, sem, *, tm):
    del xs_in_ref
    base = pl.program_id(0) * tm

    def issue(t, _):
        _row_copy(h_ref, t, xs_ref, pa_ref[base + t], sem).start()
        _row_copy(h_ref, t, xs_ref, pb_ref[base + t], sem).start()
        return 0

    lax.fori_loop(0, tm, issue, 0)

    def drain(t, _):
        _row_copy(h_ref, 0, xs_ref, 0, sem).wait()
        _row_copy(h_ref, 0, xs_ref, 0, sem).wait()
        return 0

    lax.fori_loop(0, tm, drain, 0)


def _dispatch(pos_a, pos_b, h2, rows):
    t, d = h2.shape
    tm = 256
    xs0 = jnp.zeros((rows, d), F32)
    grid_spec = pltpu.PrefetchScalarGridSpec(
        num_scalar_prefetch=2,
        grid=(t // tm,),
        in_specs=[pl.BlockSpec((tm, d), lambda i, pa, pb: (i, 0)),
                  pl.BlockSpec(memory_space=pl.ANY)],
        out_specs=pl.BlockSpec(memory_space=pl.ANY),
        scratch_shapes=[pltpu.SemaphoreType.DMA(())],
    )
    return pl.pallas_call(
        functools.partial(_dispatch_kernel, tm=tm),
        grid_spec=grid_spec,
        out_shape=jax.ShapeDtypeStruct((rows, d), F32),
        input_output_aliases={3: 0},
        compiler_params=_cparams(("arbitrary",)),
        name="dispatch",
    )(pos_a, pos_b, h2, xs0)


def _experts_kernel(texp_ref, tidx_ref, nexte_ref, used_ref, x_ref, w1_hbm, w3_hbm, w2_hbm, o_ref,
                    w1_f32, w3_f32, w2_f32, w1_scr, w3_scr, w2_scr, sems):
    i = pl.program_id(0)
    prev = texp_ref[jnp.maximum(i - 1, 0)]

    def weight_copies(e):
        return [pltpu.make_async_copy(w1_hbm.at[e], w1_f32, sems.at[0]),
                pltpu.make_async_copy(w3_hbm.at[e], w3_f32, sems.at[1]),
                pltpu.make_async_copy(w2_hbm.at[e], w2_f32, sems.at[2])]

    @pl.when(i == 0)
    def _():
        for cp in weight_copies(texp_ref[0]):
            cp.start()

    @pl.when((i == 0) | (texp_ref[i] != prev))
    def _():
        for cp in weight_copies(texp_ref[i]):
            cp.wait()
        w1_scr[...] = w1_f32[...].astype(BF16)
        w3_scr[...] = w3_f32[...].astype(BF16)
        w2_scr[...] = w2_f32[...].astype(BF16)

        @pl.when(nexte_ref[i] >= 0)
        def _():
            for cp in weight_copies(nexte_ref[i]):
                cp.start()

    @pl.when(i < used_ref[0])
    def _():
        x = x_ref[...].astype(BF16)
        a = jnp.dot(x, w1_scr[...], preferred_element_type=F32)
        b = jnp.dot(x, w3_scr[...], preferred_element_type=F32)
        hid = (a / (1.0 + jnp.exp(-a))) * b
        o_ref[...] = jnp.dot(hid.astype(BF16), w2_scr[...], preferred_element_type=F32)

    @pl.when(i >= used_ref[0])
    def _():
        o_ref[...] = jnp.zeros_like(o_ref)


def _experts(texp, tidx, nexte, used, xs, w1, w3, w2, tr):
    rows, d = xs.shape
    nt = rows // tr
    f = w1.shape[2]
    grid_spec = pltpu.PrefetchScalarGridSpec(
        num_scalar_prefetch=4,
        grid=(nt,),
        in_specs=[pl.BlockSpec((tr, d), lambda i, te, ti, ne, u: (ti[i], 0)),
                  pl.BlockSpec(memory_space=pl.ANY),
                  pl.BlockSpec(memory_space=pl.ANY),
                  pl.BlockSpec(memory_space=pl.ANY)],
        out_specs=pl.BlockSpec((tr, d), lambda i, te, ti, ne, u: (i, 0)),
        scratch_shapes=[pltpu.VMEM((d, f), F32), pltpu.VMEM((d, f), F32), pltpu.VMEM((f, d), F32),
                        pltpu.VMEM((d, f), BF16), pltpu.VMEM((d, f), BF16), pltpu.VMEM((f, d), BF16),
                        pltpu.SemaphoreType.DMA((3,))],
    )
    return pl.pallas_call(
        _experts_kernel,
        grid_spec=grid_spec,
        out_shape=jax.ShapeDtypeStruct((rows, d), F32),
        compiler_params=_cparams(("arbitrary",)),
        name="experts",
    )(texp, tidx, nexte, used, xs, w1, w3, w2)


def _combine_kernel(pa_ref, pb_ref, ys_ref, x1_ref, info_ref, mod_ref, g_ref, b_ref, o_ref,
                    buf_a, buf_b, sem, *, tm, alpha):
    base = pl.program_id(0) * tm

    def issue(t, _):
        _row_copy(ys_ref, pa_ref[base + t], buf_a, t, sem).start()
        _row_copy(ys_ref, pb_ref[base + t], buf_b, t, sem).start()
        return 0

    lax.fori_loop(0, tm, issue, 0)

    def drain(t, _):
        _row_copy(ys_ref, 0, buf_a, 0, sem).wait()
        _row_copy(ys_ref, 0, buf_b, 0, sem).wait()
        return 0

    lax.fori_loop(0, tm, drain, 0)
    info = info_ref[...]
    y = info[:, 2:3] * buf_a[...] + info[:, 3:4] * buf_b[...]
    z = alpha * x1_ref[...] + mod_ref[0, 5:6, :] * y
    o_ref[...] = _ln(z) * g_ref[...] + b_ref[...]


def _combine(pos_a, pos_b, ys, x1, info, mod3, ln_g, ln_b, seq, alpha):
    t, d = x1.shape
    tm = 256
    per_b = seq // tm
    grid_spec = pltpu.PrefetchScalarGridSpec(
        num_scalar_prefetch=2,
        grid=(t // tm,),
        in_specs=[pl.BlockSpec(memory_space=pl.ANY),
                  pl.BlockSpec((tm, d), lambda i, pa, pb: (i, 0)),
                  pl.BlockSpec((tm, LANES), lambda i, pa, pb: (i, 0)),
                  pl.BlockSpec((1, 6, d), lambda i, pa, pb: (i // per_b, 0, 0)),
                  pl.BlockSpec((1, d), lambda i, pa, pb: (0, 0)),
                  pl.BlockSpec((1, d), lambda i, pa, pb: (0, 0))],
        out_specs=pl.BlockSpec((tm, d), lambda i, pa, pb: (i, 0)),
        scratch_shapes=[pltpu.VMEM((tm, d), F32), pltpu.VMEM((tm, d), F32), pltpu.SemaphoreType.DMA(())],
    )
    return pl.pallas_call(
        functools.partial(_combine_kernel, tm=tm, alpha=alpha),
        grid_spec=grid_spec,
        out_shape=jax.ShapeDtypeStruct((t, d), F32),
        compiler_params=_cparams(("arbitrary",)),
        name="combine",
    )(pos_a, pos_b, ys, x1, info, mod3, ln_g, ln_b)


def _prep_w_in(w):
    cq = w[:, :MLA_Q_RANK]
    ckv = w[:, MLA_Q_RANK:MLA_Q_RANK + MLA_KV_RANK]
    r0 = MLA_Q_RANK + MLA_KV_RANK
    kr = w[:, r0:r0 + MLA_ROPE_DIM]
    mo = w[:, r0 + MLA_ROPE_DIM:]
    half = MLA_ROPE_DIM // 2
    z = jnp.zeros((w.shape[0], LANES - MLA_ROPE_DIM), w.dtype)
    return jnp.concatenate([cq, ckv, kr, z, kr[:, half:], kr[:, :half], z, mo], axis=1).astype(BF16)


def _prep_w_uq(w):
    r = w.shape[0]
    w = w.reshape(r, MLA_HEADS, MLA_NOPE_DIM + MLA_ROPE_DIM)
    half = MLA_ROPE_DIM // 2
    nope = w[:, :, :MLA_NOPE_DIM]
    x1 = w[:, :, MLA_NOPE_DIM:MLA_NOPE_DIM + half]
    x2 = w[:, :, MLA_NOPE_DIM + half:]
    z = jnp.zeros((r, MLA_HEADS, LANES - MLA_ROPE_DIM), w.dtype)
    return jnp.concatenate([nope, x1, x2, z, x2, x1, z], axis=2).reshape(r, MLA_HEADS * 3 * LANES).astype(BF16)


def _rope_lanes(seq):
    inv = 1.0 / (ROPE_THETA ** (jnp.arange(0, MLA_ROPE_DIM, 2, dtype=F32) / MLA_ROPE_DIM))
    ang = jnp.arange(seq, dtype=F32)[:, None] * inv[None, :]
    cos, sin = jnp.cos(ang), jnp.sin(ang)
    z = jnp.zeros((seq, LANES - MLA_ROPE_DIM), F32)
    return jnp.concatenate([cos, cos, z], axis=1), jnp.concatenate([-sin, sin, z], axis=1)


def _prep_router(w_rg, b_rg, w_re, b_re):
    d = w_rg.shape[0]
    w = jnp.concatenate([w_rg, w_re], axis=1)
    n = w.shape[1]
    hi = w.astype(BF16)
    lo = (w - hi.astype(F32)).astype(BF16)
    z = jnp.zeros((d, LANES // 2 - n), BF16)
    wr = jnp.concatenate([hi, z, lo, z], axis=1)
    br = jnp.zeros((1, LANES), F32).at[0, :n].set(jnp.concatenate([b_rg, b_re]))
    return wr, br


def _layer(xf, mod3, batch, seq, depth_alpha, w_in, q_norm_g, w_uq, kv_norm_g, w_ukv, w_out, bias_tiles,
           cos_t, sin_t, ln1_g, ln1_b, w_rg, b_rg, w_re, b_re, w1, w3, w2, ln2_g, ln2_b):
    t, d = xf.shape
    part_a, qkv_mo = _in_proj(xf, mod3, _prep_w_in(w_in), seq)
    q, k, v = _qkv(part_a, _prep_w_uq(w_uq), w_ukv.astype(BF16), q_norm_g.reshape(1, -1),
                   kv_norm_g.reshape(1, -1), cos_t, sin_t, seq)
    o_mla = _mla_attention(q, k, v, batch, seq)
    q_wide, k_wide = _moba_select(qkv_mo, batch, seq)
    o_moba = _moba_attention(q_wide, k_wide, qkv_mo, bias_tiles, batch, seq)
    wo = w_out.astype(BF16)
    wr, br = _prep_router(w_rg, b_rg, w_re, b_re)
    n_mla = MLA_HEADS * MLA_V_DIM
    x1, h2, logits = _out_proj(o_mla, o_moba, wo[:n_mla], wo[n_mla:], xf, mod3, ln1_g.reshape(1, d),
                               ln1_b.reshape(1, d), wr, seq, depth_alpha)
    info, counts = _route(logits, br)
    tr = 256
    nt = (2 * t) // tr + MOE_N_EXPERTS
    cnt = counts[0, MOE_GROUPS:MOE_GROUPS + MOE_N_EXPERTS].astype(jnp.int32)
    ntile = (cnt + tr - 1) // tr
    tile_end = jnp.cumsum(ntile)
    tile_start = tile_end - ntile
    used = tile_end[-1]
    start_lanes = jnp.zeros((1, LANES), F32).at[0, MOE_GROUPS:MOE_GROUPS + MOE_N_EXPERTS].set(
        (tile_start * tr).astype(F32))
    pos = _positions(info, start_lanes)
    pos_a, pos_b = pos[:, 0], pos[:, 1]
    tidx = jnp.minimum(jnp.arange(nt, dtype=jnp.int32), used - 1)
    texp = jnp.sum(tidx[:, None] >= tile_end[None, :], axis=1).astype(jnp.int32)
    eids = jnp.arange(MOE_N_EXPERTS, dtype=jnp.int32)
    later = (eids[None, :] > eids[:, None]) & (ntile[None, :] > 0)
    next_nonempty = jnp.where(later.any(axis=1), jnp.argmax(later, axis=1), -1).astype(jnp.int32)
    nexte = jnp.sum(jnp.where(texp[:, None] == eids[None, :], next_nonempty[None, :], 0), axis=1).astype(jnp.int32)
    xs = _dispatch(pos_a, pos_b, h2, nt * tr)
    ys = _experts(texp, tidx, nexte, used.reshape(1).astype(jnp.int32), xs, w1, w3, w2, tr)
    return _combine(pos_a, pos_b, ys, x1, info, mod3, ln2_g.reshape(1, d), ln2_b.reshape(1, d), seq, depth_alpha)


def kernel(x, c, w_ada, b_ada, w_in, q_norm_g, w_uq, kv_norm_g, w_ukv, w_out, t5_table, ln1_g, ln1_b,
           w_router_group, b_router_group, w_router_expert, b_router_expert, w1, w3, w2, ln2_g, ln2_b):
    batch, seq, d = x.shape
    depth = w_ada.shape[0]
    alpha = (2.0 * depth) ** 0.25
    cos_t, sin_t = _rope_lanes(seq)
    bias_tiles = _t5_tiles(t5_table)
    xf = x.reshape(batch * seq, d)
    for l in range(depth):
        mod3 = _ada_mod(c, w_ada[l], b_ada[l]).reshape(batch, 6, d)
        xf = _layer(xf, mod3, batch, seq, alpha, w_in[l], q_norm_g[l], w_uq[l], kv_norm_g[l], w_ukv[l],
                    w_out[l], bias_tiles, cos_t, sin_t, ln1_g[l], ln1_b[l], w_router_group[l],
                    b_router_group[l], w_router_expert[l], b_router_expert[l], w1[l], w3[l], w2[l],
                    ln2_g[l], ln2_b[l])
    return xf.reshape(batch, seq, d)
```

```python
import functools
import math

import jax
import jax.numpy as jnp
from jax import lax
from jax.experimental import pallas as pl
from jax.experimental.pallas import tpu as pltpu

D_MODEL = 2048
MLA_HEADS = 8
MLA_Q_RANK = 512
MLA_KV_RANK = 256
MLA_NOPE_DIM = 128
MLA_ROPE_DIM = 64
MLA_V_DIM = 128
ROPE_THETA = 10000.0
MOBA_HEADS = 8
MOBA_HEAD_DIM = 128
MOBA_BLOCK = 256
MOBA_TOPK = 3
T5_BUCKETS = 32
T5_MAX_DISTANCE = 128
MOE_GROUPS = 4
MOE_EXPERTS_PER_GROUP = 8
MOE_N_EXPERTS = MOE_GROUPS * MOE_EXPERTS_PER_GROUP
MOE_D_FF = 512
LN_EPS = 1e-5
RMS_EPS = 1e-6
MOBA_WIDTH = MOBA_HEADS * MOBA_HEAD_DIM

LANES = 128
QK_PAD = 256
V_PAD = 256
PART_A = 1024
NEG = -1e30
VMEM_LIMIT = 56 * 1024 * 1024

F32 = jnp.float32
BF16 = jnp.bfloat16


def _cparams(sem):
    return pltpu.CompilerParams(dimension_semantics=sem, vmem_limit_bytes=VMEM_LIMIT)


def _ln(x):
    mu = jnp.mean(x, axis=-1, keepdims=True)
    xc = x - mu
    var = jnp.mean(xc * xc, axis=-1, keepdims=True)
    return xc * lax.rsqrt(var + LN_EPS)


def _nt_dot(a, b):
    return lax.dot_general(a, b, (((1,), (1,)), ((), ())), preferred_element_type=F32)


def _ada_kernel(ct_ref, w_ref, b_ref, o_ref, *, batch):
    ct = ct_ref[...]
    ca = ct / (1.0 + jnp.exp(-ct))
    w = w_ref[...]
    rows = [jnp.sum(w * ca[:, b:b + 1], axis=0, keepdims=True) for b in range(batch)]
    o_ref[...] = jnp.concatenate(rows, axis=0) + b_ref[...]


def _ada_mod(c, w_ada, b_ada):
    batch, d = c.shape
    n = w_ada.shape[1]
    tn = 512
    ct = jnp.zeros((d, LANES), F32).at[:, :batch].set(c.T)
    return pl.pallas_call(
        functools.partial(_ada_kernel, batch=batch),
        grid=(n // tn,),
        in_specs=[pl.BlockSpec((d, LANES), lambda j: (0, 0)),
                  pl.BlockSpec((d, tn), lambda j: (0, j)),
                  pl.BlockSpec((1, tn), lambda j: (0, j))],
        out_specs=pl.BlockSpec((batch, tn), lambda j: (0, j)),
        out_shape=jax.ShapeDtypeStruct((batch, n), F32),
        compiler_params=_cparams(("arbitrary",)),
        name="ada_mod",
    )(ct, w_ada, b_ada.reshape(1, n))


def _inproj_kernel(x_ref, mod_ref, w_ref, a_ref, b_ref, h_scr):
    j = pl.program_id(1)

    @pl.when(j == 0)
    def _():
        h = _ln(x_ref[...]) * (1.0 + mod_ref[0, 1:2, :]) + mod_ref[0, 0:1, :]
        h_scr[...] = h.astype(BF16)
        a_ref[...] = jnp.dot(h_scr[...], w_ref[...], preferred_element_type=F32)

    @pl.when(j > 0)
    def _():
        b_ref[...] = jnp.dot(h_scr[...], w_ref[...], preferred_element_type=F32).astype(BF16)


def _in_proj(xf, mod3, w4, seq):
    t, d = xf.shape
    n = w4.shape[1]
    tm, tn = 512, PART_A
    per_b = seq // tm
    return pl.pallas_call(
        _inproj_kernel,
        grid=(t // tm, n // tn),
        in_specs=[pl.BlockSpec((tm, d), lambda i, j: (i, 0)),
                  pl.BlockSpec((1, 6, d), lambda i, j: (i // per_b, 0, 0)),
                  pl.BlockSpec((d, tn), lambda i, j: (0, j))],
        out_specs=[pl.BlockSpec((tm, tn), lambda i, j: (i, 0)),
                   pl.BlockSpec((tm, tn), lambda i, j: (i, jnp.maximum(j - 1, 0)))],
        out_shape=[jax.ShapeDtypeStruct((t, PART_A), F32),
                   jax.ShapeDtypeStruct((t, n - PART_A), BF16)],
        scratch_shapes=[pltpu.VMEM((tm, d), BF16)],
        compiler_params=_cparams(("arbitrary", "arbitrary")),
        name="in_proj",
    )(xf, mod3, w4)


def _qkv_kernel(a_ref, wq_ref, wkv_ref, gq_ref, gkv_ref, cos_ref, sin_ref, q_ref, k_ref, v_ref):
    a = a_ref[...]
    cq = a[:, :MLA_Q_RANK]
    ckv = a[:, MLA_Q_RANK:MLA_Q_RANK + MLA_KV_RANK]
    kr = a[:, 768:896]
    krs = a[:, 896:1024]
    cqn = (cq * lax.rsqrt(jnp.mean(cq * cq, axis=-1, keepdims=True) + RMS_EPS) * gq_ref[...]).astype(BF16)
    ckvn = (ckv * lax.rsqrt(jnp.mean(ckv * ckv, axis=-1, keepdims=True) + RMS_EPS) * gkv_ref[...]).astype(BF16)
    q3 = jnp.dot(cqn, wq_ref[...], preferred_element_type=F32)
    kv = jnp.dot(ckvn, wkv_ref[...], preferred_element_type=F32)
    cos = cos_ref[...]
    sin = sin_ref[...]
    krr = (kr * cos + krs * sin).astype(BF16)
    for h in range(MLA_HEADS):
        q0 = h * 3 * LANES
        c0 = h * QK_PAD
        q_ref[:, c0:c0 + LANES] = q3[:, q0:q0 + LANES].astype(BF16)
        q_ref[:, c0 + LANES:c0 + QK_PAD] = (
            q3[:, q0 + LANES:q0 + 2 * LANES] * cos + q3[:, q0 + 2 * LANES:q0 + 3 * LANES] * sin).astype(BF16)
        k_ref[:, c0:c0 + LANES] = kv[:, c0:c0 + LANES].astype(BF16)
        k_ref[:, c0 + LANES:c0 + QK_PAD] = krr
        v_ref[:, c0:c0 + LANES] = kv[:, c0 + LANES:c0 + QK_PAD].astype(BF16)
        v_ref[:, c0 + LANES:c0 + V_PAD] = jnp.ones((a.shape[0], V_PAD - LANES), BF16)


def _qkv(part_a, wq3, wkv, gq, gkv, cos_t, sin_t, seq):
    t = part_a.shape[0]
    tm = 512
    per_b = seq // tm
    hq = MLA_HEADS * QK_PAD
    return pl.pallas_call(
        _qkv_kernel,
        grid=(t // tm,),
        in_specs=[pl.BlockSpec((tm, PART_A), lambda i: (i, 0)),
                  pl.BlockSpec(wq3.shape, lambda i: (0, 0)),
                  pl.BlockSpec(wkv.shape, lambda i: (0, 0)),
                  pl.BlockSpec((1, MLA_Q_RANK), lambda i: (0, 0)),
                  pl.BlockSpec((1, MLA_KV_RANK), lambda i: (0, 0)),
                  pl.BlockSpec((tm, LANES), lambda i: (i % per_b, 0)),
                  pl.BlockSpec((tm, LANES), lambda i: (i % per_b, 0))],
        out_specs=[pl.BlockSpec((tm, hq), lambda i: (i, 0)),
                   pl.BlockSpec((tm, hq), lambda i: (i, 0)),
                   pl.BlockSpec((tm, MLA_HEADS * V_PAD), lambda i: (i, 0))],
        out_shape=[jax.ShapeDtypeStruct((t, hq), BF16),
                   jax.ShapeDtypeStruct((t, hq), BF16),
                   jax.ShapeDtypeStruct((t, MLA_HEADS * V_PAD), BF16)],
        compiler_params=_cparams(("arbitrary",)),
        name="qkv",
    )(part_a, wq3, wkv, gq, gkv, cos_t, sin_t)


ATTN_TILE = 512


def _attn_kernel(*refs, c, masked):
    if masked:
        q_ref, k_ref, v_ref, bias_ref, o_ref, s_a, s_b, m_scr, acc_scr = refs
    else:
        q_ref, k_ref, v_ref, o_ref, s_a, s_b, m_scr, acc_scr = refs
    tile = ATTN_TILE
    qi = pl.program_id(2)

    def put_scores(buf, t):
        k = k_ref[pl.ds(pl.multiple_of(t * tile, tile), tile), :]
        buf[...] = _nt_dot(q_ref[...], k)

    def tile_step(buf, t, kind, nxt=None):
        if nxt is not None:
            put_scores(*nxt)
        s = buf[...]
        if masked:
            if kind == "prev":
                s = s + bias_ref[0, 1] + jnp.where(qi == 0, NEG, 0.0)
            elif kind == "diag":
                s = s + bias_ref[0, 0]
        elif kind == "diag":
            row = lax.broadcasted_iota(jnp.int32, (tile, tile), 0)
            col = lax.broadcasted_iota(jnp.int32, (tile, tile), 1)
            s = jnp.where(row >= col, s, NEG)
        m_old = m_scr[...]
        m_new = jnp.maximum(m_old, jnp.broadcast_to(jnp.max(s, axis=-1, keepdims=True), m_old.shape))
        alpha = jnp.exp2((m_old - m_new) * c)
        p = jnp.concatenate([jnp.exp2(((s[:, j * LANES:(j + 1) * LANES] - m_new) * c).astype(BF16))
                             for j in range(tile // LANES)], axis=1)
        v = v_ref[pl.ds(pl.multiple_of(t * tile, tile), tile), :]
        pv = jnp.dot(p, v, preferred_element_type=F32)
        acc_scr[...] = jnp.concatenate([alpha, alpha], axis=1) * acc_scr[...] + pv
        m_scr[...] = m_new

    m_scr[...] = jnp.full(m_scr.shape, NEG, F32)
    acc_scr[...] = jnp.zeros(acc_scr.shape, F32)
    put_scores(s_a, 0)

    n_far = jnp.maximum(qi - 1, 0) if masked else qi
    pn = jnp.maximum(qi - 1, 0)

    def pair(j, _):
        t = 2 * j
        tile_step(s_a, t, "far", (s_b, t + 1))
        tile_step(s_b, t + 1, "far", (s_a, t + 2))
        return 0

    lax.fori_loop(0, n_far // 2, pair, 0)
    t0 = 2 * (n_far // 2)

    def tail(first, second):
        if masked:
            tile_step(first, pn, "prev", (second, qi))
            tile_step(second, qi, "diag")
        else:
            tile_step(first, qi, "diag")

    @pl.when(n_far % 2 == 1)
    def _():
        tile_step(s_a, t0, "far", (s_b, t0 + 1))
        tail(s_b, s_a)

    @pl.when(n_far % 2 == 0)
    def _():
        tail(s_a, s_b)

    o_ref[...] = (acc_scr[:, :LANES] / acc_scr[:, LANES:]).astype(o_ref.dtype)


def _attn_scratch():
    tile = ATTN_TILE
    return [pltpu.VMEM((tile, tile), F32), pltpu.VMEM((tile, tile), F32),
            pltpu.VMEM((tile, LANES), F32), pltpu.VMEM((tile, V_PAD), F32)]


def _mla_attention(q, k, v, batch, seq):
    tile = ATTN_TILE
    nq = seq // tile
    c = math.log2(math.e) / math.sqrt(MLA_NOPE_DIM + MLA_ROPE_DIM)
    return pl.pallas_call(
        functools.partial(_attn_kernel, c=c, masked=False),
        grid=(batch, MLA_HEADS, nq),
        in_specs=[pl.BlockSpec((tile, QK_PAD), lambda b, h, i: (b * nq + i, h)),
                  pl.BlockSpec((seq, QK_PAD), lambda b, h, i: (b, h)),
                  pl.BlockSpec((seq, V_PAD), lambda b, h, i: (b, h))],
        out_specs=pl.BlockSpec((tile, MLA_V_DIM), lambda b, h, i: (b * nq + i, h)),
        out_shape=jax.ShapeDtypeStruct((batch * seq, MLA_HEADS * MLA_V_DIM), BF16),
        scratch_shapes=_attn_scratch(),
        compiler_params=_cparams(("arbitrary", "arbitrary", "arbitrary")),
        name="mla_attn",
    )(q, k, v)


def _moba_select_kernel(q_ref, k_ref, v_ref, qa_ref, ka_ref, va_ref, *, seq, nb):
    kf = k_ref[...].astype(F32)
    km = jnp.sum(kf.reshape(nb, MOBA_BLOCK, MOBA_HEAD_DIM), axis=1) * (1.0 / MOBA_BLOCK)
    km = jnp.concatenate([km, jnp.zeros((LANES - nb, MOBA_HEAD_DIM), F32)], axis=0)
    km_hi = km.astype(BF16)
    km_lo = (km - km_hi.astype(F32)).astype(BF16)
    q = q_ref[...]
    gate = _nt_dot(q, km_hi) + _nt_dot(q, km_lo)
    blk = lax.broadcasted_iota(jnp.int32, (seq, LANES), 1)
    qblk = jnp.right_shift(lax.broadcasted_iota(jnp.int32, (seq, LANES), 0), MOBA_BLOCK.bit_length() - 1)
    g = jnp.where(blk < qblk, gate, NEG)
    visible = blk == qblk
    for _ in range(MOBA_TOPK):
        mx = jnp.max(g, axis=-1, keepdims=True)
        first = jnp.min(jnp.where(g == mx, blk, LANES), axis=-1, keepdims=True)
        pick = (blk == first) & (mx > 0.5 * NEG)
        visible = visible | pick
        g = jnp.where(pick, NEG, g)
    in_use = blk < nb
    qa_ref[:, :MOBA_HEAD_DIM] = q
    qa_ref[:, MOBA_HEAD_DIM:] = jnp.where(visible | ~in_use, 0.0, NEG).astype(BF16)
    ka_ref[:, :MOBA_HEAD_DIM] = k_ref[...]
    ka_ref[:, MOBA_HEAD_DIM:] = jnp.where(blk == qblk, 1.0, 0.0).astype(BF16)
    va_ref[:, :MOBA_HEAD_DIM] = v_ref[...]
    va_ref[:, MOBA_HEAD_DIM:] = jnp.ones((seq, V_PAD - MOBA_HEAD_DIM), BF16)


def _moba_select(qkv_mo, batch, seq):
    nb = seq // MOBA_BLOCK
    assert MOBA_HEAD_DIM == LANES and nb <= QK_PAD - MOBA_HEAD_DIM
    wide = jax.ShapeDtypeStruct((batch * seq, MOBA_HEADS * QK_PAD), BF16)
    return pl.pallas_call(
        functools.partial(_moba_select_kernel, seq=seq, nb=nb),
        grid=(batch, MOBA_HEADS),
        in_specs=[pl.BlockSpec((seq, MOBA_HEAD_DIM), lambda b, h: (b, h)),
                  pl.BlockSpec((seq, MOBA_HEAD_DIM), lambda b, h: (b, MOBA_HEADS + h)),
                  pl.BlockSpec((seq, MOBA_HEAD_DIM), lambda b, h: (b, 2 * MOBA_HEADS + h))],
        out_specs=[pl.BlockSpec((seq, QK_PAD), lambda b, h: (b, h)),
                   pl.BlockSpec((seq, QK_PAD), lambda b, h: (b, h)),
                   pl.BlockSpec((seq, V_PAD), lambda b, h: (b, h))],
        out_shape=[wide, wide, jax.ShapeDtypeStruct((batch * seq, MOBA_HEADS * V_PAD), BF16)],
        compiler_params=_cparams(("arbitrary", "arbitrary")),
        name="moba_select",
    )(qkv_mo, qkv_mo, qkv_mo)


def _t5_kernel(tab_ref, o_ref, *, inv_scale):
    h = pl.program_id(0)
    r = lax.broadcasted_iota(jnp.int32, (LANES, LANES), 0)
    c = lax.broadcasted_iota(jnp.int32, (LANES, LANES), 1)
    max_exact = T5_BUCKETS // 2
    far = tab_ref[T5_BUCKETS - 1, h]

    def block(offset):
        rel = offset + r - c
        n = jnp.maximum(rel, 0)
        nf = jnp.maximum(n, 1).astype(F32)
        large = max_exact + (jnp.log(nf / max_exact) / math.log(T5_MAX_DISTANCE / max_exact)
                             * (T5_BUCKETS - max_exact)).astype(jnp.int32)
        large = jnp.minimum(large, T5_BUCKETS - 1)
        bucket = jnp.where(n < max_exact, n, large)
        bias = jnp.zeros((LANES, LANES), F32)
        for j in range(T5_BUCKETS):
            bias = jnp.where(bucket == j, tab_ref[j, h], bias)
        return jnp.where(rel >= 0, (bias - far) * inv_scale, NEG)

    near = {0: block(0), 1: block(LANES)}
    nblk = ATTN_TILE // LANES
    for d in range(2):
        for i in range(nblk):
            for j in range(nblk):
                k = d * nblk + i - j
                if k < 0:
                    val = jnp.full((LANES, LANES), NEG, F32)
                else:
                    val = near.get(k, jnp.zeros((LANES, LANES), F32))
                o_ref[0, d, i * LANES:(i + 1) * LANES, j * LANES:(j + 1) * LANES] = val


def _t5_tiles(t5_table):
    assert LANES >= T5_MAX_DISTANCE
    tile = ATTN_TILE
    return pl.pallas_call(
        functools.partial(_t5_kernel, inv_scale=math.sqrt(MOBA_HEAD_DIM)),
        grid=(MOBA_HEADS,),
        in_specs=[pl.BlockSpec(memory_space=pltpu.SMEM)],
        out_specs=pl.BlockSpec((1, 2, tile, tile), lambda h: (h, 0, 0, 0)),
        out_shape=jax.ShapeDtypeStruct((MOBA_HEADS, 2, tile, tile), F32),
        compiler_params=_cparams(("arbitrary",)),
        name="t5_tiles",
    )(t5_table)


def _moba_attention(q_wide, k_wide, v_wide, bias, batch, seq):
    tile = ATTN_TILE
    nq = seq // tile
    c = math.log2(math.e) / math.sqrt(MOBA_HEAD_DIM)
    dh = MOBA_HEAD_DIM
    return pl.pallas_call(
        functools.partial(_attn_kernel, c=c, masked=True),
        grid=(batch, MOBA_HEADS, nq),
        in_specs=[pl.BlockSpec((tile, QK_PAD), lambda b, h, i: (b * nq + i, h)),
                  pl.BlockSpec((seq, QK_PAD), lambda b, h, i: (b, h)),
                  pl.BlockSpec((seq, V_PAD), lambda b, h, i: (b, h)),
                  pl.BlockSpec((1, 2, tile, tile), lambda b, h, i: (h, 0, 0, 0))],
        out_specs=pl.BlockSpec((tile, dh), lambda b, h, i: (b * nq + i, h)),
        out_shape=jax.ShapeDtypeStruct((batch * seq, MOBA_WIDTH), BF16),
        scratch_shapes=_attn_scratch(),
        compiler_params=_cparams(("arbitrary", "arbitrary", "arbitrary")),
        name="moba_attn",
    )(q_wide, k_wide, v_wide, bias)


def _outproj_kernel(oa_ref, ob_ref, wa_ref, wb_ref, x_ref, mod_ref, g_ref, b_ref, wr_ref,
                    x1_ref, h2_ref, lg_ref, *, alpha):
    y = (jnp.dot(oa_ref[...], wa_ref[...], preferred_element_type=F32)
         + jnp.dot(ob_ref[...], wb_ref[...], preferred_element_type=F32))
    z = alpha * x_ref[...] + mod_ref[0, 2:3, :] * y
    x1 = _ln(z) * g_ref[...] + b_ref[...]
    x1_ref[...] = x1
    h2 = _ln(x1) * (1.0 + mod_ref[0, 4:5, :]) + mod_ref[0, 3:4, :]
    h2_ref[...] = h2
    h_hi = h2.astype(BF16)
    h_lo = (h2 - h_hi.astype(F32)).astype(BF16)
    zz = (jnp.dot(h_hi, wr_ref[...], preferred_element_type=F32)
          + jnp.dot(h_lo, wr_ref[...], preferred_element_type=F32))
    lg_ref[...] = zz + pltpu.roll(zz, LANES // 2, 1)


def _out_proj(o_mla, o_moba, wa, wb, xf, mod3, ln_g, ln_b, wr, seq, alpha):
    t, d = xf.shape
    tm = 512
    per_b = seq // tm
    ka, kb = wa.shape[0], wb.shape[0]
    once = pl.Buffered(1)
    return pl.pallas_call(
        functools.partial(_outproj_kernel, alpha=alpha),
        grid=(t // tm,),
        in_specs=[pl.BlockSpec((tm, ka), lambda i: (i, 0)),
                  pl.BlockSpec((tm, kb), lambda i: (i, 0)),
                  pl.BlockSpec((ka, d), lambda i: (0, 0), pipeline_mode=once),
                  pl.BlockSpec((kb, d), lambda i: (0, 0), pipeline_mode=once),
                  pl.BlockSpec((tm, d), lambda i: (i, 0)),
                  pl.BlockSpec((1, 6, d), lambda i: (i // per_b, 0, 0)),
                  pl.BlockSpec((1, d), lambda i: (0, 0)),
                  pl.BlockSpec((1, d), lambda i: (0, 0)),
                  pl.BlockSpec((d, LANES), lambda i: (0, 0), pipeline_mode=once)],
        out_specs=[pl.BlockSpec((tm, d), lambda i: (i, 0)),
                   pl.BlockSpec((tm, d), lambda i: (i, 0)),
                   pl.BlockSpec((tm, LANES), lambda i: (i, 0))],
        out_shape=[jax.ShapeDtypeStruct((t, d), F32),
                   jax.ShapeDtypeStruct((t, d), F32),
                   jax.ShapeDtypeStruct((t, LANES), F32)],
        compiler_params=_cparams(("arbitrary",)),
        name="out_proj",
    )(o_mla, o_moba, wa, wb, xf, mod3, ln_g, ln_b, wr)


def _route_kernel(lg_ref, br_ref, info_ref, cnt_ref, run_scr, *, tm):
    i = pl.program_id(0)

    @pl.when(i == 0)
    def _():
        run_scr[...] = jnp.zeros_like(run_scr)

    lg = lg_ref[...] + br_ref[...]
    lane = lax.broadcasted_iota(jnp.int32, (tm, LANES), 1)
    e_lo, e_hi = MOE_GROUPS, MOE_GROUPS + MOE_N_EXPERTS
    is_g = lane < e_lo
    gl = jnp.where(is_g, lg, NEG)
    gmax = jnp.max(gl, axis=-1, keepdims=True)
    gidx = jnp.min(jnp.where(gl == gmax, lane, LANES), axis=-1, keepdims=True)
    g_p = 1.0 / jnp.sum(jnp.where(is_g, jnp.exp(gl - gmax), 0.0), axis=-1, keepdims=True)
    grp_of_lane = jnp.right_shift(lane - e_lo, MOE_EXPERTS_PER_GROUP.bit_length() - 1)
    in_grp = (lane >= e_lo) & (lane < e_hi) & (grp_of_lane == gidx)
    el = jnp.where(in_grp, lg, NEG)
    m1 = jnp.max(el, axis=-1, keepdims=True)
    l1 = jnp.min(jnp.where(el == m1, lane, LANES), axis=-1, keepdims=True)
    el2 = jnp.where(lane == l1, NEG, el)
    m2 = jnp.max(el2, axis=-1, keepdims=True)
    l2 = jnp.min(jnp.where(el2 == m2, lane, LANES), axis=-1, keepdims=True)
    zsum = jnp.sum(jnp.where(in_grp, jnp.exp(el - m1), 0.0), axis=-1, keepdims=True)
    p1 = 1.0 / zsum
    p2 = jnp.exp(m2 - m1) / zsum
    wa = g_p * (p1 / (p1 + p2))
    wb = g_p * (p2 / (p1 + p2))
    hot_a = lane == l1
    hot_b = lane == l2
    onehot = jnp.where(hot_a | hot_b, 1.0, 0.0)
    r = lax.broadcasted_iota(jnp.int32, (tm, tm), 0)
    c = lax.broadcasted_iota(jnp.int32, (tm, tm), 1)
    lower = jnp.where(c < r, 1.0, 0.0).astype(BF16)
    before = jnp.dot(lower, onehot.astype(BF16), preferred_element_type=F32) + run_scr[...]
    rank_a = jnp.sum(jnp.where(hot_a, before, 0.0), axis=-1, keepdims=True)
    rank_b = jnp.sum(jnp.where(hot_b, before, 0.0), axis=-1, keepdims=True)
    run_scr[...] += jnp.sum(onehot, axis=0, keepdims=True)
    info = jnp.zeros((tm, LANES), F32)
    for k, val in enumerate([(l1 - e_lo).astype(F32), (l2 - e_lo).astype(F32), wa, wb, rank_a, rank_b]):
        info = jnp.where(lane == k, val, info)
    info_ref[...] = info
    cnt_ref[...] = run_scr[...]


def _route(logits, br):
    t = logits.shape[0]
    tm = 512
    return pl.pallas_call(
        functools.partial(_route_kernel, tm=tm),
        grid=(t // tm,),
        in_specs=[pl.BlockSpec((tm, LANES), lambda i: (i, 0)),
                  pl.BlockSpec((1, LANES), lambda i: (0, 0))],
        out_specs=[pl.BlockSpec((tm, LANES), lambda i: (i, 0)),
                   pl.BlockSpec((1, LANES), lambda i: (0, 0))],
        out_shape=[jax.ShapeDtypeStruct((t, LANES), F32),
                   jax.ShapeDtypeStruct((1, LANES), F32)],
        scratch_shapes=[pltpu.VMEM((1, LANES), F32)],
        compiler_params=_cparams(("arbitrary",)),
        name="route",
    )(logits, br)


def _pos_kernel(info_ref, start_ref, pos_ref):
    info = info_ref[...]
    tm = info.shape[0]
    lane = lax.broadcasted_iota(jnp.int32, (tm, LANES), 1)
    start = start_ref[...]
    cols = []
    for k in range(2):
        e = jnp.sum(jnp.where(lane == k, info, 0.0), axis=-1, keepdims=True).astype(jnp.int32)
        rank = jnp.sum(jnp.where(lane == 4 + k, info, 0.0), axis=-1, keepdims=True)
        base = jnp.sum(jnp.where(lane == e + MOE_GROUPS, start, 0.0), axis=-1, keepdims=True)
        cols.append(base + rank)
    pos_ref[...] = jnp.where(lane == 0, cols[0], jnp.where(lane == 1, cols[1], 0.0)).astype(jnp.int32)


def _positions(info, start_lanes):
    t = info.shape[0]
    tm = 1024
    return pl.pallas_call(
        _pos_kernel,
        grid=(t // tm,),
        in_specs=[pl.BlockSpec((tm, LANES), lambda i: (i, 0)),
                  pl.BlockSpec((1, LANES), lambda i: (0, 0))],
        out_specs=pl.BlockSpec((tm, LANES), lambda i: (i, 0)),
        out_shape=jax.ShapeDtypeStruct((t, LANES), jnp.int32),
        compiler_params=_cparams(("arbitrary",)),
        name="positions",
    )(info, start_lanes)


def _row_copy(src_ref, src_row, dst_ref, dst_row, sem):
    return pltpu.make_async_copy(src_ref.at[pl.ds(src_row, 1)], dst_ref.at[pl.ds(dst_row, 1)], sem)


def _dispatch_kernel(pa_ref, pb_ref, h_ref, xs_in_ref, xs_ref, sem, *, tm):
    del xs_in_ref
    base = pl.program_id(0) * tm

    def issue(t, _):
        _row_copy(h_ref, t, xs_ref, pa_ref[base + t], sem).start()
        _row_copy(h_ref, t, xs_ref, pb_ref[base + t], sem).start()
        return 0

    lax.fori_loop(0, tm, issue, 0)

    def drain(t, _):
        _row_copy(h_ref, 0, xs_ref, 0, sem).wait()
        _row_copy(h_ref, 0, xs_ref, 0, sem).wait()
        return 0

    lax.fori_loop(0, tm, drain, 0)


def _dispatch(pos_a, pos_b, h2, rows):
    t, d = h2.shape
    tm = 256
    xs0 = jnp.zeros((rows, d), F32)
    grid_spec = pltpu.PrefetchScalarGridSpec(
        num_scalar_prefetch=2,
        grid=(t // tm,),
        in_specs=[pl.BlockSpec((tm, d), lambda i, pa, pb: (i, 0)),
                  pl.BlockSpec(memory_space=pl.ANY)],
        out_specs=pl.BlockSpec(memory_space=pl.ANY),
        scratch_shapes=[pltpu.SemaphoreType.DMA(())],
    )
    return pl.pallas_call(
        functools.partial(_dispatch_kernel, tm=tm),
        grid_spec=grid_spec,
        out_shape=jax.ShapeDtypeStruct((rows, d), F32),
        input_output_aliases={3: 0},
        compiler_params=_cparams(("arbitrary",)),
        name="dispatch",
    )(pos_a, pos_b, h2, xs0)


def _experts_kernel(texp_ref, tidx_ref, nexte_ref, used_ref, x_ref, w1_hbm, w3_hbm, w2_hbm, o_ref,
                    w1_f32, w3_f32, w2_f32, w1_scr, w3_scr, w2_scr, sems):
    i = pl.program_id(0)
    prev = texp_ref[jnp.maximum(i - 1, 0)]

    def weight_copies(e):
        return [pltpu.make_async_copy(w1_hbm.at[e], w1_f32, sems.at[0]),
                pltpu.make_async_copy(w3_hbm.at[e], w3_f32, sems.at[1]),
                pltpu.make_async_copy(w2_hbm.at[e], w2_f32, sems.at[2])]

    @pl.when(i == 0)
    def _():
        for cp in weight_copies(texp_ref[0]):
            cp.start()

    @pl.when((i == 0) | (texp_ref[i] != prev))
    def _():
        for cp in weight_copies(texp_ref[i]):
            cp.wait()
        w1_scr[...] = w1_f32[...].astype(BF16)
        w3_scr[...] = w3_f32[...].astype(BF16)
        w2_scr[...] = w2_f32[...].astype(BF16)

        @pl.when(nexte_ref[i] >= 0)
        def _():
            for cp in weight_copies(nexte_ref[i]):
                cp.start()

    @pl.when(i < used_ref[0])
    def _():
        x = x_ref[...].astype(BF16)
        a = jnp.dot(x, w1_scr[...], preferred_element_type=F32)
        b = jnp.dot(x, w3_scr[...], preferred_element_type=F32)
        hid = (a / (1.0 + jnp.exp(-a))) * b
        o_ref[...] = jnp.dot(hid.astype(BF16), w2_scr[...], preferred_element_type=F32)

    @pl.when(i >= used_ref[0])
    def _():
        o_ref[...] = jnp.zeros_like(o_ref)


def _experts(texp, tidx, nexte, used, xs, w1, w3, w2, tr):
    rows, d = xs.shape
    nt = rows // tr
    f = w1.shape[2]
    grid_spec = pltpu.PrefetchScalarGridSpec(
        num_scalar_prefetch=4,
        grid=(nt,),
        in_specs=[pl.BlockSpec((tr, d), lambda i, te, ti, ne, u: (ti[i], 0)),
                  pl.BlockSpec(memory_space=pl.ANY),
                  pl.BlockSpec(memory_space=pl.ANY),
                  pl.BlockSpec(memory_space=pl.ANY)],
        out_specs=pl.BlockSpec((tr, d), lambda i, te, ti, ne, u: (i, 0)),
        scratch_shapes=[pltpu.VMEM((d, f), F32), pltpu.VMEM((d, f), F32), pltpu.VMEM((f, d), F32),
                        pltpu.VMEM((d, f), BF16), pltpu.VMEM((d, f), BF16), pltpu.VMEM((f, d), BF16),
                        pltpu.SemaphoreType.DMA((3,))],
    )
    return pl.pallas_call(
        _experts_kernel,
        grid_spec=grid_spec,
        out_shape=jax.ShapeDtypeStruct((rows, d), F32),
        compiler_params=_cparams(("arbitrary",)),
        name="experts",
    )(texp, tidx, nexte, used, xs, w1, w3, w2)


def _combine_kernel(pa_ref, pb_ref, ys_ref, x1_ref, info_ref, mod_ref, g_ref, b_ref, o_ref,
                    buf_a, buf_b, sem, *, tm, alpha):
    base = pl.program_id(0) * tm

    def issue(t, _):
        _row_copy(ys_ref, pa_ref[base + t], buf_a, t, sem).start()
        _row_copy(ys_ref, pb_ref[base + t], buf_b, t, sem).start()
        return 0

    lax.fori_loop(0, tm, issue, 0)

    def drain(t, _):
        _row_copy(ys_ref, 0, buf_a, 0, sem).wait()
        _row_copy(ys_ref, 0, buf_b, 0, sem).wait()
        return 0

    lax.fori_loop(0, tm, drain, 0)
    info = info_ref[...]
    y = info[:, 2:3] * buf_a[...] + info[:, 3:4] * buf_b[...]
    z = alpha * x1_ref[...] + mod_ref[0, 5:6, :] * y
    o_ref[...] = _ln(z) * g_ref[...] + b_ref[...]


def _combine(pos_a, pos_b, ys, x1, info, mod3, ln_g, ln_b, seq, alpha):
    t, d = x1.shape
    tm = 256
    per_b = seq // tm
    grid_spec = pltpu.PrefetchScalarGridSpec(
        num_scalar_prefetch=2,
        grid=(t // tm,),
        in_specs=[pl.BlockSpec(memory_space=pl.ANY),
                  pl.BlockSpec((tm, d), lambda i, pa, pb: (i, 0)),
                  pl.BlockSpec((tm, LANES), lambda i, pa, pb: (i, 0)),
                  pl.BlockSpec((1, 6, d), lambda i, pa, pb: (i // per_b, 0, 0)),
                  pl.BlockSpec((1, d), lambda i, pa, pb: (0, 0)),
                  pl.BlockSpec((1, d), lambda i, pa, pb: (0, 0))],
        out_specs=pl.BlockSpec((tm, d), lambda i, pa, pb: (i, 0)),
        scratch_shapes=[pltpu.VMEM((tm, d), F32), pltpu.VMEM((tm, d), F32), pltpu.SemaphoreType.DMA(())],
    )
    return pl.pallas_call(
        functools.partial(_combine_kernel, tm=tm, alpha=alpha),
        grid_spec=grid_spec,
        out_shape=jax.ShapeDtypeStruct((t, d), F32),
        compiler_params=_cparams(("arbitrary",)),
        name="combine",
    )(pos_a, pos_b, ys, x1, info, mod3, ln_g, ln_b)


def _prep_w_in(w):
    cq = w[:, :MLA_Q_RANK]
    ckv = w[:, MLA_Q_RANK:MLA_Q_RANK + MLA_KV_RANK]
    r0 = MLA_Q_RANK + MLA_KV_RANK
    kr = w[:, r0:r0 + MLA_ROPE_DIM]
    mo = w[:, r0 + MLA_ROPE_DIM:]
    half = MLA_ROPE_DIM // 2
    z = jnp.zeros((w.shape[0], LANES - MLA_ROPE_DIM), w.dtype)
    return jnp.concatenate([cq, ckv, kr, z, kr[:, half:], kr[:, :half], z, mo], axis=1).astype(BF16)


def _prep_w_uq(w):
    r = w.shape[0]
    w = w.reshape(r, MLA_HEADS, MLA_NOPE_DIM + MLA_ROPE_DIM)
    half = MLA_ROPE_DIM // 2
    nope = w[:, :, :MLA_NOPE_DIM]
    x1 = w[:, :, MLA_NOPE_DIM:MLA_NOPE_DIM + half]
    x2 = w[:, :, MLA_NOPE_DIM + half:]
    z = jnp.zeros((r, MLA_HEADS, LANES - MLA_ROPE_DIM), w.dtype)
    return jnp.concatenate([nope, x1, x2, z, x2, x1, z], axis=2).reshape(r, MLA_HEADS * 3 * LANES).astype(BF16)


def _rope_lanes(seq):
    inv = 1.0 / (ROPE_THETA ** (jnp.arange(0, MLA_ROPE_DIM, 2, dtype=F32) / MLA_ROPE_DIM))
    ang = jnp.arange(seq, dtype=F32)[:, None] * inv[None, :]
    cos, sin = jnp.cos(ang), jnp.sin(ang)
    z = jnp.zeros((seq, LANES - MLA_ROPE_DIM), F32)
    return jnp.concatenate([cos, cos, z], axis=1), jnp.concatenate([-sin, sin, z], axis=1)


def _prep_router(w_rg, b_rg, w_re, b_re):
    d = w_rg.shape[0]
    w = jnp.concatenate([w_rg, w_re], axis=1)
    n = w.shape[1]
    hi = w.astype(BF16)
    lo = (w - hi.astype(F32)).astype(BF16)
    z = jnp.zeros((d, LANES // 2 - n), BF16)
    wr = jnp.concatenate([hi, z, lo, z], axis=1)
    br = jnp.zeros((1, LANES), F32).at[0, :n].set(jnp.concatenate([b_rg, b_re]))
    return wr, br


def _layer(xf, mod3, batch, seq, depth_alpha, w_in, q_norm_g, w_uq, kv_norm_g, w_ukv, w_out, bias_tiles,
           cos_t, sin_t, ln1_g, ln1_b, w_rg, b_rg, w_re, b_re, w1, w3, w2, ln2_g, ln2_b):
    t, d = xf.shape
    part_a, qkv_mo = _in_proj(xf, mod3, _prep_w_in(w_in), seq)
    q, k, v = _qkv(part_a, _prep_w_uq(w_uq), w_ukv.astype(BF16), q_norm_g.reshape(1, -1),
                   kv_norm_g.reshape(1, -1), cos_t, sin_t, seq)
    o_mla = _mla_attention(q, k, v, batch, seq)
    q_wide, k_wide, v_wide = _moba_select(qkv_mo, batch, seq)
    o_moba = _moba_attention(q_wide, k_wide, v_wide, bias_tiles, batch, seq)
    wo = w_out.astype(BF16)
    wr, br = _prep_router(w_rg, b_rg, w_re, b_re)
    n_mla = MLA_HEADS * MLA_V_DIM
    x1, h2, logits = _out_proj(o_mla, o_moba, wo[:n_mla], wo[n_mla:], xf, mod3, ln1_g.reshape(1, d),
                               ln1_b.reshape(1, d), wr, seq, depth_alpha)
    info, counts = _route(logits, br)
    tr = 256
    nt = (2 * t) // tr + MOE_N_EXPERTS
    cnt = counts[0, MOE_GROUPS:MOE_GROUPS + MOE_N_EXPERTS].astype(jnp.int32)
    ntile = (cnt + tr - 1) // tr
    tile_end = jnp.cumsum(ntile)
    tile_start = tile_end - ntile
    used = tile_end[-1]
    start_lanes = jnp.zeros((1, LANES), F32).at[0, MOE_GROUPS:MOE_GROUPS + MOE_N_EXPERTS].set(
        (tile_start * tr).astype(F32))
    pos = _positions(info, start_lanes)
    pos_a, pos_b = pos[:, 0], pos[:, 1]
    tidx = jnp.minimum(jnp.arange(nt, dtype=jnp.int32), used - 1)
    texp = jnp.sum(tidx[:, None] >= tile_end[None, :], axis=1).astype(jnp.int32)
    eids = jnp.arange(MOE_N_EXPERTS, dtype=jnp.int32)
    later = (eids[None, :] > eids[:, None]) & (ntile[None, :] > 0)
    next_nonempty = jnp.where(later.any(axis=1), jnp.argmax(later, axis=1), -1).astype(jnp.int32)
    nexte = jnp.sum(jnp.where(texp[:, None] == eids[None, :], next_nonempty[None, :], 0), axis=1).astype(jnp.int32)
    xs = _dispatch(pos_a, pos_b, h2, nt * tr)
    ys = _experts(texp, tidx, nexte, used.reshape(1).astype(jnp.int32), xs, w1, w3, w2, tr)
    return _combine(pos_a, pos_b, ys, x1, info, mod3, ln2_g.reshape(1, d), ln2_b.reshape(1, d), seq, depth_alpha)


def kernel(x, c, w_ada, b_ada, w_in, q_norm_g, w_uq, kv_norm_g, w_ukv, w_out, t5_table, ln1_g, ln1_b,
           w_router_group, b_router_group, w_router_expert, b_router_expert, w1, w3, w2, ln2_g, ln2_b):
    batch, seq, d = x.shape
    depth = w_ada.shape[0]
    alpha = (2.0 * depth) ** 0.25
    cos_t, sin_t = _rope_lanes(seq)
    bias_tiles = _t5_tiles(t5_table)
    xf = x.reshape(batch * seq, d)
    for l in range(depth):
        mod3 = _ada_mod(c, w_ada[l], b_ada[l]).reshape(batch, 6, d)
        xf = _layer(xf, mod3, batch, seq, alpha, w_in[l], q_norm_g[l], w_uq[l], kv_norm_g[l], w_ukv[l],
                    w_out[l], bias_tiles, cos_t, sin_t, ln1_g[l], ln1_b[l], w_router_group[l],
                    b_router_group[l], w_router_expert[l], b_router_expert[l], w1[l], w3[l], w2[l],
                    ln2_g[l], ln2_b[l])
    return xf.reshape(batch, seq, d)
```

```python
import functools
import math

import jax
import jax.numpy as jnp
from jax import lax
from jax.experimental import pallas as pl
from jax.experimental.pallas import tpu as pltpu

D_MODEL = 2048
MLA_HEADS = 8
MLA_Q_RANK = 512
MLA_KV_RANK = 256
MLA_NOPE_DIM = 128
MLA_ROPE_DIM = 64
MLA_V_DIM = 128
ROPE_THETA = 10000.0
MOBA_HEADS = 8
MOBA_HEAD_DIM = 128
MOBA_BLOCK = 256
MOBA_TOPK = 3
T5_BUCKETS = 32
T5_MAX_DISTANCE = 128
MOE_GROUPS = 4
MOE_EXPERTS_PER_GROUP = 8
MOE_N_EXPERTS = MOE_GROUPS * MOE_EXPERTS_PER_GROUP
MOE_D_FF = 512
LN_EPS = 1e-5
RMS_EPS = 1e-6
MOBA_WIDTH = MOBA_HEADS * MOBA_HEAD_DIM

LANES = 128
QK_PAD = 256
V_PAD = 256
PART_A = 1024
NEG = -1e30
VMEM_LIMIT = 56 * 1024 * 1024

F32 = jnp.float32
BF16 = jnp.bfloat16


def _cparams(sem):
    return pltpu.CompilerParams(dimension_semantics=sem, vmem_limit_bytes=VMEM_LIMIT)


def _ln(x):
    mu = jnp.mean(x, axis=-1, keepdims=True)
    xc = x - mu
    var = jnp.mean(xc * xc, axis=-1, keepdims=True)
    return xc * lax.rsqrt(var + LN_EPS)


def _nt_dot(a, b):
    return lax.dot_general(a, b, (((1,), (1,)), ((), ())), preferred_element_type=F32)


def _ada_kernel(ct_ref, w_ref, b_ref, o_ref, *, batch):
    ct = ct_ref[...]
    ca = ct / (1.0 + jnp.exp(-ct))
    w = w_ref[...]
    rows = [jnp.sum(w * ca[:, b:b + 1], axis=0, keepdims=True) for b in range(batch)]
    o_ref[...] = jnp.concatenate(rows, axis=0) + b_ref[...]


def _ada_mod(c, w_ada, b_ada):
    batch, d = c.shape
    n = w_ada.shape[1]
    tn = 512
    ct = jnp.zeros((d, LANES), F32).at[:, :batch].set(c.T)
    return pl.pallas_call(
        functools.partial(_ada_kernel, batch=batch),
        grid=(n // tn,),
        in_specs=[pl.BlockSpec((d, LANES), lambda j: (0, 0)),
                  pl.BlockSpec((d, tn), lambda j: (0, j)),
                  pl.BlockSpec((1, tn), lambda j: (0, j))],
        out_specs=pl.BlockSpec((batch, tn), lambda j: (0, j)),
        out_shape=jax.ShapeDtypeStruct((batch, n), F32),
        compiler_params=_cparams(("arbitrary",)),
        name="ada_mod",
    )(ct, w_ada, b_ada.reshape(1, n))


def _inproj_kernel(x_ref, mod_ref, w_ref, a_ref, b_ref, h_scr):
    j = pl.program_id(1)

    @pl.when(j == 0)
    def _():
        h = _ln(x_ref[...]) * (1.0 + mod_ref[0, 1:2, :]) + mod_ref[0, 0:1, :]
        h_scr[...] = h.astype(BF16)
        a_ref[...] = jnp.dot(h_scr[...], w_ref[...], preferred_element_type=F32)

    @pl.when(j > 0)
    def _():
        b_ref[...] = jnp.dot(h_scr[...], w_ref[...], preferred_element_type=F32).astype(BF16)


def _in_proj(xf, mod3, w4, seq):
    t, d = xf.shape
    n = w4.shape[1]
    tm, tn = 512, PART_A
    per_b = seq // tm
    return pl.pallas_call(
        _inproj_kernel,
        grid=(t // tm, n // tn),
        in_specs=[pl.BlockSpec((tm, d), lambda i, j: (i, 0)),
                  pl.BlockSpec((1, 6, d), lambda i, j: (i // per_b, 0, 0)),
                  pl.BlockSpec((d, tn), lambda i, j: (0, j))],
        out_specs=[pl.BlockSpec((tm, tn), lambda i, j: (i, 0)),
                   pl.BlockSpec((tm, tn), lambda i, j: (i, jnp.maximum(j - 1, 0)))],
        out_shape=[jax.ShapeDtypeStruct((t, PART_A), F32),
                   jax.ShapeDtypeStruct((t, n - PART_A), BF16)],
        scratch_shapes=[pltpu.VMEM((tm, d), BF16)],
        compiler_params=_cparams(("arbitrary", "arbitrary")),
        name="in_proj",
    )(xf, mod3, w4)


def _qkv_kernel(a_ref, wq_ref, wkv_ref, gq_ref, gkv_ref, cos_ref, sin_ref, q_ref, k_ref, v_ref):
    a = a_ref[...]
    cq = a[:, :MLA_Q_RANK]
    ckv = a[:, MLA_Q_RANK:MLA_Q_RANK + MLA_KV_RANK]
    kr = a[:, 768:896]
    krs = a[:, 896:1024]
    cqn = (cq * lax.rsqrt(jnp.mean(cq * cq, axis=-1, keepdims=True) + RMS_EPS) * gq_ref[...]).astype(BF16)
    ckvn = (ckv * lax.rsqrt(jnp.mean(ckv * ckv, axis=-1, keepdims=True) + RMS_EPS) * gkv_ref[...]).astype(BF16)
    q3 = jnp.dot(cqn, wq_ref[...], preferred_element_type=F32)
    kv = jnp.dot(ckvn, wkv_ref[...], preferred_element_type=F32)
    cos = cos_ref[...]
    sin = sin_ref[...]
    krr = (kr * cos + krs * sin).astype(BF16)
    for h in range(MLA_HEADS):
        q0 = h * 3 * LANES
        c0 = h * QK_PAD
        q_ref[:, c0:c0 + LANES] = q3[:, q0:q0 + LANES].astype(BF16)
        q_ref[:, c0 + LANES:c0 + QK_PAD] = (
            q3[:, q0 + LANES:q0 + 2 * LANES] * cos + q3[:, q0 + 2 * LANES:q0 + 3 * LANES] * sin).astype(BF16)
        k_ref[:, c0:c0 + LANES] = kv[:, c0:c0 + LANES].astype(BF16)
        k_ref[:, c0 + LANES:c0 + QK_PAD] = krr
        v_ref[:, c0:c0 + LANES] = kv[:, c0 + LANES:c0 + QK_PAD].astype(BF16)
        v_ref[:, c0 + LANES:c0 + V_PAD] = jnp.ones((a.shape[0], V_PAD - LANES), BF16)


def _qkv(part_a, wq3, wkv, gq, gkv, cos_t, sin_t, seq):
    t = part_a.shape[0]
    tm = 512
    per_b = seq // tm
    hq = MLA_HEADS * QK_PAD
    return pl.pallas_call(
        _qkv_kernel,
        grid=(t // tm,),
        in_specs=[pl.BlockSpec((tm, PART_A), lambda i: (i, 0)),
                  pl.BlockSpec(wq3.shape, lambda i: (0, 0)),
                  pl.BlockSpec(wkv.shape, lambda i: (0, 0)),
                  pl.BlockSpec((1, MLA_Q_RANK), lambda i: (0, 0)),
                  pl.BlockSpec((1, MLA_KV_RANK), lambda i: (0, 0)),
                  pl.BlockSpec((tm, LANES), lambda i: (i % per_b, 0)),
                  pl.BlockSpec((tm, LANES), lambda i: (i % per_b, 0))],
        out_specs=[pl.BlockSpec((tm, hq), lambda i: (i, 0)),
                   pl.BlockSpec((tm, hq), lambda i: (i, 0)),
                   pl.BlockSpec((tm, MLA_HEADS * V_PAD), lambda i: (i, 0))],
        out_shape=[jax.ShapeDtypeStruct((t, hq), BF16),
                   jax.ShapeDtypeStruct((t, hq), BF16),
                   jax.ShapeDtypeStruct((t, MLA_HEADS * V_PAD), BF16)],
        compiler_params=_cparams(("arbitrary",)),
        name="qkv",
    )(part_a, wq3, wkv, gq, gkv, cos_t, sin_t)


ATTN_TILE = 512


def _attn_kernel(*refs, c, masked):
    if masked:
        q_ref, k_ref, v_ref, bias_ref, o_ref, s_a, s_b, m_scr, acc_scr = refs
    else:
        q_ref, k_ref, v_ref, o_ref, s_a, s_b, m_scr, acc_scr = refs
    tile = ATTN_TILE
    qi = pl.program_id(2)

    def put_scores(buf, t):
        k = k_ref[pl.ds(pl.multiple_of(t * tile, tile), tile), :]
        buf[...] = _nt_dot(q_ref[...], k)

    def tile_step(buf, t, kind, nxt=None):
        if nxt is not None:
            put_scores(*nxt)
        s = buf[...]
        if masked:
            if kind == "prev":
                s = s + bias_ref[0, 1] + jnp.where(qi == 0, NEG, 0.0)
            elif kind == "diag":
                s = s + bias_ref[0, 0]
        elif kind == "diag":
            row = lax.broadcasted_iota(jnp.int32, (tile, tile), 0)
            col = lax.broadcasted_iota(jnp.int32, (tile, tile), 1)
            s = jnp.where(row >= col, s, NEG)
        m_old = m_scr[...]
        m_new = jnp.maximum(m_old, jnp.broadcast_to(jnp.max(s, axis=-1, keepdims=True), m_old.shape))
        alpha = jnp.exp2((m_old - m_new) * c)
        p = jnp.concatenate([jnp.exp2(((s[:, j * LANES:(j + 1) * LANES] - m_new) * c).astype(BF16))
                             for j in range(tile // LANES)], axis=1)
        v = v_ref[pl.ds(pl.multiple_of(t * tile, tile), tile), :]
        pv = jnp.dot(p, v, preferred_element_type=F32)
        acc_scr[...] = jnp.concatenate([alpha, alpha], axis=1) * acc_scr[...] + pv
        m_scr[...] = m_new

    m_scr[...] = jnp.full(m_scr.shape, NEG, F32)
    acc_scr[...] = jnp.zeros(acc_scr.shape, F32)
    put_scores(s_a, 0)

    n_far = jnp.maximum(qi - 1, 0) if masked else qi
    pn = jnp.maximum(qi - 1, 0)

    def pair(j, _):
        t = 2 * j
        tile_step(s_a, t, "far", (s_b, t + 1))
        tile_step(s_b, t + 1, "far", (s_a, t + 2))
        return 0

    lax.fori_loop(0, n_far // 2, pair, 0)
    t0 = 2 * (n_far // 2)

    def tail(first, second):
        if masked:
            tile_step(first, pn, "prev", (second, qi))
            tile_step(second, qi, "diag")
        else:
            tile_step(first, qi, "diag")

    @pl.when(n_far % 2 == 1)
    def _():
        tile_step(s_a, t0, "far", (s_b, t0 + 1))
        tail(s_b, s_a)

    @pl.when(n_far % 2 == 0)
    def _():
        tail(s_a, s_b)

    o_ref[...] = (acc_scr[:, :LANES] / acc_scr[:, LANES:]).astype(o_ref.dtype)


def _attn_scratch():
    tile = ATTN_TILE
    return [pltpu.VMEM((tile, tile), F32), pltpu.VMEM((tile, tile), F32),
            pltpu.VMEM((tile, LANES), F32), pltpu.VMEM((tile, V_PAD), F32)]


def _mla_attention(q, k, v, batch, seq):
    tile = ATTN_TILE
    nq = seq // tile
    c = math.log2(math.e) / math.sqrt(MLA_NOPE_DIM + MLA_ROPE_DIM)
    return pl.pallas_call(
        functools.partial(_attn_kernel, c=c, masked=False),
        grid=(batch, MLA_HEADS, nq),
        in_specs=[pl.BlockSpec((tile, QK_PAD), lambda b, h, i: (b * nq + i, h)),
                  pl.BlockSpec((seq, QK_PAD), lambda b, h, i: (b, h)),
                  pl.BlockSpec((seq, V_PAD), lambda b, h, i: (b, h))],
        out_specs=pl.BlockSpec((tile, MLA_V_DIM), lambda b, h, i: (b * nq + i, h)),
        out_shape=jax.ShapeDtypeStruct((batch * seq, MLA_HEADS * MLA_V_DIM), BF16),
        scratch_shapes=_attn_scratch(),
        compiler_params=_cparams(("arbitrary", "arbitrary", "arbitrary")),
        name="mla_attn",
    )(q, k, v)


def _moba_select_kernel(q_ref, k_ref, v_ref, qa_ref, ka_ref, va_ref, *, seq, nb):
    kf = k_ref[...].astype(F32)
    km = jnp.sum(kf.reshape(nb, MOBA_BLOCK, MOBA_HEAD_DIM), axis=1) * (1.0 / MOBA_BLOCK)
    km = jnp.concatenate([km, jnp.zeros((LANES - nb, MOBA_HEAD_DIM), F32)], axis=0)
    km_hi = km.astype(BF16)
    km_lo = (km - km_hi.astype(F32)).astype(BF16)
    q = q_ref[...]
    gate = _nt_dot(q, km_hi) + _nt_dot(q, km_lo)
    blk = lax.broadcasted_iota(jnp.int32, (seq, LANES), 1)
    qblk = jnp.right_shift(lax.broadcasted_iota(jnp.int32, (seq, LANES), 0), MOBA_BLOCK.bit_length() - 1)
    g = jnp.where(blk < qblk, gate, NEG)
    visible = blk == qblk
    for _ in range(MOBA_TOPK):
        mx = jnp.max(g, axis=-1, keepdims=True)
        first = jnp.min(jnp.where(g == mx, blk, LANES), axis=-1, keepdims=True)
        pick = (blk == first) & (mx > 0.5 * NEG)
        visible = visible | pick
        g = jnp.where(pick, NEG, g)
    in_use = blk < nb
    qa_ref[:, :MOBA_HEAD_DIM] = q
    qa_ref[:, MOBA_HEAD_DIM:] = jnp.where(visible | ~in_use, 0.0, NEG).astype(BF16)
    ka_ref[:, :MOBA_HEAD_DIM] = k_ref[...]
    ka_ref[:, MOBA_HEAD_DIM:] = jnp.where(blk == qblk, 1.0, 0.0).astype(BF16)
    va_ref[:, :MOBA_HEAD_DIM] = v_ref[...]
    va_ref[:, MOBA_HEAD_DIM:] = jnp.ones((seq, V_PAD - MOBA_HEAD_DIM), BF16)


def _moba_select(qkv_mo, batch, seq):
    nb = seq // MOBA_BLOCK
    assert MOBA_HEAD_DIM == LANES and nb <= QK_PAD - MOBA_HEAD_DIM
    wide = jax.ShapeDtypeStruct((batch * seq, MOBA_HEADS * QK_PAD), BF16)
    return pl.pallas_call(
        functools.partial(_moba_select_kernel, seq=seq, nb=nb),
        grid=(batch, MOBA_HEADS),
        in_specs=[pl.BlockSpec((seq, MOBA_HEAD_DIM), lambda b, h: (b, h)),
                  pl.BlockSpec((seq, MOBA_HEAD_DIM), lambda b, h: (b, MOBA_HEADS + h)),
                  pl.BlockSpec((seq, MOBA_HEAD_DIM), lambda b, h: (b, 2 * MOBA_HEADS + h))],
        out_specs=[pl.BlockSpec((seq, QK_PAD), lambda b, h: (b, h)),
                   pl.BlockSpec((seq, QK_PAD), lambda b, h: (b, h)),
                   pl.BlockSpec((seq, V_PAD), lambda b, h: (b, h))],
        out_shape=[wide, wide, jax.ShapeDtypeStruct((batch * seq, MOBA_HEADS * V_PAD), BF16)],
        compiler_params=_cparams(("arbitrary", "arbitrary")),
        name="moba_select",
    )(qkv_mo, qkv_mo, qkv_mo)


def _t5_kernel(tab_ref, o_ref, *, inv_scale):
    h = pl.program_id(0)
    r = lax.broadcasted_iota(jnp.int32, (LANES, LANES), 0)
    c = lax.broadcasted_iota(jnp.int32, (LANES, LANES), 1)
    max_exact = T5_BUCKETS // 2
    far = tab_ref[T5_BUCKETS - 1, h]

    def block(offset):
        rel = offset + r - c
        n = jnp.maximum(rel, 0)
        nf = jnp.maximum(n, 1).astype(F32)
        large = max_exact + (jnp.log(nf / max_exact) / math.log(T5_MAX_DISTANCE / max_exact)
                             * (T5_BUCKETS - max_exact)).astype(jnp.int32)
        large = jnp.minimum(large, T5_BUCKETS - 1)
        bucket = jnp.where(n < max_exact, n, large)
        bias = jnp.zeros((LANES, LANES), F32)
        for j in range(T5_BUCKETS):
            bias = jnp.where(bucket == j, tab_ref[j, h], bias)
        return jnp.where(rel >= 0, (bias - far) * inv_scale, NEG)

    near = {0: block(0), 1: block(LANES)}
    nblk = ATTN_TILE // LANES
    for d in range(2):
        for i in range(nblk):
            for j in range(nblk):
                k = d * nblk + i - j
                if k < 0:
                    val = jnp.full((LANES, LANES), NEG, F32)
                else:
                    val = near.get(k, jnp.zeros((LANES, LANES), F32))
                o_ref[0, d, i * LANES:(i + 1) * LANES, j * LANES:(j + 1) * LANES] = val


def _t5_tiles(t5_table):
    assert LANES >= T5_MAX_DISTANCE
    tile = ATTN_TILE
    return pl.pallas_call(
        functools.partial(_t5_kernel, inv_scale=math.sqrt(MOBA_HEAD_DIM)),
        grid=(MOBA_HEADS,),
        in_specs=[pl.BlockSpec(memory_space=pltpu.SMEM)],
        out_specs=pl.BlockSpec((1, 2, tile, tile), lambda h: (h, 0, 0, 0)),
        out_shape=jax.ShapeDtypeStruct((MOBA_HEADS, 2, tile, tile), F32),
        compiler_params=_cparams(("arbitrary",)),
        name="t5_tiles",
    )(t5_table)


def _moba_attention(q_wide, k_wide, v_wide, bias, batch, seq):
    tile = ATTN_TILE
    nq = seq // tile
    c = math.log2(math.e) / math.sqrt(MOBA_HEAD_DIM)
    dh = MOBA_HEAD_DIM
    return pl.pallas_call(
        functools.partial(_attn_kernel, c=c, masked=True),
        grid=(batch, MOBA_HEADS, nq),
        in_specs=[pl.BlockSpec((tile, QK_PAD), lambda b, h, i: (b * nq + i, h)),
                  pl.BlockSpec((seq, QK_PAD), lambda b, h, i: (b, h)),
                  pl.BlockSpec((seq, V_PAD), lambda b, h, i: (b, h)),
                  pl.BlockSpec((1, 2, tile, tile), lambda b, h, i: (h, 0, 0, 0))],
        out_specs=pl.BlockSpec((tile, dh), lambda b, h, i: (b * nq + i, h)),
        out_shape=jax.ShapeDtypeStruct((batch * seq, MOBA_WIDTH), BF16),
        scratch_shapes=_attn_scratch(),
        compiler_params=_cparams(("arbitrary", "arbitrary", "arbitrary")),
        name="moba_attn",
    )(q_wide, k_wide, v_wide, bias)


def _outproj_kernel(oa_ref, ob_ref, wa_ref, wb_ref, x_ref, mod_ref, g_ref, b_ref, wr_ref,
                    x1_ref, h2_ref, lg_ref, y_a, y_b, *, alpha, n_tiles):
    i = pl.program_id(0)
    tm, d = y_a.shape
    chunks = 4
    cn, cr = d // chunks, tm // chunks

    def matmul_into(y_ref, c):
        cols = slice(c * cn, (c + 1) * cn)
        y_ref[:, cols] = (jnp.dot(oa_ref[...], wa_ref[:, cols], preferred_element_type=F32)
                          + jnp.dot(ob_ref[...], wb_ref[:, cols], preferred_element_type=F32))

    def epilogue(y_ref, c):
        rows = slice(c * cr, (c + 1) * cr)
        z = alpha * x_ref[rows, :] + mod_ref[0, 2:3, :] * y_ref[rows, :]
        x1 = _ln(z) * g_ref[...] + b_ref[...]
        x1_ref[rows, :] = x1
        h2 = _ln(x1) * (1.0 + mod_ref[0, 4:5, :]) + mod_ref[0, 3:4, :]
        h2_ref[rows, :] = h2
        h_hi = h2.astype(BF16)
        h_lo = (h2 - h_hi.astype(F32)).astype(BF16)
        zz = (jnp.dot(h_hi, wr_ref[...], preferred_element_type=F32)
              + jnp.dot(h_lo, wr_ref[...], preferred_element_type=F32))
        lg_ref[rows, :] = zz + pltpu.roll(zz, LANES // 2, 1)

    def step(y_new, y_old):
        for c in range(chunks):
            if y_new is not None:
                matmul_into(y_new, c)
            if y_old is not None:
                epilogue(y_old, c)

    inner = (i > 0) & (i < n_tiles)
    pl.when(i == 0)(lambda: step(y_a, None))
    pl.when(inner & (i % 2 == 0))(lambda: step(y_a, y_b))
    pl.when(inner & (i % 2 == 1))(lambda: step(y_b, y_a))
    pl.when(i == n_tiles)(lambda: step(None, y_b if n_tiles % 2 == 0 else y_a))


def _out_proj(o_mla, o_moba, wa, wb, xf, mod3, ln_g, ln_b, wr, seq, alpha):
    t, d = xf.shape
    tm = 512
    n = t // tm
    per_b = seq // tm
    ka, kb = wa.shape[0], wb.shape[0]
    once = pl.Buffered(1)

    def cur(i):
        return (jnp.minimum(i, n - 1), 0)

    def lag(i):
        return (jnp.maximum(i - 1, 0), 0)

    return pl.pallas_call(
        functools.partial(_outproj_kernel, alpha=alpha, n_tiles=n),
        grid=(n + 1,),
        in_specs=[pl.BlockSpec((tm, ka), cur),
                  pl.BlockSpec((tm, kb), cur),
                  pl.BlockSpec((ka, d), lambda i: (0, 0), pipeline_mode=once),
                  pl.BlockSpec((kb, d), lambda i: (0, 0), pipeline_mode=once),
                  pl.BlockSpec((tm, d), lag),
                  pl.BlockSpec((1, 6, d), lambda i: (jnp.maximum(i - 1, 0) // per_b, 0, 0)),
                  pl.BlockSpec((1, d), lambda i: (0, 0)),
                  pl.BlockSpec((1, d), lambda i: (0, 0)),
                  pl.BlockSpec((d, LANES), lambda i: (0, 0), pipeline_mode=once)],
        out_specs=[pl.BlockSpec((tm, d), lag),
                   pl.BlockSpec((tm, d), lag),
                   pl.BlockSpec((tm, LANES), lag)],
        out_shape=[jax.ShapeDtypeStruct((t, d), F32),
                   jax.ShapeDtypeStruct((t, d), F32),
                   jax.ShapeDtypeStruct((t, LANES), F32)],
        scratch_shapes=[pltpu.VMEM((tm, d), F32), pltpu.VMEM((tm, d), F32)],
        compiler_params=_cparams(("arbitrary",)),
        name="out_proj",
    )(o_mla, o_moba, wa, wb, xf, mod3, ln_g, ln_b, wr)


def _route_kernel(lg_ref, br_ref, info_ref, cnt_ref, run_scr, *, tm):
    i = pl.program_id(0)

    @pl.when(i == 0)
    def _():
        run_scr[...] = jnp.zeros_like(run_scr)

    lg = lg_ref[...] + br_ref[...]
    lane = lax.broadcasted_iota(jnp.int32, (tm, LANES), 1)
    e_lo, e_hi = MOE_GROUPS, MOE_GROUPS + MOE_N_EXPERTS
    is_g = lane < e_lo
    gl = jnp.where(is_g, lg, NEG)
    gmax = jnp.max(gl, axis=-1, keepdims=True)
    gidx = jnp.min(jnp.where(gl == gmax, lane, LANES), axis=-1, keepdims=True)
    g_p = 1.0 / jnp.sum(jnp.where(is_g, jnp.exp(gl - gmax), 0.0), axis=-1, keepdims=True)
    grp_of_lane = jnp.right_shift(lane - e_lo, MOE_EXPERTS_PER_GROUP.bit_length() - 1)
    in_grp = (lane >= e_lo) & (lane < e_hi) & (grp_of_lane == gidx)
    el = jnp.where(in_grp, lg, NEG)
    m1 = jnp.max(el, axis=-1, keepdims=True)
    l1 = jnp.min(jnp.where(el == m1, lane, LANES), axis=-1, keepdims=True)
    el2 = jnp.where(lane == l1, NEG, el)
    m2 = jnp.max(el2, axis=-1, keepdims=True)
    l2 = jnp.min(jnp.where(el2 == m2, lane, LANES), axis=-1, keepdims=True)
    zsum = jnp.sum(jnp.where(in_grp, jnp.exp(el - m1), 0.0), axis=-1, keepdims=True)
    p1 = 1.0 / zsum
    p2 = jnp.exp(m2 - m1) / zsum
    wa = g_p * (p1 / (p1 + p2))
    wb = g_p * (p2 / (p1 + p2))
    hot_a = lane == l1
    hot_b = lane == l2
    onehot = jnp.where(hot_a | hot_b, 1.0, 0.0)
    r = lax.broadcasted_iota(jnp.int32, (tm, tm), 0)
    c = lax.broadcasted_iota(jnp.int32, (tm, tm), 1)
    lower = jnp.where(c < r, 1.0, 0.0).astype(BF16)
    before = jnp.dot(lower, onehot.astype(BF16), preferred_element_type=F32) + run_scr[...]
    rank_a = jnp.sum(jnp.where(hot_a, before, 0.0), axis=-1, keepdims=True)
    rank_b = jnp.sum(jnp.where(hot_b, before, 0.0), axis=-1, keepdims=True)
    run_scr[...] += jnp.sum(onehot, axis=0, keepdims=True)
    info = jnp.zeros((tm, LANES), F32)
    for k, val in enumerate([(l1 - e_lo).astype(F32), (l2 - e_lo).astype(F32), wa, wb, rank_a, rank_b]):
        info = jnp.where(lane == k, val, info)
    info_ref[...] = info
    cnt_ref[...] = run_scr[...]


def _route(logits, br):
    t = logits.shape[0]
    tm = 512
    return pl.pallas_call(
        functools.partial(_route_kernel, tm=tm),
        grid=(t // tm,),
        in_specs=[pl.BlockSpec((tm, LANES), lambda i: (i, 0)),
                  pl.BlockSpec((1, LANES), lambda i: (0, 0))],
        out_specs=[pl.BlockSpec((tm, LANES), lambda i: (i, 0)),
                   pl.BlockSpec((1, LANES), lambda i: (0, 0))],
        out_shape=[jax.ShapeDtypeStruct((t, LANES), F32),
                   jax.ShapeDtypeStruct((1, LANES), F32)],
        scratch_shapes=[pltpu.VMEM((1, LANES), F32)],
        compiler_params=_cparams(("arbitrary",)),
        name="route",
    )(logits, br)


def _pos_kernel(info_ref, start_ref, pos_ref):
    info = info_ref[...]
    tm = info.shape[0]
    lane = lax.broadcasted_iota(jnp.int32, (tm, LANES), 1)
    start = start_ref[...]
    cols = []
    for k in range(2):
        e = jnp.sum(jnp.where(lane == k, info, 0.0), axis=-1, keepdims=True).astype(jnp.int32)
        rank = jnp.sum(jnp.where(lane == 4 + k, info, 0.0), axis=-1, keepdims=True)
        base = jnp.sum(jnp.where(lane == e + MOE_GROUPS, start, 0.0), axis=-1, keepdims=True)
        cols.append(base + rank)
    pos_ref[...] = jnp.where(lane == 0, cols[0], jnp.where(lane == 1, cols[1], 0.0)).astype(jnp.int32)


def _positions(info, start_lanes):
    t = info.shape[0]
    tm = 1024
    return pl.pallas_call(
        _pos_kernel,
        grid=(t // tm,),
        in_specs=[pl.BlockSpec((tm, LANES), lambda i: (i, 0)),
                  pl.BlockSpec((1, LANES), lambda i: (0, 0))],
        out_specs=pl.BlockSpec((tm, LANES), lambda i: (i, 0)),
        out_shape=jax.ShapeDtypeStruct((t, LANES), jnp.int32),
        compiler_params=_cparams(("arbitrary",)),
        name="positions",
    )(info, start_lanes)


def _row_copy(src_ref, src_row, dst_ref, dst_row, sem):
    return pltpu.make_async_copy(src_ref.at[pl.ds(src_row, 1)], dst_ref.at[pl.ds(dst_row, 1)], sem)


def _dispatch_kernel(pa_ref, pb_ref, h_ref, xs_in_ref, xs_ref, sem, *, tm):
    del xs_in_ref
    base = pl.program_id(0) * tm

    def issue(t, _):
        _row_copy(h_ref, t, xs_ref, pa_ref[base + t], sem).start()
        _row_copy(h_ref, t, xs_ref, pb_ref[base + t], sem).start()
        return 0

    lax.fori_loop(0, tm, issue, 0)

    def drain(t, _):
        _row_copy(h_ref, 0, xs_ref, 0, sem).wait()
        _row_copy(h_ref, 0, xs_ref, 0, sem).wait()
        return 0

    lax.fori_loop(0, tm, drain, 0)


def _dispatch(pos_a, pos_b, h2, rows):
    t, d = h2.shape
    tm = 256
    xs0 = jnp.zeros((rows, d), F32)
    grid_spec = pltpu.PrefetchScalarGridSpec(
        num_scalar_prefetch=2,
        grid=(t // tm,),
        in_specs=[pl.BlockSpec((tm, d), lambda i, pa, pb: (i, 0)),
                  pl.BlockSpec(memory_space=pl.ANY)],
        out_specs=pl.BlockSpec(memory_space=pl.ANY),
        scratch_shapes=[pltpu.SemaphoreType.DMA(())],
    )
    return pl.pallas_call(
        functools.partial(_dispatch_kernel, tm=tm),
        grid_spec=grid_spec,
        out_shape=jax.ShapeDtypeStruct((rows, d), F32),
        input_output_aliases={3: 0},
        compiler_params=_cparams(("arbitrary",)),
        name="dispatch",
    )(pos_a, pos_b, h2, xs0)


def _experts_kernel(texp_ref, tidx_ref, nexte_ref, used_ref, x_ref, w1_hbm, w3_hbm, w2_hbm, o_ref,
                    w1_f32, w3_f32, w2_f32, w1_scr, w3_scr, w2_scr, sems):
    i = pl.program_id(0)
    prev = texp_ref[jnp.maximum(i - 1, 0)]

    def weight_copies(e):
        return [pltpu.make_async_copy(w1_hbm.at[e], w1_f32, sems.at[0]),
                pltpu.make_async_copy(w3_hbm.at[e], w3_f32, sems.at[1]),
                pltpu.make_async_copy(w2_hbm.at[e], w2_f32, sems.at[2])]

    @pl.when(i == 0)
    def _():
        for cp in weight_copies(texp_ref[0]):
            cp.start()

    @pl.when((i == 0) | (texp_ref[i] != prev))
    def _():
        for cp in weight_copies(texp_ref[i]):
            cp.wait()
        w1_scr[...] = w1_f32[...].astype(BF16)
        w3_scr[...] = w3_f32[...].astype(BF16)
        w2_scr[...] = w2_f32[...].astype(BF16)

        @pl.when(nexte_ref[i] >= 0)
        def _():
            for cp in weight_copies(nexte_ref[i]):
                cp.start()

    @pl.when(i < used_ref[0])
    def _():
        x = x_ref[...].astype(BF16)
        a = jnp.dot(x, w1_scr[...], preferred_element_type=F32)
        b = jnp.dot(x, w3_scr[...], preferred_element_type=F32)
        hid = (a / (1.0 + jnp.exp(-a))) * b
        o_ref[...] = jnp.dot(hid.astype(BF16), w2_scr[...], preferred_element_type=F32)

    @pl.when(i >= used_ref[0])
    def _():
        o_ref[...] = jnp.zeros_like(o_ref)


def _experts(texp, tidx, nexte, used, xs, w1, w3, w2, tr):
    rows, d = xs.shape
    nt = rows // tr
    f = w1.shape[2]
    grid_spec = pltpu.PrefetchScalarGridSpec(
        num_scalar_prefetch=4,
        grid=(nt,),
        in_specs=[pl.BlockSpec((tr, d), lambda i, te, ti, ne, u: (ti[i], 0)),
                  pl.BlockSpec(memory_space=pl.ANY),
                  pl.BlockSpec(memory_space=pl.ANY),
                  pl.BlockSpec(memory_space=pl.ANY)],
        out_specs=pl.BlockSpec((tr, d), lambda i, te, ti, ne, u: (i, 0)),
        scratch_shapes=[pltpu.VMEM((d, f), F32), pltpu.VMEM((d, f), F32), pltpu.VMEM((f, d), F32),
                        pltpu.VMEM((d, f), BF16), pltpu.VMEM((d, f), BF16), pltpu.VMEM((f, d), BF16),
                        pltpu.SemaphoreType.DMA((3,))],
    )
    return pl.pallas_call(
        _experts_kernel,
        grid_spec=grid_spec,
        out_shape=jax.ShapeDtypeStruct((rows, d), F32),
        compiler_params=_cparams(("arbitrary",)),
        name="experts",
    )(texp, tidx, nexte, used, xs, w1, w3, w2)


def _combine_kernel(pa_ref, pb_ref, ys_ref, x1_ref, info_ref, mod_ref, g_ref, b_ref, o_ref,
                    buf_a, buf_b, sem, *, tm, alpha):
    base = pl.program_id(0) * tm

    def issue(t, _):
        _row_copy(ys_ref, pa_ref[base + t], buf_a, t, sem).start()
        _row_copy(ys_ref, pb_ref[base + t], buf_b, t, sem).start()
        return 0

    lax.fori_loop(0, tm, issue, 0)

    def drain(t, _):
        _row_copy(ys_ref, 0, buf_a, 0, sem).wait()
        _row_copy(ys_ref, 0, buf_b, 0, sem).wait()
        return 0

    lax.fori_loop(0, tm, drain, 0)
    info = info_ref[...]
    y = info[:, 2:3] * buf_a[...] + info[:, 3:4] * buf_b[...]
    z = alpha * x1_ref[...] + mod_ref[0, 5:6, :] * y
    o_ref[...] = _ln(z) * g_ref[...] + b_ref[...]


def _combine(pos_a, pos_b, ys, x1, info, mod3, ln_g, ln_b, seq, alpha):
    t, d = x1.shape
    tm = 256
    per_b = seq // tm
    grid_spec = pltpu.PrefetchScalarGridSpec(
        num_scalar_prefetch=2,
        grid=(t // tm,),
        in_specs=[pl.BlockSpec(memory_space=pl.ANY),
                  pl.BlockSpec((tm, d), lambda i, pa, pb: (i, 0)),
                  pl.BlockSpec((tm, LANES), lambda i, pa, pb: (i, 0)),
                  pl.BlockSpec((1, 6, d), lambda i, pa, pb: (i // per_b, 0, 0)),
                  pl.BlockSpec((1, d), lambda i, pa, pb: (0, 0)),
                  pl.BlockSpec((1, d), lambda i, pa, pb: (0, 0))],
        out_specs=pl.BlockSpec((tm, d), lambda i, pa, pb: (i, 0)),
        scratch_shapes=[pltpu.VMEM((tm, d), F32), pltpu.VMEM((tm, d), F32), pltpu.SemaphoreType.DMA(())],
    )
    return pl.pallas_call(
        functools.partial(_combine_kernel, tm=tm, alpha=alpha),
        grid_spec=grid_spec,
        out_shape=jax.ShapeDtypeStruct((t, d), F32),
        compiler_params=_cparams(("arbitrary",)),
        name="combine",
    )(pos_a, pos_b, ys, x1, info, mod3, ln_g, ln_b)


def _prep_w_in(w):
    cq = w[:, :MLA_Q_RANK]
    ckv = w[:, MLA_Q_RANK:MLA_Q_RANK + MLA_KV_RANK]
    r0 = MLA_Q_RANK + MLA_KV_RANK
    kr = w[:, r0:r0 + MLA_ROPE_DIM]
    mo = w[:, r0 + MLA_ROPE_DIM:]
    half = MLA_ROPE_DIM // 2
    z = jnp.zeros((w.shape[0], LANES - MLA_ROPE_DIM), w.dtype)
    return jnp.concatenate([cq, ckv, kr, z, kr[:, half:], kr[:, :half], z, mo], axis=1).astype(BF16)


def _prep_w_uq(w):
    r = w.shape[0]
    w = w.reshape(r, MLA_HEADS, MLA_NOPE_DIM + MLA_ROPE_DIM)
    half = MLA_ROPE_DIM // 2
    nope = w[:, :, :MLA_NOPE_DIM]
    x1 = w[:, :, MLA_NOPE_DIM:MLA_NOPE_DIM + half]
    x2 = w[:, :, MLA_NOPE_DIM + half:]
    z = jnp.zeros((r, MLA_HEADS, LANES - MLA_ROPE_DIM), w.dtype)
    return jnp.concatenate([nope, x1, x2, z, x2, x1, z], axis=2).reshape(r, MLA_HEADS * 3 * LANES).astype(BF16)


def _rope_lanes(seq):
    inv = 1.0 / (ROPE_THETA ** (jnp.arange(0, MLA_ROPE_DIM, 2, dtype=F32) / MLA_ROPE_DIM))
    ang = jnp.arange(seq, dtype=F32)[:, None] * inv[None, :]
    cos, sin = jnp.cos(ang), jnp.sin(ang)
    z = jnp.zeros((seq, LANES - MLA_ROPE_DIM), F32)
    return jnp.concatenate([cos, cos, z], axis=1), jnp.concatenate([-sin, sin, z], axis=1)


def _prep_router(w_rg, b_rg, w_re, b_re):
    d = w_rg.shape[0]
    w = jnp.concatenate([w_rg, w_re], axis=1)
    n = w.shape[1]
    hi = w.astype(BF16)
    lo = (w - hi.astype(F32)).astype(BF16)
    z = jnp.zeros((d, LANES // 2 - n), BF16)
    wr = jnp.concatenate([hi, z, lo, z], axis=1)
    br = jnp.zeros((1, LANES), F32).at[0, :n].set(jnp.concatenate([b_rg, b_re]))
    return wr, br


def _layer(xf, mod3, batch, seq, depth_alpha, w_in, q_norm_g, w_uq, kv_norm_g, w_ukv, w_out, bias_tiles,
           cos_t, sin_t, ln1_g, ln1_b, w_rg, b_rg, w_re, b_re, w1, w3, w2, ln2_g, ln2_b):
    t, d = xf.shape
    part_a, qkv_mo = _in_proj(xf, mod3, _prep_w_in(w_in), seq)
    q, k, v = _qkv(part_a, _prep_w_uq(w_uq), w_ukv.astype(BF16), q_norm_g.reshape(1, -1),
                   kv_norm_g.reshape(1, -1), cos_t, sin_t, seq)
    o_mla = _mla_attention(q, k, v, batch, seq)
    q_wide, k_wide, v_wide = _moba_select(qkv_mo, batch, seq)
    o_moba = _moba_attention(q_wide, k_wide, v_wide, bias_tiles, batch, seq)
    wo = w_out.astype(BF16)
    wr, br = _prep_router(w_rg, b_rg, w_re, b_re)
    n_mla = MLA_HEADS * MLA_V_DIM
    x1, h2, logits = _out_proj(o_mla, o_moba, wo[:n_mla], wo[n_mla:], xf, mod3, ln1_g.reshape(1, d),
                               ln1_b.reshape(1, d), wr, seq, depth_alpha)
    info, counts = _route(logits, br)
    tr = 256
    nt = (2 * t) // tr + MOE_N_EXPERTS
    cnt = counts[0, MOE_GROUPS:MOE_GROUPS + MOE_N_EXPERTS].astype(jnp.int32)
    ntile = (cnt + tr - 1) // tr
    tile_end = jnp.cumsum(ntile)
    tile_start = tile_end - ntile
    used = tile_end[-1]
    start_lanes = jnp.zeros((1, LANES), F32).at[0, MOE_GROUPS:MOE_GROUPS + MOE_N_EXPERTS].set(
        (tile_start * tr).astype(F32))
    pos = _positions(info, start_lanes)
    pos_a, pos_b = pos[:, 0], pos[:, 1]
    tidx = jnp.minimum(jnp.arange(nt, dtype=jnp.int32), used - 1)
    texp = jnp.sum(tidx[:, None] >= tile_end[None, :], axis=1).astype(jnp.int32)
    eids = jnp.arange(MOE_N_EXPERTS, dtype=jnp.int32)
    later = (eids[None, :] > eids[:, None]) & (ntile[None, :] > 0)
    next_nonempty = jnp.where(later.any(axis=1), jnp.argmax(later, axis=1), -1).astype(jnp.int32)
    nexte = jnp.sum(jnp.where(texp[:, None] == eids[None, :], next_nonempty[None, :], 0), axis=1).astype(jnp.int32)
    xs = _dispatch(pos_a, pos_b, h2, nt * tr)
    ys = _experts(texp, tidx, nexte, used.reshape(1).astype(jnp.int32), xs, w1, w3, w2, tr)
    return _combine(pos_a, pos_b, ys, x1, info, mod3, ln2_g.reshape(1, d), ln2_b.reshape(1, d), seq, depth_alpha)


def kernel(x, c, w_ada, b_ada, w_in, q_norm_g, w_uq, kv_norm_g, w_ukv, w_out, t5_table, ln1_g, ln1_b,
           w_router_group, b_router_group, w_router_expert, b_router_expert, w1, w3, w2, ln2_g, ln2_b):
    batch, seq, d = x.shape
    depth = w_ada.shape[0]
    alpha = (2.0 * depth) ** 0.25
    cos_t, sin_t = _rope_lanes(seq)
    bias_tiles = _t5_tiles(t5_table)
    xf = x.reshape(batch * seq, d)
    for l in range(depth):
        mod3 = _ada_mod(c, w_ada[l], b_ada[l]).reshape(batch, 6, d)
        xf = _layer(xf, mod3, batch, seq, alpha, w_in[l], q_norm_g[l], w_uq[l], kv_norm_g[l], w_ukv[l],
                    w_out[l], bias_tiles, cos_t, sin_t, ln1_g[l], ln1_b[l], w_router_group[l],
                    b_router_group[l], w_router_expert[l], b_router_expert[l], w1[l], w3[l], w2[l],
                    ln2_g[l], ln2_b[l])
    return xf.reshape(batch, seq, d)
```

```python
import functools
import math

import jax
import jax.numpy as jnp
from jax import lax
from jax.experimental import pallas as pl
from jax.experimental.pallas import tpu as pltpu

D_MODEL = 2048
MLA_HEADS = 8
MLA_Q_RANK = 512
MLA_KV_RANK = 256
MLA_NOPE_DIM = 128
MLA_ROPE_DIM = 64
MLA_V_DIM = 128
ROPE_THETA = 10000.0
MOBA_HEADS = 8
MOBA_HEAD_DIM = 128
MOBA_BLOCK = 256
MOBA_TOPK = 3
T5_BUCKETS = 32
T5_MAX_DISTANCE = 128
MOE_GROUPS = 4
MOE_EXPERTS_PER_GROUP = 8
MOE_N_EXPERTS = MOE_GROUPS * MOE_EXPERTS_PER_GROUP
MOE_D_FF = 512
LN_EPS = 1e-5
RMS_EPS = 1e-6
MOBA_WIDTH = MOBA_HEADS * MOBA_HEAD_DIM

LANES = 128
QK_PAD = 256
V_PAD = 256
PART_A = 1024
NEG = -1e30
VMEM_LIMIT = 56 * 1024 * 1024

F32 = jnp.float32
BF16 = jnp.bfloat16


def _cparams(sem, row_dma=False):
    return pltpu.CompilerParams(dimension_semantics=sem, vmem_limit_bytes=VMEM_LIMIT,
                                disable_bounds_checks=row_dma)


def _ln(x):
    mu = jnp.mean(x, axis=-1, keepdims=True)
    xc = x - mu
    var = jnp.mean(xc * xc, axis=-1, keepdims=True)
    return xc * lax.rsqrt(var + LN_EPS)


def _nt_dot(a, b):
    return lax.dot_general(a, b, (((1,), (1,)), ((), ())), preferred_element_type=F32)


def _ada_kernel(ct_ref, w_ref, b_ref, o_ref, *, batch):
    ct = ct_ref[...]
    ca = ct / (1.0 + jnp.exp(-ct))
    w = w_ref[...]
    rows = [jnp.sum(w * ca[:, b:b + 1], axis=0, keepdims=True) for b in range(batch)]
    o_ref[...] = jnp.concatenate(rows, axis=0) + b_ref[...]


def _ada_mod(c, w_ada, b_ada):
    batch, d = c.shape
    n = w_ada.shape[1]
    tn = 512
    ct = jnp.zeros((d, LANES), F32).at[:, :batch].set(c.T)
    return pl.pallas_call(
        functools.partial(_ada_kernel, batch=batch),
        grid=(n // tn,),
        in_specs=[pl.BlockSpec((d, LANES), lambda j: (0, 0)),
                  pl.BlockSpec((d, tn), lambda j: (0, j)),
                  pl.BlockSpec((1, tn), lambda j: (0, j))],
        out_specs=pl.BlockSpec((batch, tn), lambda j: (0, j)),
        out_shape=jax.ShapeDtypeStruct((batch, n), F32),
        compiler_params=_cparams(("arbitrary",)),
        name="ada_mod",
    )(ct, w_ada, b_ada.reshape(1, n))


def _inproj_kernel(x_ref, mod_ref, w_ref, a_ref, b_ref, h_scr):
    j = pl.program_id(1)

    @pl.when(j == 0)
    def _():
        h = _ln(x_ref[...]) * (1.0 + mod_ref[0, 1:2, :]) + mod_ref[0, 0:1, :]
        h_scr[...] = h.astype(BF16)
        a_ref[...] = jnp.dot(h_scr[...], w_ref[...], preferred_element_type=F32)

    @pl.when(j > 0)
    def _():
        b_ref[...] = jnp.dot(h_scr[...], w_ref[...], preferred_element_type=F32).astype(BF16)


def _in_proj(xf, mod3, w4, seq):
    t, d = xf.shape
    n = w4.shape[1]
    tm, tn = 512, PART_A
    per_b = seq // tm
    return pl.pallas_call(
        _inproj_kernel,
        grid=(t // tm, n // tn),
        in_specs=[pl.BlockSpec((tm, d), lambda i, j: (i, 0)),
                  pl.BlockSpec((1, 6, d), lambda i, j: (i // per_b, 0, 0)),
                  pl.BlockSpec((d, tn), lambda i, j: (0, j))],
        out_specs=[pl.BlockSpec((tm, tn), lambda i, j: (i, 0)),
                   pl.BlockSpec((tm, tn), lambda i, j: (i, jnp.maximum(j - 1, 0)))],
        out_shape=[jax.ShapeDtypeStruct((t, PART_A), F32),
                   jax.ShapeDtypeStruct((t, n - PART_A), BF16)],
        scratch_shapes=[pltpu.VMEM((tm, d), BF16)],
        compiler_params=_cparams(("arbitrary", "arbitrary")),
        name="in_proj",
    )(xf, mod3, w4)


def _qkv_kernel(a_ref, wq_ref, wkv_ref, gq_ref, gkv_ref, cos_ref, sin_ref, q_ref, k_ref, v_ref):
    a = a_ref[...]
    cq = a[:, :MLA_Q_RANK]
    ckv = a[:, MLA_Q_RANK:MLA_Q_RANK + MLA_KV_RANK]
    kr = a[:, 768:896]
    krs = a[:, 896:1024]
    cqn = (cq * lax.rsqrt(jnp.mean(cq * cq, axis=-1, keepdims=True) + RMS_EPS) * gq_ref[...]).astype(BF16)
    ckvn = (ckv * lax.rsqrt(jnp.mean(ckv * ckv, axis=-1, keepdims=True) + RMS_EPS) * gkv_ref[...]).astype(BF16)
    q3 = jnp.dot(cqn, wq_ref[...], preferred_element_type=F32)
    kv = jnp.dot(ckvn, wkv_ref[...], preferred_element_type=F32)
    cos = cos_ref[...]
    sin = sin_ref[...]
    krr = (kr * cos + krs * sin).astype(BF16)
    for h in range(MLA_HEADS):
        q0 = h * 3 * LANES
        c0 = h * QK_PAD
        q_ref[:, c0:c0 + LANES] = q3[:, q0:q0 + LANES].astype(BF16)
        q_ref[:, c0 + LANES:c0 + QK_PAD] = (
            q3[:, q0 + LANES:q0 + 2 * LANES] * cos + q3[:, q0 + 2 * LANES:q0 + 3 * LANES] * sin).astype(BF16)
        k_ref[:, c0:c0 + LANES] = kv[:, c0:c0 + LANES].astype(BF16)
        k_ref[:, c0 + LANES:c0 + QK_PAD] = krr
        v_ref[:, c0:c0 + LANES] = kv[:, c0 + LANES:c0 + QK_PAD].astype(BF16)
        v_ref[:, c0 + LANES:c0 + V_PAD] = jnp.ones((a.shape[0], V_PAD - LANES), BF16)


def _qkv(part_a, wq3, wkv, gq, gkv, cos_t, sin_t, seq):
    t = part_a.shape[0]
    tm = 512
    per_b = seq // tm
    hq = MLA_HEADS * QK_PAD
    return pl.pallas_call(
        _qkv_kernel,
        grid=(t // tm,),
        in_specs=[pl.BlockSpec((tm, PART_A), lambda i: (i, 0)),
                  pl.BlockSpec(wq3.shape, lambda i: (0, 0)),
                  pl.BlockSpec(wkv.shape, lambda i: (0, 0)),
                  pl.BlockSpec((1, MLA_Q_RANK), lambda i: (0, 0)),
                  pl.BlockSpec((1, MLA_KV_RANK), lambda i: (0, 0)),
                  pl.BlockSpec((tm, LANES), lambda i: (i % per_b, 0)),
                  pl.BlockSpec((tm, LANES), lambda i: (i % per_b, 0))],
        out_specs=[pl.BlockSpec((tm, hq), lambda i: (i, 0)),
                   pl.BlockSpec((tm, hq), lambda i: (i, 0)),
                   pl.BlockSpec((tm, MLA_HEADS * V_PAD), lambda i: (i, 0))],
        out_shape=[jax.ShapeDtypeStruct((t, hq), BF16),
                   jax.ShapeDtypeStruct((t, hq), BF16),
                   jax.ShapeDtypeStruct((t, MLA_HEADS * V_PAD), BF16)],
        compiler_params=_cparams(("arbitrary",)),
        name="qkv",
    )(part_a, wq3, wkv, gq, gkv, cos_t, sin_t)


ATTN_TILE = 512


def _attn_kernel(*refs, c, masked):
    if masked:
        q_ref, k_ref, v_ref, bias_ref, o_ref, s_a, s_b, m_scr, acc_scr = refs
    else:
        q_ref, k_ref, v_ref, o_ref, s_a, s_b, m_scr, acc_scr = refs
    tile = ATTN_TILE
    qi = pl.program_id(2)

    def put_scores(buf, t):
        k = k_ref[pl.ds(pl.multiple_of(t * tile, tile), tile), :]
        buf[...] = _nt_dot(q_ref[...], k)

    def tile_step(buf, t, kind, nxt=None):
        if nxt is not None:
            put_scores(*nxt)
        s = buf[...]
        if masked:
            if kind == "prev":
                s = s + bias_ref[0, 1] + jnp.where(qi == 0, NEG, 0.0)
            elif kind == "diag":
                s = s + bias_ref[0, 0]
        elif kind == "diag":
            row = lax.broadcasted_iota(jnp.int32, (tile, tile), 0)
            col = lax.broadcasted_iota(jnp.int32, (tile, tile), 1)
            s = jnp.where(row >= col, s, NEG)
        m_old = m_scr[...]
        m_new = jnp.maximum(m_old, jnp.broadcast_to(jnp.max(s, axis=-1, keepdims=True), m_old.shape))
        alpha = jnp.exp2((m_old - m_new) * c)
        p = jnp.concatenate([jnp.exp2(((s[:, j * LANES:(j + 1) * LANES] - m_new) * c).astype(BF16))
                             for j in range(tile // LANES)], axis=1)
        v = v_ref[pl.ds(pl.multiple_of(t * tile, tile), tile), :]
        pv = jnp.dot(p, v, preferred_element_type=F32)
        acc_scr[...] = jnp.concatenate([alpha, alpha], axis=1) * acc_scr[...] + pv
        m_scr[...] = m_new

    m_scr[...] = jnp.full(m_scr.shape, NEG, F32)
    acc_scr[...] = jnp.zeros(acc_scr.shape, F32)
    put_scores(s_a, 0)

    n_far = jnp.maximum(qi - 1, 0) if masked else qi
    pn = jnp.maximum(qi - 1, 0)

    def pair(j, _):
        t = 2 * j
        tile_step(s_a, t, "far", (s_b, t + 1))
        tile_step(s_b, t + 1, "far", (s_a, t + 2))
        return 0

    lax.fori_loop(0, n_far // 2, pair, 0)
    t0 = 2 * (n_far // 2)

    def tail(first, second):
        if masked:
            tile_step(first, pn, "prev", (second, qi))
            tile_step(second, qi, "diag")
        else:
            tile_step(first, qi, "diag")

    @pl.when(n_far % 2 == 1)
    def _():
        tile_step(s_a, t0, "far", (s_b, t0 + 1))
        tail(s_b, s_a)

    @pl.when(n_far % 2 == 0)
    def _():
        tail(s_a, s_b)

    o_ref[...] = (acc_scr[:, :LANES] / acc_scr[:, LANES:]).astype(o_ref.dtype)


def _attn_scratch():
    tile = ATTN_TILE
    return [pltpu.VMEM((tile, tile), F32), pltpu.VMEM((tile, tile), F32),
            pltpu.VMEM((tile, LANES), F32), pltpu.VMEM((tile, V_PAD), F32)]


def _mla_attention(q, k, v, batch, seq):
    tile = ATTN_TILE
    nq = seq // tile
    c = math.log2(math.e) / math.sqrt(MLA_NOPE_DIM + MLA_ROPE_DIM)
    return pl.pallas_call(
        functools.partial(_attn_kernel, c=c, masked=False),
        grid=(batch, MLA_HEADS, nq),
        in_specs=[pl.BlockSpec((tile, QK_PAD), lambda b, h, i: (b * nq + i, h)),
                  pl.BlockSpec((seq, QK_PAD), lambda b, h, i: (b, h)),
                  pl.BlockSpec((seq, V_PAD), lambda b, h, i: (b, h))],
        out_specs=pl.BlockSpec((tile, MLA_V_DIM), lambda b, h, i: (b * nq + i, h)),
        out_shape=jax.ShapeDtypeStruct((batch * seq, MLA_HEADS * MLA_V_DIM), BF16),
        scratch_shapes=_attn_scratch(),
        compiler_params=_cparams(("arbitrary", "arbitrary", "arbitrary")),
        name="mla_attn",
    )(q, k, v)


def _moba_select_kernel(q_ref, k_ref, v_ref, qa_ref, ka_ref, va_ref, *, seq, nb):
    kf = k_ref[...].astype(F32)
    km = jnp.sum(kf.reshape(nb, MOBA_BLOCK, MOBA_HEAD_DIM), axis=1) * (1.0 / MOBA_BLOCK)
    km = jnp.concatenate([km, jnp.zeros((LANES - nb, MOBA_HEAD_DIM), F32)], axis=0)
    km_hi = km.astype(BF16)
    km_lo = (km - km_hi.astype(F32)).astype(BF16)
    q = q_ref[...]
    gate = _nt_dot(q, km_hi) + _nt_dot(q, km_lo)
    blk = lax.broadcasted_iota(jnp.int32, (seq, LANES), 1)
    qblk = jnp.right_shift(lax.broadcasted_iota(jnp.int32, (seq, LANES), 0), MOBA_BLOCK.bit_length() - 1)
    g = jnp.where(blk < qblk, gate, NEG)
    visible = blk == qblk
    for _ in range(MOBA_TOPK):
        mx = jnp.max(g, axis=-1, keepdims=True)
        first = jnp.min(jnp.where(g == mx, blk, LANES), axis=-1, keepdims=True)
        pick = (blk == first) & (mx > 0.5 * NEG)
        visible = visible | pick
        g = jnp.where(pick, NEG, g)
    in_use = blk < nb
    qa_ref[:, :MOBA_HEAD_DIM] = q
    qa_ref[:, MOBA_HEAD_DIM:] = jnp.where(visible | ~in_use, 0.0, NEG).astype(BF16)
    ka_ref[:, :MOBA_HEAD_DIM] = k_ref[...]
    ka_ref[:, MOBA_HEAD_DIM:] = jnp.where(blk == qblk, 1.0, 0.0).astype(BF16)
    va_ref[:, :MOBA_HEAD_DIM] = v_ref[...]
    va_ref[:, MOBA_HEAD_DIM:] = jnp.ones((seq, V_PAD - MOBA_HEAD_DIM), BF16)


def _moba_select(qkv_mo, batch, seq):
    nb = seq // MOBA_BLOCK
    assert MOBA_HEAD_DIM == LANES and nb <= QK_PAD - MOBA_HEAD_DIM
    wide = jax.ShapeDtypeStruct((batch * seq, MOBA_HEADS * QK_PAD), BF16)
    return pl.pallas_call(
        functools.partial(_moba_select_kernel, seq=seq, nb=nb),
        grid=(batch, MOBA_HEADS),
        in_specs=[pl.BlockSpec((seq, MOBA_HEAD_DIM), lambda b, h: (b, h)),
                  pl.BlockSpec((seq, MOBA_HEAD_DIM), lambda b, h: (b, MOBA_HEADS + h)),
                  pl.BlockSpec((seq, MOBA_HEAD_DIM), lambda b, h: (b, 2 * MOBA_HEADS + h))],
        out_specs=[pl.BlockSpec((seq, QK_PAD), lambda b, h: (b, h)),
                   pl.BlockSpec((seq, QK_PAD), lambda b, h: (b, h)),
                   pl.BlockSpec((seq, V_PAD), lambda b, h: (b, h))],
        out_shape=[wide, wide, jax.ShapeDtypeStruct((batch * seq, MOBA_HEADS * V_PAD), BF16)],
        compiler_params=_cparams(("arbitrary", "arbitrary")),
        name="moba_select",
    )(qkv_mo, qkv_mo, qkv_mo)


def _t5_kernel(tab_ref, o_ref, *, inv_scale):
    h = pl.program_id(0)
    r = lax.broadcasted_iota(jnp.int32, (LANES, LANES), 0)
    c = lax.broadcasted_iota(jnp.int32, (LANES, LANES), 1)
    max_exact = T5_BUCKETS // 2
    far = tab_ref[T5_BUCKETS - 1, h]

    def block(offset):
        rel = offset + r - c
        n = jnp.maximum(rel, 0)
        nf = jnp.maximum(n, 1).astype(F32)
        large = max_exact + (jnp.log(nf / max_exact) / math.log(T5_MAX_DISTANCE / max_exact)
                             * (T5_BUCKETS - max_exact)).astype(jnp.int32)
        large = jnp.minimum(large, T5_BUCKETS - 1)
        bucket = jnp.where(n < max_exact, n, large)
        bias = jnp.zeros((LANES, LANES), F32)
        for j in range(T5_BUCKETS):
            bias = jnp.where(bucket == j, tab_ref[j, h], bias)
        return jnp.where(rel >= 0, (bias - far) * inv_scale, NEG)

    near = {0: block(0), 1: block(LANES)}
    nblk = ATTN_TILE // LANES
    for d in range(2):
        for i in range(nblk):
            for j in range(nblk):
                k = d * nblk + i - j
                if k < 0:
                    val = jnp.full((LANES, LANES), NEG, F32)
                else:
                    val = near.get(k, jnp.zeros((LANES, LANES), F32))
                o_ref[0, d, i * LANES:(i + 1) * LANES, j * LANES:(j + 1) * LANES] = val


def _t5_tiles(t5_table):
    assert LANES >= T5_MAX_DISTANCE
    tile = ATTN_TILE
    return pl.pallas_call(
        functools.partial(_t5_kernel, inv_scale=math.sqrt(MOBA_HEAD_DIM)),
        grid=(MOBA_HEADS,),
        in_specs=[pl.BlockSpec(memory_space=pltpu.SMEM)],
        out_specs=pl.BlockSpec((1, 2, tile, tile), lambda h: (h, 0, 0, 0)),
        out_shape=jax.ShapeDtypeStruct((MOBA_HEADS, 2, tile, tile), F32),
        compiler_params=_cparams(("arbitrary",)),
        name="t5_tiles",
    )(t5_table)


def _moba_attention(q_wide, k_wide, v_wide, bias, batch, seq):
    tile = ATTN_TILE
    nq = seq // tile
    c = math.log2(math.e) / math.sqrt(MOBA_HEAD_DIM)
    dh = MOBA_HEAD_DIM
    return pl.pallas_call(
        functools.partial(_attn_kernel, c=c, masked=True),
        grid=(batch, MOBA_HEADS, nq),
        in_specs=[pl.BlockSpec((tile, QK_PAD), lambda b, h, i: (b * nq + i, h)),
                  pl.BlockSpec((seq, QK_PAD), lambda b, h, i: (b, h)),
                  pl.BlockSpec((seq, V_PAD), lambda b, h, i: (b, h)),
                  pl.BlockSpec((1, 2, tile, tile), lambda b, h, i: (h, 0, 0, 0))],
        out_specs=pl.BlockSpec((tile, dh), lambda b, h, i: (b * nq + i, h)),
        out_shape=jax.ShapeDtypeStruct((batch * seq, MOBA_WIDTH), BF16),
        scratch_shapes=_attn_scratch(),
        compiler_params=_cparams(("arbitrary", "arbitrary", "arbitrary")),
        name="moba_attn",
    )(q_wide, k_wide, v_wide, bias)


def _outproj_kernel(oa_ref, ob_ref, wa_ref, wb_ref, x_ref, mod_ref, g_ref, b_ref, wr_ref,
                    x1_ref, h2_ref, lg_ref, y_a, y_b, *, alpha, n_tiles):
    i = pl.program_id(0)
    tm, d = y_a.shape
    chunks = 4
    cn, cr = d // chunks, tm // chunks

    def matmul_into(y_ref, c):
        cols = slice(c * cn, (c + 1) * cn)
        y_ref[:, cols] = (jnp.dot(oa_ref[...], wa_ref[:, cols], preferred_element_type=F32)
                          + jnp.dot(ob_ref[...], wb_ref[:, cols], preferred_element_type=F32))

    def epilogue(y_ref, c):
        rows = slice(c * cr, (c + 1) * cr)
        z = alpha * x_ref[rows, :] + mod_ref[0, 2:3, :] * y_ref[rows, :]
        x1 = _ln(z) * g_ref[...] + b_ref[...]
        x1_ref[rows, :] = x1
        h2 = _ln(x1) * (1.0 + mod_ref[0, 4:5, :]) + mod_ref[0, 3:4, :]
        h2_ref[rows, :] = h2
        h_hi = h2.astype(BF16)
        h_lo = (h2 - h_hi.astype(F32)).astype(BF16)
        zz = (jnp.dot(h_hi, wr_ref[...], preferred_element_type=F32)
              + jnp.dot(h_lo, wr_ref[...], preferred_element_type=F32))
        lg_ref[rows, :] = zz + pltpu.roll(zz, LANES // 2, 1)

    def step(y_new, y_old):
        for c in range(chunks):
            if y_new is not None:
                matmul_into(y_new, c)
            if y_old is not None:
                epilogue(y_old, c)

    inner = (i > 0) & (i < n_tiles)
    pl.when(i == 0)(lambda: step(y_a, None))
    pl.when(inner & (i % 2 == 0))(lambda: step(y_a, y_b))
    pl.when(inner & (i % 2 == 1))(lambda: step(y_b, y_a))
    pl.when(i == n_tiles)(lambda: step(None, y_b if n_tiles % 2 == 0 else y_a))


def _out_proj(o_mla, o_moba, wa, wb, xf, mod3, ln_g, ln_b, wr, seq, alpha):
    t, d = xf.shape
    tm = 512
    n = t // tm
    per_b = seq // tm
    ka, kb = wa.shape[0], wb.shape[0]
    once = pl.Buffered(1)

    def cur(i):
        return (jnp.minimum(i, n - 1), 0)

    def lag(i):
        return (jnp.maximum(i - 1, 0), 0)

    return pl.pallas_call(
        functools.partial(_outproj_kernel, alpha=alpha, n_tiles=n),
        grid=(n + 1,),
        in_specs=[pl.BlockSpec((tm, ka), cur),
                  pl.BlockSpec((tm, kb), cur),
                  pl.BlockSpec((ka, d), lambda i: (0, 0), pipeline_mode=once),
                  pl.BlockSpec((kb, d), lambda i: (0, 0), pipeline_mode=once),
                  pl.BlockSpec((tm, d), lag),
                  pl.BlockSpec((1, 6, d), lambda i: (jnp.maximum(i - 1, 0) // per_b, 0, 0)),
                  pl.BlockSpec((1, d), lambda i: (0, 0)),
                  pl.BlockSpec((1, d), lambda i: (0, 0)),
                  pl.BlockSpec((d, LANES), lambda i: (0, 0), pipeline_mode=once)],
        out_specs=[pl.BlockSpec((tm, d), lag),
                   pl.BlockSpec((tm, d), lag),
                   pl.BlockSpec((tm, LANES), lag)],
        out_shape=[jax.ShapeDtypeStruct((t, d), F32),
                   jax.ShapeDtypeStruct((t, d), F32),
                   jax.ShapeDtypeStruct((t, LANES), F32)],
        scratch_shapes=[pltpu.VMEM((tm, d), F32), pltpu.VMEM((tm, d), F32)],
        compiler_params=_cparams(("arbitrary",)),
        name="out_proj",
    )(o_mla, o_moba, wa, wb, xf, mod3, ln_g, ln_b, wr)


def _route_kernel(lg_ref, br_ref, info_ref, cnt_ref, run_scr, *, tm):
    i = pl.program_id(0)

    @pl.when(i == 0)
    def _():
        run_scr[...] = jnp.zeros_like(run_scr)

    lg = lg_ref[...] + br_ref[...]
    lane = lax.broadcasted_iota(jnp.int32, (tm, LANES), 1)
    e_lo, e_hi = MOE_GROUPS, MOE_GROUPS + MOE_N_EXPERTS
    is_g = lane < e_lo
    gl = jnp.where(is_g, lg, NEG)
    gmax = jnp.max(gl, axis=-1, keepdims=True)
    gidx = jnp.min(jnp.where(gl == gmax, lane, LANES), axis=-1, keepdims=True)
    g_p = 1.0 / jnp.sum(jnp.where(is_g, jnp.exp(gl - gmax), 0.0), axis=-1, keepdims=True)
    grp_of_lane = jnp.right_shift(lane - e_lo, MOE_EXPERTS_PER_GROUP.bit_length() - 1)
    in_grp = (lane >= e_lo) & (lane < e_hi) & (grp_of_lane == gidx)
    el = jnp.where(in_grp, lg, NEG)
    m1 = jnp.max(el, axis=-1, keepdims=True)
    l1 = jnp.min(jnp.where(el == m1, lane, LANES), axis=-1, keepdims=True)
    el2 = jnp.where(lane == l1, NEG, el)
    m2 = jnp.max(el2, axis=-1, keepdims=True)
    l2 = jnp.min(jnp.where(el2 == m2, lane, LANES), axis=-1, keepdims=True)
    zsum = jnp.sum(jnp.where(in_grp, jnp.exp(el - m1), 0.0), axis=-1, keepdims=True)
    p1 = 1.0 / zsum
    p2 = jnp.exp(m2 - m1) / zsum
    wa = g_p * (p1 / (p1 + p2))
    wb = g_p * (p2 / (p1 + p2))
    hot_a = lane == l1
    hot_b = lane == l2
    onehot = jnp.where(hot_a | hot_b, 1.0, 0.0)
    r = lax.broadcasted_iota(jnp.int32, (tm, tm), 0)
    c = lax.broadcasted_iota(jnp.int32, (tm, tm), 1)
    lower = jnp.where(c < r, 1.0, 0.0).astype(BF16)
    before = jnp.dot(lower, onehot.astype(BF16), preferred_element_type=F32) + run_scr[...]
    rank_a = jnp.sum(jnp.where(hot_a, before, 0.0), axis=-1, keepdims=True)
    rank_b = jnp.sum(jnp.where(hot_b, before, 0.0), axis=-1, keepdims=True)
    run_scr[...] += jnp.sum(onehot, axis=0, keepdims=True)
    info = jnp.zeros((tm, LANES), F32)
    for k, val in enumerate([(l1 - e_lo).astype(F32), (l2 - e_lo).astype(F32), wa, wb, rank_a, rank_b]):
        info = jnp.where(lane == k, val, info)
    info_ref[...] = info
    cnt_ref[...] = run_scr[...]


def _route(logits, br):
    t = logits.shape[0]
    tm = 512
    return pl.pallas_call(
        functools.partial(_route_kernel, tm=tm),
        grid=(t // tm,),
        in_specs=[pl.BlockSpec((tm, LANES), lambda i: (i, 0)),
                  pl.BlockSpec((1, LANES), lambda i: (0, 0))],
        out_specs=[pl.BlockSpec((tm, LANES), lambda i: (i, 0)),
                   pl.BlockSpec((1, LANES), lambda i: (0, 0))],
        out_shape=[jax.ShapeDtypeStruct((t, LANES), F32),
                   jax.ShapeDtypeStruct((1, LANES), F32)],
        scratch_shapes=[pltpu.VMEM((1, LANES), F32)],
        compiler_params=_cparams(("arbitrary",)),
        name="route",
    )(logits, br)


def _pos_kernel(info_ref, start_ref, pos_ref):
    info = info_ref[...]
    tm = info.shape[0]
    lane = lax.broadcasted_iota(jnp.int32, (tm, LANES), 1)
    start = start_ref[...]
    cols = []
    for k in range(2):
        e = jnp.sum(jnp.where(lane == k, info, 0.0), axis=-1, keepdims=True).astype(jnp.int32)
        rank = jnp.sum(jnp.where(lane == 4 + k, info, 0.0), axis=-1, keepdims=True)
        base = jnp.sum(jnp.where(lane == e + MOE_GROUPS, start, 0.0), axis=-1, keepdims=True)
        cols.append(base + rank)
    pos_ref[...] = jnp.where(lane == 0, cols[0], jnp.where(lane == 1, cols[1], 0.0)).astype(jnp.int32)


def _positions(info, start_lanes):
    t = info.shape[0]
    tm = 1024
    return pl.pallas_call(
        _pos_kernel,
        grid=(t // tm,),
        in_specs=[pl.BlockSpec((tm, LANES), lambda i: (i, 0)),
                  pl.BlockSpec((1, LANES), lambda i: (0, 0))],
        out_specs=pl.BlockSpec((tm, LANES), lambda i: (i, 0)),
        out_shape=jax.ShapeDtypeStruct((t, LANES), jnp.int32),
        compiler_params=_cparams(("arbitrary",)),
        name="positions",
    )(info, start_lanes)


def _row_copy(src_ref, src_row, dst_ref, dst_row, sem):
    return pltpu.make_async_copy(src_ref.at[pl.ds(src_row, 1)], dst_ref.at[pl.ds(dst_row, 1)], sem)


def _dispatch_kernel(pa_ref, pb_ref, h_ref, xs_in_ref, xs_ref, sem, *, tm):
    del xs_in_ref
    base = pl.program_id(0) * tm

    def issue(t, _):
        _row_copy(h_ref, t, xs_ref, pa_ref[base + t], sem).start()
        _row_copy(h_ref, t, xs_ref, pb_ref[base + t], sem).start(priority=1)
        return 0

    lax.fori_loop(0, tm, issue, 0, unroll=8)
    for _ in range(2):
        pltpu.make_async_copy(h_ref, xs_ref.at[pl.ds(0, tm)], sem).wait()


def _dispatch(pos_a, pos_b, h2, rows):
    t, d = h2.shape
    tm = 256
    xs0 = jnp.zeros((rows, d), F32)
    grid_spec = pltpu.PrefetchScalarGridSpec(
        num_scalar_prefetch=2,
        grid=(t // tm,),
        in_specs=[pl.BlockSpec((tm, d), lambda i, pa, pb: (i, 0)),
                  pl.BlockSpec(memory_space=pl.ANY)],
        out_specs=pl.BlockSpec(memory_space=pl.ANY),
        scratch_shapes=[pltpu.SemaphoreType.DMA(())],
    )
    return pl.pallas_call(
        functools.partial(_dispatch_kernel, tm=tm),
        grid_spec=grid_spec,
        out_shape=jax.ShapeDtypeStruct((rows, d), F32),
        input_output_aliases={3: 0},
        compiler_params=_cparams(("arbitrary",), row_dma=True),
        name="dispatch",
    )(pos_a, pos_b, h2, xs0)


def _experts_kernel(texp_ref, tidx_ref, nexte_ref, used_ref, x_ref, w1_hbm, w3_hbm, w2_hbm, o_ref,
                    w1_f32, w3_f32, w2_f32, w1_scr, w3_scr, w2_scr, sems):
    i = pl.program_id(0)
    prev = texp_ref[jnp.maximum(i - 1, 0)]

    def weight_copies(e):
        return [pltpu.make_async_copy(w1_hbm.at[e], w1_f32, sems.at[0]),
                pltpu.make_async_copy(w3_hbm.at[e], w3_f32, sems.at[1]),
                pltpu.make_async_copy(w2_hbm.at[e], w2_f32, sems.at[2])]

    @pl.when(i == 0)
    def _():
        for cp in weight_copies(texp_ref[0]):
            cp.start()

    @pl.when((i == 0) | (texp_ref[i] != prev))
    def _():
        for cp in weight_copies(texp_ref[i]):
            cp.wait()
        w1_scr[...] = w1_f32[...].astype(BF16)
        w3_scr[...] = w3_f32[...].astype(BF16)
        w2_scr[...] = w2_f32[...].astype(BF16)

        @pl.when(nexte_ref[i] >= 0)
        def _():
            for cp in weight_copies(nexte_ref[i]):
                cp.start()

    @pl.when(i < used_ref[0])
    def _():
        x = x_ref[...].astype(BF16)
        a = jnp.dot(x, w1_scr[...], preferred_element_type=F32)
        b = jnp.dot(x, w3_scr[...], preferred_element_type=F32)
        hid = (a / (1.0 + jnp.exp(-a))) * b
        o_ref[...] = jnp.dot(hid.astype(BF16), w2_scr[...], preferred_element_type=F32)

    @pl.when(i >= used_ref[0])
    def _():
        o_ref[...] = jnp.zeros_like(o_ref)


def _experts(texp, tidx, nexte, used, xs, w1, w3, w2, tr):
    rows, d = xs.shape
    nt = rows // tr
    f = w1.shape[2]
    grid_spec = pltpu.PrefetchScalarGridSpec(
        num_scalar_prefetch=4,
        grid=(nt,),
        in_specs=[pl.BlockSpec((tr, d), lambda i, te, ti, ne, u: (ti[i], 0)),
                  pl.BlockSpec(memory_space=pl.ANY),
                  pl.BlockSpec(memory_space=pl.ANY),
                  pl.BlockSpec(memory_space=pl.ANY)],
        out_specs=pl.BlockSpec((tr, d), lambda i, te, ti, ne, u: (i, 0)),
        scratch_shapes=[pltpu.VMEM((d, f), F32), pltpu.VMEM((d, f), F32), pltpu.VMEM((f, d), F32),
                        pltpu.VMEM((d, f), BF16), pltpu.VMEM((d, f), BF16), pltpu.VMEM((f, d), BF16),
                        pltpu.SemaphoreType.DMA((3,))],
    )
    return pl.pallas_call(
        _experts_kernel,
        grid_spec=grid_spec,
        out_shape=jax.ShapeDtypeStruct((rows, d), F32),
        compiler_params=_cparams(("arbitrary",)),
        name="experts",
    )(texp, tidx, nexte, used, xs, w1, w3, w2)


def _combine_kernel(pa_ref, pb_ref, ys_ref, x1_ref, info_ref, mod_ref, g_ref, b_ref, o_ref,
                    buf_a, buf_b, sem, *, tm, alpha):
    base = pl.program_id(0) * tm

    def issue(t, _):
        _row_copy(ys_ref, pa_ref[base + t], buf_a, t, sem).start()
        _row_copy(ys_ref, pb_ref[base + t], buf_b, t, sem).start(priority=1)
        return 0

    lax.fori_loop(0, tm, issue, 0, unroll=8)
    for buf in (buf_a, buf_b):
        pltpu.make_async_copy(ys_ref.at[pl.ds(0, tm)], buf, sem).wait()
    info = info_ref[...]
    y = info[:, 2:3] * buf_a[...] + info[:, 3:4] * buf_b[...]
    z = alpha * x1_ref[...] + mod_ref[0, 5:6, :] * y
    o_ref[...] = _ln(z) * g_ref[...] + b_ref[...]


def _combine(pos_a, pos_b, ys, x1, info, mod3, ln_g, ln_b, seq, alpha):
    t, d = x1.shape
    tm = 256
    per_b = seq // tm
    grid_spec = pltpu.PrefetchScalarGridSpec(
        num_scalar_prefetch=2,
        grid=(t // tm,),
        in_specs=[pl.BlockSpec(memory_space=pl.ANY),
                  pl.BlockSpec((tm, d), lambda i, pa, pb: (i, 0)),
                  pl.BlockSpec((tm, LANES), lambda i, pa, pb: (i, 0)),
                  pl.BlockSpec((1, 6, d), lambda i, pa, pb: (i // per_b, 0, 0)),
                  pl.BlockSpec((1, d), lambda i, pa, pb: (0, 0)),
                  pl.BlockSpec((1, d), lambda i, pa, pb: (0, 0))],
        out_specs=pl.BlockSpec((tm, d), lambda i, pa, pb: (i, 0)),
        scratch_shapes=[pltpu.VMEM((tm, d), F32), pltpu.VMEM((tm, d), F32), pltpu.SemaphoreType.DMA(())],
    )
    return pl.pallas_call(
        functools.partial(_combine_kernel, tm=tm, alpha=alpha),
        grid_spec=grid_spec,
        out_shape=jax.ShapeDtypeStruct((t, d), F32),
        compiler_params=_cparams(("arbitrary",), row_dma=True),
        name="combine",
    )(pos_a, pos_b, ys, x1, info, mod3, ln_g, ln_b)


def _prep_w_in(w):
    cq = w[:, :MLA_Q_RANK]
    ckv = w[:, MLA_Q_RANK:MLA_Q_RANK + MLA_KV_RANK]
    r0 = MLA_Q_RANK + MLA_KV_RANK
    kr = w[:, r0:r0 + MLA_ROPE_DIM]
    mo = w[:, r0 + MLA_ROPE_DIM:]
    half = MLA_ROPE_DIM // 2
    z = jnp.zeros((w.shape[0], LANES - MLA_ROPE_DIM), w.dtype)
    return jnp.concatenate([cq, ckv, kr, z, kr[:, half:], kr[:, :half], z, mo], axis=1).astype(BF16)


def _prep_w_uq(w):
    r = w.shape[0]
    w = w.reshape(r, MLA_HEADS, MLA_NOPE_DIM + MLA_ROPE_DIM)
    half = MLA_ROPE_DIM // 2
    nope = w[:, :, :MLA_NOPE_DIM]
    x1 = w[:, :, MLA_NOPE_DIM:MLA_NOPE_DIM + half]
    x2 = w[:, :, MLA_NOPE_DIM + half:]
    z = jnp.zeros((r, MLA_HEADS, LANES - MLA_ROPE_DIM), w.dtype)
    return jnp.concatenate([nope, x1, x2, z, x2, x1, z], axis=2).reshape(r, MLA_HEADS * 3 * LANES).astype(BF16)


def _rope_lanes(seq):
    inv = 1.0 / (ROPE_THETA ** (jnp.arange(0, MLA_ROPE_DIM, 2, dtype=F32) / MLA_ROPE_DIM))
    ang = jnp.arange(seq, dtype=F32)[:, None] * inv[None, :]
    cos, sin = jnp.cos(ang), jnp.sin(ang)
    z = jnp.zeros((seq, LANES - MLA_ROPE_DIM), F32)
    return jnp.concatenate([cos, cos, z], axis=1), jnp.concatenate([-sin, sin, z], axis=1)


def _prep_router(w_rg, b_rg, w_re, b_re):
    d = w_rg.shape[0]
    w = jnp.concatenate([w_rg, w_re], axis=1)
    n = w.shape[1]
    hi = w.astype(BF16)
    lo = (w - hi.astype(F32)).astype(BF16)
    z = jnp.zeros((d, LANES // 2 - n), BF16)
    wr = jnp.concatenate([hi, z, lo, z], axis=1)
    br = jnp.zeros((1, LANES), F32).at[0, :n].set(jnp.concatenate([b_rg, b_re]))
    return wr, br


def _layer(xf, mod3, batch, seq, depth_alpha, w_in, q_norm_g, w_uq, kv_norm_g, w_ukv, w_out, bias_tiles,
           cos_t, sin_t, ln1_g, ln1_b, w_rg, b_rg, w_re, b_re, w1, w3, w2, ln2_g, ln2_b):
    t, d = xf.shape
    part_a, qkv_mo = _in_proj(xf, mod3, _prep_w_in(w_in), seq)
    q, k, v = _qkv(part_a, _prep_w_uq(w_uq), w_ukv.astype(BF16), q_norm_g.reshape(1, -1),
                   kv_norm_g.reshape(1, -1), cos_t, sin_t, seq)
    o_mla = _mla_attention(q, k, v, batch, seq)
    q_wide, k_wide, v_wide = _moba_select(qkv_mo, batch, seq)
    o_moba = _moba_attention(q_wide, k_wide, v_wide, bias_tiles, batch, seq)
    wo = w_out.astype(BF16)
    wr, br = _prep_router(w_rg, b_rg, w_re, b_re)
    n_mla = MLA_HEADS * MLA_V_DIM
    x1, h2, logits = _out_proj(o_mla, o_moba, wo[:n_mla], wo[n_mla:], xf, mod3, ln1_g.reshape(1, d),
                               ln1_b.reshape(1, d), wr, seq, depth_alpha)
    info, counts = _route(logits, br)
    tr = 256
    nt = (2 * t) // tr + MOE_N_EXPERTS
    cnt = counts[0, MOE_GROUPS:MOE_GROUPS + MOE_N_EXPERTS].astype(jnp.int32)
    ntile = (cnt + tr - 1) // tr
    tile_end = jnp.cumsum(ntile)
    tile_start = tile_end - ntile
    used = tile_end[-1]
    start_lanes = jnp.zeros((1, LANES), F32).at[0, MOE_GROUPS:MOE_GROUPS + MOE_N_EXPERTS].set(
        (tile_start * tr).astype(F32))
    pos = _positions(info, start_lanes)
    pos_a, pos_b = pos[:, 0], pos[:, 1]
    tidx = jnp.minimum(jnp.arange(nt, dtype=jnp.int32), used - 1)
    texp = jnp.sum(tidx[:, None] >= tile_end[None, :], axis=1).astype(jnp.int32)
    eids = jnp.arange(MOE_N_EXPERTS, dtype=jnp.int32)
    later = (eids[None, :] > eids[:, None]) & (ntile[None, :] > 0)
    next_nonempty = jnp.where(later.any(axis=1), jnp.argmax(later, axis=1), -1).astype(jnp.int32)
    nexte = jnp.sum(jnp.where(texp[:, None] == eids[None, :], next_nonempty[None, :], 0), axis=1).astype(jnp.int32)
    xs = _dispatch(pos_a, pos_b, h2, nt * tr)
    ys = _experts(texp, tidx, nexte, used.reshape(1).astype(jnp.int32), xs, w1, w3, w2, tr)
    return _combine(pos_a, pos_b, ys, x1, info, mod3, ln2_g.reshape(1, d), ln2_b.reshape(1, d), seq, depth_alpha)


def kernel(x, c, w_ada, b_ada, w_in, q_norm_g, w_uq, kv_norm_g, w_ukv, w_out, t5_table, ln1_g, ln1_b,
           w_router_group, b_router_group, w_router_expert, b_router_expert, w1, w3, w2, ln2_g, ln2_b):
    batch, seq, d = x.shape
    depth = w_ada.shape[0]
    alpha = (2.0 * depth) ** 0.25
    cos_t, sin_t = _rope_lanes(seq)
    bias_tiles = _t5_tiles(t5_table)
    xf = x.reshape(batch * seq, d)
    for l in range(depth):
        mod3 = _ada_mod(c, w_ada[l], b_ada[l]).reshape(batch, 6, d)
        xf = _layer(xf, mod3, batch, seq, alpha, w_in[l], q_norm_g[l], w_uq[l], kv_norm_g[l], w_ukv[l],
                    w_out[l], bias_tiles, cos_t, sin_t, ln1_g[l], ln1_b[l], w_router_group[l],
                    b_router_group[l], w_router_expert[l], b_router_expert[l], w1[l], w3[l], w2[l],
                    ln2_g[l], ln2_b[l])
    return xf.reshape(batch, seq, d)
```

```python
import functools
import math

import jax
import jax.numpy as jnp
from jax import lax
from jax.experimental import pallas as pl
from jax.experimental.pallas import tpu as pltpu

D_MODEL = 2048
MLA_HEADS = 8
MLA_Q_RANK = 512
MLA_KV_RANK = 256
MLA_NOPE_DIM = 128
MLA_ROPE_DIM = 64
MLA_V_DIM = 128
ROPE_THETA = 10000.0
MOBA_HEADS = 8
MOBA_HEAD_DIM = 128
MOBA_BLOCK = 256
MOBA_TOPK = 3
T5_BUCKETS = 32
T5_MAX_DISTANCE = 128
MOE_GROUPS = 4
MOE_EXPERTS_PER_GROUP = 8
MOE_N_EXPERTS = MOE_GROUPS * MOE_EXPERTS_PER_GROUP
MOE_D_FF = 512
LN_EPS = 1e-5
RMS_EPS = 1e-6
MOBA_WIDTH = MOBA_HEADS * MOBA_HEAD_DIM

LANES = 128
QK_PAD = 256
V_PAD = 256
PART_A = 1024
NEG = -1e30
VMEM_LIMIT = 56 * 1024 * 1024

F32 = jnp.float32
BF16 = jnp.bfloat16


def _cparams(sem, row_dma=False):
    return pltpu.CompilerParams(dimension_semantics=sem, vmem_limit_bytes=VMEM_LIMIT,
                                disable_bounds_checks=row_dma)


def _ln(x):
    mu = jnp.mean(x, axis=-1, keepdims=True)
    xc = x - mu
    var = jnp.mean(xc * xc, axis=-1, keepdims=True)
    return xc * lax.rsqrt(var + LN_EPS)


def _nt_dot(a, b):
    return lax.dot_general(a, b, (((1,), (1,)), ((), ())), preferred_element_type=F32)


def _ada_kernel(ct_ref, w_ref, b_ref, o_ref, *, batch):
    ct = ct_ref[...]
    ca = ct / (1.0 + jnp.exp(-ct))
    w = w_ref[...]
    rows = [jnp.sum(w * ca[:, b:b + 1], axis=0, keepdims=True) for b in range(batch)]
    o_ref[...] = jnp.concatenate(rows, axis=0) + b_ref[...]


def _ada_mod(c, w_ada, b_ada):
    batch, d = c.shape
    n = w_ada.shape[1]
    tn = 512
    ct = jnp.zeros((d, LANES), F32).at[:, :batch].set(c.T)
    return pl.pallas_call(
        functools.partial(_ada_kernel, batch=batch),
        grid=(n // tn,),
        in_specs=[pl.BlockSpec((d, LANES), lambda j: (0, 0)),
                  pl.BlockSpec((d, tn), lambda j: (0, j)),
                  pl.BlockSpec((1, tn), lambda j: (0, j))],
        out_specs=pl.BlockSpec((batch, tn), lambda j: (0, j)),
        out_shape=jax.ShapeDtypeStruct((batch, n), F32),
        compiler_params=_cparams(("arbitrary",)),
        name="ada_mod",
    )(ct, w_ada, b_ada.reshape(1, n))


def _inproj_kernel(x_ref, mod_ref, w_ref, a_ref, b_ref, h_scr):
    j = pl.program_id(1)

    @pl.when(j == 0)
    def _():
        h = _ln(x_ref[...]) * (1.0 + mod_ref[0, 1:2, :]) + mod_ref[0, 0:1, :]
        h_scr[...] = h.astype(BF16)
        a_ref[...] = jnp.dot(h_scr[...], w_ref[...], preferred_element_type=F32)

    @pl.when(j > 0)
    def _():
        b_ref[...] = jnp.dot(h_scr[...], w_ref[...], preferred_element_type=F32).astype(BF16)


def _in_proj(xf, mod3, w4, seq):
    t, d = xf.shape
    n = w4.shape[1]
    tm, tn = 512, PART_A
    per_b = seq // tm
    return pl.pallas_call(
        _inproj_kernel,
        grid=(t // tm, n // tn),
        in_specs=[pl.BlockSpec((tm, d), lambda i, j: (i, 0)),
                  pl.BlockSpec((1, 6, d), lambda i, j: (i // per_b, 0, 0)),
                  pl.BlockSpec((d, tn), lambda i, j: (0, j))],
        out_specs=[pl.BlockSpec((tm, tn), lambda i, j: (i, 0)),
                   pl.BlockSpec((tm, tn), lambda i, j: (i, jnp.maximum(j - 1, 0)))],
        out_shape=[jax.ShapeDtypeStruct((t, PART_A), F32),
                   jax.ShapeDtypeStruct((t, n - PART_A), BF16)],
        scratch_shapes=[pltpu.VMEM((tm, d), BF16)],
        compiler_params=_cparams(("arbitrary", "arbitrary")),
        name="in_proj",
    )(xf, mod3, w4)


def _qkv_kernel(a_ref, wq_ref, wkv_ref, gq_ref, gkv_ref, cos_ref, sin_ref, q_ref, k_ref, v_ref):
    a = a_ref[...]
    cq = a[:, :MLA_Q_RANK]
    ckv = a[:, MLA_Q_RANK:MLA_Q_RANK + MLA_KV_RANK]
    kr = a[:, 768:896]
    krs = a[:, 896:1024]
    cqn = (cq * lax.rsqrt(jnp.mean(cq * cq, axis=-1, keepdims=True) + RMS_EPS) * gq_ref[...]).astype(BF16)
    ckvn = (ckv * lax.rsqrt(jnp.mean(ckv * ckv, axis=-1, keepdims=True) + RMS_EPS) * gkv_ref[...]).astype(BF16)
    q3 = jnp.dot(cqn, wq_ref[...], preferred_element_type=F32)
    kv = jnp.dot(ckvn, wkv_ref[...], preferred_element_type=F32)
    cos = cos_ref[...]
    sin = sin_ref[...]
    krr = (kr * cos + krs * sin).astype(BF16)
    for h in range(MLA_HEADS):
        q0 = h * 3 * LANES
        c0 = h * QK_PAD
        q_ref[:, c0:c0 + LANES] = q3[:, q0:q0 + LANES].astype(BF16)
        q_ref[:, c0 + LANES:c0 + QK_PAD] = (
            q3[:, q0 + LANES:q0 + 2 * LANES] * cos + q3[:, q0 + 2 * LANES:q0 + 3 * LANES] * sin).astype(BF16)
        k_ref[:, c0:c0 + LANES] = kv[:, c0:c0 + LANES].astype(BF16)
        k_ref[:, c0 + LANES:c0 + QK_PAD] = krr
        v_ref[:, c0:c0 + LANES] = kv[:, c0 + LANES:c0 + QK_PAD].astype(BF16)
        v_ref[:, c0 + LANES:c0 + V_PAD] = jnp.ones((a.shape[0], V_PAD - LANES), BF16)


def _qkv(part_a, wq3, wkv, gq, gkv, cos_t, sin_t, seq):
    t = part_a.shape[0]
    tm = 512
    per_b = seq // tm
    hq = MLA_HEADS * QK_PAD
    return pl.pallas_call(
        _qkv_kernel,
        grid=(t // tm,),
        in_specs=[pl.BlockSpec((tm, PART_A), lambda i: (i, 0)),
                  pl.BlockSpec(wq3.shape, lambda i: (0, 0)),
                  pl.BlockSpec(wkv.shape, lambda i: (0, 0)),
                  pl.BlockSpec((1, MLA_Q_RANK), lambda i: (0, 0)),
                  pl.BlockSpec((1, MLA_KV_RANK), lambda i: (0, 0)),
                  pl.BlockSpec((tm, LANES), lambda i: (i % per_b, 0)),
                  pl.BlockSpec((tm, LANES), lambda i: (i % per_b, 0))],
        out_specs=[pl.BlockSpec((tm, hq), lambda i: (i, 0)),
                   pl.BlockSpec((tm, hq), lambda i: (i, 0)),
                   pl.BlockSpec((tm, MLA_HEADS * V_PAD), lambda i: (i, 0))],
        out_shape=[jax.ShapeDtypeStruct((t, hq), BF16),
                   jax.ShapeDtypeStruct((t, hq), BF16),
                   jax.ShapeDtypeStruct((t, MLA_HEADS * V_PAD), BF16)],
        compiler_params=_cparams(("arbitrary",)),
        name="qkv",
    )(part_a, wq3, wkv, gq, gkv, cos_t, sin_t)


ATTN_TILE = 512


def _attn_kernel(*refs, c, masked):
    if masked:
        q_ref, k_ref, v_ref, bias_ref, o_ref, s_a, s_b, m_scr, acc_scr = refs
    else:
        q_ref, k_ref, v_ref, o_ref, s_a, s_b, m_scr, acc_scr = refs
    tile = ATTN_TILE
    qi = pl.program_id(2)

    def put_scores(buf, t):
        k = k_ref[pl.ds(pl.multiple_of(t * tile, tile), tile), :]
        buf[...] = _nt_dot(q_ref[...], k)

    def tile_step(buf, t, kind, nxt=None):
        if nxt is not None:
            put_scores(*nxt)
        s = buf[...]
        if masked:
            if kind == "prev":
                s = s + bias_ref[0, 1] + jnp.where(qi == 0, NEG, 0.0)
            elif kind == "diag":
                s = s + bias_ref[0, 0]
        elif kind == "diag":
            row = lax.broadcasted_iota(jnp.int32, (tile, tile), 0)
            col = lax.broadcasted_iota(jnp.int32, (tile, tile), 1)
            s = jnp.where(row >= col, s, NEG)
        m_old = m_scr[...]
        m_new = jnp.maximum(m_old, jnp.broadcast_to(jnp.max(s, axis=-1, keepdims=True), m_old.shape))
        alpha = jnp.exp2((m_old - m_new) * c)
        p = jnp.concatenate([jnp.exp2(((s[:, j * LANES:(j + 1) * LANES] - m_new) * c).astype(BF16))
                             for j in range(tile // LANES)], axis=1)
        v = v_ref[pl.ds(pl.multiple_of(t * tile, tile), tile), :]
        pv = jnp.dot(p, v, preferred_element_type=F32)
        acc_scr[...] = jnp.concatenate([alpha, alpha], axis=1) * acc_scr[...] + pv
        m_scr[...] = m_new

    m_scr[...] = jnp.full(m_scr.shape, NEG, F32)
    acc_scr[...] = jnp.zeros(acc_scr.shape, F32)
    put_scores(s_a, 0)

    n_far = jnp.maximum(qi - 1, 0) if masked else qi
    pn = jnp.maximum(qi - 1, 0)

    def pair(j, _):
        t = 2 * j
        tile_step(s_a, t, "far", (s_b, t + 1))
        tile_step(s_b, t + 1, "far", (s_a, t + 2))
        return 0

    lax.fori_loop(0, n_far // 2, pair, 0)
    t0 = 2 * (n_far // 2)

    def tail(first, second):
        if masked:
            tile_step(first, pn, "prev", (second, qi))
            tile_step(second, qi, "diag")
        else:
            tile_step(first, qi, "diag")

    @pl.when(n_far % 2 == 1)
    def _():
        tile_step(s_a, t0, "far", (s_b, t0 + 1))
        tail(s_b, s_a)

    @pl.when(n_far % 2 == 0)
    def _():
        tail(s_a, s_b)

    o_ref[...] = (acc_scr[:, :LANES] / acc_scr[:, LANES:]).astype(o_ref.dtype)


def _attn_scratch():
    tile = ATTN_TILE
    return [pltpu.VMEM((tile, tile), F32), pltpu.VMEM((tile, tile), F32),
            pltpu.VMEM((tile, LANES), F32), pltpu.VMEM((tile, V_PAD), F32)]


def _mla_attention(q, k, v, batch, seq):
    tile = ATTN_TILE
    nq = seq // tile
    c = math.log2(math.e) / math.sqrt(MLA_NOPE_DIM + MLA_ROPE_DIM)
    return pl.pallas_call(
        functools.partial(_attn_kernel, c=c, masked=False),
        grid=(batch, MLA_HEADS, nq),
        in_specs=[pl.BlockSpec((tile, QK_PAD), lambda b, h, i: (b * nq + i, h)),
                  pl.BlockSpec((seq, QK_PAD), lambda b, h, i: (b, h)),
                  pl.BlockSpec((seq, V_PAD), lambda b, h, i: (b, h))],
        out_specs=pl.BlockSpec((tile, MLA_V_DIM), lambda b, h, i: (b * nq + i, h)),
        out_shape=jax.ShapeDtypeStruct((batch * seq, MLA_HEADS * MLA_V_DIM), BF16),
        scratch_shapes=_attn_scratch(),
        compiler_params=_cparams(("arbitrary", "arbitrary", "arbitrary")),
        name="mla_attn",
    )(q, k, v)


def _moba_select_kernel(q_ref, k_ref, v_ref, qa_ref, ka_ref, va_ref, *, seq, nb):
    kf = k_ref[...].astype(F32)
    km = jnp.sum(kf.reshape(nb, MOBA_BLOCK, MOBA_HEAD_DIM), axis=1) * (1.0 / MOBA_BLOCK)
    km_hi = km.astype(BF16)
    km_lo = (km - km_hi.astype(F32)).astype(BF16)
    q = q_ref[...]
    gate = _nt_dot(km_hi, q) + _nt_dot(km_lo, q)
    shift = MOBA_BLOCK.bit_length() - 1
    blk = lax.broadcasted_iota(jnp.int32, (nb, seq), 0)
    qblk = jnp.right_shift(lax.broadcasted_iota(jnp.int32, (nb, seq), 1), shift)
    g = jnp.where(blk < qblk, gate, NEG)
    visible = blk == qblk
    for _ in range(MOBA_TOPK):
        mx = jnp.max(g, axis=0, keepdims=True)
        first = jnp.min(jnp.where(g == mx, blk, nb), axis=0, keepdims=True)
        pick = (blk == first) & (mx > 0.5 * NEG)
        visible = visible | pick
        g = jnp.where(pick, NEG, g)
    mask_t = jnp.concatenate([jnp.where(visible, 0.0, NEG), jnp.zeros((LANES - nb, seq), F32)], axis=0)
    qa_ref[:, :MOBA_HEAD_DIM] = q
    qa_ref[:, MOBA_HEAD_DIM:] = mask_t.T.astype(BF16)
    lane = lax.broadcasted_iota(jnp.int32, (seq, LANES), 1)
    own = jnp.right_shift(lax.broadcasted_iota(jnp.int32, (seq, LANES), 0), shift)
    ka_ref[:, :MOBA_HEAD_DIM] = k_ref[...]
    ka_ref[:, MOBA_HEAD_DIM:] = jnp.where(lane == own, 1.0, 0.0).astype(BF16)
    va_ref[:, :MOBA_HEAD_DIM] = v_ref[...]
    va_ref[:, MOBA_HEAD_DIM:] = jnp.ones((seq, V_PAD - MOBA_HEAD_DIM), BF16)


def _moba_select(qkv_mo, batch, seq):
    nb = seq // MOBA_BLOCK
    assert MOBA_HEAD_DIM == LANES and nb <= QK_PAD - MOBA_HEAD_DIM
    wide = jax.ShapeDtypeStruct((batch * seq, MOBA_HEADS * QK_PAD), BF16)
    return pl.pallas_call(
        functools.partial(_moba_select_kernel, seq=seq, nb=nb),
        grid=(batch, MOBA_HEADS),
        in_specs=[pl.BlockSpec((seq, MOBA_HEAD_DIM), lambda b, h: (b, h)),
                  pl.BlockSpec((seq, MOBA_HEAD_DIM), lambda b, h: (b, MOBA_HEADS + h)),
                  pl.BlockSpec((seq, MOBA_HEAD_DIM), lambda b, h: (b, 2 * MOBA_HEADS + h))],
        out_specs=[pl.BlockSpec((seq, QK_PAD), lambda b, h: (b, h)),
                   pl.BlockSpec((seq, QK_PAD), lambda b, h: (b, h)),
                   pl.BlockSpec((seq, V_PAD), lambda b, h: (b, h))],
        out_shape=[wide, wide, jax.ShapeDtypeStruct((batch * seq, MOBA_HEADS * V_PAD), BF16)],
        compiler_params=_cparams(("arbitrary", "arbitrary")),
        name="moba_select",
    )(qkv_mo, qkv_mo, qkv_mo)


def _t5_kernel(tab_ref, o_ref, *, inv_scale):
    h = pl.program_id(0)
    r = lax.broadcasted_iota(jnp.int32, (LANES, LANES), 0)
    c = lax.broadcasted_iota(jnp.int32, (LANES, LANES), 1)
    max_exact = T5_BUCKETS // 2
    far = tab_ref[T5_BUCKETS - 1, h]

    def block(offset):
        rel = offset + r - c
        n = jnp.maximum(rel, 0)
        nf = jnp.maximum(n, 1).astype(F32)
        large = max_exact + (jnp.log(nf / max_exact) / math.log(T5_MAX_DISTANCE / max_exact)
                             * (T5_BUCKETS - max_exact)).astype(jnp.int32)
        large = jnp.minimum(large, T5_BUCKETS - 1)
        bucket = jnp.where(n < max_exact, n, large)
        bias = jnp.zeros((LANES, LANES), F32)
        for j in range(T5_BUCKETS):
            bias = jnp.where(bucket == j, tab_ref[j, h], bias)
        return jnp.where(rel >= 0, (bias - far) * inv_scale, NEG)

    near = {0: block(0), 1: block(LANES)}
    nblk = ATTN_TILE // LANES
    for d in range(2):
        for i in range(nblk):
            for j in range(nblk):
                k = d * nblk + i - j
                if k < 0:
                    val = jnp.full((LANES, LANES), NEG, F32)
                else:
                    val = near.get(k, jnp.zeros((LANES, LANES), F32))
                o_ref[0, d, i * LANES:(i + 1) * LANES, j * LANES:(j + 1) * LANES] = val


def _t5_tiles(t5_table):
    assert LANES >= T5_MAX_DISTANCE
    tile = ATTN_TILE
    return pl.pallas_call(
        functools.partial(_t5_kernel, inv_scale=math.sqrt(MOBA_HEAD_DIM)),
        grid=(MOBA_HEADS,),
        in_specs=[pl.BlockSpec(memory_space=pltpu.SMEM)],
        out_specs=pl.BlockSpec((1, 2, tile, tile), lambda h: (h, 0, 0, 0)),
        out_shape=jax.ShapeDtypeStruct((MOBA_HEADS, 2, tile, tile), F32),
        compiler_params=_cparams(("arbitrary",)),
        name="t5_tiles",
    )(t5_table)


def _moba_attention(q_wide, k_wide, v_wide, bias, batch, seq):
    tile = ATTN_TILE
    nq = seq // tile
    c = math.log2(math.e) / math.sqrt(MOBA_HEAD_DIM)
    dh = MOBA_HEAD_DIM
    return pl.pallas_call(
        functools.partial(_attn_kernel, c=c, masked=True),
        grid=(batch, MOBA_HEADS, nq),
        in_specs=[pl.BlockSpec((tile, QK_PAD), lambda b, h, i: (b * nq + i, h)),
                  pl.BlockSpec((seq, QK_PAD), lambda b, h, i: (b, h)),
                  pl.BlockSpec((seq, V_PAD), lambda b, h, i: (b, h)),
                  pl.BlockSpec((1, 2, tile, tile), lambda b, h, i: (h, 0, 0, 0))],
        out_specs=pl.BlockSpec((tile, dh), lambda b, h, i: (b * nq + i, h)),
        out_shape=jax.ShapeDtypeStruct((batch * seq, MOBA_WIDTH), BF16),
        scratch_shapes=_attn_scratch(),
        compiler_params=_cparams(("arbitrary", "arbitrary", "arbitrary")),
        name="moba_attn",
    )(q_wide, k_wide, v_wide, bias)


def _outproj_kernel(oa_ref, ob_ref, wa_ref, wb_ref, x_ref, mod_ref, g_ref, b_ref, wr_ref,
                    x1_ref, h2_ref, lg_ref, y_a, y_b, *, alpha, n_tiles):
    i = pl.program_id(0)
    tm, d = y_a.shape
    chunks = 4
    cn, cr = d // chunks, tm // chunks

    def matmul_into(y_ref, c):
        cols = slice(c * cn, (c + 1) * cn)
        y_ref[:, cols] = (jnp.dot(oa_ref[...], wa_ref[:, cols], preferred_element_type=F32)
                          + jnp.dot(ob_ref[...], wb_ref[:, cols], preferred_element_type=F32))

    def epilogue(y_ref, c):
        rows = slice(c * cr, (c + 1) * cr)
        z = alpha * x_ref[rows, :] + mod_ref[0, 2:3, :] * y_ref[rows, :]
        x1 = _ln(z) * g_ref[...] + b_ref[...]
        x1_ref[rows, :] = x1
        h2 = _ln(x1) * (1.0 + mod_ref[0, 4:5, :]) + mod_ref[0, 3:4, :]
        h2_ref[rows, :] = h2
        h_hi = h2.astype(BF16)
        h_lo = (h2 - h_hi.astype(F32)).astype(BF16)
        zz = (jnp.dot(h_hi, wr_ref[...], preferred_element_type=F32)
              + jnp.dot(h_lo, wr_ref[...], preferred_element_type=F32))
        lg_ref[rows, :] = zz + pltpu.roll(zz, LANES // 2, 1)

    def step(y_new, y_old):
        for c in range(chunks):
            if y_new is not None:
                matmul_into(y_new, c)
            if y_old is not None:
                epilogue(y_old, c)

    inner = (i > 0) & (i < n_tiles)
    pl.when(i == 0)(lambda: step(y_a, None))
    pl.when(inner & (i % 2 == 0))(lambda: step(y_a, y_b))
    pl.when(inner & (i % 2 == 1))(lambda: step(y_b, y_a))
    pl.when(i == n_tiles)(lambda: step(None, y_b if n_tiles % 2 == 0 else y_a))


def _out_proj(o_mla, o_moba, wa, wb, xf, mod3, ln_g, ln_b, wr, seq, alpha):
    t, d = xf.shape
    tm = 512
    n = t // tm
    per_b = seq // tm
    ka, kb = wa.shape[0], wb.shape[0]
    once = pl.Buffered(1)

    def cur(i):
        return (jnp.minimum(i, n - 1), 0)

    def lag(i):
        return (jnp.maximum(i - 1, 0), 0)

    return pl.pallas_call(
        functools.partial(_outproj_kernel, alpha=alpha, n_tiles=n),
        grid=(n + 1,),
        in_specs=[pl.BlockSpec((tm, ka), cur),
                  pl.BlockSpec((tm, kb), cur),
                  pl.BlockSpec((ka, d), lambda i: (0, 0), pipeline_mode=once),
                  pl.BlockSpec((kb, d), lambda i: (0, 0), pipeline_mode=once),
                  pl.BlockSpec((tm, d), lag),
                  pl.BlockSpec((1, 6, d), lambda i: (jnp.maximum(i - 1, 0) // per_b, 0, 0)),
                  pl.BlockSpec((1, d), lambda i: (0, 0)),
                  pl.BlockSpec((1, d), lambda i: (0, 0)),
                  pl.BlockSpec((d, LANES), lambda i: (0, 0), pipeline_mode=once)],
        out_specs=[pl.BlockSpec((tm, d), lag),
                   pl.BlockSpec((tm, d), lag),
                   pl.BlockSpec((tm, LANES), lag)],
        out_shape=[jax.ShapeDtypeStruct((t, d), F32),
                   jax.ShapeDtypeStruct((t, d), F32),
                   jax.ShapeDtypeStruct((t, LANES), F32)],
        scratch_shapes=[pltpu.VMEM((tm, d), F32), pltpu.VMEM((tm, d), F32)],
        compiler_params=_cparams(("arbitrary",)),
        name="out_proj",
    )(o_mla, o_moba, wa, wb, xf, mod3, ln_g, ln_b, wr)


def _route_kernel(lg_ref, br_ref, info_ref, cnt_ref, run_scr, *, tm):
    i = pl.program_id(0)

    @pl.when(i == 0)
    def _():
        run_scr[...] = jnp.zeros_like(run_scr)

    lg = lg_ref[...] + br_ref[...]
    lane = lax.broadcasted_iota(jnp.int32, (tm, LANES), 1)
    e_lo, e_hi = MOE_GROUPS, MOE_GROUPS + MOE_N_EXPERTS
    is_g = lane < e_lo
    gl = jnp.where(is_g, lg, NEG)
    gmax = jnp.max(gl, axis=-1, keepdims=True)
    gidx = jnp.min(jnp.where(gl == gmax, lane, LANES), axis=-1, keepdims=True)
    g_p = 1.0 / jnp.sum(jnp.where(is_g, jnp.exp(gl - gmax), 0.0), axis=-1, keepdims=True)
    grp_of_lane = jnp.right_shift(lane - e_lo, MOE_EXPERTS_PER_GROUP.bit_length() - 1)
    in_grp = (lane >= e_lo) & (lane < e_hi) & (grp_of_lane == gidx)
    el = jnp.where(in_grp, lg, NEG)
    m1 = jnp.max(el, axis=-1, keepdims=True)
    l1 = jnp.min(jnp.where(el == m1, lane, LANES), axis=-1, keepdims=True)
    el2 = jnp.where(lane == l1, NEG, el)
    m2 = jnp.max(el2, axis=-1, keepdims=True)
    l2 = jnp.min(jnp.where(el2 == m2, lane, LANES), axis=-1, keepdims=True)
    zsum = jnp.sum(jnp.where(in_grp, jnp.exp(el - m1), 0.0), axis=-1, keepdims=True)
    p1 = 1.0 / zsum
    p2 = jnp.exp(m2 - m1) / zsum
    wa = g_p * (p1 / (p1 + p2))
    wb = g_p * (p2 / (p1 + p2))
    hot_a = lane == l1
    hot_b = lane == l2
    onehot = jnp.where(hot_a | hot_b, 1.0, 0.0)
    r = lax.broadcasted_iota(jnp.int32, (tm, tm), 0)
    c = lax.broadcasted_iota(jnp.int32, (tm, tm), 1)
    lower = jnp.where(c < r, 1.0, 0.0).astype(BF16)
    before = jnp.dot(lower, onehot.astype(BF16), preferred_element_type=F32) + run_scr[...]
    rank_a = jnp.sum(jnp.where(hot_a, before, 0.0), axis=-1, keepdims=True)
    rank_b = jnp.sum(jnp.where(hot_b, before, 0.0), axis=-1, keepdims=True)
    run_scr[...] += jnp.sum(onehot, axis=0, keepdims=True)
    info = jnp.zeros((tm, LANES), F32)
    for k, val in enumerate([(l1 - e_lo).astype(F32), (l2 - e_lo).astype(F32), wa, wb, rank_a, rank_b]):
        info = jnp.where(lane == k, val, info)
    info_ref[...] = info
    cnt_ref[...] = run_scr[...]


def _route(logits, br):
    t = logits.shape[0]
    tm = 512
    return pl.pallas_call(
        functools.partial(_route_kernel, tm=tm),
        grid=(t // tm,),
        in_specs=[pl.BlockSpec((tm, LANES), lambda i: (i, 0)),
                  pl.BlockSpec((1, LANES), lambda i: (0, 0))],
        out_specs=[pl.BlockSpec((tm, LANES), lambda i: (i, 0)),
                   pl.BlockSpec((1, LANES), lambda i: (0, 0))],
        out_shape=[jax.ShapeDtypeStruct((t, LANES), F32),
                   jax.ShapeDtypeStruct((1, LANES), F32)],
        scratch_shapes=[pltpu.VMEM((1, LANES), F32)],
        compiler_params=_cparams(("arbitrary",)),
        name="route",
    )(logits, br)


def _pos_kernel(info_ref, start_ref, pos_ref):
    info = info_ref[...]
    tm = info.shape[0]
    lane = lax.broadcasted_iota(jnp.int32, (tm, LANES), 1)
    start = start_ref[...]
    cols = []
    for k in range(2):
        e = jnp.sum(jnp.where(lane == k, info, 0.0), axis=-1, keepdims=True).astype(jnp.int32)
        rank = jnp.sum(jnp.where(lane == 4 + k, info, 0.0), axis=-1, keepdims=True)
        base = jnp.sum(jnp.where(lane == e + MOE_GROUPS, start, 0.0), axis=-1, keepdims=True)
        cols.append(base + rank)
    wide = jnp.where(lane == 0, cols[0], jnp.where(lane == 1, cols[1], 0.0))
    pos_ref[...] = wide.T[:pos_ref.shape[0], :].astype(jnp.int32)


def _positions(info, start_lanes):
    t = info.shape[0]
    tm = 1024
    return pl.pallas_call(
        _pos_kernel,
        grid=(t // tm,),
        in_specs=[pl.BlockSpec((tm, LANES), lambda i: (i, 0)),
                  pl.BlockSpec((1, LANES), lambda i: (0, 0))],
        out_specs=pl.BlockSpec((8, tm), lambda i: (0, i)),
        out_shape=jax.ShapeDtypeStruct((8, t), jnp.int32),
        compiler_params=_cparams(("arbitrary",)),
        name="positions",
    )(info, start_lanes)


def _row_copy(src_ref, src_row, dst_ref, dst_row, sem):
    return pltpu.make_async_copy(src_ref.at[pl.ds(src_row, 1)], dst_ref.at[pl.ds(dst_row, 1)], sem)


def _dispatch_kernel(pa_ref, pb_ref, pad0_ref, padn_ref, used_ref, h_ref, xs_ref, ztile, sem, zsem,
                     *, tm, tr, n_tiles):
    i = pl.program_id(0)
    base = i * tm

    def zero_fill(act):
        def whole_tile(j, _):
            act(pltpu.make_async_copy(ztile, xs_ref.at[pl.ds(pl.multiple_of(j * tr, tr), tr)], zsem))
            return 0

        lax.fori_loop(used_ref[0], n_tiles, whole_tile, 0)

        def expert_pad(e, _):
            def pad_row(r, _):
                act(_row_copy(ztile, 0, xs_ref, pad0_ref[e] + r, zsem))
                return 0

            lax.fori_loop(0, padn_ref[e], pad_row, 0)
            return 0

        lax.fori_loop(0, MOE_N_EXPERTS, expert_pad, 0)

    @pl.when(i == 0)
    def _():
        ztile[...] = jnp.zeros(ztile.shape, ztile.dtype)
        zero_fill(lambda cp: cp.start())

    def issue(t, _):
        _row_copy(h_ref, t, xs_ref, pa_ref[base + t], sem).start()
        _row_copy(h_ref, t, xs_ref, pb_ref[base + t], sem).start(priority=1)
        return 0

    lax.fori_loop(0, tm, issue, 0, unroll=8)
    for _ in range(2):
        pltpu.make_async_copy(h_ref, xs_ref.at[pl.ds(0, tm)], sem).wait()

    @pl.when(i == pl.num_programs(0) - 1)
    def _():
        zero_fill(lambda cp: cp.wait())


def _dispatch(pos_a, pos_b, pad0, padn, used, h2, tr, n_tiles):
    t, d = h2.shape
    tm = 256
    grid_spec = pltpu.PrefetchScalarGridSpec(
        num_scalar_prefetch=5,
        grid=(t // tm,),
        in_specs=[pl.BlockSpec((tm, d), lambda i, *_: (i, 0))],
        out_specs=pl.BlockSpec(memory_space=pl.ANY),
        scratch_shapes=[pltpu.VMEM((tr, d), F32), pltpu.SemaphoreType.DMA(()), pltpu.SemaphoreType.DMA(())],
    )
    return pl.pallas_call(
        functools.partial(_dispatch_kernel, tm=tm, tr=tr, n_tiles=n_tiles),
        grid_spec=grid_spec,
        out_shape=jax.ShapeDtypeStruct((n_tiles * tr, d), F32),
        compiler_params=_cparams(("arbitrary",), row_dma=True),
        name="dispatch",
    )(pos_a, pos_b, pad0, padn, used, h2)


def _experts_kernel(texp_ref, tidx_ref, nexte_ref, used_ref, x_ref, w1_hbm, w3_hbm, w2_hbm, o_ref,
                    w1_f32, w3_f32, w2_f32, w1_scr, w3_scr, w2_scr, sems):
    i = pl.program_id(0)
    prev = texp_ref[jnp.maximum(i - 1, 0)]

    def weight_copies(e):
        return [pltpu.make_async_copy(w1_hbm.at[e], w1_f32, sems.at[0]),
                pltpu.make_async_copy(w3_hbm.at[e], w3_f32, sems.at[1]),
                pltpu.make_async_copy(w2_hbm.at[e], w2_f32, sems.at[2])]

    @pl.when(i == 0)
    def _():
        for cp in weight_copies(texp_ref[0]):
            cp.start()

    @pl.when((i == 0) | (texp_ref[i] != prev))
    def _():
        for cp in weight_copies(texp_ref[i]):
            cp.wait()
        w1_scr[...] = w1_f32[...].astype(BF16)
        w3_scr[...] = w3_f32[...].astype(BF16)
        w2_scr[...] = w2_f32[...].astype(BF16)

        @pl.when(nexte_ref[i] >= 0)
        def _():
            for cp in weight_copies(nexte_ref[i]):
                cp.start()

    @pl.when(i < used_ref[0])
    def _():
        x = x_ref[...].astype(BF16)
        a = jnp.dot(x, w1_scr[...], preferred_element_type=F32)
        b = jnp.dot(x, w3_scr[...], preferred_element_type=F32)
        hid = (a / (1.0 + jnp.exp(-a))) * b
        o_ref[...] = jnp.dot(hid.astype(BF16), w2_scr[...], preferred_element_type=F32)

    @pl.when(i >= used_ref[0])
    def _():
        o_ref[...] = jnp.zeros_like(o_ref)


def _experts(texp, tidx, nexte, used, xs, w1, w3, w2, tr):
    rows, d = xs.shape
    nt = rows // tr
    f = w1.shape[2]
    grid_spec = pltpu.PrefetchScalarGridSpec(
        num_scalar_prefetch=4,
        grid=(nt,),
        in_specs=[pl.BlockSpec((tr, d), lambda i, te, ti, ne, u: (ti[i], 0)),
                  pl.BlockSpec(memory_space=pl.ANY),
                  pl.BlockSpec(memory_space=pl.ANY),
                  pl.BlockSpec(memory_space=pl.ANY)],
        out_specs=pl.BlockSpec((tr, d), lambda i, te, ti, ne, u: (i, 0)),
        scratch_shapes=[pltpu.VMEM((d, f), F32), pltpu.VMEM((d, f), F32), pltpu.VMEM((f, d), F32),
                        pltpu.VMEM((d, f), BF16), pltpu.VMEM((d, f), BF16), pltpu.VMEM((f, d), BF16),
                        pltpu.SemaphoreType.DMA((3,))],
    )
    return pl.pallas_call(
        _experts_kernel,
        grid_spec=grid_spec,
        out_shape=jax.ShapeDtypeStruct((rows, d), F32),
        compiler_params=_cparams(("arbitrary",)),
        name="experts",
    )(texp, tidx, nexte, used, xs, w1, w3, w2)


def _combine_kernel(pa_ref, pb_ref, ys_ref, x1_ref, info_ref, mod_ref, g_ref, b_ref, o_ref,
                    buf_a, buf_b, sems, *, tm, alpha):
    i = pl.program_id(0)

    def gather(tile, slot):
        base = tile * tm

        def issue(t, _):
            _row_copy(ys_ref, pa_ref[base + t], buf_a.at[slot], t, sems.at[slot]).start()
            _row_copy(ys_ref, pb_ref[base + t], buf_b.at[slot], t, sems.at[slot]).start(priority=1)
            return 0

        lax.fori_loop(0, tm, issue, 0, unroll=8)

    @pl.when(i == 0)
    def _():
        gather(0, 0)

    @pl.when(i + 1 < pl.num_programs(0))
    def _():
        gather(i + 1, (i + 1) % 2)

    slot = i % 2
    for buf in (buf_a, buf_b):
        pltpu.make_async_copy(ys_ref.at[pl.ds(0, tm)], buf.at[slot], sems.at[slot]).wait()
    info = info_ref[...]
    y = info[:, 2:3] * buf_a[slot] + info[:, 3:4] * buf_b[slot]
    z = alpha * x1_ref[...] + mod_ref[0, 5:6, :] * y
    o_ref[...] = _ln(z) * g_ref[...] + b_ref[...]


def _combine(pos_a, pos_b, ys, x1, info, mod3, ln_g, ln_b, seq, alpha):
    t, d = x1.shape
    tm = 256
    per_b = seq // tm
    grid_spec = pltpu.PrefetchScalarGridSpec(
        num_scalar_prefetch=2,
        grid=(t // tm,),
        in_specs=[pl.BlockSpec(memory_space=pl.ANY),
                  pl.BlockSpec((tm, d), lambda i, pa, pb: (i, 0)),
                  pl.BlockSpec((tm, LANES), lambda i, pa, pb: (i, 0)),
                  pl.BlockSpec((1, 6, d), lambda i, pa, pb: (i // per_b, 0, 0)),
                  pl.BlockSpec((1, d), lambda i, pa, pb: (0, 0)),
                  pl.BlockSpec((1, d), lambda i, pa, pb: (0, 0))],
        out_specs=pl.BlockSpec((tm, d), lambda i, pa, pb: (i, 0)),
        scratch_shapes=[pltpu.VMEM((2, tm, d), F32), pltpu.VMEM((2, tm, d), F32), pltpu.SemaphoreType.DMA((2,))],
    )
    return pl.pallas_call(
        functools.partial(_combine_kernel, tm=tm, alpha=alpha),
        grid_spec=grid_spec,
        out_shape=jax.ShapeDtypeStruct((t, d), F32),
        compiler_params=_cparams(("arbitrary",), row_dma=True),
        name="combine",
    )(pos_a, pos_b, ys, x1, info, mod3, ln_g, ln_b)


def _prep_w_in(w):
    cq = w[:, :MLA_Q_RANK]
    ckv = w[:, MLA_Q_RANK:MLA_Q_RANK + MLA_KV_RANK]
    r0 = MLA_Q_RANK + MLA_KV_RANK
    kr = w[:, r0:r0 + MLA_ROPE_DIM]
    mo = w[:, r0 + MLA_ROPE_DIM:]
    half = MLA_ROPE_DIM // 2
    z = jnp.zeros((w.shape[0], LANES - MLA_ROPE_DIM), w.dtype)
    return jnp.concatenate([cq, ckv, kr, z, kr[:, half:], kr[:, :half], z, mo], axis=1).astype(BF16)


def _prep_w_uq(w):
    r = w.shape[0]
    w = w.reshape(r, MLA_HEADS, MLA_NOPE_DIM + MLA_ROPE_DIM)
    half = MLA_ROPE_DIM // 2
    nope = w[:, :, :MLA_NOPE_DIM]
    x1 = w[:, :, MLA_NOPE_DIM:MLA_NOPE_DIM + half]
    x2 = w[:, :, MLA_NOPE_DIM + half:]
    z = jnp.zeros((r, MLA_HEADS, LANES - MLA_ROPE_DIM), w.dtype)
    return jnp.concatenate([nope, x1, x2, z, x2, x1, z], axis=2).reshape(r, MLA_HEADS * 3 * LANES).astype(BF16)


def _rope_lanes(seq):
    inv = 1.0 / (ROPE_THETA ** (jnp.arange(0, MLA_ROPE_DIM, 2, dtype=F32) / MLA_ROPE_DIM))
    ang = jnp.arange(seq, dtype=F32)[:, None] * inv[None, :]
    cos, sin = jnp.cos(ang), jnp.sin(ang)
    z = jnp.zeros((seq, LANES - MLA_ROPE_DIM), F32)
    return jnp.concatenate([cos, cos, z], axis=1), jnp.concatenate([-sin, sin, z], axis=1)


def _prep_router(w_rg, b_rg, w_re, b_re):
    d = w_rg.shape[0]
    w = jnp.concatenate([w_rg, w_re], axis=1)
    n = w.shape[1]
    hi = w.astype(BF16)
    lo = (w - hi.astype(F32)).astype(BF16)
    z = jnp.zeros((d, LANES // 2 - n), BF16)
    wr = jnp.concatenate([hi, z, lo, z], axis=1)
    br = jnp.zeros((1, LANES), F32).at[0, :n].set(jnp.concatenate([b_rg, b_re]))
    return wr, br


def _layer(xf, mod3, batch, seq, depth_alpha, w_in, q_norm_g, w_uq, kv_norm_g, w_ukv, w_out, bias_tiles,
           cos_t, sin_t, ln1_g, ln1_b, w_rg, b_rg, w_re, b_re, w1, w3, w2, ln2_g, ln2_b):
    t, d = xf.shape
    part_a, qkv_mo = _in_proj(xf, mod3, _prep_w_in(w_in), seq)
    q, k, v = _qkv(part_a, _prep_w_uq(w_uq), w_ukv.astype(BF16), q_norm_g.reshape(1, -1),
                   kv_norm_g.reshape(1, -1), cos_t, sin_t, seq)
    o_mla = _mla_attention(q, k, v, batch, seq)
    q_wide, k_wide, v_wide = _moba_select(qkv_mo, batch, seq)
    o_moba = _moba_attention(q_wide, k_wide, v_wide, bias_tiles, batch, seq)
    wo = w_out.astype(BF16)
    wr, br = _prep_router(w_rg, b_rg, w_re, b_re)
    n_mla = MLA_HEADS * MLA_V_DIM
    x1, h2, logits = _out_proj(o_mla, o_moba, wo[:n_mla], wo[n_mla:], xf, mod3, ln1_g.reshape(1, d),
                               ln1_b.reshape(1, d), wr, seq, depth_alpha)
    info, counts = _route(logits, br)
    tr = 256
    nt = (2 * t) // tr + MOE_N_EXPERTS
    cnt = counts[0, MOE_GROUPS:MOE_GROUPS + MOE_N_EXPERTS].astype(jnp.int32)
    ntile = (cnt + tr - 1) // tr
    tile_end = jnp.cumsum(ntile)
    tile_start = tile_end - ntile
    used = tile_end[-1]
    start_lanes = jnp.zeros((1, LANES), F32).at[0, MOE_GROUPS:MOE_GROUPS + MOE_N_EXPERTS].set(
        (tile_start * tr).astype(F32))
    pos = _positions(info, start_lanes)
    pos_a, pos_b = pos[0], pos[1]
    tidx = jnp.minimum(jnp.arange(nt, dtype=jnp.int32), used - 1)
    texp = jnp.sum(tidx[:, None] >= tile_end[None, :], axis=1).astype(jnp.int32)
    eids = jnp.arange(MOE_N_EXPERTS, dtype=jnp.int32)
    later = (eids[None, :] > eids[:, None]) & (ntile[None, :] > 0)
    next_nonempty = jnp.where(later.any(axis=1), jnp.argmax(later, axis=1), -1).astype(jnp.int32)
    nexte = jnp.sum(jnp.where(texp[:, None] == eids[None, :], next_nonempty[None, :], 0), axis=1).astype(jnp.int32)
    used1 = used.reshape(1).astype(jnp.int32)
    pad0 = (tile_start * tr + cnt).astype(jnp.int32)
    padn = (ntile * tr - cnt).astype(jnp.int32)
    xs = _dispatch(pos_a, pos_b, pad0, padn, used1, h2, tr, nt)
    ys = _experts(texp, tidx, nexte, used1, xs, w1, w3, w2, tr)
    return _combine(pos_a, pos_b, ys, x1, info, mod3, ln2_g.reshape(1, d), ln2_b.reshape(1, d), seq, depth_alpha)


def kernel(x, c, w_ada, b_ada, w_in, q_norm_g, w_uq, kv_norm_g, w_ukv, w_out, t5_table, ln1_g, ln1_b,
           w_router_group, b_router_group, w_router_expert, b_router_expert, w1, w3, w2, ln2_g, ln2_b):
    batch, seq, d = x.shape
    depth = w_ada.shape[0]
    alpha = (2.0 * depth) ** 0.25
    cos_t, sin_t = _rope_lanes(seq)
    bias_tiles = _t5_tiles(t5_table)
    xf = x.reshape(batch * seq, d)
    for l in range(depth):
        mod3 = _ada_mod(c, w_ada[l], b_ada[l]).reshape(batch, 6, d)
        xf = _layer(xf, mod3, batch, seq, alpha, w_in[l], q_norm_g[l], w_uq[l], kv_norm_g[l], w_ukv[l],
                    w_out[l], bias_tiles, cos_t, sin_t, ln1_g[l], ln1_b[l], w_router_group[l],
                    b_router_group[l], w_router_expert[l], b_router_expert[l], w1[l], w3[l], w2[l],
                    ln2_g[l], ln2_b[l])
    return xf.reshape(batch, seq, d)
```

```python
import functools
import math

import jax
import jax.numpy as jnp
from jax import lax
from jax.experimental import pallas as pl
from jax.experimental.pallas import tpu as pltpu

D_MODEL = 2048
MLA_HEADS = 8
MLA_Q_RANK = 512
MLA_KV_RANK = 256
MLA_NOPE_DIM = 128
MLA_ROPE_DIM = 64
MLA_V_DIM = 128
ROPE_THETA = 10000.0
MOBA_HEADS = 8
MOBA_HEAD_DIM = 128
MOBA_BLOCK = 256
MOBA_TOPK = 3
T5_BUCKETS = 32
T5_MAX_DISTANCE = 128
MOE_GROUPS = 4
MOE_EXPERTS_PER_GROUP = 8
MOE_N_EXPERTS = MOE_GROUPS * MOE_EXPERTS_PER_GROUP
MOE_D_FF = 512
LN_EPS = 1e-5
RMS_EPS = 1e-6
MOBA_WIDTH = MOBA_HEADS * MOBA_HEAD_DIM

LANES = 128
SUBLANES = 8
QK_PAD = 256
V_PAD = 256
PART_A = 1024
NEG = -1e30
VMEM_LIMIT = 56 * 1024 * 1024

F32 = jnp.float32
BF16 = jnp.bfloat16


def _cparams(sem, row_dma=False):
    return pltpu.CompilerParams(dimension_semantics=sem, vmem_limit_bytes=VMEM_LIMIT,
                                disable_bounds_checks=row_dma)


def _ln(x):
    mu = jnp.mean(x, axis=-1, keepdims=True)
    xc = x - mu
    var = jnp.mean(xc * xc, axis=-1, keepdims=True)
    return xc * lax.rsqrt(var + LN_EPS)


def _nt_dot(a, b):
    return lax.dot_general(a, b, (((1,), (1,)), ((), ())), preferred_element_type=F32)


def _ada_kernel(ct_ref, w_ref, b_ref, o_ref, *, batch):
    ct = ct_ref[...]
    ca = ct / (1.0 + jnp.exp(-ct))
    w = w_ref[...]
    rows = [jnp.sum(w * ca[:, b:b + 1], axis=0, keepdims=True) for b in range(batch)]
    o_ref[...] = jnp.concatenate(rows, axis=0) + b_ref[...]


def _ada_mod(c, w_ada, b_ada):
    batch, d = c.shape
    n = w_ada.shape[1]
    tn = 512
    ct = jnp.zeros((d, LANES), F32).at[:, :batch].set(c.T)
    return pl.pallas_call(
        functools.partial(_ada_kernel, batch=batch),
        grid=(n // tn,),
        in_specs=[pl.BlockSpec((d, LANES), lambda j: (0, 0)),
                  pl.BlockSpec((d, tn), lambda j: (0, j)),
                  pl.BlockSpec((1, tn), lambda j: (0, j))],
        out_specs=pl.BlockSpec((batch, tn), lambda j: (0, j)),
        out_shape=jax.ShapeDtypeStruct((batch, n), F32),
        compiler_params=_cparams(("arbitrary",)),
        name="ada_mod",
    )(ct, w_ada, b_ada.reshape(1, n))


def _inproj_kernel(x_ref, mod_ref, w_ref, a_ref, b_ref, h_scr):
    j = pl.program_id(1)

    @pl.when(j == 0)
    def _():
        h = _ln(x_ref[...]) * (1.0 + mod_ref[0, 1:2, :]) + mod_ref[0, 0:1, :]
        h_scr[...] = h.astype(BF16)
        a_ref[...] = jnp.dot(h_scr[...], w_ref[...], preferred_element_type=F32)

    @pl.when(j > 0)
    def _():
        b_ref[...] = jnp.dot(h_scr[...], w_ref[...], preferred_element_type=F32).astype(BF16)


def _in_proj(xf, mod3, w4, seq):
    t, d = xf.shape
    n = w4.shape[1]
    tm, tn = 512, PART_A
    per_b = seq // tm
    return pl.pallas_call(
        _inproj_kernel,
        grid=(t // tm, n // tn),
        in_specs=[pl.BlockSpec((tm, d), lambda i, j: (i, 0)),
                  pl.BlockSpec((1, 6, d), lambda i, j: (i // per_b, 0, 0)),
                  pl.BlockSpec((d, tn), lambda i, j: (0, j))],
        out_specs=[pl.BlockSpec((tm, tn), lambda i, j: (i, 0)),
                   pl.BlockSpec((tm, tn), lambda i, j: (i, jnp.maximum(j - 1, 0)))],
        out_shape=[jax.ShapeDtypeStruct((t, PART_A), F32),
                   jax.ShapeDtypeStruct((t, n - PART_A), BF16)],
        scratch_shapes=[pltpu.VMEM((tm, d), BF16)],
        compiler_params=_cparams(("arbitrary", "arbitrary")),
        name="in_proj",
    )(xf, mod3, w4)


def _qkv_kernel(a_ref, wq_ref, wkv_ref, gq_ref, gkv_ref, cos_ref, sin_ref, q_ref, k_ref, v_ref):
    a = a_ref[...]
    cq = a[:, :MLA_Q_RANK]
    ckv = a[:, MLA_Q_RANK:MLA_Q_RANK + MLA_KV_RANK]
    kr = a[:, 768:896]
    krs = a[:, 896:1024]
    cqn = (cq * lax.rsqrt(jnp.mean(cq * cq, axis=-1, keepdims=True) + RMS_EPS) * gq_ref[...]).astype(BF16)
    ckvn = (ckv * lax.rsqrt(jnp.mean(ckv * ckv, axis=-1, keepdims=True) + RMS_EPS) * gkv_ref[...]).astype(BF16)
    q3 = jnp.dot(cqn, wq_ref[...], preferred_element_type=F32)
    kv = jnp.dot(ckvn, wkv_ref[...], preferred_element_type=F32)
    cos = cos_ref[...]
    sin = sin_ref[...]
    krr = (kr * cos + krs * sin).astype(BF16)
    for h in range(MLA_HEADS):
        q0 = h * 3 * LANES
        c0 = h * QK_PAD
        q_ref[:, c0:c0 + LANES] = q3[:, q0:q0 + LANES].astype(BF16)
        q_ref[:, c0 + LANES:c0 + QK_PAD] = (
            q3[:, q0 + LANES:q0 + 2 * LANES] * cos + q3[:, q0 + 2 * LANES:q0 + 3 * LANES] * sin).astype(BF16)
        k_ref[:, c0:c0 + LANES] = kv[:, c0:c0 + LANES].astype(BF16)
        k_ref[:, c0 + LANES:c0 + QK_PAD] = krr
        v_ref[:, c0:c0 + LANES] = kv[:, c0 + LANES:c0 + QK_PAD].astype(BF16)
        v_ref[:, c0 + LANES:c0 + V_PAD] = jnp.ones((a.shape[0], V_PAD - LANES), BF16)


def _qkv(part_a, wq3, wkv, gq, gkv, cos_t, sin_t, seq):
    t = part_a.shape[0]
    tm = 512
    per_b = seq // tm
    hq = MLA_HEADS * QK_PAD
    return pl.pallas_call(
        _qkv_kernel,
        grid=(t // tm,),
        in_specs=[pl.BlockSpec((tm, PART_A), lambda i: (i, 0)),
                  pl.BlockSpec(wq3.shape, lambda i: (0, 0)),
                  pl.BlockSpec(wkv.shape, lambda i: (0, 0)),
                  pl.BlockSpec((1, MLA_Q_RANK), lambda i: (0, 0)),
                  pl.BlockSpec((1, MLA_KV_RANK), lambda i: (0, 0)),
                  pl.BlockSpec((tm, LANES), lambda i: (i % per_b, 0)),
                  pl.BlockSpec((tm, LANES), lambda i: (i % per_b, 0))],
        out_specs=[pl.BlockSpec((tm, hq), lambda i: (i, 0)),
                   pl.BlockSpec((tm, hq), lambda i: (i, 0)),
                   pl.BlockSpec((tm, MLA_HEADS * V_PAD), lambda i: (i, 0))],
        out_shape=[jax.ShapeDtypeStruct((t, hq), BF16),
                   jax.ShapeDtypeStruct((t, hq), BF16),
                   jax.ShapeDtypeStruct((t, MLA_HEADS * V_PAD), BF16)],
        compiler_params=_cparams(("arbitrary",)),
        name="qkv",
    )(part_a, wq3, wkv, gq, gkv, cos_t, sin_t)


ATTN_TILE = 512


def _attn_kernel(*refs, c, masked):
    if masked:
        q_ref, k_ref, v_ref, bias_ref, o_ref, s_a, s_b, m_scr, acc_scr = refs
    else:
        q_ref, k_ref, v_ref, o_ref, s_a, s_b, m_scr, acc_scr = refs
    tile = ATTN_TILE
    qi = pl.program_id(2)

    def put_scores(buf, t):
        k = k_ref[pl.ds(pl.multiple_of(t * tile, tile), tile), :]
        buf[...] = _nt_dot(q_ref[...], k)

    def tile_step(buf, t, kind, nxt=None):
        if nxt is not None:
            put_scores(*nxt)
        s = buf[...]
        if masked:
            if kind == "prev":
                s = s + bias_ref[0, 1] + jnp.where(qi == 0, NEG, 0.0)
            elif kind == "diag":
                s = s + bias_ref[0, 0]
        elif kind == "diag":
            row = lax.broadcasted_iota(jnp.int32, (tile, tile), 0)
            col = lax.broadcasted_iota(jnp.int32, (tile, tile), 1)
            s = jnp.where(row >= col, s, NEG)
        m_old = m_scr[...]
        m_new = jnp.maximum(m_old, jnp.broadcast_to(jnp.max(s, axis=-1, keepdims=True), m_old.shape))
        alpha = jnp.exp2((m_old - m_new) * c)
        p = jnp.concatenate([jnp.exp2(((s[:, j * LANES:(j + 1) * LANES] - m_new) * c).astype(BF16))
                             for j in range(tile // LANES)], axis=1)
        v = v_ref[pl.ds(pl.multiple_of(t * tile, tile), tile), :]
        pv = jnp.dot(p, v, preferred_element_type=F32)
        acc_scr[...] = jnp.concatenate([alpha, alpha], axis=1) * acc_scr[...] + pv
        m_scr[...] = m_new

    m_scr[...] = jnp.full(m_scr.shape, NEG, F32)
    acc_scr[...] = jnp.zeros(acc_scr.shape, F32)
    put_scores(s_a, 0)

    n_far = jnp.maximum(qi - 1, 0) if masked else qi
    pn = jnp.maximum(qi - 1, 0)

    def pair(j, _):
        t = 2 * j
        tile_step(s_a, t, "far", (s_b, t + 1))
        tile_step(s_b, t + 1, "far", (s_a, t + 2))
        return 0

    lax.fori_loop(0, n_far // 2, pair, 0)
    t0 = 2 * (n_far // 2)

    def tail(first, second):
        if masked:
            tile_step(first, pn, "prev", (second, qi))
            tile_step(second, qi, "diag")
        else:
            tile_step(first, qi, "diag")

    @pl.when(n_far % 2 == 1)
    def _():
        tile_step(s_a, t0, "far", (s_b, t0 + 1))
        tail(s_b, s_a)

    @pl.when(n_far % 2 == 0)
    def _():
        tail(s_a, s_b)

    o_ref[...] = (acc_scr[:, :LANES] / acc_scr[:, LANES:]).astype(o_ref.dtype)


def _attn_scratch():
    tile = ATTN_TILE
    return [pltpu.VMEM((tile, tile), F32), pltpu.VMEM((tile, tile), F32),
            pltpu.VMEM((tile, LANES), F32), pltpu.VMEM((tile, V_PAD), F32)]


def _mla_attention(q, k, v, batch, seq):
    tile = ATTN_TILE
    nq = seq // tile
    c = math.log2(math.e) / math.sqrt(MLA_NOPE_DIM + MLA_ROPE_DIM)
    return pl.pallas_call(
        functools.partial(_attn_kernel, c=c, masked=False),
        grid=(batch, MLA_HEADS, nq),
        in_specs=[pl.BlockSpec((tile, QK_PAD), lambda b, h, i: (b * nq + i, h)),
                  pl.BlockSpec((seq, QK_PAD), lambda b, h, i: (b, h)),
                  pl.BlockSpec((seq, V_PAD), lambda b, h, i: (b, h))],
        out_specs=pl.BlockSpec((tile, MLA_V_DIM), lambda b, h, i: (b * nq + i, h)),
        out_shape=jax.ShapeDtypeStruct((batch * seq, MLA_HEADS * MLA_V_DIM), BF16),
        scratch_shapes=_attn_scratch(),
        compiler_params=_cparams(("arbitrary", "arbitrary", "arbitrary")),
        name="mla_attn",
    )(q, k, v)


def _moba_select_kernel(q_ref, k_ref, v_ref, qa_ref, ka_ref, va_ref, *, seq, nb):
    kf = k_ref[...].astype(F32)
    km = jnp.sum(kf.reshape(nb, MOBA_BLOCK, MOBA_HEAD_DIM), axis=1) * (1.0 / MOBA_BLOCK)
    km_hi = km.astype(BF16)
    km_lo = (km - km_hi.astype(F32)).astype(BF16)
    q = q_ref[...]
    gate = _nt_dot(km_hi, q) + _nt_dot(km_lo, q)
    shift = MOBA_BLOCK.bit_length() - 1
    blk = lax.broadcasted_iota(jnp.int32, (nb, seq), 0)
    qblk = jnp.right_shift(lax.broadcasted_iota(jnp.int32, (nb, seq), 1), shift)
    g = jnp.where(blk < qblk, gate, NEG)
    visible = blk == qblk
    for _ in range(MOBA_TOPK):
        mx = jnp.max(g, axis=0, keepdims=True)
        first = jnp.min(jnp.where(g == mx, blk, nb), axis=0, keepdims=True)
        pick = (blk == first) & (mx > 0.5 * NEG)
        visible = visible | pick
        g = jnp.where(pick, NEG, g)
    mask_t = jnp.concatenate([jnp.where(visible, 0.0, NEG), jnp.zeros((LANES - nb, seq), F32)], axis=0)
    qa_ref[:, :MOBA_HEAD_DIM] = q
    qa_ref[:, MOBA_HEAD_DIM:] = mask_t.T.astype(BF16)
    lane = lax.broadcasted_iota(jnp.int32, (seq, LANES), 1)
    own = jnp.right_shift(lax.broadcasted_iota(jnp.int32, (seq, LANES), 0), shift)
    ka_ref[:, :MOBA_HEAD_DIM] = k_ref[...]
    ka_ref[:, MOBA_HEAD_DIM:] = jnp.where(lane == own, 1.0, 0.0).astype(BF16)
    va_ref[:, :MOBA_HEAD_DIM] = v_ref[...]
    va_ref[:, MOBA_HEAD_DIM:] = jnp.ones((seq, V_PAD - MOBA_HEAD_DIM), BF16)


def _moba_select(qkv_mo, batch, seq):
    nb = seq // MOBA_BLOCK
    assert MOBA_HEAD_DIM == LANES and nb <= QK_PAD - MOBA_HEAD_DIM
    wide = jax.ShapeDtypeStruct((batch * seq, MOBA_HEADS * QK_PAD), BF16)
    return pl.pallas_call(
        functools.partial(_moba_select_kernel, seq=seq, nb=nb),
        grid=(batch, MOBA_HEADS),
        in_specs=[pl.BlockSpec((seq, MOBA_HEAD_DIM), lambda b, h: (b, h)),
                  pl.BlockSpec((seq, MOBA_HEAD_DIM), lambda b, h: (b, MOBA_HEADS + h)),
                  pl.BlockSpec((seq, MOBA_HEAD_DIM), lambda b, h: (b, 2 * MOBA_HEADS + h))],
        out_specs=[pl.BlockSpec((seq, QK_PAD), lambda b, h: (b, h)),
                   pl.BlockSpec((seq, QK_PAD), lambda b, h: (b, h)),
                   pl.BlockSpec((seq, V_PAD), lambda b, h: (b, h))],
        out_shape=[wide, wide, jax.ShapeDtypeStruct((batch * seq, MOBA_HEADS * V_PAD), BF16)],
        compiler_params=_cparams(("arbitrary", "arbitrary")),
        name="moba_select",
    )(qkv_mo, qkv_mo, qkv_mo)


def _t5_kernel(tab_ref, o_ref, *, inv_scale):
    h = pl.program_id(0)
    r = lax.broadcasted_iota(jnp.int32, (LANES, LANES), 0)
    c = lax.broadcasted_iota(jnp.int32, (LANES, LANES), 1)
    max_exact = T5_BUCKETS // 2
    far = tab_ref[T5_BUCKETS - 1, h]

    def block(offset):
        rel = offset + r - c
        n = jnp.maximum(rel, 0)
        nf = jnp.maximum(n, 1).astype(F32)
        large = max_exact + (jnp.log(nf / max_exact) / math.log(T5_MAX_DISTANCE / max_exact)
                             * (T5_BUCKETS - max_exact)).astype(jnp.int32)
        large = jnp.minimum(large, T5_BUCKETS - 1)
        bucket = jnp.where(n < max_exact, n, large)
        bias = jnp.zeros((LANES, LANES), F32)
        for j in range(T5_BUCKETS):
            bias = jnp.where(bucket == j, tab_ref[j, h], bias)
        return jnp.where(rel >= 0, (bias - far) * inv_scale, NEG)

    near = {0: block(0), 1: block(LANES)}
    nblk = ATTN_TILE // LANES
    for d in range(2):
        for i in range(nblk):
            for j in range(nblk):
                k = d * nblk + i - j
                if k < 0:
                    val = jnp.full((LANES, LANES), NEG, F32)
                else:
                    val = near.get(k, jnp.zeros((LANES, LANES), F32))
                o_ref[0, d, i * LANES:(i + 1) * LANES, j * LANES:(j + 1) * LANES] = val


def _t5_tiles(t5_table):
    assert LANES >= T5_MAX_DISTANCE
    tile = ATTN_TILE
    return pl.pallas_call(
        functools.partial(_t5_kernel, inv_scale=math.sqrt(MOBA_HEAD_DIM)),
        grid=(MOBA_HEADS,),
        in_specs=[pl.BlockSpec(memory_space=pltpu.SMEM)],
        out_specs=pl.BlockSpec((1, 2, tile, tile), lambda h: (h, 0, 0, 0)),
        out_shape=jax.ShapeDtypeStruct((MOBA_HEADS, 2, tile, tile), F32),
        compiler_params=_cparams(("arbitrary",)),
        name="t5_tiles",
    )(t5_table)


def _moba_attention(q_wide, k_wide, v_wide, bias, batch, seq):
    tile = ATTN_TILE
    nq = seq // tile
    c = math.log2(math.e) / math.sqrt(MOBA_HEAD_DIM)
    dh = MOBA_HEAD_DIM
    return pl.pallas_call(
        functools.partial(_attn_kernel, c=c, masked=True),
        grid=(batch, MOBA_HEADS, nq),
        in_specs=[pl.BlockSpec((tile, QK_PAD), lambda b, h, i: (b * nq + i, h)),
                  pl.BlockSpec((seq, QK_PAD), lambda b, h, i: (b, h)),
                  pl.BlockSpec((seq, V_PAD), lambda b, h, i: (b, h)),
                  pl.BlockSpec((1, 2, tile, tile), lambda b, h, i: (h, 0, 0, 0))],
        out_specs=pl.BlockSpec((tile, dh), lambda b, h, i: (b * nq + i, h)),
        out_shape=jax.ShapeDtypeStruct((batch * seq, MOBA_WIDTH), BF16),
        scratch_shapes=_attn_scratch(),
        compiler_params=_cparams(("arbitrary", "arbitrary", "arbitrary")),
        name="moba_attn",
    )(q_wide, k_wide, v_wide, bias)


def _outproj_kernel(oa_ref, ob_ref, wa_ref, wb_ref, x_ref, mod_ref, g_ref, b_ref, wr_ref,
                    x1_ref, h2_ref, lg_ref, y_a, y_b, *, alpha, n_tiles):
    i = pl.program_id(0)
    tm, d = y_a.shape
    chunks = 4
    cn, cr = d // chunks, tm // chunks

    def matmul_into(y_ref, c):
        cols = slice(c * cn, (c + 1) * cn)
        y_ref[:, cols] = (jnp.dot(oa_ref[...], wa_ref[:, cols], preferred_element_type=F32)
                          + jnp.dot(ob_ref[...], wb_ref[:, cols], preferred_element_type=F32))

    def epilogue(y_ref, c):
        rows = slice(c * cr, (c + 1) * cr)
        z = alpha * x_ref[rows, :] + mod_ref[0, 2:3, :] * y_ref[rows, :]
        x1 = _ln(z) * g_ref[...] + b_ref[...]
        x1_ref[rows, :] = x1
        h2 = _ln(x1) * (1.0 + mod_ref[0, 4:5, :]) + mod_ref[0, 3:4, :]
        h2_ref[rows, :] = h2
        h_hi = h2.astype(BF16)
        h_lo = (h2 - h_hi.astype(F32)).astype(BF16)
        zz = (jnp.dot(h_hi, wr_ref[...], preferred_element_type=F32)
              + jnp.dot(h_lo, wr_ref[...], preferred_element_type=F32))
        lg_ref[rows, :] = zz + pltpu.roll(zz, LANES // 2, 1)

    def step(y_new, y_old):
        for c in range(chunks):
            if y_new is not None:
                matmul_into(y_new, c)
            if y_old is not None:
                epilogue(y_old, c)

    inner = (i > 0) & (i < n_tiles)
    pl.when(i == 0)(lambda: step(y_a, None))
    pl.when(inner & (i % 2 == 0))(lambda: step(y_a, y_b))
    pl.when(inner & (i % 2 == 1))(lambda: step(y_b, y_a))
    pl.when(i == n_tiles)(lambda: step(None, y_b if n_tiles % 2 == 0 else y_a))


def _out_proj(o_mla, o_moba, wo, xf, mod3, ln_g, ln_b, wr, seq, alpha):
    t, d = xf.shape
    tm = 512
    n = t // tm
    per_b = seq // tm
    ka, kb = o_mla.shape[1], o_moba.shape[1]
    assert ka == kb and wo.shape[0] == ka + kb
    once = pl.Buffered(1)

    def cur(i):
        return (jnp.minimum(i, n - 1), 0)

    def lag(i):
        return (jnp.maximum(i - 1, 0), 0)

    return pl.pallas_call(
        functools.partial(_outproj_kernel, alpha=alpha, n_tiles=n),
        grid=(n + 1,),
        in_specs=[pl.BlockSpec((tm, ka), cur),
                  pl.BlockSpec((tm, kb), cur),
                  pl.BlockSpec((ka, d), lambda i: (0, 0), pipeline_mode=once),
                  pl.BlockSpec((kb, d), lambda i: (1, 0), pipeline_mode=once),
                  pl.BlockSpec((tm, d), lag),
                  pl.BlockSpec((1, 6, d), lambda i: (jnp.maximum(i - 1, 0) // per_b, 0, 0)),
                  pl.BlockSpec((1, d), lambda i: (0, 0)),
                  pl.BlockSpec((1, d), lambda i: (0, 0)),
                  pl.BlockSpec((d, LANES), lambda i: (0, 0), pipeline_mode=once)],
        out_specs=[pl.BlockSpec((tm, d), lag),
                   pl.BlockSpec((tm, d), lag),
                   pl.BlockSpec((tm, LANES), lag)],
        out_shape=[jax.ShapeDtypeStruct((t, d), F32),
                   jax.ShapeDtypeStruct((t, d), F32),
                   jax.ShapeDtypeStruct((t, LANES), F32)],
        scratch_shapes=[pltpu.VMEM((tm, d), F32), pltpu.VMEM((tm, d), F32)],
        compiler_params=_cparams(("arbitrary",)),
        name="out_proj",
    )(o_mla, o_moba, wo, wo, xf, mod3, ln_g, ln_b, wr)


def _route_kernel(lg_ref, br_ref, info_ref, cnt_ref, run_scr, *, tm):
    i = pl.program_id(0)

    @pl.when(i == 0)
    def _():
        run_scr[...] = jnp.zeros_like(run_scr)

    lg = lg_ref[...] + br_ref[...]
    lane = lax.broadcasted_iota(jnp.int32, (tm, LANES), 1)
    e_lo, e_hi = MOE_GROUPS, MOE_GROUPS + MOE_N_EXPERTS
    is_g = lane < e_lo
    gl = jnp.where(is_g, lg, NEG)
    gmax = jnp.max(gl, axis=-1, keepdims=True)
    gidx = jnp.min(jnp.where(gl == gmax, lane, LANES), axis=-1, keepdims=True)
    g_p = 1.0 / jnp.sum(jnp.where(is_g, jnp.exp(gl - gmax), 0.0), axis=-1, keepdims=True)
    grp_of_lane = jnp.right_shift(lane - e_lo, MOE_EXPERTS_PER_GROUP.bit_length() - 1)
    in_grp = (lane >= e_lo) & (lane < e_hi) & (grp_of_lane == gidx)
    el = jnp.where(in_grp, lg, NEG)
    m1 = jnp.max(el, axis=-1, keepdims=True)
    l1 = jnp.min(jnp.where(el == m1, lane, LANES), axis=-1, keepdims=True)
    el2 = jnp.where(lane == l1, NEG, el)
    m2 = jnp.max(el2, axis=-1, keepdims=True)
    l2 = jnp.min(jnp.where(el2 == m2, lane, LANES), axis=-1, keepdims=True)
    zsum = jnp.sum(jnp.where(in_grp, jnp.exp(el - m1), 0.0), axis=-1, keepdims=True)
    p1 = 1.0 / zsum
    p2 = jnp.exp(m2 - m1) / zsum
    wa = g_p * (p1 / (p1 + p2))
    wb = g_p * (p2 / (p1 + p2))
    hot_a = lane == l1
    hot_b = lane == l2
    onehot = jnp.where(hot_a | hot_b, 1.0, 0.0)
    r = lax.broadcasted_iota(jnp.int32, (tm, tm), 0)
    c = lax.broadcasted_iota(jnp.int32, (tm, tm), 1)
    lower = jnp.where(c < r, 1.0, 0.0).astype(BF16)
    before = jnp.dot(lower, onehot.astype(BF16), preferred_element_type=F32) + run_scr[...]
    rank_a = jnp.sum(jnp.where(hot_a, before, 0.0), axis=-1, keepdims=True)
    rank_b = jnp.sum(jnp.where(hot_b, before, 0.0), axis=-1, keepdims=True)
    run_scr[...] += jnp.sum(onehot, axis=0, keepdims=True)
    info = jnp.zeros((tm, LANES), F32)
    for k, val in enumerate([(l1 - e_lo).astype(F32), (l2 - e_lo).astype(F32), wa, wb, rank_a, rank_b]):
        info = jnp.where(lane == k, val, info)
    info_ref[...] = info
    cnt_ref[...] = run_scr[...]


def _route(logits, br):
    t = logits.shape[0]
    tm = 512
    return pl.pallas_call(
        functools.partial(_route_kernel, tm=tm),
        grid=(t // tm,),
        in_specs=[pl.BlockSpec((tm, LANES), lambda i: (i, 0)),
                  pl.BlockSpec((1, LANES), lambda i: (0, 0))],
        out_specs=[pl.BlockSpec((tm, LANES), lambda i: (i, 0)),
                   pl.BlockSpec((1, LANES), lambda i: (0, 0))],
        out_shape=[jax.ShapeDtypeStruct((t, LANES), F32),
                   jax.ShapeDtypeStruct((1, LANES), F32)],
        scratch_shapes=[pltpu.VMEM((1, LANES), F32)],
        compiler_params=_cparams(("arbitrary",)),
        name="route",
    )(logits, br)


def _pos_kernel(info_ref, start_ref, pos_ref):
    info = info_ref[...]
    tm = info.shape[0]
    lane = lax.broadcasted_iota(jnp.int32, (tm, LANES), 1)
    start = start_ref[...]
    cols = []
    for k in range(2):
        e = jnp.sum(jnp.where(lane == k, info, 0.0), axis=-1, keepdims=True).astype(jnp.int32)
        rank = jnp.sum(jnp.where(lane == 4 + k, info, 0.0), axis=-1, keepdims=True)
        base = jnp.sum(jnp.where(lane == e + MOE_GROUPS, start, 0.0), axis=-1, keepdims=True)
        cols.append(base + rank)
    wide = jnp.where(lane == 0, cols[0], jnp.where(lane == 1, cols[1], 0.0))
    pos_ref[...] = wide.T[:pos_ref.shape[0], :].astype(jnp.int32)


def _positions(info, start_lanes):
    t = info.shape[0]
    tm = 1024
    return pl.pallas_call(
        _pos_kernel,
        grid=(t // tm,),
        in_specs=[pl.BlockSpec((tm, LANES), lambda i: (i, 0)),
                  pl.BlockSpec((1, LANES), lambda i: (0, 0))],
        out_specs=pl.BlockSpec((8, tm), lambda i: (0, i)),
        out_shape=jax.ShapeDtypeStruct((8, t), jnp.int32),
        compiler_params=_cparams(("arbitrary",)),
        name="positions",
    )(info, start_lanes)


def _row_copy(src_ref, src_row, dst_ref, dst_row, sem):
    return pltpu.make_async_copy(src_ref.at[pl.ds(src_row, 1)], dst_ref.at[pl.ds(dst_row, 1)], sem)


def _dispatch_kernel(pa_ref, pb_ref, pad0_ref, padn_ref, used_ref, h_ref, xs_ref, ztile, sem, zsem,
                     *, tm, tr, n_tiles):
    i = pl.program_id(0)
    base = i * tm

    def zero_fill(act):
        def whole_tile(j, _):
            act(pltpu.make_async_copy(ztile, xs_ref.at[pl.ds(pl.multiple_of(j * tr, tr), tr)], zsem))
            return 0

        lax.fori_loop(used_ref[0], n_tiles, whole_tile, 0)

        def expert_pad(e, _):
            n, start = padn_ref[e], pad0_ref[e]
            head = jnp.minimum(jnp.bitwise_and(-start, SUBLANES - 1), n)
            for k in range(SUBLANES - 1):
                pl.when(k < head)(functools.partial(act, _row_copy(ztile, 0, xs_ref, start + k, zsem)))
            off = start + head
            groups = jnp.right_shift(n - head, SUBLANES.bit_length() - 1)
            for bit in reversed(range((tr // SUBLANES - 1).bit_length())):
                size = SUBLANES << bit
                take = jnp.bitwise_and(jnp.right_shift(groups, bit), 1)
                dst = xs_ref.at[pl.ds(pl.multiple_of(off, SUBLANES), size)]
                pl.when(take == 1)(functools.partial(act, pltpu.make_async_copy(ztile.at[pl.ds(0, size)], dst, zsem)))
                off = off + take * size
            return 0

        lax.fori_loop(0, MOE_N_EXPERTS, expert_pad, 0)

    @pl.when(i == 0)
    def _():
        ztile[...] = jnp.zeros(ztile.shape, ztile.dtype)
        zero_fill(lambda cp: cp.start())

    def issue(t, _):
        _row_copy(h_ref, t, xs_ref, pa_ref[base + t], sem).start()
        _row_copy(h_ref, t, xs_ref, pb_ref[base + t], sem).start(priority=1)
        return 0

    lax.fori_loop(0, tm, issue, 0, unroll=8)
    for _ in range(2):
        pltpu.make_async_copy(h_ref, xs_ref.at[pl.ds(0, tm)], sem).wait()

    @pl.when(i == pl.num_programs(0) - 1)
    def _():
        zero_fill(lambda cp: cp.wait())


def _dispatch(pos_a, pos_b, pad0, padn, used, h2, tr, n_tiles):
    t, d = h2.shape
    tm = 256
    grid_spec = pltpu.PrefetchScalarGridSpec(
        num_scalar_prefetch=5,
        grid=(t // tm,),
        in_specs=[pl.BlockSpec((tm, d), lambda i, *_: (i, 0))],
        out_specs=pl.BlockSpec(memory_space=pl.ANY),
        scratch_shapes=[pltpu.VMEM((tr, d), F32), pltpu.SemaphoreType.DMA(()), pltpu.SemaphoreType.DMA(())],
    )
    return pl.pallas_call(
        functools.partial(_dispatch_kernel, tm=tm, tr=tr, n_tiles=n_tiles),
        grid_spec=grid_spec,
        out_shape=jax.ShapeDtypeStruct((n_tiles * tr, d), F32),
        compiler_params=_cparams(("arbitrary",), row_dma=True),
        name="dispatch",
    )(pos_a, pos_b, pad0, padn, used, h2)


def _experts_kernel(texp_ref, tidx_ref, nexte_ref, used_ref, x_ref, w1_hbm, w3_hbm, w2_hbm, o_ref,
                    w1_f32, w3_f32, w2_f32, w1_scr, w3_scr, w2_scr, sems):
    i = pl.program_id(0)
    prev = texp_ref[jnp.maximum(i - 1, 0)]

    def weight_copies(e):
        return [pltpu.make_async_copy(w1_hbm.at[e], w1_f32, sems.at[0]),
                pltpu.make_async_copy(w3_hbm.at[e], w3_f32, sems.at[1]),
                pltpu.make_async_copy(w2_hbm.at[e], w2_f32, sems.at[2])]

    @pl.when(i == 0)
    def _():
        for cp in weight_copies(texp_ref[0]):
            cp.start()

    @pl.when((i == 0) | (texp_ref[i] != prev))
    def _():
        for cp in weight_copies(texp_ref[i]):
            cp.wait()
        w1_scr[...] = w1_f32[...].astype(BF16)
        w3_scr[...] = w3_f32[...].astype(BF16)
        w2_scr[...] = w2_f32[...].astype(BF16)

        @pl.when(nexte_ref[i] >= 0)
        def _():
            for cp in weight_copies(nexte_ref[i]):
                cp.start()

    @pl.when(i < used_ref[0])
    def _():
        x = x_ref[...].astype(BF16)
        a = jnp.dot(x, w1_scr[...], preferred_element_type=F32)
        b = jnp.dot(x, w3_scr[...], preferred_element_type=F32)
        hid = (a / (1.0 + jnp.exp(-a))) * b
        o_ref[...] = jnp.dot(hid.astype(BF16), w2_scr[...], preferred_element_type=F32)

    @pl.when(i >= used_ref[0])
    def _():
        o_ref[...] = jnp.zeros_like(o_ref)


def _experts(texp, tidx, nexte, used, xs, w1, w3, w2, tr):
    rows, d = xs.shape
    nt = rows // tr
    f = w1.shape[2]
    grid_spec = pltpu.PrefetchScalarGridSpec(
        num_scalar_prefetch=4,
        grid=(nt,),
        in_specs=[pl.BlockSpec((tr, d), lambda i, te, ti, ne, u: (ti[i], 0)),
                  pl.BlockSpec(memory_space=pl.ANY),
                  pl.BlockSpec(memory_space=pl.ANY),
                  pl.BlockSpec(memory_space=pl.ANY)],
        out_specs=pl.BlockSpec((tr, d), lambda i, te, ti, ne, u: (i, 0)),
        scratch_shapes=[pltpu.VMEM((d, f), F32), pltpu.VMEM((d, f), F32), pltpu.VMEM((f, d), F32),
                        pltpu.VMEM((d, f), BF16), pltpu.VMEM((d, f), BF16), pltpu.VMEM((f, d), BF16),
                        pltpu.SemaphoreType.DMA((3,))],
    )
    return pl.pallas_call(
        _experts_kernel,
        grid_spec=grid_spec,
        out_shape=jax.ShapeDtypeStruct((rows, d), F32),
        compiler_params=_cparams(("arbitrary",)),
        name="experts",
    )(texp, tidx, nexte, used, xs, w1, w3, w2)


def _combine_kernel(pa_ref, pb_ref, ys_ref, x1_ref, info_ref, mod_ref, g_ref, b_ref, o_ref,
                    buf_a, buf_b, sems, *, tm, alpha):
    i = pl.program_id(0)

    def gather(tile, slot):
        base = tile * tm

        def issue(t, _):
            _row_copy(ys_ref, pa_ref[base + t], buf_a.at[slot], t, sems.at[slot]).start()
            _row_copy(ys_ref, pb_ref[base + t], buf_b.at[slot], t, sems.at[slot]).start(priority=1)
            return 0

        lax.fori_loop(0, tm, issue, 0, unroll=8)

    @pl.when(i == 0)
    def _():
        gather(0, 0)

    @pl.when(i + 1 < pl.num_programs(0))
    def _():
        gather(i + 1, (i + 1) % 2)

    slot = i % 2
    for buf in (buf_a, buf_b):
        pltpu.make_async_copy(ys_ref.at[pl.ds(0, tm)], buf.at[slot], sems.at[slot]).wait()
    info = info_ref[...]
    y = info[:, 2:3] * buf_a[slot] + info[:, 3:4] * buf_b[slot]
    z = alpha * x1_ref[...] + mod_ref[0, 5:6, :] * y
    o_ref[...] = _ln(z) * g_ref[...] + b_ref[...]


def _combine(pos_a, pos_b, ys, x1, info, mod3, ln_g, ln_b, seq, alpha):
    t, d = x1.shape
    tm = 256
    per_b = seq // tm
    grid_spec = pltpu.PrefetchScalarGridSpec(
        num_scalar_prefetch=2,
        grid=(t // tm,),
        in_specs=[pl.BlockSpec(memory_space=pl.ANY),
                  pl.BlockSpec((tm, d), lambda i, pa, pb: (i, 0)),
                  pl.BlockSpec((tm, LANES), lambda i, pa, pb: (i, 0)),
                  pl.BlockSpec((1, 6, d), lambda i, pa, pb: (i // per_b, 0, 0)),
                  pl.BlockSpec((1, d), lambda i, pa, pb: (0, 0)),
                  pl.BlockSpec((1, d), lambda i, pa, pb: (0, 0))],
        out_specs=pl.BlockSpec((tm, d), lambda i, pa, pb: (i, 0)),
        scratch_shapes=[pltpu.VMEM((2, tm, d), F32), pltpu.VMEM((2, tm, d), F32), pltpu.SemaphoreType.DMA((2,))],
    )
    return pl.pallas_call(
        functools.partial(_combine_kernel, tm=tm, alpha=alpha),
        grid_spec=grid_spec,
        out_shape=jax.ShapeDtypeStruct((t, d), F32),
        compiler_params=_cparams(("arbitrary",), row_dma=True),
        name="combine",
    )(pos_a, pos_b, ys, x1, info, mod3, ln_g, ln_b)


def _prep_w_in(w):
    cq = w[:, :MLA_Q_RANK]
    ckv = w[:, MLA_Q_RANK:MLA_Q_RANK + MLA_KV_RANK]
    r0 = MLA_Q_RANK + MLA_KV_RANK
    kr = w[:, r0:r0 + MLA_ROPE_DIM]
    mo = w[:, r0 + MLA_ROPE_DIM:]
    half = MLA_ROPE_DIM // 2
    z = jnp.zeros((w.shape[0], LANES - MLA_ROPE_DIM), w.dtype)
    return jnp.concatenate([cq, ckv, kr, z, kr[:, half:], kr[:, :half], z, mo], axis=1).astype(BF16)


def _prep_w_uq(w):
    r = w.shape[0]
    w = w.reshape(r, MLA_HEADS, MLA_NOPE_DIM + MLA_ROPE_DIM)
    half = MLA_ROPE_DIM // 2
    nope = w[:, :, :MLA_NOPE_DIM]
    x1 = w[:, :, MLA_NOPE_DIM:MLA_NOPE_DIM + half]
    x2 = w[:, :, MLA_NOPE_DIM + half:]
    z = jnp.zeros((r, MLA_HEADS, LANES - MLA_ROPE_DIM), w.dtype)
    return jnp.concatenate([nope, x1, x2, z, x2, x1, z], axis=2).reshape(r, MLA_HEADS * 3 * LANES).astype(BF16)


def _rope_lanes(seq):
    inv = 1.0 / (ROPE_THETA ** (jnp.arange(0, MLA_ROPE_DIM, 2, dtype=F32) / MLA_ROPE_DIM))
    ang = jnp.arange(seq, dtype=F32)[:, None] * inv[None, :]
    cos, sin = jnp.cos(ang), jnp.sin(ang)
    z = jnp.zeros((seq, LANES - MLA_ROPE_DIM), F32)
    return jnp.concatenate([cos, cos, z], axis=1), jnp.concatenate([-sin, sin, z], axis=1)


def _prep_router(w_rg, b_rg, w_re, b_re):
    d = w_rg.shape[0]
    w = jnp.concatenate([w_rg, w_re], axis=1)
    n = w.shape[1]
    hi = w.astype(BF16)
    lo = (w - hi.astype(F32)).astype(BF16)
    z = jnp.zeros((d, LANES // 2 - n), BF16)
    wr = jnp.concatenate([hi, z, lo, z], axis=1)
    br = jnp.zeros((1, LANES), F32).at[0, :n].set(jnp.concatenate([b_rg, b_re]))
    return wr, br


def _layer(xf, mod3, batch, seq, depth_alpha, w_in, q_norm_g, w_uq, kv_norm_g, w_ukv, w_out, bias_tiles,
           cos_t, sin_t, ln1_g, ln1_b, w_rg, b_rg, w_re, b_re, w1, w3, w2, ln2_g, ln2_b):
    t, d = xf.shape
    part_a, qkv_mo = _in_proj(xf, mod3, _prep_w_in(w_in), seq)
    q, k, v = _qkv(part_a, _prep_w_uq(w_uq), w_ukv.astype(BF16), q_norm_g.reshape(1, -1),
                   kv_norm_g.reshape(1, -1), cos_t, sin_t, seq)
    o_mla = _mla_attention(q, k, v, batch, seq)
    q_wide, k_wide, v_wide = _moba_select(qkv_mo, batch, seq)
    o_moba = _moba_attention(q_wide, k_wide, v_wide, bias_tiles, batch, seq)
    wo = w_out.astype(BF16)
    wr, br = _prep_router(w_rg, b_rg, w_re, b_re)
    x1, h2, logits = _out_proj(o_mla, o_moba, wo, xf, mod3, ln1_g.reshape(1, d),
                               ln1_b.reshape(1, d), wr, seq, depth_alpha)
    info, counts = _route(logits, br)
    tr = 256
    nt = (2 * t) // tr + MOE_N_EXPERTS
    cnt = counts[0, MOE_GROUPS:MOE_GROUPS + MOE_N_EXPERTS].astype(jnp.int32)
    ntile = (cnt + tr - 1) // tr
    tile_end = jnp.cumsum(ntile)
    tile_start = tile_end - ntile
    used = tile_end[-1]
    start_lanes = jnp.zeros((1, LANES), F32).at[0, MOE_GROUPS:MOE_GROUPS + MOE_N_EXPERTS].set(
        (tile_start * tr).astype(F32))
    pos = _positions(info, start_lanes)
    pos_a, pos_b = pos[0], pos[1]
    tidx = jnp.minimum(jnp.arange(nt, dtype=jnp.int32), used - 1)
    texp = jnp.sum(tidx[:, None] >= tile_end[None, :], axis=1).astype(jnp.int32)
    eids = jnp.arange(MOE_N_EXPERTS, dtype=jnp.int32)
    later = (eids[None, :] > eids[:, None]) & (ntile[None, :] > 0)
    next_nonempty = jnp.where(later.any(axis=1), jnp.argmax(later, axis=1), -1).astype(jnp.int32)
    nexte = jnp.sum(jnp.where(texp[:, None] == eids[None, :], next_nonempty[None, :], 0), axis=1).astype(jnp.int32)
    used1 = used.reshape(1).astype(jnp.int32)
    pad0 = (tile_start * tr + cnt).astype(jnp.int32)
    padn = (ntile * tr - cnt).astype(jnp.int32)
    xs = _dispatch(pos_a, pos_b, pad0, padn, used1, h2, tr, nt)
    ys = _experts(texp, tidx, nexte, used1, xs, w1, w3, w2, tr)
    return _combine(pos_a, pos_b, ys, x1, info, mod3, ln2_g.reshape(1, d), ln2_b.reshape(1, d), seq, depth_alpha)


def kernel(x, c, w_ada, b_ada, w_in, q_norm_g, w_uq, kv_norm_g, w_ukv, w_out, t5_table, ln1_g, ln1_b,
           w_router_group, b_router_group, w_router_expert, b_router_expert, w1, w3, w2, ln2_g, ln2_b):
    batch, seq, d = x.shape
    depth = w_ada.shape[0]
    alpha = (2.0 * depth) ** 0.25
    cos_t, sin_t = _rope_lanes(seq)
    bias_tiles = _t5_tiles(t5_table)
    xf = x.reshape(batch * seq, d)
    for l in range(depth):
        mod3 = _ada_mod(c, w_ada[l], b_ada[l]).reshape(batch, 6, d)
        xf = _layer(xf, mod3, batch, seq, alpha, w_in[l], q_norm_g[l], w_uq[l], kv_norm_g[l], w_ukv[l],
                    w_out[l], bias_tiles, cos_t, sin_t, ln1_g[l], ln1_b[l], w_router_group[l],
                    b_router_group[l], w_router_expert[l], b_router_expert[l], w1[l], w3[l], w2[l],
                    ln2_g[l], ln2_b[l])
    return xf.reshape(batch, seq, d)
```

```python
import functools
import math

import jax
import jax.numpy as jnp
from jax import lax
from jax.experimental import pallas as pl
from jax.experimental.pallas import tpu as pltpu

D_MODEL = 2048
MLA_HEADS = 8
MLA_Q_RANK = 512
MLA_KV_RANK = 256
MLA_NOPE_DIM = 128
MLA_ROPE_DIM = 64
MLA_V_DIM = 128
ROPE_THETA = 10000.0
MOBA_HEADS = 8
MOBA_HEAD_DIM = 128
MOBA_BLOCK = 256
MOBA_TOPK = 3
T5_BUCKETS = 32
T5_MAX_DISTANCE = 128
MOE_GROUPS = 4
MOE_EXPERTS_PER_GROUP = 8
MOE_N_EXPERTS = MOE_GROUPS * MOE_EXPERTS_PER_GROUP
MOE_D_FF = 512
LN_EPS = 1e-5
RMS_EPS = 1e-6
MOBA_WIDTH = MOBA_HEADS * MOBA_HEAD_DIM

LANES = 128
SUBLANES = 8
QK_PAD = 256
V_PAD = 256
PART_A = 1024
NEG = -1e30
VMEM_LIMIT = 56 * 1024 * 1024

F32 = jnp.float32
BF16 = jnp.bfloat16


def _cparams(sem, row_dma=False):
    return pltpu.CompilerParams(dimension_semantics=sem, vmem_limit_bytes=VMEM_LIMIT,
                                disable_bounds_checks=row_dma)


def _ln(x):
    mu = jnp.mean(x, axis=-1, keepdims=True)
    xc = x - mu
    var = jnp.mean(xc * xc, axis=-1, keepdims=True)
    return xc * lax.rsqrt(var + LN_EPS)


def _nt_dot(a, b):
    return lax.dot_general(a, b, (((1,), (1,)), ((), ())), preferred_element_type=F32)


def _ada_kernel(ct_ref, w_ref, b_ref, o_ref, *, batch):
    ct = ct_ref[...]
    ca = ct / (1.0 + jnp.exp(-ct))
    w = w_ref[...]
    rows = [jnp.sum(w * ca[:, b:b + 1], axis=0, keepdims=True) for b in range(batch)]
    o_ref[...] = jnp.concatenate(rows, axis=0) + b_ref[...]


def _ada_mod(c, w_ada, b_ada):
    batch, d = c.shape
    n = w_ada.shape[1]
    tn = 512
    ct = jnp.zeros((d, LANES), F32).at[:, :batch].set(c.T)
    return pl.pallas_call(
        functools.partial(_ada_kernel, batch=batch),
        grid=(n // tn,),
        in_specs=[pl.BlockSpec((d, LANES), lambda j: (0, 0)),
                  pl.BlockSpec((d, tn), lambda j: (0, j)),
                  pl.BlockSpec((1, tn), lambda j: (0, j))],
        out_specs=pl.BlockSpec((batch, tn), lambda j: (0, j)),
        out_shape=jax.ShapeDtypeStruct((batch, n), F32),
        compiler_params=_cparams(("arbitrary",)),
        name="ada_mod",
    )(ct, w_ada, b_ada.reshape(1, n))


def _inproj_kernel(x_ref, mod_ref, w_ref, a_ref, b_ref, h_scr):
    j = pl.program_id(1)

    @pl.when(j == 0)
    def _():
        h = _ln(x_ref[...]) * (1.0 + mod_ref[0, 1:2, :]) + mod_ref[0, 0:1, :]
        h_scr[...] = h.astype(BF16)
        a_ref[...] = jnp.dot(h_scr[...], w_ref[...], preferred_element_type=F32)

    @pl.when(j > 0)
    def _():
        b_ref[...] = jnp.dot(h_scr[...], w_ref[...], preferred_element_type=F32).astype(BF16)


def _in_proj(xf, mod3, w4, seq):
    t, d = xf.shape
    n = w4.shape[1]
    tm, tn = 512, PART_A
    per_b = seq // tm
    return pl.pallas_call(
        _inproj_kernel,
        grid=(t // tm, n // tn),
        in_specs=[pl.BlockSpec((tm, d), lambda i, j: (i, 0)),
                  pl.BlockSpec((1, 6, d), lambda i, j: (i // per_b, 0, 0)),
                  pl.BlockSpec((d, tn), lambda i, j: (0, j))],
        out_specs=[pl.BlockSpec((tm, tn), lambda i, j: (i, 0)),
                   pl.BlockSpec((tm, tn), lambda i, j: (i, jnp.maximum(j - 1, 0)))],
        out_shape=[jax.ShapeDtypeStruct((t, PART_A), F32),
                   jax.ShapeDtypeStruct((t, n - PART_A), BF16)],
        scratch_shapes=[pltpu.VMEM((tm, d), BF16)],
        compiler_params=_cparams(("arbitrary", "arbitrary")),
        name="in_proj",
    )(xf, mod3, w4)


def _qkv_kernel(a_ref, wq_ref, wkv_ref, gq_ref, gkv_ref, cos_ref, sin_ref, q_ref, k_ref, v_ref):
    a = a_ref[...]
    cq = a[:, :MLA_Q_RANK]
    ckv = a[:, MLA_Q_RANK:MLA_Q_RANK + MLA_KV_RANK]
    kr = a[:, 768:896]
    krs = a[:, 896:1024]
    cqn = (cq * lax.rsqrt(jnp.mean(cq * cq, axis=-1, keepdims=True) + RMS_EPS) * gq_ref[...]).astype(BF16)
    ckvn = (ckv * lax.rsqrt(jnp.mean(ckv * ckv, axis=-1, keepdims=True) + RMS_EPS) * gkv_ref[...]).astype(BF16)
    q3 = jnp.dot(cqn, wq_ref[...], preferred_element_type=F32)
    kv = jnp.dot(ckvn, wkv_ref[...], preferred_element_type=F32)
    cos = cos_ref[...]
    sin = sin_ref[...]
    krr = (kr * cos + krs * sin).astype(BF16)
    for h in range(MLA_HEADS):
        q0 = h * 3 * LANES
        c0 = h * QK_PAD
        q_ref[:, c0:c0 + LANES] = q3[:, q0:q0 + LANES].astype(BF16)
        q_ref[:, c0 + LANES:c0 + QK_PAD] = (
            q3[:, q0 + LANES:q0 + 2 * LANES] * cos + q3[:, q0 + 2 * LANES:q0 + 3 * LANES] * sin).astype(BF16)
        k_ref[:, c0:c0 + LANES] = kv[:, c0:c0 + LANES].astype(BF16)
        k_ref[:, c0 + LANES:c0 + QK_PAD] = krr
        v_ref[:, c0:c0 + LANES] = kv[:, c0 + LANES:c0 + QK_PAD].astype(BF16)
        v_ref[:, c0 + LANES:c0 + V_PAD] = jnp.ones((a.shape[0], V_PAD - LANES), BF16)


def _qkv(part_a, wq3, wkv, gq, gkv, cos_t, sin_t, seq):
    t = part_a.shape[0]
    tm = 512
    per_b = seq // tm
    hq = MLA_HEADS * QK_PAD
    return pl.pallas_call(
        _qkv_kernel,
        grid=(t // tm,),
        in_specs=[pl.BlockSpec((tm, PART_A), lambda i: (i, 0)),
                  pl.BlockSpec(wq3.shape, lambda i: (0, 0)),
                  pl.BlockSpec(wkv.shape, lambda i: (0, 0)),
                  pl.BlockSpec((1, MLA_Q_RANK), lambda i: (0, 0)),
                  pl.BlockSpec((1, MLA_KV_RANK), lambda i: (0, 0)),
                  pl.BlockSpec((tm, LANES), lambda i: (i % per_b, 0)),
                  pl.BlockSpec((tm, LANES), lambda i: (i % per_b, 0))],
        out_specs=[pl.BlockSpec((tm, hq), lambda i: (i, 0)),
                   pl.BlockSpec((tm, hq), lambda i: (i, 0)),
                   pl.BlockSpec((tm, MLA_HEADS * V_PAD), lambda i: (i, 0))],
        out_shape=[jax.ShapeDtypeStruct((t, hq), BF16),
                   jax.ShapeDtypeStruct((t, hq), BF16),
                   jax.ShapeDtypeStruct((t, MLA_HEADS * V_PAD), BF16)],
        compiler_params=_cparams(("arbitrary",)),
        name="qkv",
    )(part_a, wq3, wkv, gq, gkv, cos_t, sin_t)


ATTN_TILE = 512


def _attn_kernel(*refs, c, masked):
    if masked:
        q_ref, k_ref, v_ref, bias_ref, o_ref, s_a, s_b, m_scr, acc_scr = refs
    else:
        q_ref, k_ref, v_ref, o_ref, s_a, s_b, m_scr, acc_scr = refs
    tile = ATTN_TILE
    qi = pl.program_id(2)

    def put_scores(buf, t):
        k = k_ref[pl.ds(pl.multiple_of(t * tile, tile), tile), :]
        buf[...] = _nt_dot(q_ref[...], k)

    def tile_step(buf, t, kind, nxt=None):
        if nxt is not None:
            put_scores(*nxt)
        s = buf[...]
        if masked:
            if kind == "prev":
                s = s + bias_ref[0, 1] + jnp.where(qi == 0, NEG, 0.0)
            elif kind == "diag":
                s = s + bias_ref[0, 0]
        elif kind == "diag":
            row = lax.broadcasted_iota(jnp.int32, (tile, tile), 0)
            col = lax.broadcasted_iota(jnp.int32, (tile, tile), 1)
            s = jnp.where(row >= col, s, NEG)
        m_old = m_scr[...]
        m_new = jnp.maximum(m_old, jnp.broadcast_to(jnp.max(s, axis=-1, keepdims=True), m_old.shape))
        alpha = jnp.exp2((m_old - m_new) * c)
        p = jnp.concatenate([jnp.exp2(((s[:, j * LANES:(j + 1) * LANES] - m_new) * c).astype(BF16))
                             for j in range(tile // LANES)], axis=1)
        v = v_ref[pl.ds(pl.multiple_of(t * tile, tile), tile), :]
        pv = jnp.dot(p, v, preferred_element_type=F32)
        acc_scr[...] = jnp.concatenate([alpha, alpha], axis=1) * acc_scr[...] + pv
        m_scr[...] = m_new

    m_scr[...] = jnp.full(m_scr.shape, NEG, F32)
    acc_scr[...] = jnp.zeros(acc_scr.shape, F32)
    put_scores(s_a, 0)

    n_far = jnp.maximum(qi - 1, 0) if masked else qi
    pn = jnp.maximum(qi - 1, 0)

    def pair(j, _):
        t = 2 * j
        tile_step(s_a, t, "far", (s_b, t + 1))
        tile_step(s_b, t + 1, "far", (s_a, t + 2))
        return 0

    lax.fori_loop(0, n_far // 2, pair, 0)
    t0 = 2 * (n_far // 2)

    def tail(first, second):
        if masked:
            tile_step(first, pn, "prev", (second, qi))
            tile_step(second, qi, "diag")
        else:
            tile_step(first, qi, "diag")

    @pl.when(n_far % 2 == 1)
    def _():
        tile_step(s_a, t0, "far", (s_b, t0 + 1))
        tail(s_b, s_a)

    @pl.when(n_far % 2 == 0)
    def _():
        tail(s_a, s_b)

    o_ref[...] = (acc_scr[:, :LANES] / acc_scr[:, LANES:]).astype(o_ref.dtype)


def _attn_scratch():
    tile = ATTN_TILE
    return [pltpu.VMEM((tile, tile), F32), pltpu.VMEM((tile, tile), F32),
            pltpu.VMEM((tile, LANES), F32), pltpu.VMEM((tile, V_PAD), F32)]


def _mla_attention(q, k, v, batch, seq):
    tile = ATTN_TILE
    nq = seq // tile
    c = math.log2(math.e) / math.sqrt(MLA_NOPE_DIM + MLA_ROPE_DIM)
    return pl.pallas_call(
        functools.partial(_attn_kernel, c=c, masked=False),
        grid=(batch, MLA_HEADS, nq),
        in_specs=[pl.BlockSpec((tile, QK_PAD), lambda b, h, i: (b * nq + i, h)),
                  pl.BlockSpec((seq, QK_PAD), lambda b, h, i: (b, h)),
                  pl.BlockSpec((seq, V_PAD), lambda b, h, i: (b, h))],
        out_specs=pl.BlockSpec((tile, MLA_V_DIM), lambda b, h, i: (b * nq + i, h)),
        out_shape=jax.ShapeDtypeStruct((batch * seq, MLA_HEADS * MLA_V_DIM), BF16),
        scratch_shapes=_attn_scratch(),
        compiler_params=_cparams(("arbitrary", "arbitrary", "arbitrary")),
        name="mla_attn",
    )(q, k, v)


def _moba_select_kernel(q_ref, k_ref, v_ref, qa_ref, ka_ref, va_ref, *, seq, nb):
    kf = k_ref[...].astype(F32)
    km = jnp.sum(kf.reshape(nb, MOBA_BLOCK, MOBA_HEAD_DIM), axis=1) * (1.0 / MOBA_BLOCK)
    km_hi = km.astype(BF16)
    km_lo = (km - km_hi.astype(F32)).astype(BF16)
    q = q_ref[...]
    gate = _nt_dot(km_hi, q) + _nt_dot(km_lo, q)
    shift = MOBA_BLOCK.bit_length() - 1
    blk = lax.broadcasted_iota(jnp.int32, (nb, seq), 0)
    qblk = jnp.right_shift(lax.broadcasted_iota(jnp.int32, (nb, seq), 1), shift)
    g = jnp.where(blk < qblk, gate, NEG)
    visible = blk == qblk
    for _ in range(MOBA_TOPK):
        mx = jnp.max(g, axis=0, keepdims=True)
        first = jnp.min(jnp.where(g == mx, blk, nb), axis=0, keepdims=True)
        pick = (blk == first) & (mx > 0.5 * NEG)
        visible = visible | pick
        g = jnp.where(pick, NEG, g)
    mask_t = jnp.concatenate([jnp.where(visible, 0.0, NEG), jnp.zeros((LANES - nb, seq), F32)], axis=0)
    qa_ref[:, :MOBA_HEAD_DIM] = q
    qa_ref[:, MOBA_HEAD_DIM:] = mask_t.T.astype(BF16)
    lane = lax.broadcasted_iota(jnp.int32, (seq, LANES), 1)
    own = jnp.right_shift(lax.broadcasted_iota(jnp.int32, (seq, LANES), 0), shift)
    ka_ref[:, :MOBA_HEAD_DIM] = k_ref[...]
    ka_ref[:, MOBA_HEAD_DIM:] = jnp.where(lane == own, 1.0, 0.0).astype(BF16)
    va_ref[:, :MOBA_HEAD_DIM] = v_ref[...]
    va_ref[:, MOBA_HEAD_DIM:] = jnp.ones((seq, V_PAD - MOBA_HEAD_DIM), BF16)


def _moba_select(qkv_mo, batch, seq):
    nb = seq // MOBA_BLOCK
    assert MOBA_HEAD_DIM == LANES and nb <= QK_PAD - MOBA_HEAD_DIM
    wide = jax.ShapeDtypeStruct((batch * seq, MOBA_HEADS * QK_PAD), BF16)
    return pl.pallas_call(
        functools.partial(_moba_select_kernel, seq=seq, nb=nb),
        grid=(batch, MOBA_HEADS),
        in_specs=[pl.BlockSpec((seq, MOBA_HEAD_DIM), lambda b, h: (b, h)),
                  pl.BlockSpec((seq, MOBA_HEAD_DIM), lambda b, h: (b, MOBA_HEADS + h)),
                  pl.BlockSpec((seq, MOBA_HEAD_DIM), lambda b, h: (b, 2 * MOBA_HEADS + h))],
        out_specs=[pl.BlockSpec((seq, QK_PAD), lambda b, h: (b, h)),
                   pl.BlockSpec((seq, QK_PAD), lambda b, h: (b, h)),
                   pl.BlockSpec((seq, V_PAD), lambda b, h: (b, h))],
        out_shape=[wide, wide, jax.ShapeDtypeStruct((batch * seq, MOBA_HEADS * V_PAD), BF16)],
        compiler_params=_cparams(("arbitrary", "arbitrary")),
        name="moba_select",
    )(qkv_mo, qkv_mo, qkv_mo)


def _t5_kernel(tab_ref, o_ref, *, inv_scale):
    h = pl.program_id(0)
    r = lax.broadcasted_iota(jnp.int32, (LANES, LANES), 0)
    c = lax.broadcasted_iota(jnp.int32, (LANES, LANES), 1)
    max_exact = T5_BUCKETS // 2
    far = tab_ref[T5_BUCKETS - 1, h]

    def block(offset):
        rel = offset + r - c
        n = jnp.maximum(rel, 0)
        nf = jnp.maximum(n, 1).astype(F32)
        large = max_exact + (jnp.log(nf / max_exact) / math.log(T5_MAX_DISTANCE / max_exact)
                             * (T5_BUCKETS - max_exact)).astype(jnp.int32)
        large = jnp.minimum(large, T5_BUCKETS - 1)
        bucket = jnp.where(n < max_exact, n, large)
        bias = jnp.zeros((LANES, LANES), F32)
        for j in range(T5_BUCKETS):
            bias = jnp.where(bucket == j, tab_ref[j, h], bias)
        return jnp.where(rel >= 0, (bias - far) * inv_scale, NEG)

    near = {0: block(0), 1: block(LANES)}
    nblk = ATTN_TILE // LANES
    for d in range(2):
        for i in range(nblk):
            for j in range(nblk):
                k = d * nblk + i - j
                if k < 0:
                    val = jnp.full((LANES, LANES), NEG, F32)
                else:
                    val = near.get(k, jnp.zeros((LANES, LANES), F32))
                o_ref[0, d, i * LANES:(i + 1) * LANES, j * LANES:(j + 1) * LANES] = val


def _t5_tiles(t5_table):
    assert LANES >= T5_MAX_DISTANCE
    tile = ATTN_TILE
    return pl.pallas_call(
        functools.partial(_t5_kernel, inv_scale=math.sqrt(MOBA_HEAD_DIM)),
        grid=(MOBA_HEADS,),
        in_specs=[pl.BlockSpec(memory_space=pltpu.SMEM)],
        out_specs=pl.BlockSpec((1, 2, tile, tile), lambda h: (h, 0, 0, 0)),
        out_shape=jax.ShapeDtypeStruct((MOBA_HEADS, 2, tile, tile), F32),
        compiler_params=_cparams(("arbitrary",)),
        name="t5_tiles",
    )(t5_table)


def _moba_attention(q_wide, k_wide, v_wide, bias, batch, seq):
    tile = ATTN_TILE
    nq = seq // tile
    c = math.log2(math.e) / math.sqrt(MOBA_HEAD_DIM)
    dh = MOBA_HEAD_DIM
    return pl.pallas_call(
        functools.partial(_attn_kernel, c=c, masked=True),
        grid=(batch, MOBA_HEADS, nq),
        in_specs=[pl.BlockSpec((tile, QK_PAD), lambda b, h, i: (b * nq + i, h)),
                  pl.BlockSpec((seq, QK_PAD), lambda b, h, i: (b, h)),
                  pl.BlockSpec((seq, V_PAD), lambda b, h, i: (b, h)),
                  pl.BlockSpec((1, 2, tile, tile), lambda b, h, i: (h, 0, 0, 0))],
        out_specs=pl.BlockSpec((tile, dh), lambda b, h, i: (b * nq + i, h)),
        out_shape=jax.ShapeDtypeStruct((batch * seq, MOBA_WIDTH), BF16),
        scratch_shapes=_attn_scratch(),
        compiler_params=_cparams(("arbitrary", "arbitrary", "arbitrary")),
        name="moba_attn",
    )(q_wide, k_wide, v_wide, bias)


def _outproj_kernel(oa_ref, ob_ref, wa_ref, wb_ref, x_ref, mod_ref, g_ref, b_ref, wr_ref,
                    x1_ref, h2_ref, lg_ref, y_a, y_b, *, alpha, n_tiles):
    i = pl.program_id(0)
    tm, d = y_a.shape
    chunks = 4
    cn, cr = d // chunks, tm // chunks

    def matmul_into(y_ref, c):
        cols = slice(c * cn, (c + 1) * cn)
        y_ref[:, cols] = (jnp.dot(oa_ref[...], wa_ref[:, cols], preferred_element_type=F32)
                          + jnp.dot(ob_ref[...], wb_ref[:, cols], preferred_element_type=F32))

    def epilogue(y_ref, c):
        rows = slice(c * cr, (c + 1) * cr)
        z = alpha * x_ref[rows, :] + mod_ref[0, 2:3, :] * y_ref[rows, :]
        x1 = _ln(z) * g_ref[...] + b_ref[...]
        x1_ref[rows, :] = x1
        h2 = _ln(x1) * (1.0 + mod_ref[0, 4:5, :]) + mod_ref[0, 3:4, :]
        h2_ref[rows, :] = h2
        h_hi = h2.astype(BF16)
        h_lo = (h2 - h_hi.astype(F32)).astype(BF16)
        zz = (jnp.dot(h_hi, wr_ref[...], preferred_element_type=F32)
              + jnp.dot(h_lo, wr_ref[...], preferred_element_type=F32))
        lg_ref[rows, :] = zz + pltpu.roll(zz, LANES // 2, 1)

    def step(y_new, y_old):
        for c in range(chunks):
            if y_new is not None:
                matmul_into(y_new, c)
            if y_old is not None:
                epilogue(y_old, c)

    inner = (i > 0) & (i < n_tiles)
    pl.when(i == 0)(lambda: step(y_a, None))
    pl.when(inner & (i % 2 == 0))(lambda: step(y_a, y_b))
    pl.when(inner & (i % 2 == 1))(lambda: step(y_b, y_a))
    pl.when(i == n_tiles)(lambda: step(None, y_b if n_tiles % 2 == 0 else y_a))


def _out_proj(o_mla, o_moba, wo, xf, mod3, ln_g, ln_b, wr, seq, alpha):
    t, d = xf.shape
    tm = 512
    n = t // tm
    per_b = seq // tm
    ka, kb = o_mla.shape[1], o_moba.shape[1]
    assert ka == kb and wo.shape[0] == ka + kb
    once = pl.Buffered(1)

    def cur(i):
        return (jnp.minimum(i, n - 1), 0)

    def lag(i):
        return (jnp.maximum(i - 1, 0), 0)

    return pl.pallas_call(
        functools.partial(_outproj_kernel, alpha=alpha, n_tiles=n),
        grid=(n + 1,),
        in_specs=[pl.BlockSpec((tm, ka), cur),
                  pl.BlockSpec((tm, kb), cur),
                  pl.BlockSpec((ka, d), lambda i: (0, 0), pipeline_mode=once),
                  pl.BlockSpec((kb, d), lambda i: (1, 0), pipeline_mode=once),
                  pl.BlockSpec((tm, d), lag),
                  pl.BlockSpec((1, 6, d), lambda i: (jnp.maximum(i - 1, 0) // per_b, 0, 0)),
                  pl.BlockSpec((1, d), lambda i: (0, 0)),
                  pl.BlockSpec((1, d), lambda i: (0, 0)),
                  pl.BlockSpec((d, LANES), lambda i: (0, 0), pipeline_mode=once)],
        out_specs=[pl.BlockSpec((tm, d), lag),
                   pl.BlockSpec((tm, d), lag),
                   pl.BlockSpec((tm, LANES), lag)],
        out_shape=[jax.ShapeDtypeStruct((t, d), F32),
                   jax.ShapeDtypeStruct((t, d), F32),
                   jax.ShapeDtypeStruct((t, LANES), F32)],
        scratch_shapes=[pltpu.VMEM((tm, d), F32), pltpu.VMEM((tm, d), F32)],
        compiler_params=_cparams(("arbitrary",)),
        name="out_proj",
    )(o_mla, o_moba, wo, wo, xf, mod3, ln_g, ln_b, wr)


def _route_kernel(lg_ref, br_ref, info_ref, cnt_ref, run_scr, *, tm):
    i = pl.program_id(0)

    @pl.when(i == 0)
    def _():
        run_scr[...] = jnp.zeros_like(run_scr)

    lg = lg_ref[...] + br_ref[...]
    lane = lax.broadcasted_iota(jnp.int32, (tm, LANES), 1)
    e_lo, e_hi = MOE_GROUPS, MOE_GROUPS + MOE_N_EXPERTS
    is_g = lane < e_lo
    gl = jnp.where(is_g, lg, NEG)
    gmax = jnp.max(gl, axis=-1, keepdims=True)
    gidx = jnp.min(jnp.where(gl == gmax, lane, LANES), axis=-1, keepdims=True)
    g_p = 1.0 / jnp.sum(jnp.where(is_g, jnp.exp(gl - gmax), 0.0), axis=-1, keepdims=True)
    grp_of_lane = jnp.right_shift(lane - e_lo, MOE_EXPERTS_PER_GROUP.bit_length() - 1)
    in_grp = (lane >= e_lo) & (lane < e_hi) & (grp_of_lane == gidx)
    el = jnp.where(in_grp, lg, NEG)
    m1 = jnp.max(el, axis=-1, keepdims=True)
    l1 = jnp.min(jnp.where(el == m1, lane, LANES), axis=-1, keepdims=True)
    el2 = jnp.where(lane == l1, NEG, el)
    m2 = jnp.max(el2, axis=-1, keepdims=True)
    l2 = jnp.min(jnp.where(el2 == m2, lane, LANES), axis=-1, keepdims=True)
    zsum = jnp.sum(jnp.where(in_grp, jnp.exp(el - m1), 0.0), axis=-1, keepdims=True)
    p1 = 1.0 / zsum
    p2 = jnp.exp(m2 - m1) / zsum
    wa = g_p * (p1 / (p1 + p2))
    wb = g_p * (p2 / (p1 + p2))
    hot_a = lane == l1
    hot_b = lane == l2
    onehot = jnp.where(hot_a | hot_b, 1.0, 0.0)
    r = lax.broadcasted_iota(jnp.int32, (tm, tm), 0)
    c = lax.broadcasted_iota(jnp.int32, (tm, tm), 1)
    lower = jnp.where(c < r, 1.0, 0.0).astype(BF16)
    before = jnp.dot(lower, onehot.astype(BF16), preferred_element_type=F32) + run_scr[...]
    rank_a = jnp.sum(jnp.where(hot_a, before, 0.0), axis=-1, keepdims=True)
    rank_b = jnp.sum(jnp.where(hot_b, before, 0.0), axis=-1, keepdims=True)
    run_scr[...] += jnp.sum(onehot, axis=0, keepdims=True)
    info = jnp.zeros((tm, LANES), F32)
    for k, val in enumerate([(l1 - e_lo).astype(F32), (l2 - e_lo).astype(F32), wa, wb, rank_a, rank_b]):
        info = jnp.where(lane == k, val, info)
    info_ref[...] = info
    cnt_ref[...] = run_scr[...]


def _route(logits, br):
    t = logits.shape[0]
    tm = 512
    return pl.pallas_call(
        functools.partial(_route_kernel, tm=tm),
        grid=(t // tm,),
        in_specs=[pl.BlockSpec((tm, LANES), lambda i: (i, 0)),
                  pl.BlockSpec((1, LANES), lambda i: (0, 0))],
        out_specs=[pl.BlockSpec((tm, LANES), lambda i: (i, 0)),
                   pl.BlockSpec((1, LANES), lambda i: (0, 0))],
        out_shape=[jax.ShapeDtypeStruct((t, LANES), F32),
                   jax.ShapeDtypeStruct((1, LANES), F32)],
        scratch_shapes=[pltpu.VMEM((1, LANES), F32)],
        compiler_params=_cparams(("arbitrary",)),
        name="route",
    )(logits, br)


def _pos_kernel(info_ref, start_ref, pos_ref):
    info = info_ref[...]
    tm = info.shape[0]
    lane = lax.broadcasted_iota(jnp.int32, (tm, LANES), 1)
    start = start_ref[...]
    cols = []
    for k in range(2):
        e = jnp.sum(jnp.where(lane == k, info, 0.0), axis=-1, keepdims=True).astype(jnp.int32)
        rank = jnp.sum(jnp.where(lane == 4 + k, info, 0.0), axis=-1, keepdims=True)
        base = jnp.sum(jnp.where(lane == e + MOE_GROUPS, start, 0.0), axis=-1, keepdims=True)
        cols.append(base + rank)
    wide = jnp.where(lane == 0, cols[0], jnp.where(lane == 1, cols[1], 0.0))
    pos_ref[...] = wide.T[:pos_ref.shape[0], :].astype(jnp.int32)


def _positions(info, start_lanes):
    t = info.shape[0]
    tm = 1024
    return pl.pallas_call(
        _pos_kernel,
        grid=(t // tm,),
        in_specs=[pl.BlockSpec((tm, LANES), lambda i: (i, 0)),
                  pl.BlockSpec((1, LANES), lambda i: (0, 0))],
        out_specs=pl.BlockSpec((8, tm), lambda i: (0, i)),
        out_shape=jax.ShapeDtypeStruct((8, t), jnp.int32),
        compiler_params=_cparams(("arbitrary",)),
        name="positions",
    )(info, start_lanes)


def _row_copy(src_ref, src_row, dst_ref, dst_row, sem):
    return pltpu.make_async_copy(src_ref.at[pl.ds(src_row, 1)], dst_ref.at[pl.ds(dst_row, 1)], sem)


def _dispatch_kernel(pa_ref, pb_ref, pad0_ref, padn_ref, used_ref, h_ref, xs_ref, ztile, sem, zsem,
                     *, tm, tr, n_tiles):
    i = pl.program_id(0)
    base = i * tm

    def zero_fill(act):
        def whole_tile(j, _):
            act(pltpu.make_async_copy(ztile, xs_ref.at[pl.ds(pl.multiple_of(j * tr, tr), tr)], zsem))
            return 0

        lax.fori_loop(used_ref[0], n_tiles, whole_tile, 0)

        def expert_pad(e, _):
            n, start = padn_ref[e], pad0_ref[e]
            head = jnp.minimum(jnp.bitwise_and(-start, SUBLANES - 1), n)
            for k in range(SUBLANES - 1):
                pl.when(k < head)(functools.partial(act, _row_copy(ztile, 0, xs_ref, start + k, zsem)))
            off = start + head
            groups = jnp.right_shift(n - head, SUBLANES.bit_length() - 1)
            for bit in reversed(range((tr // SUBLANES - 1).bit_length())):
                size = SUBLANES << bit
                take = jnp.bitwise_and(jnp.right_shift(groups, bit), 1)
                dst = xs_ref.at[pl.ds(pl.multiple_of(off, SUBLANES), size)]
                pl.when(take == 1)(functools.partial(act, pltpu.make_async_copy(ztile.at[pl.ds(0, size)], dst, zsem)))
                off = off + take * size
            return 0

        lax.fori_loop(0, MOE_N_EXPERTS, expert_pad, 0)

    @pl.when(i == 0)
    def _():
        ztile[...] = jnp.zeros(ztile.shape, ztile.dtype)
        zero_fill(lambda cp: cp.start())

    def issue(t, _):
        _row_copy(h_ref, t, xs_ref, pa_ref[base + t], sem).start()
        _row_copy(h_ref, t, xs_ref, pb_ref[base + t], sem).start(priority=1)
        return 0

    lax.fori_loop(0, tm, issue, 0, unroll=8)
    for _ in range(2):
        pltpu.make_async_copy(h_ref, xs_ref.at[pl.ds(0, tm)], sem).wait()

    @pl.when(i == pl.num_programs(0) - 1)
    def _():
        zero_fill(lambda cp: cp.wait())


def _dispatch(pos_a, pos_b, pad0, padn, used, h2, tr, n_tiles):
    t, d = h2.shape
    tm = 256
    grid_spec = pltpu.PrefetchScalarGridSpec(
        num_scalar_prefetch=5,
        grid=(t // tm,),
        in_specs=[pl.BlockSpec((tm, d), lambda i, *_: (i, 0))],
        out_specs=pl.BlockSpec(memory_space=pl.ANY),
        scratch_shapes=[pltpu.VMEM((tr, d), F32), pltpu.SemaphoreType.DMA(()), pltpu.SemaphoreType.DMA(())],
    )
    return pl.pallas_call(
        functools.partial(_dispatch_kernel, tm=tm, tr=tr, n_tiles=n_tiles),
        grid_spec=grid_spec,
        out_shape=jax.ShapeDtypeStruct((n_tiles * tr, d), F32),
        compiler_params=_cparams(("arbitrary",), row_dma=True),
        name="dispatch",
    )(pos_a, pos_b, pad0, padn, used, h2)


def _experts_kernel(texp_ref, tidx_ref, nexte_ref, used_ref, x_ref, w1_hbm, w3_hbm, w2_hbm, o_ref,
                    w1_f32, w3_f32, w2_f32, w1_scr, w3_scr, w2_scr, sems):
    i = pl.program_id(0)
    prev = texp_ref[jnp.maximum(i - 1, 0)]

    def weight_copies(e):
        return [pltpu.make_async_copy(w1_hbm.at[e], w1_f32, sems.at[0]),
                pltpu.make_async_copy(w3_hbm.at[e], w3_f32, sems.at[1]),
                pltpu.make_async_copy(w2_hbm.at[e], w2_f32, sems.at[2])]

    @pl.when(i == 0)
    def _():
        for cp in weight_copies(texp_ref[0]):
            cp.start()

    @pl.when((i == 0) | (texp_ref[i] != prev))
    def _():
        for cp in weight_copies(texp_ref[i]):
            cp.wait()
        w1_scr[...] = w1_f32[...].astype(BF16)
        w3_scr[...] = w3_f32[...].astype(BF16)
        w2_scr[...] = w2_f32[...].astype(BF16)

        @pl.when(nexte_ref[i] >= 0)
        def _():
            for cp in weight_copies(nexte_ref[i]):
                cp.start(priority=1)

    @pl.when(i < used_ref[0])
    def _():
        x = x_ref[...].astype(BF16)
        a = jnp.dot(x, w1_scr[...], preferred_element_type=F32)
        b = jnp.dot(x, w3_scr[...], preferred_element_type=F32)
        hid = (a / (1.0 + jnp.exp(-a))) * b
        o_ref[...] = jnp.dot(hid.astype(BF16), w2_scr[...], preferred_element_type=F32)

    @pl.when(i >= used_ref[0])
    def _():
        o_ref[...] = jnp.zeros_like(o_ref)


def _experts(texp, tidx, nexte, used, xs, w1, w3, w2, tr):
    rows, d = xs.shape
    nt = rows // tr
    f = w1.shape[2]
    grid_spec = pltpu.PrefetchScalarGridSpec(
        num_scalar_prefetch=4,
        grid=(nt,),
        in_specs=[pl.BlockSpec((tr, d), lambda i, te, ti, ne, u: (ti[i], 0)),
                  pl.BlockSpec(memory_space=pl.ANY),
                  pl.BlockSpec(memory_space=pl.ANY),
                  pl.BlockSpec(memory_space=pl.ANY)],
        out_specs=pl.BlockSpec((tr, d), lambda i, te, ti, ne, u: (i, 0)),
        scratch_shapes=[pltpu.VMEM((d, f), F32), pltpu.VMEM((d, f), F32), pltpu.VMEM((f, d), F32),
                        pltpu.VMEM((d, f), BF16), pltpu.VMEM((d, f), BF16), pltpu.VMEM((f, d), BF16),
                        pltpu.SemaphoreType.DMA((3,))],
    )
    return pl.pallas_call(
        _experts_kernel,
        grid_spec=grid_spec,
        out_shape=jax.ShapeDtypeStruct((rows, d), F32),
        compiler_params=_cparams(("arbitrary",)),
        name="experts",
    )(texp, tidx, nexte, used, xs, w1, w3, w2)


def _combine_kernel(pa_ref, pb_ref, ys_ref, x1_ref, info_ref, mod_ref, g_ref, b_ref, o_ref,
                    buf_a, buf_b, sems, *, tm, alpha):
    i = pl.program_id(0)

    def gather(tile, slot):
        base = tile * tm

        def issue(t, _):
            _row_copy(ys_ref, pa_ref[base + t], buf_a.at[slot], t, sems.at[slot]).start()
            _row_copy(ys_ref, pb_ref[base + t], buf_b.at[slot], t, sems.at[slot]).start(priority=1)
            return 0

        lax.fori_loop(0, tm, issue, 0, unroll=8)

    @pl.when(i == 0)
    def _():
        gather(0, 0)

    @pl.when(i + 1 < pl.num_programs(0))
    def _():
        gather(i + 1, (i + 1) % 2)

    slot = i % 2
    for buf in (buf_a, buf_b):
        pltpu.make_async_copy(ys_ref.at[pl.ds(0, tm)], buf.at[slot], sems.at[slot]).wait()
    info = info_ref[...]
    y = info[:, 2:3] * buf_a[slot] + info[:, 3:4] * buf_b[slot]
    z = alpha * x1_ref[...] + mod_ref[0, 5:6, :] * y
    o_ref[...] = _ln(z) * g_ref[...] + b_ref[...]


def _combine(pos_a, pos_b, ys, x1, info, mod3, ln_g, ln_b, seq, alpha):
    t, d = x1.shape
    tm = 256
    per_b = seq // tm
    grid_spec = pltpu.PrefetchScalarGridSpec(
        num_scalar_prefetch=2,
        grid=(t // tm,),
        in_specs=[pl.BlockSpec(memory_space=pl.ANY),
                  pl.BlockSpec((tm, d), lambda i, pa, pb: (i, 0)),
                  pl.BlockSpec((tm, LANES), lambda i, pa, pb: (i, 0)),
                  pl.BlockSpec((1, 6, d), lambda i, pa, pb: (i // per_b, 0, 0)),
                  pl.BlockSpec((1, d), lambda i, pa, pb: (0, 0)),
                  pl.BlockSpec((1, d), lambda i, pa, pb: (0, 0))],
        out_specs=pl.BlockSpec((tm, d), lambda i, pa, pb: (i, 0)),
        scratch_shapes=[pltpu.VMEM((2, tm, d), F32), pltpu.VMEM((2, tm, d), F32), pltpu.SemaphoreType.DMA((2,))],
    )
    return pl.pallas_call(
        functools.partial(_combine_kernel, tm=tm, alpha=alpha),
        grid_spec=grid_spec,
        out_shape=jax.ShapeDtypeStruct((t, d), F32),
        compiler_params=_cparams(("arbitrary",), row_dma=True),
        name="combine",
    )(pos_a, pos_b, ys, x1, info, mod3, ln_g, ln_b)


def _prep_w_in_kernel(w_ref, o_ref):
    w = w_ref[...].astype(BF16)
    r0 = MLA_Q_RANK + MLA_KV_RANK
    kr = w[:, r0:r0 + MLA_ROPE_DIM]
    half = MLA_ROPE_DIM // 2
    z = jnp.zeros((w.shape[0], LANES - MLA_ROPE_DIM), BF16)
    o_ref[:, :r0] = w[:, :r0]
    o_ref[:, r0:PART_A] = jnp.concatenate([kr, z, kr[:, half:], kr[:, :half], z], axis=1)
    o_ref[:, PART_A:] = w[:, r0 + MLA_ROPE_DIM:]


def _prep_w_in(w):
    k, n = w.shape
    tk = 256
    n_out = PART_A + 3 * MOBA_WIDTH
    assert n == MLA_Q_RANK + MLA_KV_RANK + MLA_ROPE_DIM + 3 * MOBA_WIDTH
    return pl.pallas_call(
        _prep_w_in_kernel,
        grid=(k // tk,),
        in_specs=[pl.BlockSpec((tk, n), lambda i: (i, 0))],
        out_specs=pl.BlockSpec((tk, n_out), lambda i: (i, 0)),
        out_shape=jax.ShapeDtypeStruct((k, n_out), BF16),
        compiler_params=_cparams(("arbitrary",)),
        name="prep_w_in",
    )(w)


def _prep_w_uq(w):
    r = w.shape[0]
    w = w.reshape(r, MLA_HEADS, MLA_NOPE_DIM + MLA_ROPE_DIM)
    half = MLA_ROPE_DIM // 2
    nope = w[:, :, :MLA_NOPE_DIM]
    x1 = w[:, :, MLA_NOPE_DIM:MLA_NOPE_DIM + half]
    x2 = w[:, :, MLA_NOPE_DIM + half:]
    z = jnp.zeros((r, MLA_HEADS, LANES - MLA_ROPE_DIM), w.dtype)
    return jnp.concatenate([nope, x1, x2, z, x2, x1, z], axis=2).reshape(r, MLA_HEADS * 3 * LANES).astype(BF16)


def _rope_lanes(seq):
    inv = 1.0 / (ROPE_THETA ** (jnp.arange(0, MLA_ROPE_DIM, 2, dtype=F32) / MLA_ROPE_DIM))
    ang = jnp.arange(seq, dtype=F32)[:, None] * inv[None, :]
    cos, sin = jnp.cos(ang), jnp.sin(ang)
    z = jnp.zeros((seq, LANES - MLA_ROPE_DIM), F32)
    return jnp.concatenate([cos, cos, z], axis=1), jnp.concatenate([-sin, sin, z], axis=1)


def _prep_router(w_rg, b_rg, w_re, b_re):
    d = w_rg.shape[0]
    w = jnp.concatenate([w_rg, w_re], axis=1)
    n = w.shape[1]
    hi = w.astype(BF16)
    lo = (w - hi.astype(F32)).astype(BF16)
    z = jnp.zeros((d, LANES // 2 - n), BF16)
    wr = jnp.concatenate([hi, z, lo, z], axis=1)
    br = jnp.zeros((1, LANES), F32).at[0, :n].set(jnp.concatenate([b_rg, b_re]))
    return wr, br


def _layer(xf, mod3, batch, seq, depth_alpha, w_in, q_norm_g, w_uq, kv_norm_g, w_ukv, w_out, bias_tiles,
           cos_t, sin_t, ln1_g, ln1_b, w_rg, b_rg, w_re, b_re, w1, w3, w2, ln2_g, ln2_b):
    t, d = xf.shape
    part_a, qkv_mo = _in_proj(xf, mod3, _prep_w_in(w_in), seq)
    q, k, v = _qkv(part_a, _prep_w_uq(w_uq), w_ukv.astype(BF16), q_norm_g.reshape(1, -1),
                   kv_norm_g.reshape(1, -1), cos_t, sin_t, seq)
    o_mla = _mla_attention(q, k, v, batch, seq)
    q_wide, k_wide, v_wide = _moba_select(qkv_mo, batch, seq)
    o_moba = _moba_attention(q_wide, k_wide, v_wide, bias_tiles, batch, seq)
    wo = w_out.astype(BF16)
    wr, br = _prep_router(w_rg, b_rg, w_re, b_re)
    x1, h2, logits = _out_proj(o_mla, o_moba, wo, xf, mod3, ln1_g.reshape(1, d),
                               ln1_b.reshape(1, d), wr, seq, depth_alpha)
    info, counts = _route(logits, br)
    tr = 256
    nt = (2 * t) // tr + MOE_N_EXPERTS
    cnt = counts[0, MOE_GROUPS:MOE_GROUPS + MOE_N_EXPERTS].astype(jnp.int32)
    ntile = (cnt + tr - 1) // tr
    tile_end = jnp.cumsum(ntile)
    tile_start = tile_end - ntile
    used = tile_end[-1]
    start_lanes = jnp.zeros((1, LANES), F32).at[0, MOE_GROUPS:MOE_GROUPS + MOE_N_EXPERTS].set(
        (tile_start * tr).astype(F32))
    pos = _positions(info, start_lanes)
    pos_a, pos_b = pos[0], pos[1]
    tidx = jnp.minimum(jnp.arange(nt, dtype=jnp.int32), used - 1)
    texp = jnp.sum(tidx[:, None] >= tile_end[None, :], axis=1).astype(jnp.int32)
    eids = jnp.arange(MOE_N_EXPERTS, dtype=jnp.int32)
    later = (eids[None, :] > eids[:, None]) & (ntile[None, :] > 0)
    next_nonempty = jnp.where(later.any(axis=1), jnp.argmax(later, axis=1), -1).astype(jnp.int32)
    nexte = jnp.sum(jnp.where(texp[:, None] == eids[None, :], next_nonempty[None, :], 0), axis=1).astype(jnp.int32)
    used1 = used.reshape(1).astype(jnp.int32)
    pad0 = (tile_start * tr + cnt).astype(jnp.int32)
    padn = (ntile * tr - cnt).astype(jnp.int32)
    xs = _dispatch(pos_a, pos_b, pad0, padn, used1, h2, tr, nt)
    ys = _experts(texp, tidx, nexte, used1, xs, w1, w3, w2, tr)
    return _combine(pos_a, pos_b, ys, x1, info, mod3, ln2_g.reshape(1, d), ln2_b.reshape(1, d), seq, depth_alpha)


def kernel(x, c, w_ada, b_ada, w_in, q_norm_g, w_uq, kv_norm_g, w_ukv, w_out, t5_table, ln1_g, ln1_b,
           w_router_group, b_router_group, w_router_expert, b_router_expert, w1, w3, w2, ln2_g, ln2_b):
    batch, seq, d = x.shape
    depth = w_ada.shape[0]
    alpha = (2.0 * depth) ** 0.25
    cos_t, sin_t = _rope_lanes(seq)
    bias_tiles = _t5_tiles(t5_table)
    xf = x.reshape(batch * seq, d)
    for l in range(depth):
        mod3 = _ada_mod(c, w_ada[l], b_ada[l]).reshape(batch, 6, d)
        xf = _layer(xf, mod3, batch, seq, alpha, w_in[l], q_norm_g[l], w_uq[l], kv_norm_g[l], w_ukv[l],
                    w_out[l], bias_tiles, cos_t, sin_t, ln1_g[l], ln1_b[l], w_router_group[l],
                    b_router_group[l], w_router_expert[l], b_router_expert[l], w1[l], w3[l], w2[l],
                    ln2_g[l], ln2_b[l])
    return xf.reshape(batch, seq, d)
```

```python
import functools
import math

import jax
import jax.numpy as jnp
from jax import lax
from jax.experimental import pallas as pl
from jax.experimental.pallas import tpu as pltpu

D_MODEL = 2048
MLA_HEADS = 8
MLA_Q_RANK = 512
MLA_KV_RANK = 256
MLA_NOPE_DIM = 128
MLA_ROPE_DIM = 64
MLA_V_DIM = 128
ROPE_THETA = 10000.0
MOBA_HEADS = 8
MOBA_HEAD_DIM = 128
MOBA_BLOCK = 256
MOBA_TOPK = 3
T5_BUCKETS = 32
T5_MAX_DISTANCE = 128
MOE_GROUPS = 4
MOE_EXPERTS_PER_GROUP = 8
MOE_N_EXPERTS = MOE_GROUPS * MOE_EXPERTS_PER_GROUP
MOE_D_FF = 512
LN_EPS = 1e-5
RMS_EPS = 1e-6
MOBA_WIDTH = MOBA_HEADS * MOBA_HEAD_DIM

LANES = 128
SUBLANES = 8
QK_PAD = 256
V_PAD = 256
PART_A = 1024
NEG = -1e30
VMEM_LIMIT = 56 * 1024 * 1024

F32 = jnp.float32
BF16 = jnp.bfloat16


def _cparams(sem, row_dma=False):
    return pltpu.CompilerParams(dimension_semantics=sem, vmem_limit_bytes=VMEM_LIMIT,
                                disable_bounds_checks=row_dma)


def _ln(x):
    mu = jnp.mean(x, axis=-1, keepdims=True)
    xc = x - mu
    var = jnp.mean(xc * xc, axis=-1, keepdims=True)
    return xc * lax.rsqrt(var + LN_EPS)


def _nt_dot(a, b):
    return lax.dot_general(a, b, (((1,), (1,)), ((), ())), preferred_element_type=F32)


def _ada_kernel(ct_ref, w_ref, b_ref, o_ref, *, batch):
    ct = ct_ref[...]
    ca = ct / (1.0 + jnp.exp(-ct))
    w = w_ref[...]
    rows = [jnp.sum(w * ca[:, b:b + 1], axis=0, keepdims=True) for b in range(batch)]
    o_ref[...] = jnp.concatenate(rows, axis=0) + b_ref[...]


def _ada_mod(c, w_ada, b_ada):
    batch, d = c.shape
    n = w_ada.shape[1]
    tn = 512
    ct = jnp.zeros((d, LANES), F32).at[:, :batch].set(c.T)
    return pl.pallas_call(
        functools.partial(_ada_kernel, batch=batch),
        grid=(n // tn,),
        in_specs=[pl.BlockSpec((d, LANES), lambda j: (0, 0)),
                  pl.BlockSpec((d, tn), lambda j: (0, j)),
                  pl.BlockSpec((1, tn), lambda j: (0, j))],
        out_specs=pl.BlockSpec((batch, tn), lambda j: (0, j)),
        out_shape=jax.ShapeDtypeStruct((batch, n), F32),
        compiler_params=_cparams(("arbitrary",)),
        name="ada_mod",
    )(ct, w_ada, b_ada.reshape(1, n))


def _inproj_kernel(x_ref, mod_ref, w_ref, a_ref, b_ref, h_scr):
    j = pl.program_id(1)

    @pl.when(j == 0)
    def _():
        h = _ln(x_ref[...]) * (1.0 + mod_ref[0, 1:2, :]) + mod_ref[0, 0:1, :]
        h_scr[...] = h.astype(BF16)
        a_ref[...] = _nt_dot(h_scr[...], w_ref[...])

    @pl.when(j > 0)
    def _():
        b_ref[...] = _nt_dot(h_scr[...], w_ref[...]).astype(BF16)


def _in_proj(xf, mod3, w4, seq):
    t, d = xf.shape
    n = w4.shape[0]
    tm, tn = 512, PART_A
    per_b = seq // tm
    return pl.pallas_call(
        _inproj_kernel,
        grid=(t // tm, n // tn),
        in_specs=[pl.BlockSpec((tm, d), lambda i, j: (i, 0)),
                  pl.BlockSpec((1, 6, d), lambda i, j: (i // per_b, 0, 0)),
                  pl.BlockSpec((tn, d), lambda i, j: (j, 0))],
        out_specs=[pl.BlockSpec((tm, tn), lambda i, j: (i, 0)),
                   pl.BlockSpec((tm, tn), lambda i, j: (i, jnp.maximum(j - 1, 0)))],
        out_shape=[jax.ShapeDtypeStruct((t, PART_A), F32),
                   jax.ShapeDtypeStruct((t, n - PART_A), BF16)],
        scratch_shapes=[pltpu.VMEM((tm, d), BF16)],
        compiler_params=_cparams(("arbitrary", "arbitrary")),
        name="in_proj",
    )(xf, mod3, w4)


def _qkv_kernel(a_ref, wq_ref, wkv_ref, gq_ref, gkv_ref, cos_ref, sin_ref, q_ref, k_ref, v_ref):
    a = a_ref[...]
    cq = a[:, :MLA_Q_RANK]
    ckv = a[:, MLA_Q_RANK:MLA_Q_RANK + MLA_KV_RANK]
    kr = a[:, 768:896]
    krs = a[:, 896:1024]
    cqn = (cq * lax.rsqrt(jnp.mean(cq * cq, axis=-1, keepdims=True) + RMS_EPS) * gq_ref[...]).astype(BF16)
    ckvn = (ckv * lax.rsqrt(jnp.mean(ckv * ckv, axis=-1, keepdims=True) + RMS_EPS) * gkv_ref[...]).astype(BF16)
    q3 = jnp.dot(cqn, wq_ref[...], preferred_element_type=F32)
    kv = jnp.dot(ckvn, wkv_ref[...], preferred_element_type=F32)
    cos = cos_ref[...]
    sin = sin_ref[...]
    krr = (kr * cos + krs * sin).astype(BF16)
    for h in range(MLA_HEADS):
        q0 = h * 3 * LANES
        c0 = h * QK_PAD
        q_ref[:, c0:c0 + LANES] = q3[:, q0:q0 + LANES].astype(BF16)
        q_ref[:, c0 + LANES:c0 + QK_PAD] = (
            q3[:, q0 + LANES:q0 + 2 * LANES] * cos + q3[:, q0 + 2 * LANES:q0 + 3 * LANES] * sin).astype(BF16)
        k_ref[:, c0:c0 + LANES] = kv[:, c0:c0 + LANES].astype(BF16)
        k_ref[:, c0 + LANES:c0 + QK_PAD] = krr
        v_ref[:, c0:c0 + LANES] = kv[:, c0 + LANES:c0 + QK_PAD].astype(BF16)
        v_ref[:, c0 + LANES:c0 + V_PAD] = jnp.ones((a.shape[0], V_PAD - LANES), BF16)


def _qkv(part_a, wq3, wkv, gq, gkv, cos_t, sin_t, seq):
    t = part_a.shape[0]
    tm = 512
    per_b = seq // tm
    hq = MLA_HEADS * QK_PAD
    return pl.pallas_call(
        _qkv_kernel,
        grid=(t // tm,),
        in_specs=[pl.BlockSpec((tm, PART_A), lambda i: (i, 0)),
                  pl.BlockSpec(wq3.shape, lambda i: (0, 0)),
                  pl.BlockSpec(wkv.shape, lambda i: (0, 0)),
                  pl.BlockSpec((1, MLA_Q_RANK), lambda i: (0, 0)),
                  pl.BlockSpec((1, MLA_KV_RANK), lambda i: (0, 0)),
                  pl.BlockSpec((tm, LANES), lambda i: (i % per_b, 0)),
                  pl.BlockSpec((tm, LANES), lambda i: (i % per_b, 0))],
        out_specs=[pl.BlockSpec((tm, hq), lambda i: (i, 0)),
                   pl.BlockSpec((tm, hq), lambda i: (i, 0)),
                   pl.BlockSpec((tm, MLA_HEADS * V_PAD), lambda i: (i, 0))],
        out_shape=[jax.ShapeDtypeStruct((t, hq), BF16),
                   jax.ShapeDtypeStruct((t, hq), BF16),
                   jax.ShapeDtypeStruct((t, MLA_HEADS * V_PAD), BF16)],
        compiler_params=_cparams(("arbitrary",)),
        name="qkv",
    )(part_a, wq3, wkv, gq, gkv, cos_t, sin_t)


ATTN_TILE = 512


def _attn_kernel(*refs, c, masked):
    if masked:
        q_ref, k_ref, v_ref, bias_ref, o_ref, s_a, s_b, m_scr, acc_scr = refs
    else:
        q_ref, k_ref, v_ref, o_ref, s_a, s_b, m_scr, acc_scr = refs
    tile = ATTN_TILE
    qi = pl.program_id(2)

    def put_scores(buf, t):
        k = k_ref[pl.ds(pl.multiple_of(t * tile, tile), tile), :]
        buf[...] = _nt_dot(q_ref[...], k)

    def tile_step(buf, t, kind, nxt=None):
        if nxt is not None:
            put_scores(*nxt)
        s = buf[...]
        if masked:
            if kind == "prev":
                s = s + bias_ref[0, 1] + jnp.where(qi == 0, NEG, 0.0)
            elif kind == "diag":
                s = s + bias_ref[0, 0]
        elif kind == "diag":
            row = lax.broadcasted_iota(jnp.int32, (tile, tile), 0)
            col = lax.broadcasted_iota(jnp.int32, (tile, tile), 1)
            s = jnp.where(row >= col, s, NEG)
        m_old = m_scr[...]
        m_new = jnp.maximum(m_old, jnp.broadcast_to(jnp.max(s, axis=-1, keepdims=True), m_old.shape))
        alpha = jnp.exp2((m_old - m_new) * c)
        p = jnp.concatenate([jnp.exp2(((s[:, j * LANES:(j + 1) * LANES] - m_new) * c).astype(BF16))
                             for j in range(tile // LANES)], axis=1)
        v = v_ref[pl.ds(pl.multiple_of(t * tile, tile), tile), :]
        pv = jnp.dot(p, v, preferred_element_type=F32)
        acc_scr[...] = jnp.concatenate([alpha, alpha], axis=1) * acc_scr[...] + pv
        m_scr[...] = m_new

    m_scr[...] = jnp.full(m_scr.shape, NEG, F32)
    acc_scr[...] = jnp.zeros(acc_scr.shape, F32)
    put_scores(s_a, 0)

    n_far = jnp.maximum(qi - 1, 0) if masked else qi
    pn = jnp.maximum(qi - 1, 0)

    def pair(j, _):
        t = 2 * j
        tile_step(s_a, t, "far", (s_b, t + 1))
        tile_step(s_b, t + 1, "far", (s_a, t + 2))
        return 0

    lax.fori_loop(0, n_far // 2, pair, 0)
    t0 = 2 * (n_far // 2)

    def tail(first, second):
        if masked:
            tile_step(first, pn, "prev", (second, qi))
            tile_step(second, qi, "diag")
        else:
            tile_step(first, qi, "diag")

    @pl.when(n_far % 2 == 1)
    def _():
        tile_step(s_a, t0, "far", (s_b, t0 + 1))
        tail(s_b, s_a)

    @pl.when(n_far % 2 == 0)
    def _():
        tail(s_a, s_b)

    o_ref[...] = (acc_scr[:, :LANES] / acc_scr[:, LANES:]).astype(o_ref.dtype)


def _attn_scratch():
    tile = ATTN_TILE
    return [pltpu.VMEM((tile, tile), F32), pltpu.VMEM((tile, tile), F32),
            pltpu.VMEM((tile, LANES), F32), pltpu.VMEM((tile, V_PAD), F32)]


def _mla_attention(q, k, v, batch, seq):
    tile = ATTN_TILE
    nq = seq // tile
    c = math.log2(math.e) / math.sqrt(MLA_NOPE_DIM + MLA_ROPE_DIM)
    return pl.pallas_call(
        functools.partial(_attn_kernel, c=c, masked=False),
        grid=(batch, MLA_HEADS, nq),
        in_specs=[pl.BlockSpec((tile, QK_PAD), lambda b, h, i: (b * nq + i, h)),
                  pl.BlockSpec((seq, QK_PAD), lambda b, h, i: (b, h)),
                  pl.BlockSpec((seq, V_PAD), lambda b, h, i: (b, h))],
        out_specs=pl.BlockSpec((tile, MLA_V_DIM), lambda b, h, i: (b * nq + i, h)),
        out_shape=jax.ShapeDtypeStruct((batch * seq, MLA_HEADS * MLA_V_DIM), BF16),
        scratch_shapes=_attn_scratch(),
        compiler_params=_cparams(("arbitrary", "arbitrary", "arbitrary")),
        name="mla_attn",
    )(q, k, v)


def _moba_select_kernel(q_ref, k_ref, v_ref, qa_ref, ka_ref, va_ref, *, seq, nb):
    kf = k_ref[...].astype(F32)
    km = jnp.sum(kf.reshape(nb, MOBA_BLOCK, MOBA_HEAD_DIM), axis=1) * (1.0 / MOBA_BLOCK)
    km_hi = km.astype(BF16)
    km_lo = (km - km_hi.astype(F32)).astype(BF16)
    q = q_ref[...]
    gate = _nt_dot(km_hi, q) + _nt_dot(km_lo, q)
    shift = MOBA_BLOCK.bit_length() - 1
    blk = lax.broadcasted_iota(jnp.int32, (nb, seq), 0)
    qblk = jnp.right_shift(lax.broadcasted_iota(jnp.int32, (nb, seq), 1), shift)
    g = jnp.where(blk < qblk, gate, NEG)
    visible = blk == qblk
    for _ in range(MOBA_TOPK):
        mx = jnp.max(g, axis=0, keepdims=True)
        first = jnp.min(jnp.where(g == mx, blk, nb), axis=0, keepdims=True)
        pick = (blk == first) & (mx > 0.5 * NEG)
        visible = visible | pick
        g = jnp.where(pick, NEG, g)
    mask_t = jnp.concatenate([jnp.where(visible, 0.0, NEG), jnp.zeros((LANES - nb, seq), F32)], axis=0)
    qa_ref[:, :MOBA_HEAD_DIM] = q
    qa_ref[:, MOBA_HEAD_DIM:] = mask_t.T.astype(BF16)
    lane = lax.broadcasted_iota(jnp.int32, (seq, LANES), 1)
    own = jnp.right_shift(lax.broadcasted_iota(jnp.int32, (seq, LANES), 0), shift)
    ka_ref[:, :MOBA_HEAD_DIM] = k_ref[...]
    ka_ref[:, MOBA_HEAD_DIM:] = jnp.where(lane == own, 1.0, 0.0).astype(BF16)
    va_ref[:, :MOBA_HEAD_DIM] = v_ref[...]
    va_ref[:, MOBA_HEAD_DIM:] = jnp.ones((seq, V_PAD - MOBA_HEAD_DIM), BF16)


def _moba_select(qkv_mo, batch, seq):
    nb = seq // MOBA_BLOCK
    assert MOBA_HEAD_DIM == LANES and nb <= QK_PAD - MOBA_HEAD_DIM
    wide = jax.ShapeDtypeStruct((batch * seq, MOBA_HEADS * QK_PAD), BF16)
    return pl.pallas_call(
        functools.partial(_moba_select_kernel, seq=seq, nb=nb),
        grid=(batch, MOBA_HEADS),
        in_specs=[pl.BlockSpec((seq, MOBA_HEAD_DIM), lambda b, h: (b, h)),
                  pl.BlockSpec((seq, MOBA_HEAD_DIM), lambda b, h: (b, MOBA_HEADS + h)),
                  pl.BlockSpec((seq, MOBA_HEAD_DIM), lambda b, h: (b, 2 * MOBA_HEADS + h))],
        out_specs=[pl.BlockSpec((seq, QK_PAD), lambda b, h: (b, h)),
                   pl.BlockSpec((seq, QK_PAD), lambda b, h: (b, h)),
                   pl.BlockSpec((seq, V_PAD), lambda b, h: (b, h))],
        out_shape=[wide, wide, jax.ShapeDtypeStruct((batch * seq, MOBA_HEADS * V_PAD), BF16)],
        compiler_params=_cparams(("arbitrary", "arbitrary")),
        name="moba_select",
    )(qkv_mo, qkv_mo, qkv_mo)


def _t5_kernel(tab_ref, o_ref, *, inv_scale):
    h = pl.program_id(0)
    r = lax.broadcasted_iota(jnp.int32, (LANES, LANES), 0)
    c = lax.broadcasted_iota(jnp.int32, (LANES, LANES), 1)
    max_exact = T5_BUCKETS // 2
    far = tab_ref[T5_BUCKETS - 1, h]

    def block(offset):
        rel = offset + r - c
        n = jnp.maximum(rel, 0)
        nf = jnp.maximum(n, 1).astype(F32)
        large = max_exact + (jnp.log(nf / max_exact) / math.log(T5_MAX_DISTANCE / max_exact)
                             * (T5_BUCKETS - max_exact)).astype(jnp.int32)
        large = jnp.minimum(large, T5_BUCKETS - 1)
        bucket = jnp.where(n < max_exact, n, large)
        bias = jnp.zeros((LANES, LANES), F32)
        for j in range(T5_BUCKETS):
            bias = jnp.where(bucket == j, tab_ref[j, h], bias)
        return jnp.where(rel >= 0, (bias - far) * inv_scale, NEG)

    near = {0: block(0), 1: block(LANES)}
    nblk = ATTN_TILE // LANES
    for d in range(2):
        for i in range(nblk):
            for j in range(nblk):
                k = d * nblk + i - j
                if k < 0:
                    val = jnp.full((LANES, LANES), NEG, F32)
                else:
                    val = near.get(k, jnp.zeros((LANES, LANES), F32))
                o_ref[0, d, i * LANES:(i + 1) * LANES, j * LANES:(j + 1) * LANES] = val


def _t5_tiles(t5_table):
    assert LANES >= T5_MAX_DISTANCE
    tile = ATTN_TILE
    return pl.pallas_call(
        functools.partial(_t5_kernel, inv_scale=math.sqrt(MOBA_HEAD_DIM)),
        grid=(MOBA_HEADS,),
        in_specs=[pl.BlockSpec(memory_space=pltpu.SMEM)],
        out_specs=pl.BlockSpec((1, 2, tile, tile), lambda h: (h, 0, 0, 0)),
        out_shape=jax.ShapeDtypeStruct((MOBA_HEADS, 2, tile, tile), F32),
        compiler_params=_cparams(("arbitrary",)),
        name="t5_tiles",
    )(t5_table)


def _moba_attention(q_wide, k_wide, v_wide, bias, batch, seq):
    tile = ATTN_TILE
    nq = seq // tile
    c = math.log2(math.e) / math.sqrt(MOBA_HEAD_DIM)
    dh = MOBA_HEAD_DIM
    return pl.pallas_call(
        functools.partial(_attn_kernel, c=c, masked=True),
        grid=(batch, MOBA_HEADS, nq),
        in_specs=[pl.BlockSpec((tile, QK_PAD), lambda b, h, i: (b * nq + i, h)),
                  pl.BlockSpec((seq, QK_PAD), lambda b, h, i: (b, h)),
                  pl.BlockSpec((seq, V_PAD), lambda b, h, i: (b, h)),
                  pl.BlockSpec((1, 2, tile, tile), lambda b, h, i: (h, 0, 0, 0))],
        out_specs=pl.BlockSpec((tile, dh), lambda b, h, i: (b * nq + i, h)),
        out_shape=jax.ShapeDtypeStruct((batch * seq, MOBA_WIDTH), BF16),
        scratch_shapes=_attn_scratch(),
        compiler_params=_cparams(("arbitrary", "arbitrary", "arbitrary")),
        name="moba_attn",
    )(q_wide, k_wide, v_wide, bias)


def _outproj_kernel(oa_ref, ob_ref, wa_ref, wb_ref, x_ref, mod_ref, g_ref, b_ref, wr_ref,
                    x1_ref, h2_ref, lg_ref, y_a, y_b, *, alpha, n_tiles):
    i = pl.program_id(0)
    tm, d = y_a.shape
    chunks = 4
    cn, cr = d // chunks, tm // chunks

    def matmul_into(y_ref, c):
        cols = slice(c * cn, (c + 1) * cn)
        y_ref[:, cols] = (jnp.dot(oa_ref[...], wa_ref[:, cols], preferred_element_type=F32)
                          + jnp.dot(ob_ref[...], wb_ref[:, cols], preferred_element_type=F32))

    def epilogue(y_ref, c):
        rows = slice(c * cr, (c + 1) * cr)
        z = alpha * x_ref[rows, :] + mod_ref[0, 2:3, :] * y_ref[rows, :]
        x1 = _ln(z) * g_ref[...] + b_ref[...]
        x1_ref[rows, :] = x1
        h2 = _ln(x1) * (1.0 + mod_ref[0, 4:5, :]) + mod_ref[0, 3:4, :]
        h2_ref[rows, :] = h2
        h_hi = h2.astype(BF16)
        h_lo = (h2 - h_hi.astype(F32)).astype(BF16)
        zz = (jnp.dot(h_hi, wr_ref[...], preferred_element_type=F32)
              + jnp.dot(h_lo, wr_ref[...], preferred_element_type=F32))
        lg_ref[rows, :] = zz + pltpu.roll(zz, LANES // 2, 1)

    def step(y_new, y_old):
        for c in range(chunks):
            if y_new is not None:
                matmul_into(y_new, c)
            if y_old is not None:
                epilogue(y_old, c)

    inner = (i > 0) & (i < n_tiles)
    pl.when(i == 0)(lambda: step(y_a, None))
    pl.when(inner & (i % 2 == 0))(lambda: step(y_a, y_b))
    pl.when(inner & (i % 2 == 1))(lambda: step(y_b, y_a))
    pl.when(i == n_tiles)(lambda: step(None, y_b if n_tiles % 2 == 0 else y_a))


def _out_proj(o_mla, o_moba, wo, xf, mod3, ln_g, ln_b, wr, seq, alpha):
    t, d = xf.shape
    tm = 512
    n = t // tm
    per_b = seq // tm
    ka, kb = o_mla.shape[1], o_moba.shape[1]
    assert ka == kb and wo.shape[0] == ka + kb
    once = pl.Buffered(1)

    def cur(i):
        return (jnp.minimum(i, n - 1), 0)

    def lag(i):
        return (jnp.maximum(i - 1, 0), 0)

    return pl.pallas_call(
        functools.partial(_outproj_kernel, alpha=alpha, n_tiles=n),
        grid=(n + 1,),
        in_specs=[pl.BlockSpec((tm, ka), cur),
                  pl.BlockSpec((tm, kb), cur),
                  pl.BlockSpec((ka, d), lambda i: (0, 0), pipeline_mode=once),
                  pl.BlockSpec((kb, d), lambda i: (1, 0), pipeline_mode=once),
                  pl.BlockSpec((tm, d), lag),
                  pl.BlockSpec((1, 6, d), lambda i: (jnp.maximum(i - 1, 0) // per_b, 0, 0)),
                  pl.BlockSpec((1, d), lambda i: (0, 0)),
                  pl.BlockSpec((1, d), lambda i: (0, 0)),
                  pl.BlockSpec((d, LANES), lambda i: (0, 0), pipeline_mode=once)],
        out_specs=[pl.BlockSpec((tm, d), lag),
                   pl.BlockSpec((tm, d), lag),
                   pl.BlockSpec((tm, LANES), lag)],
        out_shape=[jax.ShapeDtypeStruct((t, d), F32),
                   jax.ShapeDtypeStruct((t, d), F32),
                   jax.ShapeDtypeStruct((t, LANES), F32)],
        scratch_shapes=[pltpu.VMEM((tm, d), F32), pltpu.VMEM((tm, d), F32)],
        compiler_params=_cparams(("arbitrary",)),
        name="out_proj",
    )(o_mla, o_moba, wo, wo, xf, mod3, ln_g, ln_b, wr)


def _route_kernel(lg_ref, br_ref, info_ref, cnt_ref, run_scr, *, tm):
    i = pl.program_id(0)

    @pl.when(i == 0)
    def _():
        run_scr[...] = jnp.zeros_like(run_scr)

    lg = lg_ref[...] + br_ref[...]
    lane = lax.broadcasted_iota(jnp.int32, (tm, LANES), 1)
    e_lo, e_hi = MOE_GROUPS, MOE_GROUPS + MOE_N_EXPERTS
    is_g = lane < e_lo
    gl = jnp.where(is_g, lg, NEG)
    gmax = jnp.max(gl, axis=-1, keepdims=True)
    gidx = jnp.min(jnp.where(gl == gmax, lane, LANES), axis=-1, keepdims=True)
    g_p = 1.0 / jnp.sum(jnp.where(is_g, jnp.exp(gl - gmax), 0.0), axis=-1, keepdims=True)
    grp_of_lane = jnp.right_shift(lane - e_lo, MOE_EXPERTS_PER_GROUP.bit_length() - 1)
    in_grp = (lane >= e_lo) & (lane < e_hi) & (grp_of_lane == gidx)
    el = jnp.where(in_grp, lg, NEG)
    m1 = jnp.max(el, axis=-1, keepdims=True)
    l1 = jnp.min(jnp.where(el == m1, lane, LANES), axis=-1, keepdims=True)
    el2 = jnp.where(lane == l1, NEG, el)
    m2 = jnp.max(el2, axis=-1, keepdims=True)
    l2 = jnp.min(jnp.where(el2 == m2, lane, LANES), axis=-1, keepdims=True)
    zsum = jnp.sum(jnp.where(in_grp, jnp.exp(el - m1), 0.0), axis=-1, keepdims=True)
    p1 = 1.0 / zsum
    p2 = jnp.exp(m2 - m1) / zsum
    wa = g_p * (p1 / (p1 + p2))
    wb = g_p * (p2 / (p1 + p2))
    hot_a = lane == l1
    hot_b = lane == l2
    onehot = jnp.where(hot_a | hot_b, 1.0, 0.0)
    r = lax.broadcasted_iota(jnp.int32, (tm, tm), 0)
    c = lax.broadcasted_iota(jnp.int32, (tm, tm), 1)
    lower = jnp.where(c < r, 1.0, 0.0).astype(BF16)
    before = jnp.dot(lower, onehot.astype(BF16), preferred_element_type=F32) + run_scr[...]
    rank_a = jnp.sum(jnp.where(hot_a, before, 0.0), axis=-1, keepdims=True)
    rank_b = jnp.sum(jnp.where(hot_b, before, 0.0), axis=-1, keepdims=True)
    run_scr[...] += jnp.sum(onehot, axis=0, keepdims=True)
    info = jnp.zeros((tm, LANES), F32)
    for k, val in enumerate([(l1 - e_lo).astype(F32), (l2 - e_lo).astype(F32), wa, wb, rank_a, rank_b]):
        info = jnp.where(lane == k, val, info)
    info_ref[...] = info
    cnt_ref[...] = run_scr[...]


def _route(logits, br):
    t = logits.shape[0]
    tm = 512
    return pl.pallas_call(
        functools.partial(_route_kernel, tm=tm),
        grid=(t // tm,),
        in_specs=[pl.BlockSpec((tm, LANES), lambda i: (i, 0)),
                  pl.BlockSpec((1, LANES), lambda i: (0, 0))],
        out_specs=[pl.BlockSpec((tm, LANES), lambda i: (i, 0)),
                   pl.BlockSpec((1, LANES), lambda i: (0, 0))],
        out_shape=[jax.ShapeDtypeStruct((t, LANES), F32),
                   jax.ShapeDtypeStruct((1, LANES), F32)],
        scratch_shapes=[pltpu.VMEM((1, LANES), F32)],
        compiler_params=_cparams(("arbitrary",)),
        name="route",
    )(logits, br)


def _pos_kernel(info_ref, start_ref, pos_ref):
    info = info_ref[...]
    tm = info.shape[0]
    lane = lax.broadcasted_iota(jnp.int32, (tm, LANES), 1)
    start = start_ref[...]
    cols = []
    for k in range(2):
        e = jnp.sum(jnp.where(lane == k, info, 0.0), axis=-1, keepdims=True).astype(jnp.int32)
        rank = jnp.sum(jnp.where(lane == 4 + k, info, 0.0), axis=-1, keepdims=True)
        base = jnp.sum(jnp.where(lane == e + MOE_GROUPS, start, 0.0), axis=-1, keepdims=True)
        cols.append(base + rank)
    wide = jnp.where(lane == 0, cols[0], jnp.where(lane == 1, cols[1], 0.0))
    pos_ref[...] = wide.T[:pos_ref.shape[0], :].astype(jnp.int32)


def _positions(info, start_lanes):
    t = info.shape[0]
    tm = 1024
    return pl.pallas_call(
        _pos_kernel,
        grid=(t // tm,),
        in_specs=[pl.BlockSpec((tm, LANES), lambda i: (i, 0)),
                  pl.BlockSpec((1, LANES), lambda i: (0, 0))],
        out_specs=pl.BlockSpec((8, tm), lambda i: (0, i)),
        out_shape=jax.ShapeDtypeStruct((8, t), jnp.int32),
        compiler_params=_cparams(("arbitrary",)),
        name="positions",
    )(info, start_lanes)


def _row_copy(src_ref, src_row, dst_ref, dst_row, sem):
    return pltpu.make_async_copy(src_ref.at[pl.ds(src_row, 1)], dst_ref.at[pl.ds(dst_row, 1)], sem)


def _dispatch_kernel(pa_ref, pb_ref, pad0_ref, padn_ref, used_ref, h_ref, xs_ref, ztile, sem, zsem,
                     *, tm, tr, n_tiles):
    i = pl.program_id(0)
    base = i * tm

    def zero_fill(act):
        def whole_tile(j, _):
            act(pltpu.make_async_copy(ztile, xs_ref.at[pl.ds(pl.multiple_of(j * tr, tr), tr)], zsem))
            return 0

        lax.fori_loop(used_ref[0], n_tiles, whole_tile, 0)

        def expert_pad(e, _):
            n, start = padn_ref[e], pad0_ref[e]
            head = jnp.minimum(jnp.bitwise_and(-start, SUBLANES - 1), n)
            for k in range(SUBLANES - 1):
                pl.when(k < head)(functools.partial(act, _row_copy(ztile, 0, xs_ref, start + k, zsem)))
            off = start + head
            groups = jnp.right_shift(n - head, SUBLANES.bit_length() - 1)
            for bit in reversed(range((tr // SUBLANES - 1).bit_length())):
                size = SUBLANES << bit
                take = jnp.bitwise_and(jnp.right_shift(groups, bit), 1)
                dst = xs_ref.at[pl.ds(pl.multiple_of(off, SUBLANES), size)]
                pl.when(take == 1)(functools.partial(act, pltpu.make_async_copy(ztile.at[pl.ds(0, size)], dst, zsem)))
                off = off + take * size
            return 0

        lax.fori_loop(0, MOE_N_EXPERTS, expert_pad, 0)

    @pl.when(i == 0)
    def _():
        ztile[...] = jnp.zeros(ztile.shape, ztile.dtype)
        zero_fill(lambda cp: cp.start())

    def issue(t, _):
        _row_copy(h_ref, t, xs_ref, pa_ref[base + t], sem).start()
        _row_copy(h_ref, t, xs_ref, pb_ref[base + t], sem).start(priority=1)
        return 0

    lax.fori_loop(0, tm, issue, 0, unroll=8)
    for _ in range(2):
        pltpu.make_async_copy(h_ref, xs_ref.at[pl.ds(0, tm)], sem).wait()

    @pl.when(i == pl.num_programs(0) - 1)
    def _():
        zero_fill(lambda cp: cp.wait())


def _dispatch(pos_a, pos_b, pad0, padn, used, h2, tr, n_tiles):
    t, d = h2.shape
    tm = 256
    grid_spec = pltpu.PrefetchScalarGridSpec(
        num_scalar_prefetch=5,
        grid=(t // tm,),
        in_specs=[pl.BlockSpec((tm, d), lambda i, *_: (i, 0))],
        out_specs=pl.BlockSpec(memory_space=pl.ANY),
        scratch_shapes=[pltpu.VMEM((tr, d), F32), pltpu.SemaphoreType.DMA(()), pltpu.SemaphoreType.DMA(())],
    )
    return pl.pallas_call(
        functools.partial(_dispatch_kernel, tm=tm, tr=tr, n_tiles=n_tiles),
        grid_spec=grid_spec,
        out_shape=jax.ShapeDtypeStruct((n_tiles * tr, d), F32),
        compiler_params=_cparams(("arbitrary",), row_dma=True),
        name="dispatch",
    )(pos_a, pos_b, pad0, padn, used, h2)


def _experts_kernel(texp_ref, tidx_ref, nexte_ref, used_ref, x_ref, w1_hbm, w3_hbm, w2_hbm, o_ref,
                    w1_f32, w3_f32, w2_f32, w1_scr, w3_scr, w2_scr, sems):
    i = pl.program_id(0)
    prev = texp_ref[jnp.maximum(i - 1, 0)]

    def weight_copies(e):
        return [pltpu.make_async_copy(w1_hbm.at[e], w1_f32, sems.at[0]),
                pltpu.make_async_copy(w3_hbm.at[e], w3_f32, sems.at[1]),
                pltpu.make_async_copy(w2_hbm.at[e], w2_f32, sems.at[2])]

    @pl.when(i == 0)
    def _():
        for cp in weight_copies(texp_ref[0]):
            cp.start()

    @pl.when((i == 0) | (texp_ref[i] != prev))
    def _():
        for cp in weight_copies(texp_ref[i]):
            cp.wait()
        w1_scr[...] = w1_f32[...].astype(BF16)
        w3_scr[...] = w3_f32[...].astype(BF16)
        w2_scr[...] = w2_f32[...].astype(BF16)

        @pl.when(nexte_ref[i] >= 0)
        def _():
            for cp in weight_copies(nexte_ref[i]):
                cp.start(priority=1)

    @pl.when(i < used_ref[0])
    def _():
        x = x_ref[...].astype(BF16)
        a = jnp.dot(x, w1_scr[...], preferred_element_type=F32)
        b = jnp.dot(x, w3_scr[...], preferred_element_type=F32)
        hid = (a / (1.0 + jnp.exp(-a))) * b
        o_ref[...] = jnp.dot(hid.astype(BF16), w2_scr[...], preferred_element_type=F32)

    @pl.when(i >= used_ref[0])
    def _():
        o_ref[...] = jnp.zeros_like(o_ref)


def _experts(texp, tidx, nexte, used, xs, w1, w3, w2, tr):
    rows, d = xs.shape
    nt = rows // tr
    f = w1.shape[2]
    grid_spec = pltpu.PrefetchScalarGridSpec(
        num_scalar_prefetch=4,
        grid=(nt,),
        in_specs=[pl.BlockSpec((tr, d), lambda i, te, ti, ne, u: (ti[i], 0)),
                  pl.BlockSpec(memory_space=pl.ANY),
                  pl.BlockSpec(memory_space=pl.ANY),
                  pl.BlockSpec(memory_space=pl.ANY)],
        out_specs=pl.BlockSpec((tr, d), lambda i, te, ti, ne, u: (i, 0)),
        scratch_shapes=[pltpu.VMEM((d, f), F32), pltpu.VMEM((d, f), F32), pltpu.VMEM((f, d), F32),
                        pltpu.VMEM((d, f), BF16), pltpu.VMEM((d, f), BF16), pltpu.VMEM((f, d), BF16),
                        pltpu.SemaphoreType.DMA((3,))],
    )
    return pl.pallas_call(
        _experts_kernel,
        grid_spec=grid_spec,
        out_shape=jax.ShapeDtypeStruct((rows, d), F32),
        compiler_params=_cparams(("arbitrary",)),
        name="experts",
    )(texp, tidx, nexte, used, xs, w1, w3, w2)


def _combine_kernel(pa_ref, pb_ref, ys_ref, x1_ref, info_ref, mod_ref, g_ref, b_ref, o_ref,
                    buf_a, buf_b, sems, *, tm, alpha):
    i = pl.program_id(0)

    def gather(tile, slot):
        base = tile * tm

        def issue(t, _):
            _row_copy(ys_ref, pa_ref[base + t], buf_a.at[slot], t, sems.at[slot]).start()
            _row_copy(ys_ref, pb_ref[base + t], buf_b.at[slot], t, sems.at[slot]).start(priority=1)
            return 0

        lax.fori_loop(0, tm, issue, 0, unroll=8)

    @pl.when(i == 0)
    def _():
        gather(0, 0)

    @pl.when(i + 1 < pl.num_programs(0))
    def _():
        gather(i + 1, (i + 1) % 2)

    slot = i % 2
    for buf in (buf_a, buf_b):
        pltpu.make_async_copy(ys_ref.at[pl.ds(0, tm)], buf.at[slot], sems.at[slot]).wait()
    info = info_ref[...]
    y = info[:, 2:3] * buf_a[slot] + info[:, 3:4] * buf_b[slot]
    z = alpha * x1_ref[...] + mod_ref[0, 5:6, :] * y
    o_ref[...] = _ln(z) * g_ref[...] + b_ref[...]


def _combine(pos_a, pos_b, ys, x1, info, mod3, ln_g, ln_b, seq, alpha):
    t, d = x1.shape
    tm = 256
    per_b = seq // tm
    grid_spec = pltpu.PrefetchScalarGridSpec(
        num_scalar_prefetch=2,
        grid=(t // tm,),
        in_specs=[pl.BlockSpec(memory_space=pl.ANY),
                  pl.BlockSpec((tm, d), lambda i, pa, pb: (i, 0)),
                  pl.BlockSpec((tm, LANES), lambda i, pa, pb: (i, 0)),
                  pl.BlockSpec((1, 6, d), lambda i, pa, pb: (i // per_b, 0, 0)),
                  pl.BlockSpec((1, d), lambda i, pa, pb: (0, 0)),
                  pl.BlockSpec((1, d), lambda i, pa, pb: (0, 0))],
        out_specs=pl.BlockSpec((tm, d), lambda i, pa, pb: (i, 0)),
        scratch_shapes=[pltpu.VMEM((2, tm, d), F32), pltpu.VMEM((2, tm, d), F32), pltpu.SemaphoreType.DMA((2,))],
    )
    return pl.pallas_call(
        functools.partial(_combine_kernel, tm=tm, alpha=alpha),
        grid_spec=grid_spec,
        out_shape=jax.ShapeDtypeStruct((t, d), F32),
        compiler_params=_cparams(("arbitrary",), row_dma=True),
        name="combine",
    )(pos_a, pos_b, ys, x1, info, mod3, ln_g, ln_b)


def _prep_w_in_kernel(w_ref, o_ref):
    r0 = MLA_Q_RANK + MLA_KV_RANK
    r1 = r0 + MLA_ROPE_DIM
    half = MLA_ROPE_DIM // 2
    z = jnp.zeros((LANES - MLA_ROPE_DIM, o_ref.shape[1]), BF16)
    o_ref[:r1, :] = w_ref[:r1, :].astype(BF16)
    o_ref[r1:r0 + LANES, :] = z
    o_ref[r0 + LANES:r0 + LANES + half, :] = w_ref[r0 + half:r1, :].astype(BF16)
    o_ref[r0 + LANES + half:r0 + LANES + MLA_ROPE_DIM, :] = w_ref[r0:r0 + half, :].astype(BF16)
    o_ref[r0 + LANES + MLA_ROPE_DIM:PART_A, :] = z
    o_ref[PART_A:, :] = w_ref[r1:, :].astype(BF16)


def _prep_w_in(w_t):
    n, k = w_t.shape
    tk = 512
    n_out = PART_A + 3 * MOBA_WIDTH
    assert n == MLA_Q_RANK + MLA_KV_RANK + MLA_ROPE_DIM + 3 * MOBA_WIDTH
    return pl.pallas_call(
        _prep_w_in_kernel,
        grid=(k // tk,),
        in_specs=[pl.BlockSpec((n, tk), lambda i: (0, i))],
        out_specs=pl.BlockSpec((n_out, tk), lambda i: (0, i)),
        out_shape=jax.ShapeDtypeStruct((n_out, k), BF16),
        compiler_params=_cparams(("arbitrary",)),
        name="prep_w_in",
    )(w_t)


def _prep_w_uq(w):
    r = w.shape[0]
    w = w.reshape(r, MLA_HEADS, MLA_NOPE_DIM + MLA_ROPE_DIM)
    half = MLA_ROPE_DIM // 2
    nope = w[:, :, :MLA_NOPE_DIM]
    x1 = w[:, :, MLA_NOPE_DIM:MLA_NOPE_DIM + half]
    x2 = w[:, :, MLA_NOPE_DIM + half:]
    z = jnp.zeros((r, MLA_HEADS, LANES - MLA_ROPE_DIM), w.dtype)
    return jnp.concatenate([nope, x1, x2, z, x2, x1, z], axis=2).reshape(r, MLA_HEADS * 3 * LANES).astype(BF16)


def _rope_lanes(seq):
    inv = 1.0 / (ROPE_THETA ** (jnp.arange(0, MLA_ROPE_DIM, 2, dtype=F32) / MLA_ROPE_DIM))
    ang = jnp.arange(seq, dtype=F32)[:, None] * inv[None, :]
    cos, sin = jnp.cos(ang), jnp.sin(ang)
    z = jnp.zeros((seq, LANES - MLA_ROPE_DIM), F32)
    return jnp.concatenate([cos, cos, z], axis=1), jnp.concatenate([-sin, sin, z], axis=1)


def _prep_router(w_rg, b_rg, w_re, b_re):
    d = w_rg.shape[0]
    w = jnp.concatenate([w_rg, w_re], axis=1)
    n = w.shape[1]
    hi = w.astype(BF16)
    lo = (w - hi.astype(F32)).astype(BF16)
    z = jnp.zeros((d, LANES // 2 - n), BF16)
    wr = jnp.concatenate([hi, z, lo, z], axis=1)
    br = jnp.zeros((1, LANES), F32).at[0, :n].set(jnp.concatenate([b_rg, b_re]))
    return wr, br


def _layer(xf, mod3, batch, seq, depth_alpha, w_in, q_norm_g, w_uq, kv_norm_g, w_ukv, w_out, bias_tiles,
           cos_t, sin_t, ln1_g, ln1_b, w_rg, b_rg, w_re, b_re, w1, w3, w2, ln2_g, ln2_b):
    t, d = xf.shape
    part_a, qkv_mo = _in_proj(xf, mod3, _prep_w_in(w_in.T), seq)
    q, k, v = _qkv(part_a, _prep_w_uq(w_uq), w_ukv.astype(BF16), q_norm_g.reshape(1, -1),
                   kv_norm_g.reshape(1, -1), cos_t, sin_t, seq)
    o_mla = _mla_attention(q, k, v, batch, seq)
    q_wide, k_wide, v_wide = _moba_select(qkv_mo, batch, seq)
    o_moba = _moba_attention(q_wide, k_wide, v_wide, bias_tiles, batch, seq)
    wo = w_out.astype(BF16)
    wr, br = _prep_router(w_rg, b_rg, w_re, b_re)
    x1, h2, logits = _out_proj(o_mla, o_moba, wo, xf, mod3, ln1_g.reshape(1, d),
                               ln1_b.reshape(1, d), wr, seq, depth_alpha)
    info, counts = _route(logits, br)
    tr = 256
    nt = (2 * t) // tr + MOE_N_EXPERTS
    cnt = counts[0, MOE_GROUPS:MOE_GROUPS + MOE_N_EXPERTS].astype(jnp.int32)
    ntile = (cnt + tr - 1) // tr
    tile_end = jnp.cumsum(ntile)
    tile_start = tile_end - ntile
    used = tile_end[-1]
    start_lanes = jnp.zeros((1, LANES), F32).at[0, MOE_GROUPS:MOE_GROUPS + MOE_N_EXPERTS].set(
        (tile_start * tr).astype(F32))
    pos = _positions(info, start_lanes)
    pos_a, pos_b = pos[0], pos[1]
    tidx = jnp.minimum(jnp.arange(nt, dtype=jnp.int32), used - 1)
    texp = jnp.sum(tidx[:, None] >= tile_end[None, :], axis=1).astype(jnp.int32)
    eids = jnp.arange(MOE_N_EXPERTS, dtype=jnp.int32)
    later = (eids[None, :] > eids[:, None]) & (ntile[None, :] > 0)
    next_nonempty = jnp.where(later.any(axis=1), jnp.argmax(later, axis=1), -1).astype(jnp.int32)
    nexte = jnp.sum(jnp.where(texp[:, None] == eids[None, :], next_nonempty[None, :], 0), axis=1).astype(jnp.int32)
    used1 = used.reshape(1).astype(jnp.int32)
    pad0 = (tile_start * tr + cnt).astype(jnp.int32)
    padn = (ntile * tr - cnt).astype(jnp.int32)
    xs = _dispatch(pos_a, pos_b, pad0, padn, used1, h2, tr, nt)
    ys = _experts(texp, tidx, nexte, used1, xs, w1, w3, w2, tr)
    return _combine(pos_a, pos_b, ys, x1, info, mod3, ln2_g.reshape(1, d), ln2_b.reshape(1, d), seq, depth_alpha)


def kernel(x, c, w_ada, b_ada, w_in, q_norm_g, w_uq, kv_norm_g, w_ukv, w_out, t5_table, ln1_g, ln1_b,
           w_router_group, b_router_group, w_router_expert, b_router_expert, w1, w3, w2, ln2_g, ln2_b):
    batch, seq, d = x.shape
    depth = w_ada.shape[0]
    alpha = (2.0 * depth) ** 0.25
    cos_t, sin_t = _rope_lanes(seq)
    bias_tiles = _t5_tiles(t5_table)
    xf = x.reshape(batch * seq, d)
    for l in range(depth):
        mod3 = _ada_mod(c, w_ada[l], b_ada[l]).reshape(batch, 6, d)
        xf = _layer(xf, mod3, batch, seq, alpha, w_in[l], q_norm_g[l], w_uq[l], kv_norm_g[l], w_ukv[l],
                    w_out[l], bias_tiles, cos_t, sin_t, ln1_g[l], ln1_b[l], w_router_group[l],
                    b_router_group[l], w_router_expert[l], b_router_expert[l], w1[l], w3[l], w2[l],
                    ln2_g[l], ln2_b[l])
    return xf.reshape(batch, seq, d)
```

```python
import functools
import math

import jax
import jax.numpy as jnp
from jax import lax
from jax.experimental import pallas as pl
from jax.experimental.pallas import tpu as pltpu

D_MODEL = 2048
MLA_HEADS = 8
MLA_Q_RANK = 512
MLA_KV_RANK = 256
MLA_NOPE_DIM = 128
MLA_ROPE_DIM = 64
MLA_V_DIM = 128
ROPE_THETA = 10000.0
MOBA_HEADS = 8
MOBA_HEAD_DIM = 128
MOBA_BLOCK = 256
MOBA_TOPK = 3
T5_BUCKETS = 32
T5_MAX_DISTANCE = 128
MOE_GROUPS = 4
MOE_EXPERTS_PER_GROUP = 8
MOE_N_EXPERTS = MOE_GROUPS * MOE_EXPERTS_PER_GROUP
MOE_D_FF = 512
LN_EPS = 1e-5
RMS_EPS = 1e-6
MOBA_WIDTH = MOBA_HEADS * MOBA_HEAD_DIM

LANES = 128
SUBLANES = 8
QK_PAD = 256
V_PAD = 256
PART_A = 1024
NEG = -1e30
VMEM_LIMIT = 56 * 1024 * 1024

F32 = jnp.float32
BF16 = jnp.bfloat16


def _cparams(sem, row_dma=False):
    return pltpu.CompilerParams(dimension_semantics=sem, vmem_limit_bytes=VMEM_LIMIT,
                                disable_bounds_checks=row_dma)


def _ln(x):
    mu = jnp.mean(x, axis=-1, keepdims=True)
    xc = x - mu
    var = jnp.mean(xc * xc, axis=-1, keepdims=True)
    return xc * lax.rsqrt(var + LN_EPS)


def _nt_dot(a, b):
    return lax.dot_general(a, b, (((1,), (1,)), ((), ())), preferred_element_type=F32)


def _ada_kernel(ct_ref, w_ref, b_ref, o_ref, *, batch):
    ct = ct_ref[...]
    ca = ct / (1.0 + jnp.exp(-ct))
    w = w_ref[...]
    rows = [jnp.sum(w * ca[:, b:b + 1], axis=0, keepdims=True) for b in range(batch)]
    o_ref[...] = jnp.concatenate(rows, axis=0) + b_ref[...]


def _ada_mod(c, w_ada, b_ada):
    batch, d = c.shape
    n = w_ada.shape[1]
    tn = 512
    ct = jnp.zeros((d, LANES), F32).at[:, :batch].set(c.T)
    return pl.pallas_call(
        functools.partial(_ada_kernel, batch=batch),
        grid=(n // tn,),
        in_specs=[pl.BlockSpec((d, LANES), lambda j: (0, 0)),
                  pl.BlockSpec((d, tn), lambda j: (0, j)),
                  pl.BlockSpec((1, tn), lambda j: (0, j))],
        out_specs=pl.BlockSpec((batch, tn), lambda j: (0, j)),
        out_shape=jax.ShapeDtypeStruct((batch, n), F32),
        compiler_params=_cparams(("arbitrary",)),
        name="ada_mod",
    )(ct, w_ada, b_ada.reshape(1, n))


def _inproj_kernel(x_ref, mod_ref, w_ref, a_ref, b_ref, h_scr):
    j = pl.program_id(1)

    @pl.when(j == 0)
    def _():
        h = _ln(x_ref[...]) * (1.0 + mod_ref[0, 1:2, :]) + mod_ref[0, 0:1, :]
        h_scr[...] = h.astype(BF16)
        a_ref[...] = _nt_dot(h_scr[...], w_ref[...])

    @pl.when(j > 0)
    def _():
        b_ref[...] = _nt_dot(h_scr[...], w_ref[...]).astype(BF16)


def _in_proj(xf, mod3, w4, seq):
    t, d = xf.shape
    n = w4.shape[0]
    tm, tn = 512, PART_A
    per_b = seq // tm
    return pl.pallas_call(
        _inproj_kernel,
        grid=(t // tm, n // tn),
        in_specs=[pl.BlockSpec((tm, d), lambda i, j: (i, 0)),
                  pl.BlockSpec((1, 6, d), lambda i, j: (i // per_b, 0, 0)),
                  pl.BlockSpec((tn, d), lambda i, j: (j, 0))],
        out_specs=[pl.BlockSpec((tm, tn), lambda i, j: (i, 0)),
                   pl.BlockSpec((tm, tn), lambda i, j: (i, jnp.maximum(j - 1, 0)))],
        out_shape=[jax.ShapeDtypeStruct((t, PART_A), F32),
                   jax.ShapeDtypeStruct((t, n - PART_A), BF16)],
        scratch_shapes=[pltpu.VMEM((tm, d), BF16)],
        compiler_params=_cparams(("arbitrary", "arbitrary")),
        name="in_proj",
    )(xf, mod3, w4)


def _qkv_kernel(a_ref, wq_ref, wkv_ref, gq_ref, gkv_ref, cos_ref, sin_ref, q_ref, k_ref, v_ref):
    a = a_ref[...]
    cq = a[:, :MLA_Q_RANK]
    ckv = a[:, MLA_Q_RANK:MLA_Q_RANK + MLA_KV_RANK]
    kr = a[:, 768:896]
    krs = a[:, 896:1024]
    cqn = (cq * lax.rsqrt(jnp.mean(cq * cq, axis=-1, keepdims=True) + RMS_EPS) * gq_ref[...]).astype(BF16)
    ckvn = (ckv * lax.rsqrt(jnp.mean(ckv * ckv, axis=-1, keepdims=True) + RMS_EPS) * gkv_ref[...]).astype(BF16)
    q3 = jnp.dot(cqn, wq_ref[...], preferred_element_type=F32)
    kv = jnp.dot(ckvn, wkv_ref[...], preferred_element_type=F32)
    cos = cos_ref[...]
    sin = sin_ref[...]
    krr = (kr * cos + krs * sin).astype(BF16)
    for h in range(MLA_HEADS):
        q0 = h * 3 * LANES
        c0 = h * QK_PAD
        q_ref[:, c0:c0 + LANES] = q3[:, q0:q0 + LANES].astype(BF16)
        q_ref[:, c0 + LANES:c0 + QK_PAD] = (
            q3[:, q0 + LANES:q0 + 2 * LANES] * cos + q3[:, q0 + 2 * LANES:q0 + 3 * LANES] * sin).astype(BF16)
        k_ref[:, c0:c0 + LANES] = kv[:, c0:c0 + LANES].astype(BF16)
        k_ref[:, c0 + LANES:c0 + QK_PAD] = krr
        v_ref[:, c0:c0 + LANES] = kv[:, c0 + LANES:c0 + QK_PAD].astype(BF16)
        v_ref[:, c0 + LANES:c0 + V_PAD] = jnp.ones((a.shape[0], V_PAD - LANES), BF16)


def _qkv(part_a, wq3, wkv, gq, gkv, cos_t, sin_t, seq):
    t = part_a.shape[0]
    tm = 512
    per_b = seq // tm
    hq = MLA_HEADS * QK_PAD
    return pl.pallas_call(
        _qkv_kernel,
        grid=(t // tm,),
        in_specs=[pl.BlockSpec((tm, PART_A), lambda i: (i, 0)),
                  pl.BlockSpec(wq3.shape, lambda i: (0, 0)),
                  pl.BlockSpec(wkv.shape, lambda i: (0, 0)),
                  pl.BlockSpec((1, MLA_Q_RANK), lambda i: (0, 0)),
                  pl.BlockSpec((1, MLA_KV_RANK), lambda i: (0, 0)),
                  pl.BlockSpec((tm, LANES), lambda i: (i % per_b, 0)),
                  pl.BlockSpec((tm, LANES), lambda i: (i % per_b, 0))],
        out_specs=[pl.BlockSpec((tm, hq), lambda i: (i, 0)),
                   pl.BlockSpec((tm, hq), lambda i: (i, 0)),
                   pl.BlockSpec((tm, MLA_HEADS * V_PAD), lambda i: (i, 0))],
        out_shape=[jax.ShapeDtypeStruct((t, hq), BF16),
                   jax.ShapeDtypeStruct((t, hq), BF16),
                   jax.ShapeDtypeStruct((t, MLA_HEADS * V_PAD), BF16)],
        compiler_params=_cparams(("arbitrary",)),
        name="qkv",
    )(part_a, wq3, wkv, gq, gkv, cos_t, sin_t)


ATTN_TILE = 512


def _attn_kernel(*refs, c, masked):
    if masked:
        q_ref, k_ref, v_ref, bias_ref, o_ref = refs[:5]
    else:
        q_ref, k_ref, v_ref, o_ref = refs[:4]
    scratch = refs[-8:]
    tile = ATTN_TILE
    i = pl.program_id(2)
    chains = [dict(rows=slice(h * tile, (h + 1) * tile), s=scratch[2 * h:2 * h + 2], m=scratch[4 + 2 * h],
                   acc=scratch[5 + 2 * h]) for h in range(2)]
    ca, cb = chains

    def put_scores(chain, slot, t):
        k = k_ref[pl.ds(pl.multiple_of(t * tile, tile), tile), :]
        chain["s"][slot][...] = _nt_dot(q_ref[chain["rows"], :], k)

    def tile_step(chain, slot, t, kind, prefetch=True):
        if prefetch:
            put_scores(chain, 1 - slot, t + 1)
        m_scr, acc_scr = chain["m"], chain["acc"]
        s = chain["s"][slot][...]
        if masked:
            if kind == "prev":
                s = s + bias_ref[0, 1]
            elif kind == "diag":
                s = s + bias_ref[0, 0]
        elif kind == "diag":
            row = lax.broadcasted_iota(jnp.int32, (tile, tile), 0)
            col = lax.broadcasted_iota(jnp.int32, (tile, tile), 1)
            s = jnp.where(row >= col, s, NEG)
        m_old = m_scr[...]
        m_new = jnp.maximum(m_old, jnp.broadcast_to(jnp.max(s, axis=-1, keepdims=True), m_old.shape))
        alpha = jnp.exp2((m_old - m_new) * c)
        p = jnp.concatenate([jnp.exp2(((s[:, j * LANES:(j + 1) * LANES] - m_new) * c).astype(BF16))
                             for j in range(tile // LANES)], axis=1)
        v = v_ref[pl.ds(pl.multiple_of(t * tile, tile), tile), :]
        pv = jnp.dot(p, v, preferred_element_type=F32)
        acc_scr[...] = jnp.concatenate([alpha, alpha], axis=1) * acc_scr[...] + pv
        m_scr[...] = m_new

    for chain in chains:
        chain["m"][...] = jnp.full(chain["m"].shape, NEG, F32)
        chain["acc"][...] = jnp.zeros(chain["acc"].shape, F32)
        put_scores(chain, 0, 0)

    def far_pair(j, _):
        t = 2 * j
        tile_step(ca, 0, t, "far")
        tile_step(cb, 0, t, "far")
        tile_step(ca, 1, t + 1, "far")
        tile_step(cb, 1, t + 1, "far")
        return 0

    if masked:
        lax.fori_loop(0, jnp.maximum(i - 1, 0), far_pair, 0)

        @pl.when(i > 0)
        def _():
            tile_step(ca, 0, 2 * i - 2, "far")
            tile_step(cb, 0, 2 * i - 2, "far")
            tile_step(ca, 1, 2 * i - 1, "prev")
            tile_step(cb, 1, 2 * i - 1, "far")

        tile_step(ca, 0, 2 * i, "diag", prefetch=False)
        tile_step(cb, 0, 2 * i, "prev")
        tile_step(cb, 1, 2 * i + 1, "diag", prefetch=False)
    else:
        lax.fori_loop(0, i, far_pair, 0)
        tile_step(ca, 0, 2 * i, "diag", prefetch=False)
        tile_step(cb, 0, 2 * i, "far")
        tile_step(cb, 1, 2 * i + 1, "diag", prefetch=False)

    for chain in chains:
        acc = chain["acc"]
        o_ref[chain["rows"], :] = (acc[:, :LANES] / acc[:, LANES:]).astype(o_ref.dtype)


def _attn_scratch():
    tile = ATTN_TILE
    per_chain = [pltpu.VMEM((tile, tile), F32), pltpu.VMEM((tile, tile), F32)]
    stats = [pltpu.VMEM((tile, LANES), F32), pltpu.VMEM((tile, V_PAD), F32)]
    return per_chain + per_chain + stats + stats


def _mla_attention(q, k, v, batch, seq):
    tile = 2 * ATTN_TILE
    nq = seq // tile
    c = math.log2(math.e) / math.sqrt(MLA_NOPE_DIM + MLA_ROPE_DIM)
    return pl.pallas_call(
        functools.partial(_attn_kernel, c=c, masked=False),
        grid=(batch, MLA_HEADS, nq),
        in_specs=[pl.BlockSpec((tile, QK_PAD), lambda b, h, i: (b * nq + i, h)),
                  pl.BlockSpec((seq, QK_PAD), lambda b, h, i: (b, h)),
                  pl.BlockSpec((seq, V_PAD), lambda b, h, i: (b, h))],
        out_specs=pl.BlockSpec((tile, MLA_V_DIM), lambda b, h, i: (b * nq + i, h)),
        out_shape=jax.ShapeDtypeStruct((batch * seq, MLA_HEADS * MLA_V_DIM), BF16),
        scratch_shapes=_attn_scratch(),
        compiler_params=_cparams(("arbitrary", "arbitrary", "arbitrary")),
        name="mla_attn",
    )(q, k, v)


def _moba_select_kernel(q_ref, k_ref, v_ref, qa_ref, ka_ref, va_ref, *, seq, nb):
    kf = k_ref[...].astype(F32)
    km = jnp.sum(kf.reshape(nb, MOBA_BLOCK, MOBA_HEAD_DIM), axis=1) * (1.0 / MOBA_BLOCK)
    km_hi = km.astype(BF16)
    km_lo = (km - km_hi.astype(F32)).astype(BF16)
    q = q_ref[...]
    gate = _nt_dot(km_hi, q) + _nt_dot(km_lo, q)
    shift = MOBA_BLOCK.bit_length() - 1
    blk = lax.broadcasted_iota(jnp.int32, (nb, seq), 0)
    qblk = jnp.right_shift(lax.broadcasted_iota(jnp.int32, (nb, seq), 1), shift)
    g = jnp.where(blk < qblk, gate, NEG)
    visible = blk == qblk
    for _ in range(MOBA_TOPK):
        mx = jnp.max(g, axis=0, keepdims=True)
        first = jnp.min(jnp.where(g == mx, blk, nb), axis=0, keepdims=True)
        pick = (blk == first) & (mx > 0.5 * NEG)
        visible = visible | pick
        g = jnp.where(pick, NEG, g)
    mask_t = jnp.concatenate([jnp.where(visible, 0.0, NEG), jnp.zeros((LANES - nb, seq), F32)], axis=0)
    qa_ref[:, :MOBA_HEAD_DIM] = q
    qa_ref[:, MOBA_HEAD_DIM:] = mask_t.T.astype(BF16)
    lane = lax.broadcasted_iota(jnp.int32, (seq, LANES), 1)
    own = jnp.right_shift(lax.broadcasted_iota(jnp.int32, (seq, LANES), 0), shift)
    ka_ref[:, :MOBA_HEAD_DIM] = k_ref[...]
    ka_ref[:, MOBA_HEAD_DIM:] = jnp.where(lane == own, 1.0, 0.0).astype(BF16)
    va_ref[:, :MOBA_HEAD_DIM] = v_ref[...]
    va_ref[:, MOBA_HEAD_DIM:] = jnp.ones((seq, V_PAD - MOBA_HEAD_DIM), BF16)


def _moba_select(qkv_mo, batch, seq):
    nb = seq // MOBA_BLOCK
    assert MOBA_HEAD_DIM == LANES and nb <= QK_PAD - MOBA_HEAD_DIM
    wide = jax.ShapeDtypeStruct((batch * seq, MOBA_HEADS * QK_PAD), BF16)
    return pl.pallas_call(
        functools.partial(_moba_select_kernel, seq=seq, nb=nb),
        grid=(batch, MOBA_HEADS),
        in_specs=[pl.BlockSpec((seq, MOBA_HEAD_DIM), lambda b, h: (b, h)),
                  pl.BlockSpec((seq, MOBA_HEAD_DIM), lambda b, h: (b, MOBA_HEADS + h)),
                  pl.BlockSpec((seq, MOBA_HEAD_DIM), lambda b, h: (b, 2 * MOBA_HEADS + h))],
        out_specs=[pl.BlockSpec((seq, QK_PAD), lambda b, h: (b, h)),
                   pl.BlockSpec((seq, QK_PAD), lambda b, h: (b, h)),
                   pl.BlockSpec((seq, V_PAD), lambda b, h: (b, h))],
        out_shape=[wide, wide, jax.ShapeDtypeStruct((batch * seq, MOBA_HEADS * V_PAD), BF16)],
        compiler_params=_cparams(("arbitrary", "arbitrary")),
        name="moba_select",
    )(qkv_mo, qkv_mo, qkv_mo)


def _t5_kernel(tab_ref, o_ref, *, inv_scale):
    h = pl.program_id(0)
    r = lax.broadcasted_iota(jnp.int32, (LANES, LANES), 0)
    c = lax.broadcasted_iota(jnp.int32, (LANES, LANES), 1)
    max_exact = T5_BUCKETS // 2
    far = tab_ref[T5_BUCKETS - 1, h]

    def block(offset):
        rel = offset + r - c
        n = jnp.maximum(rel, 0)
        nf = jnp.maximum(n, 1).astype(F32)
        large = max_exact + (jnp.log(nf / max_exact) / math.log(T5_MAX_DISTANCE / max_exact)
                             * (T5_BUCKETS - max_exact)).astype(jnp.int32)
        large = jnp.minimum(large, T5_BUCKETS - 1)
        bucket = jnp.where(n < max_exact, n, large)
        bias = jnp.zeros((LANES, LANES), F32)
        for j in range(T5_BUCKETS):
            bias = jnp.where(bucket == j, tab_ref[j, h], bias)
        return jnp.where(rel >= 0, (bias - far) * inv_scale, NEG)

    near = {0: block(0), 1: block(LANES)}
    nblk = ATTN_TILE // LANES
    for d in range(2):
        for i in range(nblk):
            for j in range(nblk):
                k = d * nblk + i - j
                if k < 0:
                    val = jnp.full((LANES, LANES), NEG, F32)
                else:
                    val = near.get(k, jnp.zeros((LANES, LANES), F32))
                o_ref[0, d, i * LANES:(i + 1) * LANES, j * LANES:(j + 1) * LANES] = val


def _t5_tiles(t5_table):
    assert LANES >= T5_MAX_DISTANCE
    tile = ATTN_TILE
    return pl.pallas_call(
        functools.partial(_t5_kernel, inv_scale=math.sqrt(MOBA_HEAD_DIM)),
        grid=(MOBA_HEADS,),
        in_specs=[pl.BlockSpec(memory_space=pltpu.SMEM)],
        out_specs=pl.BlockSpec((1, 2, tile, tile), lambda h: (h, 0, 0, 0)),
        out_shape=jax.ShapeDtypeStruct((MOBA_HEADS, 2, tile, tile), F32),
        compiler_params=_cparams(("arbitrary",)),
        name="t5_tiles",
    )(t5_table)


def _moba_attention(q_wide, k_wide, v_wide, bias, batch, seq):
    tile = 2 * ATTN_TILE
    nq = seq // tile
    c = math.log2(math.e) / math.sqrt(MOBA_HEAD_DIM)
    dh = MOBA_HEAD_DIM
    return pl.pallas_call(
        functools.partial(_attn_kernel, c=c, masked=True),
        grid=(batch, MOBA_HEADS, nq),
        in_specs=[pl.BlockSpec((tile, QK_PAD), lambda b, h, i: (b * nq + i, h)),
                  pl.BlockSpec((seq, QK_PAD), lambda b, h, i: (b, h)),
                  pl.BlockSpec((seq, V_PAD), lambda b, h, i: (b, h)),
                  pl.BlockSpec((1, 2, ATTN_TILE, ATTN_TILE), lambda b, h, i: (h, 0, 0, 0))],
        out_specs=pl.BlockSpec((tile, dh), lambda b, h, i: (b * nq + i, h)),
        out_shape=jax.ShapeDtypeStruct((batch * seq, MOBA_WIDTH), BF16),
        scratch_shapes=_attn_scratch(),
        compiler_params=_cparams(("arbitrary", "arbitrary", "arbitrary")),
        name="moba_attn",
    )(q_wide, k_wide, v_wide, bias)


def _outproj_kernel(oa_ref, ob_ref, wa_ref, wb_ref, x_ref, mod_ref, g_ref, b_ref, wr_ref,
                    x1_ref, h2_ref, lg_ref, y_a, y_b, *, alpha, n_tiles):
    i = pl.program_id(0)
    tm, d = y_a.shape
    chunks = 4
    cn, cr = d // chunks, tm // chunks

    def matmul_into(y_ref, c):
        cols = slice(c * cn, (c + 1) * cn)
        y_ref[:, cols] = (jnp.dot(oa_ref[...], wa_ref[:, cols], preferred_element_type=F32)
                          + jnp.dot(ob_ref[...], wb_ref[:, cols], preferred_element_type=F32))

    def epilogue(y_ref, c):
        rows = slice(c * cr, (c + 1) * cr)
        z = alpha * x_ref[rows, :] + mod_ref[0, 2:3, :] * y_ref[rows, :]
        x1 = _ln(z) * g_ref[...] + b_ref[...]
        x1_ref[rows, :] = x1
        h2 = _ln(x1) * (1.0 + mod_ref[0, 4:5, :]) + mod_ref[0, 3:4, :]
        h2_ref[rows, :] = h2
        h_hi = h2.astype(BF16)
        h_lo = (h2 - h_hi.astype(F32)).astype(BF16)
        zz = (jnp.dot(h_hi, wr_ref[...], preferred_element_type=F32)
              + jnp.dot(h_lo, wr_ref[...], preferred_element_type=F32))
        lg_ref[rows, :] = zz + pltpu.roll(zz, LANES // 2, 1)

    def step(y_new, y_old):
        for c in range(chunks):
            if y_new is not None:
                matmul_into(y_new, c)
            if y_old is not None:
                epilogue(y_old, c)

    inner = (i > 0) & (i < n_tiles)
    pl.when(i == 0)(lambda: step(y_a, None))
    pl.when(inner & (i % 2 == 0))(lambda: step(y_a, y_b))
    pl.when(inner & (i % 2 == 1))(lambda: step(y_b, y_a))
    pl.when(i == n_tiles)(lambda: step(None, y_b if n_tiles % 2 == 0 else y_a))


def _out_proj(o_mla, o_moba, wo, xf, mod3, ln_g, ln_b, wr, seq, alpha):
    t, d = xf.shape
    tm = 512
    n = t // tm
    per_b = seq // tm
    ka, kb = o_mla.shape[1], o_moba.shape[1]
    assert ka == kb and wo.shape[0] == ka + kb
    once = pl.Buffered(1)

    def cur(i):
        return (jnp.minimum(i, n - 1), 0)

    def lag(i):
        return (jnp.maximum(i - 1, 0), 0)

    return pl.pallas_call(
        functools.partial(_outproj_kernel, alpha=alpha, n_tiles=n),
        grid=(n + 1,),
        in_specs=[pl.BlockSpec((tm, ka), cur),
                  pl.BlockSpec((tm, kb), cur),
                  pl.BlockSpec((ka, d), lambda i: (0, 0), pipeline_mode=once),
                  pl.BlockSpec((kb, d), lambda i: (1, 0), pipeline_mode=once),
                  pl.BlockSpec((tm, d), lag),
                  pl.BlockSpec((1, 6, d), lambda i: (jnp.maximum(i - 1, 0) // per_b, 0, 0)),
                  pl.BlockSpec((1, d), lambda i: (0, 0)),
                  pl.BlockSpec((1, d), lambda i: (0, 0)),
                  pl.BlockSpec((d, LANES), lambda i: (0, 0), pipeline_mode=once)],
        out_specs=[pl.BlockSpec((tm, d), lag),
                   pl.BlockSpec((tm, d), lag),
                   pl.BlockSpec((tm, LANES), lag)],
        out_shape=[jax.ShapeDtypeStruct((t, d), F32),
                   jax.ShapeDtypeStruct((t, d), F32),
                   jax.ShapeDtypeStruct((t, LANES), F32)],
        scratch_shapes=[pltpu.VMEM((tm, d), F32), pltpu.VMEM((tm, d), F32)],
        compiler_params=_cparams(("arbitrary",)),
        name="out_proj",
    )(o_mla, o_moba, wo, wo, xf, mod3, ln_g, ln_b, wr)


def _route_kernel(lg_ref, br_ref, info_ref, cnt_ref, run_scr, *, tm):
    i = pl.program_id(0)

    @pl.when(i == 0)
    def _():
        run_scr[...] = jnp.zeros_like(run_scr)

    lg = lg_ref[...] + br_ref[...]
    lane = lax.broadcasted_iota(jnp.int32, (tm, LANES), 1)
    e_lo, e_hi = MOE_GROUPS, MOE_GROUPS + MOE_N_EXPERTS
    is_g = lane < e_lo
    gl = jnp.where(is_g, lg, NEG)
    gmax = jnp.max(gl, axis=-1, keepdims=True)
    gidx = jnp.min(jnp.where(gl == gmax, lane, LANES), axis=-1, keepdims=True)
    g_p = 1.0 / jnp.sum(jnp.where(is_g, jnp.exp(gl - gmax), 0.0), axis=-1, keepdims=True)
    grp_of_lane = jnp.right_shift(lane - e_lo, MOE_EXPERTS_PER_GROUP.bit_length() - 1)
    in_grp = (lane >= e_lo) & (lane < e_hi) & (grp_of_lane == gidx)
    el = jnp.where(in_grp, lg, NEG)
    m1 = jnp.max(el, axis=-1, keepdims=True)
    l1 = jnp.min(jnp.where(el == m1, lane, LANES), axis=-1, keepdims=True)
    el2 = jnp.where(lane == l1, NEG, el)
    m2 = jnp.max(el2, axis=-1, keepdims=True)
    l2 = jnp.min(jnp.where(el2 == m2, lane, LANES), axis=-1, keepdims=True)
    zsum = jnp.sum(jnp.where(in_grp, jnp.exp(el - m1), 0.0), axis=-1, keepdims=True)
    p1 = 1.0 / zsum
    p2 = jnp.exp(m2 - m1) / zsum
    wa = g_p * (p1 / (p1 + p2))
    wb = g_p * (p2 / (p1 + p2))
    hot_a = lane == l1
    hot_b = lane == l2
    onehot = jnp.where(hot_a | hot_b, 1.0, 0.0)
    r = lax.broadcasted_iota(jnp.int32, (tm, tm), 0)
    c = lax.broadcasted_iota(jnp.int32, (tm, tm), 1)
    lower = jnp.where(c < r, 1.0, 0.0).astype(BF16)
    before = jnp.dot(lower, onehot.astype(BF16), preferred_element_type=F32) + run_scr[...]
    rank_a = jnp.sum(jnp.where(hot_a, before, 0.0), axis=-1, keepdims=True)
    rank_b = jnp.sum(jnp.where(hot_b, before, 0.0), axis=-1, keepdims=True)
    run_scr[...] += jnp.sum(onehot, axis=0, keepdims=True)
    info = jnp.zeros((tm, LANES), F32)
    for k, val in enumerate([(l1 - e_lo).astype(F32), (l2 - e_lo).astype(F32), wa, wb, rank_a, rank_b]):
        info = jnp.where(lane == k, val, info)
    info_ref[...] = info
    cnt_ref[...] = run_scr[...]


def _route(logits, br):
    t = logits.shape[0]
    tm = 512
    return pl.pallas_call(
        functools.partial(_route_kernel, tm=tm),
        grid=(t // tm,),
        in_specs=[pl.BlockSpec((tm, LANES), lambda i: (i, 0)),
                  pl.BlockSpec((1, LANES), lambda i: (0, 0))],
        out_specs=[pl.BlockSpec((tm, LANES), lambda i: (i, 0)),
                   pl.BlockSpec((1, LANES), lambda i: (0, 0))],
        out_shape=[jax.ShapeDtypeStruct((t, LANES), F32),
                   jax.ShapeDtypeStruct((1, LANES), F32)],
        scratch_shapes=[pltpu.VMEM((1, LANES), F32)],
        compiler_params=_cparams(("arbitrary",)),
        name="route",
    )(logits, br)


def _pos_kernel(info_ref, start_ref, pos_ref):
    info = info_ref[...]
    tm = info.shape[0]
    lane = lax.broadcasted_iota(jnp.int32, (tm, LANES), 1)
    start = start_ref[...]
    cols = []
    for k in range(2):
        e = jnp.sum(jnp.where(lane == k, info, 0.0), axis=-1, keepdims=True).astype(jnp.int32)
        rank = jnp.sum(jnp.where(lane == 4 + k, info, 0.0), axis=-1, keepdims=True)
        base = jnp.sum(jnp.where(lane == e + MOE_GROUPS, start, 0.0), axis=-1, keepdims=True)
        cols.append(base + rank)
    wide = jnp.where(lane == 0, cols[0], jnp.where(lane == 1, cols[1], 0.0))
    pos_ref[...] = wide.T[:pos_ref.shape[0], :].astype(jnp.int32)


def _positions(info, start_lanes):
    t = info.shape[0]
    tm = 1024
    return pl.pallas_call(
        _pos_kernel,
        grid=(t // tm,),
        in_specs=[pl.BlockSpec((tm, LANES), lambda i: (i, 0)),
                  pl.BlockSpec((1, LANES), lambda i: (0, 0))],
        out_specs=pl.BlockSpec((8, tm), lambda i: (0, i)),
        out_shape=jax.ShapeDtypeStruct((8, t), jnp.int32),
        compiler_params=_cparams(("arbitrary",)),
        name="positions",
    )(info, start_lanes)


def _row_copy(src_ref, src_row, dst_ref, dst_row, sem):
    return pltpu.make_async_copy(src_ref.at[pl.ds(src_row, 1)], dst_ref.at[pl.ds(dst_row, 1)], sem)


def _dispatch_kernel(pa_ref, pb_ref, pad0_ref, padn_ref, used_ref, h_ref, xs_ref, ztile, sem, zsem,
                     *, tm, tr, n_tiles):
    i = pl.program_id(0)
    base = i * tm

    def zero_fill(act):
        def whole_tile(j, _):
            act(pltpu.make_async_copy(ztile, xs_ref.at[pl.ds(pl.multiple_of(j * tr, tr), tr)], zsem))
            return 0

        lax.fori_loop(used_ref[0], n_tiles, whole_tile, 0)

        def expert_pad(e, _):
            n, start = padn_ref[e], pad0_ref[e]
            head = jnp.minimum(jnp.bitwise_and(-start, SUBLANES - 1), n)
            for k in range(SUBLANES - 1):
                pl.when(k < head)(functools.partial(act, _row_copy(ztile, 0, xs_ref, start + k, zsem)))
            off = start + head
            groups = jnp.right_shift(n - head, SUBLANES.bit_length() - 1)
            for bit in reversed(range((tr // SUBLANES - 1).bit_length())):
                size = SUBLANES << bit
                take = jnp.bitwise_and(jnp.right_shift(groups, bit), 1)
                dst = xs_ref.at[pl.ds(pl.multiple_of(off, SUBLANES), size)]
                pl.when(take == 1)(functools.partial(act, pltpu.make_async_copy(ztile.at[pl.ds(0, size)], dst, zsem)))
                off = off + take * size
            return 0

        lax.fori_loop(0, MOE_N_EXPERTS, expert_pad, 0)

    @pl.when(i == 0)
    def _():
        ztile[...] = jnp.zeros(ztile.shape, ztile.dtype)
        zero_fill(lambda cp: cp.start())

    def issue(t, _):
        _row_copy(h_ref, t, xs_ref, pa_ref[base + t], sem).start()
        _row_copy(h_ref, t, xs_ref, pb_ref[base + t], sem).start(priority=1)
        return 0

    lax.fori_loop(0, tm, issue, 0, unroll=8)
    for _ in range(2):
        pltpu.make_async_copy(h_ref, xs_ref.at[pl.ds(0, tm)], sem).wait()

    @pl.when(i == pl.num_programs(0) - 1)
    def _():
        zero_fill(lambda cp: cp.wait())


def _dispatch(pos_a, pos_b, pad0, padn, used, h2, tr, n_tiles):
    t, d = h2.shape
    tm = 256
    grid_spec = pltpu.PrefetchScalarGridSpec(
        num_scalar_prefetch=5,
        grid=(t // tm,),
        in_specs=[pl.BlockSpec((tm, d), lambda i, *_: (i, 0))],
        out_specs=pl.BlockSpec(memory_space=pl.ANY),
        scratch_shapes=[pltpu.VMEM((tr, d), F32), pltpu.SemaphoreType.DMA(()), pltpu.SemaphoreType.DMA(())],
    )
    return pl.pallas_call(
        functools.partial(_dispatch_kernel, tm=tm, tr=tr, n_tiles=n_tiles),
        grid_spec=grid_spec,
        out_shape=jax.ShapeDtypeStruct((n_tiles * tr, d), F32),
        compiler_params=_cparams(("arbitrary",), row_dma=True),
        name="dispatch",
    )(pos_a, pos_b, pad0, padn, used, h2)


def _experts_kernel(texp_ref, tidx_ref, nexte_ref, used_ref, x_ref, w1_hbm, w3_hbm, w2_hbm, o_ref,
                    w1_f32, w3_f32, w2_f32, w1_scr, w3_scr, w2_scr, sems):
    i = pl.program_id(0)
    prev = texp_ref[jnp.maximum(i - 1, 0)]

    def weight_copies(e):
        return [pltpu.make_async_copy(w1_hbm.at[e], w1_f32, sems.at[0]),
                pltpu.make_async_copy(w3_hbm.at[e], w3_f32, sems.at[1]),
                pltpu.make_async_copy(w2_hbm.at[e], w2_f32, sems.at[2])]

    @pl.when(i == 0)
    def _():
        for cp in weight_copies(texp_ref[0]):
            cp.start()

    @pl.when((i == 0) | (texp_ref[i] != prev))
    def _():
        for cp in weight_copies(texp_ref[i]):
            cp.wait()
        w1_scr[...] = w1_f32[...].astype(BF16)
        w3_scr[...] = w3_f32[...].astype(BF16)
        w2_scr[...] = w2_f32[...].astype(BF16)

        @pl.when(nexte_ref[i] >= 0)
        def _():
            for cp in weight_copies(nexte_ref[i]):
                cp.start(priority=1)

    @pl.when(i < used_ref[0])
    def _():
        x = x_ref[...].astype(BF16)
        a = jnp.dot(x, w1_scr[...], preferred_element_type=F32)
        b = jnp.dot(x, w3_scr[...], preferred_element_type=F32)
        hid = (a / (1.0 + jnp.exp(-a))) * b
        o_ref[...] = jnp.dot(hid.astype(BF16), w2_scr[...], preferred_element_type=F32)

    @pl.when(i >= used_ref[0])
    def _():
        o_ref[...] = jnp.zeros_like(o_ref)


def _experts(texp, tidx, nexte, used, xs, w1, w3, w2, tr):
    rows, d = xs.shape
    nt = rows // tr
    f = w1.shape[2]
    grid_spec = pltpu.PrefetchScalarGridSpec(
        num_scalar_prefetch=4,
        grid=(nt,),
        in_specs=[pl.BlockSpec((tr, d), lambda i, te, ti, ne, u: (ti[i], 0)),
                  pl.BlockSpec(memory_space=pl.ANY),
                  pl.BlockSpec(memory_space=pl.ANY),
                  pl.BlockSpec(memory_space=pl.ANY)],
        out_specs=pl.BlockSpec((tr, d), lambda i, te, ti, ne, u: (i, 0)),
        scratch_shapes=[pltpu.VMEM((d, f), F32), pltpu.VMEM((d, f), F32), pltpu.VMEM((f, d), F32),
                        pltpu.VMEM((d, f), BF16), pltpu.VMEM((d, f), BF16), pltpu.VMEM((f, d), BF16),
                        pltpu.SemaphoreType.DMA((3,))],
    )
    return pl.pallas_call(
        _experts_kernel,
        grid_spec=grid_spec,
        out_shape=jax.ShapeDtypeStruct((rows, d), F32),
        compiler_params=_cparams(("arbitrary",)),
        name="experts",
    )(texp, tidx, nexte, used, xs, w1, w3, w2)


def _combine_kernel(pa_ref, pb_ref, ys_ref, x1_ref, info_ref, mod_ref, g_ref, b_ref, o_ref,
                    buf_a, buf_b, sems, *, tm, alpha):
    i = pl.program_id(0)

    def gather(tile, slot):
        base = tile * tm

        def issue(t, _):
            _row_copy(ys_ref, pa_ref[base + t], buf_a.at[slot], t, sems.at[slot]).start()
            _row_copy(ys_ref, pb_ref[base + t], buf_b.at[slot], t, sems.at[slot]).start(priority=1)
            return 0

        lax.fori_loop(0, tm, issue, 0, unroll=8)

    @pl.when(i == 0)
    def _():
        gather(0, 0)

    @pl.when(i + 1 < pl.num_programs(0))
    def _():
        gather(i + 1, (i + 1) % 2)

    slot = i % 2
    for buf in (buf_a, buf_b):
        pltpu.make_async_copy(ys_ref.at[pl.ds(0, tm)], buf.at[slot], sems.at[slot]).wait()
    info = info_ref[...]
    y = info[:, 2:3] * buf_a[slot] + info[:, 3:4] * buf_b[slot]
    z = alpha * x1_ref[...] + mod_ref[0, 5:6, :] * y
    o_ref[...] = _ln(z) * g_ref[...] + b_ref[...]


def _combine(pos_a, pos_b, ys, x1, info, mod3, ln_g, ln_b, seq, alpha):
    t, d = x1.shape
    tm = 256
    per_b = seq // tm
    grid_spec = pltpu.PrefetchScalarGridSpec(
        num_scalar_prefetch=2,
        grid=(t // tm,),
        in_specs=[pl.BlockSpec(memory_space=pl.ANY),
                  pl.BlockSpec((tm, d), lambda i, pa, pb: (i, 0)),
                  pl.BlockSpec((tm, LANES), lambda i, pa, pb: (i, 0)),
                  pl.BlockSpec((1, 6, d), lambda i, pa, pb: (i // per_b, 0, 0)),
                  pl.BlockSpec((1, d), lambda i, pa, pb: (0, 0)),
                  pl.BlockSpec((1, d), lambda i, pa, pb: (0, 0))],
        out_specs=pl.BlockSpec((tm, d), lambda i, pa, pb: (i, 0)),
        scratch_shapes=[pltpu.VMEM((2, tm, d), F32), pltpu.VMEM((2, tm, d), F32), pltpu.SemaphoreType.DMA((2,))],
    )
    return pl.pallas_call(
        functools.partial(_combine_kernel, tm=tm, alpha=alpha),
        grid_spec=grid_spec,
        out_shape=jax.ShapeDtypeStruct((t, d), F32),
        compiler_params=_cparams(("arbitrary",), row_dma=True),
        name="combine",
    )(pos_a, pos_b, ys, x1, info, mod3, ln_g, ln_b)


def _prep_w_in_kernel(w_ref, o_ref):
    r0 = MLA_Q_RANK + MLA_KV_RANK
    r1 = r0 + MLA_ROPE_DIM
    half = MLA_ROPE_DIM // 2
    z = jnp.zeros((LANES - MLA_ROPE_DIM, o_ref.shape[1]), BF16)
    o_ref[:r1, :] = w_ref[:r1, :].astype(BF16)
    o_ref[r1:r0 + LANES, :] = z
    o_ref[r0 + LANES:r0 + LANES + half, :] = w_ref[r0 + half:r1, :].astype(BF16)
    o_ref[r0 + LANES + half:r0 + LANES + MLA_ROPE_DIM, :] = w_ref[r0:r0 + half, :].astype(BF16)
    o_ref[r0 + LANES + MLA_ROPE_DIM:PART_A, :] = z
    o_ref[PART_A:, :] = w_ref[r1:, :].astype(BF16)


def _prep_w_in(w_t):
    n, k = w_t.shape
    tk = 512
    n_out = PART_A + 3 * MOBA_WIDTH
    assert n == MLA_Q_RANK + MLA_KV_RANK + MLA_ROPE_DIM + 3 * MOBA_WIDTH
    return pl.pallas_call(
        _prep_w_in_kernel,
        grid=(k // tk,),
        in_specs=[pl.BlockSpec((n, tk), lambda i: (0, i))],
        out_specs=pl.BlockSpec((n_out, tk), lambda i: (0, i)),
        out_shape=jax.ShapeDtypeStruct((n_out, k), BF16),
        compiler_params=_cparams(("arbitrary",)),
        name="prep_w_in",
    )(w_t)


def _prep_w_uq(w):
    r = w.shape[0]
    w = w.reshape(r, MLA_HEADS, MLA_NOPE_DIM + MLA_ROPE_DIM)
    half = MLA_ROPE_DIM // 2
    nope = w[:, :, :MLA_NOPE_DIM]
    x1 = w[:, :, MLA_NOPE_DIM:MLA_NOPE_DIM + half]
    x2 = w[:, :, MLA_NOPE_DIM + half:]
    z = jnp.zeros((r, MLA_HEADS, LANES - MLA_ROPE_DIM), w.dtype)
    return jnp.concatenate([nope, x1, x2, z, x2, x1, z], axis=2).reshape(r, MLA_HEADS * 3 * LANES).astype(BF16)


def _rope_lanes(seq):
    inv = 1.0 / (ROPE_THETA ** (jnp.arange(0, MLA_ROPE_DIM, 2, dtype=F32) / MLA_ROPE_DIM))
    ang = jnp.arange(seq, dtype=F32)[:, None] * inv[None, :]
    cos, sin = jnp.cos(ang), jnp.sin(ang)
    z = jnp.zeros((seq, LANES - MLA_ROPE_DIM), F32)
    return jnp.concatenate([cos, cos, z], axis=1), jnp.concatenate([-sin, sin, z], axis=1)


def _prep_router(w_rg, b_rg, w_re, b_re):
    d = w_rg.shape[0]
    w = jnp.concatenate([w_rg, w_re], axis=1)
    n = w.shape[1]
    hi = w.astype(BF16)
    lo = (w - hi.astype(F32)).astype(BF16)
    z = jnp.zeros((d, LANES // 2 - n), BF16)
    wr = jnp.concatenate([hi, z, lo, z], axis=1)
    br = jnp.zeros((1, LANES), F32).at[0, :n].set(jnp.concatenate([b_rg, b_re]))
    return wr, br


def _layer(xf, mod3, batch, seq, depth_alpha, w_in, q_norm_g, w_uq, kv_norm_g, w_ukv, w_out, bias_tiles,
           cos_t, sin_t, ln1_g, ln1_b, w_rg, b_rg, w_re, b_re, w1, w3, w2, ln2_g, ln2_b):
    t, d = xf.shape
    part_a, qkv_mo = _in_proj(xf, mod3, _prep_w_in(w_in.T), seq)
    q, k, v = _qkv(part_a, _prep_w_uq(w_uq), w_ukv.astype(BF16), q_norm_g.reshape(1, -1),
                   kv_norm_g.reshape(1, -1), cos_t, sin_t, seq)
    o_mla = _mla_attention(q, k, v, batch, seq)
    q_wide, k_wide, v_wide = _moba_select(qkv_mo, batch, seq)
    o_moba = _moba_attention(q_wide, k_wide, v_wide, bias_tiles, batch, seq)
    wo = w_out.astype(BF16)
    wr, br = _prep_router(w_rg, b_rg, w_re, b_re)
    x1, h2, logits = _out_proj(o_mla, o_moba, wo, xf, mod3, ln1_g.reshape(1, d),
                               ln1_b.reshape(1, d), wr, seq, depth_alpha)
    info, counts = _route(logits, br)
    tr = 256
    nt = (2 * t) // tr + MOE_N_EXPERTS
    cnt = counts[0, MOE_GROUPS:MOE_GROUPS + MOE_N_EXPERTS].astype(jnp.int32)
    ntile = (cnt + tr - 1) // tr
    tile_end = jnp.cumsum(ntile)
    tile_start = tile_end - ntile
    used = tile_end[-1]
    start_lanes = jnp.zeros((1, LANES), F32).at[0, MOE_GROUPS:MOE_GROUPS + MOE_N_EXPERTS].set(
        (tile_start * tr).astype(F32))
    pos = _positions(info, start_lanes)
    pos_a, pos_b = pos[0], pos[1]
    tidx = jnp.minimum(jnp.arange(nt, dtype=jnp.int32), used - 1)
    texp = jnp.sum(tidx[:, None] >= tile_end[None, :], axis=1).astype(jnp.int32)
    eids = jnp.arange(MOE_N_EXPERTS, dtype=jnp.int32)
    later = (eids[None, :] > eids[:, None]) & (ntile[None, :] > 0)
    next_nonempty = jnp.where(later.any(axis=1), jnp.argmax(later, axis=1), -1).astype(jnp.int32)
    nexte = jnp.sum(jnp.where(texp[:, None] == eids[None, :], next_nonempty[None, :], 0), axis=1).astype(jnp.int32)
    used1 = used.reshape(1).astype(jnp.int32)
    pad0 = (tile_start * tr + cnt).astype(jnp.int32)
    padn = (ntile * tr - cnt).astype(jnp.int32)
    xs = _dispatch(pos_a, pos_b, pad0, padn, used1, h2, tr, nt)
    ys = _experts(texp, tidx, nexte, used1, xs, w1, w3, w2, tr)
    return _combine(pos_a, pos_b, ys, x1, info, mod3, ln2_g.reshape(1, d), ln2_b.reshape(1, d), seq, depth_alpha)


def kernel(x, c, w_ada, b_ada, w_in, q_norm_g, w_uq, kv_norm_g, w_ukv, w_out, t5_table, ln1_g, ln1_b,
           w_router_group, b_router_group, w_router_expert, b_router_expert, w1, w3, w2, ln2_g, ln2_b):
    batch, seq, d = x.shape
    depth = w_ada.shape[0]
    alpha = (2.0 * depth) ** 0.25
    cos_t, sin_t = _rope_lanes(seq)
    bias_tiles = _t5_tiles(t5_table)
    xf = x.reshape(batch * seq, d)
    for l in range(depth):
        mod3 = _ada_mod(c, w_ada[l], b_ada[l]).reshape(batch, 6, d)
        xf = _layer(xf, mod3, batch, seq, alpha, w_in[l], q_norm_g[l], w_uq[l], kv_norm_g[l], w_ukv[l],
                    w_out[l], bias_tiles, cos_t, sin_t, ln1_g[l], ln1_b[l], w_router_group[l],
                    b_router_group[l], w_router_expert[l], b_router_expert[l], w1[l], w3[l], w2[l],
                    ln2_g[l], ln2_b[l])
    return xf.reshape(batch, seq, d)
```

```python
import functools
import math

import jax
import jax.numpy as jnp
from jax import lax
from jax.experimental import pallas as pl
from jax.experimental.pallas import tpu as pltpu

D_MODEL = 2048
MLA_HEADS = 8
MLA_Q_RANK = 512
MLA_KV_RANK = 256
MLA_NOPE_DIM = 128
MLA_ROPE_DIM = 64
MLA_V_DIM = 128
ROPE_THETA = 10000.0
MOBA_HEADS = 8
MOBA_HEAD_DIM = 128
MOBA_BLOCK = 256
MOBA_TOPK = 3
T5_BUCKETS = 32
T5_MAX_DISTANCE = 128
MOE_GROUPS = 4
MOE_EXPERTS_PER_GROUP = 8
MOE_N_EXPERTS = MOE_GROUPS * MOE_EXPERTS_PER_GROUP
MOE_D_FF = 512
LN_EPS = 1e-5
RMS_EPS = 1e-6
MOBA_WIDTH = MOBA_HEADS * MOBA_HEAD_DIM

LANES = 128
SUBLANES = 8
QK_PAD = 256
V_PAD = 256
PART_A = 1024
NEG = -1e30
VMEM_LIMIT = 56 * 1024 * 1024

F32 = jnp.float32
BF16 = jnp.bfloat16


def _cparams(sem, row_dma=False):
    return pltpu.CompilerParams(dimension_semantics=sem, vmem_limit_bytes=VMEM_LIMIT,
                                disable_bounds_checks=row_dma)


def _ln(x):
    mu = jnp.mean(x, axis=-1, keepdims=True)
    xc = x - mu
    var = jnp.mean(xc * xc, axis=-1, keepdims=True)
    return xc * lax.rsqrt(var + LN_EPS)


def _nt_dot(a, b):
    return lax.dot_general(a, b, (((1,), (1,)), ((), ())), preferred_element_type=F32)


def _ada_kernel(ct_ref, w_ref, b_ref, o_ref, *, batch):
    ct = ct_ref[...]
    ca = ct / (1.0 + jnp.exp(-ct))
    w = w_ref[...]
    rows = [jnp.sum(w * ca[:, b:b + 1], axis=0, keepdims=True) for b in range(batch)]
    o_ref[...] = jnp.concatenate(rows, axis=0) + b_ref[...]


def _ada_mod(c, w_ada, b_ada):
    batch, d = c.shape
    n = w_ada.shape[1]
    tn = 512
    ct = jnp.zeros((d, LANES), F32).at[:, :batch].set(c.T)
    return pl.pallas_call(
        functools.partial(_ada_kernel, batch=batch),
        grid=(n // tn,),
        in_specs=[pl.BlockSpec((d, LANES), lambda j: (0, 0)),
                  pl.BlockSpec((d, tn), lambda j: (0, j)),
                  pl.BlockSpec((1, tn), lambda j: (0, j))],
        out_specs=pl.BlockSpec((batch, tn), lambda j: (0, j)),
        out_shape=jax.ShapeDtypeStruct((batch, n), F32),
        compiler_params=_cparams(("arbitrary",)),
        name="ada_mod",
    )(ct, w_ada, b_ada.reshape(1, n))


def _inproj_kernel(x_ref, mod_ref, w_ref, a_ref, b_ref, h_scr):
    j = pl.program_id(1)

    @pl.when(j == 0)
    def _():
        h = _ln(x_ref[...]) * (1.0 + mod_ref[0, 1:2, :]) + mod_ref[0, 0:1, :]
        h_scr[...] = h.astype(BF16)
        a_ref[...] = _nt_dot(h_scr[...], w_ref[...])

    @pl.when(j > 0)
    def _():
        b_ref[...] = _nt_dot(h_scr[...], w_ref[...]).astype(BF16)


def _in_proj(xf, mod3, w4, seq):
    t, d = xf.shape
    n = w4.shape[0]
    tm, tn = 512, PART_A
    per_b = seq // tm
    return pl.pallas_call(
        _inproj_kernel,
        grid=(t // tm, n // tn),
        in_specs=[pl.BlockSpec((tm, d), lambda i, j: (i, 0)),
                  pl.BlockSpec((1, 6, d), lambda i, j: (i // per_b, 0, 0)),
                  pl.BlockSpec((tn, d), lambda i, j: (j, 0))],
        out_specs=[pl.BlockSpec((tm, tn), lambda i, j: (i, 0)),
                   pl.BlockSpec((tm, tn), lambda i, j: (i, jnp.maximum(j - 1, 0)))],
        out_shape=[jax.ShapeDtypeStruct((t, PART_A), F32),
                   jax.ShapeDtypeStruct((t, n - PART_A), BF16)],
        scratch_shapes=[pltpu.VMEM((tm, d), BF16)],
        compiler_params=_cparams(("arbitrary", "arbitrary")),
        name="in_proj",
    )(xf, mod3, w4)


def _qkv_kernel(a_ref, wq_ref, wkv_ref, gq_ref, gkv_ref, cos_ref, sin_ref, q_ref, k_ref, v_ref):
    a = a_ref[...]
    cq = a[:, :MLA_Q_RANK]
    ckv = a[:, MLA_Q_RANK:MLA_Q_RANK + MLA_KV_RANK]
    kr = a[:, 768:896]
    krs = a[:, 896:1024]
    cqn = (cq * lax.rsqrt(jnp.mean(cq * cq, axis=-1, keepdims=True) + RMS_EPS) * gq_ref[...]).astype(BF16)
    ckvn = (ckv * lax.rsqrt(jnp.mean(ckv * ckv, axis=-1, keepdims=True) + RMS_EPS) * gkv_ref[...]).astype(BF16)
    q3 = jnp.dot(cqn, wq_ref[...], preferred_element_type=F32)
    kv = jnp.dot(ckvn, wkv_ref[...], preferred_element_type=F32)
    cos = cos_ref[...]
    sin = sin_ref[...]
    krr = (kr * cos + krs * sin).astype(BF16)
    for h in range(MLA_HEADS):
        q0 = h * 3 * LANES
        c0 = h * QK_PAD
        q_ref[:, c0:c0 + LANES] = q3[:, q0:q0 + LANES].astype(BF16)
        q_ref[:, c0 + LANES:c0 + QK_PAD] = (
            q3[:, q0 + LANES:q0 + 2 * LANES] * cos + q3[:, q0 + 2 * LANES:q0 + 3 * LANES] * sin).astype(BF16)
        k_ref[:, c0:c0 + LANES] = kv[:, c0:c0 + LANES].astype(BF16)
        k_ref[:, c0 + LANES:c0 + QK_PAD] = krr
        v_ref[:, c0:c0 + LANES] = kv[:, c0 + LANES:c0 + QK_PAD].astype(BF16)
        v_ref[:, c0 + LANES:c0 + V_PAD] = jnp.ones((a.shape[0], V_PAD - LANES), BF16)


def _qkv(part_a, wq3, wkv, gq, gkv, cos_t, sin_t, seq):
    t = part_a.shape[0]
    tm = 512
    per_b = seq // tm
    hq = MLA_HEADS * QK_PAD
    return pl.pallas_call(
        _qkv_kernel,
        grid=(t // tm,),
        in_specs=[pl.BlockSpec((tm, PART_A), lambda i: (i, 0)),
                  pl.BlockSpec(wq3.shape, lambda i: (0, 0)),
                  pl.BlockSpec(wkv.shape, lambda i: (0, 0)),
                  pl.BlockSpec((1, MLA_Q_RANK), lambda i: (0, 0)),
                  pl.BlockSpec((1, MLA_KV_RANK), lambda i: (0, 0)),
                  pl.BlockSpec((tm, LANES), lambda i: (i % per_b, 0)),
                  pl.BlockSpec((tm, LANES), lambda i: (i % per_b, 0))],
        out_specs=[pl.BlockSpec((tm, hq), lambda i: (i, 0)),
                   pl.BlockSpec((tm, hq), lambda i: (i, 0)),
                   pl.BlockSpec((tm, MLA_HEADS * V_PAD), lambda i: (i, 0))],
        out_shape=[jax.ShapeDtypeStruct((t, hq), BF16),
                   jax.ShapeDtypeStruct((t, hq), BF16),
                   jax.ShapeDtypeStruct((t, MLA_HEADS * V_PAD), BF16)],
        compiler_params=_cparams(("arbitrary",)),
        name="qkv",
    )(part_a, wq3, wkv, gq, gkv, cos_t, sin_t)


ATTN_TILE = 512
ATTN_CHAINS = 4


def _attn_kernel(*refs, c, masked):
    if masked:
        q_ref, k_ref, v_ref, bias_ref, o_ref = refs[:5]
    else:
        q_ref, k_ref, v_ref, o_ref = refs[:4]
    nch = ATTN_CHAINS
    scratch = refs[-4 * nch:]
    tile = ATTN_TILE
    i = pl.program_id(2)
    chains = [dict(rows=slice(n * tile, (n + 1) * tile), s=scratch[4 * n:4 * n + 2], m=scratch[4 * n + 2],
                   acc=scratch[4 * n + 3]) for n in range(nch)]

    def put_scores(chain, slot, t):
        k = k_ref[pl.ds(pl.multiple_of(t * tile, tile), tile), :]
        chain["s"][slot][...] = _nt_dot(q_ref[chain["rows"], :], k)

    def tile_step(chain, slot, t, kind, prefetch=True):
        if prefetch:
            put_scores(chain, 1 - slot, t + 1)
        m_scr, acc_scr = chain["m"], chain["acc"]
        s = chain["s"][slot][...]
        if masked:
            if kind == "prev":
                s = s + bias_ref[0, 1]
            elif kind == "diag":
                s = s + bias_ref[0, 0]
        elif kind == "diag":
            row = lax.broadcasted_iota(jnp.int32, (tile, tile), 0)
            col = lax.broadcasted_iota(jnp.int32, (tile, tile), 1)
            s = jnp.where(row >= col, s, NEG)
        m_old = m_scr[...]
        m_new = jnp.maximum(m_old, jnp.broadcast_to(jnp.max(s, axis=-1, keepdims=True), m_old.shape))
        alpha = jnp.exp2((m_old - m_new) * c)
        p = jnp.concatenate([jnp.exp2(((s[:, j * LANES:(j + 1) * LANES] - m_new) * c).astype(BF16))
                             for j in range(tile // LANES)], axis=1)
        v = v_ref[pl.ds(pl.multiple_of(t * tile, tile), tile), :]
        pv = jnp.dot(p, v, preferred_element_type=F32)
        acc_scr[...] = jnp.concatenate([alpha, alpha], axis=1) * acc_scr[...] + pv
        m_scr[...] = m_new

    for chain in chains:
        chain["m"][...] = jnp.full(chain["m"].shape, NEG, F32)
        chain["acc"][...] = jnp.zeros(chain["acc"].shape, F32)
        put_scores(chain, 0, 0)

    def far_pair(j, _):
        for slot in range(2):
            for chain in chains:
                tile_step(chain, slot, 2 * j + slot, "far")
        return 0

    first = nch * i
    if masked:
        lax.fori_loop(0, jnp.maximum(first // 2 - 1, 0), far_pair, 0)

        @pl.when(i > 0)
        def _():
            for chain in chains:
                tile_step(chain, 0, first - 2, "far")
            for n, chain in enumerate(chains):
                tile_step(chain, 1, first - 1, "prev" if n == 0 else "far")
    else:
        lax.fori_loop(0, first // 2, far_pair, 0)

    for k in range(nch):
        for n, chain in enumerate(chains):
            if n < k:
                continue
            kind = "diag" if n == k else ("prev" if masked and n == k + 1 else "far")
            tile_step(chain, k % 2, first + k, kind, prefetch=(n != k))

    for chain in chains:
        acc = chain["acc"]
        o_ref[chain["rows"], :] = (acc[:, :LANES] / acc[:, LANES:]).astype(o_ref.dtype)


def _attn_scratch():
    tile = ATTN_TILE
    per_chain = [pltpu.VMEM((tile, tile), F32), pltpu.VMEM((tile, tile), F32),
                 pltpu.VMEM((tile, LANES), F32), pltpu.VMEM((tile, V_PAD), F32)]
    return per_chain * ATTN_CHAINS


def _mla_attention(q, k, v, batch, seq):
    tile = ATTN_CHAINS * ATTN_TILE
    nq = seq // tile
    c = math.log2(math.e) / math.sqrt(MLA_NOPE_DIM + MLA_ROPE_DIM)
    return pl.pallas_call(
        functools.partial(_attn_kernel, c=c, masked=False),
        grid=(batch, MLA_HEADS, nq),
        in_specs=[pl.BlockSpec((tile, QK_PAD), lambda b, h, i: (b * nq + i, h)),
                  pl.BlockSpec((seq, QK_PAD), lambda b, h, i: (b, h)),
                  pl.BlockSpec((seq, V_PAD), lambda b, h, i: (b, h))],
        out_specs=pl.BlockSpec((tile, MLA_V_DIM), lambda b, h, i: (b * nq + i, h)),
        out_shape=jax.ShapeDtypeStruct((batch * seq, MLA_HEADS * MLA_V_DIM), BF16),
        scratch_shapes=_attn_scratch(),
        compiler_params=_cparams(("arbitrary", "arbitrary", "arbitrary")),
        name="mla_attn",
    )(q, k, v)


def _moba_select_kernel(q_ref, k_ref, v_ref, qa_ref, ka_ref, va_ref, *, seq, nb):
    kf = k_ref[...].astype(F32)
    km = jnp.sum(kf.reshape(nb, MOBA_BLOCK, MOBA_HEAD_DIM), axis=1) * (1.0 / MOBA_BLOCK)
    km_hi = km.astype(BF16)
    km_lo = (km - km_hi.astype(F32)).astype(BF16)
    q = q_ref[...]
    gate = _nt_dot(km_hi, q) + _nt_dot(km_lo, q)
    shift = MOBA_BLOCK.bit_length() - 1
    blk = lax.broadcasted_iota(jnp.int32, (nb, seq), 0)
    qblk = jnp.right_shift(lax.broadcasted_iota(jnp.int32, (nb, seq), 1), shift)
    g = jnp.where(blk < qblk, gate, NEG)
    visible = blk == qblk
    for _ in range(MOBA_TOPK):
        mx = jnp.max(g, axis=0, keepdims=True)
        first = jnp.min(jnp.where(g == mx, blk, nb), axis=0, keepdims=True)
        pick = (blk == first) & (mx > 0.5 * NEG)
        visible = visible | pick
        g = jnp.where(pick, NEG, g)
    mask_t = jnp.concatenate([jnp.where(visible, 0.0, NEG), jnp.zeros((LANES - nb, seq), F32)], axis=0)
    qa_ref[:, :MOBA_HEAD_DIM] = q
    qa_ref[:, MOBA_HEAD_DIM:] = mask_t.T.astype(BF16)
    lane = lax.broadcasted_iota(jnp.int32, (seq, LANES), 1)
    own = jnp.right_shift(lax.broadcasted_iota(jnp.int32, (seq, LANES), 0), shift)
    ka_ref[:, :MOBA_HEAD_DIM] = k_ref[...]
    ka_ref[:, MOBA_HEAD_DIM:] = jnp.where(lane == own, 1.0, 0.0).astype(BF16)
    va_ref[:, :MOBA_HEAD_DIM] = v_ref[...]
    va_ref[:, MOBA_HEAD_DIM:] = jnp.ones((seq, V_PAD - MOBA_HEAD_DIM), BF16)


def _moba_select(qkv_mo, batch, seq):
    nb = seq // MOBA_BLOCK
    assert MOBA_HEAD_DIM == LANES and nb <= QK_PAD - MOBA_HEAD_DIM
    wide = jax.ShapeDtypeStruct((batch * seq, MOBA_HEADS * QK_PAD), BF16)
    return pl.pallas_call(
        functools.partial(_moba_select_kernel, seq=seq, nb=nb),
        grid=(batch, MOBA_HEADS),
        in_specs=[pl.BlockSpec((seq, MOBA_HEAD_DIM), lambda b, h: (b, h)),
                  pl.BlockSpec((seq, MOBA_HEAD_DIM), lambda b, h: (b, MOBA_HEADS + h)),
                  pl.BlockSpec((seq, MOBA_HEAD_DIM), lambda b, h: (b, 2 * MOBA_HEADS + h))],
        out_specs=[pl.BlockSpec((seq, QK_PAD), lambda b, h: (b, h)),
                   pl.BlockSpec((seq, QK_PAD), lambda b, h: (b, h)),
                   pl.BlockSpec((seq, V_PAD), lambda b, h: (b, h))],
        out_shape=[wide, wide, jax.ShapeDtypeStruct((batch * seq, MOBA_HEADS * V_PAD), BF16)],
        compiler_params=_cparams(("arbitrary", "arbitrary")),
        name="moba_select",
    )(qkv_mo, qkv_mo, qkv_mo)


def _t5_kernel(tab_ref, o_ref, *, inv_scale):
    h = pl.program_id(0)
    r = lax.broadcasted_iota(jnp.int32, (LANES, LANES), 0)
    c = lax.broadcasted_iota(jnp.int32, (LANES, LANES), 1)
    max_exact = T5_BUCKETS // 2
    far = tab_ref[T5_BUCKETS - 1, h]

    def block(offset):
        rel = offset + r - c
        n = jnp.maximum(rel, 0)
        nf = jnp.maximum(n, 1).astype(F32)
        large = max_exact + (jnp.log(nf / max_exact) / math.log(T5_MAX_DISTANCE / max_exact)
                             * (T5_BUCKETS - max_exact)).astype(jnp.int32)
        large = jnp.minimum(large, T5_BUCKETS - 1)
        bucket = jnp.where(n < max_exact, n, large)
        bias = jnp.zeros((LANES, LANES), F32)
        for j in range(T5_BUCKETS):
            bias = jnp.where(bucket == j, tab_ref[j, h], bias)
        return jnp.where(rel >= 0, (bias - far) * inv_scale, NEG)

    near = {0: block(0), 1: block(LANES)}
    nblk = ATTN_TILE // LANES
    for d in range(2):
        for i in range(nblk):
            for j in range(nblk):
                k = d * nblk + i - j
                if k < 0:
                    val = jnp.full((LANES, LANES), NEG, F32)
                else:
                    val = near.get(k, jnp.zeros((LANES, LANES), F32))
                o_ref[0, d, i * LANES:(i + 1) * LANES, j * LANES:(j + 1) * LANES] = val


def _t5_tiles(t5_table):
    assert LANES >= T5_MAX_DISTANCE
    tile = ATTN_TILE
    return pl.pallas_call(
        functools.partial(_t5_kernel, inv_scale=math.sqrt(MOBA_HEAD_DIM)),
        grid=(MOBA_HEADS,),
        in_specs=[pl.BlockSpec(memory_space=pltpu.SMEM)],
        out_specs=pl.BlockSpec((1, 2, tile, tile), lambda h: (h, 0, 0, 0)),
        out_shape=jax.ShapeDtypeStruct((MOBA_HEADS, 2, tile, tile), F32),
        compiler_params=_cparams(("arbitrary",)),
        name="t5_tiles",
    )(t5_table)


def _moba_attention(q_wide, k_wide, v_wide, bias, batch, seq):
    tile = ATTN_CHAINS * ATTN_TILE
    nq = seq // tile
    c = math.log2(math.e) / math.sqrt(MOBA_HEAD_DIM)
    dh = MOBA_HEAD_DIM
    return pl.pallas_call(
        functools.partial(_attn_kernel, c=c, masked=True),
        grid=(batch, MOBA_HEADS, nq),
        in_specs=[pl.BlockSpec((tile, QK_PAD), lambda b, h, i: (b * nq + i, h)),
                  pl.BlockSpec((seq, QK_PAD), lambda b, h, i: (b, h)),
                  pl.BlockSpec((seq, V_PAD), lambda b, h, i: (b, h)),
                  pl.BlockSpec((1, 2, ATTN_TILE, ATTN_TILE), lambda b, h, i: (h, 0, 0, 0))],
        out_specs=pl.BlockSpec((tile, dh), lambda b, h, i: (b * nq + i, h)),
        out_shape=jax.ShapeDtypeStruct((batch * seq, MOBA_WIDTH), BF16),
        scratch_shapes=_attn_scratch(),
        compiler_params=_cparams(("arbitrary", "arbitrary", "arbitrary")),
        name="moba_attn",
    )(q_wide, k_wide, v_wide, bias)


def _outproj_kernel(oa_ref, ob_ref, wa_ref, wb_ref, x_ref, mod_ref, g_ref, b_ref, wr_ref,
                    x1_ref, h2_ref, lg_ref, y_a, y_b, *, alpha, n_tiles):
    i = pl.program_id(0)
    tm, d = y_a.shape
    chunks = 4
    cn, cr = d // chunks, tm // chunks

    def matmul_into(y_ref, c):
        cols = slice(c * cn, (c + 1) * cn)
        y_ref[:, cols] = (jnp.dot(oa_ref[...], wa_ref[:, cols], preferred_element_type=F32)
                          + jnp.dot(ob_ref[...], wb_ref[:, cols], preferred_element_type=F32))

    def epilogue(y_ref, c):
        rows = slice(c * cr, (c + 1) * cr)
        z = alpha * x_ref[rows, :] + mod_ref[0, 2:3, :] * y_ref[rows, :]
        x1 = _ln(z) * g_ref[...] + b_ref[...]
        x1_ref[rows, :] = x1
        h2 = _ln(x1) * (1.0 + mod_ref[0, 4:5, :]) + mod_ref[0, 3:4, :]
        h2_ref[rows, :] = h2
        h_hi = h2.astype(BF16)
        h_lo = (h2 - h_hi.astype(F32)).astype(BF16)
        zz = (jnp.dot(h_hi, wr_ref[...], preferred_element_type=F32)
              + jnp.dot(h_lo, wr_ref[...], preferred_element_type=F32))
        lg_ref[rows, :] = zz + pltpu.roll(zz, LANES // 2, 1)

    def step(y_new, y_old):
        for c in range(chunks):
            if y_new is not None:
                matmul_into(y_new, c)
            if y_old is not None:
                epilogue(y_old, c)

    inner = (i > 0) & (i < n_tiles)
    pl.when(i == 0)(lambda: step(y_a, None))
    pl.when(inner & (i % 2 == 0))(lambda: step(y_a, y_b))
    pl.when(inner & (i % 2 == 1))(lambda: step(y_b, y_a))
    pl.when(i == n_tiles)(lambda: step(None, y_b if n_tiles % 2 == 0 else y_a))


def _out_proj(o_mla, o_moba, wo, xf, mod3, ln_g, ln_b, wr, seq, alpha):
    t, d = xf.shape
    tm = 512
    n = t // tm
    per_b = seq // tm
    ka, kb = o_mla.shape[1], o_moba.shape[1]
    assert ka == kb and wo.shape[0] == ka + kb
    once = pl.Buffered(1)

    def cur(i):
        return (jnp.minimum(i, n - 1), 0)

    def lag(i):
        return (jnp.maximum(i - 1, 0), 0)

    return pl.pallas_call(
        functools.partial(_outproj_kernel, alpha=alpha, n_tiles=n),
        grid=(n + 1,),
        in_specs=[pl.BlockSpec((tm, ka), cur),
                  pl.BlockSpec((tm, kb), cur),
                  pl.BlockSpec((ka, d), lambda i: (0, 0), pipeline_mode=once),
                  pl.BlockSpec((kb, d), lambda i: (1, 0), pipeline_mode=once),
                  pl.BlockSpec((tm, d), lag),
                  pl.BlockSpec((1, 6, d), lambda i: (jnp.maximum(i - 1, 0) // per_b, 0, 0)),
                  pl.BlockSpec((1, d), lambda i: (0, 0)),
                  pl.BlockSpec((1, d), lambda i: (0, 0)),
                  pl.BlockSpec((d, LANES), lambda i: (0, 0), pipeline_mode=once)],
        out_specs=[pl.BlockSpec((tm, d), lag),
                   pl.BlockSpec((tm, d), lag),
                   pl.BlockSpec((tm, LANES), lag)],
        out_shape=[jax.ShapeDtypeStruct((t, d), F32),
                   jax.ShapeDtypeStruct((t, d), F32),
                   jax.ShapeDtypeStruct((t, LANES), F32)],
        scratch_shapes=[pltpu.VMEM((tm, d), F32), pltpu.VMEM((tm, d), F32)],
        compiler_params=_cparams(("arbitrary",)),
        name="out_proj",
    )(o_mla, o_moba, wo, wo, xf, mod3, ln_g, ln_b, wr)


def _route_kernel(lg_ref, br_ref, info_ref, cnt_ref, run_scr, *, tm):
    i = pl.program_id(0)

    @pl.when(i == 0)
    def _():
        run_scr[...] = jnp.zeros_like(run_scr)

    lg = lg_ref[...] + br_ref[...]
    lane = lax.broadcasted_iota(jnp.int32, (tm, LANES), 1)
    e_lo, e_hi = MOE_GROUPS, MOE_GROUPS + MOE_N_EXPERTS
    is_g = lane < e_lo
    gl = jnp.where(is_g, lg, NEG)
    gmax = jnp.max(gl, axis=-1, keepdims=True)
    gidx = jnp.min(jnp.where(gl == gmax, lane, LANES), axis=-1, keepdims=True)
    g_p = 1.0 / jnp.sum(jnp.where(is_g, jnp.exp(gl - gmax), 0.0), axis=-1, keepdims=True)
    grp_of_lane = jnp.right_shift(lane - e_lo, MOE_EXPERTS_PER_GROUP.bit_length() - 1)
    in_grp = (lane >= e_lo) & (lane < e_hi) & (grp_of_lane == gidx)
    el = jnp.where(in_grp, lg, NEG)
    m1 = jnp.max(el, axis=-1, keepdims=True)
    l1 = jnp.min(jnp.where(el == m1, lane, LANES), axis=-1, keepdims=True)
    el2 = jnp.where(lane == l1, NEG, el)
    m2 = jnp.max(el2, axis=-1, keepdims=True)
    l2 = jnp.min(jnp.where(el2 == m2, lane, LANES), axis=-1, keepdims=True)
    zsum = jnp.sum(jnp.where(in_grp, jnp.exp(el - m1), 0.0), axis=-1, keepdims=True)
    p1 = 1.0 / zsum
    p2 = jnp.exp(m2 - m1) / zsum
    wa = g_p * (p1 / (p1 + p2))
    wb = g_p * (p2 / (p1 + p2))
    hot_a = lane == l1
    hot_b = lane == l2
    onehot = jnp.where(hot_a | hot_b, 1.0, 0.0)
    r = lax.broadcasted_iota(jnp.int32, (tm, tm), 0)
    c = lax.broadcasted_iota(jnp.int32, (tm, tm), 1)
    lower = jnp.where(c < r, 1.0, 0.0).astype(BF16)
    before = jnp.dot(lower, onehot.astype(BF16), preferred_element_type=F32) + run_scr[...]
    rank_a = jnp.sum(jnp.where(hot_a, before, 0.0), axis=-1, keepdims=True)
    rank_b = jnp.sum(jnp.where(hot_b, before, 0.0), axis=-1, keepdims=True)
    run_scr[...] += jnp.sum(onehot, axis=0, keepdims=True)
    info = jnp.zeros((tm, LANES), F32)
    for k, val in enumerate([(l1 - e_lo).astype(F32), (l2 - e_lo).astype(F32), wa, wb, rank_a, rank_b]):
        info = jnp.where(lane == k, val, info)
    info_ref[...] = info
    cnt_ref[...] = run_scr[...]


def _route(logits, br):
    t = logits.shape[0]
    tm = 512
    return pl.pallas_call(
        functools.partial(_route_kernel, tm=tm),
        grid=(t // tm,),
        in_specs=[pl.BlockSpec((tm, LANES), lambda i: (i, 0)),
                  pl.BlockSpec((1, LANES), lambda i: (0, 0))],
        out_specs=[pl.BlockSpec((tm, LANES), lambda i: (i, 0)),
                   pl.BlockSpec((1, LANES), lambda i: (0, 0))],
        out_shape=[jax.ShapeDtypeStruct((t, LANES), F32),
                   jax.ShapeDtypeStruct((1, LANES), F32)],
        scratch_shapes=[pltpu.VMEM((1, LANES), F32)],
        compiler_params=_cparams(("arbitrary",)),
        name="route",
    )(logits, br)


def _pos_kernel(info_ref, start_ref, pos_ref):
    info = info_ref[...]
    tm = info.shape[0]
    lane = lax.broadcasted_iota(jnp.int32, (tm, LANES), 1)
    start = start_ref[...]
    cols = []
    for k in range(2):
        e = jnp.sum(jnp.where(lane == k, info, 0.0), axis=-1, keepdims=True).astype(jnp.int32)
        rank = jnp.sum(jnp.where(lane == 4 + k, info, 0.0), axis=-1, keepdims=True)
        base = jnp.sum(jnp.where(lane == e + MOE_GROUPS, start, 0.0), axis=-1, keepdims=True)
        cols.append(base + rank)
    wide = jnp.where(lane == 0, cols[0], jnp.where(lane == 1, cols[1], 0.0))
    pos_ref[...] = wide.T[:pos_ref.shape[0], :].astype(jnp.int32)


def _positions(info, start_lanes):
    t = info.shape[0]
    tm = 1024
    return pl.pallas_call(
        _pos_kernel,
        grid=(t // tm,),
        in_specs=[pl.BlockSpec((tm, LANES), lambda i: (i, 0)),
                  pl.BlockSpec((1, LANES), lambda i: (0, 0))],
        out_specs=pl.BlockSpec((8, tm), lambda i: (0, i)),
        out_shape=jax.ShapeDtypeStruct((8, t), jnp.int32),
        compiler_params=_cparams(("arbitrary",)),
        name="positions",
    )(info, start_lanes)


def _row_copy(src_ref, src_row, dst_ref, dst_row, sem):
    return pltpu.make_async_copy(src_ref.at[pl.ds(src_row, 1)], dst_ref.at[pl.ds(dst_row, 1)], sem)


def _dispatch_kernel(pa_ref, pb_ref, pad0_ref, padn_ref, used_ref, h_ref, xs_ref, ztile, sem, zsem,
                     *, tm, tr, n_tiles):
    i = pl.program_id(0)
    base = i * tm

    def zero_fill(act):
        def whole_tile(j, _):
            act(pltpu.make_async_copy(ztile, xs_ref.at[pl.ds(pl.multiple_of(j * tr, tr), tr)], zsem))
            return 0

        lax.fori_loop(used_ref[0], n_tiles, whole_tile, 0)

        def expert_pad(e, _):
            n, start = padn_ref[e], pad0_ref[e]
            head = jnp.minimum(jnp.bitwise_and(-start, SUBLANES - 1), n)
            for k in range(SUBLANES - 1):
                pl.when(k < head)(functools.partial(act, _row_copy(ztile, 0, xs_ref, start + k, zsem)))
            off = start + head
            groups = jnp.right_shift(n - head, SUBLANES.bit_length() - 1)
            for bit in reversed(range((tr // SUBLANES - 1).bit_length())):
                size = SUBLANES << bit
                take = jnp.bitwise_and(jnp.right_shift(groups, bit), 1)
                dst = xs_ref.at[pl.ds(pl.multiple_of(off, SUBLANES), size)]
                pl.when(take == 1)(functools.partial(act, pltpu.make_async_copy(ztile.at[pl.ds(0, size)], dst, zsem)))
                off = off + take * size
            return 0

        lax.fori_loop(0, MOE_N_EXPERTS, expert_pad, 0)

    @pl.when(i == 0)
    def _():
        ztile[...] = jnp.zeros(ztile.shape, ztile.dtype)
        zero_fill(lambda cp: cp.start())

    def issue(t, _):
        _row_copy(h_ref, t, xs_ref, pa_ref[base + t], sem).start()
        _row_copy(h_ref, t, xs_ref, pb_ref[base + t], sem).start(priority=1)
        return 0

    lax.fori_loop(0, tm, issue, 0, unroll=8)
    for _ in range(2):
        pltpu.make_async_copy(h_ref, xs_ref.at[pl.ds(0, tm)], sem).wait()

    @pl.when(i == pl.num_programs(0) - 1)
    def _():
        zero_fill(lambda cp: cp.wait())


def _dispatch(pos_a, pos_b, pad0, padn, used, h2, tr, n_tiles):
    t, d = h2.shape
    tm = 256
    grid_spec = pltpu.PrefetchScalarGridSpec(
        num_scalar_prefetch=5,
        grid=(t // tm,),
        in_specs=[pl.BlockSpec((tm, d), lambda i, *_: (i, 0))],
        out_specs=pl.BlockSpec(memory_space=pl.ANY),
        scratch_shapes=[pltpu.VMEM((tr, d), F32), pltpu.SemaphoreType.DMA(()), pltpu.SemaphoreType.DMA(())],
    )
    return pl.pallas_call(
        functools.partial(_dispatch_kernel, tm=tm, tr=tr, n_tiles=n_tiles),
        grid_spec=grid_spec,
        out_shape=jax.ShapeDtypeStruct((n_tiles * tr, d), F32),
        compiler_params=_cparams(("arbitrary",), row_dma=True),
        name="dispatch",
    )(pos_a, pos_b, pad0, padn, used, h2)


def _experts_kernel(texp_ref, tidx_ref, nexte_ref, used_ref, x_ref, w1_hbm, w3_hbm, w2_hbm, o_ref,
                    w1_f32, w3_f32, w2_f32, w1_scr, w3_scr, w2_scr, sems):
    i = pl.program_id(0)
    prev = texp_ref[jnp.maximum(i - 1, 0)]

    def weight_copies(e):
        return [pltpu.make_async_copy(w1_hbm.at[e], w1_f32, sems.at[0]),
                pltpu.make_async_copy(w3_hbm.at[e], w3_f32, sems.at[1]),
                pltpu.make_async_copy(w2_hbm.at[e], w2_f32, sems.at[2])]

    @pl.when(i == 0)
    def _():
        for cp in weight_copies(texp_ref[0]):
            cp.start()

    @pl.when((i == 0) | (texp_ref[i] != prev))
    def _():
        for cp in weight_copies(texp_ref[i]):
            cp.wait()
        w1_scr[...] = w1_f32[...].astype(BF16)
        w3_scr[...] = w3_f32[...].astype(BF16)
        w2_scr[...] = w2_f32[...].astype(BF16)

        @pl.when(nexte_ref[i] >= 0)
        def _():
            for cp in weight_copies(nexte_ref[i]):
                cp.start(priority=1)

    @pl.when(i < used_ref[0])
    def _():
        x = x_ref[...].astype(BF16)
        a = jnp.dot(x, w1_scr[...], preferred_element_type=F32)
        b = jnp.dot(x, w3_scr[...], preferred_element_type=F32)
        hid = (a / (1.0 + jnp.exp(-a))) * b
        o_ref[...] = jnp.dot(hid.astype(BF16), w2_scr[...], preferred_element_type=F32)

    @pl.when(i >= used_ref[0])
    def _():
        o_ref[...] = jnp.zeros_like(o_ref)


def _experts(texp, tidx, nexte, used, xs, w1, w3, w2, tr):
    rows, d = xs.shape
    nt = rows // tr
    f = w1.shape[2]
    grid_spec = pltpu.PrefetchScalarGridSpec(
        num_scalar_prefetch=4,
        grid=(nt,),
        in_specs=[pl.BlockSpec((tr, d), lambda i, te, ti, ne, u: (ti[i], 0)),
                  pl.BlockSpec(memory_space=pl.ANY),
                  pl.BlockSpec(memory_space=pl.ANY),
                  pl.BlockSpec(memory_space=pl.ANY)],
        out_specs=pl.BlockSpec((tr, d), lambda i, te, ti, ne, u: (i, 0)),
        scratch_shapes=[pltpu.VMEM((d, f), F32), pltpu.VMEM((d, f), F32), pltpu.VMEM((f, d), F32),
                        pltpu.VMEM((d, f), BF16), pltpu.VMEM((d, f), BF16), pltpu.VMEM((f, d), BF16),
                        pltpu.SemaphoreType.DMA((3,))],
    )
    return pl.pallas_call(
        _experts_kernel,
        grid_spec=grid_spec,
        out_shape=jax.ShapeDtypeStruct((rows, d), F32),
        compiler_params=_cparams(("arbitrary",)),
        name="experts",
    )(texp, tidx, nexte, used, xs, w1, w3, w2)


def _combine_kernel(pa_ref, pb_ref, ys_ref, x1_ref, info_ref, mod_ref, g_ref, b_ref, o_ref,
                    buf_a, buf_b, sems, *, tm, alpha):
    i = pl.program_id(0)

    def gather(tile, slot):
        base = tile * tm

        def issue(t, _):
            _row_copy(ys_ref, pa_ref[base + t], buf_a.at[slot], t, sems.at[slot]).start()
            _row_copy(ys_ref, pb_ref[base + t], buf_b.at[slot], t, sems.at[slot]).start(priority=1)
            return 0

        lax.fori_loop(0, tm, issue, 0, unroll=8)

    @pl.when(i == 0)
    def _():
        gather(0, 0)

    @pl.when(i + 1 < pl.num_programs(0))
    def _():
        gather(i + 1, (i + 1) % 2)

    slot = i % 2
    for buf in (buf_a, buf_b):
        pltpu.make_async_copy(ys_ref.at[pl.ds(0, tm)], buf.at[slot], sems.at[slot]).wait()
    info = info_ref[...]
    y = info[:, 2:3] * buf_a[slot] + info[:, 3:4] * buf_b[slot]
    z = alpha * x1_ref[...] + mod_ref[0, 5:6, :] * y
    o_ref[...] = _ln(z) * g_ref[...] + b_ref[...]


def _combine(pos_a, pos_b, ys, x1, info, mod3, ln_g, ln_b, seq, alpha):
    t, d = x1.shape
    tm = 256
    per_b = seq // tm
    grid_spec = pltpu.PrefetchScalarGridSpec(
        num_scalar_prefetch=2,
        grid=(t // tm,),
        in_specs=[pl.BlockSpec(memory_space=pl.ANY),
                  pl.BlockSpec((tm, d), lambda i, pa, pb: (i, 0)),
                  pl.BlockSpec((tm, LANES), lambda i, pa, pb: (i, 0)),
                  pl.BlockSpec((1, 6, d), lambda i, pa, pb: (i // per_b, 0, 0)),
                  pl.BlockSpec((1, d), lambda i, pa, pb: (0, 0)),
                  pl.BlockSpec((1, d), lambda i, pa, pb: (0, 0))],
        out_specs=pl.BlockSpec((tm, d), lambda i, pa, pb: (i, 0)),
        scratch_shapes=[pltpu.VMEM((2, tm, d), F32), pltpu.VMEM((2, tm, d), F32), pltpu.SemaphoreType.DMA((2,))],
    )
    return pl.pallas_call(
        functools.partial(_combine_kernel, tm=tm, alpha=alpha),
        grid_spec=grid_spec,
        out_shape=jax.ShapeDtypeStruct((t, d), F32),
        compiler_params=_cparams(("arbitrary",), row_dma=True),
        name="combine",
    )(pos_a, pos_b, ys, x1, info, mod3, ln_g, ln_b)


def _prep_w_in_kernel(w_ref, o_ref):
    r0 = MLA_Q_RANK + MLA_KV_RANK
    r1 = r0 + MLA_ROPE_DIM
    half = MLA_ROPE_DIM // 2
    z = jnp.zeros((LANES - MLA_ROPE_DIM, o_ref.shape[1]), BF16)
    o_ref[:r1, :] = w_ref[:r1, :].astype(BF16)
    o_ref[r1:r0 + LANES, :] = z
    o_ref[r0 + LANES:r0 + LANES + half, :] = w_ref[r0 + half:r1, :].astype(BF16)
    o_ref[r0 + LANES + half:r0 + LANES + MLA_ROPE_DIM, :] = w_ref[r0:r0 + half, :].astype(BF16)
    o_ref[r0 + LANES + MLA_ROPE_DIM:PART_A, :] = z
    o_ref[PART_A:, :] = w_ref[r1:, :].astype(BF16)


def _prep_w_in(w_t):
    n, k = w_t.shape
    tk = 512
    n_out = PART_A + 3 * MOBA_WIDTH
    assert n == MLA_Q_RANK + MLA_KV_RANK + MLA_ROPE_DIM + 3 * MOBA_WIDTH
    return pl.pallas_call(
        _prep_w_in_kernel,
        grid=(k // tk,),
        in_specs=[pl.BlockSpec((n, tk), lambda i: (0, i))],
        out_specs=pl.BlockSpec((n_out, tk), lambda i: (0, i)),
        out_shape=jax.ShapeDtypeStruct((n_out, k), BF16),
        compiler_params=_cparams(("arbitrary",)),
        name="prep_w_in",
    )(w_t)


def _prep_w_uq(w):
    r = w.shape[0]
    w = w.reshape(r, MLA_HEADS, MLA_NOPE_DIM + MLA_ROPE_DIM)
    half = MLA_ROPE_DIM // 2
    nope = w[:, :, :MLA_NOPE_DIM]
    x1 = w[:, :, MLA_NOPE_DIM:MLA_NOPE_DIM + half]
    x2 = w[:, :, MLA_NOPE_DIM + half:]
    z = jnp.zeros((r, MLA_HEADS, LANES - MLA_ROPE_DIM), w.dtype)
    return jnp.concatenate([nope, x1, x2, z, x2, x1, z], axis=2).reshape(r, MLA_HEADS * 3 * LANES).astype(BF16)


def _rope_lanes(seq):
    inv = 1.0 / (ROPE_THETA ** (jnp.arange(0, MLA_ROPE_DIM, 2, dtype=F32) / MLA_ROPE_DIM))
    ang = jnp.arange(seq, dtype=F32)[:, None] * inv[None, :]
    cos, sin = jnp.cos(ang), jnp.sin(ang)
    z = jnp.zeros((seq, LANES - MLA_ROPE_DIM), F32)
    return jnp.concatenate([cos, cos, z], axis=1), jnp.concatenate([-sin, sin, z], axis=1)


def _prep_router(w_rg, b_rg, w_re, b_re):
    d = w_rg.shape[0]
    w = jnp.concatenate([w_rg, w_re], axis=1)
    n = w.shape[1]
    hi = w.astype(BF16)
    lo = (w - hi.astype(F32)).astype(BF16)
    z = jnp.zeros((d, LANES // 2 - n), BF16)
    wr = jnp.concatenate([hi, z, lo, z], axis=1)
    br = jnp.zeros((1, LANES), F32).at[0, :n].set(jnp.concatenate([b_rg, b_re]))
    return wr, br


def _layer(xf, mod3, batch, seq, depth_alpha, w_in, q_norm_g, w_uq, kv_norm_g, w_ukv, w_out, bias_tiles,
           cos_t, sin_t, ln1_g, ln1_b, w_rg, b_rg, w_re, b_re, w1, w3, w2, ln2_g, ln2_b):
    t, d = xf.shape
    part_a, qkv_mo = _in_proj(xf, mod3, _prep_w_in(w_in.T), seq)
    q, k, v = _qkv(part_a, _prep_w_uq(w_uq), w_ukv.astype(BF16), q_norm_g.reshape(1, -1),
                   kv_norm_g.reshape(1, -1), cos_t, sin_t, seq)
    o_mla = _mla_attention(q, k, v, batch, seq)
    q_wide, k_wide, v_wide = _moba_select(qkv_mo, batch, seq)
    o_moba = _moba_attention(q_wide, k_wide, v_wide, bias_tiles, batch, seq)
    wo = w_out.astype(BF16)
    wr, br = _prep_router(w_rg, b_rg, w_re, b_re)
    x1, h2, logits = _out_proj(o_mla, o_moba, wo, xf, mod3, ln1_g.reshape(1, d),
                               ln1_b.reshape(1, d), wr, seq, depth_alpha)
    info, counts = _route(logits, br)
    tr = 256
    nt = (2 * t) // tr + MOE_N_EXPERTS
    cnt = counts[0, MOE_GROUPS:MOE_GROUPS + MOE_N_EXPERTS].astype(jnp.int32)
    ntile = (cnt + tr - 1) // tr
    tile_end = jnp.cumsum(ntile)
    tile_start = tile_end - ntile
    used = tile_end[-1]
    start_lanes = jnp.zeros((1, LANES), F32).at[0, MOE_GROUPS:MOE_GROUPS + MOE_N_EXPERTS].set(
        (tile_start * tr).astype(F32))
    pos = _positions(info, start_lanes)
    pos_a, pos_b = pos[0], pos[1]
    tidx = jnp.minimum(jnp.arange(nt, dtype=jnp.int32), used - 1)
    texp = jnp.sum(tidx[:, None] >= tile_end[None, :], axis=1).astype(jnp.int32)
    eids = jnp.arange(MOE_N_EXPERTS, dtype=jnp.int32)
    later = (eids[None, :] > eids[:, None]) & (ntile[None, :] > 0)
    next_nonempty = jnp.where(later.any(axis=1), jnp.argmax(later, axis=1), -1).astype(jnp.int32)
    nexte = jnp.sum(jnp.where(texp[:, None] == eids[None, :], next_nonempty[None, :], 0), axis=1).astype(jnp.int32)
    used1 = used.reshape(1).astype(jnp.int32)
    pad0 = (tile_start * tr + cnt).astype(jnp.int32)
    padn = (ntile * tr - cnt).astype(jnp.int32)
    xs = _dispatch(pos_a, pos_b, pad0, padn, used1, h2, tr, nt)
    ys = _experts(texp, tidx, nexte, used1, xs, w1, w3, w2, tr)
    return _combine(pos_a, pos_b, ys, x1, info, mod3, ln2_g.reshape(1, d), ln2_b.reshape(1, d), seq, depth_alpha)


def kernel(x, c, w_ada, b_ada, w_in, q_norm_g, w_uq, kv_norm_g, w_ukv, w_out, t5_table, ln1_g, ln1_b,
           w_router_group, b_router_group, w_router_expert, b_router_expert, w1, w3, w2, ln2_g, ln2_b):
    batch, seq, d = x.shape
    depth = w_ada.shape[0]
    alpha = (2.0 * depth) ** 0.25
    cos_t, sin_t = _rope_lanes(seq)
    bias_tiles = _t5_tiles(t5_table)
    xf = x.reshape(batch * seq, d)
    for l in range(depth):
        mod3 = _ada_mod(c, w_ada[l], b_ada[l]).reshape(batch, 6, d)
        xf = _layer(xf, mod3, batch, seq, alpha, w_in[l], q_norm_g[l], w_uq[l], kv_norm_g[l], w_ukv[l],
                    w_out[l], bias_tiles, cos_t, sin_t, ln1_g[l], ln1_b[l], w_router_group[l],
                    b_router_group[l], w_router_expert[l], b_router_expert[l], w1[l], w3[l], w2[l],
                    ln2_g[l], ln2_b[l])
    return xf.reshape(batch, seq, d)
```

```python
import functools
import math

import jax
import jax.numpy as jnp
from jax import lax
from jax.experimental import pallas as pl
from jax.experimental.pallas import tpu as pltpu

D_MODEL = 2048
MLA_HEADS = 8
MLA_Q_RANK = 512
MLA_KV_RANK = 256
MLA_NOPE_DIM = 128
MLA_ROPE_DIM = 64
MLA_V_DIM = 128
ROPE_THETA = 10000.0
MOBA_HEADS = 8
MOBA_HEAD_DIM = 128
MOBA_BLOCK = 256
MOBA_TOPK = 3
T5_BUCKETS = 32
T5_MAX_DISTANCE = 128
MOE_GROUPS = 4
MOE_EXPERTS_PER_GROUP = 8
MOE_N_EXPERTS = MOE_GROUPS * MOE_EXPERTS_PER_GROUP
MOE_D_FF = 512
LN_EPS = 1e-5
RMS_EPS = 1e-6
MOBA_WIDTH = MOBA_HEADS * MOBA_HEAD_DIM

LANES = 128
SUBLANES = 8
QK_PAD = 256
V_PAD = 256
PART_A = 1024
NEG = -1e30
VMEM_LIMIT = 56 * 1024 * 1024

F32 = jnp.float32
BF16 = jnp.bfloat16


def _cparams(sem, row_dma=False):
    return pltpu.CompilerParams(dimension_semantics=sem, vmem_limit_bytes=VMEM_LIMIT,
                                disable_bounds_checks=row_dma)


def _ln(x):
    mu = jnp.mean(x, axis=-1, keepdims=True)
    xc = x - mu
    var = jnp.mean(xc * xc, axis=-1, keepdims=True)
    return xc * lax.rsqrt(var + LN_EPS)


def _nt_dot(a, b):
    return lax.dot_general(a, b, (((1,), (1,)), ((), ())), preferred_element_type=F32)


def _ada_kernel(ct_ref, w_ref, b_ref, o_ref, *, batch):
    ct = ct_ref[...]
    ca = ct / (1.0 + jnp.exp(-ct))
    w = w_ref[...]
    rows = [jnp.sum(w * ca[:, b:b + 1], axis=0, keepdims=True) for b in range(batch)]
    o_ref[...] = jnp.concatenate(rows, axis=0) + b_ref[...]


def _ada_mod(c, w_ada, b_ada):
    batch, d = c.shape
    n = w_ada.shape[1]
    tn = 512
    ct = jnp.zeros((d, LANES), F32).at[:, :batch].set(c.T)
    return pl.pallas_call(
        functools.partial(_ada_kernel, batch=batch),
        grid=(n // tn,),
        in_specs=[pl.BlockSpec((d, LANES), lambda j: (0, 0)),
                  pl.BlockSpec((d, tn), lambda j: (0, j)),
                  pl.BlockSpec((1, tn), lambda j: (0, j))],
        out_specs=pl.BlockSpec((batch, tn), lambda j: (0, j)),
        out_shape=jax.ShapeDtypeStruct((batch, n), F32),
        compiler_params=_cparams(("arbitrary",)),
        name="ada_mod",
    )(ct, w_ada, b_ada.reshape(1, n))


def _inproj_kernel(x_ref, mod_ref, w_ref, a_ref, b_ref, h_scr):
    j = pl.program_id(1)

    @pl.when(j == 0)
    def _():
        h = _ln(x_ref[...]) * (1.0 + mod_ref[0, 1:2, :]) + mod_ref[0, 0:1, :]
        h_scr[...] = h.astype(BF16)
        a_ref[...] = _nt_dot(h_scr[...], w_ref[...])

    @pl.when(j > 0)
    def _():
        b_ref[...] = _nt_dot(h_scr[...], w_ref[...]).astype(BF16)


def _in_proj(xf, mod3, w4, seq):
    t, d = xf.shape
    n = w4.shape[0]
    tm, tn = 512, PART_A
    per_b = seq // tm
    return pl.pallas_call(
        _inproj_kernel,
        grid=(t // tm, n // tn),
        in_specs=[pl.BlockSpec((tm, d), lambda i, j: (i, 0)),
                  pl.BlockSpec((1, 6, d), lambda i, j: (i // per_b, 0, 0)),
                  pl.BlockSpec((tn, d), lambda i, j: (j, 0))],
        out_specs=[pl.BlockSpec((tm, tn), lambda i, j: (i, 0)),
                   pl.BlockSpec((tm, tn), lambda i, j: (i, jnp.maximum(j - 1, 0)))],
        out_shape=[jax.ShapeDtypeStruct((t, PART_A), F32),
                   jax.ShapeDtypeStruct((t, n - PART_A), BF16)],
        scratch_shapes=[pltpu.VMEM((tm, d), BF16)],
        compiler_params=_cparams(("arbitrary", "arbitrary")),
        name="in_proj",
    )(xf, mod3, w4)


def _qkv_kernel(a_ref, wq_ref, wkv_ref, gq_ref, gkv_ref, cos_ref, sin_ref, q_ref, k_ref, v_ref):
    a = a_ref[...]
    cq = a[:, :MLA_Q_RANK]
    ckv = a[:, MLA_Q_RANK:MLA_Q_RANK + MLA_KV_RANK]
    kr = a[:, 768:896]
    krs = a[:, 896:1024]
    cqn = (cq * lax.rsqrt(jnp.mean(cq * cq, axis=-1, keepdims=True) + RMS_EPS) * gq_ref[...]).astype(BF16)
    ckvn = (ckv * lax.rsqrt(jnp.mean(ckv * ckv, axis=-1, keepdims=True) + RMS_EPS) * gkv_ref[...]).astype(BF16)
    q3 = jnp.dot(cqn, wq_ref[...], preferred_element_type=F32)
    kv = jnp.dot(ckvn, wkv_ref[...], preferred_element_type=F32)
    cos = cos_ref[...]
    sin = sin_ref[...]
    krr = (kr * cos + krs * sin).astype(BF16)
    for h in range(MLA_HEADS):
        q0 = h * 3 * LANES
        c0 = h * QK_PAD
        q_ref[:, c0:c0 + LANES] = q3[:, q0:q0 + LANES].astype(BF16)
        q_ref[:, c0 + LANES:c0 + QK_PAD] = (
            q3[:, q0 + LANES:q0 + 2 * LANES] * cos + q3[:, q0 + 2 * LANES:q0 + 3 * LANES] * sin).astype(BF16)
        k_ref[:, c0:c0 + LANES] = kv[:, c0:c0 + LANES].astype(BF16)
        k_ref[:, c0 + LANES:c0 + QK_PAD] = krr
        v_ref[:, c0:c0 + LANES] = kv[:, c0 + LANES:c0 + QK_PAD].astype(BF16)
        v_ref[:, c0 + LANES:c0 + V_PAD] = jnp.ones((a.shape[0], V_PAD - LANES), BF16)


def _qkv(part_a, wq3, wkv, gq, gkv, cos_t, sin_t, seq):
    t = part_a.shape[0]
    tm = 512
    per_b = seq // tm
    hq = MLA_HEADS * QK_PAD
    return pl.pallas_call(
        _qkv_kernel,
        grid=(t // tm,),
        in_specs=[pl.BlockSpec((tm, PART_A), lambda i: (i, 0)),
                  pl.BlockSpec(wq3.shape, lambda i: (0, 0)),
                  pl.BlockSpec(wkv.shape, lambda i: (0, 0)),
                  pl.BlockSpec((1, MLA_Q_RANK), lambda i: (0, 0)),
                  pl.BlockSpec((1, MLA_KV_RANK), lambda i: (0, 0)),
                  pl.BlockSpec((tm, LANES), lambda i: (i % per_b, 0)),
                  pl.BlockSpec((tm, LANES), lambda i: (i % per_b, 0))],
        out_specs=[pl.BlockSpec((tm, hq), lambda i: (i, 0)),
                   pl.BlockSpec((tm, hq), lambda i: (i, 0)),
                   pl.BlockSpec((tm, MLA_HEADS * V_PAD), lambda i: (i, 0))],
        out_shape=[jax.ShapeDtypeStruct((t, hq), BF16),
                   jax.ShapeDtypeStruct((t, hq), BF16),
                   jax.ShapeDtypeStruct((t, MLA_HEADS * V_PAD), BF16)],
        compiler_params=_cparams(("arbitrary",)),
        name="qkv",
    )(part_a, wq3, wkv, gq, gkv, cos_t, sin_t)


ATTN_TILE = 512
ATTN_CHAINS = 8


def _attn_kernel(*refs, c, masked):
    if masked:
        q_ref, k_ref, v_ref, bias_ref, o_ref = refs[:5]
    else:
        q_ref, k_ref, v_ref, o_ref = refs[:4]
    nch = ATTN_CHAINS
    scratch = refs[-4 * nch:]
    tile = ATTN_TILE
    i = pl.program_id(2)
    chains = [dict(rows=slice(n * tile, (n + 1) * tile), s=scratch[4 * n:4 * n + 2], m=scratch[4 * n + 2],
                   acc=scratch[4 * n + 3]) for n in range(nch)]

    def put_scores(chain, slot, t):
        k = k_ref[pl.ds(pl.multiple_of(t * tile, tile), tile), :]
        chain["s"][slot][...] = _nt_dot(q_ref[chain["rows"], :], k)

    def tile_step(chain, slot, t, kind, prefetch=True):
        if prefetch:
            put_scores(chain, 1 - slot, t + 1)
        m_scr, acc_scr = chain["m"], chain["acc"]
        s = chain["s"][slot][...]
        if masked:
            if kind == "prev":
                s = s + bias_ref[0, 1]
            elif kind == "diag":
                s = s + bias_ref[0, 0]
        elif kind == "diag":
            row = lax.broadcasted_iota(jnp.int32, (tile, tile), 0)
            col = lax.broadcasted_iota(jnp.int32, (tile, tile), 1)
            s = jnp.where(row >= col, s, NEG)
        m_old = m_scr[...]
        m_new = jnp.maximum(m_old, jnp.broadcast_to(jnp.max(s, axis=-1, keepdims=True), m_old.shape))
        alpha = jnp.exp2((m_old - m_new) * c)
        p = jnp.concatenate([jnp.exp2(((s[:, j * LANES:(j + 1) * LANES] - m_new) * c).astype(BF16))
                             for j in range(tile // LANES)], axis=1)
        v = v_ref[pl.ds(pl.multiple_of(t * tile, tile), tile), :]
        pv = jnp.dot(p, v, preferred_element_type=F32)
        acc_scr[...] = jnp.concatenate([alpha, alpha], axis=1) * acc_scr[...] + pv
        m_scr[...] = m_new

    for chain in chains:
        chain["m"][...] = jnp.full(chain["m"].shape, NEG, F32)
        chain["acc"][...] = jnp.zeros(chain["acc"].shape, F32)
        put_scores(chain, 0, 0)

    def far_pair(j, _):
        for slot in range(2):
            for chain in chains:
                tile_step(chain, slot, 2 * j + slot, "far")
        return 0

    first = nch * i
    if masked:
        lax.fori_loop(0, jnp.maximum(first // 2 - 1, 0), far_pair, 0)

        @pl.when(i > 0)
        def _():
            for chain in chains:
                tile_step(chain, 0, first - 2, "far")
            for n, chain in enumerate(chains):
                tile_step(chain, 1, first - 1, "prev" if n == 0 else "far")
    else:
        lax.fori_loop(0, first // 2, far_pair, 0)

    for k in range(nch):
        for n, chain in enumerate(chains):
            if n < k:
                continue
            kind = "diag" if n == k else ("prev" if masked and n == k + 1 else "far")
            tile_step(chain, k % 2, first + k, kind, prefetch=(n != k))

    for chain in chains:
        acc = chain["acc"]
        o_ref[chain["rows"], :] = (acc[:, :LANES] / acc[:, LANES:]).astype(o_ref.dtype)


def _attn_scratch():
    tile = ATTN_TILE
    per_chain = [pltpu.VMEM((tile, tile), F32), pltpu.VMEM((tile, tile), F32),
                 pltpu.VMEM((tile, LANES), F32), pltpu.VMEM((tile, V_PAD), F32)]
    return per_chain * ATTN_CHAINS


def _mla_attention(q, k, v, batch, seq):
    tile = ATTN_CHAINS * ATTN_TILE
    nq = seq // tile
    c = math.log2(math.e) / math.sqrt(MLA_NOPE_DIM + MLA_ROPE_DIM)
    return pl.pallas_call(
        functools.partial(_attn_kernel, c=c, masked=False),
        grid=(batch, MLA_HEADS, nq),
        in_specs=[pl.BlockSpec((tile, QK_PAD), lambda b, h, i: (b * nq + i, h)),
                  pl.BlockSpec((seq, QK_PAD), lambda b, h, i: (b, h)),
                  pl.BlockSpec((seq, V_PAD), lambda b, h, i: (b, h))],
        out_specs=pl.BlockSpec((tile, MLA_V_DIM), lambda b, h, i: (b * nq + i, h)),
        out_shape=jax.ShapeDtypeStruct((batch * seq, MLA_HEADS * MLA_V_DIM), BF16),
        scratch_shapes=_attn_scratch(),
        compiler_params=_cparams(("arbitrary", "arbitrary", "arbitrary")),
        name="mla_attn",
    )(q, k, v)


def _moba_select_kernel(q_ref, k_ref, v_ref, qa_ref, ka_ref, va_ref, *, seq, nb):
    kf = k_ref[...].astype(F32)
    km = jnp.sum(kf.reshape(nb, MOBA_BLOCK, MOBA_HEAD_DIM), axis=1) * (1.0 / MOBA_BLOCK)
    km_hi = km.astype(BF16)
    km_lo = (km - km_hi.astype(F32)).astype(BF16)
    q = q_ref[...]
    gate = _nt_dot(km_hi, q) + _nt_dot(km_lo, q)
    shift = MOBA_BLOCK.bit_length() - 1
    blk = lax.broadcasted_iota(jnp.int32, (nb, seq), 0)
    qblk = jnp.right_shift(lax.broadcasted_iota(jnp.int32, (nb, seq), 1), shift)
    g = jnp.where(blk < qblk, gate, NEG)
    visible = blk == qblk
    for _ in range(MOBA_TOPK):
        mx = jnp.max(g, axis=0, keepdims=True)
        first = jnp.min(jnp.where(g == mx, blk, nb), axis=0, keepdims=True)
        pick = (blk == first) & (mx > 0.5 * NEG)
        visible = visible | pick
        g = jnp.where(pick, NEG, g)
    mask_t = jnp.concatenate([jnp.where(visible, 0.0, NEG), jnp.zeros((LANES - nb, seq), F32)], axis=0)
    qa_ref[:, :MOBA_HEAD_DIM] = q
    qa_ref[:, MOBA_HEAD_DIM:] = mask_t.T.astype(BF16)
    lane = lax.broadcasted_iota(jnp.int32, (seq, LANES), 1)
    own = jnp.right_shift(lax.broadcasted_iota(jnp.int32, (seq, LANES), 0), shift)
    ka_ref[:, :MOBA_HEAD_DIM] = k_ref[...]
    ka_ref[:, MOBA_HEAD_DIM:] = jnp.where(lane == own, 1.0, 0.0).astype(BF16)
    va_ref[:, :MOBA_HEAD_DIM] = v_ref[...]
    va_ref[:, MOBA_HEAD_DIM:] = jnp.ones((seq, V_PAD - MOBA_HEAD_DIM), BF16)


def _moba_select(qkv_mo, batch, seq):
    nb = seq // MOBA_BLOCK
    assert MOBA_HEAD_DIM == LANES and nb <= QK_PAD - MOBA_HEAD_DIM
    wide = jax.ShapeDtypeStruct((batch * seq, MOBA_HEADS * QK_PAD), BF16)
    return pl.pallas_call(
        functools.partial(_moba_select_kernel, seq=seq, nb=nb),
        grid=(batch, MOBA_HEADS),
        in_specs=[pl.BlockSpec((seq, MOBA_HEAD_DIM), lambda b, h: (b, h)),
                  pl.BlockSpec((seq, MOBA_HEAD_DIM), lambda b, h: (b, MOBA_HEADS + h)),
                  pl.BlockSpec((seq, MOBA_HEAD_DIM), lambda b, h: (b, 2 * MOBA_HEADS + h))],
        out_specs=[pl.BlockSpec((seq, QK_PAD), lambda b, h: (b, h)),
                   pl.BlockSpec((seq, QK_PAD), lambda b, h: (b, h)),
                   pl.BlockSpec((seq, V_PAD), lambda b, h: (b, h))],
        out_shape=[wide, wide, jax.ShapeDtypeStruct((batch * seq, MOBA_HEADS * V_PAD), BF16)],
        compiler_params=_cparams(("arbitrary", "arbitrary")),
        name="moba_select",
    )(qkv_mo, qkv_mo, qkv_mo)


def _t5_kernel(tab_ref, o_ref, *, inv_scale):
    h = pl.program_id(0)
    r = lax.broadcasted_iota(jnp.int32, (LANES, LANES), 0)
    c = lax.broadcasted_iota(jnp.int32, (LANES, LANES), 1)
    max_exact = T5_BUCKETS // 2
    far = tab_ref[T5_BUCKETS - 1, h]

    def block(offset):
        rel = offset + r - c
        n = jnp.maximum(rel, 0)
        nf = jnp.maximum(n, 1).astype(F32)
        large = max_exact + (jnp.log(nf / max_exact) / math.log(T5_MAX_DISTANCE / max_exact)
                             * (T5_BUCKETS - max_exact)).astype(jnp.int32)
        large = jnp.minimum(large, T5_BUCKETS - 1)
        bucket = jnp.where(n < max_exact, n, large)
        bias = jnp.zeros((LANES, LANES), F32)
        for j in range(T5_BUCKETS):
            bias = jnp.where(bucket == j, tab_ref[j, h], bias)
        return jnp.where(rel >= 0, (bias - far) * inv_scale, NEG)

    near = {0: block(0), 1: block(LANES)}
    nblk = ATTN_TILE // LANES
    for d in range(2):
        for i in range(nblk):
            for j in range(nblk):
                k = d * nblk + i - j
                if k < 0:
                    val = jnp.full((LANES, LANES), NEG, F32)
                else:
                    val = near.get(k, jnp.zeros((LANES, LANES), F32))
                o_ref[0, d, i * LANES:(i + 1) * LANES, j * LANES:(j + 1) * LANES] = val


def _t5_tiles(t5_table):
    assert LANES >= T5_MAX_DISTANCE
    tile = ATTN_TILE
    return pl.pallas_call(
        functools.partial(_t5_kernel, inv_scale=math.sqrt(MOBA_HEAD_DIM)),
        grid=(MOBA_HEADS,),
        in_specs=[pl.BlockSpec(memory_space=pltpu.SMEM)],
        out_specs=pl.BlockSpec((1, 2, tile, tile), lambda h: (h, 0, 0, 0)),
        out_shape=jax.ShapeDtypeStruct((MOBA_HEADS, 2, tile, tile), F32),
        compiler_params=_cparams(("arbitrary",)),
        name="t5_tiles",
    )(t5_table)


def _moba_attention(q_wide, k_wide, v_wide, bias, batch, seq):
    tile = ATTN_CHAINS * ATTN_TILE
    nq = seq // tile
    c = math.log2(math.e) / math.sqrt(MOBA_HEAD_DIM)
    dh = MOBA_HEAD_DIM
    return pl.pallas_call(
        functools.partial(_attn_kernel, c=c, masked=True),
        grid=(batch, MOBA_HEADS, nq),
        in_specs=[pl.BlockSpec((tile, QK_PAD), lambda b, h, i: (b * nq + i, h)),
                  pl.BlockSpec((seq, QK_PAD), lambda b, h, i: (b, h)),
                  pl.BlockSpec((seq, V_PAD), lambda b, h, i: (b, h)),
                  pl.BlockSpec((1, 2, ATTN_TILE, ATTN_TILE), lambda b, h, i: (h, 0, 0, 0))],
        out_specs=pl.BlockSpec((tile, dh), lambda b, h, i: (b * nq + i, h)),
        out_shape=jax.ShapeDtypeStruct((batch * seq, MOBA_WIDTH), BF16),
        scratch_shapes=_attn_scratch(),
        compiler_params=_cparams(("arbitrary", "arbitrary", "arbitrary")),
        name="moba_attn",
    )(q_wide, k_wide, v_wide, bias)


def _outproj_kernel(oa_ref, ob_ref, wa_ref, wb_ref, x_ref, mod_ref, g_ref, b_ref, wr_ref,
                    x1_ref, h2_ref, lg_ref, y_a, y_b, *, alpha, n_tiles):
    i = pl.program_id(0)
    tm, d = y_a.shape
    chunks = 4
    cn, cr = d // chunks, tm // chunks

    def matmul_into(y_ref, c):
        cols = slice(c * cn, (c + 1) * cn)
        y_ref[:, cols] = (jnp.dot(oa_ref[...], wa_ref[:, cols], preferred_element_type=F32)
                          + jnp.dot(ob_ref[...], wb_ref[:, cols], preferred_element_type=F32))

    def epilogue(y_ref, c):
        rows = slice(c * cr, (c + 1) * cr)
        z = alpha * x_ref[rows, :] + mod_ref[0, 2:3, :] * y_ref[rows, :]
        x1 = _ln(z) * g_ref[...] + b_ref[...]
        x1_ref[rows, :] = x1
        h2 = _ln(x1) * (1.0 + mod_ref[0, 4:5, :]) + mod_ref[0, 3:4, :]
        h2_ref[rows, :] = h2
        h_hi = h2.astype(BF16)
        h_lo = (h2 - h_hi.astype(F32)).astype(BF16)
        zz = (jnp.dot(h_hi, wr_ref[...], preferred_element_type=F32)
              + jnp.dot(h_lo, wr_ref[...], preferred_element_type=F32))
        lg_ref[rows, :] = zz + pltpu.roll(zz, LANES // 2, 1)

    def step(y_new, y_old):
        for c in range(chunks):
            if y_new is not None:
                matmul_into(y_new, c)
            if y_old is not None:
                epilogue(y_old, c)

    inner = (i > 0) & (i < n_tiles)
    pl.when(i == 0)(lambda: step(y_a, None))
    pl.when(inner & (i % 2 == 0))(lambda: step(y_a, y_b))
    pl.when(inner & (i % 2 == 1))(lambda: step(y_b, y_a))
    pl.when(i == n_tiles)(lambda: step(None, y_b if n_tiles % 2 == 0 else y_a))


def _out_proj(o_mla, o_moba, wo, xf, mod3, ln_g, ln_b, wr, seq, alpha):
    t, d = xf.shape
    tm = 512
    n = t // tm
    per_b = seq // tm
    ka, kb = o_mla.shape[1], o_moba.shape[1]
    assert ka == kb and wo.shape[0] == ka + kb
    once = pl.Buffered(1)

    def cur(i):
        return (jnp.minimum(i, n - 1), 0)

    def lag(i):
        return (jnp.maximum(i - 1, 0), 0)

    return pl.pallas_call(
        functools.partial(_outproj_kernel, alpha=alpha, n_tiles=n),
        grid=(n + 1,),
        in_specs=[pl.BlockSpec((tm, ka), cur),
                  pl.BlockSpec((tm, kb), cur),
                  pl.BlockSpec((ka, d), lambda i: (0, 0), pipeline_mode=once),
                  pl.BlockSpec((kb, d), lambda i: (1, 0), pipeline_mode=once),
                  pl.BlockSpec((tm, d), lag),
                  pl.BlockSpec((1, 6, d), lambda i: (jnp.maximum(i - 1, 0) // per_b, 0, 0)),
                  pl.BlockSpec((1, d), lambda i: (0, 0)),
                  pl.BlockSpec((1, d), lambda i: (0, 0)),
                  pl.BlockSpec((d, LANES), lambda i: (0, 0), pipeline_mode=once)],
        out_specs=[pl.BlockSpec((tm, d), lag),
                   pl.BlockSpec((tm, d), lag),
                   pl.BlockSpec((tm, LANES), lag)],
        out_shape=[jax.ShapeDtypeStruct((t, d), F32),
                   jax.ShapeDtypeStruct((t, d), F32),
                   jax.ShapeDtypeStruct((t, LANES), F32)],
        scratch_shapes=[pltpu.VMEM((tm, d), F32), pltpu.VMEM((tm, d), F32)],
        compiler_params=_cparams(("arbitrary",)),
        name="out_proj",
    )(o_mla, o_moba, wo, wo, xf, mod3, ln_g, ln_b, wr)


def _route_kernel(lg_ref, br_ref, info_ref, cnt_ref, run_scr, *, tm):
    i = pl.program_id(0)

    @pl.when(i == 0)
    def _():
        run_scr[...] = jnp.zeros_like(run_scr)

    lg = lg_ref[...] + br_ref[...]
    lane = lax.broadcasted_iota(jnp.int32, (tm, LANES), 1)
    e_lo, e_hi = MOE_GROUPS, MOE_GROUPS + MOE_N_EXPERTS
    is_g = lane < e_lo
    gl = jnp.where(is_g, lg, NEG)
    gmax = jnp.max(gl, axis=-1, keepdims=True)
    gidx = jnp.min(jnp.where(gl == gmax, lane, LANES), axis=-1, keepdims=True)
    g_p = 1.0 / jnp.sum(jnp.where(is_g, jnp.exp(gl - gmax), 0.0), axis=-1, keepdims=True)
    grp_of_lane = jnp.right_shift(lane - e_lo, MOE_EXPERTS_PER_GROUP.bit_length() - 1)
    in_grp = (lane >= e_lo) & (lane < e_hi) & (grp_of_lane == gidx)
    el = jnp.where(in_grp, lg, NEG)
    m1 = jnp.max(el, axis=-1, keepdims=True)
    l1 = jnp.min(jnp.where(el == m1, lane, LANES), axis=-1, keepdims=True)
    el2 = jnp.where(lane == l1, NEG, el)
    m2 = jnp.max(el2, axis=-1, keepdims=True)
    l2 = jnp.min(jnp.where(el2 == m2, lane, LANES), axis=-1, keepdims=True)
    zsum = jnp.sum(jnp.where(in_grp, jnp.exp(el - m1), 0.0), axis=-1, keepdims=True)
    p1 = 1.0 / zsum
    p2 = jnp.exp(m2 - m1) / zsum
    wa = g_p * (p1 / (p1 + p2))
    wb = g_p * (p2 / (p1 + p2))
    hot_a = lane == l1
    hot_b = lane == l2
    onehot = jnp.where(hot_a | hot_b, 1.0, 0.0)
    r = lax.broadcasted_iota(jnp.int32, (tm, tm), 0)
    c = lax.broadcasted_iota(jnp.int32, (tm, tm), 1)
    lower = jnp.where(c < r, 1.0, 0.0).astype(BF16)
    before = jnp.dot(lower, onehot.astype(BF16), preferred_element_type=F32) + run_scr[...]
    rank_a = jnp.sum(jnp.where(hot_a, before, 0.0), axis=-1, keepdims=True)
    rank_b = jnp.sum(jnp.where(hot_b, before, 0.0), axis=-1, keepdims=True)
    run_scr[...] += jnp.sum(onehot, axis=0, keepdims=True)
    info = jnp.zeros((tm, LANES), F32)
    for k, val in enumerate([(l1 - e_lo).astype(F32), (l2 - e_lo).astype(F32), wa, wb, rank_a, rank_b]):
        info = jnp.where(lane == k, val, info)
    info_ref[...] = info
    cnt_ref[...] = run_scr[...]


def _route(logits, br):
    t = logits.shape[0]
    tm = 512
    return pl.pallas_call(
        functools.partial(_route_kernel, tm=tm),
        grid=(t // tm,),
        in_specs=[pl.BlockSpec((tm, LANES), lambda i: (i, 0)),
                  pl.BlockSpec((1, LANES), lambda i: (0, 0))],
        out_specs=[pl.BlockSpec((tm, LANES), lambda i: (i, 0)),
                   pl.BlockSpec((1, LANES), lambda i: (0, 0))],
        out_shape=[jax.ShapeDtypeStruct((t, LANES), F32),
                   jax.ShapeDtypeStruct((1, LANES), F32)],
        scratch_shapes=[pltpu.VMEM((1, LANES), F32)],
        compiler_params=_cparams(("arbitrary",)),
        name="route",
    )(logits, br)


def _pos_kernel(info_ref, start_ref, pos_ref):
    info = info_ref[...]
    tm = info.shape[0]
    lane = lax.broadcasted_iota(jnp.int32, (tm, LANES), 1)
    start = start_ref[...]
    cols = []
    for k in range(2):
        e = jnp.sum(jnp.where(lane == k, info, 0.0), axis=-1, keepdims=True).astype(jnp.int32)
        rank = jnp.sum(jnp.where(lane == 4 + k, info, 0.0), axis=-1, keepdims=True)
        base = jnp.sum(jnp.where(lane == e + MOE_GROUPS, start, 0.0), axis=-1, keepdims=True)
        cols.append(base + rank)
    wide = jnp.where(lane == 0, cols[0], jnp.where(lane == 1, cols[1], 0.0))
    pos_ref[...] = wide.T[:pos_ref.shape[0], :].astype(jnp.int32)


def _positions(info, start_lanes):
    t = info.shape[0]
    tm = 1024
    return pl.pallas_call(
        _pos_kernel,
        grid=(t // tm,),
        in_specs=[pl.BlockSpec((tm, LANES), lambda i: (i, 0)),
                  pl.BlockSpec((1, LANES), lambda i: (0, 0))],
        out_specs=pl.BlockSpec((8, tm), lambda i: (0, i)),
        out_shape=jax.ShapeDtypeStruct((8, t), jnp.int32),
        compiler_params=_cparams(("arbitrary",)),
        name="positions",
    )(info, start_lanes)


def _row_copy(src_ref, src_row, dst_ref, dst_row, sem):
    return pltpu.make_async_copy(src_ref.at[pl.ds(src_row, 1)], dst_ref.at[pl.ds(dst_row, 1)], sem)


def _dispatch_kernel(pa_ref, pb_ref, pad0_ref, padn_ref, used_ref, h_ref, xs_ref, ztile, sem, zsem,
                     *, tm, tr, n_tiles):
    i = pl.program_id(0)
    base = i * tm

    def zero_fill(act):
        def whole_tile(j, _):
            act(pltpu.make_async_copy(ztile, xs_ref.at[pl.ds(pl.multiple_of(j * tr, tr), tr)], zsem))
            return 0

        lax.fori_loop(used_ref[0], n_tiles, whole_tile, 0)

        def expert_pad(e, _):
            n, start = padn_ref[e], pad0_ref[e]
            head = jnp.minimum(jnp.bitwise_and(-start, SUBLANES - 1), n)
            for k in range(SUBLANES - 1):
                pl.when(k < head)(functools.partial(act, _row_copy(ztile, 0, xs_ref, start + k, zsem)))
            off = start + head
            groups = jnp.right_shift(n - head, SUBLANES.bit_length() - 1)
            for bit in reversed(range((tr // SUBLANES - 1).bit_length())):
                size = SUBLANES << bit
                take = jnp.bitwise_and(jnp.right_shift(groups, bit), 1)
                dst = xs_ref.at[pl.ds(pl.multiple_of(off, SUBLANES), size)]
                pl.when(take == 1)(functools.partial(act, pltpu.make_async_copy(ztile.at[pl.ds(0, size)], dst, zsem)))
                off = off + take * size
            return 0

        lax.fori_loop(0, MOE_N_EXPERTS, expert_pad, 0)

    @pl.when(i == 0)
    def _():
        ztile[...] = jnp.zeros(ztile.shape, ztile.dtype)
        zero_fill(lambda cp: cp.start())

    def issue(t, _):
        _row_copy(h_ref, t, xs_ref, pa_ref[base + t], sem).start()
        _row_copy(h_ref, t, xs_ref, pb_ref[base + t], sem).start(priority=1)
        return 0

    lax.fori_loop(0, tm, issue, 0, unroll=8)
    for _ in range(2):
        pltpu.make_async_copy(h_ref, xs_ref.at[pl.ds(0, tm)], sem).wait()

    @pl.when(i == pl.num_programs(0) - 1)
    def _():
        zero_fill(lambda cp: cp.wait())


def _dispatch(pos_a, pos_b, pad0, padn, used, h2, tr, n_tiles):
    t, d = h2.shape
    tm = 256
    grid_spec = pltpu.PrefetchScalarGridSpec(
        num_scalar_prefetch=5,
        grid=(t // tm,),
        in_specs=[pl.BlockSpec((tm, d), lambda i, *_: (i, 0))],
        out_specs=pl.BlockSpec(memory_space=pl.ANY),
        scratch_shapes=[pltpu.VMEM((tr, d), F32), pltpu.SemaphoreType.DMA(()), pltpu.SemaphoreType.DMA(())],
    )
    return pl.pallas_call(
        functools.partial(_dispatch_kernel, tm=tm, tr=tr, n_tiles=n_tiles),
        grid_spec=grid_spec,
        out_shape=jax.ShapeDtypeStruct((n_tiles * tr, d), F32),
        compiler_params=_cparams(("arbitrary",), row_dma=True),
        name="dispatch",
    )(pos_a, pos_b, pad0, padn, used, h2)


def _experts_kernel(texp_ref, tidx_ref, nexte_ref, used_ref, x_ref, w1_hbm, w3_hbm, w2_hbm, o_ref,
                    w1_f32, w3_f32, w2_f32, w1_scr, w3_scr, w2_scr, sems):
    i = pl.program_id(0)
    prev = texp_ref[jnp.maximum(i - 1, 0)]

    def weight_copies(e):
        return [pltpu.make_async_copy(w1_hbm.at[e], w1_f32, sems.at[0]),
                pltpu.make_async_copy(w3_hbm.at[e], w3_f32, sems.at[1]),
                pltpu.make_async_copy(w2_hbm.at[e], w2_f32, sems.at[2])]

    @pl.when(i == 0)
    def _():
        for cp in weight_copies(texp_ref[0]):
            cp.start()

    @pl.when((i == 0) | (texp_ref[i] != prev))
    def _():
        for cp in weight_copies(texp_ref[i]):
            cp.wait()
        w1_scr[...] = w1_f32[...].astype(BF16)
        w3_scr[...] = w3_f32[...].astype(BF16)
        w2_scr[...] = w2_f32[...].astype(BF16)

        @pl.when(nexte_ref[i] >= 0)
        def _():
            for cp in weight_copies(nexte_ref[i]):
                cp.start(priority=1)

    @pl.when(i < used_ref[0])
    def _():
        x = x_ref[...].astype(BF16)
        a = jnp.dot(x, w1_scr[...], preferred_element_type=F32)
        b = jnp.dot(x, w3_scr[...], preferred_element_type=F32)
        hid = (a / (1.0 + jnp.exp(-a))) * b
        o_ref[...] = jnp.dot(hid.astype(BF16), w2_scr[...], preferred_element_type=F32)

    @pl.when(i >= used_ref[0])
    def _():
        o_ref[...] = jnp.zeros_like(o_ref)


def _experts(texp, tidx, nexte, used, xs, w1, w3, w2, tr):
    rows, d = xs.shape
    nt = rows // tr
    f = w1.shape[2]
    grid_spec = pltpu.PrefetchScalarGridSpec(
        num_scalar_prefetch=4,
        grid=(nt,),
        in_specs=[pl.BlockSpec((tr, d), lambda i, te, ti, ne, u: (ti[i], 0)),
                  pl.BlockSpec(memory_space=pl.ANY),
                  pl.BlockSpec(memory_space=pl.ANY),
                  pl.BlockSpec(memory_space=pl.ANY)],
        out_specs=pl.BlockSpec((tr, d), lambda i, te, ti, ne, u: (i, 0)),
        scratch_shapes=[pltpu.VMEM((d, f), F32), pltpu.VMEM((d, f), F32), pltpu.VMEM((f, d), F32),
                        pltpu.VMEM((d, f), BF16), pltpu.VMEM((d, f), BF16), pltpu.VMEM((f, d), BF16),
                        pltpu.SemaphoreType.DMA((3,))],
    )
    return pl.pallas_call(
        _experts_kernel,
        grid_spec=grid_spec,
        out_shape=jax.ShapeDtypeStruct((rows, d), F32),
        compiler_params=_cparams(("arbitrary",)),
        name="experts",
    )(texp, tidx, nexte, used, xs, w1, w3, w2)


def _combine_kernel(pa_ref, pb_ref, ys_ref, x1_ref, info_ref, mod_ref, g_ref, b_ref, o_ref,
                    buf_a, buf_b, sems, *, tm, alpha):
    i = pl.program_id(0)

    def gather(tile, slot):
        base = tile * tm

        def issue(t, _):
            _row_copy(ys_ref, pa_ref[base + t], buf_a.at[slot], t, sems.at[slot]).start()
            _row_copy(ys_ref, pb_ref[base + t], buf_b.at[slot], t, sems.at[slot]).start(priority=1)
            return 0

        lax.fori_loop(0, tm, issue, 0, unroll=8)

    @pl.when(i == 0)
    def _():
        gather(0, 0)

    @pl.when(i + 1 < pl.num_programs(0))
    def _():
        gather(i + 1, (i + 1) % 2)

    slot = i % 2
    for buf in (buf_a, buf_b):
        pltpu.make_async_copy(ys_ref.at[pl.ds(0, tm)], buf.at[slot], sems.at[slot]).wait()
    info = info_ref[...]
    y = info[:, 2:3] * buf_a[slot] + info[:, 3:4] * buf_b[slot]
    z = alpha * x1_ref[...] + mod_ref[0, 5:6, :] * y
    o_ref[...] = _ln(z) * g_ref[...] + b_ref[...]


def _combine(pos_a, pos_b, ys, x1, info, mod3, ln_g, ln_b, seq, alpha):
    t, d = x1.shape
    tm = 256
    per_b = seq // tm
    grid_spec = pltpu.PrefetchScalarGridSpec(
        num_scalar_prefetch=2,
        grid=(t // tm,),
        in_specs=[pl.BlockSpec(memory_space=pl.ANY),
                  pl.BlockSpec((tm, d), lambda i, pa, pb: (i, 0)),
                  pl.BlockSpec((tm, LANES), lambda i, pa, pb: (i, 0)),
                  pl.BlockSpec((1, 6, d), lambda i, pa, pb: (i // per_b, 0, 0)),
                  pl.BlockSpec((1, d), lambda i, pa, pb: (0, 0)),
                  pl.BlockSpec((1, d), lambda i, pa, pb: (0, 0))],
        out_specs=pl.BlockSpec((tm, d), lambda i, pa, pb: (i, 0)),
        scratch_shapes=[pltpu.VMEM((2, tm, d), F32), pltpu.VMEM((2, tm, d), F32), pltpu.SemaphoreType.DMA((2,))],
    )
    return pl.pallas_call(
        functools.partial(_combine_kernel, tm=tm, alpha=alpha),
        grid_spec=grid_spec,
        out_shape=jax.ShapeDtypeStruct((t, d), F32),
        compiler_params=_cparams(("arbitrary",), row_dma=True),
        name="combine",
    )(pos_a, pos_b, ys, x1, info, mod3, ln_g, ln_b)


def _prep_w_in_kernel(w_ref, o_ref):
    r0 = MLA_Q_RANK + MLA_KV_RANK
    r1 = r0 + MLA_ROPE_DIM
    half = MLA_ROPE_DIM // 2
    z = jnp.zeros((LANES - MLA_ROPE_DIM, o_ref.shape[1]), BF16)
    o_ref[:r1, :] = w_ref[:r1, :].astype(BF16)
    o_ref[r1:r0 + LANES, :] = z
    o_ref[r0 + LANES:r0 + LANES + half, :] = w_ref[r0 + half:r1, :].astype(BF16)
    o_ref[r0 + LANES + half:r0 + LANES + MLA_ROPE_DIM, :] = w_ref[r0:r0 + half, :].astype(BF16)
    o_ref[r0 + LANES + MLA_ROPE_DIM:PART_A, :] = z
    o_ref[PART_A:, :] = w_ref[r1:, :].astype(BF16)


def _prep_w_in(w_t):
    n, k = w_t.shape
    tk = 512
    n_out = PART_A + 3 * MOBA_WIDTH
    assert n == MLA_Q_RANK + MLA_KV_RANK + MLA_ROPE_DIM + 3 * MOBA_WIDTH
    return pl.pallas_call(
        _prep_w_in_kernel,
        grid=(k // tk,),
        in_specs=[pl.BlockSpec((n, tk), lambda i: (0, i))],
        out_specs=pl.BlockSpec((n_out, tk), lambda i: (0, i)),
        out_shape=jax.ShapeDtypeStruct((n_out, k), BF16),
        compiler_params=_cparams(("arbitrary",)),
        name="prep_w_in",
    )(w_t)


def _prep_w_uq(w):
    r = w.shape[0]
    w = w.reshape(r, MLA_HEADS, MLA_NOPE_DIM + MLA_ROPE_DIM)
    half = MLA_ROPE_DIM // 2
    nope = w[:, :, :MLA_NOPE_DIM]
    x1 = w[:, :, MLA_NOPE_DIM:MLA_NOPE_DIM + half]
    x2 = w[:, :, MLA_NOPE_DIM + half:]
    z = jnp.zeros((r, MLA_HEADS, LANES - MLA_ROPE_DIM), w.dtype)
    return jnp.concatenate([nope, x1, x2, z, x2, x1, z], axis=2).reshape(r, MLA_HEADS * 3 * LANES).astype(BF16)


def _rope_lanes(seq):
    inv = 1.0 / (ROPE_THETA ** (jnp.arange(0, MLA_ROPE_DIM, 2, dtype=F32) / MLA_ROPE_DIM))
    ang = jnp.arange(seq, dtype=F32)[:, None] * inv[None, :]
    cos, sin = jnp.cos(ang), jnp.sin(ang)
    z = jnp.zeros((seq, LANES - MLA_ROPE_DIM), F32)
    return jnp.concatenate([cos, cos, z], axis=1), jnp.concatenate([-sin, sin, z], axis=1)


def _prep_router(w_rg, b_rg, w_re, b_re):
    d = w_rg.shape[0]
    w = jnp.concatenate([w_rg, w_re], axis=1)
    n = w.shape[1]
    hi = w.astype(BF16)
    lo = (w - hi.astype(F32)).astype(BF16)
    z = jnp.zeros((d, LANES // 2 - n), BF16)
    wr = jnp.concatenate([hi, z, lo, z], axis=1)
    br = jnp.zeros((1, LANES), F32).at[0, :n].set(jnp.concatenate([b_rg, b_re]))
    return wr, br


def _layer(xf, mod3, batch, seq, depth_alpha, w_in, q_norm_g, w_uq, kv_norm_g, w_ukv, w_out, bias_tiles,
           cos_t, sin_t, ln1_g, ln1_b, w_rg, b_rg, w_re, b_re, w1, w3, w2, ln2_g, ln2_b):
    t, d = xf.shape
    part_a, qkv_mo = _in_proj(xf, mod3, _prep_w_in(w_in.T), seq)
    q, k, v = _qkv(part_a, _prep_w_uq(w_uq), w_ukv.astype(BF16), q_norm_g.reshape(1, -1),
                   kv_norm_g.reshape(1, -1), cos_t, sin_t, seq)
    o_mla = _mla_attention(q, k, v, batch, seq)
    q_wide, k_wide, v_wide = _moba_select(qkv_mo, batch, seq)
    o_moba = _moba_attention(q_wide, k_wide, v_wide, bias_tiles, batch, seq)
    wo = w_out.astype(BF16)
    wr, br = _prep_router(w_rg, b_rg, w_re, b_re)
    x1, h2, logits = _out_proj(o_mla, o_moba, wo, xf, mod3, ln1_g.reshape(1, d),
                               ln1_b.reshape(1, d), wr, seq, depth_alpha)
    info, counts = _route(logits, br)
    tr = 256
    nt = (2 * t) // tr + MOE_N_EXPERTS
    cnt = counts[0, MOE_GROUPS:MOE_GROUPS + MOE_N_EXPERTS].astype(jnp.int32)
    ntile = (cnt + tr - 1) // tr
    tile_end = jnp.cumsum(ntile)
    tile_start = tile_end - ntile
    used = tile_end[-1]
    start_lanes = jnp.zeros((1, LANES), F32).at[0, MOE_GROUPS:MOE_GROUPS + MOE_N_EXPERTS].set(
        (tile_start * tr).astype(F32))
    pos = _positions(info, start_lanes)
    pos_a, pos_b = pos[0], pos[1]
    tidx = jnp.minimum(jnp.arange(nt, dtype=jnp.int32), used - 1)
    texp = jnp.sum(tidx[:, None] >= tile_end[None, :], axis=1).astype(jnp.int32)
    eids = jnp.arange(MOE_N_EXPERTS, dtype=jnp.int32)
    later = (eids[None, :] > eids[:, None]) & (ntile[None, :] > 0)
    next_nonempty = jnp.where(later.any(axis=1), jnp.argmax(later, axis=1), -1).astype(jnp.int32)
    nexte = jnp.sum(jnp.where(texp[:, None] == eids[None, :], next_nonempty[None, :], 0), axis=1).astype(jnp.int32)
    used1 = used.reshape(1).astype(jnp.int32)
    pad0 = (tile_start * tr + cnt).astype(jnp.int32)
    padn = (ntile * tr - cnt).astype(jnp.int32)
    xs = _dispatch(pos_a, pos_b, pad0, padn, used1, h2, tr, nt)
    ys = _experts(texp, tidx, nexte, used1, xs, w1, w3, w2, tr)
    return _combine(pos_a, pos_b, ys, x1, info, mod3, ln2_g.reshape(1, d), ln2_b.reshape(1, d), seq, depth_alpha)


def kernel(x, c, w_ada, b_ada, w_in, q_norm_g, w_uq, kv_norm_g, w_ukv, w_out, t5_table, ln1_g, ln1_b,
           w_router_group, b_router_group, w_router_expert, b_router_expert, w1, w3, w2, ln2_g, ln2_b):
    batch, seq, d = x.shape
    depth = w_ada.shape[0]
    alpha = (2.0 * depth) ** 0.25
    cos_t, sin_t = _rope_lanes(seq)
    bias_tiles = _t5_tiles(t5_table)
    xf = x.reshape(batch * seq, d)
    for l in range(depth):
        mod3 = _ada_mod(c, w_ada[l], b_ada[l]).reshape(batch, 6, d)
        xf = _layer(xf, mod3, batch, seq, alpha, w_in[l], q_norm_g[l], w_uq[l], kv_norm_g[l], w_ukv[l],
                    w_out[l], bias_tiles, cos_t, sin_t, ln1_g[l], ln1_b[l], w_router_group[l],
                    b_router_group[l], w_router_expert[l], b_router_expert[l], w1[l], w3[l], w2[l],
                    ln2_g[l], ln2_b[l])
    return xf.reshape(batch, seq, d)
```

```python
import functools
import math

import jax
import jax.numpy as jnp
from jax import lax
from jax.experimental import pallas as pl
from jax.experimental.pallas import tpu as pltpu

D_MODEL = 2048
MLA_HEADS = 8
MLA_Q_RANK = 512
MLA_KV_RANK = 256
MLA_NOPE_DIM = 128
MLA_ROPE_DIM = 64
MLA_V_DIM = 128
ROPE_THETA = 10000.0
MOBA_HEADS = 8
MOBA_HEAD_DIM = 128
MOBA_BLOCK = 256
MOBA_TOPK = 3
T5_BUCKETS = 32
T5_MAX_DISTANCE = 128
MOE_GROUPS = 4
MOE_EXPERTS_PER_GROUP = 8
MOE_N_EXPERTS = MOE_GROUPS * MOE_EXPERTS_PER_GROUP
MOE_D_FF = 512
LN_EPS = 1e-5
RMS_EPS = 1e-6
MOBA_WIDTH = MOBA_HEADS * MOBA_HEAD_DIM

LANES = 128
SUBLANES = 8
QK_PAD = 256
V_PAD = 256
PART_A = 1024
NEG = -1e30
VMEM_LIMIT = 56 * 1024 * 1024

F32 = jnp.float32
BF16 = jnp.bfloat16


def _cparams(sem, row_dma=False):
    return pltpu.CompilerParams(dimension_semantics=sem, vmem_limit_bytes=VMEM_LIMIT,
                                disable_bounds_checks=row_dma)


def _ln(x):
    mu = jnp.mean(x, axis=-1, keepdims=True)
    xc = x - mu
    var = jnp.mean(xc * xc, axis=-1, keepdims=True)
    return xc * lax.rsqrt(var + LN_EPS)


def _nt_dot(a, b):
    return lax.dot_general(a, b, (((1,), (1,)), ((), ())), preferred_element_type=F32)


def _ada_kernel(ct_ref, w_ref, b_ref, o_ref, *, batch):
    ct = ct_ref[...]
    ca = ct / (1.0 + jnp.exp(-ct))
    w = w_ref[...]
    rows = [jnp.sum(w * ca[:, b:b + 1], axis=0, keepdims=True) for b in range(batch)]
    o_ref[...] = jnp.concatenate(rows, axis=0) + b_ref[...]


def _ada_mod(c, w_ada, b_ada):
    batch, d = c.shape
    n = w_ada.shape[1]
    tn = 1024
    ct = jnp.zeros((d, LANES), F32).at[:, :batch].set(c.T)
    return pl.pallas_call(
        functools.partial(_ada_kernel, batch=batch),
        grid=(n // tn,),
        in_specs=[pl.BlockSpec((d, LANES), lambda j: (0, 0)),
                  pl.BlockSpec((d, tn), lambda j: (0, j)),
                  pl.BlockSpec((1, tn), lambda j: (0, j))],
        out_specs=pl.BlockSpec((batch, tn), lambda j: (0, j)),
        out_shape=jax.ShapeDtypeStruct((batch, n), F32),
        compiler_params=_cparams(("arbitrary",)),
        name="ada_mod",
    )(ct, w_ada, b_ada.reshape(1, n))


def _inproj_kernel(x_ref, mod_ref, w_ref, a_ref, b_ref, h_scr):
    j = pl.program_id(1)

    @pl.when(j == 0)
    def _():
        rows = 256
        for r in range(0, x_ref.shape[0], rows):
            h = _ln(x_ref[r:r + rows, :]) * (1.0 + mod_ref[0, 1:2, :]) + mod_ref[0, 0:1, :]
            h_scr[r:r + rows, :] = h.astype(BF16)
        a_ref[...] = _nt_dot(h_scr[...], w_ref[...])

    @pl.when(j > 0)
    def _():
        b_ref[...] = _nt_dot(h_scr[...], w_ref[...]).astype(BF16)


def _in_proj(xf, mod3, w4, seq):
    t, d = xf.shape
    n = w4.shape[0]
    tm, tn = 1024, PART_A
    per_b = seq // tm
    return pl.pallas_call(
        _inproj_kernel,
        grid=(t // tm, n // tn),
        in_specs=[pl.BlockSpec((tm, d), lambda i, j: (i, 0)),
                  pl.BlockSpec((1, 6, d), lambda i, j: (i // per_b, 0, 0)),
                  pl.BlockSpec((tn, d), lambda i, j: (j, 0))],
        out_specs=[pl.BlockSpec((tm, tn), lambda i, j: (i, 0)),
                   pl.BlockSpec((tm, tn), lambda i, j: (i, jnp.maximum(j - 1, 0)))],
        out_shape=[jax.ShapeDtypeStruct((t, PART_A), F32),
                   jax.ShapeDtypeStruct((t, n - PART_A), BF16)],
        scratch_shapes=[pltpu.VMEM((tm, d), BF16)],
        compiler_params=_cparams(("arbitrary", "arbitrary")),
        name="in_proj",
    )(xf, mod3, w4)


def _qkv_kernel(a_ref, wq_ref, wkv_ref, gq_ref, gkv_ref, cos_ref, sin_ref, q_ref, k_ref, v_ref):
    a = a_ref[...]
    cq = a[:, :MLA_Q_RANK]
    ckv = a[:, MLA_Q_RANK:MLA_Q_RANK + MLA_KV_RANK]
    kr = a[:, 768:896]
    krs = a[:, 896:1024]
    cqn = (cq * lax.rsqrt(jnp.mean(cq * cq, axis=-1, keepdims=True) + RMS_EPS) * gq_ref[...]).astype(BF16)
    ckvn = (ckv * lax.rsqrt(jnp.mean(ckv * ckv, axis=-1, keepdims=True) + RMS_EPS) * gkv_ref[...]).astype(BF16)
    cos = cos_ref[...]
    sin = sin_ref[...]
    krr = (kr * cos + krs * sin).astype(BF16)
    for h in range(MLA_HEADS):
        q0 = h * 3 * LANES
        c0 = h * QK_PAD
        q3 = jnp.dot(cqn, wq_ref[:, q0:q0 + 3 * LANES], preferred_element_type=F32)
        kv = jnp.dot(ckvn, wkv_ref[:, c0:c0 + QK_PAD], preferred_element_type=F32)
        q_ref[:, c0:c0 + LANES] = q3[:, :LANES].astype(BF16)
        q_ref[:, c0 + LANES:c0 + QK_PAD] = (q3[:, LANES:2 * LANES] * cos + q3[:, 2 * LANES:] * sin).astype(BF16)
        k_ref[:, c0:c0 + LANES] = kv[:, :LANES].astype(BF16)
        k_ref[:, c0 + LANES:c0 + QK_PAD] = krr
        v_ref[:, h * LANES:(h + 1) * LANES] = kv[:, LANES:].astype(BF16)


def _qkv(part_a, wq3, wkv, gq, gkv, cos_t, sin_t, seq):
    t = part_a.shape[0]
    tm = 512
    per_b = seq // tm
    hq = MLA_HEADS * QK_PAD
    return pl.pallas_call(
        _qkv_kernel,
        grid=(t // tm,),
        in_specs=[pl.BlockSpec((tm, PART_A), lambda i: (i, 0)),
                  pl.BlockSpec(wq3.shape, lambda i: (0, 0)),
                  pl.BlockSpec(wkv.shape, lambda i: (0, 0)),
                  pl.BlockSpec((1, MLA_Q_RANK), lambda i: (0, 0)),
                  pl.BlockSpec((1, MLA_KV_RANK), lambda i: (0, 0)),
                  pl.BlockSpec((tm, LANES), lambda i: (i % per_b, 0)),
                  pl.BlockSpec((tm, LANES), lambda i: (i % per_b, 0))],
        out_specs=[pl.BlockSpec((tm, hq), lambda i: (i, 0)),
                   pl.BlockSpec((tm, hq), lambda i: (i, 0)),
                   pl.BlockSpec((tm, MLA_HEADS * MLA_V_DIM), lambda i: (i, 0))],
        out_shape=[jax.ShapeDtypeStruct((t, hq), BF16),
                   jax.ShapeDtypeStruct((t, hq), BF16),
                   jax.ShapeDtypeStruct((t, MLA_HEADS * MLA_V_DIM), BF16)],
        compiler_params=_cparams(("arbitrary",)),
        name="qkv",
    )(part_a, wq3, wkv, gq, gkv, cos_t, sin_t)


ATTN_TILE = 512
ATTN_CHAINS = 8


def _attn_kernel(*refs, c, masked):
    nch = ATTN_CHAINS
    scratch = refs[-4 * nch:]
    tile = ATTN_TILE
    i = pl.program_id(2)
    if masked:
        q_ref, kin_ref, vin_ref, bias_ref, o_ref, v_ref, k_ref = refs[:7]
    else:
        q_ref, k_ref, vin_ref, o_ref, v_ref = refs[:5]

    @pl.when(i == 0)
    def _():
        seq = vin_ref.shape[0]
        v_ref[:, :LANES] = vin_ref[...]
        v_ref[:, LANES:] = jnp.ones((seq, V_PAD - LANES), BF16)
        if masked:
            lane = lax.broadcasted_iota(jnp.int32, (seq, LANES), 1)
            own = jnp.right_shift(lax.broadcasted_iota(jnp.int32, (seq, LANES), 0), MOBA_BLOCK.bit_length() - 1)
            k_ref[:, :LANES] = kin_ref[...]
            k_ref[:, LANES:] = jnp.where(lane == own, 1.0, 0.0).astype(BF16)

    chains = [dict(rows=slice(n * tile, (n + 1) * tile), s=scratch[4 * n:4 * n + 2], m=scratch[4 * n + 2],
                   acc=scratch[4 * n + 3]) for n in range(nch)]

    def put_scores(chain, slot, t):
        k = k_ref[pl.ds(pl.multiple_of(t * tile, tile), tile), :]
        chain["s"][slot][...] = _nt_dot(q_ref[chain["rows"], :], k)

    def tile_step(chain, slot, t, kind, prefetch=True):
        if prefetch:
            put_scores(chain, 1 - slot, t + 1)
        m_scr, acc_scr = chain["m"], chain["acc"]
        s = chain["s"][slot][...]
        if masked:
            if kind == "prev":
                s = s + bias_ref[0, 1]
            elif kind == "diag":
                s = s + bias_ref[0, 0]
        elif kind == "diag":
            row = lax.broadcasted_iota(jnp.int32, (tile, tile), 0)
            col = lax.broadcasted_iota(jnp.int32, (tile, tile), 1)
            s = jnp.where(row >= col, s, NEG)
        m_old = m_scr[...]
        m_new = jnp.maximum(m_old, jnp.broadcast_to(jnp.max(s, axis=-1, keepdims=True), m_old.shape))
        alpha = jnp.exp2((m_old - m_new) * c)
        p = jnp.concatenate([jnp.exp2(((s[:, j * LANES:(j + 1) * LANES] - m_new) * c).astype(BF16))
                             for j in range(tile // LANES)], axis=1)
        v = v_ref[pl.ds(pl.multiple_of(t * tile, tile), tile), :]
        pv = jnp.dot(p, v, preferred_element_type=F32)
        acc_scr[...] = jnp.concatenate([alpha, alpha], axis=1) * acc_scr[...] + pv
        m_scr[...] = m_new

    for chain in chains:
        chain["m"][...] = jnp.full(chain["m"].shape, NEG, F32)
        chain["acc"][...] = jnp.zeros(chain["acc"].shape, F32)
        put_scores(chain, 0, 0)

    def far_pair(j, _):
        for slot in range(2):
            for chain in chains:
                tile_step(chain, slot, 2 * j + slot, "far")
        return 0

    first = nch * i
    if masked:
        lax.fori_loop(0, jnp.maximum(first // 2 - 1, 0), far_pair, 0)

        @pl.when(i > 0)
        def _():
            for chain in chains:
                tile_step(chain, 0, first - 2, "far")
            for n, chain in enumerate(chains):
                tile_step(chain, 1, first - 1, "prev" if n == 0 else "far")
    else:
        lax.fori_loop(0, first // 2, far_pair, 0)

    for k in range(nch):
        for n, chain in enumerate(chains):
            if n < k:
                continue
            kind = "diag" if n == k else ("prev" if masked and n == k + 1 else "far")
            tile_step(chain, k % 2, first + k, kind, prefetch=(n != k))

    for chain in chains:
        acc = chain["acc"]
        o_ref[chain["rows"], :] = (acc[:, :LANES] / acc[:, LANES:]).astype(o_ref.dtype)


def _attn_scratch(seq, masked):
    tile = ATTN_TILE
    wide = [pltpu.VMEM((seq, V_PAD), BF16)] + ([pltpu.VMEM((seq, QK_PAD), BF16)] if masked else [])
    per_chain = [pltpu.VMEM((tile, tile), F32), pltpu.VMEM((tile, tile), F32),
                 pltpu.VMEM((tile, LANES), F32), pltpu.VMEM((tile, V_PAD), F32)]
    return wide + per_chain * ATTN_CHAINS


def _mla_attention(q, k, v, batch, seq):
    tile = ATTN_CHAINS * ATTN_TILE
    nq = seq // tile
    c = math.log2(math.e) / math.sqrt(MLA_NOPE_DIM + MLA_ROPE_DIM)
    return pl.pallas_call(
        functools.partial(_attn_kernel, c=c, masked=False),
        grid=(batch, MLA_HEADS, nq),
        in_specs=[pl.BlockSpec((tile, QK_PAD), lambda b, h, i: (b * nq + i, h)),
                  pl.BlockSpec((seq, QK_PAD), lambda b, h, i: (b, h)),
                  pl.BlockSpec((seq, MLA_V_DIM), lambda b, h, i: (b, h))],
        out_specs=pl.BlockSpec((tile, MLA_V_DIM), lambda b, h, i: (b * nq + i, h)),
        out_shape=jax.ShapeDtypeStruct((batch * seq, MLA_HEADS * MLA_V_DIM), BF16),
        scratch_shapes=_attn_scratch(seq, False),
        compiler_params=_cparams(("arbitrary", "arbitrary", "arbitrary")),
        name="mla_attn",
    )(q, k, v)


def _moba_select_kernel(q_ref, k_ref, qa_ref, *, seq, nb):
    kf = k_ref[...].astype(F32)
    km = jnp.sum(kf.reshape(nb, MOBA_BLOCK, MOBA_HEAD_DIM), axis=1) * (1.0 / MOBA_BLOCK)
    km_hi = km.astype(BF16)
    km_lo = (km - km_hi.astype(F32)).astype(BF16)
    q = q_ref[...]
    gate = _nt_dot(km_hi, q) + _nt_dot(km_lo, q)
    shift = MOBA_BLOCK.bit_length() - 1
    blk = lax.broadcasted_iota(jnp.int32, (nb, seq), 0)
    qblk = jnp.right_shift(lax.broadcasted_iota(jnp.int32, (nb, seq), 1), shift)
    g = jnp.where(blk < qblk, gate, NEG)
    visible = blk == qblk
    for _ in range(MOBA_TOPK):
        mx = jnp.max(g, axis=0, keepdims=True)
        first = jnp.min(jnp.where(g == mx, blk, nb), axis=0, keepdims=True)
        pick = (blk == first) & (mx > 0.5 * NEG)
        visible = visible | pick
        g = jnp.where(pick, NEG, g)
    mask_t = jnp.concatenate([jnp.where(visible, 0.0, NEG), jnp.zeros((LANES - nb, seq), F32)], axis=0)
    qa_ref[:, :MOBA_HEAD_DIM] = q
    qa_ref[:, MOBA_HEAD_DIM:] = mask_t.T.astype(BF16)


def _moba_select(qkv_mo, batch, seq):
    nb = seq // MOBA_BLOCK
    assert MOBA_HEAD_DIM == LANES and nb <= QK_PAD - MOBA_HEAD_DIM
    return pl.pallas_call(
        functools.partial(_moba_select_kernel, seq=seq, nb=nb),
        grid=(batch, MOBA_HEADS),
        in_specs=[pl.BlockSpec((seq, MOBA_HEAD_DIM), lambda b, h: (b, h)),
                  pl.BlockSpec((seq, MOBA_HEAD_DIM), lambda b, h: (b, MOBA_HEADS + h))],
        out_specs=pl.BlockSpec((seq, QK_PAD), lambda b, h: (b, h)),
        out_shape=jax.ShapeDtypeStruct((batch * seq, MOBA_HEADS * QK_PAD), BF16),
        compiler_params=_cparams(("arbitrary", "arbitrary")),
        name="moba_select",
    )(qkv_mo, qkv_mo)


def _t5_kernel(tab_ref, o_ref, *, inv_scale):
    h = pl.program_id(0)
    r = lax.broadcasted_iota(jnp.int32, (LANES, LANES), 0)
    c = lax.broadcasted_iota(jnp.int32, (LANES, LANES), 1)
    max_exact = T5_BUCKETS // 2
    far = tab_ref[T5_BUCKETS - 1, h]

    def block(offset):
        rel = offset + r - c
        n = jnp.maximum(rel, 0)
        nf = jnp.maximum(n, 1).astype(F32)
        large = max_exact + (jnp.log(nf / max_exact) / math.log(T5_MAX_DISTANCE / max_exact)
                             * (T5_BUCKETS - max_exact)).astype(jnp.int32)
        large = jnp.minimum(large, T5_BUCKETS - 1)
        bucket = jnp.where(n < max_exact, n, large)
        bias = jnp.zeros((LANES, LANES), F32)
        for j in range(T5_BUCKETS):
            bias = jnp.where(bucket == j, tab_ref[j, h], bias)
        return jnp.where(rel >= 0, (bias - far) * inv_scale, NEG)

    near = {0: block(0), 1: block(LANES)}
    nblk = ATTN_TILE // LANES
    for d in range(2):
        for i in range(nblk):
            for j in range(nblk):
                k = d * nblk + i - j
                if k < 0:
                    val = jnp.full((LANES, LANES), NEG, F32)
                else:
                    val = near.get(k, jnp.zeros((LANES, LANES), F32))
                o_ref[0, d, i * LANES:(i + 1) * LANES, j * LANES:(j + 1) * LANES] = val


def _t5_tiles(t5_table):
    assert LANES >= T5_MAX_DISTANCE
    tile = ATTN_TILE
    return pl.pallas_call(
        functools.partial(_t5_kernel, inv_scale=math.sqrt(MOBA_HEAD_DIM)),
        grid=(MOBA_HEADS,),
        in_specs=[pl.BlockSpec(memory_space=pltpu.SMEM)],
        out_specs=pl.BlockSpec((1, 2, tile, tile), lambda h: (h, 0, 0, 0)),
        out_shape=jax.ShapeDtypeStruct((MOBA_HEADS, 2, tile, tile), F32),
        compiler_params=_cparams(("arbitrary",)),
        name="t5_tiles",
    )(t5_table)


def _moba_attention(q_wide, qkv_mo, bias, batch, seq):
    tile = ATTN_CHAINS * ATTN_TILE
    nq = seq // tile
    c = math.log2(math.e) / math.sqrt(MOBA_HEAD_DIM)
    dh = MOBA_HEAD_DIM
    return pl.pallas_call(
        functools.partial(_attn_kernel, c=c, masked=True),
        grid=(batch, MOBA_HEADS, nq),
        in_specs=[pl.BlockSpec((tile, QK_PAD), lambda b, h, i: (b * nq + i, h)),
                  pl.BlockSpec((seq, dh), lambda b, h, i: (b, MOBA_HEADS + h)),
                  pl.BlockSpec((seq, dh), lambda b, h, i: (b, 2 * MOBA_HEADS + h)),
                  pl.BlockSpec((1, 2, ATTN_TILE, ATTN_TILE), lambda b, h, i: (h, 0, 0, 0))],
        out_specs=pl.BlockSpec((tile, dh), lambda b, h, i: (b * nq + i, h)),
        out_shape=jax.ShapeDtypeStruct((batch * seq, MOBA_WIDTH), BF16),
        scratch_shapes=_attn_scratch(seq, True),
        compiler_params=_cparams(("arbitrary", "arbitrary", "arbitrary")),
        name="moba_attn",
    )(q_wide, qkv_mo, qkv_mo, bias)


def _outproj_kernel(oa_ref, ob_ref, wa_ref, wb_ref, x_ref, mod_ref, g_ref, b_ref, wr_ref,
                    x1_ref, h2_ref, lg_ref, y_a, y_b, *, alpha, n_tiles):
    i = pl.program_id(0)
    tm, d = y_a.shape
    chunks = 4
    cn, cr = d // chunks, tm // chunks

    def matmul_into(y_ref, c):
        cols = slice(c * cn, (c + 1) * cn)
        y_ref[:, cols] = (jnp.dot(oa_ref[...], wa_ref[:, cols], preferred_element_type=F32)
                          + jnp.dot(ob_ref[...], wb_ref[:, cols], preferred_element_type=F32))

    def epilogue(y_ref, c):
        rows = slice(c * cr, (c + 1) * cr)
        z = alpha * x_ref[rows, :] + mod_ref[0, 2:3, :] * y_ref[rows, :]
        x1 = _ln(z) * g_ref[...] + b_ref[...]
        x1_ref[rows, :] = x1
        h2 = _ln(x1) * (1.0 + mod_ref[0, 4:5, :]) + mod_ref[0, 3:4, :]
        h2_ref[rows, :] = h2
        h_hi = h2.astype(BF16)
        h_lo = (h2 - h_hi.astype(F32)).astype(BF16)
        zz = (jnp.dot(h_hi, wr_ref[...], preferred_element_type=F32)
              + jnp.dot(h_lo, wr_ref[...], preferred_element_type=F32))
        lg_ref[rows, :] = zz + pltpu.roll(zz, LANES // 2, 1)

    def step(y_new, y_old):
        for c in range(chunks):
            if y_new is not None:
                matmul_into(y_new, c)
            if y_old is not None:
                epilogue(y_old, c)

    inner = (i > 0) & (i < n_tiles)
    pl.when(i == 0)(lambda: step(y_a, None))
    pl.when(inner & (i % 2 == 0))(lambda: step(y_a, y_b))
    pl.when(inner & (i % 2 == 1))(lambda: step(y_b, y_a))
    pl.when(i == n_tiles)(lambda: step(None, y_b if n_tiles % 2 == 0 else y_a))


def _out_proj(o_mla, o_moba, wo, xf, mod3, ln_g, ln_b, wr, seq, alpha):
    t, d = xf.shape
    tm = 512
    n = t // tm
    per_b = seq // tm
    ka, kb = o_mla.shape[1], o_moba.shape[1]
    assert ka == kb and wo.shape[0] == ka + kb
    once = pl.Buffered(1)

    def cur(i):
        return (jnp.minimum(i, n - 1), 0)

    def lag(i):
        return (jnp.maximum(i - 1, 0), 0)

    return pl.pallas_call(
        functools.partial(_outproj_kernel, alpha=alpha, n_tiles=n),
        grid=(n + 1,),
        in_specs=[pl.BlockSpec((tm, ka), cur),
                  pl.BlockSpec((tm, kb), cur),
                  pl.BlockSpec((ka, d), lambda i: (0, 0), pipeline_mode=once),
                  pl.BlockSpec((kb, d), lambda i: (1, 0), pipeline_mode=once),
                  pl.BlockSpec((tm, d), lag),
                  pl.BlockSpec((1, 6, d), lambda i: (jnp.maximum(i - 1, 0) // per_b, 0, 0)),
                  pl.BlockSpec((1, d), lambda i: (0, 0)),
                  pl.BlockSpec((1, d), lambda i: (0, 0)),
                  pl.BlockSpec((d, LANES), lambda i: (0, 0), pipeline_mode=once)],
        out_specs=[pl.BlockSpec((tm, d), lag),
                   pl.BlockSpec((tm, d), lag),
                   pl.BlockSpec((tm, LANES), lag)],
        out_shape=[jax.ShapeDtypeStruct((t, d), F32),
                   jax.ShapeDtypeStruct((t, d), F32),
                   jax.ShapeDtypeStruct((t, LANES), F32)],
        scratch_shapes=[pltpu.VMEM((tm, d), F32), pltpu.VMEM((tm, d), F32)],
        compiler_params=_cparams(("arbitrary",)),
        name="out_proj",
    )(o_mla, o_moba, wo, wo, xf, mod3, ln_g, ln_b, wr)


def _route_kernel(lg_ref, br_ref, info_ref, cnt_ref, run_scr, *, tm):
    i = pl.program_id(0)

    @pl.when(i == 0)
    def _():
        run_scr[...] = jnp.zeros_like(run_scr)

    lg = lg_ref[...] + br_ref[...]
    lane = lax.broadcasted_iota(jnp.int32, (tm, LANES), 1)
    e_lo, e_hi = MOE_GROUPS, MOE_GROUPS + MOE_N_EXPERTS
    is_g = lane < e_lo
    gl = jnp.where(is_g, lg, NEG)
    gmax = jnp.max(gl, axis=-1, keepdims=True)
    gidx = jnp.min(jnp.where(gl == gmax, lane, LANES), axis=-1, keepdims=True)
    g_p = 1.0 / jnp.sum(jnp.where(is_g, jnp.exp(gl - gmax), 0.0), axis=-1, keepdims=True)
    grp_of_lane = jnp.right_shift(lane - e_lo, MOE_EXPERTS_PER_GROUP.bit_length() - 1)
    in_grp = (lane >= e_lo) & (lane < e_hi) & (grp_of_lane == gidx)
    el = jnp.where(in_grp, lg, NEG)
    m1 = jnp.max(el, axis=-1, keepdims=True)
    l1 = jnp.min(jnp.where(el == m1, lane, LANES), axis=-1, keepdims=True)
    el2 = jnp.where(lane == l1, NEG, el)
    m2 = jnp.max(el2, axis=-1, keepdims=True)
    l2 = jnp.min(jnp.where(el2 == m2, lane, LANES), axis=-1, keepdims=True)
    zsum = jnp.sum(jnp.where(in_grp, jnp.exp(el - m1), 0.0), axis=-1, keepdims=True)
    p1 = 1.0 / zsum
    p2 = jnp.exp(m2 - m1) / zsum
    wa = g_p * (p1 / (p1 + p2))
    wb = g_p * (p2 / (p1 + p2))
    hot_a = lane == l1
    hot_b = lane == l2
    onehot = jnp.where(hot_a | hot_b, 1.0, 0.0)
    r = lax.broadcasted_iota(jnp.int32, (tm, tm), 0)
    c = lax.broadcasted_iota(jnp.int32, (tm, tm), 1)
    lower = jnp.where(c < r, 1.0, 0.0).astype(BF16)
    before = jnp.dot(lower, onehot.astype(BF16), preferred_element_type=F32) + run_scr[...]
    rank_a = jnp.sum(jnp.where(hot_a, before, 0.0), axis=-1, keepdims=True)
    rank_b = jnp.sum(jnp.where(hot_b, before, 0.0), axis=-1, keepdims=True)
    run_scr[...] += jnp.sum(onehot, axis=0, keepdims=True)
    info = jnp.zeros((tm, LANES), F32)
    for k, val in enumerate([(l1 - e_lo).astype(F32), (l2 - e_lo).astype(F32), wa, wb, rank_a, rank_b]):
        info = jnp.where(lane == k, val, info)
    info_ref[...] = info
    cnt_ref[...] = run_scr[...]


def _route(logits, br):
    t = logits.shape[0]
    tm = 512
    return pl.pallas_call(
        functools.partial(_route_kernel, tm=tm),
        grid=(t // tm,),
        in_specs=[pl.BlockSpec((tm, LANES), lambda i: (i, 0)),
                  pl.BlockSpec((1, LANES), lambda i: (0, 0))],
        out_specs=[pl.BlockSpec((tm, LANES), lambda i: (i, 0)),
                   pl.BlockSpec((1, LANES), lambda i: (0, 0))],
        out_shape=[jax.ShapeDtypeStruct((t, LANES), F32),
                   jax.ShapeDtypeStruct((1, LANES), F32)],
        scratch_shapes=[pltpu.VMEM((1, LANES), F32)],
        compiler_params=_cparams(("arbitrary",)),
        name="route",
    )(logits, br)


def _pos_kernel(info_ref, start_ref, pos_ref):
    info = info_ref[...]
    tm = info.shape[0]
    lane = lax.broadcasted_iota(jnp.int32, (tm, LANES), 1)
    start = start_ref[...]
    cols = []
    for k in range(2):
        e = jnp.sum(jnp.where(lane == k, info, 0.0), axis=-1, keepdims=True).astype(jnp.int32)
        rank = jnp.sum(jnp.where(lane == 4 + k, info, 0.0), axis=-1, keepdims=True)
        base = jnp.sum(jnp.where(lane == e + MOE_GROUPS, start, 0.0), axis=-1, keepdims=True)
        cols.append(base + rank)
    wide = jnp.where(lane == 0, cols[0], jnp.where(lane == 1, cols[1], 0.0))
    pos_ref[...] = wide.T[:pos_ref.shape[0], :].astype(jnp.int32)


def _positions(info, start_lanes):
    t = info.shape[0]
    tm = 1024
    return pl.pallas_call(
        _pos_kernel,
        grid=(t // tm,),
        in_specs=[pl.BlockSpec((tm, LANES), lambda i: (i, 0)),
                  pl.BlockSpec((1, LANES), lambda i: (0, 0))],
        out_specs=pl.BlockSpec((8, tm), lambda i: (0, i)),
        out_shape=jax.ShapeDtypeStruct((8, t), jnp.int32),
        compiler_params=_cparams(("arbitrary",)),
        name="positions",
    )(info, start_lanes)


def _row_copy(src_ref, src_row, dst_ref, dst_row, sem):
    return pltpu.make_async_copy(src_ref.at[pl.ds(src_row, 1)], dst_ref.at[pl.ds(dst_row, 1)], sem)


def _dispatch_kernel(pa_ref, pb_ref, pad0_ref, padn_ref, used_ref, h_ref, xs_ref, ztile, sem, zsem,
                     *, tm, tr, n_tiles):
    i = pl.program_id(0)
    base = i * tm

    def zero_fill(act):
        def whole_tile(j, _):
            act(pltpu.make_async_copy(ztile, xs_ref.at[pl.ds(pl.multiple_of(j * tr, tr), tr)], zsem))
            return 0

        lax.fori_loop(used_ref[0], n_tiles, whole_tile, 0)

        def expert_pad(e, _):
            n, start = padn_ref[e], pad0_ref[e]
            head = jnp.minimum(jnp.bitwise_and(-start, SUBLANES - 1), n)
            for k in range(SUBLANES - 1):
                pl.when(k < head)(functools.partial(act, _row_copy(ztile, 0, xs_ref, start + k, zsem)))
            off = start + head
            groups = jnp.right_shift(n - head, SUBLANES.bit_length() - 1)
            for bit in reversed(range((tr // SUBLANES - 1).bit_length())):
                size = SUBLANES << bit
                take = jnp.bitwise_and(jnp.right_shift(groups, bit), 1)
                dst = xs_ref.at[pl.ds(pl.multiple_of(off, SUBLANES), size)]
                pl.when(take == 1)(functools.partial(act, pltpu.make_async_copy(ztile.at[pl.ds(0, size)], dst, zsem)))
                off = off + take * size
            return 0

        lax.fori_loop(0, MOE_N_EXPERTS, expert_pad, 0)

    @pl.when(i == 0)
    def _():
        ztile[...] = jnp.zeros(ztile.shape, ztile.dtype)
        zero_fill(lambda cp: cp.start())

    def issue(t, _):
        _row_copy(h_ref, t, xs_ref, pa_ref[base + t], sem).start()
        _row_copy(h_ref, t, xs_ref, pb_ref[base + t], sem).start(priority=1)
        return 0

    lax.fori_loop(0, tm, issue, 0, unroll=8)
    for _ in range(2):
        pltpu.make_async_copy(h_ref, xs_ref.at[pl.ds(0, tm)], sem).wait()

    @pl.when(i == pl.num_programs(0) - 1)
    def _():
        zero_fill(lambda cp: cp.wait())


def _dispatch(pos_a, pos_b, pad0, padn, used, h2, tr, n_tiles):
    t, d = h2.shape
    tm = 256
    grid_spec = pltpu.PrefetchScalarGridSpec(
        num_scalar_prefetch=5,
        grid=(t // tm,),
        in_specs=[pl.BlockSpec((tm, d), lambda i, *_: (i, 0))],
        out_specs=pl.BlockSpec(memory_space=pl.ANY),
        scratch_shapes=[pltpu.VMEM((tr, d), F32), pltpu.SemaphoreType.DMA(()), pltpu.SemaphoreType.DMA(())],
    )
    return pl.pallas_call(
        functools.partial(_dispatch_kernel, tm=tm, tr=tr, n_tiles=n_tiles),
        grid_spec=grid_spec,
        out_shape=jax.ShapeDtypeStruct((n_tiles * tr, d), F32),
        compiler_params=_cparams(("arbitrary",), row_dma=True),
        name="dispatch",
    )(pos_a, pos_b, pad0, padn, used, h2)


def _experts_kernel(texp_ref, tidx_ref, nexte_ref, used_ref, x_ref, w1_hbm, w3_hbm, w2_hbm, o_ref,
                    w1_f32, w3_f32, w2_f32, w1_scr, w3_scr, w2_scr, sems):
    i = pl.program_id(0)
    prev = texp_ref[jnp.maximum(i - 1, 0)]

    def weight_copies(e):
        return [pltpu.make_async_copy(w1_hbm.at[e], w1_f32, sems.at[0]),
                pltpu.make_async_copy(w3_hbm.at[e], w3_f32, sems.at[1]),
                pltpu.make_async_copy(w2_hbm.at[e], w2_f32, sems.at[2])]

    @pl.when(i == 0)
    def _():
        for cp in weight_copies(texp_ref[0]):
            cp.start()

    @pl.when((i == 0) | (texp_ref[i] != prev))
    def _():
        for cp in weight_copies(texp_ref[i]):
            cp.wait()
        w1_scr[...] = w1_f32[...].astype(BF16)
        w3_scr[...] = w3_f32[...].astype(BF16)
        w2_scr[...] = w2_f32[...].astype(BF16)

        @pl.when(nexte_ref[i] >= 0)
        def _():
            for cp in weight_copies(nexte_ref[i]):
                cp.start(priority=1)

    @pl.when(i < used_ref[0])
    def _():
        x = x_ref[...].astype(BF16)
        a = jnp.dot(x, w1_scr[...], preferred_element_type=F32)
        b = jnp.dot(x, w3_scr[...], preferred_element_type=F32)
        hid = (a / (1.0 + jnp.exp(-a))) * b
        o_ref[...] = jnp.dot(hid.astype(BF16), w2_scr[...], preferred_element_type=F32)

    @pl.when(i >= used_ref[0])
    def _():
        o_ref[...] = jnp.zeros_like(o_ref)


def _experts(texp, tidx, nexte, used, xs, w1, w3, w2, tr):
    rows, d = xs.shape
    nt = rows // tr
    f = w1.shape[2]
    grid_spec = pltpu.PrefetchScalarGridSpec(
        num_scalar_prefetch=4,
        grid=(nt,),
        in_specs=[pl.BlockSpec((tr, d), lambda i, te, ti, ne, u: (ti[i], 0)),
                  pl.BlockSpec(memory_space=pl.ANY),
                  pl.BlockSpec(memory_space=pl.ANY),
                  pl.BlockSpec(memory_space=pl.ANY)],
        out_specs=pl.BlockSpec((tr, d), lambda i, te, ti, ne, u: (i, 0)),
        scratch_shapes=[pltpu.VMEM((d, f), F32), pltpu.VMEM((d, f), F32), pltpu.VMEM((f, d), F32),
                        pltpu.VMEM((d, f), BF16), pltpu.VMEM((d, f), BF16), pltpu.VMEM((f, d), BF16),
                        pltpu.SemaphoreType.DMA((3,))],
    )
    return pl.pallas_call(
        _experts_kernel,
        grid_spec=grid_spec,
        out_shape=jax.ShapeDtypeStruct((rows, d), F32),
        compiler_params=_cparams(("arbitrary",)),
        name="experts",
    )(texp, tidx, nexte, used, xs, w1, w3, w2)


def _combine_kernel(pa_ref, pb_ref, ys_ref, x1_ref, info_ref, mod_ref, g_ref, b_ref, o_ref,
                    buf_a, buf_b, sems, *, tm, alpha):
    i = pl.program_id(0)

    def gather(tile, slot):
        base = tile * tm

        def issue(t, _):
            _row_copy(ys_ref, pa_ref[base + t], buf_a.at[slot], t, sems.at[slot]).start()
            _row_copy(ys_ref, pb_ref[base + t], buf_b.at[slot], t, sems.at[slot]).start(priority=1)
            return 0

        lax.fori_loop(0, tm, issue, 0, unroll=8)

    @pl.when(i == 0)
    def _():
        gather(0, 0)

    @pl.when(i + 1 < pl.num_programs(0))
    def _():
        gather(i + 1, (i + 1) % 2)

    slot = i % 2
    for buf in (buf_a, buf_b):
        pltpu.make_async_copy(ys_ref.at[pl.ds(0, tm)], buf.at[slot], sems.at[slot]).wait()
    info = info_ref[...]
    y = info[:, 2:3] * buf_a[slot] + info[:, 3:4] * buf_b[slot]
    z = alpha * x1_ref[...] + mod_ref[0, 5:6, :] * y
    o_ref[...] = _ln(z) * g_ref[...] + b_ref[...]


def _combine(pos_a, pos_b, ys, x1, info, mod3, ln_g, ln_b, seq, alpha):
    t, d = x1.shape
    tm = 256
    per_b = seq // tm
    grid_spec = pltpu.PrefetchScalarGridSpec(
        num_scalar_prefetch=2,
        grid=(t // tm,),
        in_specs=[pl.BlockSpec(memory_space=pl.ANY),
                  pl.BlockSpec((tm, d), lambda i, pa, pb: (i, 0)),
                  pl.BlockSpec((tm, LANES), lambda i, pa, pb: (i, 0)),
                  pl.BlockSpec((1, 6, d), lambda i, pa, pb: (i // per_b, 0, 0)),
                  pl.BlockSpec((1, d), lambda i, pa, pb: (0, 0)),
                  pl.BlockSpec((1, d), lambda i, pa, pb: (0, 0))],
        out_specs=pl.BlockSpec((tm, d), lambda i, pa, pb: (i, 0)),
        scratch_shapes=[pltpu.VMEM((2, tm, d), F32), pltpu.VMEM((2, tm, d), F32), pltpu.SemaphoreType.DMA((2,))],
    )
    return pl.pallas_call(
        functools.partial(_combine_kernel, tm=tm, alpha=alpha),
        grid_spec=grid_spec,
        out_shape=jax.ShapeDtypeStruct((t, d), F32),
        compiler_params=_cparams(("arbitrary",), row_dma=True),
        name="combine",
    )(pos_a, pos_b, ys, x1, info, mod3, ln_g, ln_b)


def _prep_w_in_kernel(w_ref, o_ref):
    r0 = MLA_Q_RANK + MLA_KV_RANK
    r1 = r0 + MLA_ROPE_DIM
    half = MLA_ROPE_DIM // 2
    z = jnp.zeros((LANES - MLA_ROPE_DIM, o_ref.shape[1]), BF16)
    o_ref[:r1, :] = w_ref[:r1, :].astype(BF16)
    o_ref[r1:r0 + LANES, :] = z
    o_ref[r0 + LANES:r0 + LANES + half, :] = w_ref[r0 + half:r1, :].astype(BF16)
    o_ref[r0 + LANES + half:r0 + LANES + MLA_ROPE_DIM, :] = w_ref[r0:r0 + half, :].astype(BF16)
    o_ref[r0 + LANES + MLA_ROPE_DIM:PART_A, :] = z
    o_ref[PART_A:, :] = w_ref[r1:, :].astype(BF16)


def _prep_w_in(w_t):
    n, k = w_t.shape
    tk = 512
    n_out = PART_A + 3 * MOBA_WIDTH
    assert n == MLA_Q_RANK + MLA_KV_RANK + MLA_ROPE_DIM + 3 * MOBA_WIDTH
    return pl.pallas_call(
        _prep_w_in_kernel,
        grid=(k // tk,),
        in_specs=[pl.BlockSpec((n, tk), lambda i: (0, i))],
        out_specs=pl.BlockSpec((n_out, tk), lambda i: (0, i)),
        out_shape=jax.ShapeDtypeStruct((n_out, k), BF16),
        compiler_params=_cparams(("arbitrary",)),
        name="prep_w_in",
    )(w_t)


def _prep_w_uq(w):
    r = w.shape[0]
    w = w.reshape(r, MLA_HEADS, MLA_NOPE_DIM + MLA_ROPE_DIM)
    half = MLA_ROPE_DIM // 2
    nope = w[:, :, :MLA_NOPE_DIM]
    x1 = w[:, :, MLA_NOPE_DIM:MLA_NOPE_DIM + half]
    x2 = w[:, :, MLA_NOPE_DIM + half:]
    z = jnp.zeros((r, MLA_HEADS, LANES - MLA_ROPE_DIM), w.dtype)
    return jnp.concatenate([nope, x1, x2, z, x2, x1, z], axis=2).reshape(r, MLA_HEADS * 3 * LANES).astype(BF16)


def _rope_lanes(seq):
    inv = 1.0 / (ROPE_THETA ** (jnp.arange(0, MLA_ROPE_DIM, 2, dtype=F32) / MLA_ROPE_DIM))
    ang = jnp.arange(seq, dtype=F32)[:, None] * inv[None, :]
    cos, sin = jnp.cos(ang), jnp.sin(ang)
    z = jnp.zeros((seq, LANES - MLA_ROPE_DIM), F32)
    return jnp.concatenate([cos, cos, z], axis=1), jnp.concatenate([-sin, sin, z], axis=1)


def _prep_router(w_rg, b_rg, w_re, b_re):
    d = w_rg.shape[0]
    w = jnp.concatenate([w_rg, w_re], axis=1)
    n = w.shape[1]
    hi = w.astype(BF16)
    lo = (w - hi.astype(F32)).astype(BF16)
    z = jnp.zeros((d, LANES // 2 - n), BF16)
    wr = jnp.concatenate([hi, z, lo, z], axis=1)
    br = jnp.zeros((1, LANES), F32).at[0, :n].set(jnp.concatenate([b_rg, b_re]))
    return wr, br


def _layer(xf, mod3, batch, seq, depth_alpha, w_in, q_norm_g, w_uq, kv_norm_g, w_ukv, w_out, bias_tiles,
           cos_t, sin_t, ln1_g, ln1_b, w_rg, b_rg, w_re, b_re, w1, w3, w2, ln2_g, ln2_b):
    t, d = xf.shape
    part_a, qkv_mo = _in_proj(xf, mod3, _prep_w_in(w_in.T), seq)
    q, k, v = _qkv(part_a, _prep_w_uq(w_uq), w_ukv.astype(BF16), q_norm_g.reshape(1, -1),
                   kv_norm_g.reshape(1, -1), cos_t, sin_t, seq)
    o_mla = _mla_attention(q, k, v, batch, seq)
    q_wide = _moba_select(qkv_mo, batch, seq)
    o_moba = _moba_attention(q_wide, qkv_mo, bias_tiles, batch, seq)
    wo = w_out.astype(BF16)
    wr, br = _prep_router(w_rg, b_rg, w_re, b_re)
    x1, h2, logits = _out_proj(o_mla, o_moba, wo, xf, mod3, ln1_g.reshape(1, d),
                               ln1_b.reshape(1, d), wr, seq, depth_alpha)
    info, counts = _route(logits, br)
    tr = 256
    nt = (2 * t) // tr + MOE_N_EXPERTS
    cnt = counts[0, MOE_GROUPS:MOE_GROUPS + MOE_N_EXPERTS].astype(jnp.int32)
    ntile = (cnt + tr - 1) // tr
    tile_end = jnp.cumsum(ntile)
    tile_start = tile_end - ntile
    used = tile_end[-1]
    start_lanes = jnp.zeros((1, LANES), F32).at[0, MOE_GROUPS:MOE_GROUPS + MOE_N_EXPERTS].set(
        (tile_start * tr).astype(F32))
    pos = _positions(info, start_lanes)
    pos_a, pos_b = pos[0], pos[1]
    tidx = jnp.minimum(jnp.arange(nt, dtype=jnp.int32), used - 1)
    texp = jnp.sum(tidx[:, None] >= tile_end[None, :], axis=1).astype(jnp.int32)
    eids = jnp.arange(MOE_N_EXPERTS, dtype=jnp.int32)
    later = (eids[None, :] > eids[:, None]) & (ntile[None, :] > 0)
    next_nonempty = jnp.where(later.any(axis=1), jnp.argmax(later, axis=1), -1).astype(jnp.int32)
    nexte = jnp.sum(jnp.where(texp[:, None] == eids[None, :], next_nonempty[None, :], 0), axis=1).astype(jnp.int32)
    used1 = used.reshape(1).astype(jnp.int32)
    pad0 = (tile_start * tr + cnt).astype(jnp.int32)
    padn = (ntile * tr - cnt).astype(jnp.int32)
    xs = _dispatch(pos_a, pos_b, pad0, padn, used1, h2, tr, nt)
    ys = _experts(texp, tidx, nexte, used1, xs, w1, w3, w2, tr)
    return _combine(pos_a, pos_b, ys, x1, info, mod3, ln2_g.reshape(1, d), ln2_b.reshape(1, d), seq, depth_alpha)


def kernel(x, c, w_ada, b_ada, w_in, q_norm_g, w_uq, kv_norm_g, w_ukv, w_out, t5_table, ln1_g, ln1_b,
           w_router_group, b_router_group, w_router_expert, b_router_expert, w1, w3, w2, ln2_g, ln2_b):
    batch, seq, d = x.shape
    depth = w_ada.shape[0]
    alpha = (2.0 * depth) ** 0.25
    cos_t, sin_t = _rope_lanes(seq)
    bias_tiles = _t5_tiles(t5_table)
    xf = x.reshape(batch * seq, d)
    for l in range(depth):
        mod3 = _ada_mod(c, w_ada[l], b_ada[l]).reshape(batch, 6, d)
        xf = _layer(xf, mod3, batch, seq, alpha, w_in[l], q_norm_g[l], w_uq[l], kv_norm_g[l], w_ukv[l],
                    w_out[l], bias_tiles, cos_t, sin_t, ln1_g[l], ln1_b[l], w_router_group[l],
                    b_router_group[l], w_router_expert[l], b_router_expert[l], w1[l], w3[l], w2[l],
                    ln2_g[l], ln2_b[l])
    return xf.reshape(batch, seq, d)
```

```python
import functools
import math

import jax
import jax.numpy as jnp
from jax import lax
from jax.experimental import pallas as pl
from jax.experimental.pallas import tpu as pltpu

D_MODEL = 2048
MLA_HEADS = 8
MLA_Q_RANK = 512
MLA_KV_RANK = 256
MLA_NOPE_DIM = 128
MLA_ROPE_DIM = 64
MLA_V_DIM = 128
ROPE_THETA = 10000.0
MOBA_HEADS = 8
MOBA_HEAD_DIM = 128
MOBA_BLOCK = 256
MOBA_TOPK = 3
T5_BUCKETS = 32
T5_MAX_DISTANCE = 128
MOE_GROUPS = 4
MOE_EXPERTS_PER_GROUP = 8
MOE_N_EXPERTS = MOE_GROUPS * MOE_EXPERTS_PER_GROUP
MOE_D_FF = 512
LN_EPS = 1e-5
RMS_EPS = 1e-6
MOBA_WIDTH = MOBA_HEADS * MOBA_HEAD_DIM

LANES = 128
SUBLANES = 8
QK_PAD = 256
V_PAD = 256
PART_A = 1024
NEG = -1e30
VMEM_LIMIT = 56 * 1024 * 1024

F32 = jnp.float32
BF16 = jnp.bfloat16


def _cparams(sem, row_dma=False):
    return pltpu.CompilerParams(dimension_semantics=sem, vmem_limit_bytes=VMEM_LIMIT,
                                disable_bounds_checks=row_dma)


def _ln(x):
    mu = jnp.mean(x, axis=-1, keepdims=True)
    xc = x - mu
    var = jnp.mean(xc * xc, axis=-1, keepdims=True)
    return xc * lax.rsqrt(var + LN_EPS)


def _nt_dot(a, b):
    return lax.dot_general(a, b, (((1,), (1,)), ((), ())), preferred_element_type=F32)


def _ada_kernel(ct_ref, w_ref, b_ref, o_ref, *, batch):
    ct = ct_ref[...]
    ca = ct / (1.0 + jnp.exp(-ct))
    w = w_ref[...]
    rows = [jnp.sum(w * ca[:, b:b + 1], axis=0, keepdims=True) for b in range(batch)]
    o_ref[...] = jnp.concatenate(rows, axis=0) + b_ref[...]


def _ada_mod(c, w_ada, b_ada):
    batch, d = c.shape
    n = w_ada.shape[1]
    tn = 1024
    ct = jnp.zeros((d, LANES), F32).at[:, :batch].set(c.T)
    return pl.pallas_call(
        functools.partial(_ada_kernel, batch=batch),
        grid=(n // tn,),
        in_specs=[pl.BlockSpec((d, LANES), lambda j: (0, 0)),
                  pl.BlockSpec((d, tn), lambda j: (0, j)),
                  pl.BlockSpec((1, tn), lambda j: (0, j))],
        out_specs=pl.BlockSpec((batch, tn), lambda j: (0, j)),
        out_shape=jax.ShapeDtypeStruct((batch, n), F32),
        compiler_params=_cparams(("arbitrary",)),
        name="ada_mod",
    )(ct, w_ada, b_ada.reshape(1, n))


def _inproj_kernel(x_ref, mod_ref, w_ref, a_ref, b_ref, h_scr):
    j = pl.program_id(1)

    @pl.when(j == 0)
    def _():
        rows = 256
        for r in range(0, x_ref.shape[0], rows):
            h = _ln(x_ref[r:r + rows, :]) * (1.0 + mod_ref[0, 1:2, :]) + mod_ref[0, 0:1, :]
            h_scr[r:r + rows, :] = h.astype(BF16)
        a_ref[...] = _nt_dot(h_scr[...], w_ref[...])

    @pl.when(j > 0)
    def _():
        b_ref[...] = _nt_dot(h_scr[...], w_ref[...]).astype(BF16)


def _in_proj(xf, mod3, w4, seq):
    t, d = xf.shape
    n = w4.shape[0]
    tm, tn = 1024, PART_A
    per_b = seq // tm
    return pl.pallas_call(
        _inproj_kernel,
        grid=(t // tm, n // tn),
        in_specs=[pl.BlockSpec((tm, d), lambda i, j: (i, 0)),
                  pl.BlockSpec((1, 6, d), lambda i, j: (i // per_b, 0, 0)),
                  pl.BlockSpec((tn, d), lambda i, j: (j, 0))],
        out_specs=[pl.BlockSpec((tm, tn), lambda i, j: (i, 0)),
                   pl.BlockSpec((tm, tn), lambda i, j: (i, jnp.maximum(j - 1, 0)))],
        out_shape=[jax.ShapeDtypeStruct((t, PART_A), F32),
                   jax.ShapeDtypeStruct((t, n - PART_A), BF16)],
        scratch_shapes=[pltpu.VMEM((tm, d), BF16)],
        compiler_params=_cparams(("arbitrary", "arbitrary")),
        name="in_proj",
    )(xf, mod3, w4)


def _qkv_kernel(a_ref, wq_ref, wkv_ref, gq_ref, gkv_ref, cos_ref, sin_ref, q_ref, k_ref, v_ref):
    a = a_ref[...]
    cq = a[:, :MLA_Q_RANK]
    ckv = a[:, MLA_Q_RANK:MLA_Q_RANK + MLA_KV_RANK]
    kr = a[:, 768:896]
    krs = a[:, 896:1024]
    cqn = (cq * lax.rsqrt(jnp.mean(cq * cq, axis=-1, keepdims=True) + RMS_EPS) * gq_ref[...]).astype(BF16)
    ckvn = (ckv * lax.rsqrt(jnp.mean(ckv * ckv, axis=-1, keepdims=True) + RMS_EPS) * gkv_ref[...]).astype(BF16)
    cos = cos_ref[...]
    sin = sin_ref[...]
    krr = (kr * cos + krs * sin).astype(BF16)
    for h in range(MLA_HEADS):
        q0 = h * 3 * LANES
        c0 = h * QK_PAD
        q3 = jnp.dot(cqn, wq_ref[:, q0:q0 + 3 * LANES], preferred_element_type=F32)
        kv = jnp.dot(ckvn, wkv_ref[:, c0:c0 + QK_PAD], preferred_element_type=F32)
        q_ref[:, c0:c0 + LANES] = q3[:, :LANES].astype(BF16)
        q_ref[:, c0 + LANES:c0 + QK_PAD] = (q3[:, LANES:2 * LANES] * cos + q3[:, 2 * LANES:] * sin).astype(BF16)
        k_ref[:, c0:c0 + LANES] = kv[:, :LANES].astype(BF16)
        k_ref[:, c0 + LANES:c0 + QK_PAD] = krr
        v_ref[:, h * LANES:(h + 1) * LANES] = kv[:, LANES:].astype(BF16)


def _qkv(part_a, wq3, wkv, gq, gkv, cos_t, sin_t, seq):
    t = part_a.shape[0]
    tm = 512
    per_b = seq // tm
    hq = MLA_HEADS * QK_PAD
    return pl.pallas_call(
        _qkv_kernel,
        grid=(t // tm,),
        in_specs=[pl.BlockSpec((tm, PART_A), lambda i: (i, 0)),
                  pl.BlockSpec(wq3.shape, lambda i: (0, 0)),
                  pl.BlockSpec(wkv.shape, lambda i: (0, 0)),
                  pl.BlockSpec((1, MLA_Q_RANK), lambda i: (0, 0)),
                  pl.BlockSpec((1, MLA_KV_RANK), lambda i: (0, 0)),
                  pl.BlockSpec((tm, LANES), lambda i: (i % per_b, 0)),
                  pl.BlockSpec((tm, LANES), lambda i: (i % per_b, 0))],
        out_specs=[pl.BlockSpec((tm, hq), lambda i: (i, 0)),
                   pl.BlockSpec((tm, hq), lambda i: (i, 0)),
                   pl.BlockSpec((tm, MLA_HEADS * MLA_V_DIM), lambda i: (i, 0))],
        out_shape=[jax.ShapeDtypeStruct((t, hq), BF16),
                   jax.ShapeDtypeStruct((t, hq), BF16),
                   jax.ShapeDtypeStruct((t, MLA_HEADS * MLA_V_DIM), BF16)],
        compiler_params=_cparams(("arbitrary",)),
        name="qkv",
    )(part_a, wq3, wkv, gq, gkv, cos_t, sin_t)


ATTN_TILE = 512
ATTN_CHAINS = 8


def _attn_kernel(*refs, c, masked):
    nch = ATTN_CHAINS
    scratch = refs[-4 * nch:]
    tile = ATTN_TILE
    i = pl.program_id(2)
    if masked:
        q_ref, kin_ref, vin_ref, bias_ref, o_ref, v_ref, k_ref = refs[:7]
    else:
        q_ref, k_ref, vin_ref, o_ref, v_ref = refs[:5]

    @pl.when(i == 0)
    def _():
        seq = vin_ref.shape[0]
        v_ref[:, :LANES] = vin_ref[...]
        v_ref[:, LANES:] = jnp.ones((seq, V_PAD - LANES), BF16)
        if masked:
            lane = lax.broadcasted_iota(jnp.int32, (seq, LANES), 1)
            own = jnp.right_shift(lax.broadcasted_iota(jnp.int32, (seq, LANES), 0), MOBA_BLOCK.bit_length() - 1)
            k_ref[:, :LANES] = kin_ref[...]
            k_ref[:, LANES:] = jnp.where(lane == own, 1.0, 0.0).astype(BF16)

    chains = [dict(rows=slice(n * tile, (n + 1) * tile), s=scratch[4 * n:4 * n + 2], m=scratch[4 * n + 2],
                   acc=scratch[4 * n + 3]) for n in range(nch)]

    def put_scores(chain, slot, t):
        k = k_ref[pl.ds(pl.multiple_of(t * tile, tile), tile), :]
        chain["s"][slot][...] = _nt_dot(q_ref[chain["rows"], :], k)

    def tile_step(chain, slot, t, kind, prefetch=True):
        if prefetch:
            put_scores(chain, 1 - slot, t + 1)
        m_scr, acc_scr = chain["m"], chain["acc"]
        s = chain["s"][slot][...]
        if masked:
            if kind == "prev":
                s = s + bias_ref[0, 1]
            elif kind == "diag":
                s = s + bias_ref[0, 0]
        elif kind == "diag":
            row = lax.broadcasted_iota(jnp.int32, (tile, tile), 0)
            col = lax.broadcasted_iota(jnp.int32, (tile, tile), 1)
            s = jnp.where(row >= col, s, NEG)
        m_old = m_scr[...]
        m_new = jnp.maximum(m_old, jnp.broadcast_to(jnp.max(s, axis=-1, keepdims=True), m_old.shape))
        alpha = jnp.exp2((m_old - m_new) * c)
        p = jnp.concatenate([jnp.exp2(((s[:, j * LANES:(j + 1) * LANES] - m_new) * c).astype(BF16))
                             for j in range(tile // LANES)], axis=1)
        v = v_ref[pl.ds(pl.multiple_of(t * tile, tile), tile), :]
        pv = jnp.dot(p, v, preferred_element_type=F32)
        acc_scr[...] = jnp.concatenate([alpha, alpha], axis=1) * acc_scr[...] + pv
        m_scr[...] = m_new

    for chain in chains:
        chain["m"][...] = jnp.full(chain["m"].shape, NEG, F32)
        chain["acc"][...] = jnp.zeros(chain["acc"].shape, F32)
        put_scores(chain, 0, 0)

    def far_pair(j, _):
        for slot in range(2):
            for chain in chains:
                tile_step(chain, slot, 2 * j + slot, "far")
        return 0

    first = nch * i
    if masked:
        lax.fori_loop(0, jnp.maximum(first // 2 - 1, 0), far_pair, 0)

        @pl.when(i > 0)
        def _():
            for chain in chains:
                tile_step(chain, 0, first - 2, "far")
            for n, chain in enumerate(chains):
                tile_step(chain, 1, first - 1, "prev" if n == 0 else "far")
    else:
        lax.fori_loop(0, first // 2, far_pair, 0)

    for k in range(nch):
        for n, chain in enumerate(chains):
            if n < k:
                continue
            kind = "diag" if n == k else ("prev" if masked and n == k + 1 else "far")
            tile_step(chain, k % 2, first + k, kind, prefetch=(n != k))

    for chain in chains:
        acc = chain["acc"]
        o_ref[chain["rows"], :] = (acc[:, :LANES] / acc[:, LANES:]).astype(o_ref.dtype)


def _attn_scratch(seq, masked):
    tile = ATTN_TILE
    wide = [pltpu.VMEM((seq, V_PAD), BF16)] + ([pltpu.VMEM((seq, QK_PAD), BF16)] if masked else [])
    per_chain = [pltpu.VMEM((tile, tile), F32), pltpu.VMEM((tile, tile), F32),
                 pltpu.VMEM((tile, LANES), F32), pltpu.VMEM((tile, V_PAD), F32)]
    return wide + per_chain * ATTN_CHAINS


def _mla_attention(q, k, v, batch, seq):
    tile = ATTN_CHAINS * ATTN_TILE
    nq = seq // tile
    c = math.log2(math.e) / math.sqrt(MLA_NOPE_DIM + MLA_ROPE_DIM)
    return pl.pallas_call(
        functools.partial(_attn_kernel, c=c, masked=False),
        grid=(batch, MLA_HEADS, nq),
        in_specs=[pl.BlockSpec((tile, QK_PAD), lambda b, h, i: (b * nq + i, h)),
                  pl.BlockSpec((seq, QK_PAD), lambda b, h, i: (b, h)),
                  pl.BlockSpec((seq, MLA_V_DIM), lambda b, h, i: (b, h))],
        out_specs=pl.BlockSpec((tile, MLA_V_DIM), lambda b, h, i: (b * nq + i, h)),
        out_shape=jax.ShapeDtypeStruct((batch * seq, MLA_HEADS * MLA_V_DIM), BF16),
        scratch_shapes=_attn_scratch(seq, False),
        compiler_params=_cparams(("arbitrary", "arbitrary", "arbitrary")),
        name="mla_attn",
    )(q, k, v)


def _moba_select_kernel(q_ref, k_ref, qa_ref, *, seq, nb):
    kf = k_ref[...].astype(F32)
    km = jnp.sum(kf.reshape(nb, MOBA_BLOCK, MOBA_HEAD_DIM), axis=1) * (1.0 / MOBA_BLOCK)
    km_hi = km.astype(BF16)
    km_lo = (km - km_hi.astype(F32)).astype(BF16)
    q = q_ref[...]
    gate = _nt_dot(km_hi, q) + _nt_dot(km_lo, q)
    shift = MOBA_BLOCK.bit_length() - 1
    blk = lax.broadcasted_iota(jnp.int32, (nb, seq), 0)
    qblk = jnp.right_shift(lax.broadcasted_iota(jnp.int32, (nb, seq), 1), shift)
    g = jnp.where(blk < qblk, gate, NEG)
    visible = blk == qblk
    for _ in range(MOBA_TOPK):
        mx = jnp.max(g, axis=0, keepdims=True)
        first = jnp.min(jnp.where(g == mx, blk, nb), axis=0, keepdims=True)
        pick = (blk == first) & (mx > 0.5 * NEG)
        visible = visible | pick
        g = jnp.where(pick, NEG, g)
    mask_t = jnp.concatenate([jnp.where(visible, 0.0, NEG), jnp.zeros((LANES - nb, seq), F32)], axis=0)
    qa_ref[:, :MOBA_HEAD_DIM] = q
    qa_ref[:, MOBA_HEAD_DIM:] = mask_t.T.astype(BF16)


def _moba_select(qkv_mo, batch, seq):
    nb = seq // MOBA_BLOCK
    assert MOBA_HEAD_DIM == LANES and nb <= QK_PAD - MOBA_HEAD_DIM
    return pl.pallas_call(
        functools.partial(_moba_select_kernel, seq=seq, nb=nb),
        grid=(batch, MOBA_HEADS),
        in_specs=[pl.BlockSpec((seq, MOBA_HEAD_DIM), lambda b, h: (b, h)),
                  pl.BlockSpec((seq, MOBA_HEAD_DIM), lambda b, h: (b, MOBA_HEADS + h))],
        out_specs=pl.BlockSpec((seq, QK_PAD), lambda b, h: (b, h)),
        out_shape=jax.ShapeDtypeStruct((batch * seq, MOBA_HEADS * QK_PAD), BF16),
        compiler_params=_cparams(("arbitrary", "arbitrary")),
        name="moba_select",
    )(qkv_mo, qkv_mo)


def _t5_kernel(tab_ref, o_ref, *, inv_scale):
    h = pl.program_id(0)
    r = lax.broadcasted_iota(jnp.int32, (LANES, LANES), 0)
    c = lax.broadcasted_iota(jnp.int32, (LANES, LANES), 1)
    max_exact = T5_BUCKETS // 2
    far = tab_ref[T5_BUCKETS - 1, h]

    def block(offset):
        rel = offset + r - c
        n = jnp.maximum(rel, 0)
        nf = jnp.maximum(n, 1).astype(F32)
        large = max_exact + (jnp.log(nf / max_exact) / math.log(T5_MAX_DISTANCE / max_exact)
                             * (T5_BUCKETS - max_exact)).astype(jnp.int32)
        large = jnp.minimum(large, T5_BUCKETS - 1)
        bucket = jnp.where(n < max_exact, n, large)
        bias = jnp.zeros((LANES, LANES), F32)
        for j in range(T5_BUCKETS):
            bias = jnp.where(bucket == j, tab_ref[j, h], bias)
        return jnp.where(rel >= 0, (bias - far) * inv_scale, NEG)

    near = {0: block(0), 1: block(LANES)}
    nblk = ATTN_TILE // LANES
    for d in range(2):
        for i in range(nblk):
            for j in range(nblk):
                k = d * nblk + i - j
                if k < 0:
                    val = jnp.full((LANES, LANES), NEG, F32)
                else:
                    val = near.get(k, jnp.zeros((LANES, LANES), F32))
                o_ref[0, d, i * LANES:(i + 1) * LANES, j * LANES:(j + 1) * LANES] = val


def _t5_tiles(t5_table):
    assert LANES >= T5_MAX_DISTANCE
    tile = ATTN_TILE
    return pl.pallas_call(
        functools.partial(_t5_kernel, inv_scale=math.sqrt(MOBA_HEAD_DIM)),
        grid=(MOBA_HEADS,),
        in_specs=[pl.BlockSpec(memory_space=pltpu.SMEM)],
        out_specs=pl.BlockSpec((1, 2, tile, tile), lambda h: (h, 0, 0, 0)),
        out_shape=jax.ShapeDtypeStruct((MOBA_HEADS, 2, tile, tile), F32),
        compiler_params=_cparams(("arbitrary",)),
        name="t5_tiles",
    )(t5_table)


def _moba_attention(q_wide, qkv_mo, bias, batch, seq):
    tile = ATTN_CHAINS * ATTN_TILE
    nq = seq // tile
    c = math.log2(math.e) / math.sqrt(MOBA_HEAD_DIM)
    dh = MOBA_HEAD_DIM
    return pl.pallas_call(
        functools.partial(_attn_kernel, c=c, masked=True),
        grid=(batch, MOBA_HEADS, nq),
        in_specs=[pl.BlockSpec((tile, QK_PAD), lambda b, h, i: (b * nq + i, h)),
                  pl.BlockSpec((seq, dh), lambda b, h, i: (b, MOBA_HEADS + h)),
                  pl.BlockSpec((seq, dh), lambda b, h, i: (b, 2 * MOBA_HEADS + h)),
                  pl.BlockSpec((1, 2, ATTN_TILE, ATTN_TILE), lambda b, h, i: (h, 0, 0, 0))],
        out_specs=pl.BlockSpec((tile, dh), lambda b, h, i: (b * nq + i, h)),
        out_shape=jax.ShapeDtypeStruct((batch * seq, MOBA_WIDTH), BF16),
        scratch_shapes=_attn_scratch(seq, True),
        compiler_params=_cparams(("arbitrary", "arbitrary", "arbitrary")),
        name="moba_attn",
    )(q_wide, qkv_mo, qkv_mo, bias)


def _outproj_kernel(oa_ref, ob_ref, wa_ref, wb_ref, x_ref, mod_ref, g_ref, b_ref, wr_ref,
                    x1_ref, h2_ref, lg_ref, y_a, y_b, *, alpha, n_tiles):
    i = pl.program_id(0)
    tm, d = y_a.shape
    chunks = 4
    cn, cr = d // chunks, tm // chunks

    def matmul_into(y_ref, c):
        cols = slice(c * cn, (c + 1) * cn)
        y_ref[:, cols] = (jnp.dot(oa_ref[...], wa_ref[:, cols], preferred_element_type=F32)
                          + jnp.dot(ob_ref[...], wb_ref[:, cols], preferred_element_type=F32))

    def epilogue(y_ref, c):
        rows = slice(c * cr, (c + 1) * cr)
        z = alpha * x_ref[rows, :] + mod_ref[0, 2:3, :] * y_ref[rows, :]
        x1 = _ln(z) * g_ref[...] + b_ref[...]
        x1_ref[rows, :] = x1
        h2 = _ln(x1) * (1.0 + mod_ref[0, 4:5, :]) + mod_ref[0, 3:4, :]
        h2_ref[rows, :] = h2
        h_hi = h2.astype(BF16)
        h_lo = (h2 - h_hi.astype(F32)).astype(BF16)
        zz = (jnp.dot(h_hi, wr_ref[...], preferred_element_type=F32)
              + jnp.dot(h_lo, wr_ref[...], preferred_element_type=F32))
        lg_ref[rows, :] = zz + pltpu.roll(zz, LANES // 2, 1)

    def step(y_new, y_old):
        for c in range(chunks):
            if y_new is not None:
                matmul_into(y_new, c)
            if y_old is not None:
                epilogue(y_old, c)

    inner = (i > 0) & (i < n_tiles)
    pl.when(i == 0)(lambda: step(y_a, None))
    pl.when(inner & (i % 2 == 0))(lambda: step(y_a, y_b))
    pl.when(inner & (i % 2 == 1))(lambda: step(y_b, y_a))
    pl.when(i == n_tiles)(lambda: step(None, y_b if n_tiles % 2 == 0 else y_a))


def _out_proj(o_mla, o_moba, wo, xf, mod3, ln_g, ln_b, wr, seq, alpha):
    t, d = xf.shape
    tm = 512
    n = t // tm
    per_b = seq // tm
    ka, kb = o_mla.shape[1], o_moba.shape[1]
    assert ka == kb and wo.shape[0] == ka + kb
    once = pl.Buffered(1)

    def cur(i):
        return (jnp.minimum(i, n - 1), 0)

    def lag(i):
        return (jnp.maximum(i - 1, 0), 0)

    return pl.pallas_call(
        functools.partial(_outproj_kernel, alpha=alpha, n_tiles=n),
        grid=(n + 1,),
        in_specs=[pl.BlockSpec((tm, ka), cur),
                  pl.BlockSpec((tm, kb), cur),
                  pl.BlockSpec((ka, d), lambda i: (0, 0), pipeline_mode=once),
                  pl.BlockSpec((kb, d), lambda i: (1, 0), pipeline_mode=once),
                  pl.BlockSpec((tm, d), lag),
                  pl.BlockSpec((1, 6, d), lambda i: (jnp.maximum(i - 1, 0) // per_b, 0, 0)),
                  pl.BlockSpec((1, d), lambda i: (0, 0)),
                  pl.BlockSpec((1, d), lambda i: (0, 0)),
                  pl.BlockSpec((d, LANES), lambda i: (0, 0), pipeline_mode=once)],
        out_specs=[pl.BlockSpec((tm, d), lag),
                   pl.BlockSpec((tm, d), lag),
                   pl.BlockSpec((tm, LANES), lag)],
        out_shape=[jax.ShapeDtypeStruct((t, d), F32),
                   jax.ShapeDtypeStruct((t, d), F32),
                   jax.ShapeDtypeStruct((t, LANES), F32)],
        scratch_shapes=[pltpu.VMEM((tm, d), F32), pltpu.VMEM((tm, d), F32)],
        compiler_params=_cparams(("arbitrary",)),
        name="out_proj",
    )(o_mla, o_moba, wo, wo, xf, mod3, ln_g, ln_b, wr)


def _route_kernel(lg_ref, br_ref, info_ref, cnt_ref, run_scr, *, tm):
    i = pl.program_id(0)

    @pl.when(i == 0)
    def _():
        run_scr[...] = jnp.zeros_like(run_scr)

    lg = lg_ref[...] + br_ref[...]
    lane = lax.broadcasted_iota(jnp.int32, (tm, LANES), 1)
    e_lo, e_hi = MOE_GROUPS, MOE_GROUPS + MOE_N_EXPERTS
    is_g = lane < e_lo
    gl = jnp.where(is_g, lg, NEG)
    gmax = jnp.max(gl, axis=-1, keepdims=True)
    gidx = jnp.min(jnp.where(gl == gmax, lane, LANES), axis=-1, keepdims=True)
    g_p = 1.0 / jnp.sum(jnp.where(is_g, jnp.exp(gl - gmax), 0.0), axis=-1, keepdims=True)
    grp_of_lane = jnp.right_shift(lane - e_lo, MOE_EXPERTS_PER_GROUP.bit_length() - 1)
    in_grp = (lane >= e_lo) & (lane < e_hi) & (grp_of_lane == gidx)
    el = jnp.where(in_grp, lg, NEG)
    m1 = jnp.max(el, axis=-1, keepdims=True)
    l1 = jnp.min(jnp.where(el == m1, lane, LANES), axis=-1, keepdims=True)
    el2 = jnp.where(lane == l1, NEG, el)
    m2 = jnp.max(el2, axis=-1, keepdims=True)
    l2 = jnp.min(jnp.where(el2 == m2, lane, LANES), axis=-1, keepdims=True)
    zsum = jnp.sum(jnp.where(in_grp, jnp.exp(el - m1), 0.0), axis=-1, keepdims=True)
    p1 = 1.0 / zsum
    p2 = jnp.exp(m2 - m1) / zsum
    wa = g_p * (p1 / (p1 + p2))
    wb = g_p * (p2 / (p1 + p2))
    hot_a = lane == l1
    hot_b = lane == l2
    onehot = jnp.where(hot_a | hot_b, 1.0, 0.0)
    r = lax.broadcasted_iota(jnp.int32, (tm, tm), 0)
    c = lax.broadcasted_iota(jnp.int32, (tm, tm), 1)
    lower = jnp.where(c < r, 1.0, 0.0).astype(BF16)
    before = jnp.dot(lower, onehot.astype(BF16), preferred_element_type=F32) + run_scr[...]
    rank_a = jnp.sum(jnp.where(hot_a, before, 0.0), axis=-1, keepdims=True)
    rank_b = jnp.sum(jnp.where(hot_b, before, 0.0), axis=-1, keepdims=True)
    run_scr[...] += jnp.sum(onehot, axis=0, keepdims=True)
    info = jnp.zeros((tm, LANES), F32)
    for k, val in enumerate([(l1 - e_lo).astype(F32), (l2 - e_lo).astype(F32), wa, wb, rank_a, rank_b]):
        info = jnp.where(lane == k, val, info)
    info_ref[...] = info
    cnt_ref[...] = run_scr[...]


def _route(logits, br):
    t = logits.shape[0]
    tm = 512
    return pl.pallas_call(
        functools.partial(_route_kernel, tm=tm),
        grid=(t // tm,),
        in_specs=[pl.BlockSpec((tm, LANES), lambda i: (i, 0)),
                  pl.BlockSpec((1, LANES), lambda i: (0, 0))],
        out_specs=[pl.BlockSpec((tm, LANES), lambda i: (i, 0)),
                   pl.BlockSpec((1, LANES), lambda i: (0, 0))],
        out_shape=[jax.ShapeDtypeStruct((t, LANES), F32),
                   jax.ShapeDtypeStruct((1, LANES), F32)],
        scratch_shapes=[pltpu.VMEM((1, LANES), F32)],
        compiler_params=_cparams(("arbitrary",)),
        name="route",
    )(logits, br)


def _pos_kernel(info_ref, start_ref, pos_ref):
    info = info_ref[...]
    tm = info.shape[0]
    lane = lax.broadcasted_iota(jnp.int32, (tm, LANES), 1)
    start = start_ref[...]
    cols = []
    for k in range(2):
        e = jnp.sum(jnp.where(lane == k, info, 0.0), axis=-1, keepdims=True).astype(jnp.int32)
        rank = jnp.sum(jnp.where(lane == 4 + k, info, 0.0), axis=-1, keepdims=True)
        base = jnp.sum(jnp.where(lane == e + MOE_GROUPS, start, 0.0), axis=-1, keepdims=True)
        cols.append(base + rank)
    wide = jnp.where(lane == 0, cols[0], jnp.where(lane == 1, cols[1], 0.0))
    pos_ref[...] = wide.T[:pos_ref.shape[0], :].astype(jnp.int32)


def _positions(info, start_lanes):
    t = info.shape[0]
    tm = 1024
    return pl.pallas_call(
        _pos_kernel,
        grid=(t // tm,),
        in_specs=[pl.BlockSpec((tm, LANES), lambda i: (i, 0)),
                  pl.BlockSpec((1, LANES), lambda i: (0, 0))],
        out_specs=pl.BlockSpec((8, tm), lambda i: (0, i)),
        out_shape=jax.ShapeDtypeStruct((8, t), jnp.int32),
        compiler_params=_cparams(("arbitrary",)),
        name="positions",
    )(info, start_lanes)


def _row_copy(src_ref, src_row, dst_ref, dst_row, sem):
    return pltpu.make_async_copy(src_ref.at[pl.ds(src_row, 1)], dst_ref.at[pl.ds(dst_row, 1)], sem)


def _dispatch_kernel(pa_ref, pb_ref, pad0_ref, padn_ref, used_ref, h_ref, xs_ref, ztile, sem, zsem,
                     *, tm, tr, n_tiles):
    i = pl.program_id(0)
    base = i * tm

    def zero_fill(act):
        def whole_tile(j, _):
            act(pltpu.make_async_copy(ztile, xs_ref.at[pl.ds(pl.multiple_of(j * tr, tr), tr)], zsem))
            return 0

        lax.fori_loop(used_ref[0], n_tiles, whole_tile, 0)

        def expert_pad(e, _):
            n, start = padn_ref[e], pad0_ref[e]
            head = jnp.minimum(jnp.bitwise_and(-start, SUBLANES - 1), n)
            for k in range(SUBLANES - 1):
                pl.when(k < head)(functools.partial(act, _row_copy(ztile, 0, xs_ref, start + k, zsem)))
            off = start + head
            groups = jnp.right_shift(n - head, SUBLANES.bit_length() - 1)
            for bit in reversed(range((tr // SUBLANES - 1).bit_length())):
                size = SUBLANES << bit
                take = jnp.bitwise_and(jnp.right_shift(groups, bit), 1)
                dst = xs_ref.at[pl.ds(pl.multiple_of(off, SUBLANES), size)]
                pl.when(take == 1)(functools.partial(act, pltpu.make_async_copy(ztile.at[pl.ds(0, size)], dst, zsem)))
                off = off + take * size
            return 0

        lax.fori_loop(0, MOE_N_EXPERTS, expert_pad, 0)

    @pl.when(i == 0)
    def _():
        ztile[...] = jnp.zeros(ztile.shape, ztile.dtype)
        zero_fill(lambda cp: cp.start())

    def issue(t, _):
        _row_copy(h_ref, t, xs_ref, pa_ref[base + t], sem).start()
        _row_copy(h_ref, t, xs_ref, pb_ref[base + t], sem).start(priority=1)
        return 0

    lax.fori_loop(0, tm, issue, 0, unroll=8)
    for _ in range(2):
        pltpu.make_async_copy(h_ref, xs_ref.at[pl.ds(0, tm)], sem).wait()

    @pl.when(i == pl.num_programs(0) - 1)
    def _():
        zero_fill(lambda cp: cp.wait())


def _dispatch(pos_a, pos_b, pad0, padn, used, h2, tr, n_tiles):
    t, d = h2.shape
    tm = 256
    grid_spec = pltpu.PrefetchScalarGridSpec(
        num_scalar_prefetch=5,
        grid=(t // tm,),
        in_specs=[pl.BlockSpec((tm, d), lambda i, *_: (i, 0))],
        out_specs=pl.BlockSpec(memory_space=pl.ANY),
        scratch_shapes=[pltpu.VMEM((tr, d), F32), pltpu.SemaphoreType.DMA(()), pltpu.SemaphoreType.DMA(())],
    )
    return pl.pallas_call(
        functools.partial(_dispatch_kernel, tm=tm, tr=tr, n_tiles=n_tiles),
        grid_spec=grid_spec,
        out_shape=jax.ShapeDtypeStruct((n_tiles * tr, d), F32),
        compiler_params=_cparams(("arbitrary",), row_dma=True),
        name="dispatch",
    )(pos_a, pos_b, pad0, padn, used, h2)


def _experts_kernel(texp_ref, tidx_ref, nexte_ref, used_ref, x_ref, w1_hbm, w3_hbm, w2_hbm, o_ref,
                    w1_f32, w3_f32, w2_f32, w1_scr, w3_scr, w2_scr, sems):
    i = pl.program_id(0)
    prev = texp_ref[jnp.maximum(i - 1, 0)]

    def weight_copies(e):
        return [pltpu.make_async_copy(w1_hbm.at[e], w1_f32, sems.at[0]),
                pltpu.make_async_copy(w3_hbm.at[e], w3_f32, sems.at[1]),
                pltpu.make_async_copy(w2_hbm.at[e], w2_f32, sems.at[2])]

    @pl.when(i == 0)
    def _():
        for cp in weight_copies(texp_ref[0]):
            cp.start()

    @pl.when((i == 0) | (texp_ref[i] != prev))
    def _():
        for cp in weight_copies(texp_ref[i]):
            cp.wait()
        w1_scr[...] = w1_f32[...].astype(BF16)
        w3_scr[...] = w3_f32[...].astype(BF16)
        w2_scr[...] = w2_f32[...].astype(BF16)

        @pl.when(nexte_ref[i] >= 0)
        def _():
            for cp in weight_copies(nexte_ref[i]):
                cp.start(priority=1)

    @pl.when(i < used_ref[0])
    def _():
        x = x_ref[...].astype(BF16)
        a = jnp.dot(x, w1_scr[...], preferred_element_type=F32)
        b = jnp.dot(x, w3_scr[...], preferred_element_type=F32)
        hid = (a / (1.0 + jnp.exp(-a))) * b
        o_ref[...] = jnp.dot(hid.astype(BF16), w2_scr[...], preferred_element_type=F32)

    @pl.when(i >= used_ref[0])
    def _():
        o_ref[...] = jnp.zeros_like(o_ref)


def _experts(texp, tidx, nexte, used, xs, w1, w3, w2, tr):
    rows, d = xs.shape
    nt = rows // tr
    f = w1.shape[2]
    grid_spec = pltpu.PrefetchScalarGridSpec(
        num_scalar_prefetch=4,
        grid=(nt,),
        in_specs=[pl.BlockSpec((tr, d), lambda i, te, ti, ne, u: (ti[i], 0)),
                  pl.BlockSpec(memory_space=pl.ANY),
                  pl.BlockSpec(memory_space=pl.ANY),
                  pl.BlockSpec(memory_space=pl.ANY)],
        out_specs=pl.BlockSpec((tr, d), lambda i, te, ti, ne, u: (i, 0)),
        scratch_shapes=[pltpu.VMEM((d, f), F32), pltpu.VMEM((d, f), F32), pltpu.VMEM((f, d), F32),
                        pltpu.VMEM((d, f), BF16), pltpu.VMEM((d, f), BF16), pltpu.VMEM((f, d), BF16),
                        pltpu.SemaphoreType.DMA((3,))],
    )
    return pl.pallas_call(
        _experts_kernel,
        grid_spec=grid_spec,
        out_shape=jax.ShapeDtypeStruct((rows, d), F32),
        compiler_params=_cparams(("arbitrary",)),
        name="experts",
    )(texp, tidx, nexte, used, xs, w1, w3, w2)


def _combine_kernel(pa_ref, pb_ref, ys_ref, x1_ref, info_ref, mod_ref, g_ref, b_ref, o_ref,
                    buf_a, buf_b, sems, *, tm, alpha):
    i = pl.program_id(0)
    last = pl.num_programs(0) - 1

    def issue_row(tile, slot, t):
        _row_copy(ys_ref, pa_ref[tile * tm + t], buf_a.at[slot], t, sems.at[slot]).start()
        _row_copy(ys_ref, pb_ref[tile * tm + t], buf_b.at[slot], t, sems.at[slot]).start(priority=1)

    def combine_rows(slot, rows):
        info = info_ref[rows, :]
        y = info[:, 2:3] * buf_a[slot, rows, :] + info[:, 3:4] * buf_b[slot, rows, :]
        z = alpha * x1_ref[rows, :] + mod_ref[0, 5:6, :] * y
        o_ref[rows, :] = _ln(z) * g_ref[...] + b_ref[...]

    @pl.when(i == 0)
    def _():
        lax.fori_loop(0, tm, lambda t, _: (issue_row(0, 0, t), 0)[1], 0, unroll=8)

    slot = i % 2
    for buf in (buf_a, buf_b):
        pltpu.make_async_copy(ys_ref.at[pl.ds(0, tm)], buf.at[slot], sems.at[slot]).wait()

    chunk = 32

    @pl.when(i < last)
    def _():
        for r in range(0, tm, chunk):
            for t in range(r, r + chunk):
                issue_row(i + 1, 1 - slot, t)
            combine_rows(slot, slice(r, r + chunk))

    @pl.when(i == last)
    def _():
        combine_rows(slot, slice(0, tm))


def _combine(pos_a, pos_b, ys, x1, info, mod3, ln_g, ln_b, seq, alpha):
    t, d = x1.shape
    tm = 256
    per_b = seq // tm
    grid_spec = pltpu.PrefetchScalarGridSpec(
        num_scalar_prefetch=2,
        grid=(t // tm,),
        in_specs=[pl.BlockSpec(memory_space=pl.ANY),
                  pl.BlockSpec((tm, d), lambda i, pa, pb: (i, 0)),
                  pl.BlockSpec((tm, LANES), lambda i, pa, pb: (i, 0)),
                  pl.BlockSpec((1, 6, d), lambda i, pa, pb: (i // per_b, 0, 0)),
                  pl.BlockSpec((1, d), lambda i, pa, pb: (0, 0)),
                  pl.BlockSpec((1, d), lambda i, pa, pb: (0, 0))],
        out_specs=pl.BlockSpec((tm, d), lambda i, pa, pb: (i, 0)),
        scratch_shapes=[pltpu.VMEM((2, tm, d), F32), pltpu.VMEM((2, tm, d), F32), pltpu.SemaphoreType.DMA((2,))],
    )
    return pl.pallas_call(
        functools.partial(_combine_kernel, tm=tm, alpha=alpha),
        grid_spec=grid_spec,
        out_shape=jax.ShapeDtypeStruct((t, d), F32),
        compiler_params=_cparams(("arbitrary",), row_dma=True),
        name="combine",
    )(pos_a, pos_b, ys, x1, info, mod3, ln_g, ln_b)


def _prep_w_in_kernel(w_ref, o_ref):
    r0 = MLA_Q_RANK + MLA_KV_RANK
    r1 = r0 + MLA_ROPE_DIM
    half = MLA_ROPE_DIM // 2
    z = jnp.zeros((LANES - MLA_ROPE_DIM, o_ref.shape[1]), BF16)
    o_ref[:r1, :] = w_ref[:r1, :].astype(BF16)
    o_ref[r1:r0 + LANES, :] = z
    o_ref[r0 + LANES:r0 + LANES + half, :] = w_ref[r0 + half:r1, :].astype(BF16)
    o_ref[r0 + LANES + half:r0 + LANES + MLA_ROPE_DIM, :] = w_ref[r0:r0 + half, :].astype(BF16)
    o_ref[r0 + LANES + MLA_ROPE_DIM:PART_A, :] = z
    o_ref[PART_A:, :] = w_ref[r1:, :].astype(BF16)


def _prep_w_in(w_t):
    n, k = w_t.shape
    tk = 512
    n_out = PART_A + 3 * MOBA_WIDTH
    assert n == MLA_Q_RANK + MLA_KV_RANK + MLA_ROPE_DIM + 3 * MOBA_WIDTH
    return pl.pallas_call(
        _prep_w_in_kernel,
        grid=(k // tk,),
        in_specs=[pl.BlockSpec((n, tk), lambda i: (0, i))],
        out_specs=pl.BlockSpec((n_out, tk), lambda i: (0, i)),
        out_shape=jax.ShapeDtypeStruct((n_out, k), BF16),
        compiler_params=_cparams(("arbitrary",)),
        name="prep_w_in",
    )(w_t)


def _prep_w_uq(w):
    r = w.shape[0]
    w = w.reshape(r, MLA_HEADS, MLA_NOPE_DIM + MLA_ROPE_DIM)
    half = MLA_ROPE_DIM // 2
    nope = w[:, :, :MLA_NOPE_DIM]
    x1 = w[:, :, MLA_NOPE_DIM:MLA_NOPE_DIM + half]
    x2 = w[:, :, MLA_NOPE_DIM + half:]
    z = jnp.zeros((r, MLA_HEADS, LANES - MLA_ROPE_DIM), w.dtype)
    return jnp.concatenate([nope, x1, x2, z, x2, x1, z], axis=2).reshape(r, MLA_HEADS * 3 * LANES).astype(BF16)


def _rope_lanes(seq):
    inv = 1.0 / (ROPE_THETA ** (jnp.arange(0, MLA_ROPE_DIM, 2, dtype=F32) / MLA_ROPE_DIM))
    ang = jnp.arange(seq, dtype=F32)[:, None] * inv[None, :]
    cos, sin = jnp.cos(ang), jnp.sin(ang)
    z = jnp.zeros((seq, LANES - MLA_ROPE_DIM), F32)
    return jnp.concatenate([cos, cos, z], axis=1), jnp.concatenate([-sin, sin, z], axis=1)


def _prep_router(w_rg, b_rg, w_re, b_re):
    d = w_rg.shape[0]
    w = jnp.concatenate([w_rg, w_re], axis=1)
    n = w.shape[1]
    hi = w.astype(BF16)
    lo = (w - hi.astype(F32)).astype(BF16)
    z = jnp.zeros((d, LANES // 2 - n), BF16)
    wr = jnp.concatenate([hi, z, lo, z], axis=1)
    br = jnp.zeros((1, LANES), F32).at[0, :n].set(jnp.concatenate([b_rg, b_re]))
    return wr, br


def _layer(xf, mod3, batch, seq, depth_alpha, w_in, q_norm_g, w_uq, kv_norm_g, w_ukv, w_out, bias_tiles,
           cos_t, sin_t, ln1_g, ln1_b, w_rg, b_rg, w_re, b_re, w1, w3, w2, ln2_g, ln2_b):
    t, d = xf.shape
    part_a, qkv_mo = _in_proj(xf, mod3, _prep_w_in(w_in.T), seq)
    q, k, v = _qkv(part_a, _prep_w_uq(w_uq), w_ukv.astype(BF16), q_norm_g.reshape(1, -1),
                   kv_norm_g.reshape(1, -1), cos_t, sin_t, seq)
    o_mla = _mla_attention(q, k, v, batch, seq)
    q_wide = _moba_select(qkv_mo, batch, seq)
    o_moba = _moba_attention(q_wide, qkv_mo, bias_tiles, batch, seq)
    wo = w_out.astype(BF16)
    wr, br = _prep_router(w_rg, b_rg, w_re, b_re)
    x1, h2, logits = _out_proj(o_mla, o_moba, wo, xf, mod3, ln1_g.reshape(1, d),
                               ln1_b.reshape(1, d), wr, seq, depth_alpha)
    info, counts = _route(logits, br)
    tr = 256
    nt = (2 * t) // tr + MOE_N_EXPERTS
    cnt = counts[0, MOE_GROUPS:MOE_GROUPS + MOE_N_EXPERTS].astype(jnp.int32)
    ntile = (cnt + tr - 1) // tr
    tile_end = jnp.cumsum(ntile)
    tile_start = tile_end - ntile
    used = tile_end[-1]
    start_lanes = jnp.zeros((1, LANES), F32).at[0, MOE_GROUPS:MOE_GROUPS + MOE_N_EXPERTS].set(
        (tile_start * tr).astype(F32))
    pos = _positions(info, start_lanes)
    pos_a, pos_b = pos[0], pos[1]
    tidx = jnp.minimum(jnp.arange(nt, dtype=jnp.int32), used - 1)
    texp = jnp.sum(tidx[:, None] >= tile_end[None, :], axis=1).astype(jnp.int32)
    eids = jnp.arange(MOE_N_EXPERTS, dtype=jnp.int32)
    later = (eids[None, :] > eids[:, None]) & (ntile[None, :] > 0)
    next_nonempty = jnp.where(later.any(axis=1), jnp.argmax(later, axis=1), -1).astype(jnp.int32)
    nexte = jnp.sum(jnp.where(texp[:, None] == eids[None, :], next_nonempty[None, :], 0), axis=1).astype(jnp.int32)
    used1 = used.reshape(1).astype(jnp.int32)
    pad0 = (tile_start * tr + cnt).astype(jnp.int32)
    padn = (ntile * tr - cnt).astype(jnp.int32)
    xs = _dispatch(pos_a, pos_b, pad0, padn, used1, h2, tr, nt)
    ys = _experts(texp, tidx, nexte, used1, xs, w1, w3, w2, tr)
    return _combine(pos_a, pos_b, ys, x1, info, mod3, ln2_g.reshape(1, d), ln2_b.reshape(1, d), seq, depth_alpha)


def kernel(x, c, w_ada, b_ada, w_in, q_norm_g, w_uq, kv_norm_g, w_ukv, w_out, t5_table, ln1_g, ln1_b,
           w_router_group, b_router_group, w_router_expert, b_router_expert, w1, w3, w2, ln2_g, ln2_b):
    batch, seq, d = x.shape
    depth = w_ada.shape[0]
    alpha = (2.0 * depth) ** 0.25
    cos_t, sin_t = _rope_lanes(seq)
    bias_tiles = _t5_tiles(t5_table)
    xf = x.reshape(batch * seq, d)
    for l in range(depth):
        mod3 = _ada_mod(c, w_ada[l], b_ada[l]).reshape(batch, 6, d)
        xf = _layer(xf, mod3, batch, seq, alpha, w_in[l], q_norm_g[l], w_uq[l], kv_norm_g[l], w_ukv[l],
                    w_out[l], bias_tiles, cos_t, sin_t, ln1_g[l], ln1_b[l], w_router_group[l],
                    b_router_group[l], w_router_expert[l], b_router_expert[l], w1[l], w3[l], w2[l],
                    ln2_g[l], ln2_b[l])
    return xf.reshape(batch, seq, d)
```

```python
import functools
import math

import jax
import jax.numpy as jnp
from jax import lax
from jax.experimental import pallas as pl
from jax.experimental.pallas import tpu as pltpu

D_MODEL = 2048
MLA_HEADS = 8
MLA_Q_RANK = 512
MLA_KV_RANK = 256
MLA_NOPE_DIM = 128
MLA_ROPE_DIM = 64
MLA_V_DIM = 128
ROPE_THETA = 10000.0
MOBA_HEADS = 8
MOBA_HEAD_DIM = 128
MOBA_BLOCK = 256
MOBA_TOPK = 3
T5_BUCKETS = 32
T5_MAX_DISTANCE = 128
MOE_GROUPS = 4
MOE_EXPERTS_PER_GROUP = 8
MOE_N_EXPERTS = MOE_GROUPS * MOE_EXPERTS_PER_GROUP
MOE_D_FF = 512
LN_EPS = 1e-5
RMS_EPS = 1e-6
MOBA_WIDTH = MOBA_HEADS * MOBA_HEAD_DIM

LANES = 128
SUBLANES = 8
QK_PAD = 256
V_PAD = 256
PART_A = 1024
NEG = -1e30
VMEM_LIMIT = 56 * 1024 * 1024

F32 = jnp.float32
BF16 = jnp.bfloat16


def _cparams(sem, row_dma=False):
    return pltpu.CompilerParams(dimension_semantics=sem, vmem_limit_bytes=VMEM_LIMIT,
                                disable_bounds_checks=row_dma)


def _ln(x):
    mu = jnp.mean(x, axis=-1, keepdims=True)
    xc = x - mu
    var = jnp.mean(xc * xc, axis=-1, keepdims=True)
    return xc * lax.rsqrt(var + LN_EPS)


def _nt_dot(a, b):
    return lax.dot_general(a, b, (((1,), (1,)), ((), ())), preferred_element_type=F32)


def _ada_kernel(ct_ref, w_ref, b_ref, o_ref, *, batch):
    ct = ct_ref[...]
    ca = ct / (1.0 + jnp.exp(-ct))
    w = w_ref[...]
    rows = [jnp.sum(w * ca[:, b:b + 1], axis=0, keepdims=True) for b in range(batch)]
    o_ref[...] = jnp.concatenate(rows, axis=0) + b_ref[...]


def _ada_mod(c, w_ada, b_ada):
    batch, d = c.shape
    n = w_ada.shape[1]
    tn = 1024
    ct = jnp.zeros((d, LANES), F32).at[:, :batch].set(c.T)
    return pl.pallas_call(
        functools.partial(_ada_kernel, batch=batch),
        grid=(n // tn,),
        in_specs=[pl.BlockSpec((d, LANES), lambda j: (0, 0)),
                  pl.BlockSpec((d, tn), lambda j: (0, j)),
                  pl.BlockSpec((1, tn), lambda j: (0, j))],
        out_specs=pl.BlockSpec((batch, tn), lambda j: (0, j)),
        out_shape=jax.ShapeDtypeStruct((batch, n), F32),
        compiler_params=_cparams(("arbitrary",)),
        name="ada_mod",
    )(ct, w_ada, b_ada.reshape(1, n))


def _inproj_kernel(x_ref, mod_ref, w_ref, a_ref, b_ref, h_scr):
    j = pl.program_id(1)

    @pl.when(j == 0)
    def _():
        rows = 256
        for r in range(0, x_ref.shape[0], rows):
            h = _ln(x_ref[r:r + rows, :]) * (1.0 + mod_ref[0, 1:2, :]) + mod_ref[0, 0:1, :]
            h_scr[r:r + rows, :] = h.astype(BF16)
        a_ref[...] = _nt_dot(h_scr[...], w_ref[...])

    @pl.when(j > 0)
    def _():
        b_ref[...] = _nt_dot(h_scr[...], w_ref[...]).astype(BF16)


def _in_proj(xf, mod3, w4, seq):
    t, d = xf.shape
    n = w4.shape[0]
    tm, tn = 1024, PART_A
    per_b = seq // tm
    return pl.pallas_call(
        _inproj_kernel,
        grid=(t // tm, n // tn),
        in_specs=[pl.BlockSpec((tm, d), lambda i, j: (i, 0)),
                  pl.BlockSpec((1, 6, d), lambda i, j: (i // per_b, 0, 0)),
                  pl.BlockSpec((tn, d), lambda i, j: (j, 0))],
        out_specs=[pl.BlockSpec((tm, tn), lambda i, j: (i, 0)),
                   pl.BlockSpec((tm, tn), lambda i, j: (i, jnp.maximum(j - 1, 0)))],
        out_shape=[jax.ShapeDtypeStruct((t, PART_A), F32),
                   jax.ShapeDtypeStruct((t, n - PART_A), BF16)],
        scratch_shapes=[pltpu.VMEM((tm, d), BF16)],
        compiler_params=_cparams(("arbitrary", "arbitrary")),
        name="in_proj",
    )(xf, mod3, w4)


def _qkv_kernel(a_ref, wq_ref, wkv_ref, gq_ref, gkv_ref, cos_ref, sin_ref, q_ref, k_ref, v_ref):
    a = a_ref[...]
    cq = a[:, :MLA_Q_RANK]
    ckv = a[:, MLA_Q_RANK:MLA_Q_RANK + MLA_KV_RANK]
    kr = a[:, 768:896]
    krs = a[:, 896:1024]
    cqn = (cq * lax.rsqrt(jnp.mean(cq * cq, axis=-1, keepdims=True) + RMS_EPS) * gq_ref[...]).astype(BF16)
    ckvn = (ckv * lax.rsqrt(jnp.mean(ckv * ckv, axis=-1, keepdims=True) + RMS_EPS) * gkv_ref[...]).astype(BF16)
    cos = cos_ref[...]
    sin = sin_ref[...]
    krr = (kr * cos + krs * sin).astype(BF16)
    for h in range(MLA_HEADS):
        q0 = h * 3 * LANES
        c0 = h * QK_PAD
        q3 = jnp.dot(cqn, wq_ref[:, q0:q0 + 3 * LANES], preferred_element_type=F32)
        kv = jnp.dot(ckvn, wkv_ref[:, c0:c0 + QK_PAD], preferred_element_type=F32)
        q_ref[:, c0:c0 + LANES] = q3[:, :LANES].astype(BF16)
        q_ref[:, c0 + LANES:c0 + QK_PAD] = (q3[:, LANES:2 * LANES] * cos + q3[:, 2 * LANES:] * sin).astype(BF16)
        k_ref[:, c0:c0 + LANES] = kv[:, :LANES].astype(BF16)
        k_ref[:, c0 + LANES:c0 + QK_PAD] = krr
        v_ref[:, h * LANES:(h + 1) * LANES] = kv[:, LANES:].astype(BF16)


def _qkv(part_a, wq3, wkv, gq, gkv, cos_t, sin_t, seq):
    t = part_a.shape[0]
    tm = 512
    per_b = seq // tm
    hq = MLA_HEADS * QK_PAD
    return pl.pallas_call(
        _qkv_kernel,
        grid=(t // tm,),
        in_specs=[pl.BlockSpec((tm, PART_A), lambda i: (i, 0)),
                  pl.BlockSpec(wq3.shape, lambda i: (0, 0)),
                  pl.BlockSpec(wkv.shape, lambda i: (0, 0)),
                  pl.BlockSpec((1, MLA_Q_RANK), lambda i: (0, 0)),
                  pl.BlockSpec((1, MLA_KV_RANK), lambda i: (0, 0)),
                  pl.BlockSpec((tm, LANES), lambda i: (i % per_b, 0)),
                  pl.BlockSpec((tm, LANES), lambda i: (i % per_b, 0))],
        out_specs=[pl.BlockSpec((tm, hq), lambda i: (i, 0)),
                   pl.BlockSpec((tm, hq), lambda i: (i, 0)),
                   pl.BlockSpec((tm, MLA_HEADS * MLA_V_DIM), lambda i: (i, 0))],
        out_shape=[jax.ShapeDtypeStruct((t, hq), BF16),
                   jax.ShapeDtypeStruct((t, hq), BF16),
                   jax.ShapeDtypeStruct((t, MLA_HEADS * MLA_V_DIM), BF16)],
        compiler_params=_cparams(("arbitrary",)),
        name="qkv",
    )(part_a, wq3, wkv, gq, gkv, cos_t, sin_t)


ATTN_TILE = 512
ATTN_CHAINS = 8


def _attn_kernel(*refs, c, masked):
    nch = ATTN_CHAINS
    scratch = refs[-4 * nch:]
    tile = ATTN_TILE
    i = pl.program_id(2)
    if masked:
        q_ref, kin_ref, vin_ref, bias_ref, o_ref, v_ref, k_ref = refs[:7]
    else:
        q_ref, k_ref, vin_ref, o_ref, v_ref = refs[:5]

    @pl.when((pl.program_id(0) == 0) & (pl.program_id(1) == 0) & (i == 0))
    def _():
        seq = vin_ref.shape[0]
        v_ref[:, LANES:] = jnp.ones((seq, V_PAD - LANES), BF16)
        if masked:
            lane = lax.broadcasted_iota(jnp.int32, (seq, LANES), 1)
            own = jnp.right_shift(lax.broadcasted_iota(jnp.int32, (seq, LANES), 0), MOBA_BLOCK.bit_length() - 1)
            k_ref[:, LANES:] = jnp.where(lane == own, 1.0, 0.0).astype(BF16)

    @pl.when(i == 0)
    def _():
        v_ref[:, :LANES] = vin_ref[...]
        if masked:
            k_ref[:, :LANES] = kin_ref[...]

    chains = [dict(rows=slice(n * tile, (n + 1) * tile), s=scratch[4 * n:4 * n + 2], m=scratch[4 * n + 2],
                   acc=scratch[4 * n + 3]) for n in range(nch)]

    def put_scores(chain, slot, t):
        k = k_ref[pl.ds(pl.multiple_of(t * tile, tile), tile), :]
        chain["s"][slot][...] = _nt_dot(q_ref[chain["rows"], :], k)

    def tile_step(chain, slot, t, kind, prefetch=True):
        if prefetch:
            put_scores(chain, 1 - slot, t + 1)
        m_scr, acc_scr = chain["m"], chain["acc"]
        s = chain["s"][slot][...]
        if masked:
            if kind == "prev":
                s = s + bias_ref[0, 1]
            elif kind == "diag":
                s = s + bias_ref[0, 0]
        elif kind == "diag":
            row = lax.broadcasted_iota(jnp.int32, (tile, tile), 0)
            col = lax.broadcasted_iota(jnp.int32, (tile, tile), 1)
            s = jnp.where(row >= col, s, NEG)
        m_old = m_scr[...]
        m_new = jnp.maximum(m_old, jnp.broadcast_to(jnp.max(s, axis=-1, keepdims=True), m_old.shape))
        alpha = jnp.exp2((m_old - m_new) * c)
        p = jnp.concatenate([jnp.exp2(((s[:, j * LANES:(j + 1) * LANES] - m_new) * c).astype(BF16))
                             for j in range(tile // LANES)], axis=1)
        v = v_ref[pl.ds(pl.multiple_of(t * tile, tile), tile), :]
        pv = jnp.dot(p, v, preferred_element_type=F32)
        acc_scr[...] = jnp.concatenate([alpha, alpha], axis=1) * acc_scr[...] + pv
        m_scr[...] = m_new

    for chain in chains:
        chain["m"][...] = jnp.full(chain["m"].shape, NEG, F32)
        chain["acc"][...] = jnp.zeros(chain["acc"].shape, F32)
        put_scores(chain, 0, 0)

    def far_pair(j, _):
        for slot in range(2):
            for chain in chains:
                tile_step(chain, slot, 2 * j + slot, "far")
        return 0

    first = nch * i
    if masked:
        lax.fori_loop(0, jnp.maximum(first // 2 - 1, 0), far_pair, 0)

        @pl.when(i > 0)
        def _():
            for chain in chains:
                tile_step(chain, 0, first - 2, "far")
            for n, chain in enumerate(chains):
                tile_step(chain, 1, first - 1, "prev" if n == 0 else "far")
    else:
        lax.fori_loop(0, first // 2, far_pair, 0)

    for k in range(nch):
        for n, chain in enumerate(chains):
            if n < k:
                continue
            kind = "diag" if n == k else ("prev" if masked and n == k + 1 else "far")
            tile_step(chain, k % 2, first + k, kind, prefetch=(n != k))

    for chain in chains:
        acc = chain["acc"]
        o_ref[chain["rows"], :] = (acc[:, :LANES] / acc[:, LANES:]).astype(o_ref.dtype)


def _attn_scratch(seq, masked):
    tile = ATTN_TILE
    wide = [pltpu.VMEM((seq, V_PAD), BF16)] + ([pltpu.VMEM((seq, QK_PAD), BF16)] if masked else [])
    per_chain = [pltpu.VMEM((tile, tile), F32), pltpu.VMEM((tile, tile), F32),
                 pltpu.VMEM((tile, LANES), F32), pltpu.VMEM((tile, V_PAD), F32)]
    return wide + per_chain * ATTN_CHAINS


def _mla_attention(q, k, v, batch, seq):
    tile = ATTN_CHAINS * ATTN_TILE
    nq = seq // tile
    c = math.log2(math.e) / math.sqrt(MLA_NOPE_DIM + MLA_ROPE_DIM)
    return pl.pallas_call(
        functools.partial(_attn_kernel, c=c, masked=False),
        grid=(batch, MLA_HEADS, nq),
        in_specs=[pl.BlockSpec((tile, QK_PAD), lambda b, h, i: (b * nq + i, h)),
                  pl.BlockSpec((seq, QK_PAD), lambda b, h, i: (b, h)),
                  pl.BlockSpec((seq, MLA_V_DIM), lambda b, h, i: (b, h))],
        out_specs=pl.BlockSpec((tile, MLA_V_DIM), lambda b, h, i: (b * nq + i, h)),
        out_shape=jax.ShapeDtypeStruct((batch * seq, MLA_HEADS * MLA_V_DIM), BF16),
        scratch_shapes=_attn_scratch(seq, False),
        compiler_params=_cparams(("arbitrary", "arbitrary", "arbitrary")),
        name="mla_attn",
    )(q, k, v)


def _moba_select_kernel(q_ref, k_ref, qa_ref, *, seq, nb):
    kf = k_ref[...].astype(F32)
    km = jnp.sum(kf.reshape(nb, MOBA_BLOCK, MOBA_HEAD_DIM), axis=1) * (1.0 / MOBA_BLOCK)
    km_hi = km.astype(BF16)
    km_lo = (km - km_hi.astype(F32)).astype(BF16)
    q = q_ref[...]
    gate = _nt_dot(km_hi, q) + _nt_dot(km_lo, q)
    shift = MOBA_BLOCK.bit_length() - 1
    blk = lax.broadcasted_iota(jnp.int32, (nb, seq), 0)
    qblk = jnp.right_shift(lax.broadcasted_iota(jnp.int32, (nb, seq), 1), shift)
    g = jnp.where(blk < qblk, gate, NEG)
    visible = blk == qblk
    for _ in range(MOBA_TOPK):
        mx = jnp.max(g, axis=0, keepdims=True)
        first = jnp.min(jnp.where(g == mx, blk, nb), axis=0, keepdims=True)
        pick = (blk == first) & (mx > 0.5 * NEG)
        visible = visible | pick
        g = jnp.where(pick, NEG, g)
    mask_t = jnp.concatenate([jnp.where(visible, 0.0, NEG), jnp.zeros((LANES - nb, seq), F32)], axis=0)
    qa_ref[:, :MOBA_HEAD_DIM] = q
    qa_ref[:, MOBA_HEAD_DIM:] = mask_t.T.astype(BF16)


def _moba_select(qkv_mo, batch, seq):
    nb = seq // MOBA_BLOCK
    assert MOBA_HEAD_DIM == LANES and nb <= QK_PAD - MOBA_HEAD_DIM
    return pl.pallas_call(
        functools.partial(_moba_select_kernel, seq=seq, nb=nb),
        grid=(batch, MOBA_HEADS),
        in_specs=[pl.BlockSpec((seq, MOBA_HEAD_DIM), lambda b, h: (b, h)),
                  pl.BlockSpec((seq, MOBA_HEAD_DIM), lambda b, h: (b, MOBA_HEADS + h))],
        out_specs=pl.BlockSpec((seq, QK_PAD), lambda b, h: (b, h)),
        out_shape=jax.ShapeDtypeStruct((batch * seq, MOBA_HEADS * QK_PAD), BF16),
        compiler_params=_cparams(("arbitrary", "arbitrary")),
        name="moba_select",
    )(qkv_mo, qkv_mo)


def _t5_kernel(tab_ref, o_ref, *, inv_scale):
    h = pl.program_id(0)
    r = lax.broadcasted_iota(jnp.int32, (LANES, LANES), 0)
    c = lax.broadcasted_iota(jnp.int32, (LANES, LANES), 1)
    max_exact = T5_BUCKETS // 2
    far = tab_ref[T5_BUCKETS - 1, h]

    def block(offset):
        rel = offset + r - c
        n = jnp.maximum(rel, 0)
        nf = jnp.maximum(n, 1).astype(F32)
        large = max_exact + (jnp.log(nf / max_exact) / math.log(T5_MAX_DISTANCE / max_exact)
                             * (T5_BUCKETS - max_exact)).astype(jnp.int32)
        large = jnp.minimum(large, T5_BUCKETS - 1)
        bucket = jnp.where(n < max_exact, n, large)
        bias = jnp.zeros((LANES, LANES), F32)
        for j in range(T5_BUCKETS):
            bias = jnp.where(bucket == j, tab_ref[j, h], bias)
        return jnp.where(rel >= 0, (bias - far) * inv_scale, NEG)

    near = {0: block(0), 1: block(LANES)}
    nblk = ATTN_TILE // LANES
    for d in range(2):
        for i in range(nblk):
            for j in range(nblk):
                k = d * nblk + i - j
                if k < 0:
                    val = jnp.full((LANES, LANES), NEG, F32)
                else:
                    val = near.get(k, jnp.zeros((LANES, LANES), F32))
                o_ref[0, d, i * LANES:(i + 1) * LANES, j * LANES:(j + 1) * LANES] = val


def _t5_tiles(t5_table):
    assert LANES >= T5_MAX_DISTANCE
    tile = ATTN_TILE
    return pl.pallas_call(
        functools.partial(_t5_kernel, inv_scale=math.sqrt(MOBA_HEAD_DIM)),
        grid=(MOBA_HEADS,),
        in_specs=[pl.BlockSpec(memory_space=pltpu.SMEM)],
        out_specs=pl.BlockSpec((1, 2, tile, tile), lambda h: (h, 0, 0, 0)),
        out_shape=jax.ShapeDtypeStruct((MOBA_HEADS, 2, tile, tile), F32),
        compiler_params=_cparams(("arbitrary",)),
        name="t5_tiles",
    )(t5_table)


def _moba_attention(q_wide, qkv_mo, bias, batch, seq):
    tile = ATTN_CHAINS * ATTN_TILE
    nq = seq // tile
    c = math.log2(math.e) / math.sqrt(MOBA_HEAD_DIM)
    dh = MOBA_HEAD_DIM
    return pl.pallas_call(
        functools.partial(_attn_kernel, c=c, masked=True),
        grid=(batch, MOBA_HEADS, nq),
        in_specs=[pl.BlockSpec((tile, QK_PAD), lambda b, h, i: (b * nq + i, h)),
                  pl.BlockSpec((seq, dh), lambda b, h, i: (b, MOBA_HEADS + h)),
                  pl.BlockSpec((seq, dh), lambda b, h, i: (b, 2 * MOBA_HEADS + h)),
                  pl.BlockSpec((1, 2, ATTN_TILE, ATTN_TILE), lambda b, h, i: (h, 0, 0, 0))],
        out_specs=pl.BlockSpec((tile, dh), lambda b, h, i: (b * nq + i, h)),
        out_shape=jax.ShapeDtypeStruct((batch * seq, MOBA_WIDTH), BF16),
        scratch_shapes=_attn_scratch(seq, True),
        compiler_params=_cparams(("arbitrary", "arbitrary", "arbitrary")),
        name="moba_attn",
    )(q_wide, qkv_mo, qkv_mo, bias)


def _outproj_kernel(oa_ref, ob_ref, wa_ref, wb_ref, x_ref, mod_ref, g_ref, b_ref, wr_ref,
                    x1_ref, h2_ref, lg_ref, y_a, y_b, *, alpha, n_tiles):
    i = pl.program_id(0)
    tm, d = y_a.shape
    chunks = 4
    cn, cr = d // chunks, tm // chunks

    def matmul_into(y_ref, c):
        cols = slice(c * cn, (c + 1) * cn)
        y_ref[:, cols] = (jnp.dot(oa_ref[...], wa_ref[:, cols], preferred_element_type=F32)
                          + jnp.dot(ob_ref[...], wb_ref[:, cols], preferred_element_type=F32))

    def epilogue(y_ref, c):
        rows = slice(c * cr, (c + 1) * cr)
        z = alpha * x_ref[rows, :] + mod_ref[0, 2:3, :] * y_ref[rows, :]
        x1 = _ln(z) * g_ref[...] + b_ref[...]
        x1_ref[rows, :] = x1
        h2 = _ln(x1) * (1.0 + mod_ref[0, 4:5, :]) + mod_ref[0, 3:4, :]
        h_hi = h2.astype(BF16)
        h_lo = (h2 - h_hi.astype(F32)).astype(BF16)
        h_bits = pltpu.bitcast(h_hi.astype(F32), jnp.uint32)
        h2_ref[rows, :] = h_bits[:, :d // 2] | jnp.right_shift(h_bits[:, d // 2:], 16)
        zz = (jnp.dot(h_hi, wr_ref[...], preferred_element_type=F32)
              + jnp.dot(h_lo, wr_ref[...], preferred_element_type=F32))
        lg_ref[rows, :] = zz + pltpu.roll(zz, LANES // 2, 1)

    def step(y_new, y_old):
        for c in range(chunks):
            if y_new is not None:
                matmul_into(y_new, c)
            if y_old is not None:
                epilogue(y_old, c)

    inner = (i > 0) & (i < n_tiles)
    pl.when(i == 0)(lambda: step(y_a, None))
    pl.when(inner & (i % 2 == 0))(lambda: step(y_a, y_b))
    pl.when(inner & (i % 2 == 1))(lambda: step(y_b, y_a))
    pl.when(i == n_tiles)(lambda: step(None, y_b if n_tiles % 2 == 0 else y_a))


def _out_proj(o_mla, o_moba, wo, xf, mod3, ln_g, ln_b, wr, seq, alpha):
    t, d = xf.shape
    tm = 512
    n = t // tm
    per_b = seq // tm
    ka, kb = o_mla.shape[1], o_moba.shape[1]
    assert ka == kb and wo.shape[0] == ka + kb
    once = pl.Buffered(1)

    def cur(i):
        return (jnp.minimum(i, n - 1), 0)

    def lag(i):
        return (jnp.maximum(i - 1, 0), 0)

    return pl.pallas_call(
        functools.partial(_outproj_kernel, alpha=alpha, n_tiles=n),
        grid=(n + 1,),
        in_specs=[pl.BlockSpec((tm, ka), cur),
                  pl.BlockSpec((tm, kb), cur),
                  pl.BlockSpec((ka, d), lambda i: (0, 0), pipeline_mode=once),
                  pl.BlockSpec((kb, d), lambda i: (1, 0), pipeline_mode=once),
                  pl.BlockSpec((tm, d), lag),
                  pl.BlockSpec((1, 6, d), lambda i: (jnp.maximum(i - 1, 0) // per_b, 0, 0)),
                  pl.BlockSpec((1, d), lambda i: (0, 0)),
                  pl.BlockSpec((1, d), lambda i: (0, 0)),
                  pl.BlockSpec((d, LANES), lambda i: (0, 0), pipeline_mode=once)],
        out_specs=[pl.BlockSpec((tm, d), lag),
                   pl.BlockSpec((tm, d // 2), lag),
                   pl.BlockSpec((tm, LANES), lag)],
        out_shape=[jax.ShapeDtypeStruct((t, d), F32),
                   jax.ShapeDtypeStruct((t, d // 2), jnp.uint32),
                   jax.ShapeDtypeStruct((t, LANES), F32)],
        scratch_shapes=[pltpu.VMEM((tm, d), F32), pltpu.VMEM((tm, d), F32)],
        compiler_params=_cparams(("arbitrary",)),
        name="out_proj",
    )(o_mla, o_moba, wo, wo, xf, mod3, ln_g, ln_b, wr)


def _route_kernel(lg_ref, br_ref, info_ref, cnt_ref, run_scr, *, tm):
    i = pl.program_id(0)

    @pl.when(i == 0)
    def _():
        run_scr[...] = jnp.zeros_like(run_scr)

    lg = lg_ref[...] + br_ref[...]
    lane = lax.broadcasted_iota(jnp.int32, (tm, LANES), 1)
    e_lo, e_hi = MOE_GROUPS, MOE_GROUPS + MOE_N_EXPERTS
    is_g = lane < e_lo
    gl = jnp.where(is_g, lg, NEG)
    gmax = jnp.max(gl, axis=-1, keepdims=True)
    gidx = jnp.min(jnp.where(gl == gmax, lane, LANES), axis=-1, keepdims=True)
    g_p = 1.0 / jnp.sum(jnp.where(is_g, jnp.exp(gl - gmax), 0.0), axis=-1, keepdims=True)
    grp_of_lane = jnp.right_shift(lane - e_lo, MOE_EXPERTS_PER_GROUP.bit_length() - 1)
    in_grp = (lane >= e_lo) & (lane < e_hi) & (grp_of_lane == gidx)
    el = jnp.where(in_grp, lg, NEG)
    m1 = jnp.max(el, axis=-1, keepdims=True)
    l1 = jnp.min(jnp.where(el == m1, lane, LANES), axis=-1, keepdims=True)
    el2 = jnp.where(lane == l1, NEG, el)
    m2 = jnp.max(el2, axis=-1, keepdims=True)
    l2 = jnp.min(jnp.where(el2 == m2, lane, LANES), axis=-1, keepdims=True)
    zsum = jnp.sum(jnp.where(in_grp, jnp.exp(el - m1), 0.0), axis=-1, keepdims=True)
    p1 = 1.0 / zsum
    p2 = jnp.exp(m2 - m1) / zsum
    wa = g_p * (p1 / (p1 + p2))
    wb = g_p * (p2 / (p1 + p2))
    hot_a = lane == l1
    hot_b = lane == l2
    onehot = jnp.where(hot_a | hot_b, 1.0, 0.0)
    r = lax.broadcasted_iota(jnp.int32, (tm, tm), 0)
    c = lax.broadcasted_iota(jnp.int32, (tm, tm), 1)
    lower = jnp.where(c < r, 1.0, 0.0).astype(BF16)
    before = jnp.dot(lower, onehot.astype(BF16), preferred_element_type=F32) + run_scr[...]
    rank_a = jnp.sum(jnp.where(hot_a, before, 0.0), axis=-1, keepdims=True)
    rank_b = jnp.sum(jnp.where(hot_b, before, 0.0), axis=-1, keepdims=True)
    run_scr[...] += jnp.sum(onehot, axis=0, keepdims=True)
    info = jnp.zeros((tm, LANES), F32)
    for k, val in enumerate([(l1 - e_lo).astype(F32), (l2 - e_lo).astype(F32), wa, wb, rank_a, rank_b]):
        info = jnp.where(lane == k, val, info)
    info_ref[...] = info
    cnt_ref[...] = run_scr[...]


def _route(logits, br):
    t = logits.shape[0]
    tm = 512
    return pl.pallas_call(
        functools.partial(_route_kernel, tm=tm),
        grid=(t // tm,),
        in_specs=[pl.BlockSpec((tm, LANES), lambda i: (i, 0)),
                  pl.BlockSpec((1, LANES), lambda i: (0, 0))],
        out_specs=[pl.BlockSpec((tm, LANES), lambda i: (i, 0)),
                   pl.BlockSpec((1, LANES), lambda i: (0, 0))],
        out_shape=[jax.ShapeDtypeStruct((t, LANES), F32),
                   jax.ShapeDtypeStruct((1, LANES), F32)],
        scratch_shapes=[pltpu.VMEM((1, LANES), F32)],
        compiler_params=_cparams(("arbitrary",)),
        name="route",
    )(logits, br)


def _pos_kernel(info_ref, start_ref, pos_ref):
    info = info_ref[...]
    tm = info.shape[0]
    lane = lax.broadcasted_iota(jnp.int32, (tm, LANES), 1)
    start = start_ref[...]
    cols = []
    for k in range(2):
        e = jnp.sum(jnp.where(lane == k, info, 0.0), axis=-1, keepdims=True).astype(jnp.int32)
        rank = jnp.sum(jnp.where(lane == 4 + k, info, 0.0), axis=-1, keepdims=True)
        base = jnp.sum(jnp.where(lane == e + MOE_GROUPS, start, 0.0), axis=-1, keepdims=True)
        cols.append(base + rank)
    wide = jnp.where(lane == 0, cols[0], jnp.where(lane == 1, cols[1], 0.0))
    pos_ref[...] = wide.T[:pos_ref.shape[0], :].astype(jnp.int32)


def _positions(info, start_lanes):
    t = info.shape[0]
    tm = 1024
    return pl.pallas_call(
        _pos_kernel,
        grid=(t // tm,),
        in_specs=[pl.BlockSpec((tm, LANES), lambda i: (i, 0)),
                  pl.BlockSpec((1, LANES), lambda i: (0, 0))],
        out_specs=pl.BlockSpec((8, tm), lambda i: (0, i)),
        out_shape=jax.ShapeDtypeStruct((8, t), jnp.int32),
        compiler_params=_cparams(("arbitrary",)),
        name="positions",
    )(info, start_lanes)


def _row_copy(src_ref, src_row, dst_ref, dst_row, sem):
    return pltpu.make_async_copy(src_ref.at[pl.ds(src_row, 1)], dst_ref.at[pl.ds(dst_row, 1)], sem)


def _dispatch_kernel(pa_ref, pb_ref, pad0_ref, padn_ref, used_ref, h_ref, xs_ref, ztile, sem, zsem,
                     *, tm, tr, n_tiles):
    i = pl.program_id(0)
    base = i * tm

    def zero_fill(act):
        def whole_tile(j, _):
            act(pltpu.make_async_copy(ztile, xs_ref.at[pl.ds(pl.multiple_of(j * tr, tr), tr)], zsem))
            return 0

        lax.fori_loop(used_ref[0], n_tiles, whole_tile, 0)

        def expert_pad(e, _):
            n, start = padn_ref[e], pad0_ref[e]
            head = jnp.minimum(jnp.bitwise_and(-start, SUBLANES - 1), n)
            for k in range(SUBLANES - 1):
                pl.when(k < head)(functools.partial(act, _row_copy(ztile, 0, xs_ref, start + k, zsem)))
            off = start + head
            groups = jnp.right_shift(n - head, SUBLANES.bit_length() - 1)
            for bit in reversed(range((tr // SUBLANES - 1).bit_length())):
                size = SUBLANES << bit
                take = jnp.bitwise_and(jnp.right_shift(groups, bit), 1)
                dst = xs_ref.at[pl.ds(pl.multiple_of(off, SUBLANES), size)]
                pl.when(take == 1)(functools.partial(act, pltpu.make_async_copy(ztile.at[pl.ds(0, size)], dst, zsem)))
                off = off + take * size
            return 0

        lax.fori_loop(0, MOE_N_EXPERTS, expert_pad, 0)

    @pl.when(i == 0)
    def _():
        ztile[...] = jnp.zeros(ztile.shape, ztile.dtype)
        zero_fill(lambda cp: cp.start())

    def issue(t, _):
        _row_copy(h_ref, t, xs_ref, pa_ref[base + t], sem).start()
        _row_copy(h_ref, t, xs_ref, pb_ref[base + t], sem).start(priority=1)
        return 0

    lax.fori_loop(0, tm, issue, 0, unroll=8)
    for _ in range(2):
        pltpu.make_async_copy(h_ref, xs_ref.at[pl.ds(0, tm)], sem).wait()

    @pl.when(i == pl.num_programs(0) - 1)
    def _():
        zero_fill(lambda cp: cp.wait())


def _dispatch(pos_a, pos_b, pad0, padn, used, h2, tr, n_tiles):
    t, d = h2.shape
    tm = 256
    grid_spec = pltpu.PrefetchScalarGridSpec(
        num_scalar_prefetch=5,
        grid=(t // tm,),
        in_specs=[pl.BlockSpec((tm, d), lambda i, *_: (i, 0))],
        out_specs=pl.BlockSpec(memory_space=pl.ANY),
        scratch_shapes=[pltpu.VMEM((tr, d), h2.dtype), pltpu.SemaphoreType.DMA(()), pltpu.SemaphoreType.DMA(())],
    )
    return pl.pallas_call(
        functools.partial(_dispatch_kernel, tm=tm, tr=tr, n_tiles=n_tiles),
        grid_spec=grid_spec,
        out_shape=jax.ShapeDtypeStruct((n_tiles * tr, d), h2.dtype),
        compiler_params=_cparams(("arbitrary",), row_dma=True),
        name="dispatch",
    )(pos_a, pos_b, pad0, padn, used, h2)


def _experts_kernel(texp_ref, tidx_ref, nexte_ref, used_ref, x_ref, w1_hbm, w3_hbm, w2_hbm, o_ref,
                    w1_f32, w3_f32, w2_f32, w1_scr, w3_scr, w2_scr, sems):
    i = pl.program_id(0)
    prev = texp_ref[jnp.maximum(i - 1, 0)]

    def weight_copies(e):
        return [pltpu.make_async_copy(w1_hbm.at[e], w1_f32, sems.at[0]),
                pltpu.make_async_copy(w3_hbm.at[e], w3_f32, sems.at[1]),
                pltpu.make_async_copy(w2_hbm.at[e], w2_f32, sems.at[2])]

    @pl.when(i == 0)
    def _():
        for cp in weight_copies(texp_ref[0]):
            cp.start()

    @pl.when((i == 0) | (texp_ref[i] != prev))
    def _():
        for cp in weight_copies(texp_ref[i]):
            cp.wait()
        w1_scr[...] = w1_f32[...].astype(BF16)
        w3_scr[...] = w3_f32[...].astype(BF16)
        w2_scr[...] = w2_f32[...].astype(BF16)

        @pl.when(nexte_ref[i] >= 0)
        def _():
            for cp in weight_copies(nexte_ref[i]):
                cp.start(priority=1)

    @pl.when(i < used_ref[0])
    def _():
        bits = x_ref[...]
        x = jnp.concatenate(
            [pltpu.bitcast(jnp.bitwise_and(bits, jnp.uint32(0xFFFF0000)), F32).astype(BF16),
             pltpu.bitcast(jnp.left_shift(bits, 16), F32).astype(BF16)], axis=1)
        a = jnp.dot(x, w1_scr[...], preferred_element_type=F32)
        b = jnp.dot(x, w3_scr[...], preferred_element_type=F32)
        hid = (a / (1.0 + jnp.exp(-a))) * b
        o_ref[...] = jnp.dot(hid.astype(BF16), w2_scr[...], preferred_element_type=F32)

    @pl.when(i >= used_ref[0])
    def _():
        o_ref[...] = jnp.zeros_like(o_ref)


def _experts(texp, tidx, nexte, used, xs, w1, w3, w2, tr):
    rows = xs.shape[0]
    nt = rows // tr
    d, f = w1.shape[1:]
    assert xs.shape[1] * 2 == d and xs.dtype == jnp.uint32
    grid_spec = pltpu.PrefetchScalarGridSpec(
        num_scalar_prefetch=4,
        grid=(nt,),
        in_specs=[pl.BlockSpec((tr, d // 2), lambda i, te, ti, ne, u: (ti[i], 0)),
                  pl.BlockSpec(memory_space=pl.ANY),
                  pl.BlockSpec(memory_space=pl.ANY),
                  pl.BlockSpec(memory_space=pl.ANY)],
        out_specs=pl.BlockSpec((tr, d), lambda i, te, ti, ne, u: (i, 0)),
        scratch_shapes=[pltpu.VMEM((d, f), F32), pltpu.VMEM((d, f), F32), pltpu.VMEM((f, d), F32),
                        pltpu.VMEM((d, f), BF16), pltpu.VMEM((d, f), BF16), pltpu.VMEM((f, d), BF16),
                        pltpu.SemaphoreType.DMA((3,))],
    )
    return pl.pallas_call(
        _experts_kernel,
        grid_spec=grid_spec,
        out_shape=jax.ShapeDtypeStruct((rows, d), F32),
        compiler_params=_cparams(("arbitrary",)),
        name="experts",
    )(texp, tidx, nexte, used, xs, w1, w3, w2)


def _combine_kernel(pa_ref, pb_ref, ys_ref, x1_ref, info_ref, mod_ref, g_ref, b_ref, o_ref,
                    buf_a, buf_b, sems, *, tm, alpha):
    i = pl.program_id(0)
    last = pl.num_programs(0) - 1

    def issue_row(tile, slot, t):
        _row_copy(ys_ref, pa_ref[tile * tm + t], buf_a.at[slot], t, sems.at[slot]).start()
        _row_copy(ys_ref, pb_ref[tile * tm + t], buf_b.at[slot], t, sems.at[slot]).start(priority=1)

    def combine_rows(slot, rows):
        info = info_ref[rows, :]
        y = info[:, 2:3] * buf_a[slot, rows, :] + info[:, 3:4] * buf_b[slot, rows, :]
        z = alpha * x1_ref[rows, :] + mod_ref[0, 5:6, :] * y
        o_ref[rows, :] = _ln(z) * g_ref[...] + b_ref[...]

    @pl.when(i == 0)
    def _():
        lax.fori_loop(0, tm, lambda t, _: (issue_row(0, 0, t), 0)[1], 0, unroll=8)

    slot = i % 2
    for buf in (buf_a, buf_b):
        pltpu.make_async_copy(ys_ref.at[pl.ds(0, tm)], buf.at[slot], sems.at[slot]).wait()

    chunk = 32

    @pl.when(i < last)
    def _():
        for r in range(0, tm, chunk):
            for t in range(r, r + chunk):
                issue_row(i + 1, 1 - slot, t)
            combine_rows(slot, slice(r, r + chunk))

    @pl.when(i == last)
    def _():
        combine_rows(slot, slice(0, tm))


def _combine(pos_a, pos_b, ys, x1, info, mod3, ln_g, ln_b, seq, alpha):
    t, d = x1.shape
    tm = 256
    per_b = seq // tm
    grid_spec = pltpu.PrefetchScalarGridSpec(
        num_scalar_prefetch=2,
        grid=(t // tm,),
        in_specs=[pl.BlockSpec(memory_space=pl.ANY),
                  pl.BlockSpec((tm, d), lambda i, pa, pb: (i, 0)),
                  pl.BlockSpec((tm, LANES), lambda i, pa, pb: (i, 0)),
                  pl.BlockSpec((1, 6, d), lambda i, pa, pb: (i // per_b, 0, 0)),
                  pl.BlockSpec((1, d), lambda i, pa, pb: (0, 0)),
                  pl.BlockSpec((1, d), lambda i, pa, pb: (0, 0))],
        out_specs=pl.BlockSpec((tm, d), lambda i, pa, pb: (i, 0)),
        scratch_shapes=[pltpu.VMEM((2, tm, d), F32), pltpu.VMEM((2, tm, d), F32), pltpu.SemaphoreType.DMA((2,))],
    )
    return pl.pallas_call(
        functools.partial(_combine_kernel, tm=tm, alpha=alpha),
        grid_spec=grid_spec,
        out_shape=jax.ShapeDtypeStruct((t, d), F32),
        compiler_params=_cparams(("arbitrary",), row_dma=True),
        name="combine",
    )(pos_a, pos_b, ys, x1, info, mod3, ln_g, ln_b)


def _prep_w_in_kernel(w_ref, o_ref):
    r0 = MLA_Q_RANK + MLA_KV_RANK
    r1 = r0 + MLA_ROPE_DIM
    half = MLA_ROPE_DIM // 2
    z = jnp.zeros((LANES - MLA_ROPE_DIM, o_ref.shape[1]), BF16)
    o_ref[:r1, :] = w_ref[:r1, :].astype(BF16)
    o_ref[r1:r0 + LANES, :] = z
    o_ref[r0 + LANES:r0 + LANES + half, :] = w_ref[r0 + half:r1, :].astype(BF16)
    o_ref[r0 + LANES + half:r0 + LANES + MLA_ROPE_DIM, :] = w_ref[r0:r0 + half, :].astype(BF16)
    o_ref[r0 + LANES + MLA_ROPE_DIM:PART_A, :] = z
    o_ref[PART_A:, :] = w_ref[r1:, :].astype(BF16)


def _prep_w_in(w_t):
    n, k = w_t.shape
    tk = 512
    n_out = PART_A + 3 * MOBA_WIDTH
    assert n == MLA_Q_RANK + MLA_KV_RANK + MLA_ROPE_DIM + 3 * MOBA_WIDTH
    return pl.pallas_call(
        _prep_w_in_kernel,
        grid=(k // tk,),
        in_specs=[pl.BlockSpec((n, tk), lambda i: (0, i))],
        out_specs=pl.BlockSpec((n_out, tk), lambda i: (0, i)),
        out_shape=jax.ShapeDtypeStruct((n_out, k), BF16),
        compiler_params=_cparams(("arbitrary",)),
        name="prep_w_in",
    )(w_t)


def _prep_w_uq(w):
    r = w.shape[0]
    w = w.reshape(r, MLA_HEADS, MLA_NOPE_DIM + MLA_ROPE_DIM)
    half = MLA_ROPE_DIM // 2
    nope = w[:, :, :MLA_NOPE_DIM]
    x1 = w[:, :, MLA_NOPE_DIM:MLA_NOPE_DIM + half]
    x2 = w[:, :, MLA_NOPE_DIM + half:]
    z = jnp.zeros((r, MLA_HEADS, LANES - MLA_ROPE_DIM), w.dtype)
    return jnp.concatenate([nope, x1, x2, z, x2, x1, z], axis=2).reshape(r, MLA_HEADS * 3 * LANES).astype(BF16)


def _rope_lanes(seq):
    inv = 1.0 / (ROPE_THETA ** (jnp.arange(0, MLA_ROPE_DIM, 2, dtype=F32) / MLA_ROPE_DIM))
    ang = jnp.arange(seq, dtype=F32)[:, None] * inv[None, :]
    cos, sin = jnp.cos(ang), jnp.sin(ang)
    z = jnp.zeros((seq, LANES - MLA_ROPE_DIM), F32)
    return jnp.concatenate([cos, cos, z], axis=1), jnp.concatenate([-sin, sin, z], axis=1)


def _prep_router(w_rg, b_rg, w_re, b_re):
    d = w_rg.shape[0]
    w = jnp.concatenate([w_rg, w_re], axis=1)
    n = w.shape[1]
    hi = w.astype(BF16)
    lo = (w - hi.astype(F32)).astype(BF16)
    z = jnp.zeros((d, LANES // 2 - n), BF16)
    wr = jnp.concatenate([hi, z, lo, z], axis=1)
    br = jnp.zeros((1, LANES), F32).at[0, :n].set(jnp.concatenate([b_rg, b_re]))
    return wr, br


def _layer(xf, mod3, batch, seq, depth_alpha, w_in, q_norm_g, w_uq, kv_norm_g, w_ukv, w_out, bias_tiles,
           cos_t, sin_t, ln1_g, ln1_b, w_rg, b_rg, w_re, b_re, w1, w3, w2, ln2_g, ln2_b):
    t, d = xf.shape
    part_a, qkv_mo = _in_proj(xf, mod3, _prep_w_in(w_in.T), seq)
    q, k, v = _qkv(part_a, _prep_w_uq(w_uq), w_ukv.astype(BF16), q_norm_g.reshape(1, -1),
                   kv_norm_g.reshape(1, -1), cos_t, sin_t, seq)
    o_mla = _mla_attention(q, k, v, batch, seq)
    q_wide = _moba_select(qkv_mo, batch, seq)
    o_moba = _moba_attention(q_wide, qkv_mo, bias_tiles, batch, seq)
    wo = w_out.astype(BF16)
    wr, br = _prep_router(w_rg, b_rg, w_re, b_re)
    x1, h2, logits = _out_proj(o_mla, o_moba, wo, xf, mod3, ln1_g.reshape(1, d),
                               ln1_b.reshape(1, d), wr, seq, depth_alpha)
    info, counts = _route(logits, br)
    tr = 256
    nt = (2 * t) // tr + MOE_N_EXPERTS
    cnt = counts[0, MOE_GROUPS:MOE_GROUPS + MOE_N_EXPERTS].astype(jnp.int32)
    ntile = (cnt + tr - 1) // tr
    tile_end = jnp.cumsum(ntile)
    tile_start = tile_end - ntile
    used = tile_end[-1]
    start_lanes = jnp.zeros((1, LANES), F32).at[0, MOE_GROUPS:MOE_GROUPS + MOE_N_EXPERTS].set(
        (tile_start * tr).astype(F32))
    pos = _positions(info, start_lanes)
    pos_a, pos_b = pos[0], pos[1]
    tidx = jnp.minimum(jnp.arange(nt, dtype=jnp.int32), used - 1)
    texp = jnp.sum(tidx[:, None] >= tile_end[None, :], axis=1).astype(jnp.int32)
    eids = jnp.arange(MOE_N_EXPERTS, dtype=jnp.int32)
    later = (eids[None, :] > eids[:, None]) & (ntile[None, :] > 0)
    next_nonempty = jnp.where(later.any(axis=1), jnp.argmax(later, axis=1), -1).astype(jnp.int32)
    nexte = jnp.sum(jnp.where(texp[:, None] == eids[None, :], next_nonempty[None, :], 0), axis=1).astype(jnp.int32)
    used1 = used.reshape(1).astype(jnp.int32)
    pad0 = (tile_start * tr + cnt).astype(jnp.int32)
    padn = (ntile * tr - cnt).astype(jnp.int32)
    xs = _dispatch(pos_a, pos_b, pad0, padn, used1, h2, tr, nt)
    ys = _experts(texp, tidx, nexte, used1, xs, w1, w3, w2, tr)
    return _combine(pos_a, pos_b, ys, x1, info, mod3, ln2_g.reshape(1, d), ln2_b.reshape(1, d), seq, depth_alpha)


def kernel(x, c, w_ada, b_ada, w_in, q_norm_g, w_uq, kv_norm_g, w_ukv, w_out, t5_table, ln1_g, ln1_b,
           w_router_group, b_router_group, w_router_expert, b_router_expert, w1, w3, w2, ln2_g, ln2_b):
    batch, seq, d = x.shape
    depth = w_ada.shape[0]
    alpha = (2.0 * depth) ** 0.25
    cos_t, sin_t = _rope_lanes(seq)
    bias_tiles = _t5_tiles(t5_table)
    xf = x.reshape(batch * seq, d)
    for l in range(depth):
        mod3 = _ada_mod(c, w_ada[l], b_ada[l]).reshape(batch, 6, d)
        xf = _layer(xf, mod3, batch, seq, alpha, w_in[l], q_norm_g[l], w_uq[l], kv_norm_g[l], w_ukv[l],
                    w_out[l], bias_tiles, cos_t, sin_t, ln1_g[l], ln1_b[l], w_router_group[l],
                    b_router_group[l], w_router_expert[l], b_router_expert[l], w1[l], w3[l], w2[l],
                    ln2_g[l], ln2_b[l])
    return xf.reshape(batch, seq, d)
```

```python
import functools
import math

import jax
import jax.numpy as jnp
from jax import lax
from jax.experimental import pallas as pl
from jax.experimental.pallas import tpu as pltpu

D_MODEL = 2048
MLA_HEADS = 8
MLA_Q_RANK = 512
MLA_KV_RANK = 256
MLA_NOPE_DIM = 128
MLA_ROPE_DIM = 64
MLA_V_DIM = 128
ROPE_THETA = 10000.0
MOBA_HEADS = 8
MOBA_HEAD_DIM = 128
MOBA_BLOCK = 256
MOBA_TOPK = 3
T5_BUCKETS = 32
T5_MAX_DISTANCE = 128
MOE_GROUPS = 4
MOE_EXPERTS_PER_GROUP = 8
MOE_N_EXPERTS = MOE_GROUPS * MOE_EXPERTS_PER_GROUP
MOE_D_FF = 512
LN_EPS = 1e-5
RMS_EPS = 1e-6
MOBA_WIDTH = MOBA_HEADS * MOBA_HEAD_DIM

LANES = 128
SUBLANES = 8
QK_PAD = 256
V_PAD = 256
PART_A = 1024
NEG = -1e30
VMEM_LIMIT = 56 * 1024 * 1024

F32 = jnp.float32
BF16 = jnp.bfloat16


def _cparams(sem, row_dma=False):
    return pltpu.CompilerParams(dimension_semantics=sem, vmem_limit_bytes=VMEM_LIMIT,
                                disable_bounds_checks=row_dma)


def _ln(x):
    mu = jnp.mean(x, axis=-1, keepdims=True)
    xc = x - mu
    var = jnp.mean(xc * xc, axis=-1, keepdims=True)
    return xc * lax.rsqrt(var + LN_EPS)


def _nt_dot(a, b):
    return lax.dot_general(a, b, (((1,), (1,)), ((), ())), preferred_element_type=F32)


def _ada_kernel(ct_ref, w_ref, b_ref, o_ref, *, batch):
    ct = ct_ref[...]
    ca = ct / (1.0 + jnp.exp(-ct))
    w = w_ref[...]
    rows = [jnp.sum(w * ca[:, b:b + 1], axis=0, keepdims=True) for b in range(batch)]
    o_ref[...] = jnp.concatenate(rows, axis=0) + b_ref[...]


def _ada_mod(c, w_ada, b_ada):
    batch, d = c.shape
    n = w_ada.shape[1]
    tn = 1024
    ct = jnp.zeros((d, LANES), F32).at[:, :batch].set(c.T)
    return pl.pallas_call(
        functools.partial(_ada_kernel, batch=batch),
        grid=(n // tn,),
        in_specs=[pl.BlockSpec((d, LANES), lambda j: (0, 0)),
                  pl.BlockSpec((d, tn), lambda j: (0, j)),
                  pl.BlockSpec((1, tn), lambda j: (0, j))],
        out_specs=pl.BlockSpec((batch, tn), lambda j: (0, j)),
        out_shape=jax.ShapeDtypeStruct((batch, n), F32),
        compiler_params=_cparams(("arbitrary",)),
        name="ada_mod",
    )(ct, w_ada, b_ada.reshape(1, n))


def _inproj_kernel(x_ref, mod_ref, w_ref, a_ref, b_ref, h_scr):
    j = pl.program_id(1)

    @pl.when(j == 0)
    def _():
        rows = 256
        for r in range(0, x_ref.shape[0], rows):
            h = _ln(x_ref[r:r + rows, :]) * (1.0 + mod_ref[0, 1:2, :]) + mod_ref[0, 0:1, :]
            h_scr[r:r + rows, :] = h.astype(BF16)
        a_ref[...] = _nt_dot(h_scr[...], w_ref[...])

    @pl.when(j > 0)
    def _():
        b_ref[...] = _nt_dot(h_scr[...], w_ref[...]).astype(BF16)


def _in_proj(xf, mod3, w4, seq):
    t, d = xf.shape
    n = w4.shape[0]
    tm, tn = 1024, PART_A
    per_b = seq // tm
    return pl.pallas_call(
        _inproj_kernel,
        grid=(t // tm, n // tn),
        in_specs=[pl.BlockSpec((tm, d), lambda i, j: (i, 0)),
                  pl.BlockSpec((1, 6, d), lambda i, j: (i // per_b, 0, 0)),
                  pl.BlockSpec((tn, d), lambda i, j: (j, 0))],
        out_specs=[pl.BlockSpec((tm, tn), lambda i, j: (i, 0)),
                   pl.BlockSpec((tm, tn), lambda i, j: (i, jnp.maximum(j - 1, 0)))],
        out_shape=[jax.ShapeDtypeStruct((t, PART_A), F32),
                   jax.ShapeDtypeStruct((t, n - PART_A), BF16)],
        scratch_shapes=[pltpu.VMEM((tm, d), BF16)],
        compiler_params=_cparams(("arbitrary", "arbitrary")),
        name="in_proj",
    )(xf, mod3, w4)


def _qkv_kernel(a_ref, wq_ref, wkv_ref, gq_ref, gkv_ref, cos_ref, sin_ref, q_ref, k_ref, v_ref):
    a = a_ref[...]
    cq = a[:, :MLA_Q_RANK]
    ckv = a[:, MLA_Q_RANK:MLA_Q_RANK + MLA_KV_RANK]
    kr = a[:, 768:896]
    krs = a[:, 896:1024]
    cqn = (cq * lax.rsqrt(jnp.mean(cq * cq, axis=-1, keepdims=True) + RMS_EPS) * gq_ref[...]).astype(BF16)
    ckvn = (ckv * lax.rsqrt(jnp.mean(ckv * ckv, axis=-1, keepdims=True) + RMS_EPS) * gkv_ref[...]).astype(BF16)
    cos = cos_ref[...]
    sin = sin_ref[...]
    krr = (kr * cos + krs * sin).astype(BF16)
    for h in range(MLA_HEADS):
        c0 = h * QK_PAD
        q2 = jnp.dot(cqn, wq_ref[:, c0:c0 + QK_PAD], preferred_element_type=F32)
        kv = jnp.dot(ckvn, wkv_ref[:, c0:c0 + QK_PAD], preferred_element_type=F32)
        rope = q2[:, LANES:]
        q_ref[:, c0:c0 + LANES] = q2[:, :LANES].astype(BF16)
        q_ref[:, c0 + LANES:c0 + QK_PAD] = (rope * cos + pltpu.roll(rope, LANES // 2, 1) * sin).astype(BF16)
        k_ref[:, c0:c0 + LANES] = kv[:, :LANES].astype(BF16)
        k_ref[:, c0 + LANES:c0 + QK_PAD] = krr
        v_ref[:, h * LANES:(h + 1) * LANES] = kv[:, LANES:].astype(BF16)


def _qkv(part_a, wq3, wkv, gq, gkv, cos_t, sin_t, seq):
    t = part_a.shape[0]
    tm = 512
    per_b = seq // tm
    hq = MLA_HEADS * QK_PAD
    return pl.pallas_call(
        _qkv_kernel,
        grid=(t // tm,),
        in_specs=[pl.BlockSpec((tm, PART_A), lambda i: (i, 0)),
                  pl.BlockSpec(wq3.shape, lambda i: (0, 0)),
                  pl.BlockSpec(wkv.shape, lambda i: (0, 0)),
                  pl.BlockSpec((1, MLA_Q_RANK), lambda i: (0, 0)),
                  pl.BlockSpec((1, MLA_KV_RANK), lambda i: (0, 0)),
                  pl.BlockSpec((tm, LANES), lambda i: (i % per_b, 0)),
                  pl.BlockSpec((tm, LANES), lambda i: (i % per_b, 0))],
        out_specs=[pl.BlockSpec((tm, hq), lambda i: (i, 0)),
                   pl.BlockSpec((tm, hq), lambda i: (i, 0)),
                   pl.BlockSpec((tm, MLA_HEADS * MLA_V_DIM), lambda i: (i, 0))],
        out_shape=[jax.ShapeDtypeStruct((t, hq), BF16),
                   jax.ShapeDtypeStruct((t, hq), BF16),
                   jax.ShapeDtypeStruct((t, MLA_HEADS * MLA_V_DIM), BF16)],
        compiler_params=_cparams(("arbitrary",)),
        name="qkv",
    )(part_a, wq3, wkv, gq, gkv, cos_t, sin_t)


ATTN_TILE = 512
ATTN_CHAINS = 8


def _attn_kernel(*refs, c, masked):
    nch = ATTN_CHAINS
    scratch = refs[-4 * nch:]
    tile = ATTN_TILE
    i = pl.program_id(2)
    if masked:
        q_ref, kin_ref, vin_ref, bias_ref, o_ref, v_ref, k_ref = refs[:7]
    else:
        q_ref, k_ref, vin_ref, o_ref, v_ref = refs[:5]

    @pl.when((pl.program_id(0) == 0) & (pl.program_id(1) == 0) & (i == 0))
    def _():
        seq = vin_ref.shape[0]
        v_ref[:, LANES:] = jnp.ones((seq, V_PAD - LANES), BF16)
        if masked:
            lane = lax.broadcasted_iota(jnp.int32, (seq, LANES), 1)
            own = jnp.right_shift(lax.broadcasted_iota(jnp.int32, (seq, LANES), 0), MOBA_BLOCK.bit_length() - 1)
            k_ref[:, LANES:] = jnp.where(lane == own, 1.0, 0.0).astype(BF16)

    @pl.when(i == 0)
    def _():
        v_ref[:, :LANES] = vin_ref[...]
        if masked:
            k_ref[:, :LANES] = kin_ref[...]

    chains = [dict(rows=slice(n * tile, (n + 1) * tile), s=scratch[4 * n:4 * n + 2], m=scratch[4 * n + 2],
                   acc=scratch[4 * n + 3]) for n in range(nch)]

    half = tile // 2

    def put_scores(chain, slot, t, diag=False):
        k0 = pl.multiple_of(t * tile, tile)
        q0 = chain["rows"].start
        buf = chain["s"][slot]
        if diag:
            buf[:half, :half] = _nt_dot(q_ref[q0:q0 + half, :], k_ref[pl.ds(k0, half), :])
            buf[half:, :] = _nt_dot(q_ref[q0 + half:q0 + tile, :], k_ref[pl.ds(k0, tile), :])
        else:
            buf[...] = _nt_dot(q_ref[q0:q0 + tile, :], k_ref[pl.ds(k0, tile), :])

    def softmax_update(chain, s, t, rows, keys):
        m_scr, acc_scr = chain["m"], chain["acc"]
        m_old = m_scr[rows, :]
        m_new = jnp.maximum(m_old, jnp.broadcast_to(jnp.max(s, axis=-1, keepdims=True), m_old.shape))
        alpha = jnp.exp2((m_old - m_new) * c)
        p = jnp.concatenate([jnp.exp2(((s[:, j * LANES:(j + 1) * LANES] - m_new) * c).astype(BF16))
                             for j in range(keys // LANES)], axis=1)
        v = v_ref[pl.ds(pl.multiple_of(t * tile, tile), keys), :]
        pv = jnp.dot(p, v, preferred_element_type=F32)
        acc_scr[rows, :] = jnp.concatenate([alpha, alpha], axis=1) * acc_scr[rows, :] + pv
        m_scr[rows, :] = m_new

    def tile_step(chain, slot, t, kind, prefetch=True, next_diag=False):
        if prefetch:
            put_scores(chain, 1 - slot, t + 1, diag=next_diag)
        buf = chain["s"][slot]
        if kind != "diag":
            s = buf[...]
            if masked and kind == "prev":
                s = s + bias_ref[0, 1]
            softmax_update(chain, s, t, slice(0, tile), tile)
            return
        for r0, keys in ((0, half), (half, tile)):
            rows = slice(r0, r0 + half)
            s = buf[rows, :keys]
            if masked:
                s = s + bias_ref[0, 0, rows, :keys]
            else:
                row = lax.broadcasted_iota(jnp.int32, (half, keys), 0) + r0
                col = lax.broadcasted_iota(jnp.int32, (half, keys), 1)
                s = jnp.where(row >= col, s, NEG)
            softmax_update(chain, s, t, rows, keys)

    for chain in chains:
        chain["m"][...] = jnp.full(chain["m"].shape, NEG, F32)
        chain["acc"][...] = jnp.zeros(chain["acc"].shape, F32)
        put_scores(chain, 0, 0)

    def far_pair(j, _):
        for slot in range(2):
            for chain in chains:
                tile_step(chain, slot, 2 * j + slot, "far")
        return 0

    first = nch * i
    if masked:
        lax.fori_loop(0, jnp.maximum(first // 2 - 1, 0), far_pair, 0)

        @pl.when(i > 0)
        def _():
            for chain in chains:
                tile_step(chain, 0, first - 2, "far")
            for n, chain in enumerate(chains):
                tile_step(chain, 1, first - 1, "prev" if n == 0 else "far", next_diag=(n == 0))
    else:
        lax.fori_loop(0, first // 2, far_pair, 0)

    for k in range(nch):
        for n, chain in enumerate(chains):
            if n < k:
                continue
            kind = "diag" if n == k else ("prev" if masked and n == k + 1 else "far")
            tile_step(chain, k % 2, first + k, kind, prefetch=(n != k), next_diag=(n == k + 1))

    for chain in chains:
        acc = chain["acc"]
        o_ref[chain["rows"], :] = (acc[:, :LANES] / acc[:, LANES:]).astype(o_ref.dtype)


def _attn_scratch(seq, masked):
    tile = ATTN_TILE
    wide = [pltpu.VMEM((seq, V_PAD), BF16)] + ([pltpu.VMEM((seq, QK_PAD), BF16)] if masked else [])
    per_chain = [pltpu.VMEM((tile, tile), F32), pltpu.VMEM((tile, tile), F32),
                 pltpu.VMEM((tile, LANES), F32), pltpu.VMEM((tile, V_PAD), F32)]
    return wide + per_chain * ATTN_CHAINS


def _mla_attention(q, k, v, batch, seq):
    tile = ATTN_CHAINS * ATTN_TILE
    nq = seq // tile
    c = math.log2(math.e) / math.sqrt(MLA_NOPE_DIM + MLA_ROPE_DIM)
    return pl.pallas_call(
        functools.partial(_attn_kernel, c=c, masked=False),
        grid=(batch, MLA_HEADS, nq),
        in_specs=[pl.BlockSpec((tile, QK_PAD), lambda b, h, i: (b * nq + i, h)),
                  pl.BlockSpec((seq, QK_PAD), lambda b, h, i: (b, h)),
                  pl.BlockSpec((seq, MLA_V_DIM), lambda b, h, i: (b, h))],
        out_specs=pl.BlockSpec((tile, MLA_V_DIM), lambda b, h, i: (b * nq + i, h)),
        out_shape=jax.ShapeDtypeStruct((batch * seq, MLA_HEADS * MLA_V_DIM), BF16),
        scratch_shapes=_attn_scratch(seq, False),
        compiler_params=_cparams(("arbitrary", "arbitrary", "arbitrary")),
        name="mla_attn",
    )(q, k, v)


def _moba_select_kernel(q_ref, k_ref, qa_ref, *, seq, nb):
    kf = k_ref[...].astype(F32)
    km = jnp.sum(kf.reshape(nb, MOBA_BLOCK, MOBA_HEAD_DIM), axis=1) * (1.0 / MOBA_BLOCK)
    km_hi = km.astype(BF16)
    km_lo = (km - km_hi.astype(F32)).astype(BF16)
    q = q_ref[...]
    gate = _nt_dot(km_hi, q) + _nt_dot(km_lo, q)
    shift = MOBA_BLOCK.bit_length() - 1
    blk = lax.broadcasted_iota(jnp.int32, (nb, seq), 0)
    qblk = jnp.right_shift(lax.broadcasted_iota(jnp.int32, (nb, seq), 1), shift)
    g = jnp.where(blk < qblk, gate, NEG)
    visible = blk == qblk
    for _ in range(MOBA_TOPK):
        mx = jnp.max(g, axis=0, keepdims=True)
        first = jnp.min(jnp.where(g == mx, blk, nb), axis=0, keepdims=True)
        pick = (blk == first) & (mx > 0.5 * NEG)
        visible = visible | pick
        g = jnp.where(pick, NEG, g)
    mask_t = jnp.concatenate([jnp.where(visible, 0.0, NEG), jnp.zeros((LANES - nb, seq), F32)], axis=0)
    qa_ref[:, :MOBA_HEAD_DIM] = q
    qa_ref[:, MOBA_HEAD_DIM:] = mask_t.T.astype(BF16)


def _moba_select(qkv_mo, batch, seq):
    nb = seq // MOBA_BLOCK
    assert MOBA_HEAD_DIM == LANES and nb <= QK_PAD - MOBA_HEAD_DIM
    return pl.pallas_call(
        functools.partial(_moba_select_kernel, seq=seq, nb=nb),
        grid=(batch, MOBA_HEADS),
        in_specs=[pl.BlockSpec((seq, MOBA_HEAD_DIM), lambda b, h: (b, h)),
                  pl.BlockSpec((seq, MOBA_HEAD_DIM), lambda b, h: (b, MOBA_HEADS + h))],
        out_specs=pl.BlockSpec((seq, QK_PAD), lambda b, h: (b, h)),
        out_shape=jax.ShapeDtypeStruct((batch * seq, MOBA_HEADS * QK_PAD), BF16),
        compiler_params=_cparams(("arbitrary", "arbitrary")),
        name="moba_select",
    )(qkv_mo, qkv_mo)


def _t5_kernel(tab_ref, o_ref, *, inv_scale):
    h = pl.program_id(0)
    r = lax.broadcasted_iota(jnp.int32, (LANES, LANES), 0)
    c = lax.broadcasted_iota(jnp.int32, (LANES, LANES), 1)
    max_exact = T5_BUCKETS // 2
    far = tab_ref[T5_BUCKETS - 1, h]

    def block(offset):
        rel = offset + r - c
        n = jnp.maximum(rel, 0)
        nf = jnp.maximum(n, 1).astype(F32)
        large = max_exact + (jnp.log(nf / max_exact) / math.log(T5_MAX_DISTANCE / max_exact)
                             * (T5_BUCKETS - max_exact)).astype(jnp.int32)
        large = jnp.minimum(large, T5_BUCKETS - 1)
        bucket = jnp.where(n < max_exact, n, large)
        bias = jnp.zeros((LANES, LANES), F32)
        for j in range(T5_BUCKETS):
            bias = jnp.where(bucket == j, tab_ref[j, h], bias)
        return jnp.where(rel >= 0, (bias - far) * inv_scale, NEG)

    near = {0: block(0), 1: block(LANES)}
    nblk = ATTN_TILE // LANES
    for d in range(2):
        for i in range(nblk):
            for j in range(nblk):
                k = d * nblk + i - j
                if k < 0:
                    val = jnp.full((LANES, LANES), NEG, F32)
                else:
                    val = near.get(k, jnp.zeros((LANES, LANES), F32))
                o_ref[0, d, i * LANES:(i + 1) * LANES, j * LANES:(j + 1) * LANES] = val


def _t5_tiles(t5_table):
    assert LANES >= T5_MAX_DISTANCE
    tile = ATTN_TILE
    return pl.pallas_call(
        functools.partial(_t5_kernel, inv_scale=math.sqrt(MOBA_HEAD_DIM)),
        grid=(MOBA_HEADS,),
        in_specs=[pl.BlockSpec(memory_space=pltpu.SMEM)],
        out_specs=pl.BlockSpec((1, 2, tile, tile), lambda h: (h, 0, 0, 0)),
        out_shape=jax.ShapeDtypeStruct((MOBA_HEADS, 2, tile, tile), F32),
        compiler_params=_cparams(("arbitrary",)),
        name="t5_tiles",
    )(t5_table)


def _moba_attention(q_wide, qkv_mo, bias, batch, seq):
    tile = ATTN_CHAINS * ATTN_TILE
    nq = seq // tile
    c = math.log2(math.e) / math.sqrt(MOBA_HEAD_DIM)
    dh = MOBA_HEAD_DIM
    return pl.pallas_call(
        functools.partial(_attn_kernel, c=c, masked=True),
        grid=(batch, MOBA_HEADS, nq),
        in_specs=[pl.BlockSpec((tile, QK_PAD), lambda b, h, i: (b * nq + i, h)),
                  pl.BlockSpec((seq, dh), lambda b, h, i: (b, MOBA_HEADS + h)),
                  pl.BlockSpec((seq, dh), lambda b, h, i: (b, 2 * MOBA_HEADS + h)),
                  pl.BlockSpec((1, 2, ATTN_TILE, ATTN_TILE), lambda b, h, i: (h, 0, 0, 0))],
        out_specs=pl.BlockSpec((tile, dh), lambda b, h, i: (b * nq + i, h)),
        out_shape=jax.ShapeDtypeStruct((batch * seq, MOBA_WIDTH), BF16),
        scratch_shapes=_attn_scratch(seq, True),
        compiler_params=_cparams(("arbitrary", "arbitrary", "arbitrary")),
        name="moba_attn",
    )(q_wide, qkv_mo, qkv_mo, bias)


def _outproj_kernel(oa_ref, ob_ref, wa_ref, wb_ref, x_ref, mod_ref, g_ref, b_ref, wr_ref,
                    x1_ref, h2_ref, lg_ref, y_a, y_b, *, alpha, n_tiles):
    i = pl.program_id(0)
    tm, d = y_a.shape
    chunks = 4
    cn, cr = d // chunks, tm // chunks

    def matmul_into(y_ref, c):
        cols = slice(c * cn, (c + 1) * cn)
        y_ref[:, cols] = (jnp.dot(oa_ref[...], wa_ref[:, cols], preferred_element_type=F32)
                          + jnp.dot(ob_ref[...], wb_ref[:, cols], preferred_element_type=F32))

    def epilogue(y_ref, c):
        rows = slice(c * cr, (c + 1) * cr)
        z = alpha * x_ref[rows, :] + mod_ref[0, 2:3, :] * y_ref[rows, :]
        x1 = _ln(z) * g_ref[...] + b_ref[...]
        x1_ref[rows, :] = x1
        h2 = _ln(x1) * (1.0 + mod_ref[0, 4:5, :]) + mod_ref[0, 3:4, :]
        h_hi = h2.astype(BF16)
        h_lo = (h2 - h_hi.astype(F32)).astype(BF16)
        h_bits = pltpu.bitcast(h_hi.astype(F32), jnp.uint32)
        h2_ref[rows, :] = h_bits[:, :d // 2] | jnp.right_shift(h_bits[:, d // 2:], 16)
        zz = (jnp.dot(h_hi, wr_ref[...], preferred_element_type=F32)
              + jnp.dot(h_lo, wr_ref[...], preferred_element_type=F32))
        lg_ref[rows, :] = zz + pltpu.roll(zz, LANES // 2, 1)

    def step(y_new, y_old):
        for c in range(chunks):
            if y_new is not None:
                matmul_into(y_new, c)
            if y_old is not None:
                epilogue(y_old, c)

    inner = (i > 0) & (i < n_tiles)
    pl.when(i == 0)(lambda: step(y_a, None))
    pl.when(inner & (i % 2 == 0))(lambda: step(y_a, y_b))
    pl.when(inner & (i % 2 == 1))(lambda: step(y_b, y_a))
    pl.when(i == n_tiles)(lambda: step(None, y_b if n_tiles % 2 == 0 else y_a))


def _out_proj(o_mla, o_moba, wo, xf, mod3, ln_g, ln_b, wr, seq, alpha):
    t, d = xf.shape
    tm = 512
    n = t // tm
    per_b = seq // tm
    ka, kb = o_mla.shape[1], o_moba.shape[1]
    assert ka == kb and wo.shape[0] == ka + kb
    once = pl.Buffered(1)

    def cur(i):
        return (jnp.minimum(i, n - 1), 0)

    def lag(i):
        return (jnp.maximum(i - 1, 0), 0)

    return pl.pallas_call(
        functools.partial(_outproj_kernel, alpha=alpha, n_tiles=n),
        grid=(n + 1,),
        in_specs=[pl.BlockSpec((tm, ka), cur),
                  pl.BlockSpec((tm, kb), cur),
                  pl.BlockSpec((ka, d), lambda i: (0, 0), pipeline_mode=once),
                  pl.BlockSpec((kb, d), lambda i: (1, 0), pipeline_mode=once),
                  pl.BlockSpec((tm, d), lag),
                  pl.BlockSpec((1, 6, d), lambda i: (jnp.maximum(i - 1, 0) // per_b, 0, 0)),
                  pl.BlockSpec((1, d), lambda i: (0, 0)),
                  pl.BlockSpec((1, d), lambda i: (0, 0)),
                  pl.BlockSpec((d, LANES), lambda i: (0, 0), pipeline_mode=once)],
        out_specs=[pl.BlockSpec((tm, d), lag),
                   pl.BlockSpec((tm, d // 2), lag),
                   pl.BlockSpec((tm, LANES), lag)],
        out_shape=[jax.ShapeDtypeStruct((t, d), F32),
                   jax.ShapeDtypeStruct((t, d // 2), jnp.uint32),
                   jax.ShapeDtypeStruct((t, LANES), F32)],
        scratch_shapes=[pltpu.VMEM((tm, d), F32), pltpu.VMEM((tm, d), F32)],
        compiler_params=_cparams(("arbitrary",)),
        name="out_proj",
    )(o_mla, o_moba, wo, wo, xf, mod3, ln_g, ln_b, wr)


def _route_kernel(lg_ref, br_ref, info_ref, cnt_ref, run_scr, *, tm):
    i = pl.program_id(0)

    @pl.when(i == 0)
    def _():
        run_scr[...] = jnp.zeros_like(run_scr)

    lg = lg_ref[...] + br_ref[...]
    lane = lax.broadcasted_iota(jnp.int32, (tm, LANES), 1)
    e_lo, e_hi = MOE_GROUPS, MOE_GROUPS + MOE_N_EXPERTS
    is_g = lane < e_lo
    gl = jnp.where(is_g, lg, NEG)
    gmax = jnp.max(gl, axis=-1, keepdims=True)
    gidx = jnp.min(jnp.where(gl == gmax, lane, LANES), axis=-1, keepdims=True)
    g_p = 1.0 / jnp.sum(jnp.where(is_g, jnp.exp(gl - gmax), 0.0), axis=-1, keepdims=True)
    grp_of_lane = jnp.right_shift(lane - e_lo, MOE_EXPERTS_PER_GROUP.bit_length() - 1)
    in_grp = (lane >= e_lo) & (lane < e_hi) & (grp_of_lane == gidx)
    el = jnp.where(in_grp, lg, NEG)
    m1 = jnp.max(el, axis=-1, keepdims=True)
    l1 = jnp.min(jnp.where(el == m1, lane, LANES), axis=-1, keepdims=True)
    el2 = jnp.where(lane == l1, NEG, el)
    m2 = jnp.max(el2, axis=-1, keepdims=True)
    l2 = jnp.min(jnp.where(el2 == m2, lane, LANES), axis=-1, keepdims=True)
    zsum = jnp.sum(jnp.where(in_grp, jnp.exp(el - m1), 0.0), axis=-1, keepdims=True)
    p1 = 1.0 / zsum
    p2 = jnp.exp(m2 - m1) / zsum
    wa = g_p * (p1 / (p1 + p2))
    wb = g_p * (p2 / (p1 + p2))
    hot_a = lane == l1
    hot_b = lane == l2
    onehot = jnp.where(hot_a | hot_b, 1.0, 0.0)
    r = lax.broadcasted_iota(jnp.int32, (tm, tm), 0)
    c = lax.broadcasted_iota(jnp.int32, (tm, tm), 1)
    lower = jnp.where(c < r, 1.0, 0.0).astype(BF16)
    before = jnp.dot(lower, onehot.astype(BF16), preferred_element_type=F32) + run_scr[...]
    rank_a = jnp.sum(jnp.where(hot_a, before, 0.0), axis=-1, keepdims=True)
    rank_b = jnp.sum(jnp.where(hot_b, before, 0.0), axis=-1, keepdims=True)
    run_scr[...] += jnp.sum(onehot, axis=0, keepdims=True)
    info = jnp.zeros((tm, LANES), F32)
    for k, val in enumerate([(l1 - e_lo).astype(F32), (l2 - e_lo).astype(F32), wa, wb, rank_a, rank_b]):
        info = jnp.where(lane == k, val, info)
    info_ref[...] = info
    cnt_ref[...] = run_scr[...]


def _route(logits, br):
    t = logits.shape[0]
    tm = 512
    return pl.pallas_call(
        functools.partial(_route_kernel, tm=tm),
        grid=(t // tm,),
        in_specs=[pl.BlockSpec((tm, LANES), lambda i: (i, 0)),
                  pl.BlockSpec((1, LANES), lambda i: (0, 0))],
        out_specs=[pl.BlockSpec((tm, LANES), lambda i: (i, 0)),
                   pl.BlockSpec((1, LANES), lambda i: (0, 0))],
        out_shape=[jax.ShapeDtypeStruct((t, LANES), F32),
                   jax.ShapeDtypeStruct((1, LANES), F32)],
        scratch_shapes=[pltpu.VMEM((1, LANES), F32)],
        compiler_params=_cparams(("arbitrary",)),
        name="route",
    )(logits, br)


def _pos_kernel(info_ref, start_ref, pos_ref):
    info = info_ref[...]
    tm = info.shape[0]
    lane = lax.broadcasted_iota(jnp.int32, (tm, LANES), 1)
    start = start_ref[...]
    cols = []
    for k in range(2):
        e = jnp.sum(jnp.where(lane == k, info, 0.0), axis=-1, keepdims=True).astype(jnp.int32)
        rank = jnp.sum(jnp.where(lane == 4 + k, info, 0.0), axis=-1, keepdims=True)
        base = jnp.sum(jnp.where(lane == e + MOE_GROUPS, start, 0.0), axis=-1, keepdims=True)
        cols.append(base + rank)
    wide = jnp.where(lane == 0, cols[0], jnp.where(lane == 1, cols[1], 0.0))
    pos_ref[...] = wide.T[:pos_ref.shape[0], :].astype(jnp.int32)


def _positions(info, start_lanes):
    t = info.shape[0]
    tm = 1024
    return pl.pallas_call(
        _pos_kernel,
        grid=(t // tm,),
        in_specs=[pl.BlockSpec((tm, LANES), lambda i: (i, 0)),
                  pl.BlockSpec((1, LANES), lambda i: (0, 0))],
        out_specs=pl.BlockSpec((8, tm), lambda i: (0, i)),
        out_shape=jax.ShapeDtypeStruct((8, t), jnp.int32),
        compiler_params=_cparams(("arbitrary",)),
        name="positions",
    )(info, start_lanes)


def _row_copy(src_ref, src_row, dst_ref, dst_row, sem):
    return pltpu.make_async_copy(src_ref.at[pl.ds(src_row, 1)], dst_ref.at[pl.ds(dst_row, 1)], sem)


def _dispatch_kernel(pa_ref, pb_ref, pad0_ref, padn_ref, used_ref, h_ref, xs_ref, ztile, sem, zsem,
                     *, tm, tr, n_tiles):
    i = pl.program_id(0)
    base = i * tm

    def zero_fill(act):
        def whole_tile(j, _):
            act(pltpu.make_async_copy(ztile, xs_ref.at[pl.ds(pl.multiple_of(j * tr, tr), tr)], zsem))
            return 0

        lax.fori_loop(used_ref[0], n_tiles, whole_tile, 0)

        def expert_pad(e, _):
            n, start = padn_ref[e], pad0_ref[e]
            head = jnp.minimum(jnp.bitwise_and(-start, SUBLANES - 1), n)
            for k in range(SUBLANES - 1):
                pl.when(k < head)(functools.partial(act, _row_copy(ztile, 0, xs_ref, start + k, zsem)))
            off = start + head
            groups = jnp.right_shift(n - head, SUBLANES.bit_length() - 1)
            for bit in reversed(range((tr // SUBLANES - 1).bit_length())):
                size = SUBLANES << bit
                take = jnp.bitwise_and(jnp.right_shift(groups, bit), 1)
                dst = xs_ref.at[pl.ds(pl.multiple_of(off, SUBLANES), size)]
                pl.when(take == 1)(functools.partial(act, pltpu.make_async_copy(ztile.at[pl.ds(0, size)], dst, zsem)))
                off = off + take * size
            return 0

        lax.fori_loop(0, MOE_N_EXPERTS, expert_pad, 0)

    @pl.when(i == 0)
    def _():
        ztile[...] = jnp.zeros(ztile.shape, ztile.dtype)
        zero_fill(lambda cp: cp.start())

    for t in range(tm):
        _row_copy(h_ref, t, xs_ref, pa_ref[base + t], sem).start()
        _row_copy(h_ref, t, xs_ref, pb_ref[base + t], sem).start(priority=1)
    for _ in range(2):
        pltpu.make_async_copy(h_ref, xs_ref.at[pl.ds(0, tm)], sem).wait()

    @pl.when(i == pl.num_programs(0) - 1)
    def _():
        zero_fill(lambda cp: cp.wait())


def _dispatch(pos_a, pos_b, pad0, padn, used, h2, tr, n_tiles):
    t, d = h2.shape
    tm = 256
    grid_spec = pltpu.PrefetchScalarGridSpec(
        num_scalar_prefetch=5,
        grid=(t // tm,),
        in_specs=[pl.BlockSpec((tm, d), lambda i, *_: (i, 0))],
        out_specs=pl.BlockSpec(memory_space=pl.ANY),
        scratch_shapes=[pltpu.VMEM((tr, d), h2.dtype), pltpu.SemaphoreType.DMA(()), pltpu.SemaphoreType.DMA(())],
    )
    return pl.pallas_call(
        functools.partial(_dispatch_kernel, tm=tm, tr=tr, n_tiles=n_tiles),
        grid_spec=grid_spec,
        out_shape=jax.ShapeDtypeStruct((n_tiles * tr, d), h2.dtype),
        compiler_params=_cparams(("arbitrary",), row_dma=True),
        name="dispatch",
    )(pos_a, pos_b, pad0, padn, used, h2)


def _experts_kernel(texp_ref, tidx_ref, nexte_ref, used_ref, x_ref, w1_hbm, w3_hbm, w2_hbm, o_ref,
                    w1_f32, w3_f32, w2_f32, w1_scr, w3_scr, w2_scr, sems):
    i = pl.program_id(0)
    prev = texp_ref[jnp.maximum(i - 1, 0)]

    def weight_copies(e):
        return [pltpu.make_async_copy(w1_hbm.at[e], w1_f32, sems.at[0]),
                pltpu.make_async_copy(w3_hbm.at[e], w3_f32, sems.at[1]),
                pltpu.make_async_copy(w2_hbm.at[e], w2_f32, sems.at[2])]

    @pl.when(i == 0)
    def _():
        for cp in weight_copies(texp_ref[0]):
            cp.start()

    @pl.when((i == 0) | (texp_ref[i] != prev))
    def _():
        for cp in weight_copies(texp_ref[i]):
            cp.wait()
        w1_scr[...] = w1_f32[...].astype(BF16)
        w3_scr[...] = w3_f32[...].astype(BF16)
        w2_scr[...] = w2_f32[...].astype(BF16)

        @pl.when(nexte_ref[i] >= 0)
        def _():
            for cp in weight_copies(nexte_ref[i]):
                cp.start(priority=1)

    @pl.when(i < used_ref[0])
    def _():
        bits = x_ref[...]
        x = jnp.concatenate(
            [pltpu.bitcast(jnp.bitwise_and(bits, jnp.uint32(0xFFFF0000)), F32).astype(BF16),
             pltpu.bitcast(jnp.left_shift(bits, 16), F32).astype(BF16)], axis=1)
        a = jnp.dot(x, w1_scr[...], preferred_element_type=F32)
        b = jnp.dot(x, w3_scr[...], preferred_element_type=F32)
        hid = (a / (1.0 + jnp.exp(-a))) * b
        o_ref[...] = jnp.dot(hid.astype(BF16), w2_scr[...], preferred_element_type=F32)

    @pl.when(i >= used_ref[0])
    def _():
        o_ref[...] = jnp.zeros_like(o_ref)


def _experts(texp, tidx, nexte, used, xs, w1, w3, w2, tr):
    rows = xs.shape[0]
    nt = rows // tr
    d, f = w1.shape[1:]
    assert xs.shape[1] * 2 == d and xs.dtype == jnp.uint32
    grid_spec = pltpu.PrefetchScalarGridSpec(
        num_scalar_prefetch=4,
        grid=(nt,),
        in_specs=[pl.BlockSpec((tr, d // 2), lambda i, te, ti, ne, u: (ti[i], 0)),
                  pl.BlockSpec(memory_space=pl.ANY),
                  pl.BlockSpec(memory_space=pl.ANY),
                  pl.BlockSpec(memory_space=pl.ANY)],
        out_specs=pl.BlockSpec((tr, d), lambda i, te, ti, ne, u: (i, 0)),
        scratch_shapes=[pltpu.VMEM((d, f), F32), pltpu.VMEM((d, f), F32), pltpu.VMEM((f, d), F32),
                        pltpu.VMEM((d, f), BF16), pltpu.VMEM((d, f), BF16), pltpu.VMEM((f, d), BF16),
                        pltpu.SemaphoreType.DMA((3,))],
    )
    return pl.pallas_call(
        _experts_kernel,
        grid_spec=grid_spec,
        out_shape=jax.ShapeDtypeStruct((rows, d), F32),
        compiler_params=_cparams(("arbitrary",)),
        name="experts",
    )(texp, tidx, nexte, used, xs, w1, w3, w2)


def _combine_kernel(pa_ref, pb_ref, ys_ref, x1_ref, info_ref, mod_ref, g_ref, b_ref, o_ref,
                    buf_a, buf_b, sems, *, tm, alpha):
    i = pl.program_id(0)
    last = pl.num_programs(0) - 1

    def issue_row(tile, slot, t):
        _row_copy(ys_ref, pa_ref[tile * tm + t], buf_a.at[slot], t, sems.at[slot]).start()
        _row_copy(ys_ref, pb_ref[tile * tm + t], buf_b.at[slot], t, sems.at[slot]).start(priority=1)

    def combine_rows(slot, rows):
        info = info_ref[rows, :]
        y = info[:, 2:3] * buf_a[slot, rows, :] + info[:, 3:4] * buf_b[slot, rows, :]
        z = alpha * x1_ref[rows, :] + mod_ref[0, 5:6, :] * y
        o_ref[rows, :] = _ln(z) * g_ref[...] + b_ref[...]

    @pl.when(i == 0)
    def _():
        lax.fori_loop(0, tm, lambda t, _: (issue_row(0, 0, t), 0)[1], 0, unroll=8)

    slot = i % 2
    for buf in (buf_a, buf_b):
        pltpu.make_async_copy(ys_ref.at[pl.ds(0, tm)], buf.at[slot], sems.at[slot]).wait()

    chunk = 32

    @pl.when(i < last)
    def _():
        for r in range(0, tm, chunk):
            for t in range(r, r + chunk):
                issue_row(i + 1, 1 - slot, t)
            combine_rows(slot, slice(r, r + chunk))

    @pl.when(i == last)
    def _():
        combine_rows(slot, slice(0, tm))


def _combine(pos_a, pos_b, ys, x1, info, mod3, ln_g, ln_b, seq, alpha):
    t, d = x1.shape
    tm = 256
    per_b = seq // tm
    grid_spec = pltpu.PrefetchScalarGridSpec(
        num_scalar_prefetch=2,
        grid=(t // tm,),
        in_specs=[pl.BlockSpec(memory_space=pl.ANY),
                  pl.BlockSpec((tm, d), lambda i, pa, pb: (i, 0)),
                  pl.BlockSpec((tm, LANES), lambda i, pa, pb: (i, 0)),
                  pl.BlockSpec((1, 6, d), lambda i, pa, pb: (i // per_b, 0, 0)),
                  pl.BlockSpec((1, d), lambda i, pa, pb: (0, 0)),
                  pl.BlockSpec((1, d), lambda i, pa, pb: (0, 0))],
        out_specs=pl.BlockSpec((tm, d), lambda i, pa, pb: (i, 0)),
        scratch_shapes=[pltpu.VMEM((2, tm, d), F32), pltpu.VMEM((2, tm, d), F32), pltpu.SemaphoreType.DMA((2,))],
    )
    return pl.pallas_call(
        functools.partial(_combine_kernel, tm=tm, alpha=alpha),
        grid_spec=grid_spec,
        out_shape=jax.ShapeDtypeStruct((t, d), F32),
        compiler_params=_cparams(("arbitrary",), row_dma=True),
        name="combine",
    )(pos_a, pos_b, ys, x1, info, mod3, ln_g, ln_b)


def _prep_w_in_kernel(w_ref, o_ref):
    r0 = MLA_Q_RANK + MLA_KV_RANK
    r1 = r0 + MLA_ROPE_DIM
    half = MLA_ROPE_DIM // 2
    z = jnp.zeros((LANES - MLA_ROPE_DIM, o_ref.shape[1]), BF16)
    o_ref[:r1, :] = w_ref[:r1, :].astype(BF16)
    o_ref[r1:r0 + LANES, :] = z
    o_ref[r0 + LANES:r0 + LANES + half, :] = w_ref[r0 + half:r1, :].astype(BF16)
    o_ref[r0 + LANES + half:r0 + LANES + MLA_ROPE_DIM, :] = w_ref[r0:r0 + half, :].astype(BF16)
    o_ref[r0 + LANES + MLA_ROPE_DIM:PART_A, :] = z
    o_ref[PART_A:, :] = w_ref[r1:, :].astype(BF16)


def _prep_w_in(w_t):
    n, k = w_t.shape
    tk = 512
    n_out = PART_A + 3 * MOBA_WIDTH
    assert n == MLA_Q_RANK + MLA_KV_RANK + MLA_ROPE_DIM + 3 * MOBA_WIDTH
    return pl.pallas_call(
        _prep_w_in_kernel,
        grid=(k // tk,),
        in_specs=[pl.BlockSpec((n, tk), lambda i: (0, i))],
        out_specs=pl.BlockSpec((n_out, tk), lambda i: (0, i)),
        out_shape=jax.ShapeDtypeStruct((n_out, k), BF16),
        compiler_params=_cparams(("arbitrary",)),
        name="prep_w_in",
    )(w_t)


def _prep_w_uq(w):
    r = w.shape[0]
    w = w.reshape(r, MLA_HEADS, MLA_NOPE_DIM + MLA_ROPE_DIM)
    half = MLA_ROPE_DIM // 2
    nope = w[:, :, :MLA_NOPE_DIM]
    x1 = w[:, :, MLA_NOPE_DIM:MLA_NOPE_DIM + half]
    x2 = w[:, :, MLA_NOPE_DIM + half:]
    assert 2 * MLA_ROPE_DIM == LANES
    return jnp.concatenate([nope, x1, x2, x2, x1], axis=2).reshape(r, MLA_HEADS * QK_PAD).astype(BF16)


def _rope_lanes(seq):
    inv = 1.0 / (ROPE_THETA ** (jnp.arange(0, MLA_ROPE_DIM, 2, dtype=F32) / MLA_ROPE_DIM))
    ang = jnp.arange(seq, dtype=F32)[:, None] * inv[None, :]
    cos, sin = jnp.cos(ang), jnp.sin(ang)
    z = jnp.zeros((seq, LANES - MLA_ROPE_DIM), F32)
    return jnp.concatenate([cos, cos, z], axis=1), jnp.concatenate([-sin, sin, z], axis=1)


def _prep_router(w_rg, b_rg, w_re, b_re):
    d = w_rg.shape[0]
    w = jnp.concatenate([w_rg, w_re], axis=1)
    n = w.shape[1]
    hi = w.astype(BF16)
    lo = (w - hi.astype(F32)).astype(BF16)
    z = jnp.zeros((d, LANES // 2 - n), BF16)
    wr = jnp.concatenate([hi, z, lo, z], axis=1)
    br = jnp.zeros((1, LANES), F32).at[0, :n].set(jnp.concatenate([b_rg, b_re]))
    return wr, br


def _layer(xf, mod3, batch, seq, depth_alpha, w_in, q_norm_g, w_uq, kv_norm_g, w_ukv, w_out, bias_tiles,
           cos_t, sin_t, ln1_g, ln1_b, w_rg, b_rg, w_re, b_re, w1, w3, w2, ln2_g, ln2_b):
    t, d = xf.shape
    part_a, qkv_mo = _in_proj(xf, mod3, _prep_w_in(w_in.T), seq)
    q, k, v = _qkv(part_a, _prep_w_uq(w_uq), w_ukv.astype(BF16), q_norm_g.reshape(1, -1),
                   kv_norm_g.reshape(1, -1), cos_t, sin_t, seq)
    o_mla = _mla_attention(q, k, v, batch, seq)
    q_wide = _moba_select(qkv_mo, batch, seq)
    o_moba = _moba_attention(q_wide, qkv_mo, bias_tiles, batch, seq)
    wo = w_out.astype(BF16)
    wr, br = _prep_router(w_rg, b_rg, w_re, b_re)
    x1, h2, logits = _out_proj(o_mla, o_moba, wo, xf, mod3, ln1_g.reshape(1, d),
                               ln1_b.reshape(1, d), wr, seq, depth_alpha)
    info, counts = _route(logits, br)
    tr = 256
    nt = (2 * t) // tr + MOE_N_EXPERTS
    cnt = counts[0, MOE_GROUPS:MOE_GROUPS + MOE_N_EXPERTS].astype(jnp.int32)
    ntile = (cnt + tr - 1) // tr
    tile_end = jnp.cumsum(ntile)
    tile_start = tile_end - ntile
    used = tile_end[-1]
    start_lanes = jnp.zeros((1, LANES), F32).at[0, MOE_GROUPS:MOE_GROUPS + MOE_N_EXPERTS].set(
        (tile_start * tr).astype(F32))
    pos = _positions(info, start_lanes)
    pos_a, pos_b = pos[0], pos[1]
    tidx = jnp.minimum(jnp.arange(nt, dtype=jnp.int32), used - 1)
    texp = jnp.sum(tidx[:, None] >= tile_end[None, :], axis=1).astype(jnp.int32)
    eids = jnp.arange(MOE_N_EXPERTS, dtype=jnp.int32)
    later = (eids[None, :] > eids[:, None]) & (ntile[None, :] > 0)
    next_nonempty = jnp.where(later.any(axis=1), jnp.argmax(later, axis=1), -1).astype(jnp.int32)
    nexte = jnp.sum(jnp.where(texp[:, None] == eids[None, :], next_nonempty[None, :], 0), axis=1).astype(jnp.int32)
    used1 = used.reshape(1).astype(jnp.int32)
    pad0 = (tile_start * tr + cnt).astype(jnp.int32)
    padn = (ntile * tr - cnt).astype(jnp.int32)
    xs = _dispatch(pos_a, pos_b, pad0, padn, used1, h2, tr, nt)
    ys = _experts(texp, tidx, nexte, used1, xs, w1, w3, w2, tr)
    return _combine(pos_a, pos_b, ys, x1, info, mod3, ln2_g.reshape(1, d), ln2_b.reshape(1, d), seq, depth_alpha)


def kernel(x, c, w_ada, b_ada, w_in, q_norm_g, w_uq, kv_norm_g, w_ukv, w_out, t5_table, ln1_g, ln1_b,
           w_router_group, b_router_group, w_router_expert, b_router_expert, w1, w3, w2, ln2_g, ln2_b):
    batch, seq, d = x.shape
    depth = w_ada.shape[0]
    alpha = (2.0 * depth) ** 0.25
    cos_t, sin_t = _rope_lanes(seq)
    bias_tiles = _t5_tiles(t5_table)
    xf = x.reshape(batch * seq, d)
    for l in range(depth):
        mod3 = _ada_mod(c, w_ada[l], b_ada[l]).reshape(batch, 6, d)
        xf = _layer(xf, mod3, batch, seq, alpha, w_in[l], q_norm_g[l], w_uq[l], kv_norm_g[l], w_ukv[l],
                    w_out[l], bias_tiles, cos_t, sin_t, ln1_g[l], ln1_b[l], w_router_group[l],
                    b_router_group[l], w_router_expert[l], b_router_expert[l], w1[l], w3[l], w2[l],
                    ln2_g[l], ln2_b[l])
    return xf.reshape(batch, seq, d)
```

```python
import functools
import math

import jax
import jax.numpy as jnp
from jax import lax
from jax.experimental import pallas as pl
from jax.experimental.pallas import tpu as pltpu

D_MODEL = 2048
MLA_HEADS = 8
MLA_Q_RANK = 512
MLA_KV_RANK = 256
MLA_NOPE_DIM = 128
MLA_ROPE_DIM = 64
MLA_V_DIM = 128
ROPE_THETA = 10000.0
MOBA_HEADS = 8
MOBA_HEAD_DIM = 128
MOBA_BLOCK = 256
MOBA_TOPK = 3
T5_BUCKETS = 32
T5_MAX_DISTANCE = 128
MOE_GROUPS = 4
MOE_EXPERTS_PER_GROUP = 8
MOE_N_EXPERTS = MOE_GROUPS * MOE_EXPERTS_PER_GROUP
MOE_D_FF = 512
LN_EPS = 1e-5
RMS_EPS = 1e-6
MOBA_WIDTH = MOBA_HEADS * MOBA_HEAD_DIM

LANES = 128
SUBLANES = 8
QK_PAD = 256
V_PAD = 256
PART_A = 1024
NEG = -1e30
VMEM_LIMIT = 56 * 1024 * 1024

F32 = jnp.float32
BF16 = jnp.bfloat16


def _cparams(sem, row_dma=False):
    return pltpu.CompilerParams(dimension_semantics=sem, vmem_limit_bytes=VMEM_LIMIT,
                                disable_bounds_checks=row_dma)


def _ln(x):
    mu = jnp.mean(x, axis=-1, keepdims=True)
    xc = x - mu
    var = jnp.mean(xc * xc, axis=-1, keepdims=True)
    return xc * lax.rsqrt(var + LN_EPS)


def _nt_dot(a, b):
    return lax.dot_general(a, b, (((1,), (1,)), ((), ())), preferred_element_type=F32)


def _ada_kernel(ct_ref, w_ref, b_ref, o_ref, *, batch):
    ct = ct_ref[...]
    ca = ct / (1.0 + jnp.exp(-ct))
    w = w_ref[...]
    rows = [jnp.sum(w * ca[:, b:b + 1], axis=0, keepdims=True) for b in range(batch)]
    o_ref[...] = jnp.concatenate(rows, axis=0) + b_ref[...]


def _ada_mod(c, w_ada, b_ada):
    batch, d = c.shape
    n = w_ada.shape[1]
    tn = 1024
    ct = jnp.zeros((d, LANES), F32).at[:, :batch].set(c.T)
    return pl.pallas_call(
        functools.partial(_ada_kernel, batch=batch),
        grid=(n // tn,),
        in_specs=[pl.BlockSpec((d, LANES), lambda j: (0, 0)),
                  pl.BlockSpec((d, tn), lambda j: (0, j)),
                  pl.BlockSpec((1, tn), lambda j: (0, j))],
        out_specs=pl.BlockSpec((batch, tn), lambda j: (0, j)),
        out_shape=jax.ShapeDtypeStruct((batch, n), F32),
        compiler_params=_cparams(("arbitrary",)),
        name="ada_mod",
    )(ct, w_ada, b_ada.reshape(1, n))


def _inproj_kernel(x_ref, mod_ref, w_ref, a_ref, b_ref, h_scr):
    j = pl.program_id(1)

    @pl.when(j == 0)
    def _():
        rows = 256
        for r in range(0, x_ref.shape[0], rows):
            h = _ln(x_ref[r:r + rows, :]) * (1.0 + mod_ref[0, 1:2, :]) + mod_ref[0, 0:1, :]
            h_scr[r:r + rows, :] = h.astype(BF16)
        a_ref[...] = _nt_dot(h_scr[...], w_ref[...])

    @pl.when(j > 0)
    def _():
        b_ref[...] = _nt_dot(h_scr[...], w_ref[...]).astype(BF16)


def _in_proj(xf, mod3, w4, seq):
    t, d = xf.shape
    n = w4.shape[0]
    tm, tn = 1024, PART_A
    per_b = seq // tm
    return pl.pallas_call(
        _inproj_kernel,
        grid=(t // tm, n // tn),
        in_specs=[pl.BlockSpec((tm, d), lambda i, j: (i, 0)),
                  pl.BlockSpec((1, 6, d), lambda i, j: (i // per_b, 0, 0)),
                  pl.BlockSpec((tn, d), lambda i, j: (j, 0))],
        out_specs=[pl.BlockSpec((tm, tn), lambda i, j: (i, 0)),
                   pl.BlockSpec((tm, tn), lambda i, j: (i, jnp.maximum(j - 1, 0)))],
        out_shape=[jax.ShapeDtypeStruct((t, PART_A), F32),
                   jax.ShapeDtypeStruct((t, n - PART_A), BF16)],
        scratch_shapes=[pltpu.VMEM((tm, d), BF16)],
        compiler_params=_cparams(("arbitrary", "arbitrary")),
        name="in_proj",
    )(xf, mod3, w4)


def _qkv_kernel(a_ref, wq_ref, wkv_ref, gq_ref, gkv_ref, cos_ref, sin_ref, q_ref, k_ref, v_ref):
    a = a_ref[...]
    cq = a[:, :MLA_Q_RANK]
    ckv = a[:, MLA_Q_RANK:MLA_Q_RANK + MLA_KV_RANK]
    kr = a[:, 768:896]
    krs = a[:, 896:1024]
    cqn = (cq * lax.rsqrt(jnp.mean(cq * cq, axis=-1, keepdims=True) + RMS_EPS) * gq_ref[...]).astype(BF16)
    ckvn = (ckv * lax.rsqrt(jnp.mean(ckv * ckv, axis=-1, keepdims=True) + RMS_EPS) * gkv_ref[...]).astype(BF16)
    cos = cos_ref[...]
    sin = sin_ref[...]
    krr = (kr * cos + krs * sin).astype(BF16)
    for h in range(MLA_HEADS):
        c0 = h * QK_PAD
        q2 = jnp.dot(cqn, wq_ref[:, c0:c0 + QK_PAD], preferred_element_type=F32)
        kv = jnp.dot(ckvn, wkv_ref[:, c0:c0 + QK_PAD], preferred_element_type=F32)
        rope = q2[:, LANES:]
        q_ref[:, c0:c0 + LANES] = q2[:, :LANES].astype(BF16)
        q_ref[:, c0 + LANES:c0 + QK_PAD] = (rope * cos + pltpu.roll(rope, LANES // 2, 1) * sin).astype(BF16)
        k_ref[:, c0:c0 + LANES] = kv[:, :LANES].astype(BF16)
        k_ref[:, c0 + LANES:c0 + QK_PAD] = krr
        v_ref[:, h * LANES:(h + 1) * LANES] = kv[:, LANES:].astype(BF16)


def _qkv(part_a, wq3, wkv, gq, gkv, cos_t, sin_t, seq):
    t = part_a.shape[0]
    tm = 512
    per_b = seq // tm
    hq = MLA_HEADS * QK_PAD
    return pl.pallas_call(
        _qkv_kernel,
        grid=(t // tm,),
        in_specs=[pl.BlockSpec((tm, PART_A), lambda i: (i, 0)),
                  pl.BlockSpec(wq3.shape, lambda i: (0, 0)),
                  pl.BlockSpec(wkv.shape, lambda i: (0, 0)),
                  pl.BlockSpec((1, MLA_Q_RANK), lambda i: (0, 0)),
                  pl.BlockSpec((1, MLA_KV_RANK), lambda i: (0, 0)),
                  pl.BlockSpec((tm, LANES), lambda i: (i % per_b, 0)),
                  pl.BlockSpec((tm, LANES), lambda i: (i % per_b, 0))],
        out_specs=[pl.BlockSpec((tm, hq), lambda i: (i, 0)),
                   pl.BlockSpec((tm, hq), lambda i: (i, 0)),
                   pl.BlockSpec((tm, MLA_HEADS * MLA_V_DIM), lambda i: (i, 0))],
        out_shape=[jax.ShapeDtypeStruct((t, hq), BF16),
                   jax.ShapeDtypeStruct((t, hq), BF16),
                   jax.ShapeDtypeStruct((t, MLA_HEADS * MLA_V_DIM), BF16)],
        compiler_params=_cparams(("arbitrary",)),
        name="qkv",
    )(part_a, wq3, wkv, gq, gkv, cos_t, sin_t)


ATTN_TILE = 512
ATTN_CHAINS = 8


def _attn_kernel(*refs, c, masked):
    nch = ATTN_CHAINS
    scratch = refs[-4 * nch:]
    tile = ATTN_TILE
    i = pl.program_id(2)
    if masked:
        q_ref, kin_ref, vin_ref, bias_ref, o_ref, v_ref, k_ref = refs[:7]
    else:
        q_ref, k_ref, vin_ref, o_ref, v_ref = refs[:5]

    @pl.when((pl.program_id(0) == 0) & (pl.program_id(1) == 0) & (i == 0))
    def _():
        seq = vin_ref.shape[0]
        v_ref[:, LANES:] = jnp.ones((seq, V_PAD - LANES), BF16)
        if masked:
            lane = lax.broadcasted_iota(jnp.int32, (seq, LANES), 1)
            own = jnp.right_shift(lax.broadcasted_iota(jnp.int32, (seq, LANES), 0), MOBA_BLOCK.bit_length() - 1)
            k_ref[:, LANES:] = jnp.where(lane == own, 1.0, 0.0).astype(BF16)

    @pl.when(i == 0)
    def _():
        v_ref[:, :LANES] = vin_ref[...]
        if masked:
            k_ref[:, :LANES] = kin_ref[...]

    chains = [dict(rows=slice(n * tile, (n + 1) * tile), s=scratch[4 * n:4 * n + 2], m=scratch[4 * n + 2],
                   acc=scratch[4 * n + 3]) for n in range(nch)]

    half = tile // 2

    def put_scores(chain, slot, t, diag=False):
        k0 = pl.multiple_of(t * tile, tile)
        q0 = chain["rows"].start
        buf = chain["s"][slot]
        if diag:
            buf[:half, :half] = _nt_dot(q_ref[q0:q0 + half, :], k_ref[pl.ds(k0, half), :])
            buf[half:, :] = _nt_dot(q_ref[q0 + half:q0 + tile, :], k_ref[pl.ds(k0, tile), :])
        else:
            buf[...] = _nt_dot(q_ref[q0:q0 + tile, :], k_ref[pl.ds(k0, tile), :])

    def softmax_update(chain, s, t, rows, keys):
        m_scr, acc_scr = chain["m"], chain["acc"]
        m_old = m_scr[rows, :]
        m_new = jnp.maximum(m_old, jnp.broadcast_to(jnp.max(s, axis=-1, keepdims=True), m_old.shape))
        alpha = jnp.exp2((m_old - m_new) * c)
        p = jnp.concatenate([jnp.exp2(((s[:, j * LANES:(j + 1) * LANES] - m_new) * c).astype(BF16))
                             for j in range(keys // LANES)], axis=1)
        v = v_ref[pl.ds(pl.multiple_of(t * tile, tile), keys), :]
        pv = jnp.dot(p, v, preferred_element_type=F32)
        acc_scr[rows, :] = jnp.concatenate([alpha, alpha], axis=1) * acc_scr[rows, :] + pv
        m_scr[rows, :] = m_new

    def tile_step(chain, slot, t, kind, prefetch=True, next_diag=False):
        if prefetch:
            put_scores(chain, 1 - slot, t + 1, diag=next_diag)
        buf = chain["s"][slot]
        if kind != "diag":
            s = buf[...]
            if masked and kind == "prev":
                s = s + bias_ref[0, 1]
            softmax_update(chain, s, t, slice(0, tile), tile)
            return
        for r0, keys in ((0, half), (half, tile)):
            rows = slice(r0, r0 + half)
            s = buf[rows, :keys]
            if masked:
                s = s + bias_ref[0, 0, rows, :keys]
            else:
                row = lax.broadcasted_iota(jnp.int32, (half, keys), 0) + r0
                col = lax.broadcasted_iota(jnp.int32, (half, keys), 1)
                s = jnp.where(row >= col, s, NEG)
            softmax_update(chain, s, t, rows, keys)

    for chain in chains:
        chain["m"][...] = jnp.full(chain["m"].shape, NEG, F32)
        chain["acc"][...] = jnp.zeros(chain["acc"].shape, F32)
        put_scores(chain, 0, 0)

    def far_pair(j, _):
        for slot in range(2):
            for chain in chains:
                tile_step(chain, slot, 2 * j + slot, "far")
        return 0

    first = nch * i
    if masked:
        lax.fori_loop(0, jnp.maximum(first // 2 - 1, 0), far_pair, 0)

        @pl.when(i > 0)
        def _():
            for chain in chains:
                tile_step(chain, 0, first - 2, "far")
            for n, chain in enumerate(chains):
                tile_step(chain, 1, first - 1, "prev" if n == 0 else "far", next_diag=(n == 0))
    else:
        lax.fori_loop(0, first // 2, far_pair, 0)

    for k in range(nch):
        for n, chain in enumerate(chains):
            if n < k:
                continue
            kind = "diag" if n == k else ("prev" if masked and n == k + 1 else "far")
            tile_step(chain, k % 2, first + k, kind, prefetch=(n != k), next_diag=(n == k + 1))

    for chain in chains:
        acc = chain["acc"]
        o_ref[chain["rows"], :] = (acc[:, :LANES] / acc[:, LANES:]).astype(o_ref.dtype)


def _attn_scratch(seq, masked):
    tile = ATTN_TILE
    wide = [pltpu.VMEM((seq, V_PAD), BF16)] + ([pltpu.VMEM((seq, QK_PAD), BF16)] if masked else [])
    per_chain = [pltpu.VMEM((tile, tile), F32), pltpu.VMEM((tile, tile), F32),
                 pltpu.VMEM((tile, LANES), F32), pltpu.VMEM((tile, V_PAD), F32)]
    return wide + per_chain * ATTN_CHAINS


def _mla_attention(q, k, v, batch, seq):
    tile = ATTN_CHAINS * ATTN_TILE
    nq = seq // tile
    c = math.log2(math.e) / math.sqrt(MLA_NOPE_DIM + MLA_ROPE_DIM)
    return pl.pallas_call(
        functools.partial(_attn_kernel, c=c, masked=False),
        grid=(batch, MLA_HEADS, nq),
        in_specs=[pl.BlockSpec((tile, QK_PAD), lambda b, h, i: (b * nq + i, h)),
                  pl.BlockSpec((seq, QK_PAD), lambda b, h, i: (b, h)),
                  pl.BlockSpec((seq, MLA_V_DIM), lambda b, h, i: (b, h))],
        out_specs=pl.BlockSpec((tile, MLA_V_DIM), lambda b, h, i: (b * nq + i, h)),
        out_shape=jax.ShapeDtypeStruct((batch * seq, MLA_HEADS * MLA_V_DIM), BF16),
        scratch_shapes=_attn_scratch(seq, False),
        compiler_params=_cparams(("arbitrary", "arbitrary", "arbitrary")),
        name="mla_attn",
    )(q, k, v)


def _moba_select_kernel(q_ref, k_ref, qa_ref, *, seq, nb):
    kf = k_ref[...].astype(F32)
    km = jnp.sum(kf.reshape(nb, MOBA_BLOCK, MOBA_HEAD_DIM), axis=1) * (1.0 / MOBA_BLOCK)
    km_hi = km.astype(BF16)
    km_lo = (km - km_hi.astype(F32)).astype(BF16)
    q = q_ref[...]
    gate = _nt_dot(km_hi, q) + _nt_dot(km_lo, q)
    shift = MOBA_BLOCK.bit_length() - 1
    blk = lax.broadcasted_iota(jnp.int32, (nb, seq), 0)
    qblk = jnp.right_shift(lax.broadcasted_iota(jnp.int32, (nb, seq), 1), shift)
    g = jnp.where(blk < qblk, gate, NEG)
    visible = blk == qblk
    for _ in range(MOBA_TOPK):
        mx = jnp.max(g, axis=0, keepdims=True)
        first = jnp.min(jnp.where(g == mx, blk, nb), axis=0, keepdims=True)
        pick = (blk == first) & (mx > 0.5 * NEG)
        visible = visible | pick
        g = jnp.where(pick, NEG, g)
    mask_t = jnp.concatenate([jnp.where(visible, 0.0, NEG), jnp.zeros((LANES - nb, seq), F32)], axis=0)
    qa_ref[:, :MOBA_HEAD_DIM] = q
    qa_ref[:, MOBA_HEAD_DIM:] = mask_t.T.astype(BF16)


def _moba_select(qkv_mo, batch, seq):
    nb = seq // MOBA_BLOCK
    assert MOBA_HEAD_DIM == LANES and nb <= QK_PAD - MOBA_HEAD_DIM
    return pl.pallas_call(
        functools.partial(_moba_select_kernel, seq=seq, nb=nb),
        grid=(batch, MOBA_HEADS),
        in_specs=[pl.BlockSpec((seq, MOBA_HEAD_DIM), lambda b, h: (b, h)),
                  pl.BlockSpec((seq, MOBA_HEAD_DIM), lambda b, h: (b, MOBA_HEADS + h))],
        out_specs=pl.BlockSpec((seq, QK_PAD), lambda b, h: (b, h)),
        out_shape=jax.ShapeDtypeStruct((batch * seq, MOBA_HEADS * QK_PAD), BF16),
        compiler_params=_cparams(("arbitrary", "arbitrary")),
        name="moba_select",
    )(qkv_mo, qkv_mo)


def _t5_kernel(tab_ref, o_ref, *, inv_scale):
    h = pl.program_id(0)
    r = lax.broadcasted_iota(jnp.int32, (LANES, LANES), 0)
    c = lax.broadcasted_iota(jnp.int32, (LANES, LANES), 1)
    max_exact = T5_BUCKETS // 2
    far = tab_ref[T5_BUCKETS - 1, h]

    def block(offset):
        rel = offset + r - c
        n = jnp.maximum(rel, 0)
        nf = jnp.maximum(n, 1).astype(F32)
        large = max_exact + (jnp.log(nf / max_exact) / math.log(T5_MAX_DISTANCE / max_exact)
                             * (T5_BUCKETS - max_exact)).astype(jnp.int32)
        large = jnp.minimum(large, T5_BUCKETS - 1)
        bucket = jnp.where(n < max_exact, n, large)
        bias = jnp.zeros((LANES, LANES), F32)
        for j in range(T5_BUCKETS):
            bias = jnp.where(bucket == j, tab_ref[j, h], bias)
        return jnp.where(rel >= 0, (bias - far) * inv_scale, NEG)

    near = {0: block(0), 1: block(LANES)}
    nblk = ATTN_TILE // LANES
    for d in range(2):
        for i in range(nblk):
            for j in range(nblk):
                k = d * nblk + i - j
                if k < 0:
                    val = jnp.full((LANES, LANES), NEG, F32)
                else:
                    val = near.get(k, jnp.zeros((LANES, LANES), F32))
                o_ref[0, d, i * LANES:(i + 1) * LANES, j * LANES:(j + 1) * LANES] = val


def _t5_tiles(t5_table):
    assert LANES >= T5_MAX_DISTANCE
    tile = ATTN_TILE
    return pl.pallas_call(
        functools.partial(_t5_kernel, inv_scale=math.sqrt(MOBA_HEAD_DIM)),
        grid=(MOBA_HEADS,),
        in_specs=[pl.BlockSpec(memory_space=pltpu.SMEM)],
        out_specs=pl.BlockSpec((1, 2, tile, tile), lambda h: (h, 0, 0, 0)),
        out_shape=jax.ShapeDtypeStruct((MOBA_HEADS, 2, tile, tile), F32),
        compiler_params=_cparams(("arbitrary",)),
        name="t5_tiles",
    )(t5_table)


def _moba_attention(q_wide, qkv_mo, bias, batch, seq):
    tile = ATTN_CHAINS * ATTN_TILE
    nq = seq // tile
    c = math.log2(math.e) / math.sqrt(MOBA_HEAD_DIM)
    dh = MOBA_HEAD_DIM
    return pl.pallas_call(
        functools.partial(_attn_kernel, c=c, masked=True),
        grid=(batch, MOBA_HEADS, nq),
        in_specs=[pl.BlockSpec((tile, QK_PAD), lambda b, h, i: (b * nq + i, h)),
                  pl.BlockSpec((seq, dh), lambda b, h, i: (b, MOBA_HEADS + h)),
                  pl.BlockSpec((seq, dh), lambda b, h, i: (b, 2 * MOBA_HEADS + h)),
                  pl.BlockSpec((1, 2, ATTN_TILE, ATTN_TILE), lambda b, h, i: (h, 0, 0, 0))],
        out_specs=pl.BlockSpec((tile, dh), lambda b, h, i: (b * nq + i, h)),
        out_shape=jax.ShapeDtypeStruct((batch * seq, MOBA_WIDTH), BF16),
        scratch_shapes=_attn_scratch(seq, True),
        compiler_params=_cparams(("arbitrary", "arbitrary", "arbitrary")),
        name="moba_attn",
    )(q_wide, qkv_mo, qkv_mo, bias)


def _outproj_kernel(oa_ref, ob_ref, wa_ref, wb_ref, x_ref, mod_ref, g_ref, b_ref, wr_ref,
                    x1_ref, h2_ref, lg_ref, y_a, y_b, *, alpha, n_tiles):
    i = pl.program_id(0)
    tm, d = y_a.shape
    chunks = 4
    cn, cr = d // chunks, tm // chunks

    def matmul_into(y_ref, c):
        cols = slice(c * cn, (c + 1) * cn)
        y_ref[:, cols] = (jnp.dot(oa_ref[...], wa_ref[:, cols], preferred_element_type=F32)
                          + jnp.dot(ob_ref[...], wb_ref[:, cols], preferred_element_type=F32))

    def epilogue(y_ref, c):
        rows = slice(c * cr, (c + 1) * cr)
        z = alpha * x_ref[rows, :] + mod_ref[0, 2:3, :] * y_ref[rows, :]
        x1 = _ln(z) * g_ref[...] + b_ref[...]
        x1_ref[rows, :] = x1
        h2 = _ln(x1) * (1.0 + mod_ref[0, 4:5, :]) + mod_ref[0, 3:4, :]
        h_hi = h2.astype(BF16)
        h_lo = (h2 - h_hi.astype(F32)).astype(BF16)
        h_bits = pltpu.bitcast(h_hi.astype(F32), jnp.uint32)
        h2_ref[rows, :] = h_bits[:, :d // 2] | jnp.right_shift(h_bits[:, d // 2:], 16)
        zz = (jnp.dot(h_hi, wr_ref[...], preferred_element_type=F32)
              + jnp.dot(h_lo, wr_ref[...], preferred_element_type=F32))
        lg_ref[rows, :] = zz + pltpu.roll(zz, LANES // 2, 1)

    def step(y_new, y_old):
        for c in range(chunks):
            if y_new is not None:
                matmul_into(y_new, c)
            if y_old is not None:
                epilogue(y_old, c)

    inner = (i > 0) & (i < n_tiles)
    pl.when(i == 0)(lambda: step(y_a, None))
    pl.when(inner & (i % 2 == 0))(lambda: step(y_a, y_b))
    pl.when(inner & (i % 2 == 1))(lambda: step(y_b, y_a))
    pl.when(i == n_tiles)(lambda: step(None, y_b if n_tiles % 2 == 0 else y_a))


def _out_proj(o_mla, o_moba, wo, xf, mod3, ln_g, ln_b, wr, seq, alpha):
    t, d = xf.shape
    tm = 512
    n = t // tm
    per_b = seq // tm
    ka, kb = o_mla.shape[1], o_moba.shape[1]
    assert ka == kb and wo.shape[0] == ka + kb
    once = pl.Buffered(1)

    def cur(i):
        return (jnp.minimum(i, n - 1), 0)

    def lag(i):
        return (jnp.maximum(i - 1, 0), 0)

    return pl.pallas_call(
        functools.partial(_outproj_kernel, alpha=alpha, n_tiles=n),
        grid=(n + 1,),
        in_specs=[pl.BlockSpec((tm, ka), cur),
                  pl.BlockSpec((tm, kb), cur),
                  pl.BlockSpec((ka, d), lambda i: (0, 0), pipeline_mode=once),
                  pl.BlockSpec((kb, d), lambda i: (1, 0), pipeline_mode=once),
                  pl.BlockSpec((tm, d), lag),
                  pl.BlockSpec((1, 6, d), lambda i: (jnp.maximum(i - 1, 0) // per_b, 0, 0)),
                  pl.BlockSpec((1, d), lambda i: (0, 0)),
                  pl.BlockSpec((1, d), lambda i: (0, 0)),
                  pl.BlockSpec((d, LANES), lambda i: (0, 0), pipeline_mode=once)],
        out_specs=[pl.BlockSpec((tm, d), lag),
                   pl.BlockSpec((tm, d // 2), lag),
                   pl.BlockSpec((tm, LANES), lag)],
        out_shape=[jax.ShapeDtypeStruct((t, d), F32),
                   jax.ShapeDtypeStruct((t, d // 2), jnp.uint32),
                   jax.ShapeDtypeStruct((t, LANES), F32)],
        scratch_shapes=[pltpu.VMEM((tm, d), F32), pltpu.VMEM((tm, d), F32)],
        compiler_params=_cparams(("arbitrary",)),
        name="out_proj",
    )(o_mla, o_moba, wo, wo, xf, mod3, ln_g, ln_b, wr)


def _route_kernel(lg_ref, br_ref, info_ref, cnt_ref, run_scr, *, tm):
    i = pl.program_id(0)

    @pl.when(i == 0)
    def _():
        run_scr[...] = jnp.zeros_like(run_scr)

    lg = lg_ref[...] + br_ref[...]
    lane = lax.broadcasted_iota(jnp.int32, (tm, LANES), 1)
    e_lo, e_hi = MOE_GROUPS, MOE_GROUPS + MOE_N_EXPERTS
    is_g = lane < e_lo
    gl = jnp.where(is_g, lg, NEG)
    gmax = jnp.max(gl, axis=-1, keepdims=True)
    gidx = jnp.min(jnp.where(gl == gmax, lane, LANES), axis=-1, keepdims=True)
    g_p = 1.0 / jnp.sum(jnp.where(is_g, jnp.exp(gl - gmax), 0.0), axis=-1, keepdims=True)
    grp_of_lane = jnp.right_shift(lane - e_lo, MOE_EXPERTS_PER_GROUP.bit_length() - 1)
    in_grp = (lane >= e_lo) & (lane < e_hi) & (grp_of_lane == gidx)
    el = jnp.where(in_grp, lg, NEG)
    m1 = jnp.max(el, axis=-1, keepdims=True)
    l1 = jnp.min(jnp.where(el == m1, lane, LANES), axis=-1, keepdims=True)
    el2 = jnp.where(lane == l1, NEG, el)
    m2 = jnp.max(el2, axis=-1, keepdims=True)
    l2 = jnp.min(jnp.where(el2 == m2, lane, LANES), axis=-1, keepdims=True)
    zsum = jnp.sum(jnp.where(in_grp, jnp.exp(el - m1), 0.0), axis=-1, keepdims=True)
    p1 = 1.0 / zsum
    p2 = jnp.exp(m2 - m1) / zsum
    wa = g_p * (p1 / (p1 + p2))
    wb = g_p * (p2 / (p1 + p2))
    hot_a = lane == l1
    hot_b = lane == l2
    onehot = jnp.where(hot_a | hot_b, 1.0, 0.0)
    r = lax.broadcasted_iota(jnp.int32, (tm, tm), 0)
    c = lax.broadcasted_iota(jnp.int32, (tm, tm), 1)
    lower = jnp.where(c < r, 1.0, 0.0).astype(BF16)
    before = jnp.dot(lower, onehot.astype(BF16), preferred_element_type=F32) + run_scr[...]
    rank_a = jnp.sum(jnp.where(hot_a, before, 0.0), axis=-1, keepdims=True)
    rank_b = jnp.sum(jnp.where(hot_b, before, 0.0), axis=-1, keepdims=True)
    run_scr[...] += jnp.sum(onehot, axis=0, keepdims=True)
    info = jnp.zeros((tm, LANES), F32)
    for k, val in enumerate([(l1 - e_lo).astype(F32), (l2 - e_lo).astype(F32), wa, wb, rank_a, rank_b]):
        info = jnp.where(lane == k, val, info)
    info_ref[...] = info
    cnt_ref[...] = run_scr[...]


def _route(logits, br):
    t = logits.shape[0]
    tm = 512
    return pl.pallas_call(
        functools.partial(_route_kernel, tm=tm),
        grid=(t // tm,),
        in_specs=[pl.BlockSpec((tm, LANES), lambda i: (i, 0)),
                  pl.BlockSpec((1, LANES), lambda i: (0, 0))],
        out_specs=[pl.BlockSpec((tm, LANES), lambda i: (i, 0)),
                   pl.BlockSpec((1, LANES), lambda i: (0, 0))],
        out_shape=[jax.ShapeDtypeStruct((t, LANES), F32),
                   jax.ShapeDtypeStruct((1, LANES), F32)],
        scratch_shapes=[pltpu.VMEM((1, LANES), F32)],
        compiler_params=_cparams(("arbitrary",)),
        name="route",
    )(logits, br)


def _pos_kernel(info_ref, start_ref, pos_ref):
    info = info_ref[...]
    tm = info.shape[0]
    lane = lax.broadcasted_iota(jnp.int32, (tm, LANES), 1)
    start = start_ref[...]
    cols = []
    for k in range(2):
        e = jnp.sum(jnp.where(lane == k, info, 0.0), axis=-1, keepdims=True).astype(jnp.int32)
        rank = jnp.sum(jnp.where(lane == 4 + k, info, 0.0), axis=-1, keepdims=True)
        base = jnp.sum(jnp.where(lane == e + MOE_GROUPS, start, 0.0), axis=-1, keepdims=True)
        cols.append(base + rank)
    wide = jnp.where(lane == 0, cols[0], jnp.where(lane == 1, cols[1], 0.0))
    pos_ref[...] = wide.T[:pos_ref.shape[0], :].astype(jnp.int32)


def _positions(info, start_lanes):
    t = info.shape[0]
    tm = 1024
    return pl.pallas_call(
        _pos_kernel,
        grid=(t // tm,),
        in_specs=[pl.BlockSpec((tm, LANES), lambda i: (i, 0)),
                  pl.BlockSpec((1, LANES), lambda i: (0, 0))],
        out_specs=pl.BlockSpec((8, tm), lambda i: (0, i)),
        out_shape=jax.ShapeDtypeStruct((8, t), jnp.int32),
        compiler_params=_cparams(("arbitrary",)),
        name="positions",
    )(info, start_lanes)


def _row_copy(src_ref, src_row, dst_ref, dst_row, sem):
    return pltpu.make_async_copy(src_ref.at[pl.ds(src_row, 1)], dst_ref.at[pl.ds(dst_row, 1)], sem)


def _dispatch_kernel(pa_ref, pb_ref, pad0_ref, padn_ref, used_ref, h_ref, xs_ref, ztile, sem, zsem,
                     *, tm, tr, n_tiles):
    i = pl.program_id(0)
    base = i * tm

    def zero_fill(act):
        def whole_tile(j, _):
            act(pltpu.make_async_copy(ztile, xs_ref.at[pl.ds(pl.multiple_of(j * tr, tr), tr)], zsem))
            return 0

        lax.fori_loop(used_ref[0], n_tiles, whole_tile, 0)

        def expert_pad(e, _):
            n, start = padn_ref[e], pad0_ref[e]
            head = jnp.minimum(jnp.bitwise_and(-start, SUBLANES - 1), n)
            for k in range(SUBLANES - 1):
                pl.when(k < head)(functools.partial(act, _row_copy(ztile, 0, xs_ref, start + k, zsem)))
            off = start + head
            groups = jnp.right_shift(n - head, SUBLANES.bit_length() - 1)
            for bit in reversed(range((tr // SUBLANES - 1).bit_length())):
                size = SUBLANES << bit
                take = jnp.bitwise_and(jnp.right_shift(groups, bit), 1)
                dst = xs_ref.at[pl.ds(pl.multiple_of(off, SUBLANES), size)]
                pl.when(take == 1)(functools.partial(act, pltpu.make_async_copy(ztile.at[pl.ds(0, size)], dst, zsem)))
                off = off + take * size
            return 0

        lax.fori_loop(0, MOE_N_EXPERTS, expert_pad, 0)

    @pl.when(i == 0)
    def _():
        ztile[...] = jnp.zeros(ztile.shape, ztile.dtype)
        zero_fill(lambda cp: cp.start())

    for t in range(tm):
        _row_copy(h_ref, t, xs_ref, pa_ref[base + t], sem).start()
        _row_copy(h_ref, t, xs_ref, pb_ref[base + t], sem).start(priority=1)
    for _ in range(2):
        pltpu.make_async_copy(h_ref, xs_ref.at[pl.ds(0, tm)], sem).wait()

    @pl.when(i == pl.num_programs(0) - 1)
    def _():
        zero_fill(lambda cp: cp.wait())


def _dispatch(pos_a, pos_b, pad0, padn, used, h2, tr, n_tiles):
    t, d = h2.shape
    tm = 256
    grid_spec = pltpu.PrefetchScalarGridSpec(
        num_scalar_prefetch=5,
        grid=(t // tm,),
        in_specs=[pl.BlockSpec((tm, d), lambda i, *_: (i, 0))],
        out_specs=pl.BlockSpec(memory_space=pl.ANY),
        scratch_shapes=[pltpu.VMEM((tr, d), h2.dtype), pltpu.SemaphoreType.DMA(()), pltpu.SemaphoreType.DMA(())],
    )
    return pl.pallas_call(
        functools.partial(_dispatch_kernel, tm=tm, tr=tr, n_tiles=n_tiles),
        grid_spec=grid_spec,
        out_shape=jax.ShapeDtypeStruct((n_tiles * tr, d), h2.dtype),
        compiler_params=_cparams(("arbitrary",), row_dma=True),
        name="dispatch",
    )(pos_a, pos_b, pad0, padn, used, h2)


def _experts_kernel(texp_ref, tidx_ref, nexte_ref, used_ref, x_ref, w1_hbm, w3_hbm, w2_hbm, o_ref,
                    w1_f32, w3_f32, w2_f32, w1_scr, w3_scr, w2_scr, sems):
    i = pl.program_id(0)
    prev = texp_ref[jnp.maximum(i - 1, 0)]
    first_of_expert = (i == 0) | (texp_ref[i] != prev)

    def weight_copies(e):
        return [pltpu.make_async_copy(w1_hbm.at[e], w1_f32, sems.at[0]),
                pltpu.make_async_copy(w3_hbm.at[e], w3_f32, sems.at[1]),
                pltpu.make_async_copy(w2_hbm.at[e], w2_f32, sems.at[2])]

    @pl.when(i == 0)
    def _():
        for cp in weight_copies(texp_ref[0]):
            cp.start()

    def tile(cast):
        bits = x_ref[...]
        x = jnp.concatenate(
            [pltpu.bitcast(jnp.bitwise_and(bits, jnp.uint32(0xFFFF0000)), F32).astype(BF16),
             pltpu.bitcast(jnp.left_shift(bits, 16), F32).astype(BF16)], axis=1)
        if cast:
            w1_scr[...] = w1_f32[...].astype(BF16)
        a = jnp.dot(x, w1_scr[...], preferred_element_type=F32)
        if cast:
            w3_scr[...] = w3_f32[...].astype(BF16)
        b = jnp.dot(x, w3_scr[...], preferred_element_type=F32)
        if cast:
            w2_scr[...] = w2_f32[...].astype(BF16)
            nxt = jnp.where(nexte_ref[i] >= 0, nexte_ref[i], texp_ref[i])
            for cp in weight_copies(nxt):
                cp.start(priority=1)
        hid = (a / (1.0 + jnp.exp(-a))) * b
        o_ref[...] = jnp.dot(hid.astype(BF16), w2_scr[...], preferred_element_type=F32)

    @pl.when(first_of_expert)
    def _():
        for cp in weight_copies(texp_ref[i]):
            cp.wait()
        tile(cast=True)

    @pl.when(jnp.logical_not(first_of_expert) & (i < used_ref[0]))
    def _():
        tile(cast=False)

    @pl.when(i == pl.num_programs(0) - 1)
    def _():
        for cp in weight_copies(texp_ref[i]):
            cp.wait()

    @pl.when(i >= used_ref[0])
    def _():
        o_ref[...] = jnp.zeros_like(o_ref)


def _experts(texp, tidx, nexte, used, xs, w1, w3, w2, tr):
    rows = xs.shape[0]
    nt = rows // tr
    d, f = w1.shape[1:]
    assert xs.shape[1] * 2 == d and xs.dtype == jnp.uint32
    grid_spec = pltpu.PrefetchScalarGridSpec(
        num_scalar_prefetch=4,
        grid=(nt,),
        in_specs=[pl.BlockSpec((tr, d // 2), lambda i, te, ti, ne, u: (ti[i], 0)),
                  pl.BlockSpec(memory_space=pl.ANY),
                  pl.BlockSpec(memory_space=pl.ANY),
                  pl.BlockSpec(memory_space=pl.ANY)],
        out_specs=pl.BlockSpec((tr, d), lambda i, te, ti, ne, u: (i, 0)),
        scratch_shapes=[pltpu.VMEM((d, f), F32), pltpu.VMEM((d, f), F32), pltpu.VMEM((f, d), F32),
                        pltpu.VMEM((d, f), BF16), pltpu.VMEM((d, f), BF16), pltpu.VMEM((f, d), BF16),
                        pltpu.SemaphoreType.DMA((3,))],
    )
    return pl.pallas_call(
        _experts_kernel,
        grid_spec=grid_spec,
        out_shape=jax.ShapeDtypeStruct((rows, d), F32),
        compiler_params=_cparams(("arbitrary",)),
        name="experts",
    )(texp, tidx, nexte, used, xs, w1, w3, w2)


def _combine_kernel(pa_ref, pb_ref, ys_ref, x1_ref, info_ref, mod_ref, g_ref, b_ref, o_ref,
                    buf_a, buf_b, sems, *, tm, alpha):
    i = pl.program_id(0)
    last = pl.num_programs(0) - 1

    def issue_row(tile, slot, t):
        _row_copy(ys_ref, pa_ref[tile * tm + t], buf_a.at[slot], t, sems.at[slot]).start()
        _row_copy(ys_ref, pb_ref[tile * tm + t], buf_b.at[slot], t, sems.at[slot]).start(priority=1)

    def combine_rows(slot, rows):
        info = info_ref[rows, :]
        y = info[:, 2:3] * buf_a[slot, rows, :] + info[:, 3:4] * buf_b[slot, rows, :]
        z = alpha * x1_ref[rows, :] + mod_ref[0, 5:6, :] * y
        o_ref[rows, :] = _ln(z) * g_ref[...] + b_ref[...]

    @pl.when(i == 0)
    def _():
        lax.fori_loop(0, tm, lambda t, _: (issue_row(0, 0, t), 0)[1], 0, unroll=8)

    slot = i % 2
    for buf in (buf_a, buf_b):
        pltpu.make_async_copy(ys_ref.at[pl.ds(0, tm)], buf.at[slot], sems.at[slot]).wait()

    chunk = 32

    @pl.when(i < last)
    def _():
        for r in range(0, tm, chunk):
            for t in range(r, r + chunk):
                issue_row(i + 1, 1 - slot, t)
            combine_rows(slot, slice(r, r + chunk))

    @pl.when(i == last)
    def _():
        combine_rows(slot, slice(0, tm))


def _combine(pos_a, pos_b, ys, x1, info, mod3, ln_g, ln_b, seq, alpha):
    t, d = x1.shape
    tm = 256
    per_b = seq // tm
    grid_spec = pltpu.PrefetchScalarGridSpec(
        num_scalar_prefetch=2,
        grid=(t // tm,),
        in_specs=[pl.BlockSpec(memory_space=pl.ANY),
                  pl.BlockSpec((tm, d), lambda i, pa, pb: (i, 0)),
                  pl.BlockSpec((tm, LANES), lambda i, pa, pb: (i, 0)),
                  pl.BlockSpec((1, 6, d), lambda i, pa, pb: (i // per_b, 0, 0)),
                  pl.BlockSpec((1, d), lambda i, pa, pb: (0, 0)),
                  pl.BlockSpec((1, d), lambda i, pa, pb: (0, 0))],
        out_specs=pl.BlockSpec((tm, d), lambda i, pa, pb: (i, 0)),
        scratch_shapes=[pltpu.VMEM((2, tm, d), F32), pltpu.VMEM((2, tm, d), F32), pltpu.SemaphoreType.DMA((2,))],
    )
    return pl.pallas_call(
        functools.partial(_combine_kernel, tm=tm, alpha=alpha),
        grid_spec=grid_spec,
        out_shape=jax.ShapeDtypeStruct((t, d), F32),
        compiler_params=_cparams(("arbitrary",), row_dma=True),
        name="combine",
    )(pos_a, pos_b, ys, x1, info, mod3, ln_g, ln_b)


def _prep_w_in_kernel(w_ref, o_ref):
    r0 = MLA_Q_RANK + MLA_KV_RANK
    r1 = r0 + MLA_ROPE_DIM
    half = MLA_ROPE_DIM // 2
    z = jnp.zeros((LANES - MLA_ROPE_DIM, o_ref.shape[1]), BF16)
    o_ref[:r1, :] = w_ref[:r1, :].astype(BF16)
    o_ref[r1:r0 + LANES, :] = z
    o_ref[r0 + LANES:r0 + LANES + half, :] = w_ref[r0 + half:r1, :].astype(BF16)
    o_ref[r0 + LANES + half:r0 + LANES + MLA_ROPE_DIM, :] = w_ref[r0:r0 + half, :].astype(BF16)
    o_ref[r0 + LANES + MLA_ROPE_DIM:PART_A, :] = z
    o_ref[PART_A:, :] = w_ref[r1:, :].astype(BF16)


def _prep_w_in(w_t):
    n, k = w_t.shape
    tk = 512
    n_out = PART_A + 3 * MOBA_WIDTH
    assert n == MLA_Q_RANK + MLA_KV_RANK + MLA_ROPE_DIM + 3 * MOBA_WIDTH
    return pl.pallas_call(
        _prep_w_in_kernel,
        grid=(k // tk,),
        in_specs=[pl.BlockSpec((n, tk), lambda i: (0, i))],
        out_specs=pl.BlockSpec((n_out, tk), lambda i: (0, i)),
        out_shape=jax.ShapeDtypeStruct((n_out, k), BF16),
        compiler_params=_cparams(("arbitrary",)),
        name="prep_w_in",
    )(w_t)


def _prep_w_uq(w):
    r = w.shape[0]
    w = w.reshape(r, MLA_HEADS, MLA_NOPE_DIM + MLA_ROPE_DIM)
    half = MLA_ROPE_DIM // 2
    nope = w[:, :, :MLA_NOPE_DIM]
    x1 = w[:, :, MLA_NOPE_DIM:MLA_NOPE_DIM + half]
    x2 = w[:, :, MLA_NOPE_DIM + half:]
    assert 2 * MLA_ROPE_DIM == LANES
    return jnp.concatenate([nope, x1, x2, x2, x1], axis=2).reshape(r, MLA_HEADS * QK_PAD).astype(BF16)


def _rope_lanes(seq):
    inv = 1.0 / (ROPE_THETA ** (jnp.arange(0, MLA_ROPE_DIM, 2, dtype=F32) / MLA_ROPE_DIM))
    ang = jnp.arange(seq, dtype=F32)[:, None] * inv[None, :]
    cos, sin = jnp.cos(ang), jnp.sin(ang)
    z = jnp.zeros((seq, LANES - MLA_ROPE_DIM), F32)
    return jnp.concatenate([cos, cos, z], axis=1), jnp.concatenate([-sin, sin, z], axis=1)


def _prep_router(w_rg, b_rg, w_re, b_re):
    d = w_rg.shape[0]
    w = jnp.concatenate([w_rg, w_re], axis=1)
    n = w.shape[1]
    hi = w.astype(BF16)
    lo = (w - hi.astype(F32)).astype(BF16)
    z = jnp.zeros((d, LANES // 2 - n), BF16)
    wr = jnp.concatenate([hi, z, lo, z], axis=1)
    br = jnp.zeros((1, LANES), F32).at[0, :n].set(jnp.concatenate([b_rg, b_re]))
    return wr, br


def _layer(xf, mod3, batch, seq, depth_alpha, w_in, q_norm_g, w_uq, kv_norm_g, w_ukv, w_out, bias_tiles,
           cos_t, sin_t, ln1_g, ln1_b, w_rg, b_rg, w_re, b_re, w1, w3, w2, ln2_g, ln2_b):
    t, d = xf.shape
    part_a, qkv_mo = _in_proj(xf, mod3, _prep_w_in(w_in.T), seq)
    q, k, v = _qkv(part_a, _prep_w_uq(w_uq), w_ukv.astype(BF16), q_norm_g.reshape(1, -1),
                   kv_norm_g.reshape(1, -1), cos_t, sin_t, seq)
    o_mla = _mla_attention(q, k, v, batch, seq)
    q_wide = _moba_select(qkv_mo, batch, seq)
    o_moba = _moba_attention(q_wide, qkv_mo, bias_tiles, batch, seq)
    wo = w_out.astype(BF16)
    wr, br = _prep_router(w_rg, b_rg, w_re, b_re)
    x1, h2, logits = _out_proj(o_mla, o_moba, wo, xf, mod3, ln1_g.reshape(1, d),
                               ln1_b.reshape(1, d), wr, seq, depth_alpha)
    info, counts = _route(logits, br)
    tr = 256
    nt = (2 * t) // tr + MOE_N_EXPERTS
    cnt = counts[0, MOE_GROUPS:MOE_GROUPS + MOE_N_EXPERTS].astype(jnp.int32)
    ntile = (cnt + tr - 1) // tr
    tile_end = jnp.cumsum(ntile)
    tile_start = tile_end - ntile
    used = tile_end[-1]
    start_lanes = jnp.zeros((1, LANES), F32).at[0, MOE_GROUPS:MOE_GROUPS + MOE_N_EXPERTS].set(
        (tile_start * tr).astype(F32))
    pos = _positions(info, start_lanes)
    pos_a, pos_b = pos[0], pos[1]
    tidx = jnp.minimum(jnp.arange(nt, dtype=jnp.int32), used - 1)
    texp = jnp.sum(tidx[:, None] >= tile_end[None, :], axis=1).astype(jnp.int32)
    eids = jnp.arange(MOE_N_EXPERTS, dtype=jnp.int32)
    later = (eids[None, :] > eids[:, None]) & (ntile[None, :] > 0)
    next_nonempty = jnp.where(later.any(axis=1), jnp.argmax(later, axis=1), -1).astype(jnp.int32)
    nexte = jnp.sum(jnp.where(texp[:, None] == eids[None, :], next_nonempty[None, :], 0), axis=1).astype(jnp.int32)
    used1 = used.reshape(1).astype(jnp.int32)
    pad0 = (tile_start * tr + cnt).astype(jnp.int32)
    padn = (ntile * tr - cnt).astype(jnp.int32)
    xs = _dispatch(pos_a, pos_b, pad0, padn, used1, h2, tr, nt)
    ys = _experts(texp, tidx, nexte, used1, xs, w1, w3, w2, tr)
    return _combine(pos_a, pos_b, ys, x1, info, mod3, ln2_g.reshape(1, d), ln2_b.reshape(1, d), seq, depth_alpha)


def kernel(x, c, w_ada, b_ada, w_in, q_norm_g, w_uq, kv_norm_g, w_ukv, w_out, t5_table, ln1_g, ln1_b,
           w_router_group, b_router_group, w_router_expert, b_router_expert, w1, w3, w2, ln2_g, ln2_b):
    batch, seq, d = x.shape
    depth = w_ada.shape[0]
    alpha = (2.0 * depth) ** 0.25
    cos_t, sin_t = _rope_lanes(seq)
    bias_tiles = _t5_tiles(t5_table)
    xf = x.reshape(batch * seq, d)
    for l in range(depth):
        mod3 = _ada_mod(c, w_ada[l], b_ada[l]).reshape(batch, 6, d)
        xf = _layer(xf, mod3, batch, seq, alpha, w_in[l], q_norm_g[l], w_uq[l], kv_norm_g[l], w_ukv[l],
                    w_out[l], bias_tiles, cos_t, sin_t, ln1_g[l], ln1_b[l], w_router_group[l],
                    b_router_group[l], w_router_expert[l], b_router_expert[l], w1[l], w3[l], w2[l],
                    ln2_g[l], ln2_b[l])
    return xf.reshape(batch, seq, d)
```

```python
import functools
import math

import jax
import jax.numpy as jnp
from jax import lax
from jax.experimental import pallas as pl
from jax.experimental.pallas import tpu as pltpu

D_MODEL = 2048
MLA_HEADS = 8
MLA_Q_RANK = 512
MLA_KV_RANK = 256
MLA_NOPE_DIM = 128
MLA_ROPE_DIM = 64
MLA_V_DIM = 128
ROPE_THETA = 10000.0
MOBA_HEADS = 8
MOBA_HEAD_DIM = 128
MOBA_BLOCK = 256
MOBA_TOPK = 3
T5_BUCKETS = 32
T5_MAX_DISTANCE = 128
MOE_GROUPS = 4
MOE_EXPERTS_PER_GROUP = 8
MOE_N_EXPERTS = MOE_GROUPS * MOE_EXPERTS_PER_GROUP
MOE_D_FF = 512
LN_EPS = 1e-5
RMS_EPS = 1e-6
MOBA_WIDTH = MOBA_HEADS * MOBA_HEAD_DIM

LANES = 128
SUBLANES = 8
QK_PAD = 256
V_PAD = 256
PART_A = 1024
NEG = -1e30
VMEM_LIMIT = 56 * 1024 * 1024

F32 = jnp.float32
BF16 = jnp.bfloat16


def _cparams(sem, row_dma=False):
    return pltpu.CompilerParams(dimension_semantics=sem, vmem_limit_bytes=VMEM_LIMIT,
                                disable_bounds_checks=row_dma)


def _ln(x):
    mu = jnp.mean(x, axis=-1, keepdims=True)
    xc = x - mu
    var = jnp.mean(xc * xc, axis=-1, keepdims=True)
    return xc * lax.rsqrt(var + LN_EPS)


def _nt_dot(a, b):
    return lax.dot_general(a, b, (((1,), (1,)), ((), ())), preferred_element_type=F32)


def _ada_kernel(ct_ref, w_ref, b_ref, o_ref, *, batch):
    ct = ct_ref[...]
    ca = ct / (1.0 + jnp.exp(-ct))
    w = w_ref[...]
    rows = [jnp.sum(w * ca[:, b:b + 1], axis=0, keepdims=True) for b in range(batch)]
    o_ref[...] = jnp.concatenate(rows, axis=0) + b_ref[...]


def _ada_mod(c, w_ada, b_ada):
    batch, d = c.shape
    n = w_ada.shape[1]
    tn = 1024
    ct = jnp.zeros((d, LANES), F32).at[:, :batch].set(c.T)
    return pl.pallas_call(
        functools.partial(_ada_kernel, batch=batch),
        grid=(n // tn,),
        in_specs=[pl.BlockSpec((d, LANES), lambda j: (0, 0)),
                  pl.BlockSpec((d, tn), lambda j: (0, j)),
                  pl.BlockSpec((1, tn), lambda j: (0, j))],
        out_specs=pl.BlockSpec((batch, tn), lambda j: (0, j)),
        out_shape=jax.ShapeDtypeStruct((batch, n), F32),
        compiler_params=_cparams(("arbitrary",)),
        name="ada_mod",
    )(ct, w_ada, b_ada.reshape(1, n))


def _inproj_kernel(x_ref, mod_ref, w_ref, a_ref, b_ref, h_scr):
    j = pl.program_id(1)

    @pl.when(j == 0)
    def _():
        rows = 256
        for r in range(0, x_ref.shape[0], rows):
            h = _ln(x_ref[r:r + rows, :]) * (1.0 + mod_ref[0, 1:2, :]) + mod_ref[0, 0:1, :]
            h_scr[r:r + rows, :] = h.astype(BF16)
        a_ref[...] = _nt_dot(h_scr[...], w_ref[...])

    @pl.when(j > 0)
    def _():
        b_ref[...] = _nt_dot(h_scr[...], w_ref[...]).astype(BF16)


def _in_proj(xf, mod3, w4, seq):
    t, d = xf.shape
    n = w4.shape[0]
    tm, tn = 1024, PART_A
    per_b = seq // tm
    return pl.pallas_call(
        _inproj_kernel,
        grid=(t // tm, n // tn),
        in_specs=[pl.BlockSpec((tm, d), lambda i, j: (i, 0)),
                  pl.BlockSpec((1, 6, d), lambda i, j: (i // per_b, 0, 0)),
                  pl.BlockSpec((tn, d), lambda i, j: (j, 0))],
        out_specs=[pl.BlockSpec((tm, tn), lambda i, j: (i, 0)),
                   pl.BlockSpec((tm, tn), lambda i, j: (i, jnp.maximum(j - 1, 0)))],
        out_shape=[jax.ShapeDtypeStruct((t, PART_A), F32),
                   jax.ShapeDtypeStruct((t, n - PART_A), BF16)],
        scratch_shapes=[pltpu.VMEM((tm, d), BF16)],
        compiler_params=_cparams(("arbitrary", "arbitrary")),
        name="in_proj",
    )(xf, mod3, w4)


def _qkv_kernel(a_ref, wq_ref, wkv_ref, gq_ref, gkv_ref, cos_ref, sin_ref, q_ref, k_ref, v_ref):
    a = a_ref[...]
    cq = a[:, :MLA_Q_RANK]
    ckv = a[:, MLA_Q_RANK:MLA_Q_RANK + MLA_KV_RANK]
    kr = a[:, 768:896]
    krs = a[:, 896:1024]
    cqn = (cq * lax.rsqrt(jnp.mean(cq * cq, axis=-1, keepdims=True) + RMS_EPS) * gq_ref[...]).astype(BF16)
    ckvn = (ckv * lax.rsqrt(jnp.mean(ckv * ckv, axis=-1, keepdims=True) + RMS_EPS) * gkv_ref[...]).astype(BF16)
    cos = cos_ref[...]
    sin = sin_ref[...]
    krr = (kr * cos + krs * sin).astype(BF16)
    for h in range(MLA_HEADS):
        c0 = h * QK_PAD
        q2 = jnp.dot(cqn, wq_ref[:, c0:c0 + QK_PAD], preferred_element_type=F32)
        kv = jnp.dot(ckvn, wkv_ref[:, c0:c0 + QK_PAD], preferred_element_type=F32)
        rope = q2[:, LANES:]
        q_ref[:, c0:c0 + LANES] = q2[:, :LANES].astype(BF16)
        q_ref[:, c0 + LANES:c0 + QK_PAD] = (rope * cos + pltpu.roll(rope, LANES // 2, 1) * sin).astype(BF16)
        k_ref[:, c0:c0 + LANES] = kv[:, :LANES].astype(BF16)
        k_ref[:, c0 + LANES:c0 + QK_PAD] = krr
        v_ref[:, h * LANES:(h + 1) * LANES] = kv[:, LANES:].astype(BF16)


def _qkv(part_a, wq3, wkv, gq, gkv, cos_t, sin_t, seq):
    t = part_a.shape[0]
    tm = 512
    per_b = seq // tm
    hq = MLA_HEADS * QK_PAD
    return pl.pallas_call(
        _qkv_kernel,
        grid=(t // tm,),
        in_specs=[pl.BlockSpec((tm, PART_A), lambda i: (i, 0)),
                  pl.BlockSpec(wq3.shape, lambda i: (0, 0)),
                  pl.BlockSpec(wkv.shape, lambda i: (0, 0)),
                  pl.BlockSpec((1, MLA_Q_RANK), lambda i: (0, 0)),
                  pl.BlockSpec((1, MLA_KV_RANK), lambda i: (0, 0)),
                  pl.BlockSpec((tm, LANES), lambda i: (i % per_b, 0)),
                  pl.BlockSpec((tm, LANES), lambda i: (i % per_b, 0))],
        out_specs=[pl.BlockSpec((tm, hq), lambda i: (i, 0)),
                   pl.BlockSpec((tm, hq), lambda i: (i, 0)),
                   pl.BlockSpec((tm, MLA_HEADS * MLA_V_DIM), lambda i: (i, 0))],
        out_shape=[jax.ShapeDtypeStruct((t, hq), BF16),
                   jax.ShapeDtypeStruct((t, hq), BF16),
                   jax.ShapeDtypeStruct((t, MLA_HEADS * MLA_V_DIM), BF16)],
        compiler_params=_cparams(("arbitrary",)),
        name="qkv",
    )(part_a, wq3, wkv, gq, gkv, cos_t, sin_t)


ATTN_TILE = 512
ATTN_CHAINS = 8


def _attn_kernel(*refs, c, masked):
    nch = ATTN_CHAINS
    scratch = refs[-4 * nch:]
    tile = ATTN_TILE
    i = pl.program_id(2)
    if masked:
        q_ref, kin_ref, vin_ref, bias_ref, o_ref, v_ref, k_ref = refs[:7]
    else:
        q_ref, k_ref, vin_ref, o_ref, v_ref = refs[:5]

    @pl.when((pl.program_id(0) == 0) & (pl.program_id(1) == 0) & (i == 0))
    def _():
        seq = vin_ref.shape[0]
        v_ref[:, LANES:] = jnp.ones((seq, V_PAD - LANES), BF16)
        if masked:
            lane = lax.broadcasted_iota(jnp.int32, (seq, LANES), 1)
            own = jnp.right_shift(lax.broadcasted_iota(jnp.int32, (seq, LANES), 0), MOBA_BLOCK.bit_length() - 1)
            k_ref[:, LANES:] = jnp.where(lane == own, 1.0, 0.0).astype(BF16)

    @pl.when(i == 0)
    def _():
        v_ref[:, :LANES] = vin_ref[...]
        if masked:
            k_ref[:, :LANES] = kin_ref[...]

    chains = [dict(rows=slice(n * tile, (n + 1) * tile), s=scratch[4 * n:4 * n + 2], m=scratch[4 * n + 2],
                   acc=scratch[4 * n + 3]) for n in range(nch)]

    half = tile // 2

    def put_scores(chain, slot, t, diag=False):
        k0 = pl.multiple_of(t * tile, tile)
        q0 = chain["rows"].start
        buf = chain["s"][slot]
        if diag:
            buf[:half, :half] = _nt_dot(q_ref[q0:q0 + half, :], k_ref[pl.ds(k0, half), :])
            buf[half:, :] = _nt_dot(q_ref[q0 + half:q0 + tile, :], k_ref[pl.ds(k0, tile), :])
        else:
            buf[...] = _nt_dot(q_ref[q0:q0 + tile, :], k_ref[pl.ds(k0, tile), :])

    def softmax_update(chain, s, t, rows, keys):
        m_scr, acc_scr = chain["m"], chain["acc"]
        m_old = m_scr[rows, :]
        m_new = jnp.maximum(m_old, jnp.broadcast_to(jnp.max(s, axis=-1, keepdims=True), m_old.shape))
        alpha = jnp.exp2((m_old - m_new) * c)
        p = jnp.concatenate([jnp.exp2(((s[:, j * LANES:(j + 1) * LANES] - m_new) * c).astype(BF16))
                             for j in range(keys // LANES)], axis=1)
        v = v_ref[pl.ds(pl.multiple_of(t * tile, tile), keys), :]
        pv = jnp.dot(p, v, preferred_element_type=F32)
        acc_scr[rows, :] = jnp.concatenate([alpha, alpha], axis=1) * acc_scr[rows, :] + pv
        m_scr[rows, :] = m_new

    def tile_step(chain, slot, t, kind, prefetch=True, next_diag=False):
        if prefetch:
            put_scores(chain, 1 - slot, t + 1, diag=next_diag)
        buf = chain["s"][slot]
        if kind != "diag":
            s = buf[...]
            if masked and kind == "prev":
                s = s + bias_ref[0, 1]
            softmax_update(chain, s, t, slice(0, tile), tile)
            return
        for r0, keys in ((0, half), (half, tile)):
            rows = slice(r0, r0 + half)
            s = buf[rows, :keys]
            if masked:
                s = s + bias_ref[0, 0, rows, :keys]
            else:
                row = lax.broadcasted_iota(jnp.int32, (half, keys), 0) + r0
                col = lax.broadcasted_iota(jnp.int32, (half, keys), 1)
                s = jnp.where(row >= col, s, NEG)
            softmax_update(chain, s, t, rows, keys)

    for chain in chains:
        chain["m"][...] = jnp.full(chain["m"].shape, NEG, F32)
        chain["acc"][...] = jnp.zeros(chain["acc"].shape, F32)
        put_scores(chain, 0, 0)

    def far_pair(j, _):
        for slot in range(2):
            for chain in chains:
                tile_step(chain, slot, 2 * j + slot, "far")
        return 0

    first = nch * i
    if masked:
        lax.fori_loop(0, jnp.maximum(first // 2 - 1, 0), far_pair, 0)

        @pl.when(i > 0)
        def _():
            for chain in chains:
                tile_step(chain, 0, first - 2, "far")
            for n, chain in enumerate(chains):
                tile_step(chain, 1, first - 1, "prev" if n == 0 else "far", next_diag=(n == 0))
    else:
        lax.fori_loop(0, first // 2, far_pair, 0)

    for k in range(nch):
        for n, chain in enumerate(chains):
            if n < k:
                continue
            kind = "diag" if n == k else ("prev" if masked and n == k + 1 else "far")
            tile_step(chain, k % 2, first + k, kind, prefetch=(n != k), next_diag=(n == k + 1))

    for chain in chains:
        acc = chain["acc"]
        o_ref[chain["rows"], :] = (acc[:, :LANES] / acc[:, LANES:]).astype(o_ref.dtype)


def _attn_scratch(seq, masked):
    tile = ATTN_TILE
    wide = [pltpu.VMEM((seq, V_PAD), BF16)] + ([pltpu.VMEM((seq, QK_PAD), BF16)] if masked else [])
    per_chain = [pltpu.VMEM((tile, tile), F32), pltpu.VMEM((tile, tile), F32),
                 pltpu.VMEM((tile, LANES), F32), pltpu.VMEM((tile, V_PAD), F32)]
    return wide + per_chain * ATTN_CHAINS


def _mla_attention(q, k, v, batch, seq):
    tile = ATTN_CHAINS * ATTN_TILE
    nq = seq // tile
    c = math.log2(math.e) / math.sqrt(MLA_NOPE_DIM + MLA_ROPE_DIM)
    return pl.pallas_call(
        functools.partial(_attn_kernel, c=c, masked=False),
        grid=(batch, MLA_HEADS, nq),
        in_specs=[pl.BlockSpec((tile, QK_PAD), lambda b, h, i: (b * nq + i, h)),
                  pl.BlockSpec((seq, QK_PAD), lambda b, h, i: (b, h)),
                  pl.BlockSpec((seq, MLA_V_DIM), lambda b, h, i: (b, h))],
        out_specs=pl.BlockSpec((tile, MLA_V_DIM), lambda b, h, i: (b * nq + i, h)),
        out_shape=jax.ShapeDtypeStruct((batch * seq, MLA_HEADS * MLA_V_DIM), BF16),
        scratch_shapes=_attn_scratch(seq, False),
        compiler_params=_cparams(("arbitrary", "arbitrary", "arbitrary")),
        name="mla_attn",
    )(q, k, v)


def _moba_select_kernel(q_ref, k_ref, qa_ref, *, seq, nb):
    kf = k_ref[...].astype(F32)
    km = jnp.sum(kf.reshape(nb, MOBA_BLOCK, MOBA_HEAD_DIM), axis=1) * (1.0 / MOBA_BLOCK)
    km_hi = km.astype(BF16)
    km_lo = (km - km_hi.astype(F32)).astype(BF16)
    q = q_ref[...]
    gate = _nt_dot(km_hi, q) + _nt_dot(km_lo, q)
    shift = MOBA_BLOCK.bit_length() - 1
    blk = lax.broadcasted_iota(jnp.int32, (nb, seq), 0)
    qblk = jnp.right_shift(lax.broadcasted_iota(jnp.int32, (nb, seq), 1), shift)
    g = jnp.where(blk < qblk, gate, NEG)
    visible = blk == qblk
    for _ in range(MOBA_TOPK):
        mx = jnp.max(g, axis=0, keepdims=True)
        first = jnp.min(jnp.where(g == mx, blk, nb), axis=0, keepdims=True)
        pick = (blk == first) & (mx > 0.5 * NEG)
        visible = visible | pick
        g = jnp.where(pick, NEG, g)
    mask_t = jnp.concatenate([jnp.where(visible, 0.0, NEG), jnp.zeros((LANES - nb, seq), F32)], axis=0)
    qa_ref[:, :MOBA_HEAD_DIM] = q
    qa_ref[:, MOBA_HEAD_DIM:] = mask_t.T.astype(BF16)


def _moba_select(qkv_mo, batch, seq):
    nb = seq // MOBA_BLOCK
    assert MOBA_HEAD_DIM == LANES and nb <= QK_PAD - MOBA_HEAD_DIM
    return pl.pallas_call(
        functools.partial(_moba_select_kernel, seq=seq, nb=nb),
        grid=(batch, MOBA_HEADS),
        in_specs=[pl.BlockSpec((seq, MOBA_HEAD_DIM), lambda b, h: (b, h)),
                  pl.BlockSpec((seq, MOBA_HEAD_DIM), lambda b, h: (b, MOBA_HEADS + h))],
        out_specs=pl.BlockSpec((seq, QK_PAD), lambda b, h: (b, h)),
        out_shape=jax.ShapeDtypeStruct((batch * seq, MOBA_HEADS * QK_PAD), BF16),
        compiler_params=_cparams(("arbitrary", "arbitrary")),
        name="moba_select",
    )(qkv_mo, qkv_mo)


def _t5_kernel(tab_ref, o_ref, *, inv_scale):
    h = pl.program_id(0)
    r = lax.broadcasted_iota(jnp.int32, (LANES, LANES), 0)
    c = lax.broadcasted_iota(jnp.int32, (LANES, LANES), 1)
    max_exact = T5_BUCKETS // 2
    far = tab_ref[T5_BUCKETS - 1, h]

    def block(offset):
        rel = offset + r - c
        n = jnp.maximum(rel, 0)
        nf = jnp.maximum(n, 1).astype(F32)
        large = max_exact + (jnp.log(nf / max_exact) / math.log(T5_MAX_DISTANCE / max_exact)
                             * (T5_BUCKETS - max_exact)).astype(jnp.int32)
        large = jnp.minimum(large, T5_BUCKETS - 1)
        bucket = jnp.where(n < max_exact, n, large)
        bias = jnp.zeros((LANES, LANES), F32)
        for j in range(T5_BUCKETS):
            bias = jnp.where(bucket == j, tab_ref[j, h], bias)
        return jnp.where(rel >= 0, (bias - far) * inv_scale, NEG)

    near = {0: block(0), 1: block(LANES)}
    nblk = ATTN_TILE // LANES
    for d in range(2):
        for i in range(nblk):
            for j in range(nblk):
                k = d * nblk + i - j
                if k < 0:
                    val = jnp.full((LANES, LANES), NEG, F32)
                else:
                    val = near.get(k, jnp.zeros((LANES, LANES), F32))
                o_ref[0, d, i * LANES:(i + 1) * LANES, j * LANES:(j + 1) * LANES] = val


def _t5_tiles(t5_table):
    assert LANES >= T5_MAX_DISTANCE
    tile = ATTN_TILE
    return pl.pallas_call(
        functools.partial(_t5_kernel, inv_scale=math.sqrt(MOBA_HEAD_DIM)),
        grid=(MOBA_HEADS,),
        in_specs=[pl.BlockSpec(memory_space=pltpu.SMEM)],
        out_specs=pl.BlockSpec((1, 2, tile, tile), lambda h: (h, 0, 0, 0)),
        out_shape=jax.ShapeDtypeStruct((MOBA_HEADS, 2, tile, tile), F32),
        compiler_params=_cparams(("arbitrary",)),
        name="t5_tiles",
    )(t5_table)


def _moba_attention(q_wide, qkv_mo, bias, batch, seq):
    tile = ATTN_CHAINS * ATTN_TILE
    nq = seq // tile
    c = math.log2(math.e) / math.sqrt(MOBA_HEAD_DIM)
    dh = MOBA_HEAD_DIM
    return pl.pallas_call(
        functools.partial(_attn_kernel, c=c, masked=True),
        grid=(batch, MOBA_HEADS, nq),
        in_specs=[pl.BlockSpec((tile, QK_PAD), lambda b, h, i: (b * nq + i, h)),
                  pl.BlockSpec((seq, dh), lambda b, h, i: (b, MOBA_HEADS + h)),
                  pl.BlockSpec((seq, dh), lambda b, h, i: (b, 2 * MOBA_HEADS + h)),
                  pl.BlockSpec((1, 2, ATTN_TILE, ATTN_TILE), lambda b, h, i: (h, 0, 0, 0))],
        out_specs=pl.BlockSpec((tile, dh), lambda b, h, i: (b * nq + i, h)),
        out_shape=jax.ShapeDtypeStruct((batch * seq, MOBA_WIDTH), BF16),
        scratch_shapes=_attn_scratch(seq, True),
        compiler_params=_cparams(("arbitrary", "arbitrary", "arbitrary")),
        name="moba_attn",
    )(q_wide, qkv_mo, qkv_mo, bias)


def _outproj_kernel(oa_ref, ob_ref, wa_ref, wb_ref, x_ref, mod_ref, g_ref, b_ref, wr_ref,
                    x1_ref, h2_ref, lg_ref, y_a, y_b, *, alpha, n_tiles):
    i = pl.program_id(0)
    tm, d = y_a.shape
    chunks = 4
    cn, cr = d // chunks, tm // chunks

    def matmul_into(y_ref, c):
        cols = slice(c * cn, (c + 1) * cn)
        y_ref[:, cols] = (jnp.dot(oa_ref[...], wa_ref[:, cols], preferred_element_type=F32)
                          + jnp.dot(ob_ref[...], wb_ref[:, cols], preferred_element_type=F32))

    def epilogue(y_ref, c):
        rows = slice(c * cr, (c + 1) * cr)
        z = alpha * x_ref[rows, :] + mod_ref[0, 2:3, :] * y_ref[rows, :]
        x1 = _ln(z) * g_ref[...] + b_ref[...]
        x1_ref[rows, :] = x1
        h2 = _ln(x1) * (1.0 + mod_ref[0, 4:5, :]) + mod_ref[0, 3:4, :]
        h_hi = h2.astype(BF16)
        h_lo = (h2 - h_hi.astype(F32)).astype(BF16)
        h_bits = pltpu.bitcast(h_hi.astype(F32), jnp.uint32)
        h2_ref[rows, :] = h_bits[:, :d // 2] | jnp.right_shift(h_bits[:, d // 2:], 16)
        zz = (jnp.dot(h_hi, wr_ref[...], preferred_element_type=F32)
              + jnp.dot(h_lo, wr_ref[...], preferred_element_type=F32))
        lg_ref[rows, :] = zz + pltpu.roll(zz, LANES // 2, 1)

    def step(y_new, y_old):
        for c in range(chunks):
            if y_new is not None:
                matmul_into(y_new, c)
            if y_old is not None:
                epilogue(y_old, c)

    inner = (i > 0) & (i < n_tiles)
    pl.when(i == 0)(lambda: step(y_a, None))
    pl.when(inner & (i % 2 == 0))(lambda: step(y_a, y_b))
    pl.when(inner & (i % 2 == 1))(lambda: step(y_b, y_a))
    pl.when(i == n_tiles)(lambda: step(None, y_b if n_tiles % 2 == 0 else y_a))


def _out_proj(o_mla, o_moba, wo, xf, mod3, ln_g, ln_b, wr, seq, alpha):
    t, d = xf.shape
    tm = 512
    n = t // tm
    per_b = seq // tm
    ka, kb = o_mla.shape[1], o_moba.shape[1]
    assert ka == kb and wo.shape[0] == ka + kb
    once = pl.Buffered(1)

    def cur(i):
        return (jnp.minimum(i, n - 1), 0)

    def lag(i):
        return (jnp.maximum(i - 1, 0), 0)

    return pl.pallas_call(
        functools.partial(_outproj_kernel, alpha=alpha, n_tiles=n),
        grid=(n + 1,),
        in_specs=[pl.BlockSpec((tm, ka), cur),
                  pl.BlockSpec((tm, kb), cur),
                  pl.BlockSpec((ka, d), lambda i: (0, 0), pipeline_mode=once),
                  pl.BlockSpec((kb, d), lambda i: (1, 0), pipeline_mode=once),
                  pl.BlockSpec((tm, d), lag),
                  pl.BlockSpec((1, 6, d), lambda i: (jnp.maximum(i - 1, 0) // per_b, 0, 0)),
                  pl.BlockSpec((1, d), lambda i: (0, 0)),
                  pl.BlockSpec((1, d), lambda i: (0, 0)),
                  pl.BlockSpec((d, LANES), lambda i: (0, 0), pipeline_mode=once)],
        out_specs=[pl.BlockSpec((tm, d), lag),
                   pl.BlockSpec((tm, d // 2), lag),
                   pl.BlockSpec((tm, LANES), lag)],
        out_shape=[jax.ShapeDtypeStruct((t, d), F32),
                   jax.ShapeDtypeStruct((t, d // 2), jnp.uint32),
                   jax.ShapeDtypeStruct((t, LANES), F32)],
        scratch_shapes=[pltpu.VMEM((tm, d), F32), pltpu.VMEM((tm, d), F32)],
        compiler_params=_cparams(("arbitrary",)),
        name="out_proj",
    )(o_mla, o_moba, wo, wo, xf, mod3, ln_g, ln_b, wr)


def _route_kernel(lg_ref, br_ref, info_ref, cnt_ref, run_scr, *, tm):
    i = pl.program_id(0)

    @pl.when(i == 0)
    def _():
        run_scr[...] = jnp.zeros_like(run_scr)

    lg = lg_ref[...] + br_ref[...]
    lane = lax.broadcasted_iota(jnp.int32, (tm, LANES), 1)
    e_lo, e_hi = MOE_GROUPS, MOE_GROUPS + MOE_N_EXPERTS
    is_g = lane < e_lo
    gl = jnp.where(is_g, lg, NEG)
    gmax = jnp.max(gl, axis=-1, keepdims=True)
    gidx = jnp.min(jnp.where(gl == gmax, lane, LANES), axis=-1, keepdims=True)
    g_p = 1.0 / jnp.sum(jnp.where(is_g, jnp.exp(gl - gmax), 0.0), axis=-1, keepdims=True)
    grp_of_lane = jnp.right_shift(lane - e_lo, MOE_EXPERTS_PER_GROUP.bit_length() - 1)
    in_grp = (lane >= e_lo) & (lane < e_hi) & (grp_of_lane == gidx)
    el = jnp.where(in_grp, lg, NEG)
    m1 = jnp.max(el, axis=-1, keepdims=True)
    l1 = jnp.min(jnp.where(el == m1, lane, LANES), axis=-1, keepdims=True)
    el2 = jnp.where(lane == l1, NEG, el)
    m2 = jnp.max(el2, axis=-1, keepdims=True)
    l2 = jnp.min(jnp.where(el2 == m2, lane, LANES), axis=-1, keepdims=True)
    zsum = jnp.sum(jnp.where(in_grp, jnp.exp(el - m1), 0.0), axis=-1, keepdims=True)
    p1 = 1.0 / zsum
    p2 = jnp.exp(m2 - m1) / zsum
    wa = g_p * (p1 / (p1 + p2))
    wb = g_p * (p2 / (p1 + p2))
    hot_a = lane == l1
    hot_b = lane == l2
    onehot = jnp.where(hot_a | hot_b, 1.0, 0.0)
    r = lax.broadcasted_iota(jnp.int32, (tm, tm), 0)
    c = lax.broadcasted_iota(jnp.int32, (tm, tm), 1)
    lower = jnp.where(c < r, 1.0, 0.0).astype(BF16)
    before = jnp.dot(lower, onehot.astype(BF16), preferred_element_type=F32) + run_scr[...]
    rank_a = jnp.sum(jnp.where(hot_a, before, 0.0), axis=-1, keepdims=True)
    rank_b = jnp.sum(jnp.where(hot_b, before, 0.0), axis=-1, keepdims=True)
    run_scr[...] += jnp.sum(onehot, axis=0, keepdims=True)
    info = jnp.zeros((tm, LANES), F32)
    for k, val in enumerate([(l1 - e_lo).astype(F32), (l2 - e_lo).astype(F32), wa, wb, rank_a, rank_b]):
        info = jnp.where(lane == k, val, info)
    info_ref[...] = info
    cnt_ref[...] = run_scr[...]


def _route(logits, br):
    t = logits.shape[0]
    tm = 512
    return pl.pallas_call(
        functools.partial(_route_kernel, tm=tm),
        grid=(t // tm,),
        in_specs=[pl.BlockSpec((tm, LANES), lambda i: (i, 0)),
                  pl.BlockSpec((1, LANES), lambda i: (0, 0))],
        out_specs=[pl.BlockSpec((tm, LANES), lambda i: (i, 0)),
                   pl.BlockSpec((1, LANES), lambda i: (0, 0))],
        out_shape=[jax.ShapeDtypeStruct((t, LANES), F32),
                   jax.ShapeDtypeStruct((1, LANES), F32)],
        scratch_shapes=[pltpu.VMEM((1, LANES), F32)],
        compiler_params=_cparams(("arbitrary",)),
        name="route",
    )(logits, br)


def _pos_kernel(info_ref, start_ref, pos_ref):
    info = info_ref[...]
    tm = info.shape[0]
    lane = lax.broadcasted_iota(jnp.int32, (tm, LANES), 1)
    start = start_ref[...]
    cols = []
    for k in range(2):
        e = jnp.sum(jnp.where(lane == k, info, 0.0), axis=-1, keepdims=True).astype(jnp.int32)
        rank = jnp.sum(jnp.where(lane == 4 + k, info, 0.0), axis=-1, keepdims=True)
        base = jnp.sum(jnp.where(lane == e + MOE_GROUPS, start, 0.0), axis=-1, keepdims=True)
        cols.append(base + rank)
    wide = jnp.where(lane == 0, cols[0], jnp.where(lane == 1, cols[1], 0.0))
    pos_ref[...] = wide.T[:pos_ref.shape[0], :].astype(jnp.int32)


def _positions(info, start_lanes):
    t = info.shape[0]
    tm = 1024
    return pl.pallas_call(
        _pos_kernel,
        grid=(t // tm,),
        in_specs=[pl.BlockSpec((tm, LANES), lambda i: (i, 0)),
                  pl.BlockSpec((1, LANES), lambda i: (0, 0))],
        out_specs=pl.BlockSpec((8, tm), lambda i: (0, i)),
        out_shape=jax.ShapeDtypeStruct((8, t), jnp.int32),
        compiler_params=_cparams(("arbitrary",)),
        name="positions",
    )(info, start_lanes)


def _row_copy(src_ref, src_row, dst_ref, dst_row, sem):
    return pltpu.make_async_copy(src_ref.at[pl.ds(src_row, 1)], dst_ref.at[pl.ds(dst_row, 1)], sem)


def _dispatch_kernel(pa_ref, pb_ref, pad0_ref, padn_ref, used_ref, h_ref, xs_ref, ztile, sem, zsem,
                     *, tm, tr, n_tiles):
    i = pl.program_id(0)
    base = i * tm

    def zero_fill(act):
        def whole_tile(j, _):
            act(pltpu.make_async_copy(ztile, xs_ref.at[pl.ds(pl.multiple_of(j * tr, tr), tr)], zsem))
            return 0

        lax.fori_loop(used_ref[0], n_tiles, whole_tile, 0)

        def expert_pad(e, _):
            n, start = padn_ref[e], pad0_ref[e]
            head = jnp.minimum(jnp.bitwise_and(-start, SUBLANES - 1), n)
            for k in range(SUBLANES - 1):
                pl.when(k < head)(functools.partial(act, _row_copy(ztile, 0, xs_ref, start + k, zsem)))
            off = start + head
            groups = jnp.right_shift(n - head, SUBLANES.bit_length() - 1)
            for bit in reversed(range((tr // SUBLANES - 1).bit_length())):
                size = SUBLANES << bit
                take = jnp.bitwise_and(jnp.right_shift(groups, bit), 1)
                dst = xs_ref.at[pl.ds(pl.multiple_of(off, SUBLANES), size)]
                pl.when(take == 1)(functools.partial(act, pltpu.make_async_copy(ztile.at[pl.ds(0, size)], dst, zsem)))
                off = off + take * size
            return 0

        lax.fori_loop(0, MOE_N_EXPERTS, expert_pad, 0)

    @pl.when(i == 0)
    def _():
        ztile[...] = jnp.zeros(ztile.shape, ztile.dtype)
        zero_fill(lambda cp: cp.start())

    for t in range(tm):
        _row_copy(h_ref, t, xs_ref, pa_ref[base + t], sem).start()
        _row_copy(h_ref, t, xs_ref, pb_ref[base + t], sem).start(priority=1)
    for _ in range(2):
        pltpu.make_async_copy(h_ref, xs_ref.at[pl.ds(0, tm)], sem).wait()

    @pl.when(i == pl.num_programs(0) - 1)
    def _():
        zero_fill(lambda cp: cp.wait())


def _dispatch(pos_a, pos_b, pad0, padn, used, h2, tr, n_tiles):
    t, d = h2.shape
    tm = 256
    grid_spec = pltpu.PrefetchScalarGridSpec(
        num_scalar_prefetch=5,
        grid=(t // tm,),
        in_specs=[pl.BlockSpec((tm, d), lambda i, *_: (i, 0))],
        out_specs=pl.BlockSpec(memory_space=pl.ANY),
        scratch_shapes=[pltpu.VMEM((tr, d), h2.dtype), pltpu.SemaphoreType.DMA(()), pltpu.SemaphoreType.DMA(())],
    )
    return pl.pallas_call(
        functools.partial(_dispatch_kernel, tm=tm, tr=tr, n_tiles=n_tiles),
        grid_spec=grid_spec,
        out_shape=jax.ShapeDtypeStruct((n_tiles * tr, d), h2.dtype),
        compiler_params=_cparams(("arbitrary",), row_dma=True),
        name="dispatch",
    )(pos_a, pos_b, pad0, padn, used, h2)


def _experts_kernel(texp_ref, tidx_ref, fetch_ref, slot_ref, used_ref, x_ref, w1_hbm, w3_hbm, w2_hbm, o_ref,
                    w1_f32, w3_f32, w2_f32, w1_scr, w3_scr, w2_scr, sems):
    i = pl.program_id(0)
    n_steps = pl.num_programs(0)
    prev = texp_ref[jnp.maximum(i - 1, 0)]
    first_of_expert = (i == 0) | (texp_ref[i] != prev)
    slot = slot_ref[i]

    def weight_copies(e, s):
        return [pltpu.make_async_copy(w1_hbm.at[e], w1_f32.at[s], sems.at[s, 0]),
                pltpu.make_async_copy(w3_hbm.at[e], w3_f32.at[s], sems.at[s, 1]),
                pltpu.make_async_copy(w2_hbm.at[e], w2_f32.at[s], sems.at[s, 2])]

    @pl.when(i == 0)
    def _():
        for s in range(2):
            for cp in weight_copies(fetch_ref[n_steps + s], s):
                cp.start(priority=1)

    def tile(cast):
        bits = x_ref[...]
        x = jnp.concatenate(
            [pltpu.bitcast(jnp.bitwise_and(bits, jnp.uint32(0xFFFF0000)), F32).astype(BF16),
             pltpu.bitcast(jnp.left_shift(bits, 16), F32).astype(BF16)], axis=1)
        if cast:
            w1_scr[...] = w1_f32[slot].astype(BF16)
        a = jnp.dot(x, w1_scr[...], preferred_element_type=F32)
        if cast:
            w3_scr[...] = w3_f32[slot].astype(BF16)
        b = jnp.dot(x, w3_scr[...], preferred_element_type=F32)
        if cast:
            w2_scr[...] = w2_f32[slot].astype(BF16)
            for cp in weight_copies(fetch_ref[i], slot):
                cp.start(priority=1)
        hid = (a / (1.0 + jnp.exp(-a))) * b
        o_ref[...] = jnp.dot(hid.astype(BF16), w2_scr[...], preferred_element_type=F32)

    @pl.when(first_of_expert)
    def _():
        for cp in weight_copies(texp_ref[i], slot):
            cp.wait()
        tile(cast=True)

    @pl.when(jnp.logical_not(first_of_expert) & (i < used_ref[0]))
    def _():
        tile(cast=False)

    @pl.when(i == n_steps - 1)
    def _():
        for s in range(2):
            for cp in weight_copies(texp_ref[i], s):
                cp.wait()

    @pl.when(i >= used_ref[0])
    def _():
        o_ref[...] = jnp.zeros_like(o_ref)


def _experts(texp, tidx, fetch, slot, used, xs, w1, w3, w2, tr):
    rows = xs.shape[0]
    nt = rows // tr
    d, f = w1.shape[1:]
    assert xs.shape[1] * 2 == d and xs.dtype == jnp.uint32
    grid_spec = pltpu.PrefetchScalarGridSpec(
        num_scalar_prefetch=5,
        grid=(nt,),
        in_specs=[pl.BlockSpec((tr, d // 2), lambda i, te, ti, *_: (ti[i], 0)),
                  pl.BlockSpec(memory_space=pl.ANY),
                  pl.BlockSpec(memory_space=pl.ANY),
                  pl.BlockSpec(memory_space=pl.ANY)],
        out_specs=pl.BlockSpec((tr, d), lambda i, *_: (i, 0)),
        scratch_shapes=[pltpu.VMEM((2, d, f), F32), pltpu.VMEM((2, d, f), F32), pltpu.VMEM((2, f, d), F32),
                        pltpu.VMEM((d, f), BF16), pltpu.VMEM((d, f), BF16), pltpu.VMEM((f, d), BF16),
                        pltpu.SemaphoreType.DMA((2, 3))],
    )
    return pl.pallas_call(
        _experts_kernel,
        grid_spec=grid_spec,
        out_shape=jax.ShapeDtypeStruct((rows, d), F32),
        compiler_params=_cparams(("arbitrary",)),
        name="experts",
    )(texp, tidx, fetch, slot, used, xs, w1, w3, w2)


def _combine_kernel(pa_ref, pb_ref, ys_ref, x1_ref, info_ref, mod_ref, g_ref, b_ref, o_ref,
                    buf_a, buf_b, sems, *, tm, alpha):
    i = pl.program_id(0)
    last = pl.num_programs(0) - 1

    def issue_row(tile, slot, t):
        _row_copy(ys_ref, pa_ref[tile * tm + t], buf_a.at[slot], t, sems.at[slot]).start()
        _row_copy(ys_ref, pb_ref[tile * tm + t], buf_b.at[slot], t, sems.at[slot]).start(priority=1)

    def combine_rows(slot, rows):
        info = info_ref[rows, :]
        y = info[:, 2:3] * buf_a[slot, rows, :] + info[:, 3:4] * buf_b[slot, rows, :]
        z = alpha * x1_ref[rows, :] + mod_ref[0, 5:6, :] * y
        o_ref[rows, :] = _ln(z) * g_ref[...] + b_ref[...]

    @pl.when(i == 0)
    def _():
        lax.fori_loop(0, tm, lambda t, _: (issue_row(0, 0, t), 0)[1], 0, unroll=8)

    slot = i % 2
    for buf in (buf_a, buf_b):
        pltpu.make_async_copy(ys_ref.at[pl.ds(0, tm)], buf.at[slot], sems.at[slot]).wait()

    chunk = 32

    @pl.when(i < last)
    def _():
        for r in range(0, tm, chunk):
            for t in range(r, r + chunk):
                issue_row(i + 1, 1 - slot, t)
            combine_rows(slot, slice(r, r + chunk))

    @pl.when(i == last)
    def _():
        combine_rows(slot, slice(0, tm))


def _combine(pos_a, pos_b, ys, x1, info, mod3, ln_g, ln_b, seq, alpha):
    t, d = x1.shape
    tm = 256
    per_b = seq // tm
    grid_spec = pltpu.PrefetchScalarGridSpec(
        num_scalar_prefetch=2,
        grid=(t // tm,),
        in_specs=[pl.BlockSpec(memory_space=pl.ANY),
                  pl.BlockSpec((tm, d), lambda i, pa, pb: (i, 0)),
                  pl.BlockSpec((tm, LANES), lambda i, pa, pb: (i, 0)),
                  pl.BlockSpec((1, 6, d), lambda i, pa, pb: (i // per_b, 0, 0)),
                  pl.BlockSpec((1, d), lambda i, pa, pb: (0, 0)),
                  pl.BlockSpec((1, d), lambda i, pa, pb: (0, 0))],
        out_specs=pl.BlockSpec((tm, d), lambda i, pa, pb: (i, 0)),
        scratch_shapes=[pltpu.VMEM((2, tm, d), F32), pltpu.VMEM((2, tm, d), F32), pltpu.SemaphoreType.DMA((2,))],
    )
    return pl.pallas_call(
        functools.partial(_combine_kernel, tm=tm, alpha=alpha),
        grid_spec=grid_spec,
        out_shape=jax.ShapeDtypeStruct((t, d), F32),
        compiler_params=_cparams(("arbitrary",), row_dma=True),
        name="combine",
    )(pos_a, pos_b, ys, x1, info, mod3, ln_g, ln_b)


def _prep_w_in_kernel(w_ref, o_ref):
    r0 = MLA_Q_RANK + MLA_KV_RANK
    r1 = r0 + MLA_ROPE_DIM
    half = MLA_ROPE_DIM // 2
    z = jnp.zeros((LANES - MLA_ROPE_DIM, o_ref.shape[1]), BF16)
    o_ref[:r1, :] = w_ref[:r1, :].astype(BF16)
    o_ref[r1:r0 + LANES, :] = z
    o_ref[r0 + LANES:r0 + LANES + half, :] = w_ref[r0 + half:r1, :].astype(BF16)
    o_ref[r0 + LANES + half:r0 + LANES + MLA_ROPE_DIM, :] = w_ref[r0:r0 + half, :].astype(BF16)
    o_ref[r0 + LANES + MLA_ROPE_DIM:PART_A, :] = z
    o_ref[PART_A:, :] = w_ref[r1:, :].astype(BF16)


def _prep_w_in(w_t):
    n, k = w_t.shape
    tk = 512
    n_out = PART_A + 3 * MOBA_WIDTH
    assert n == MLA_Q_RANK + MLA_KV_RANK + MLA_ROPE_DIM + 3 * MOBA_WIDTH
    return pl.pallas_call(
        _prep_w_in_kernel,
        grid=(k // tk,),
        in_specs=[pl.BlockSpec((n, tk), lambda i: (0, i))],
        out_specs=pl.BlockSpec((n_out, tk), lambda i: (0, i)),
        out_shape=jax.ShapeDtypeStruct((n_out, k), BF16),
        compiler_params=_cparams(("arbitrary",)),
        name="prep_w_in",
    )(w_t)


def _prep_w_uq(w):
    r = w.shape[0]
    w = w.reshape(r, MLA_HEADS, MLA_NOPE_DIM + MLA_ROPE_DIM)
    half = MLA_ROPE_DIM // 2
    nope = w[:, :, :MLA_NOPE_DIM]
    x1 = w[:, :, MLA_NOPE_DIM:MLA_NOPE_DIM + half]
    x2 = w[:, :, MLA_NOPE_DIM + half:]
    assert 2 * MLA_ROPE_DIM == LANES
    return jnp.concatenate([nope, x1, x2, x2, x1], axis=2).reshape(r, MLA_HEADS * QK_PAD).astype(BF16)


def _rope_lanes(seq):
    inv = 1.0 / (ROPE_THETA ** (jnp.arange(0, MLA_ROPE_DIM, 2, dtype=F32) / MLA_ROPE_DIM))
    ang = jnp.arange(seq, dtype=F32)[:, None] * inv[None, :]
    cos, sin = jnp.cos(ang), jnp.sin(ang)
    z = jnp.zeros((seq, LANES - MLA_ROPE_DIM), F32)
    return jnp.concatenate([cos, cos, z], axis=1), jnp.concatenate([-sin, sin, z], axis=1)


def _prep_router(w_rg, b_rg, w_re, b_re):
    d = w_rg.shape[0]
    w = jnp.concatenate([w_rg, w_re], axis=1)
    n = w.shape[1]
    hi = w.astype(BF16)
    lo = (w - hi.astype(F32)).astype(BF16)
    z = jnp.zeros((d, LANES // 2 - n), BF16)
    wr = jnp.concatenate([hi, z, lo, z], axis=1)
    br = jnp.zeros((1, LANES), F32).at[0, :n].set(jnp.concatenate([b_rg, b_re]))
    return wr, br


def _layer(xf, mod3, batch, seq, depth_alpha, w_in, q_norm_g, w_uq, kv_norm_g, w_ukv, w_out, bias_tiles,
           cos_t, sin_t, ln1_g, ln1_b, w_rg, b_rg, w_re, b_re, w1, w3, w2, ln2_g, ln2_b):
    t, d = xf.shape
    part_a, qkv_mo = _in_proj(xf, mod3, _prep_w_in(w_in.T), seq)
    q, k, v = _qkv(part_a, _prep_w_uq(w_uq), w_ukv.astype(BF16), q_norm_g.reshape(1, -1),
                   kv_norm_g.reshape(1, -1), cos_t, sin_t, seq)
    o_mla = _mla_attention(q, k, v, batch, seq)
    q_wide = _moba_select(qkv_mo, batch, seq)
    o_moba = _moba_attention(q_wide, qkv_mo, bias_tiles, batch, seq)
    wo = w_out.astype(BF16)
    wr, br = _prep_router(w_rg, b_rg, w_re, b_re)
    x1, h2, logits = _out_proj(o_mla, o_moba, wo, xf, mod3, ln1_g.reshape(1, d),
                               ln1_b.reshape(1, d), wr, seq, depth_alpha)
    info, counts = _route(logits, br)
    tr = 256
    nt = (2 * t) // tr + MOE_N_EXPERTS
    cnt = counts[0, MOE_GROUPS:MOE_GROUPS + MOE_N_EXPERTS].astype(jnp.int32)
    ntile = (cnt + tr - 1) // tr
    tile_end = jnp.cumsum(ntile)
    tile_start = tile_end - ntile
    used = tile_end[-1]
    start_lanes = jnp.zeros((1, LANES), F32).at[0, MOE_GROUPS:MOE_GROUPS + MOE_N_EXPERTS].set(
        (tile_start * tr).astype(F32))
    pos = _positions(info, start_lanes)
    pos_a, pos_b = pos[0], pos[1]
    tidx = jnp.minimum(jnp.arange(nt, dtype=jnp.int32), used - 1)
    texp = jnp.sum(tidx[:, None] >= tile_end[None, :], axis=1).astype(jnp.int32)
    eids = jnp.arange(MOE_N_EXPERTS, dtype=jnp.int32)
    owns = ntile > 0
    position = jnp.cumsum(owns) - 1
    two_later = owns[None, :] & (position[None, :] == position[:, None] + 2)
    fetch_of = jnp.where(two_later.any(axis=1), jnp.argmax(two_later, axis=1), eids).astype(jnp.int32)
    by_position = jnp.argmax(owns[None, :] & (position[None, :] == jnp.arange(2)[:, None]), axis=1)
    fetch = jnp.concatenate([fetch_of[texp], by_position.astype(jnp.int32)])
    slot = (position[texp] % 2).astype(jnp.int32)
    used1 = used.reshape(1).astype(jnp.int32)
    pad0 = (tile_start * tr + cnt).astype(jnp.int32)
    padn = (ntile * tr - cnt).astype(jnp.int32)
    xs = _dispatch(pos_a, pos_b, pad0, padn, used1, h2, tr, nt)
    ys = _experts(texp, tidx, fetch, slot, used1, xs, w1, w3, w2, tr)
    return _combine(pos_a, pos_b, ys, x1, info, mod3, ln2_g.reshape(1, d), ln2_b.reshape(1, d), seq, depth_alpha)


def kernel(x, c, w_ada, b_ada, w_in, q_norm_g, w_uq, kv_norm_g, w_ukv, w_out, t5_table, ln1_g, ln1_b,
           w_router_group, b_router_group, w_router_expert, b_router_expert, w1, w3, w2, ln2_g, ln2_b):
    batch, seq, d = x.shape
    depth = w_ada.shape[0]
    alpha = (2.0 * depth) ** 0.25
    cos_t, sin_t = _rope_lanes(seq)
    bias_tiles = _t5_tiles(t5_table)
    xf = x.reshape(batch * seq, d)
    for l in range(depth):
        mod3 = _ada_mod(c, w_ada[l], b_ada[l]).reshape(batch, 6, d)
        xf = _layer(xf, mod3, batch, seq, alpha, w_in[l], q_norm_g[l], w_uq[l], kv_norm_g[l], w_ukv[l],
                    w_out[l], bias_tiles, cos_t, sin_t, ln1_g[l], ln1_b[l], w_router_group[l],
                    b_router_group[l], w_router_expert[l], b_router_expert[l], w1[l], w3[l], w2[l],
                    ln2_g[l], ln2_b[l])
    return xf.reshape(batch, seq, d)
```

```python
import functools
import math

import jax
import jax.numpy as jnp
from jax import lax
from jax.experimental import pallas as pl
from jax.experimental.pallas import tpu as pltpu

D_MODEL = 2048
MLA_HEADS = 8
MLA_Q_RANK = 512
MLA_KV_RANK = 256
MLA_NOPE_DIM = 128
MLA_ROPE_DIM = 64
MLA_V_DIM = 128
ROPE_THETA = 10000.0
MOBA_HEADS = 8
MOBA_HEAD_DIM = 128
MOBA_BLOCK = 256
MOBA_TOPK = 3
T5_BUCKETS = 32
T5_MAX_DISTANCE = 128
MOE_GROUPS = 4
MOE_EXPERTS_PER_GROUP = 8
MOE_N_EXPERTS = MOE_GROUPS * MOE_EXPERTS_PER_GROUP
MOE_D_FF = 512
LN_EPS = 1e-5
RMS_EPS = 1e-6
MOBA_WIDTH = MOBA_HEADS * MOBA_HEAD_DIM

LANES = 128
SUBLANES = 8
QK_PAD = 256
V_PAD = 256
PART_A = 1024
NEG = -1e30
VMEM_LIMIT = 56 * 1024 * 1024

F32 = jnp.float32
BF16 = jnp.bfloat16


def _cparams(sem, row_dma=False):
    return pltpu.CompilerParams(dimension_semantics=sem, vmem_limit_bytes=VMEM_LIMIT,
                                disable_bounds_checks=row_dma)


def _ln(x):
    mu = jnp.mean(x, axis=-1, keepdims=True)
    xc = x - mu
    var = jnp.mean(xc * xc, axis=-1, keepdims=True)
    return xc * lax.rsqrt(var + LN_EPS)


def _nt_dot(a, b):
    return lax.dot_general(a, b, (((1,), (1,)), ((), ())), preferred_element_type=F32)


def _ada_kernel(ct_ref, w_ref, b_ref, o_ref, *, batch):
    ct = ct_ref[...]
    ca = ct / (1.0 + jnp.exp(-ct))
    w = w_ref[...]
    rows = [jnp.sum(w * ca[:, b:b + 1], axis=0, keepdims=True) for b in range(batch)]
    o_ref[...] = jnp.concatenate(rows, axis=0) + b_ref[...]


def _ada_mod(c, w_ada, b_ada):
    batch, d = c.shape
    n = w_ada.shape[1]
    tn = 1024
    ct = jnp.zeros((d, LANES), F32).at[:, :batch].set(c.T)
    return pl.pallas_call(
        functools.partial(_ada_kernel, batch=batch),
        grid=(n // tn,),
        in_specs=[pl.BlockSpec((d, LANES), lambda j: (0, 0)),
                  pl.BlockSpec((d, tn), lambda j: (0, j)),
                  pl.BlockSpec((1, tn), lambda j: (0, j))],
        out_specs=pl.BlockSpec((batch, tn), lambda j: (0, j)),
        out_shape=jax.ShapeDtypeStruct((batch, n), F32),
        compiler_params=_cparams(("arbitrary",)),
        name="ada_mod",
    )(ct, w_ada, b_ada.reshape(1, n))


def _inproj_kernel(x_ref, mod_ref, w_ref, a_ref, b_ref, h_scr):
    j = pl.program_id(1)

    @pl.when(j == 0)
    def _():
        rows = 256
        for r in range(0, x_ref.shape[0], rows):
            h = _ln(x_ref[r:r + rows, :]) * (1.0 + mod_ref[0, 1:2, :]) + mod_ref[0, 0:1, :]
            h_scr[r:r + rows, :] = h.astype(BF16)
        a_ref[...] = _nt_dot(h_scr[...], w_ref[...])

    @pl.when(j > 0)
    def _():
        b_ref[...] = _nt_dot(h_scr[...], w_ref[...]).astype(BF16)


def _in_proj(xf, mod3, w4, seq):
    t, d = xf.shape
    n = w4.shape[0]
    tm, tn = 1024, PART_A
    per_b = seq // tm
    return pl.pallas_call(
        _inproj_kernel,
        grid=(t // tm, n // tn),
        in_specs=[pl.BlockSpec((tm, d), lambda i, j: (i, 0)),
                  pl.BlockSpec((1, 6, d), lambda i, j: (i // per_b, 0, 0)),
                  pl.BlockSpec((tn, d), lambda i, j: (j, 0))],
        out_specs=[pl.BlockSpec((tm, tn), lambda i, j: (i, 0)),
                   pl.BlockSpec((tm, tn), lambda i, j: (i, jnp.maximum(j - 1, 0)))],
        out_shape=[jax.ShapeDtypeStruct((t, PART_A), F32),
                   jax.ShapeDtypeStruct((t, n - PART_A), BF16)],
        scratch_shapes=[pltpu.VMEM((tm, d), BF16)],
        compiler_params=_cparams(("arbitrary", "arbitrary")),
        name="in_proj",
    )(xf, mod3, w4)


def _qkv_kernel(a_ref, wq_ref, wkv_ref, gq_ref, gkv_ref, cos_ref, sin_ref, q_ref, k_ref, v_ref):
    a = a_ref[...]
    cq = a[:, :MLA_Q_RANK]
    ckv = a[:, MLA_Q_RANK:MLA_Q_RANK + MLA_KV_RANK]
    r0 = MLA_Q_RANK + MLA_KV_RANK
    kr = a[:, r0:r0 + LANES]
    krs = a[:, r0 + LANES:PART_A]
    cqn = (cq * lax.rsqrt(jnp.mean(cq * cq, axis=-1, keepdims=True) + RMS_EPS) * gq_ref[...]).astype(BF16)
    ckvn = (ckv * lax.rsqrt(jnp.mean(ckv * ckv, axis=-1, keepdims=True) + RMS_EPS) * gkv_ref[...]).astype(BF16)
    cos = cos_ref[...]
    sin = sin_ref[...]
    krr = (kr * cos + krs * sin).astype(BF16)
    for h in range(MLA_HEADS):
        c0 = h * QK_PAD
        q2 = jnp.dot(cqn, wq_ref[:, c0:c0 + QK_PAD], preferred_element_type=F32)
        kv = jnp.dot(ckvn, wkv_ref[:, c0:c0 + QK_PAD], preferred_element_type=F32)
        rope = q2[:, LANES:]
        q_ref[:, c0:c0 + LANES] = q2[:, :LANES].astype(BF16)
        q_ref[:, c0 + LANES:c0 + QK_PAD] = (rope * cos + pltpu.roll(rope, LANES // 2, 1) * sin).astype(BF16)
        k_ref[:, c0:c0 + LANES] = kv[:, :LANES].astype(BF16)
        k_ref[:, c0 + LANES:c0 + QK_PAD] = krr
        v_ref[:, h * LANES:(h + 1) * LANES] = kv[:, LANES:].astype(BF16)


def _qkv(part_a, wq3, wkv, gq, gkv, cos_t, sin_t, seq):
    t = part_a.shape[0]
    tm = 512
    per_b = seq // tm
    hq = MLA_HEADS * QK_PAD
    return pl.pallas_call(
        _qkv_kernel,
        grid=(t // tm,),
        in_specs=[pl.BlockSpec((tm, PART_A), lambda i: (i, 0)),
                  pl.BlockSpec(wq3.shape, lambda i: (0, 0)),
                  pl.BlockSpec(wkv.shape, lambda i: (0, 0)),
                  pl.BlockSpec((1, MLA_Q_RANK), lambda i: (0, 0)),
                  pl.BlockSpec((1, MLA_KV_RANK), lambda i: (0, 0)),
                  pl.BlockSpec((tm, LANES), lambda i: (i % per_b, 0)),
                  pl.BlockSpec((tm, LANES), lambda i: (i % per_b, 0))],
        out_specs=[pl.BlockSpec((tm, hq), lambda i: (i, 0)),
                   pl.BlockSpec((tm, hq), lambda i: (i, 0)),
                   pl.BlockSpec((tm, MLA_HEADS * MLA_V_DIM), lambda i: (i, 0))],
        out_shape=[jax.ShapeDtypeStruct((t, hq), BF16),
                   jax.ShapeDtypeStruct((t, hq), BF16),
                   jax.ShapeDtypeStruct((t, MLA_HEADS * MLA_V_DIM), BF16)],
        compiler_params=_cparams(("arbitrary",)),
        name="qkv",
    )(part_a, wq3, wkv, gq, gkv, cos_t, sin_t)


ATTN_TILE = 512
ATTN_CHAINS = 8


def _attn_kernel(*refs, c, masked):
    nch = ATTN_CHAINS
    scratch = refs[-4 * nch:]
    tile = ATTN_TILE
    i = pl.program_id(2)
    if masked:
        q_ref, kin_ref, vin_ref, bias_ref, o_ref, v_ref, k_ref = refs[:7]
    else:
        q_ref, k_ref, vin_ref, o_ref, v_ref = refs[:5]

    @pl.when((pl.program_id(0) == 0) & (pl.program_id(1) == 0) & (i == 0))
    def _():
        seq = vin_ref.shape[0]
        v_ref[:, LANES:] = jnp.ones((seq, V_PAD - LANES), BF16)
        if masked:
            lane = lax.broadcasted_iota(jnp.int32, (seq, LANES), 1)
            own = jnp.right_shift(lax.broadcasted_iota(jnp.int32, (seq, LANES), 0), MOBA_BLOCK.bit_length() - 1)
            k_ref[:, LANES:] = jnp.where(lane == own, 1.0, 0.0).astype(BF16)

    @pl.when(i == 0)
    def _():
        v_ref[:, :LANES] = vin_ref[...]
        if masked:
            k_ref[:, :LANES] = kin_ref[...]

    chains = [dict(rows=slice(n * tile, (n + 1) * tile), s=scratch[4 * n:4 * n + 2], m=scratch[4 * n + 2],
                   acc=scratch[4 * n + 3]) for n in range(nch)]

    half = tile // 2

    def put_scores(chain, slot, t, diag=False):
        k0 = pl.multiple_of(t * tile, tile)
        q0 = chain["rows"].start
        buf = chain["s"][slot]
        if diag:
            buf[:half, :half] = _nt_dot(q_ref[q0:q0 + half, :], k_ref[pl.ds(k0, half), :])
            buf[half:, :] = _nt_dot(q_ref[q0 + half:q0 + tile, :], k_ref[pl.ds(k0, tile), :])
        else:
            buf[...] = _nt_dot(q_ref[q0:q0 + tile, :], k_ref[pl.ds(k0, tile), :])

    def softmax_update(chain, s, t, rows, keys):
        m_scr, acc_scr = chain["m"], chain["acc"]
        m_old = m_scr[rows, :]
        m_new = jnp.maximum(m_old, jnp.broadcast_to(jnp.max(s, axis=-1, keepdims=True), m_old.shape))
        alpha = jnp.exp2((m_old - m_new) * c)
        p = jnp.concatenate([jnp.exp2(((s[:, j * LANES:(j + 1) * LANES] - m_new) * c).astype(BF16))
                             for j in range(keys // LANES)], axis=1)
        v = v_ref[pl.ds(pl.multiple_of(t * tile, tile), keys), :]
        pv = jnp.dot(p, v, preferred_element_type=F32)
        acc_scr[rows, :] = jnp.concatenate([alpha, alpha], axis=1) * acc_scr[rows, :] + pv
        m_scr[rows, :] = m_new

    def tile_step(chain, slot, t, kind, prefetch=True, next_diag=False):
        if prefetch:
            put_scores(chain, 1 - slot, t + 1, diag=next_diag)
        buf = chain["s"][slot]
        if kind != "diag":
            s = buf[...]
            if masked and kind == "prev":
                s = s + bias_ref[0, 1]
            softmax_update(chain, s, t, slice(0, tile), tile)
            return
        for r0, keys in ((0, half), (half, tile)):
            rows = slice(r0, r0 + half)
            s = buf[rows, :keys]
            if masked:
                s = s + bias_ref[0, 0, rows, :keys]
            else:
                row = lax.broadcasted_iota(jnp.int32, (half, keys), 0) + r0
                col = lax.broadcasted_iota(jnp.int32, (half, keys), 1)
                s = jnp.where(row >= col, s, NEG)
            softmax_update(chain, s, t, rows, keys)

    for chain in chains:
        chain["m"][...] = jnp.full(chain["m"].shape, NEG, F32)
        chain["acc"][...] = jnp.zeros(chain["acc"].shape, F32)
        put_scores(chain, 0, 0)

    def far_pair(j, _):
        for slot in range(2):
            for chain in chains:
                tile_step(chain, slot, 2 * j + slot, "far")
        return 0

    first = nch * i
    if masked:
        lax.fori_loop(0, jnp.maximum(first // 2 - 1, 0), far_pair, 0)

        @pl.when(i > 0)
        def _():
            for chain in chains:
                tile_step(chain, 0, first - 2, "far")
            for n, chain in enumerate(chains):
                tile_step(chain, 1, first - 1, "prev" if n == 0 else "far", next_diag=(n == 0))
    else:
        lax.fori_loop(0, first // 2, far_pair, 0)

    for k in range(nch):
        for n, chain in enumerate(chains):
            if n < k:
                continue
            kind = "diag" if n == k else ("prev" if masked and n == k + 1 else "far")
            tile_step(chain, k % 2, first + k, kind, prefetch=(n != k), next_diag=(n == k + 1))

    for chain in chains:
        acc = chain["acc"]
        o_ref[chain["rows"], :] = (acc[:, :LANES] / acc[:, LANES:]).astype(o_ref.dtype)


def _attn_scratch(seq, masked):
    tile = ATTN_TILE
    wide = [pltpu.VMEM((seq, V_PAD), BF16)] + ([pltpu.VMEM((seq, QK_PAD), BF16)] if masked else [])
    per_chain = [pltpu.VMEM((tile, tile), F32), pltpu.VMEM((tile, tile), F32),
                 pltpu.VMEM((tile, LANES), F32), pltpu.VMEM((tile, V_PAD), F32)]
    return wide + per_chain * ATTN_CHAINS


def _mla_attention(q, k, v, batch, seq):
    tile = ATTN_CHAINS * ATTN_TILE
    nq = seq // tile
    c = math.log2(math.e) / math.sqrt(MLA_NOPE_DIM + MLA_ROPE_DIM)
    return pl.pallas_call(
        functools.partial(_attn_kernel, c=c, masked=False),
        grid=(batch, MLA_HEADS, nq),
        in_specs=[pl.BlockSpec((tile, QK_PAD), lambda b, h, i: (b * nq + i, h)),
                  pl.BlockSpec((seq, QK_PAD), lambda b, h, i: (b, h)),
                  pl.BlockSpec((seq, MLA_V_DIM), lambda b, h, i: (b, h))],
        out_specs=pl.BlockSpec((tile, MLA_V_DIM), lambda b, h, i: (b * nq + i, h)),
        out_shape=jax.ShapeDtypeStruct((batch * seq, MLA_HEADS * MLA_V_DIM), BF16),
        scratch_shapes=_attn_scratch(seq, False),
        compiler_params=_cparams(("arbitrary", "arbitrary", "arbitrary")),
        name="mla_attn",
    )(q, k, v)


def _moba_select_kernel(q_ref, k_ref, qa_ref, *, seq, nb):
    kf = k_ref[...].astype(F32)
    km = jnp.sum(kf.reshape(nb, MOBA_BLOCK, MOBA_HEAD_DIM), axis=1) * (1.0 / MOBA_BLOCK)
    km_hi = km.astype(BF16)
    km_lo = (km - km_hi.astype(F32)).astype(BF16)
    q = q_ref[...]
    gate = _nt_dot(km_hi, q) + _nt_dot(km_lo, q)
    shift = MOBA_BLOCK.bit_length() - 1
    blk = lax.broadcasted_iota(jnp.int32, (nb, seq), 0)
    qblk = jnp.right_shift(lax.broadcasted_iota(jnp.int32, (nb, seq), 1), shift)
    g = jnp.where(blk < qblk, gate, NEG)
    visible = blk == qblk
    for _ in range(MOBA_TOPK):
        mx = jnp.max(g, axis=0, keepdims=True)
        first = jnp.min(jnp.where(g == mx, blk, nb), axis=0, keepdims=True)
        pick = (blk == first) & (mx > 0.5 * NEG)
        visible = visible | pick
        g = jnp.where(pick, NEG, g)
    mask_t = jnp.concatenate([jnp.where(visible, 0.0, NEG), jnp.zeros((LANES - nb, seq), F32)], axis=0)
    qa_ref[:, :MOBA_HEAD_DIM] = q
    qa_ref[:, MOBA_HEAD_DIM:] = mask_t.T.astype(BF16)


def _moba_select(qkv_mo, batch, seq):
    nb = seq // MOBA_BLOCK
    assert MOBA_HEAD_DIM == LANES and nb <= QK_PAD - MOBA_HEAD_DIM
    return pl.pallas_call(
        functools.partial(_moba_select_kernel, seq=seq, nb=nb),
        grid=(batch, MOBA_HEADS),
        in_specs=[pl.BlockSpec((seq, MOBA_HEAD_DIM), lambda b, h: (b, h)),
                  pl.BlockSpec((seq, MOBA_HEAD_DIM), lambda b, h: (b, MOBA_HEADS + h))],
        out_specs=pl.BlockSpec((seq, QK_PAD), lambda b, h: (b, h)),
        out_shape=jax.ShapeDtypeStruct((batch * seq, MOBA_HEADS * QK_PAD), BF16),
        compiler_params=_cparams(("arbitrary", "arbitrary")),
        name="moba_select",
    )(qkv_mo, qkv_mo)


def _t5_kernel(tab_ref, o_ref, *, inv_scale):
    h = pl.program_id(0)
    r = lax.broadcasted_iota(jnp.int32, (LANES, LANES), 0)
    c = lax.broadcasted_iota(jnp.int32, (LANES, LANES), 1)
    max_exact = T5_BUCKETS // 2
    far = tab_ref[T5_BUCKETS - 1, h]

    def block(offset):
        rel = offset + r - c
        n = jnp.maximum(rel, 0)
        nf = jnp.maximum(n, 1).astype(F32)
        large = max_exact + (jnp.log(nf / max_exact) / math.log(T5_MAX_DISTANCE / max_exact)
                             * (T5_BUCKETS - max_exact)).astype(jnp.int32)
        large = jnp.minimum(large, T5_BUCKETS - 1)
        bucket = jnp.where(n < max_exact, n, large)
        bias = jnp.zeros((LANES, LANES), F32)
        for j in range(T5_BUCKETS):
            bias = jnp.where(bucket == j, tab_ref[j, h], bias)
        return jnp.where(rel >= 0, (bias - far) * inv_scale, NEG)

    near = {0: block(0), 1: block(LANES)}
    nblk = ATTN_TILE // LANES
    for d in range(2):
        for i in range(nblk):
            for j in range(nblk):
                k = d * nblk + i - j
                if k < 0:
                    val = jnp.full((LANES, LANES), NEG, F32)
                else:
                    val = near.get(k, jnp.zeros((LANES, LANES), F32))
                o_ref[0, d, i * LANES:(i + 1) * LANES, j * LANES:(j + 1) * LANES] = val


def _t5_tiles(t5_table):
    assert LANES >= T5_MAX_DISTANCE
    tile = ATTN_TILE
    return pl.pallas_call(
        functools.partial(_t5_kernel, inv_scale=math.sqrt(MOBA_HEAD_DIM)),
        grid=(MOBA_HEADS,),
        in_specs=[pl.BlockSpec(memory_space=pltpu.SMEM)],
        out_specs=pl.BlockSpec((1, 2, tile, tile), lambda h: (h, 0, 0, 0)),
        out_shape=jax.ShapeDtypeStruct((MOBA_HEADS, 2, tile, tile), F32),
        compiler_params=_cparams(("arbitrary",)),
        name="t5_tiles",
    )(t5_table)


def _moba_attention(q_wide, qkv_mo, bias, batch, seq):
    tile = ATTN_CHAINS * ATTN_TILE
    nq = seq // tile
    c = math.log2(math.e) / math.sqrt(MOBA_HEAD_DIM)
    dh = MOBA_HEAD_DIM
    return pl.pallas_call(
        functools.partial(_attn_kernel, c=c, masked=True),
        grid=(batch, MOBA_HEADS, nq),
        in_specs=[pl.BlockSpec((tile, QK_PAD), lambda b, h, i: (b * nq + i, h)),
                  pl.BlockSpec((seq, dh), lambda b, h, i: (b, MOBA_HEADS + h)),
                  pl.BlockSpec((seq, dh), lambda b, h, i: (b, 2 * MOBA_HEADS + h)),
                  pl.BlockSpec((1, 2, ATTN_TILE, ATTN_TILE), lambda b, h, i: (h, 0, 0, 0))],
        out_specs=pl.BlockSpec((tile, dh), lambda b, h, i: (b * nq + i, h)),
        out_shape=jax.ShapeDtypeStruct((batch * seq, MOBA_WIDTH), BF16),
        scratch_shapes=_attn_scratch(seq, True),
        compiler_params=_cparams(("arbitrary", "arbitrary", "arbitrary")),
        name="moba_attn",
    )(q_wide, qkv_mo, qkv_mo, bias)


def _outproj_kernel(oa_ref, ob_ref, wa_ref, wb_ref, x_ref, mod_ref, g_ref, b_ref, wr_ref,
                    x1_ref, h2_ref, lg_ref, y_a, y_b, *, alpha, n_tiles):
    i = pl.program_id(0)
    tm, d = y_a.shape
    chunks = 4
    cn, cr = d // chunks, tm // chunks

    def matmul_into(y_ref, c):
        cols = slice(c * cn, (c + 1) * cn)
        y_ref[:, cols] = (jnp.dot(oa_ref[...], wa_ref[:, cols], preferred_element_type=F32)
                          + jnp.dot(ob_ref[...], wb_ref[:, cols], preferred_element_type=F32))

    def epilogue(y_ref, c):
        rows = slice(c * cr, (c + 1) * cr)
        z = alpha * x_ref[rows, :] + mod_ref[0, 2:3, :] * y_ref[rows, :]
        x1 = _ln(z) * g_ref[...] + b_ref[...]
        x1_ref[rows, :] = x1
        h2 = _ln(x1) * (1.0 + mod_ref[0, 4:5, :]) + mod_ref[0, 3:4, :]
        h_hi = h2.astype(BF16)
        h_lo = (h2 - h_hi.astype(F32)).astype(BF16)
        h_bits = pltpu.bitcast(h_hi.astype(F32), jnp.uint32)
        h2_ref[rows, :] = h_bits[:, :d // 2] | jnp.right_shift(h_bits[:, d // 2:], 16)
        zz = (jnp.dot(h_hi, wr_ref[...], preferred_element_type=F32)
              + jnp.dot(h_lo, wr_ref[...], preferred_element_type=F32))
        lg_ref[rows, :] = zz + pltpu.roll(zz, LANES // 2, 1)

    def step(y_new, y_old):
        for c in range(chunks):
            if y_new is not None:
                matmul_into(y_new, c)
            if y_old is not None:
                epilogue(y_old, c)

    inner = (i > 0) & (i < n_tiles)
    pl.when(i == 0)(lambda: step(y_a, None))
    pl.when(inner & (i % 2 == 0))(lambda: step(y_a, y_b))
    pl.when(inner & (i % 2 == 1))(lambda: step(y_b, y_a))
    pl.when(i == n_tiles)(lambda: step(None, y_b if n_tiles % 2 == 0 else y_a))


def _out_proj(o_mla, o_moba, wo, xf, mod3, ln_g, ln_b, wr, seq, alpha):
    t, d = xf.shape
    tm = 512
    n = t // tm
    per_b = seq // tm
    ka, kb = o_mla.shape[1], o_moba.shape[1]
    assert ka == kb and wo.shape[0] == ka + kb
    once = pl.Buffered(1)

    def cur(i):
        return (jnp.minimum(i, n - 1), 0)

    def lag(i):
        return (jnp.maximum(i - 1, 0), 0)

    return pl.pallas_call(
        functools.partial(_outproj_kernel, alpha=alpha, n_tiles=n),
        grid=(n + 1,),
        in_specs=[pl.BlockSpec((tm, ka), cur),
                  pl.BlockSpec((tm, kb), cur),
                  pl.BlockSpec((ka, d), lambda i: (0, 0), pipeline_mode=once),
                  pl.BlockSpec((kb, d), lambda i: (1, 0), pipeline_mode=once),
                  pl.BlockSpec((tm, d), lag),
                  pl.BlockSpec((1, 6, d), lambda i: (jnp.maximum(i - 1, 0) // per_b, 0, 0)),
                  pl.BlockSpec((1, d), lambda i: (0, 0)),
                  pl.BlockSpec((1, d), lambda i: (0, 0)),
                  pl.BlockSpec((d, LANES), lambda i: (0, 0), pipeline_mode=once)],
        out_specs=[pl.BlockSpec((tm, d), lag),
                   pl.BlockSpec((tm, d // 2), lag),
                   pl.BlockSpec((tm, LANES), lag)],
        out_shape=[jax.ShapeDtypeStruct((t, d), F32),
                   jax.ShapeDtypeStruct((t, d // 2), jnp.uint32),
                   jax.ShapeDtypeStruct((t, LANES), F32)],
        scratch_shapes=[pltpu.VMEM((tm, d), F32), pltpu.VMEM((tm, d), F32)],
        compiler_params=_cparams(("arbitrary",)),
        name="out_proj",
    )(o_mla, o_moba, wo, wo, xf, mod3, ln_g, ln_b, wr)


def _route_kernel(lg_ref, br_ref, info_ref, cnt_ref, run_scr, *, tm):
    i = pl.program_id(0)

    @pl.when(i == 0)
    def _():
        run_scr[...] = jnp.zeros_like(run_scr)

    lg = lg_ref[...] + br_ref[...]
    lane = lax.broadcasted_iota(jnp.int32, (tm, LANES), 1)
    e_lo, e_hi = MOE_GROUPS, MOE_GROUPS + MOE_N_EXPERTS
    is_g = lane < e_lo
    gl = jnp.where(is_g, lg, NEG)
    gmax = jnp.max(gl, axis=-1, keepdims=True)
    gidx = jnp.min(jnp.where(gl == gmax, lane, LANES), axis=-1, keepdims=True)
    g_p = 1.0 / jnp.sum(jnp.where(is_g, jnp.exp(gl - gmax), 0.0), axis=-1, keepdims=True)
    grp_of_lane = jnp.right_shift(lane - e_lo, MOE_EXPERTS_PER_GROUP.bit_length() - 1)
    in_grp = (lane >= e_lo) & (lane < e_hi) & (grp_of_lane == gidx)
    el = jnp.where(in_grp, lg, NEG)
    m1 = jnp.max(el, axis=-1, keepdims=True)
    l1 = jnp.min(jnp.where(el == m1, lane, LANES), axis=-1, keepdims=True)
    el2 = jnp.where(lane == l1, NEG, el)
    m2 = jnp.max(el2, axis=-1, keepdims=True)
    l2 = jnp.min(jnp.where(el2 == m2, lane, LANES), axis=-1, keepdims=True)
    zsum = jnp.sum(jnp.where(in_grp, jnp.exp(el - m1), 0.0), axis=-1, keepdims=True)
    p1 = 1.0 / zsum
    p2 = jnp.exp(m2 - m1) / zsum
    wa = g_p * (p1 / (p1 + p2))
    wb = g_p * (p2 / (p1 + p2))
    hot_a = lane == l1
    hot_b = lane == l2
    onehot = jnp.where(hot_a | hot_b, 1.0, 0.0)
    r = lax.broadcasted_iota(jnp.int32, (tm, tm), 0)
    c = lax.broadcasted_iota(jnp.int32, (tm, tm), 1)
    lower = jnp.where(c < r, 1.0, 0.0).astype(BF16)
    before = jnp.dot(lower, onehot.astype(BF16), preferred_element_type=F32) + run_scr[...]
    rank_a = jnp.sum(jnp.where(hot_a, before, 0.0), axis=-1, keepdims=True)
    rank_b = jnp.sum(jnp.where(hot_b, before, 0.0), axis=-1, keepdims=True)
    run_scr[...] += jnp.sum(onehot, axis=0, keepdims=True)
    info = jnp.zeros((tm, LANES), F32)
    for k, val in enumerate([(l1 - e_lo).astype(F32), (l2 - e_lo).astype(F32), wa, wb, rank_a, rank_b]):
        info = jnp.where(lane == k, val, info)
    info_ref[...] = info
    cnt_ref[...] = run_scr[...]


def _route(logits, br):
    t = logits.shape[0]
    tm = 512
    return pl.pallas_call(
        functools.partial(_route_kernel, tm=tm),
        grid=(t // tm,),
        in_specs=[pl.BlockSpec((tm, LANES), lambda i: (i, 0)),
                  pl.BlockSpec((1, LANES), lambda i: (0, 0))],
        out_specs=[pl.BlockSpec((tm, LANES), lambda i: (i, 0)),
                   pl.BlockSpec((1, LANES), lambda i: (0, 0))],
        out_shape=[jax.ShapeDtypeStruct((t, LANES), F32),
                   jax.ShapeDtypeStruct((1, LANES), F32)],
        scratch_shapes=[pltpu.VMEM((1, LANES), F32)],
        compiler_params=_cparams(("arbitrary",)),
        name="route",
    )(logits, br)


def _pos_kernel(info_ref, start_ref, pos_ref):
    info = info_ref[...]
    tm = info.shape[0]
    lane = lax.broadcasted_iota(jnp.int32, (tm, LANES), 1)
    start = start_ref[...]
    cols = []
    for k in range(2):
        e = jnp.sum(jnp.where(lane == k, info, 0.0), axis=-1, keepdims=True).astype(jnp.int32)
        rank = jnp.sum(jnp.where(lane == 4 + k, info, 0.0), axis=-1, keepdims=True)
        base = jnp.sum(jnp.where(lane == e + MOE_GROUPS, start, 0.0), axis=-1, keepdims=True)
        cols.append(base + rank)
    wide = jnp.where(lane == 0, cols[0], jnp.where(lane == 1, cols[1], 0.0))
    pos_ref[...] = wide.T[:pos_ref.shape[0], :].astype(jnp.int32)


def _positions(info, start_lanes):
    t = info.shape[0]
    tm = 1024
    return pl.pallas_call(
        _pos_kernel,
        grid=(t // tm,),
        in_specs=[pl.BlockSpec((tm, LANES), lambda i: (i, 0)),
                  pl.BlockSpec((1, LANES), lambda i: (0, 0))],
        out_specs=pl.BlockSpec((8, tm), lambda i: (0, i)),
        out_shape=jax.ShapeDtypeStruct((8, t), jnp.int32),
        compiler_params=_cparams(("arbitrary",)),
        name="positions",
    )(info, start_lanes)


def _row_copy(src_ref, src_row, dst_ref, dst_row, sem):
    return pltpu.make_async_copy(src_ref.at[pl.ds(src_row, 1)], dst_ref.at[pl.ds(dst_row, 1)], sem)


def _dispatch_kernel(pa_ref, pb_ref, pad0_ref, padn_ref, used_ref, h_ref, xs_ref, ztile, sem, zsem,
                     *, tm, tr, n_tiles):
    i = pl.program_id(0)
    base = i * tm

    def zero_fill(act):
        def whole_tile(j, _):
            act(pltpu.make_async_copy(ztile, xs_ref.at[pl.ds(pl.multiple_of(j * tr, tr), tr)], zsem))
            return 0

        lax.fori_loop(used_ref[0], n_tiles, whole_tile, 0)

        def expert_pad(e, _):
            n, start = padn_ref[e], pad0_ref[e]
            head = jnp.minimum(jnp.bitwise_and(-start, SUBLANES - 1), n)
            for k in range(SUBLANES - 1):
                pl.when(k < head)(functools.partial(act, _row_copy(ztile, 0, xs_ref, start + k, zsem)))
            off = start + head
            groups = jnp.right_shift(n - head, SUBLANES.bit_length() - 1)
            for bit in reversed(range((tr // SUBLANES - 1).bit_length())):
                size = SUBLANES << bit
                take = jnp.bitwise_and(jnp.right_shift(groups, bit), 1)
                dst = xs_ref.at[pl.ds(pl.multiple_of(off, SUBLANES), size)]
                pl.when(take == 1)(functools.partial(act, pltpu.make_async_copy(ztile.at[pl.ds(0, size)], dst, zsem)))
                off = off + take * size
            return 0

        lax.fori_loop(0, MOE_N_EXPERTS, expert_pad, 0)

    @pl.when(i == 0)
    def _():
        ztile[...] = jnp.zeros(ztile.shape, ztile.dtype)
        zero_fill(lambda cp: cp.start())

    for t in range(tm):
        _row_copy(h_ref, t, xs_ref, pa_ref[base + t], sem).start()
        _row_copy(h_ref, t, xs_ref, pb_ref[base + t], sem).start(priority=1)
    for _ in range(2):
        pltpu.make_async_copy(h_ref, xs_ref.at[pl.ds(0, tm)], sem).wait()

    @pl.when(i == pl.num_programs(0) - 1)
    def _():
        zero_fill(lambda cp: cp.wait())


def _dispatch(pos_a, pos_b, pad0, padn, used, h2, tr, n_tiles):
    t, d = h2.shape
    tm = 256
    grid_spec = pltpu.PrefetchScalarGridSpec(
        num_scalar_prefetch=5,
        grid=(t // tm,),
        in_specs=[pl.BlockSpec((tm, d), lambda i, *_: (i, 0))],
        out_specs=pl.BlockSpec(memory_space=pl.ANY),
        scratch_shapes=[pltpu.VMEM((tr, d), h2.dtype), pltpu.SemaphoreType.DMA(()), pltpu.SemaphoreType.DMA(())],
    )
    return pl.pallas_call(
        functools.partial(_dispatch_kernel, tm=tm, tr=tr, n_tiles=n_tiles),
        grid_spec=grid_spec,
        out_shape=jax.ShapeDtypeStruct((n_tiles * tr, d), h2.dtype),
        compiler_params=_cparams(("arbitrary",), row_dma=True),
        name="dispatch",
    )(pos_a, pos_b, pad0, padn, used, h2)


def _experts_kernel(texp_ref, tidx_ref, fetch_ref, slot_ref, used_ref, x_ref, w1_hbm, w3_hbm, w2_hbm, o_ref,
                    w1_f32, w3_f32, w2_f32, w1_scr, w3_scr, w2_scr, sems):
    i = pl.program_id(0)
    n_steps = pl.num_programs(0)
    prev = texp_ref[jnp.maximum(i - 1, 0)]
    first_of_expert = (i == 0) | (texp_ref[i] != prev)
    slot = slot_ref[i]

    def weight_copies(e, s):
        return [pltpu.make_async_copy(w1_hbm.at[e], w1_f32.at[s], sems.at[s, 0]),
                pltpu.make_async_copy(w3_hbm.at[e], w3_f32.at[s], sems.at[s, 1]),
                pltpu.make_async_copy(w2_hbm.at[e], w2_f32.at[s], sems.at[s, 2])]

    @pl.when(i == 0)
    def _():
        for s in range(2):
            for cp in weight_copies(fetch_ref[n_steps + s], s):
                cp.start(priority=1)

    def tile(cast):
        bits = x_ref[...]
        x = jnp.concatenate(
            [pltpu.bitcast(jnp.bitwise_and(bits, jnp.uint32(0xFFFF0000)), F32).astype(BF16),
             pltpu.bitcast(jnp.left_shift(bits, 16), F32).astype(BF16)], axis=1)
        if cast:
            w1_scr[...] = w1_f32[slot].astype(BF16)
        a = jnp.dot(x, w1_scr[...], preferred_element_type=F32)
        if cast:
            w3_scr[...] = w3_f32[slot].astype(BF16)
        b = jnp.dot(x, w3_scr[...], preferred_element_type=F32)
        if cast:
            w2_scr[...] = w2_f32[slot].astype(BF16)
            for cp in weight_copies(fetch_ref[i], slot):
                cp.start(priority=1)
        hid = (a / (1.0 + jnp.exp(-a))) * b
        o_ref[...] = jnp.dot(hid.astype(BF16), w2_scr[...], preferred_element_type=F32)

    @pl.when(first_of_expert)
    def _():
        for cp in weight_copies(texp_ref[i], slot):
            cp.wait()
        tile(cast=True)

    @pl.when(jnp.logical_not(first_of_expert) & (i < used_ref[0]))
    def _():
        tile(cast=False)

    @pl.when(i == n_steps - 1)
    def _():
        for s in range(2):
            for cp in weight_copies(texp_ref[i], s):
                cp.wait()

    @pl.when(i >= used_ref[0])
    def _():
        o_ref[...] = jnp.zeros_like(o_ref)


def _experts(texp, tidx, fetch, slot, used, xs, w1, w3, w2, tr):
    rows = xs.shape[0]
    nt = rows // tr
    d, f = w1.shape[1:]
    assert xs.shape[1] * 2 == d and xs.dtype == jnp.uint32
    grid_spec = pltpu.PrefetchScalarGridSpec(
        num_scalar_prefetch=5,
        grid=(nt,),
        in_specs=[pl.BlockSpec((tr, d // 2), lambda i, te, ti, *_: (ti[i], 0)),
                  pl.BlockSpec(memory_space=pl.ANY),
                  pl.BlockSpec(memory_space=pl.ANY),
                  pl.BlockSpec(memory_space=pl.ANY)],
        out_specs=pl.BlockSpec((tr, d), lambda i, *_: (i, 0)),
        scratch_shapes=[pltpu.VMEM((2, d, f), F32), pltpu.VMEM((2, d, f), F32), pltpu.VMEM((2, f, d), F32),
                        pltpu.VMEM((d, f), BF16), pltpu.VMEM((d, f), BF16), pltpu.VMEM((f, d), BF16),
                        pltpu.SemaphoreType.DMA((2, 3))],
    )
    return pl.pallas_call(
        _experts_kernel,
        grid_spec=grid_spec,
        out_shape=jax.ShapeDtypeStruct((rows, d), F32),
        compiler_params=_cparams(("arbitrary",)),
        name="experts",
    )(texp, tidx, fetch, slot, used, xs, w1, w3, w2)


def _combine_kernel(pa_ref, pb_ref, ys_ref, x1_ref, info_ref, mod_ref, g_ref, b_ref, o_ref,
                    buf_a, buf_b, sems, *, tm, alpha):
    i = pl.program_id(0)
    last = pl.num_programs(0) - 1

    def issue_row(tile, slot, t):
        _row_copy(ys_ref, pa_ref[tile * tm + t], buf_a.at[slot], t, sems.at[slot]).start()
        _row_copy(ys_ref, pb_ref[tile * tm + t], buf_b.at[slot], t, sems.at[slot]).start(priority=1)

    def combine_rows(slot, rows):
        info = info_ref[rows, :]
        y = info[:, 2:3] * buf_a[slot, rows, :] + info[:, 3:4] * buf_b[slot, rows, :]
        z = alpha * x1_ref[rows, :] + mod_ref[0, 5:6, :] * y
        o_ref[rows, :] = _ln(z) * g_ref[...] + b_ref[...]

    @pl.when(i == 0)
    def _():
        lax.fori_loop(0, tm, lambda t, _: (issue_row(0, 0, t), 0)[1], 0, unroll=8)

    slot = i % 2
    for buf in (buf_a, buf_b):
        pltpu.make_async_copy(ys_ref.at[pl.ds(0, tm)], buf.at[slot], sems.at[slot]).wait()

    chunk = 32

    @pl.when(i < last)
    def _():
        for r in range(0, tm, chunk):
            for t in range(r, r + chunk):
                issue_row(i + 1, 1 - slot, t)
            combine_rows(slot, slice(r, r + chunk))

    @pl.when(i == last)
    def _():
        combine_rows(slot, slice(0, tm))


def _combine(pos_a, pos_b, ys, x1, info, mod3, ln_g, ln_b, seq, alpha):
    t, d = x1.shape
    tm = 256
    per_b = seq // tm
    grid_spec = pltpu.PrefetchScalarGridSpec(
        num_scalar_prefetch=2,
        grid=(t // tm,),
        in_specs=[pl.BlockSpec(memory_space=pl.ANY),
                  pl.BlockSpec((tm, d), lambda i, pa, pb: (i, 0)),
                  pl.BlockSpec((tm, LANES), lambda i, pa, pb: (i, 0)),
                  pl.BlockSpec((1, 6, d), lambda i, pa, pb: (i // per_b, 0, 0)),
                  pl.BlockSpec((1, d), lambda i, pa, pb: (0, 0)),
                  pl.BlockSpec((1, d), lambda i, pa, pb: (0, 0))],
        out_specs=pl.BlockSpec((tm, d), lambda i, pa, pb: (i, 0)),
        scratch_shapes=[pltpu.VMEM((2, tm, d), F32), pltpu.VMEM((2, tm, d), F32), pltpu.SemaphoreType.DMA((2,))],
    )
    return pl.pallas_call(
        functools.partial(_combine_kernel, tm=tm, alpha=alpha),
        grid_spec=grid_spec,
        out_shape=jax.ShapeDtypeStruct((t, d), F32),
        compiler_params=_cparams(("arbitrary",), row_dma=True),
        name="combine",
    )(pos_a, pos_b, ys, x1, info, mod3, ln_g, ln_b)


def _prep_w_in_kernel(w_ref, o_ref):
    r0 = MLA_Q_RANK + MLA_KV_RANK
    r1 = r0 + MLA_ROPE_DIM
    half = MLA_ROPE_DIM // 2
    z = jnp.zeros((LANES - MLA_ROPE_DIM, o_ref.shape[1]), BF16)
    o_ref[:r1, :] = w_ref[:r1, :].astype(BF16)
    o_ref[r1:r0 + LANES, :] = z
    o_ref[r0 + LANES:r0 + LANES + half, :] = w_ref[r0 + half:r1, :].astype(BF16)
    o_ref[r0 + LANES + half:r0 + LANES + MLA_ROPE_DIM, :] = w_ref[r0:r0 + half, :].astype(BF16)
    o_ref[r0 + LANES + MLA_ROPE_DIM:PART_A, :] = z
    o_ref[PART_A:, :] = w_ref[r1:, :].astype(BF16)


def _prep_w_in(w_t):
    n, k = w_t.shape
    tk = 512
    n_out = PART_A + 3 * MOBA_WIDTH
    assert n == MLA_Q_RANK + MLA_KV_RANK + MLA_ROPE_DIM + 3 * MOBA_WIDTH
    return pl.pallas_call(
        _prep_w_in_kernel,
        grid=(k // tk,),
        in_specs=[pl.BlockSpec((n, tk), lambda i: (0, i))],
        out_specs=pl.BlockSpec((n_out, tk), lambda i: (0, i)),
        out_shape=jax.ShapeDtypeStruct((n_out, k), BF16),
        compiler_params=_cparams(("arbitrary",)),
        name="prep_w_in",
    )(w_t)


def _prep_w_uq(w):
    r = w.shape[0]
    w = w.reshape(r, MLA_HEADS, MLA_NOPE_DIM + MLA_ROPE_DIM)
    half = MLA_ROPE_DIM // 2
    nope = w[:, :, :MLA_NOPE_DIM]
    x1 = w[:, :, MLA_NOPE_DIM:MLA_NOPE_DIM + half]
    x2 = w[:, :, MLA_NOPE_DIM + half:]
    assert 2 * MLA_ROPE_DIM == LANES
    return jnp.concatenate([nope, x1, x2, x2, x1], axis=2).reshape(r, MLA_HEADS * QK_PAD).astype(BF16)


def _rope_lanes(seq):
    inv = 1.0 / (ROPE_THETA ** (jnp.arange(0, MLA_ROPE_DIM, 2, dtype=F32) / MLA_ROPE_DIM))
    ang = jnp.arange(seq, dtype=F32)[:, None] * inv[None, :]
    cos, sin = jnp.cos(ang), jnp.sin(ang)
    z = jnp.zeros((seq, LANES - MLA_ROPE_DIM), F32)
    return jnp.concatenate([cos, cos, z], axis=1), jnp.concatenate([-sin, sin, z], axis=1)


def _prep_router(w_rg, b_rg, w_re, b_re):
    d = w_rg.shape[0]
    w = jnp.concatenate([w_rg, w_re], axis=1)
    n = w.shape[1]
    hi = w.astype(BF16)
    lo = (w - hi.astype(F32)).astype(BF16)
    z = jnp.zeros((d, LANES // 2 - n), BF16)
    wr = jnp.concatenate([hi, z, lo, z], axis=1)
    br = jnp.zeros((1, LANES), F32).at[0, :n].set(jnp.concatenate([b_rg, b_re]))
    return wr, br


def _layer(xf, mod3, batch, seq, depth_alpha, w_in, q_norm_g, w_uq, kv_norm_g, w_ukv, w_out, bias_tiles,
           cos_t, sin_t, ln1_g, ln1_b, w_rg, b_rg, w_re, b_re, w1, w3, w2, ln2_g, ln2_b):
    t, d = xf.shape
    part_a, qkv_mo = _in_proj(xf, mod3, _prep_w_in(w_in.T), seq)
    q, k, v = _qkv(part_a, _prep_w_uq(w_uq), w_ukv.astype(BF16), q_norm_g.reshape(1, -1),
                   kv_norm_g.reshape(1, -1), cos_t, sin_t, seq)
    o_mla = _mla_attention(q, k, v, batch, seq)
    q_wide = _moba_select(qkv_mo, batch, seq)
    o_moba = _moba_attention(q_wide, qkv_mo, bias_tiles, batch, seq)
    wo = w_out.astype(BF16)
    wr, br = _prep_router(w_rg, b_rg, w_re, b_re)
    x1, h2, logits = _out_proj(o_mla, o_moba, wo, xf, mod3, ln1_g.reshape(1, d),
                               ln1_b.reshape(1, d), wr, seq, depth_alpha)
    info, counts = _route(logits, br)
    tr = 256
    nt = (2 * t) // tr + MOE_N_EXPERTS
    cnt = counts[0, MOE_GROUPS:MOE_GROUPS + MOE_N_EXPERTS].astype(jnp.int32)
    ntile = (cnt + tr - 1) // tr
    tile_end = jnp.cumsum(ntile)
    tile_start = tile_end - ntile
    used = tile_end[-1]
    start_lanes = jnp.zeros((1, LANES), F32).at[0, MOE_GROUPS:MOE_GROUPS + MOE_N_EXPERTS].set(
        (tile_start * tr).astype(F32))
    pos = _positions(info, start_lanes)
    pos_a, pos_b = pos[0], pos[1]
    tidx = jnp.minimum(jnp.arange(nt, dtype=jnp.int32), used - 1)
    texp = jnp.sum(tidx[:, None] >= tile_end[None, :], axis=1).astype(jnp.int32)
    eids = jnp.arange(MOE_N_EXPERTS, dtype=jnp.int32)
    owns = ntile > 0
    position = jnp.cumsum(owns) - 1
    two_later = owns[None, :] & (position[None, :] == position[:, None] + 2)
    fetch_of = jnp.where(two_later.any(axis=1), jnp.argmax(two_later, axis=1), eids).astype(jnp.int32)
    by_position = jnp.argmax(owns[None, :] & (position[None, :] == jnp.arange(2)[:, None]), axis=1)
    fetch = jnp.concatenate([fetch_of[texp], by_position.astype(jnp.int32)])
    slot = (position[texp] % 2).astype(jnp.int32)
    used1 = used.reshape(1).astype(jnp.int32)
    pad0 = (tile_start * tr + cnt).astype(jnp.int32)
    padn = (ntile * tr - cnt).astype(jnp.int32)
    xs = _dispatch(pos_a, pos_b, pad0, padn, used1, h2, tr, nt)
    ys = _experts(texp, tidx, fetch, slot, used1, xs, w1, w3, w2, tr)
    return _combine(pos_a, pos_b, ys, x1, info, mod3, ln2_g.reshape(1, d), ln2_b.reshape(1, d), seq, depth_alpha)


def kernel(x, c, w_ada, b_ada, w_in, q_norm_g, w_uq, kv_norm_g, w_ukv, w_out, t5_table, ln1_g, ln1_b,
           w_router_group, b_router_group, w_router_expert, b_router_expert, w1, w3, w2, ln2_g, ln2_b):
    batch, seq, d = x.shape
    depth = w_ada.shape[0]
    alpha = (2.0 * depth) ** 0.25
    cos_t, sin_t = _rope_lanes(seq)
    bias_tiles = _t5_tiles(t5_table)
    xf = x.reshape(batch * seq, d)
    for l in range(depth):
        mod3 = _ada_mod(c, w_ada[l], b_ada[l]).reshape(batch, 6, d)
        xf = _layer(xf, mod3, batch, seq, alpha, w_in[l], q_norm_g[l], w_uq[l], kv_norm_g[l], w_ukv[l],
                    w_out[l], bias_tiles, cos_t, sin_t, ln1_g[l], ln1_b[l], w_router_group[l],
                    b_router_group[l], w_router_expert[l], b_router_expert[l], w1[l], w3[l], w2[l],
                    ln2_g[l], ln2_b[l])
    return xf.reshape(batch, seq, d)
```

```python
import functools
import math

import jax
import jax.numpy as jnp
from jax import lax
from jax.experimental import pallas as pl
from jax.experimental.pallas import tpu as pltpu

D_MODEL = 2048
MLA_HEADS = 8
MLA_Q_RANK = 512
MLA_KV_RANK = 256
MLA_NOPE_DIM = 128
MLA_ROPE_DIM = 64
MLA_V_DIM = 128
ROPE_THETA = 10000.0
MOBA_HEADS = 8
MOBA_HEAD_DIM = 128
MOBA_BLOCK = 256
MOBA_TOPK = 3
T5_BUCKETS = 32
T5_MAX_DISTANCE = 128
MOE_GROUPS = 4
MOE_EXPERTS_PER_GROUP = 8
MOE_N_EXPERTS = MOE_GROUPS * MOE_EXPERTS_PER_GROUP
MOE_D_FF = 512
LN_EPS = 1e-5
RMS_EPS = 1e-6
MOBA_WIDTH = MOBA_HEADS * MOBA_HEAD_DIM

LANES = 128
SUBLANES = 8
QK_PAD = 256
V_PAD = 256
PART_A = 1024
NEG = -1e30
VMEM_LIMIT = 56 * 1024 * 1024

F32 = jnp.float32
BF16 = jnp.bfloat16


def _cparams(sem, row_dma=False):
    return pltpu.CompilerParams(dimension_semantics=sem, vmem_limit_bytes=VMEM_LIMIT,
                                disable_bounds_checks=row_dma)


def _ln(x):
    mu = jnp.mean(x, axis=-1, keepdims=True)
    xc = x - mu
    var = jnp.mean(xc * xc, axis=-1, keepdims=True)
    return xc * lax.rsqrt(var + LN_EPS)


def _nt_dot(a, b):
    return lax.dot_general(a, b, (((1,), (1,)), ((), ())), preferred_element_type=F32)


def _ada_kernel(ct_ref, w_ref, b_ref, o_ref, *, batch):
    ct = ct_ref[...]
    ca = ct / (1.0 + jnp.exp(-ct))
    w = w_ref[...]
    rows = [jnp.sum(w * ca[:, b:b + 1], axis=0, keepdims=True) for b in range(batch)]
    o_ref[...] = jnp.concatenate(rows, axis=0) + b_ref[...]


def _ada_mod(c, w_ada, b_ada):
    batch, d = c.shape
    n = w_ada.shape[1]
    tn = 1024
    ct = jnp.zeros((d, LANES), F32).at[:, :batch].set(c.T)
    return pl.pallas_call(
        functools.partial(_ada_kernel, batch=batch),
        grid=(n // tn,),
        in_specs=[pl.BlockSpec((d, LANES), lambda j: (0, 0)),
                  pl.BlockSpec((d, tn), lambda j: (0, j)),
                  pl.BlockSpec((1, tn), lambda j: (0, j))],
        out_specs=pl.BlockSpec((batch, tn), lambda j: (0, j)),
        out_shape=jax.ShapeDtypeStruct((batch, n), F32),
        compiler_params=_cparams(("arbitrary",)),
        name="ada_mod",
    )(ct, w_ada, b_ada.reshape(1, n))


def _inproj_kernel(x_ref, mod_ref, w_ref, a_ref, b_ref, h_scr):
    j = pl.program_id(1)

    @pl.when(j == 0)
    def _():
        rows = 256
        for r in range(0, x_ref.shape[0], rows):
            h = _ln(x_ref[r:r + rows, :]) * (1.0 + mod_ref[0, 1:2, :]) + mod_ref[0, 0:1, :]
            h_scr[r:r + rows, :] = h.astype(BF16)
        a_ref[...] = _nt_dot(h_scr[...], w_ref[...])

    @pl.when(j > 0)
    def _():
        b_ref[...] = _nt_dot(h_scr[...], w_ref[...]).astype(BF16)


def _in_proj(xf, mod3, w4, seq):
    t, d = xf.shape
    n = w4.shape[0]
    tm, tn = 1024, PART_A
    per_b = seq // tm
    return pl.pallas_call(
        _inproj_kernel,
        grid=(t // tm, n // tn),
        in_specs=[pl.BlockSpec((tm, d), lambda i, j: (i, 0)),
                  pl.BlockSpec((1, 6, d), lambda i, j: (i // per_b, 0, 0)),
                  pl.BlockSpec((tn, d), lambda i, j: (j, 0))],
        out_specs=[pl.BlockSpec((tm, tn), lambda i, j: (i, 0)),
                   pl.BlockSpec((tm, tn), lambda i, j: (i, jnp.maximum(j - 1, 0)))],
        out_shape=[jax.ShapeDtypeStruct((t, PART_A), F32),
                   jax.ShapeDtypeStruct((t, n - PART_A), BF16)],
        scratch_shapes=[pltpu.VMEM((tm, d), BF16)],
        compiler_params=_cparams(("arbitrary", "arbitrary")),
        name="in_proj",
    )(xf, mod3, w4)


def _qkv_kernel(a_ref, wq_ref, wkv_ref, gq_ref, gkv_ref, cos_ref, sin_ref, q_ref, k_ref, v_ref):
    a = a_ref[...]
    cq = a[:, :MLA_Q_RANK]
    ckv = a[:, MLA_Q_RANK:MLA_Q_RANK + MLA_KV_RANK]
    r0 = MLA_Q_RANK + MLA_KV_RANK
    kr = a[:, r0:r0 + LANES]
    krs = a[:, r0 + LANES:PART_A]
    cqn = (cq * lax.rsqrt(jnp.mean(cq * cq, axis=-1, keepdims=True) + RMS_EPS) * gq_ref[...]).astype(BF16)
    ckvn = (ckv * lax.rsqrt(jnp.mean(ckv * ckv, axis=-1, keepdims=True) + RMS_EPS) * gkv_ref[...]).astype(BF16)
    cos = cos_ref[...]
    sin = sin_ref[...]
    krr = (kr * cos + krs * sin).astype(BF16)
    for h in range(MLA_HEADS):
        c0 = h * QK_PAD
        q2 = jnp.dot(cqn, wq_ref[:, c0:c0 + QK_PAD], preferred_element_type=F32)
        kv = jnp.dot(ckvn, wkv_ref[:, c0:c0 + QK_PAD], preferred_element_type=F32)
        rope = q2[:, LANES:]
        q_ref[:, c0:c0 + LANES] = q2[:, :LANES].astype(BF16)
        q_ref[:, c0 + LANES:c0 + QK_PAD] = (rope * cos + pltpu.roll(rope, LANES // 2, 1) * sin).astype(BF16)
        k_ref[:, c0:c0 + LANES] = kv[:, :LANES].astype(BF16)
        k_ref[:, c0 + LANES:c0 + QK_PAD] = krr
        v_ref[:, h * LANES:(h + 1) * LANES] = kv[:, LANES:].astype(BF16)


def _qkv(part_a, wq3, wkv, gq, gkv, cos_t, sin_t, seq):
    t = part_a.shape[0]
    tm = 512
    per_b = seq // tm
    hq = MLA_HEADS * QK_PAD
    return pl.pallas_call(
        _qkv_kernel,
        grid=(t // tm,),
        in_specs=[pl.BlockSpec((tm, PART_A), lambda i: (i, 0)),
                  pl.BlockSpec(wq3.shape, lambda i: (0, 0)),
                  pl.BlockSpec(wkv.shape, lambda i: (0, 0)),
                  pl.BlockSpec((1, MLA_Q_RANK), lambda i: (0, 0)),
                  pl.BlockSpec((1, MLA_KV_RANK), lambda i: (0, 0)),
                  pl.BlockSpec((tm, LANES), lambda i: (i % per_b, 0)),
                  pl.BlockSpec((tm, LANES), lambda i: (i % per_b, 0))],
        out_specs=[pl.BlockSpec((tm, hq), lambda i: (i, 0)),
                   pl.BlockSpec((tm, hq), lambda i: (i, 0)),
                   pl.BlockSpec((tm, MLA_HEADS * MLA_V_DIM), lambda i: (i, 0))],
        out_shape=[jax.ShapeDtypeStruct((t, hq), BF16),
                   jax.ShapeDtypeStruct((t, hq), BF16),
                   jax.ShapeDtypeStruct((t, MLA_HEADS * MLA_V_DIM), BF16)],
        compiler_params=_cparams(("arbitrary",)),
        name="qkv",
    )(part_a, wq3, wkv, gq, gkv, cos_t, sin_t)


ATTN_TILE = 512
ATTN_CHAINS = 8


def _attn_kernel(*refs, c, masked):
    nch = ATTN_CHAINS
    scratch = refs[-4 * nch:]
    tile = ATTN_TILE
    i = pl.program_id(2)
    if masked:
        q_ref, kin_ref, vin_ref, bias_ref, o_ref, v_ref, k_ref = refs[:7]
    else:
        q_ref, k_ref, vin_ref, o_ref, v_ref = refs[:5]

    @pl.when((pl.program_id(0) == 0) & (pl.program_id(1) == 0) & (i == 0))
    def _():
        seq = vin_ref.shape[0]
        v_ref[:, LANES:] = jnp.ones((seq, V_PAD - LANES), BF16)
        if masked:
            lane = lax.broadcasted_iota(jnp.int32, (seq, LANES), 1)
            own = jnp.right_shift(lax.broadcasted_iota(jnp.int32, (seq, LANES), 0), MOBA_BLOCK.bit_length() - 1)
            k_ref[:, LANES:] = jnp.where(lane == own, 1.0, 0.0).astype(BF16)

    @pl.when(i == 0)
    def _():
        v_ref[:, :LANES] = vin_ref[...]
        if masked:
            k_ref[:, :LANES] = kin_ref[...]

    chains = [dict(rows=slice(n * tile, (n + 1) * tile), s=scratch[4 * n:4 * n + 2], m=scratch[4 * n + 2],
                   acc=scratch[4 * n + 3]) for n in range(nch)]

    half = tile // 2

    def put_scores(chain, slot, t, diag=False):
        k0 = pl.multiple_of(t * tile, tile)
        q0 = chain["rows"].start
        buf = chain["s"][slot]
        if diag:
            buf[:half, :half] = _nt_dot(q_ref[q0:q0 + half, :], k_ref[pl.ds(k0, half), :])
            buf[half:, :] = _nt_dot(q_ref[q0 + half:q0 + tile, :], k_ref[pl.ds(k0, tile), :])
        else:
            buf[...] = _nt_dot(q_ref[q0:q0 + tile, :], k_ref[pl.ds(k0, tile), :])

    def softmax_update(chain, s, t, rows, keys):
        m_scr, acc_scr = chain["m"], chain["acc"]
        m_old = m_scr[rows, :]
        m_new = jnp.maximum(m_old, jnp.broadcast_to(jnp.max(s, axis=-1, keepdims=True), m_old.shape))
        alpha = jnp.exp2((m_old - m_new) * c)
        p = jnp.concatenate([jnp.exp2(((s[:, j * LANES:(j + 1) * LANES] - m_new) * c).astype(BF16))
                             for j in range(keys // LANES)], axis=1)
        v = v_ref[pl.ds(pl.multiple_of(t * tile, tile), keys), :]
        pv = jnp.dot(p, v, preferred_element_type=F32)
        acc_scr[rows, :] = jnp.concatenate([alpha, alpha], axis=1) * acc_scr[rows, :] + pv
        m_scr[rows, :] = m_new

    def tile_step(chain, slot, t, kind, prefetch=True, next_diag=False):
        if prefetch:
            put_scores(chain, 1 - slot, t + 1, diag=next_diag)
        buf = chain["s"][slot]
        if kind != "diag":
            s = buf[...]
            if masked and kind == "prev":
                s = s + bias_ref[0, 1]
            softmax_update(chain, s, t, slice(0, tile), tile)
            return
        for r0, keys in ((0, half), (half, tile)):
            rows = slice(r0, r0 + half)
            s = buf[rows, :keys]
            if masked:
                s = s + bias_ref[0, 0, rows, :keys]
            else:
                row = lax.broadcasted_iota(jnp.int32, (half, keys), 0) + r0
                col = lax.broadcasted_iota(jnp.int32, (half, keys), 1)
                s = jnp.where(row >= col, s, NEG)
            softmax_update(chain, s, t, rows, keys)

    for chain in chains:
        chain["m"][...] = jnp.full(chain["m"].shape, NEG, F32)
        chain["acc"][...] = jnp.zeros(chain["acc"].shape, F32)
        put_scores(chain, 0, 0)

    def far_pair(j, _):
        for slot in range(2):
            for chain in chains:
                tile_step(chain, slot, 2 * j + slot, "far")
        return 0

    first = nch * i
    if masked:
        lax.fori_loop(0, jnp.maximum(first // 2 - 1, 0), far_pair, 0)

        @pl.when(i > 0)
        def _():
            for chain in chains:
                tile_step(chain, 0, first - 2, "far")
            for n, chain in enumerate(chains):
                tile_step(chain, 1, first - 1, "prev" if n == 0 else "far", next_diag=(n == 0))
    else:
        lax.fori_loop(0, first // 2, far_pair, 0)

    for k in range(nch):
        for n, chain in enumerate(chains):
            if n < k:
                continue
            kind = "diag" if n == k else ("prev" if masked and n == k + 1 else "far")
            tile_step(chain, k % 2, first + k, kind, prefetch=(n != k), next_diag=(n == k + 1))

    for chain in chains:
        acc = chain["acc"]
        o_ref[chain["rows"], :] = (acc[:, :LANES] / acc[:, LANES:]).astype(o_ref.dtype)


def _attn_scratch(seq, masked):
    tile = ATTN_TILE
    wide = [pltpu.VMEM((seq, V_PAD), BF16)] + ([pltpu.VMEM((seq, QK_PAD), BF16)] if masked else [])
    per_chain = [pltpu.VMEM((tile, tile), F32), pltpu.VMEM((tile, tile), F32),
                 pltpu.VMEM((tile, LANES), F32), pltpu.VMEM((tile, V_PAD), F32)]
    return wide + per_chain * ATTN_CHAINS


def _mla_attention(q, k, v, batch, seq):
    tile = ATTN_CHAINS * ATTN_TILE
    nq = seq // tile
    c = math.log2(math.e) / math.sqrt(MLA_NOPE_DIM + MLA_ROPE_DIM)
    return pl.pallas_call(
        functools.partial(_attn_kernel, c=c, masked=False),
        grid=(batch, MLA_HEADS, nq),
        in_specs=[pl.BlockSpec((tile, QK_PAD), lambda b, h, i: (b * nq + i, h)),
                  pl.BlockSpec((seq, QK_PAD), lambda b, h, i: (b, h)),
                  pl.BlockSpec((seq, MLA_V_DIM), lambda b, h, i: (b, h))],
        out_specs=pl.BlockSpec((tile, MLA_V_DIM), lambda b, h, i: (b * nq + i, h)),
        out_shape=jax.ShapeDtypeStruct((batch * seq, MLA_HEADS * MLA_V_DIM), BF16),
        scratch_shapes=_attn_scratch(seq, False),
        compiler_params=_cparams(("arbitrary", "arbitrary", "arbitrary")),
        name="mla_attn",
    )(q, k, v)


def _moba_select_kernel(q_ref, k_ref, qa_ref, *, seq, nb):
    kf = k_ref[...].astype(F32)
    km = jnp.sum(kf.reshape(nb, MOBA_BLOCK, MOBA_HEAD_DIM), axis=1) * (1.0 / MOBA_BLOCK)
    km_hi = km.astype(BF16)
    km_lo = (km - km_hi.astype(F32)).astype(BF16)
    q = q_ref[...]
    gate = _nt_dot(km_hi, q) + _nt_dot(km_lo, q)
    shift = MOBA_BLOCK.bit_length() - 1
    blk = lax.broadcasted_iota(jnp.int32, (nb, seq), 0)
    qblk = jnp.right_shift(lax.broadcasted_iota(jnp.int32, (nb, seq), 1), shift)
    g = jnp.where(blk < qblk, gate, NEG)
    visible = blk == qblk
    for _ in range(MOBA_TOPK):
        mx = jnp.max(g, axis=0, keepdims=True)
        first = jnp.min(jnp.where(g == mx, blk, nb), axis=0, keepdims=True)
        pick = (blk == first) & (mx > 0.5 * NEG)
        visible = visible | pick
        g = jnp.where(pick, NEG, g)
    mask_t = jnp.concatenate([jnp.where(visible, 0.0, NEG), jnp.zeros((LANES - nb, seq), F32)], axis=0)
    qa_ref[:, :MOBA_HEAD_DIM] = q
    qa_ref[:, MOBA_HEAD_DIM:] = mask_t.T.astype(BF16)


def _moba_select(qkv_mo, batch, seq):
    nb = seq // MOBA_BLOCK
    assert MOBA_HEAD_DIM == LANES and nb <= QK_PAD - MOBA_HEAD_DIM
    return pl.pallas_call(
        functools.partial(_moba_select_kernel, seq=seq, nb=nb),
        grid=(batch, MOBA_HEADS),
        in_specs=[pl.BlockSpec((seq, MOBA_HEAD_DIM), lambda b, h: (b, h)),
                  pl.BlockSpec((seq, MOBA_HEAD_DIM), lambda b, h: (b, MOBA_HEADS + h))],
        out_specs=pl.BlockSpec((seq, QK_PAD), lambda b, h: (b, h)),
        out_shape=jax.ShapeDtypeStruct((batch * seq, MOBA_HEADS * QK_PAD), BF16),
        compiler_params=_cparams(("arbitrary", "arbitrary")),
        name="moba_select",
    )(qkv_mo, qkv_mo)


def _t5_kernel(tab_ref, o_ref, *, inv_scale):
    h = pl.program_id(0)
    r = lax.broadcasted_iota(jnp.int32, (LANES, LANES), 0)
    c = lax.broadcasted_iota(jnp.int32, (LANES, LANES), 1)
    max_exact = T5_BUCKETS // 2
    far = tab_ref[T5_BUCKETS - 1, h]

    def block(offset):
        rel = offset + r - c
        n = jnp.maximum(rel, 0)
        nf = jnp.maximum(n, 1).astype(F32)
        large = max_exact + (jnp.log(nf / max_exact) / math.log(T5_MAX_DISTANCE / max_exact)
                             * (T5_BUCKETS - max_exact)).astype(jnp.int32)
        large = jnp.minimum(large, T5_BUCKETS - 1)
        bucket = jnp.where(n < max_exact, n, large)
        bias = jnp.zeros((LANES, LANES), F32)
        for j in range(T5_BUCKETS):
            bias = jnp.where(bucket == j, tab_ref[j, h], bias)
        return jnp.where(rel >= 0, (bias - far) * inv_scale, NEG)

    near = {0: block(0), 1: block(LANES)}
    nblk = ATTN_TILE // LANES
    for d in range(2):
        for i in range(nblk):
            for j in range(nblk):
                k = d * nblk + i - j
                if k < 0:
                    val = jnp.full((LANES, LANES), NEG, F32)
                else:
                    val = near.get(k, jnp.zeros((LANES, LANES), F32))
                o_ref[0, d, i * LANES:(i + 1) * LANES, j * LANES:(j + 1) * LANES] = val


def _t5_tiles(t5_table):
    assert LANES >= T5_MAX_DISTANCE
    tile = ATTN_TILE
    return pl.pallas_call(
        functools.partial(_t5_kernel, inv_scale=math.sqrt(MOBA_HEAD_DIM)),
        grid=(MOBA_HEADS,),
        in_specs=[pl.BlockSpec(memory_space=pltpu.SMEM)],
        out_specs=pl.BlockSpec((1, 2, tile, tile), lambda h: (h, 0, 0, 0)),
        out_shape=jax.ShapeDtypeStruct((MOBA_HEADS, 2, tile, tile), F32),
        compiler_params=_cparams(("arbitrary",)),
        name="t5_tiles",
    )(t5_table)


def _moba_attention(q_wide, qkv_mo, bias, batch, seq):
    tile = ATTN_CHAINS * ATTN_TILE
    nq = seq // tile
    c = math.log2(math.e) / math.sqrt(MOBA_HEAD_DIM)
    dh = MOBA_HEAD_DIM
    return pl.pallas_call(
        functools.partial(_attn_kernel, c=c, masked=True),
        grid=(batch, MOBA_HEADS, nq),
        in_specs=[pl.BlockSpec((tile, QK_PAD), lambda b, h, i: (b * nq + i, h)),
                  pl.BlockSpec((seq, dh), lambda b, h, i: (b, MOBA_HEADS + h)),
                  pl.BlockSpec((seq, dh), lambda b, h, i: (b, 2 * MOBA_HEADS + h)),
                  pl.BlockSpec((1, 2, ATTN_TILE, ATTN_TILE), lambda b, h, i: (h, 0, 0, 0))],
        out_specs=pl.BlockSpec((tile, dh), lambda b, h, i: (b * nq + i, h)),
        out_shape=jax.ShapeDtypeStruct((batch * seq, MOBA_WIDTH), BF16),
        scratch_shapes=_attn_scratch(seq, True),
        compiler_params=_cparams(("arbitrary", "arbitrary", "arbitrary")),
        name="moba_attn",
    )(q_wide, qkv_mo, qkv_mo, bias)


def _outproj_kernel(oa_ref, ob_ref, wa_ref, wb_ref, x_ref, mod_ref, g_ref, b_ref, wr_ref,
                    x1_ref, h2_ref, lg_ref, y_a, y_b, *, alpha, n_tiles):
    i = pl.program_id(0)
    tm, d = y_a.shape
    chunks = 4
    cn, cr = d // chunks, tm // chunks

    def matmul_into(y_ref, c):
        cols = slice(c * cn, (c + 1) * cn)
        y_ref[:, cols] = (jnp.dot(oa_ref[...], wa_ref[:, cols], preferred_element_type=F32)
                          + jnp.dot(ob_ref[...], wb_ref[:, cols], preferred_element_type=F32))

    def epilogue(y_ref, c):
        rows = slice(c * cr, (c + 1) * cr)
        z = alpha * x_ref[rows, :] + mod_ref[0, 2:3, :] * y_ref[rows, :]
        x1 = _ln(z) * g_ref[...] + b_ref[...]
        x1_ref[rows, :] = x1
        h2 = _ln(x1) * (1.0 + mod_ref[0, 4:5, :]) + mod_ref[0, 3:4, :]
        h_hi = h2.astype(BF16)
        h_lo = (h2 - h_hi.astype(F32)).astype(BF16)
        h_bits = pltpu.bitcast(h_hi.astype(F32), jnp.uint32)
        h2_ref[rows, :] = h_bits[:, :d // 2] | jnp.right_shift(h_bits[:, d // 2:], 16)
        zz = (jnp.dot(h_hi, wr_ref[...], preferred_element_type=F32)
              + jnp.dot(h_lo, wr_ref[...], preferred_element_type=F32))
        lg_ref[rows, :] = zz + pltpu.roll(zz, LANES // 2, 1)

    def step(y_new, y_old):
        for c in range(chunks):
            if y_new is not None:
                matmul_into(y_new, c)
            if y_old is not None:
                epilogue(y_old, c)

    inner = (i > 0) & (i < n_tiles)
    pl.when(i == 0)(lambda: step(y_a, None))
    pl.when(inner & (i % 2 == 0))(lambda: step(y_a, y_b))
    pl.when(inner & (i % 2 == 1))(lambda: step(y_b, y_a))
    pl.when(i == n_tiles)(lambda: step(None, y_b if n_tiles % 2 == 0 else y_a))


def _out_proj(o_mla, o_moba, wo, xf, mod3, ln_g, ln_b, wr, seq, alpha):
    t, d = xf.shape
    tm = 512
    n = t // tm
    per_b = seq // tm
    ka, kb = o_mla.shape[1], o_moba.shape[1]
    assert ka == kb and wo.shape[0] == ka + kb
    once = pl.Buffered(1)

    def cur(i):
        return (jnp.minimum(i, n - 1), 0)

    def lag(i):
        return (jnp.maximum(i - 1, 0), 0)

    return pl.pallas_call(
        functools.partial(_outproj_kernel, alpha=alpha, n_tiles=n),
        grid=(n + 1,),
        in_specs=[pl.BlockSpec((tm, ka), cur),
                  pl.BlockSpec((tm, kb), cur),
                  pl.BlockSpec((ka, d), lambda i: (0, 0), pipeline_mode=once),
                  pl.BlockSpec((kb, d), lambda i: (1, 0), pipeline_mode=once),
                  pl.BlockSpec((tm, d), lag),
                  pl.BlockSpec((1, 6, d), lambda i: (jnp.maximum(i - 1, 0) // per_b, 0, 0)),
                  pl.BlockSpec((1, d), lambda i: (0, 0)),
                  pl.BlockSpec((1, d), lambda i: (0, 0)),
                  pl.BlockSpec((d, LANES), lambda i: (0, 0), pipeline_mode=once)],
        out_specs=[pl.BlockSpec((tm, d), lag),
                   pl.BlockSpec((tm, d // 2), lag),
                   pl.BlockSpec((tm, LANES), lag)],
        out_shape=[jax.ShapeDtypeStruct((t, d), F32),
                   jax.ShapeDtypeStruct((t, d // 2), jnp.uint32),
                   jax.ShapeDtypeStruct((t, LANES), F32)],
        scratch_shapes=[pltpu.VMEM((tm, d), F32), pltpu.VMEM((tm, d), F32)],
        compiler_params=_cparams(("arbitrary",)),
        name="out_proj",
    )(o_mla, o_moba, wo, wo, xf, mod3, ln_g, ln_b, wr)


def _route_kernel(lg_ref, br_ref, info_ref, cnt_ref, run_scr, *, tm):
    i = pl.program_id(0)

    @pl.when(i == 0)
    def _():
        run_scr[...] = jnp.zeros_like(run_scr)

    lg = lg_ref[...] + br_ref[...]
    lane = lax.broadcasted_iota(jnp.int32, (tm, LANES), 1)
    e_lo, e_hi = MOE_GROUPS, MOE_GROUPS + MOE_N_EXPERTS
    is_g = lane < e_lo
    gl = jnp.where(is_g, lg, NEG)
    gmax = jnp.max(gl, axis=-1, keepdims=True)
    gidx = jnp.min(jnp.where(gl == gmax, lane, LANES), axis=-1, keepdims=True)
    g_p = 1.0 / jnp.sum(jnp.where(is_g, jnp.exp(gl - gmax), 0.0), axis=-1, keepdims=True)
    grp_of_lane = jnp.right_shift(lane - e_lo, MOE_EXPERTS_PER_GROUP.bit_length() - 1)
    in_grp = (lane >= e_lo) & (lane < e_hi) & (grp_of_lane == gidx)
    el = jnp.where(in_grp, lg, NEG)
    m1 = jnp.max(el, axis=-1, keepdims=True)
    l1 = jnp.min(jnp.where(el == m1, lane, LANES), axis=-1, keepdims=True)
    el2 = jnp.where(lane == l1, NEG, el)
    m2 = jnp.max(el2, axis=-1, keepdims=True)
    l2 = jnp.min(jnp.where(el2 == m2, lane, LANES), axis=-1, keepdims=True)
    zsum = jnp.sum(jnp.where(in_grp, jnp.exp(el - m1), 0.0), axis=-1, keepdims=True)
    p1 = 1.0 / zsum
    p2 = jnp.exp(m2 - m1) / zsum
    wa = g_p * (p1 / (p1 + p2))
    wb = g_p * (p2 / (p1 + p2))
    hot_a = lane == l1
    hot_b = lane == l2
    onehot = jnp.where(hot_a | hot_b, 1.0, 0.0)
    r = lax.broadcasted_iota(jnp.int32, (tm, tm), 0)
    c = lax.broadcasted_iota(jnp.int32, (tm, tm), 1)
    lower = jnp.where(c < r, 1.0, 0.0).astype(BF16)
    before = jnp.dot(lower, onehot.astype(BF16), preferred_element_type=F32) + run_scr[...]
    rank_a = jnp.sum(jnp.where(hot_a, before, 0.0), axis=-1, keepdims=True)
    rank_b = jnp.sum(jnp.where(hot_b, before, 0.0), axis=-1, keepdims=True)
    run_scr[...] += jnp.sum(onehot, axis=0, keepdims=True)
    info = jnp.zeros((tm, LANES), F32)
    for k, val in enumerate([(l1 - e_lo).astype(F32), (l2 - e_lo).astype(F32), wa, wb, rank_a, rank_b]):
        info = jnp.where(lane == k, val, info)
    info_ref[...] = info
    cnt_ref[...] = run_scr[...]


def _route(logits, br):
    t = logits.shape[0]
    tm = 512
    return pl.pallas_call(
        functools.partial(_route_kernel, tm=tm),
        grid=(t // tm,),
        in_specs=[pl.BlockSpec((tm, LANES), lambda i: (i, 0)),
                  pl.BlockSpec((1, LANES), lambda i: (0, 0))],
        out_specs=[pl.BlockSpec((tm, LANES), lambda i: (i, 0)),
                   pl.BlockSpec((1, LANES), lambda i: (0, 0))],
        out_shape=[jax.ShapeDtypeStruct((t, LANES), F32),
                   jax.ShapeDtypeStruct((1, LANES), F32)],
        scratch_shapes=[pltpu.VMEM((1, LANES), F32)],
        compiler_params=_cparams(("arbitrary",)),
        name="route",
    )(logits, br)


def _pos_kernel(info_ref, start_ref, pos_ref):
    info = info_ref[...]
    tm = info.shape[0]
    lane = lax.broadcasted_iota(jnp.int32, (tm, LANES), 1)
    start = start_ref[...]
    cols = []
    for k in range(2):
        e = jnp.sum(jnp.where(lane == k, info, 0.0), axis=-1, keepdims=True).astype(jnp.int32)
        rank = jnp.sum(jnp.where(lane == 4 + k, info, 0.0), axis=-1, keepdims=True)
        base = jnp.sum(jnp.where(lane == e + MOE_GROUPS, start, 0.0), axis=-1, keepdims=True)
        cols.append(base + rank)
    wide = jnp.where(lane == 0, cols[0], jnp.where(lane == 1, cols[1], 0.0))
    pos_ref[...] = wide.T[:pos_ref.shape[0], :].astype(jnp.int32)


def _positions(info, start_lanes):
    t = info.shape[0]
    tm = 1024
    return pl.pallas_call(
        _pos_kernel,
        grid=(t // tm,),
        in_specs=[pl.BlockSpec((tm, LANES), lambda i: (i, 0)),
                  pl.BlockSpec((1, LANES), lambda i: (0, 0))],
        out_specs=pl.BlockSpec((8, tm), lambda i: (0, i)),
        out_shape=jax.ShapeDtypeStruct((8, t), jnp.int32),
        compiler_params=_cparams(("arbitrary",)),
        name="positions",
    )(info, start_lanes)


def _row_copy(src_ref, src_row, dst_ref, dst_row, sem):
    return pltpu.make_async_copy(src_ref.at[pl.ds(src_row, 1)], dst_ref.at[pl.ds(dst_row, 1)], sem)


def _dispatch_kernel(pa_ref, pb_ref, pad0_ref, padn_ref, used_ref, h_ref, xs_ref, ztile, sem, zsem,
                     *, tm, tr, n_tiles):
    i = pl.program_id(0)
    base = i * tm

    def zero_fill(act):
        def whole_tile(j, _):
            act(pltpu.make_async_copy(ztile, xs_ref.at[pl.ds(pl.multiple_of(j * tr, tr), tr)], zsem))
            return 0

        lax.fori_loop(used_ref[0], n_tiles, whole_tile, 0)

        def expert_pad(e, _):
            n, start = padn_ref[e], pad0_ref[e]
            head = jnp.minimum(jnp.bitwise_and(-start, SUBLANES - 1), n)
            for k in range(SUBLANES - 1):
                pl.when(k < head)(functools.partial(act, _row_copy(ztile, 0, xs_ref, start + k, zsem)))
            off = start + head
            groups = jnp.right_shift(n - head, SUBLANES.bit_length() - 1)
            for bit in reversed(range((tr // SUBLANES - 1).bit_length())):
                size = SUBLANES << bit
                take = jnp.bitwise_and(jnp.right_shift(groups, bit), 1)
                dst = xs_ref.at[pl.ds(pl.multiple_of(off, SUBLANES), size)]
                pl.when(take == 1)(functools.partial(act, pltpu.make_async_copy(ztile.at[pl.ds(0, size)], dst, zsem)))
                off = off + take * size
            return 0

        lax.fori_loop(0, MOE_N_EXPERTS, expert_pad, 0)

    @pl.when(i == 0)
    def _():
        ztile[...] = jnp.zeros(ztile.shape, ztile.dtype)
        zero_fill(lambda cp: cp.start())

    for t in range(tm):
        _row_copy(h_ref, t, xs_ref, pa_ref[base + t], sem).start()
        _row_copy(h_ref, t, xs_ref, pb_ref[base + t], sem).start(priority=1)
    for _ in range(2):
        pltpu.make_async_copy(h_ref, xs_ref.at[pl.ds(0, tm)], sem).wait()

    @pl.when(i == pl.num_programs(0) - 1)
    def _():
        zero_fill(lambda cp: cp.wait())


def _dispatch(pos_a, pos_b, pad0, padn, used, h2, tr, n_tiles):
    t, d = h2.shape
    tm = 512
    grid_spec = pltpu.PrefetchScalarGridSpec(
        num_scalar_prefetch=5,
        grid=(t // tm,),
        in_specs=[pl.BlockSpec((tm, d), lambda i, *_: (i, 0))],
        out_specs=pl.BlockSpec(memory_space=pl.ANY),
        scratch_shapes=[pltpu.VMEM((tr, d), h2.dtype), pltpu.SemaphoreType.DMA(()), pltpu.SemaphoreType.DMA(())],
    )
    return pl.pallas_call(
        functools.partial(_dispatch_kernel, tm=tm, tr=tr, n_tiles=n_tiles),
        grid_spec=grid_spec,
        out_shape=jax.ShapeDtypeStruct((n_tiles * tr, d), h2.dtype),
        compiler_params=_cparams(("arbitrary",), row_dma=True),
        name="dispatch",
    )(pos_a, pos_b, pad0, padn, used, h2)


def _experts_kernel(texp_ref, tidx_ref, fetch_ref, slot_ref, used_ref, x_ref, w1_hbm, w3_hbm, w2_hbm, o_ref,
                    w1_f32, w3_f32, w2_f32, w1_scr, w3_scr, w2_scr, sems):
    i = pl.program_id(0)
    n_steps = pl.num_programs(0)
    prev = texp_ref[jnp.maximum(i - 1, 0)]
    first_of_expert = (i == 0) | (texp_ref[i] != prev)
    slot = slot_ref[i]

    def weight_copies(e, s):
        return [pltpu.make_async_copy(w1_hbm.at[e], w1_f32.at[s], sems.at[s, 0]),
                pltpu.make_async_copy(w3_hbm.at[e], w3_f32.at[s], sems.at[s, 1]),
                pltpu.make_async_copy(w2_hbm.at[e], w2_f32.at[s], sems.at[s, 2])]

    @pl.when(i == 0)
    def _():
        for s in range(2):
            for cp in weight_copies(fetch_ref[n_steps + s], s):
                cp.start(priority=1)

    def tile(cast):
        bits = x_ref[...]
        x = jnp.concatenate(
            [pltpu.bitcast(jnp.bitwise_and(bits, jnp.uint32(0xFFFF0000)), F32).astype(BF16),
             pltpu.bitcast(jnp.left_shift(bits, 16), F32).astype(BF16)], axis=1)
        if cast:
            w1_scr[...] = w1_f32[slot].astype(BF16)
        a = jnp.dot(x, w1_scr[...], preferred_element_type=F32)
        if cast:
            w3_scr[...] = w3_f32[slot].astype(BF16)
        b = jnp.dot(x, w3_scr[...], preferred_element_type=F32)
        if cast:
            w2_scr[...] = w2_f32[slot].astype(BF16)
            for cp in weight_copies(fetch_ref[i], slot):
                cp.start(priority=1)
        hid = (a / (1.0 + jnp.exp(-a))) * b
        o_ref[...] = jnp.dot(hid.astype(BF16), w2_scr[...], preferred_element_type=F32)

    @pl.when(first_of_expert)
    def _():
        for cp in weight_copies(texp_ref[i], slot):
            cp.wait()
        tile(cast=True)

    @pl.when(jnp.logical_not(first_of_expert) & (i < used_ref[0]))
    def _():
        tile(cast=False)

    @pl.when(i == n_steps - 1)
    def _():
        for s in range(2):
            for cp in weight_copies(texp_ref[i], s):
                cp.wait()

    @pl.when(i >= used_ref[0])
    def _():
        o_ref[...] = jnp.zeros_like(o_ref)


def _experts(texp, tidx, fetch, slot, used, xs, w1, w3, w2, tr):
    rows = xs.shape[0]
    nt = rows // tr
    d, f = w1.shape[1:]
    assert xs.shape[1] * 2 == d and xs.dtype == jnp.uint32
    grid_spec = pltpu.PrefetchScalarGridSpec(
        num_scalar_prefetch=5,
        grid=(nt,),
        in_specs=[pl.BlockSpec((tr, d // 2), lambda i, te, ti, *_: (ti[i], 0)),
                  pl.BlockSpec(memory_space=pl.ANY),
                  pl.BlockSpec(memory_space=pl.ANY),
                  pl.BlockSpec(memory_space=pl.ANY)],
        out_specs=pl.BlockSpec((tr, d), lambda i, *_: (i, 0)),
        scratch_shapes=[pltpu.VMEM((2, d, f), F32), pltpu.VMEM((2, d, f), F32), pltpu.VMEM((2, f, d), F32),
                        pltpu.VMEM((d, f), BF16), pltpu.VMEM((d, f), BF16), pltpu.VMEM((f, d), BF16),
                        pltpu.SemaphoreType.DMA((2, 3))],
    )
    return pl.pallas_call(
        _experts_kernel,
        grid_spec=grid_spec,
        out_shape=jax.ShapeDtypeStruct((rows, d), F32),
        compiler_params=_cparams(("arbitrary",)),
        name="experts",
    )(texp, tidx, fetch, slot, used, xs, w1, w3, w2)


def _combine_kernel(pa_ref, pb_ref, ys_ref, x1_ref, info_ref, mod_ref, g_ref, b_ref, o_ref,
                    buf_a, buf_b, sems, *, tm, alpha):
    i = pl.program_id(0)
    last = pl.num_programs(0) - 1

    def issue_row(tile, slot, t):
        _row_copy(ys_ref, pa_ref[tile * tm + t], buf_a.at[slot], t, sems.at[slot]).start()
        _row_copy(ys_ref, pb_ref[tile * tm + t], buf_b.at[slot], t, sems.at[slot]).start(priority=1)

    def combine_rows(slot, rows):
        info = info_ref[rows, :]
        y = info[:, 2:3] * buf_a[slot, rows, :] + info[:, 3:4] * buf_b[slot, rows, :]
        z = alpha * x1_ref[rows, :] + mod_ref[0, 5:6, :] * y
        o_ref[rows, :] = _ln(z) * g_ref[...] + b_ref[...]

    @pl.when(i == 0)
    def _():
        lax.fori_loop(0, tm, lambda t, _: (issue_row(0, 0, t), 0)[1], 0, unroll=8)

    slot = i % 2
    for buf in (buf_a, buf_b):
        pltpu.make_async_copy(ys_ref.at[pl.ds(0, tm)], buf.at[slot], sems.at[slot]).wait()

    chunk = 32

    @pl.when(i < last)
    def _():
        for r in range(0, tm, chunk):
            for t in range(r, r + chunk):
                issue_row(i + 1, 1 - slot, t)
            combine_rows(slot, slice(r, r + chunk))

    @pl.when(i == last)
    def _():
        combine_rows(slot, slice(0, tm))


def _combine(pos_a, pos_b, ys, x1, info, mod3, ln_g, ln_b, seq, alpha):
    t, d = x1.shape
    tm = 512
    per_b = seq // tm
    grid_spec = pltpu.PrefetchScalarGridSpec(
        num_scalar_prefetch=2,
        grid=(t // tm,),
        in_specs=[pl.BlockSpec(memory_space=pl.ANY),
                  pl.BlockSpec((tm, d), lambda i, pa, pb: (i, 0)),
                  pl.BlockSpec((tm, LANES), lambda i, pa, pb: (i, 0)),
                  pl.BlockSpec((1, 6, d), lambda i, pa, pb: (i // per_b, 0, 0)),
                  pl.BlockSpec((1, d), lambda i, pa, pb: (0, 0)),
                  pl.BlockSpec((1, d), lambda i, pa, pb: (0, 0))],
        out_specs=pl.BlockSpec((tm, d), lambda i, pa, pb: (i, 0)),
        scratch_shapes=[pltpu.VMEM((2, tm, d), F32), pltpu.VMEM((2, tm, d), F32), pltpu.SemaphoreType.DMA((2,))],
    )
    return pl.pallas_call(
        functools.partial(_combine_kernel, tm=tm, alpha=alpha),
        grid_spec=grid_spec,
        out_shape=jax.ShapeDtypeStruct((t, d), F32),
        compiler_params=_cparams(("arbitrary",), row_dma=True),
        name="combine",
    )(pos_a, pos_b, ys, x1, info, mod3, ln_g, ln_b)


def _prep_w_in_kernel(w_ref, o_ref):
    r0 = MLA_Q_RANK + MLA_KV_RANK
    r1 = r0 + MLA_ROPE_DIM
    half = MLA_ROPE_DIM // 2
    z = jnp.zeros((LANES - MLA_ROPE_DIM, o_ref.shape[1]), BF16)
    o_ref[:r1, :] = w_ref[:r1, :].astype(BF16)
    o_ref[r1:r0 + LANES, :] = z
    o_ref[r0 + LANES:r0 + LANES + half, :] = w_ref[r0 + half:r1, :].astype(BF16)
    o_ref[r0 + LANES + half:r0 + LANES + MLA_ROPE_DIM, :] = w_ref[r0:r0 + half, :].astype(BF16)
    o_ref[r0 + LANES + MLA_ROPE_DIM:PART_A, :] = z
    o_ref[PART_A:, :] = w_ref[r1:, :].astype(BF16)


def _prep_w_in(w_t):
    n, k = w_t.shape
    tk = 512
    n_out = PART_A + 3 * MOBA_WIDTH
    assert n == MLA_Q_RANK + MLA_KV_RANK + MLA_ROPE_DIM + 3 * MOBA_WIDTH
    return pl.pallas_call(
        _prep_w_in_kernel,
        grid=(k // tk,),
        in_specs=[pl.BlockSpec((n, tk), lambda i: (0, i))],
        out_specs=pl.BlockSpec((n_out, tk), lambda i: (0, i)),
        out_shape=jax.ShapeDtypeStruct((n_out, k), BF16),
        compiler_params=_cparams(("arbitrary",)),
        name="prep_w_in",
    )(w_t)


def _prep_w_uq(w):
    r = w.shape[0]
    w = w.reshape(r, MLA_HEADS, MLA_NOPE_DIM + MLA_ROPE_DIM)
    half = MLA_ROPE_DIM // 2
    nope = w[:, :, :MLA_NOPE_DIM]
    x1 = w[:, :, MLA_NOPE_DIM:MLA_NOPE_DIM + half]
    x2 = w[:, :, MLA_NOPE_DIM + half:]
    assert 2 * MLA_ROPE_DIM == LANES
    return jnp.concatenate([nope, x1, x2, x2, x1], axis=2).reshape(r, MLA_HEADS * QK_PAD).astype(BF16)


def _rope_lanes(seq):
    inv = 1.0 / (ROPE_THETA ** (jnp.arange(0, MLA_ROPE_DIM, 2, dtype=F32) / MLA_ROPE_DIM))
    hi = (jnp.arange(seq // LANES, dtype=F32) * LANES)[:, None] * inv[None, :]
    lo = jnp.arange(LANES, dtype=F32)[:, None] * inv[None, :]
    ch, sh, cl, sl = jnp.cos(hi)[:, None, :], jnp.sin(hi)[:, None, :], jnp.cos(lo)[None], jnp.sin(lo)[None]
    cos = (ch * cl - sh * sl).reshape(seq, -1)
    sin = (sh * cl + ch * sl).reshape(seq, -1)
    z = jnp.zeros((seq, LANES - MLA_ROPE_DIM), F32)
    return jnp.concatenate([cos, cos, z], axis=1), jnp.concatenate([-sin, sin, z], axis=1)


def _prep_router(w_rg, b_rg, w_re, b_re):
    d = w_rg.shape[0]
    w = jnp.concatenate([w_rg, w_re], axis=1)
    n = w.shape[1]
    hi = w.astype(BF16)
    lo = (w - hi.astype(F32)).astype(BF16)
    z = jnp.zeros((d, LANES // 2 - n), BF16)
    wr = jnp.concatenate([hi, z, lo, z], axis=1)
    br = jnp.zeros((1, LANES), F32).at[0, :n].set(jnp.concatenate([b_rg, b_re]))
    return wr, br


def _layer(xf, mod3, batch, seq, depth_alpha, w_in, q_norm_g, w_uq, kv_norm_g, w_ukv, w_out, bias_tiles,
           cos_t, sin_t, ln1_g, ln1_b, w_rg, b_rg, w_re, b_re, w1, w3, w2, ln2_g, ln2_b):
    t, d = xf.shape
    part_a, qkv_mo = _in_proj(xf, mod3, _prep_w_in(w_in.T), seq)
    q, k, v = _qkv(part_a, _prep_w_uq(w_uq), w_ukv.astype(BF16), q_norm_g.reshape(1, -1),
                   kv_norm_g.reshape(1, -1), cos_t, sin_t, seq)
    o_mla = _mla_attention(q, k, v, batch, seq)
    q_wide = _moba_select(qkv_mo, batch, seq)
    o_moba = _moba_attention(q_wide, qkv_mo, bias_tiles, batch, seq)
    wo = w_out.astype(BF16)
    wr, br = _prep_router(w_rg, b_rg, w_re, b_re)
    x1, h2, logits = _out_proj(o_mla, o_moba, wo, xf, mod3, ln1_g.reshape(1, d),
                               ln1_b.reshape(1, d), wr, seq, depth_alpha)
    info, counts = _route(logits, br)
    tr = 256
    nt = (2 * t) // tr + MOE_N_EXPERTS
    cnt = counts[0, MOE_GROUPS:MOE_GROUPS + MOE_N_EXPERTS].astype(jnp.int32)
    ntile = (cnt + tr - 1) // tr
    tile_end = jnp.cumsum(ntile)
    tile_start = tile_end - ntile
    used = tile_end[-1]
    start_lanes = jnp.zeros((1, LANES), F32).at[0, MOE_GROUPS:MOE_GROUPS + MOE_N_EXPERTS].set(
        (tile_start * tr).astype(F32))
    pos = _positions(info, start_lanes)
    pos_a, pos_b = pos[0], pos[1]
    tidx = jnp.minimum(jnp.arange(nt, dtype=jnp.int32), used - 1)
    texp = jnp.sum(tidx[:, None] >= tile_end[None, :], axis=1).astype(jnp.int32)
    eids = jnp.arange(MOE_N_EXPERTS, dtype=jnp.int32)
    owns = ntile > 0
    position = jnp.cumsum(owns) - 1
    two_later = owns[None, :] & (position[None, :] == position[:, None] + 2)
    fetch_of = jnp.where(two_later.any(axis=1), jnp.argmax(two_later, axis=1), eids).astype(jnp.int32)
    by_position = jnp.argmax(owns[None, :] & (position[None, :] == jnp.arange(2)[:, None]), axis=1)
    fetch = jnp.concatenate([fetch_of[texp], by_position.astype(jnp.int32)])
    slot = (position[texp] % 2).astype(jnp.int32)
    used1 = used.reshape(1).astype(jnp.int32)
    pad0 = (tile_start * tr + cnt).astype(jnp.int32)
    padn = (ntile * tr - cnt).astype(jnp.int32)
    xs = _dispatch(pos_a, pos_b, pad0, padn, used1, h2, tr, nt)
    ys = _experts(texp, tidx, fetch, slot, used1, xs, w1, w3, w2, tr)
    return _combine(pos_a, pos_b, ys, x1, info, mod3, ln2_g.reshape(1, d), ln2_b.reshape(1, d), seq, depth_alpha)


def kernel(x, c, w_ada, b_ada, w_in, q_norm_g, w_uq, kv_norm_g, w_ukv, w_out, t5_table, ln1_g, ln1_b,
           w_router_group, b_router_group, w_router_expert, b_router_expert, w1, w3, w2, ln2_g, ln2_b):
    batch, seq, d = x.shape
    depth = w_ada.shape[0]
    alpha = (2.0 * depth) ** 0.25
    cos_t, sin_t = _rope_lanes(seq)
    bias_tiles = _t5_tiles(t5_table)
    xf = x.reshape(batch * seq, d)
    for l in range(depth):
        mod3 = _ada_mod(c, w_ada[l], b_ada[l]).reshape(batch, 6, d)
        xf = _layer(xf, mod3, batch, seq, alpha, w_in[l], q_norm_g[l], w_uq[l], kv_norm_g[l], w_ukv[l],
                    w_out[l], bias_tiles, cos_t, sin_t, ln1_g[l], ln1_b[l], w_router_group[l],
                    b_router_group[l], w_router_expert[l], b_router_expert[l], w1[l], w3[l], w2[l],
                    ln2_g[l], ln2_b[l])
    return xf.reshape(batch, seq, d)
```

```python
import functools
import math

import jax
import jax.numpy as jnp
from jax import lax
from jax.experimental import pallas as pl
from jax.experimental.pallas import tpu as pltpu

D_MODEL = 2048
MLA_HEADS = 8
MLA_Q_RANK = 512
MLA_KV_RANK = 256
MLA_NOPE_DIM = 128
MLA_ROPE_DIM = 64
MLA_V_DIM = 128
ROPE_THETA = 10000.0
MOBA_HEADS = 8
MOBA_HEAD_DIM = 128
MOBA_BLOCK = 256
MOBA_TOPK = 3
T5_BUCKETS = 32
T5_MAX_DISTANCE = 128
MOE_GROUPS = 4
MOE_EXPERTS_PER_GROUP = 8
MOE_N_EXPERTS = MOE_GROUPS * MOE_EXPERTS_PER_GROUP
MOE_D_FF = 512
LN_EPS = 1e-5
RMS_EPS = 1e-6
MOBA_WIDTH = MOBA_HEADS * MOBA_HEAD_DIM

LANES = 128
SUBLANES = 8
QK_PAD = 256
V_PAD = 256
PART_A = 1024
NEG = -1e30
VMEM_LIMIT = 56 * 1024 * 1024

F32 = jnp.float32
BF16 = jnp.bfloat16


def _cparams(sem, row_dma=False):
    return pltpu.CompilerParams(dimension_semantics=sem, vmem_limit_bytes=VMEM_LIMIT,
                                disable_bounds_checks=row_dma)


def _ln(x):
    mu = jnp.mean(x, axis=-1, keepdims=True)
    xc = x - mu
    var = jnp.mean(xc * xc, axis=-1, keepdims=True)
    return xc * lax.rsqrt(var + LN_EPS)


def _nt_dot(a, b):
    return lax.dot_general(a, b, (((1,), (1,)), ((), ())), preferred_element_type=F32)


def _ada_kernel(ct_ref, w_ref, b_ref, o_ref, *, batch):
    ct = ct_ref[...]
    ca = ct / (1.0 + jnp.exp(-ct))
    w = w_ref[...]
    rows = [jnp.sum(w * ca[:, b:b + 1], axis=0, keepdims=True) for b in range(batch)]
    o_ref[...] = jnp.concatenate(rows, axis=0) + b_ref[...]


def _ada_mod(c, w_ada, b_ada):
    batch, d = c.shape
    n = w_ada.shape[1]
    tn = 1024
    ct = jnp.zeros((d, LANES), F32).at[:, :batch].set(c.T)
    return pl.pallas_call(
        functools.partial(_ada_kernel, batch=batch),
        grid=(n // tn,),
        in_specs=[pl.BlockSpec((d, LANES), lambda j: (0, 0)),
                  pl.BlockSpec((d, tn), lambda j: (0, j)),
                  pl.BlockSpec((1, tn), lambda j: (0, j))],
        out_specs=pl.BlockSpec((batch, tn), lambda j: (0, j)),
        out_shape=jax.ShapeDtypeStruct((batch, n), F32),
        compiler_params=_cparams(("arbitrary",)),
        name="ada_mod",
    )(ct, w_ada, b_ada.reshape(1, n))


def _inproj_kernel(x_ref, mod_ref, w_ref, a_ref, b_ref, h_scr):
    j = pl.program_id(1)

    @pl.when(j == 0)
    def _():
        rows = 256
        for r in range(0, x_ref.shape[0], rows):
            h = _ln(x_ref[r:r + rows, :]) * (1.0 + mod_ref[0, 1:2, :]) + mod_ref[0, 0:1, :]
            h_scr[r:r + rows, :] = h.astype(BF16)
        a_ref[...] = _nt_dot(h_scr[...], w_ref[...])

    @pl.when(j > 0)
    def _():
        b_ref[...] = _nt_dot(h_scr[...], w_ref[...]).astype(BF16)


def _in_proj(xf, mod3, w4, seq):
    t, d = xf.shape
    n = w4.shape[0]
    tm, tn = 1024, PART_A
    per_b = seq // tm
    return pl.pallas_call(
        _inproj_kernel,
        grid=(t // tm, n // tn),
        in_specs=[pl.BlockSpec((tm, d), lambda i, j: (i, 0)),
                  pl.BlockSpec((1, 6, d), lambda i, j: (i // per_b, 0, 0)),
                  pl.BlockSpec((tn, d), lambda i, j: (j, 0))],
        out_specs=[pl.BlockSpec((tm, tn), lambda i, j: (i, 0)),
                   pl.BlockSpec((tm, tn), lambda i, j: (i, jnp.maximum(j - 1, 0)))],
        out_shape=[jax.ShapeDtypeStruct((t, PART_A), F32),
                   jax.ShapeDtypeStruct((t, n - PART_A), BF16)],
        scratch_shapes=[pltpu.VMEM((tm, d), BF16)],
        compiler_params=_cparams(("arbitrary", "arbitrary")),
        name="in_proj",
    )(xf, mod3, w4)


def _qkv_kernel(a_ref, wq_ref, wkv_ref, gq_ref, gkv_ref, cos_ref, sin_ref, q_ref, k_ref, v_ref):
    a = a_ref[...]
    cq = a[:, :MLA_Q_RANK]
    ckv = a[:, MLA_Q_RANK:MLA_Q_RANK + MLA_KV_RANK]
    r0 = MLA_Q_RANK + MLA_KV_RANK
    kr = a[:, r0:r0 + LANES]
    krs = a[:, r0 + LANES:PART_A]
    cqn = (cq * lax.rsqrt(jnp.mean(cq * cq, axis=-1, keepdims=True) + RMS_EPS) * gq_ref[...]).astype(BF16)
    ckvn = (ckv * lax.rsqrt(jnp.mean(ckv * ckv, axis=-1, keepdims=True) + RMS_EPS) * gkv_ref[...]).astype(BF16)
    cos = cos_ref[...]
    sin = sin_ref[...]
    krr = (kr * cos + krs * sin).astype(BF16)
    for h in range(MLA_HEADS):
        c0 = h * QK_PAD
        q2 = jnp.dot(cqn, wq_ref[:, c0:c0 + QK_PAD], preferred_element_type=F32)
        kv = jnp.dot(ckvn, wkv_ref[:, c0:c0 + QK_PAD], preferred_element_type=F32)
        rope = q2[:, LANES:]
        q_ref[:, c0:c0 + LANES] = q2[:, :LANES].astype(BF16)
        q_ref[:, c0 + LANES:c0 + QK_PAD] = (rope * cos + pltpu.roll(rope, LANES // 2, 1) * sin).astype(BF16)
        k_ref[:, c0:c0 + LANES] = kv[:, :LANES].astype(BF16)
        k_ref[:, c0 + LANES:c0 + QK_PAD] = krr
        v_ref[:, h * LANES:(h + 1) * LANES] = kv[:, LANES:].astype(BF16)


def _qkv(part_a, wq3, wkv, gq, gkv, cos_t, sin_t, seq):
    t = part_a.shape[0]
    tm = 512
    per_b = seq // tm
    hq = MLA_HEADS * QK_PAD
    return pl.pallas_call(
        _qkv_kernel,
        grid=(t // tm,),
        in_specs=[pl.BlockSpec((tm, PART_A), lambda i: (i, 0)),
                  pl.BlockSpec(wq3.shape, lambda i: (0, 0)),
                  pl.BlockSpec(wkv.shape, lambda i: (0, 0)),
                  pl.BlockSpec((1, MLA_Q_RANK), lambda i: (0, 0)),
                  pl.BlockSpec((1, MLA_KV_RANK), lambda i: (0, 0)),
                  pl.BlockSpec((tm, LANES), lambda i: (i % per_b, 0)),
                  pl.BlockSpec((tm, LANES), lambda i: (i % per_b, 0))],
        out_specs=[pl.BlockSpec((tm, hq), lambda i: (i, 0)),
                   pl.BlockSpec((tm, hq), lambda i: (i, 0)),
                   pl.BlockSpec((tm, MLA_HEADS * MLA_V_DIM), lambda i: (i, 0))],
        out_shape=[jax.ShapeDtypeStruct((t, hq), BF16),
                   jax.ShapeDtypeStruct((t, hq), BF16),
                   jax.ShapeDtypeStruct((t, MLA_HEADS * MLA_V_DIM), BF16)],
        compiler_params=_cparams(("arbitrary",)),
        name="qkv",
    )(part_a, wq3, wkv, gq, gkv, cos_t, sin_t)


ATTN_TILE = 512
ATTN_CHAINS = 8


def _attn_kernel(*refs, c, masked):
    nch = ATTN_CHAINS
    scratch = refs[-4 * nch:]
    tile = ATTN_TILE
    i = pl.program_id(2)
    if masked:
        q_ref, kin_ref, vin_ref, bias_ref, o_ref, v_ref, k_ref = refs[:7]
    else:
        q_ref, k_ref, vin_ref, o_ref, v_ref = refs[:5]

    @pl.when((pl.program_id(0) == 0) & (pl.program_id(1) == 0) & (i == 0))
    def _():
        seq = vin_ref.shape[0]
        v_ref[:, LANES:] = jnp.ones((seq, V_PAD - LANES), BF16)
        if masked:
            lane = lax.broadcasted_iota(jnp.int32, (seq, LANES), 1)
            own = jnp.right_shift(lax.broadcasted_iota(jnp.int32, (seq, LANES), 0), MOBA_BLOCK.bit_length() - 1)
            k_ref[:, LANES:] = jnp.where(lane == own, 1.0, 0.0).astype(BF16)

    @pl.when(i == 0)
    def _():
        v_ref[:, :LANES] = vin_ref[...]
        if masked:
            k_ref[:, :LANES] = kin_ref[...]

    chains = [dict(rows=slice(n * tile, (n + 1) * tile), s=scratch[4 * n:4 * n + 2], m=scratch[4 * n + 2],
                   acc=scratch[4 * n + 3]) for n in range(nch)]

    half = tile // 2

    def put_scores(chain, slot, t, diag=False):
        k0 = pl.multiple_of(t * tile, tile)
        q0 = chain["rows"].start
        buf = chain["s"][slot]
        if diag:
            buf[:half, :half] = _nt_dot(q_ref[q0:q0 + half, :], k_ref[pl.ds(k0, half), :])
            buf[half:, :] = _nt_dot(q_ref[q0 + half:q0 + tile, :], k_ref[pl.ds(k0, tile), :])
        else:
            buf[...] = _nt_dot(q_ref[q0:q0 + tile, :], k_ref[pl.ds(k0, tile), :])

    def softmax_update(chain, s, t, rows, keys):
        m_scr, acc_scr = chain["m"], chain["acc"]
        m_old = m_scr[rows, :]
        m_new = jnp.maximum(m_old, jnp.broadcast_to(jnp.max(s, axis=-1, keepdims=True), m_old.shape))
        alpha = jnp.exp2((m_old - m_new) * c)
        p = jnp.concatenate([jnp.exp2(((s[:, j * LANES:(j + 1) * LANES] - m_new) * c).astype(BF16))
                             for j in range(keys // LANES)], axis=1)
        v = v_ref[pl.ds(pl.multiple_of(t * tile, tile), keys), :]
        pv = jnp.dot(p, v, preferred_element_type=F32)
        acc_scr[rows, :] = jnp.concatenate([alpha, alpha], axis=1) * acc_scr[rows, :] + pv
        m_scr[rows, :] = m_new

    def tile_step(chain, slot, t, kind, prefetch=True, next_diag=False):
        if prefetch:
            put_scores(chain, 1 - slot, t + 1, diag=next_diag)
        buf = chain["s"][slot]
        if kind != "diag":
            s = buf[...]
            if masked and kind == "prev":
                s = s + bias_ref[0, 1]
            softmax_update(chain, s, t, slice(0, tile), tile)
            return
        for r0, keys in ((0, half), (half, tile)):
            rows = slice(r0, r0 + half)
            s = buf[rows, :keys]
            if masked:
                s = s + bias_ref[0, 0, rows, :keys]
            else:
                row = lax.broadcasted_iota(jnp.int32, (half, keys), 0) + r0
                col = lax.broadcasted_iota(jnp.int32, (half, keys), 1)
                s = jnp.where(row >= col, s, NEG)
            softmax_update(chain, s, t, rows, keys)

    for chain in chains:
        chain["m"][...] = jnp.full(chain["m"].shape, NEG, F32)
        chain["acc"][...] = jnp.zeros(chain["acc"].shape, F32)
        put_scores(chain, 0, 0)

    def far_pair(j, _):
        for slot in range(2):
            for chain in chains:
                tile_step(chain, slot, 2 * j + slot, "far")
        return 0

    first = nch * i
    if masked:
        lax.fori_loop(0, jnp.maximum(first // 2 - 1, 0), far_pair, 0)

        @pl.when(i > 0)
        def _():
            for chain in chains:
                tile_step(chain, 0, first - 2, "far")
            for n, chain in enumerate(chains):
                tile_step(chain, 1, first - 1, "prev" if n == 0 else "far", next_diag=(n == 0))
    else:
        lax.fori_loop(0, first // 2, far_pair, 0)

    for k in range(nch):
        for n, chain in enumerate(chains):
            if n < k:
                continue
            kind = "diag" if n == k else ("prev" if masked and n == k + 1 else "far")
            tile_step(chain, k % 2, first + k, kind, prefetch=(n != k), next_diag=(n == k + 1))

    for chain in chains:
        acc = chain["acc"]
        o_ref[chain["rows"], :] = (acc[:, :LANES] / acc[:, LANES:]).astype(o_ref.dtype)


def _attn_scratch(seq, masked):
    tile = ATTN_TILE
    wide = [pltpu.VMEM((seq, V_PAD), BF16)] + ([pltpu.VMEM((seq, QK_PAD), BF16)] if masked else [])
    per_chain = [pltpu.VMEM((tile, tile), F32), pltpu.VMEM((tile, tile), F32),
                 pltpu.VMEM((tile, LANES), F32), pltpu.VMEM((tile, V_PAD), F32)]
    return wide + per_chain * ATTN_CHAINS


def _mla_attention(q, k, v, batch, seq):
    tile = ATTN_CHAINS * ATTN_TILE
    nq = seq // tile
    c = math.log2(math.e) / math.sqrt(MLA_NOPE_DIM + MLA_ROPE_DIM)
    return pl.pallas_call(
        functools.partial(_attn_kernel, c=c, masked=False),
        grid=(batch, MLA_HEADS, nq),
        in_specs=[pl.BlockSpec((tile, QK_PAD), lambda b, h, i: (b * nq + i, h)),
                  pl.BlockSpec((seq, QK_PAD), lambda b, h, i: (b, h)),
                  pl.BlockSpec((seq, MLA_V_DIM), lambda b, h, i: (b, h))],
        out_specs=pl.BlockSpec((tile, MLA_V_DIM), lambda b, h, i: (b * nq + i, h)),
        out_shape=jax.ShapeDtypeStruct((batch * seq, MLA_HEADS * MLA_V_DIM), BF16),
        scratch_shapes=_attn_scratch(seq, False),
        compiler_params=_cparams(("arbitrary", "arbitrary", "arbitrary")),
        name="mla_attn",
    )(q, k, v)


def _moba_select_kernel(q_ref, k_ref, qa_ref, *, seq, nb):
    kf = k_ref[...].astype(F32)
    km = jnp.sum(kf.reshape(nb, MOBA_BLOCK, MOBA_HEAD_DIM), axis=1) * (1.0 / MOBA_BLOCK)
    km_hi = km.astype(BF16)
    km_lo = (km - km_hi.astype(F32)).astype(BF16)
    q = q_ref[...]
    gate = _nt_dot(km_hi, q) + _nt_dot(km_lo, q)
    shift = MOBA_BLOCK.bit_length() - 1
    blk = lax.broadcasted_iota(jnp.int32, (nb, seq), 0)
    qblk = jnp.right_shift(lax.broadcasted_iota(jnp.int32, (nb, seq), 1), shift)
    g = jnp.where(blk < qblk, gate, NEG)
    visible = blk == qblk
    for _ in range(MOBA_TOPK):
        mx = jnp.max(g, axis=0, keepdims=True)
        first = jnp.min(jnp.where(g == mx, blk, nb), axis=0, keepdims=True)
        pick = (blk == first) & (mx > 0.5 * NEG)
        visible = visible | pick
        g = jnp.where(pick, NEG, g)
    mask_t = jnp.concatenate([jnp.where(visible, 0.0, NEG), jnp.zeros((LANES - nb, seq), F32)], axis=0)
    qa_ref[:, :MOBA_HEAD_DIM] = q
    qa_ref[:, MOBA_HEAD_DIM:] = mask_t.T.astype(BF16)


def _moba_select(qkv_mo, batch, seq):
    nb = seq // MOBA_BLOCK
    assert MOBA_HEAD_DIM == LANES and nb <= QK_PAD - MOBA_HEAD_DIM
    return pl.pallas_call(
        functools.partial(_moba_select_kernel, seq=seq, nb=nb),
        grid=(batch, MOBA_HEADS),
        in_specs=[pl.BlockSpec((seq, MOBA_HEAD_DIM), lambda b, h: (b, h)),
                  pl.BlockSpec((seq, MOBA_HEAD_DIM), lambda b, h: (b, MOBA_HEADS + h))],
        out_specs=pl.BlockSpec((seq, QK_PAD), lambda b, h: (b, h)),
        out_shape=jax.ShapeDtypeStruct((batch * seq, MOBA_HEADS * QK_PAD), BF16),
        compiler_params=_cparams(("arbitrary", "arbitrary")),
        name="moba_select",
    )(qkv_mo, qkv_mo)


def _t5_kernel(tab_ref, o_ref, *, inv_scale):
    h = pl.program_id(0)
    r = lax.broadcasted_iota(jnp.int32, (LANES, LANES), 0)
    c = lax.broadcasted_iota(jnp.int32, (LANES, LANES), 1)
    max_exact = T5_BUCKETS // 2
    far = tab_ref[T5_BUCKETS - 1, h]

    def block(offset):
        rel = offset + r - c
        n = jnp.maximum(rel, 0)
        nf = jnp.maximum(n, 1).astype(F32)
        large = max_exact + (jnp.log(nf / max_exact) / math.log(T5_MAX_DISTANCE / max_exact)
                             * (T5_BUCKETS - max_exact)).astype(jnp.int32)
        large = jnp.minimum(large, T5_BUCKETS - 1)
        bucket = jnp.where(n < max_exact, n, large)
        bias = jnp.zeros((LANES, LANES), F32)
        for j in range(T5_BUCKETS):
            bias = jnp.where(bucket == j, tab_ref[j, h], bias)
        return jnp.where(rel >= 0, (bias - far) * inv_scale, NEG)

    near = {0: block(0), 1: block(LANES)}
    nblk = ATTN_TILE // LANES
    for d in range(2):
        for i in range(nblk):
            for j in range(nblk):
                k = d * nblk + i - j
                if k < 0:
                    val = jnp.full((LANES, LANES), NEG, F32)
                else:
                    val = near.get(k, jnp.zeros((LANES, LANES), F32))
                o_ref[0, d, i * LANES:(i + 1) * LANES, j * LANES:(j + 1) * LANES] = val


def _t5_tiles(t5_table):
    assert LANES >= T5_MAX_DISTANCE
    tile = ATTN_TILE
    return pl.pallas_call(
        functools.partial(_t5_kernel, inv_scale=math.sqrt(MOBA_HEAD_DIM)),
        grid=(MOBA_HEADS,),
        in_specs=[pl.BlockSpec(memory_space=pltpu.SMEM)],
        out_specs=pl.BlockSpec((1, 2, tile, tile), lambda h: (h, 0, 0, 0)),
        out_shape=jax.ShapeDtypeStruct((MOBA_HEADS, 2, tile, tile), F32),
        compiler_params=_cparams(("arbitrary",)),
        name="t5_tiles",
    )(t5_table)


def _moba_attention(q_wide, qkv_mo, bias, batch, seq):
    tile = ATTN_CHAINS * ATTN_TILE
    nq = seq // tile
    c = math.log2(math.e) / math.sqrt(MOBA_HEAD_DIM)
    dh = MOBA_HEAD_DIM
    return pl.pallas_call(
        functools.partial(_attn_kernel, c=c, masked=True),
        grid=(batch, MOBA_HEADS, nq),
        in_specs=[pl.BlockSpec((tile, QK_PAD), lambda b, h, i: (b * nq + i, h)),
                  pl.BlockSpec((seq, dh), lambda b, h, i: (b, MOBA_HEADS + h)),
                  pl.BlockSpec((seq, dh), lambda b, h, i: (b, 2 * MOBA_HEADS + h)),
                  pl.BlockSpec((1, 2, ATTN_TILE, ATTN_TILE), lambda b, h, i: (h, 0, 0, 0))],
        out_specs=pl.BlockSpec((tile, dh), lambda b, h, i: (b * nq + i, h)),
        out_shape=jax.ShapeDtypeStruct((batch * seq, MOBA_WIDTH), BF16),
        scratch_shapes=_attn_scratch(seq, True),
        compiler_params=_cparams(("arbitrary", "arbitrary", "arbitrary")),
        name="moba_attn",
    )(q_wide, qkv_mo, qkv_mo, bias)


def _outproj_kernel(oa_ref, ob_ref, wa_ref, wb_ref, x_ref, mod_ref, g_ref, b_ref, wr_ref,
                    x1_ref, h2_ref, lg_ref, y_a, y_b, *, alpha, n_tiles):
    i = pl.program_id(0)
    tm, d = y_a.shape
    chunks = 4
    cn, cr = d // chunks, tm // chunks

    def matmul_into(y_ref, c):
        cols = slice(c * cn, (c + 1) * cn)
        y_ref[:, cols] = (jnp.dot(oa_ref[...], wa_ref[:, cols], preferred_element_type=F32)
                          + jnp.dot(ob_ref[...], wb_ref[:, cols], preferred_element_type=F32))

    def epilogue(y_ref, c):
        rows = slice(c * cr, (c + 1) * cr)
        z = alpha * x_ref[rows, :] + mod_ref[0, 2:3, :] * y_ref[rows, :]
        x1 = _ln(z) * g_ref[...] + b_ref[...]
        x1_ref[rows, :] = x1
        h2 = _ln(x1) * (1.0 + mod_ref[0, 4:5, :]) + mod_ref[0, 3:4, :]
        h_hi = h2.astype(BF16)
        h_lo = (h2 - h_hi.astype(F32)).astype(BF16)
        h_bits = pltpu.bitcast(h_hi.astype(F32), jnp.uint32)
        h2_ref[rows, :] = h_bits[:, :d // 2] | jnp.right_shift(h_bits[:, d // 2:], 16)
        zz = (jnp.dot(h_hi, wr_ref[...], preferred_element_type=F32)
              + jnp.dot(h_lo, wr_ref[...], preferred_element_type=F32))
        lg_ref[rows, :] = zz + pltpu.roll(zz, LANES // 2, 1)

    def step(y_new, y_old):
        for c in range(chunks):
            if y_new is not None:
                matmul_into(y_new, c)
            if y_old is not None:
                epilogue(y_old, c)

    inner = (i > 0) & (i < n_tiles)
    pl.when(i == 0)(lambda: step(y_a, None))
    pl.when(inner & (i % 2 == 0))(lambda: step(y_a, y_b))
    pl.when(inner & (i % 2 == 1))(lambda: step(y_b, y_a))
    pl.when(i == n_tiles)(lambda: step(None, y_b if n_tiles % 2 == 0 else y_a))


def _out_proj(o_mla, o_moba, wo, xf, mod3, ln_g, ln_b, wr, seq, alpha):
    t, d = xf.shape
    tm = 512
    n = t // tm
    per_b = seq // tm
    ka, kb = o_mla.shape[1], o_moba.shape[1]
    assert ka == kb and wo.shape[0] == ka + kb
    once = pl.Buffered(1)

    def cur(i):
        return (jnp.minimum(i, n - 1), 0)

    def lag(i):
        return (jnp.maximum(i - 1, 0), 0)

    return pl.pallas_call(
        functools.partial(_outproj_kernel, alpha=alpha, n_tiles=n),
        grid=(n + 1,),
        in_specs=[pl.BlockSpec((tm, ka), cur),
                  pl.BlockSpec((tm, kb), cur),
                  pl.BlockSpec((ka, d), lambda i: (0, 0), pipeline_mode=once),
                  pl.BlockSpec((kb, d), lambda i: (1, 0), pipeline_mode=once),
                  pl.BlockSpec((tm, d), lag),
                  pl.BlockSpec((1, 6, d), lambda i: (jnp.maximum(i - 1, 0) // per_b, 0, 0)),
                  pl.BlockSpec((1, d), lambda i: (0, 0)),
                  pl.BlockSpec((1, d), lambda i: (0, 0)),
                  pl.BlockSpec((d, LANES), lambda i: (0, 0), pipeline_mode=once)],
        out_specs=[pl.BlockSpec((tm, d), lag),
                   pl.BlockSpec((tm, d // 2), lag),
                   pl.BlockSpec((tm, LANES), lag)],
        out_shape=[jax.ShapeDtypeStruct((t, d), F32),
                   jax.ShapeDtypeStruct((t, d // 2), jnp.uint32),
                   jax.ShapeDtypeStruct((t, LANES), F32)],
        scratch_shapes=[pltpu.VMEM((tm, d), F32), pltpu.VMEM((tm, d), F32)],
        compiler_params=_cparams(("arbitrary",)),
        name="out_proj",
    )(o_mla, o_moba, wo, wo, xf, mod3, ln_g, ln_b, wr)


def _route_kernel(lg_ref, br_ref, info_ref, cnt_ref, run_scr, *, tm):
    i = pl.program_id(0)

    @pl.when(i == 0)
    def _():
        run_scr[...] = jnp.zeros_like(run_scr)

    lg = lg_ref[...] + br_ref[...]
    lane = lax.broadcasted_iota(jnp.int32, (tm, LANES), 1)
    e_lo, e_hi = MOE_GROUPS, MOE_GROUPS + MOE_N_EXPERTS
    is_g = lane < e_lo
    gl = jnp.where(is_g, lg, NEG)
    gmax = jnp.max(gl, axis=-1, keepdims=True)
    gidx = jnp.min(jnp.where(gl == gmax, lane, LANES), axis=-1, keepdims=True)
    g_p = 1.0 / jnp.sum(jnp.where(is_g, jnp.exp(gl - gmax), 0.0), axis=-1, keepdims=True)
    grp_of_lane = jnp.right_shift(lane - e_lo, MOE_EXPERTS_PER_GROUP.bit_length() - 1)
    in_grp = (lane >= e_lo) & (lane < e_hi) & (grp_of_lane == gidx)
    el = jnp.where(in_grp, lg, NEG)
    m1 = jnp.max(el, axis=-1, keepdims=True)
    l1 = jnp.min(jnp.where(el == m1, lane, LANES), axis=-1, keepdims=True)
    el2 = jnp.where(lane == l1, NEG, el)
    m2 = jnp.max(el2, axis=-1, keepdims=True)
    l2 = jnp.min(jnp.where(el2 == m2, lane, LANES), axis=-1, keepdims=True)
    zsum = jnp.sum(jnp.where(in_grp, jnp.exp(el - m1), 0.0), axis=-1, keepdims=True)
    p1 = 1.0 / zsum
    p2 = jnp.exp(m2 - m1) / zsum
    wa = g_p * (p1 / (p1 + p2))
    wb = g_p * (p2 / (p1 + p2))
    hot_a = lane == l1
    hot_b = lane == l2
    onehot = jnp.where(hot_a | hot_b, 1.0, 0.0)
    r = lax.broadcasted_iota(jnp.int32, (tm, tm), 0)
    c = lax.broadcasted_iota(jnp.int32, (tm, tm), 1)
    lower = jnp.where(c < r, 1.0, 0.0).astype(BF16)
    before = jnp.dot(lower, onehot.astype(BF16), preferred_element_type=F32) + run_scr[...]
    rank_a = jnp.sum(jnp.where(hot_a, before, 0.0), axis=-1, keepdims=True)
    rank_b = jnp.sum(jnp.where(hot_b, before, 0.0), axis=-1, keepdims=True)
    run_scr[...] += jnp.sum(onehot, axis=0, keepdims=True)
    info = jnp.zeros((tm, LANES), F32)
    for k, val in enumerate([(l1 - e_lo).astype(F32), (l2 - e_lo).astype(F32), wa, wb, rank_a, rank_b]):
        info = jnp.where(lane == k, val, info)
    info_ref[...] = info
    cnt_ref[...] = run_scr[...]


def _route(logits, br):
    t = logits.shape[0]
    tm = 512
    return pl.pallas_call(
        functools.partial(_route_kernel, tm=tm),
        grid=(t // tm,),
        in_specs=[pl.BlockSpec((tm, LANES), lambda i: (i, 0)),
                  pl.BlockSpec((1, LANES), lambda i: (0, 0))],
        out_specs=[pl.BlockSpec((tm, LANES), lambda i: (i, 0)),
                   pl.BlockSpec((1, LANES), lambda i: (0, 0))],
        out_shape=[jax.ShapeDtypeStruct((t, LANES), F32),
                   jax.ShapeDtypeStruct((1, LANES), F32)],
        scratch_shapes=[pltpu.VMEM((1, LANES), F32)],
        compiler_params=_cparams(("arbitrary",)),
        name="route",
    )(logits, br)


def _pos_kernel(info_ref, start_ref, pos_ref):
    info = info_ref[...]
    tm = info.shape[0]
    lane = lax.broadcasted_iota(jnp.int32, (tm, LANES), 1)
    start = start_ref[...]
    cols = []
    for k in range(2):
        e = jnp.sum(jnp.where(lane == k, info, 0.0), axis=-1, keepdims=True).astype(jnp.int32)
        rank = jnp.sum(jnp.where(lane == 4 + k, info, 0.0), axis=-1, keepdims=True)
        base = jnp.sum(jnp.where(lane == e + MOE_GROUPS, start, 0.0), axis=-1, keepdims=True)
        cols.append(base + rank)
    wide = jnp.where(lane == 0, cols[0], jnp.where(lane == 1, cols[1], 0.0))
    pos_ref[...] = wide.T[:pos_ref.shape[0], :].astype(jnp.int32)


def _positions(info, start_lanes):
    t = info.shape[0]
    tm = 1024
    return pl.pallas_call(
        _pos_kernel,
        grid=(t // tm,),
        in_specs=[pl.BlockSpec((tm, LANES), lambda i: (i, 0)),
                  pl.BlockSpec((1, LANES), lambda i: (0, 0))],
        out_specs=pl.BlockSpec((8, tm), lambda i: (0, i)),
        out_shape=jax.ShapeDtypeStruct((8, t), jnp.int32),
        compiler_params=_cparams(("arbitrary",)),
        name="positions",
    )(info, start_lanes)


def _row_copy(src_ref, src_row, dst_ref, dst_row, sem):
    return pltpu.make_async_copy(src_ref.at[pl.ds(src_row, 1)], dst_ref.at[pl.ds(dst_row, 1)], sem)


def _dispatch_kernel(pa_ref, pb_ref, pad0_ref, padn_ref, used_ref, h_ref, xs_ref, ztile, sem, zsem,
                     *, tm, tr, n_tiles):
    i = pl.program_id(0)
    base = i * tm

    def zero_fill(act):
        def whole_tile(j, _):
            act(pltpu.make_async_copy(ztile, xs_ref.at[pl.ds(pl.multiple_of(j * tr, tr), tr)], zsem))
            return 0

        lax.fori_loop(used_ref[0], n_tiles, whole_tile, 0)

        def expert_pad(e, _):
            n, start = padn_ref[e], pad0_ref[e]
            head = jnp.minimum(jnp.bitwise_and(-start, SUBLANES - 1), n)
            for k in range(SUBLANES - 1):
                pl.when(k < head)(functools.partial(act, _row_copy(ztile, 0, xs_ref, start + k, zsem)))
            off = start + head
            groups = jnp.right_shift(n - head, SUBLANES.bit_length() - 1)
            for bit in reversed(range((tr // SUBLANES - 1).bit_length())):
                size = SUBLANES << bit
                take = jnp.bitwise_and(jnp.right_shift(groups, bit), 1)
                dst = xs_ref.at[pl.ds(pl.multiple_of(off, SUBLANES), size)]
                pl.when(take == 1)(functools.partial(act, pltpu.make_async_copy(ztile.at[pl.ds(0, size)], dst, zsem)))
                off = off + take * size
            return 0

        lax.fori_loop(0, MOE_N_EXPERTS, expert_pad, 0)

    @pl.when(i == 0)
    def _():
        ztile[...] = jnp.zeros(ztile.shape, ztile.dtype)
        zero_fill(lambda cp: cp.start())

    for t in range(tm):
        _row_copy(h_ref, t, xs_ref, pa_ref[base + t], sem).start()
        _row_copy(h_ref, t, xs_ref, pb_ref[base + t], sem).start(priority=1)
    for _ in range(2):
        pltpu.make_async_copy(h_ref, xs_ref.at[pl.ds(0, tm)], sem).wait()

    @pl.when(i == pl.num_programs(0) - 1)
    def _():
        zero_fill(lambda cp: cp.wait())


def _dispatch(pos_a, pos_b, pad0, padn, used, h2, tr, n_tiles):
    t, d = h2.shape
    tm = 1024
    grid_spec = pltpu.PrefetchScalarGridSpec(
        num_scalar_prefetch=5,
        grid=(t // tm,),
        in_specs=[pl.BlockSpec((tm, d), lambda i, *_: (i, 0))],
        out_specs=pl.BlockSpec(memory_space=pl.ANY),
        scratch_shapes=[pltpu.VMEM((tr, d), h2.dtype), pltpu.SemaphoreType.DMA(()), pltpu.SemaphoreType.DMA(())],
    )
    return pl.pallas_call(
        functools.partial(_dispatch_kernel, tm=tm, tr=tr, n_tiles=n_tiles),
        grid_spec=grid_spec,
        out_shape=jax.ShapeDtypeStruct((n_tiles * tr, d), h2.dtype),
        compiler_params=_cparams(("arbitrary",), row_dma=True),
        name="dispatch",
    )(pos_a, pos_b, pad0, padn, used, h2)


def _experts_kernel(texp_ref, tidx_ref, fetch_ref, slot_ref, used_ref, x_ref, w1_hbm, w3_hbm, w2_hbm, o_ref,
                    w1_f32, w3_f32, w2_f32, w1_scr, w3_scr, w2_scr, sems):
    i = pl.program_id(0)
    n_steps = pl.num_programs(0)
    prev = texp_ref[jnp.maximum(i - 1, 0)]
    first_of_expert = (i == 0) | (texp_ref[i] != prev)
    slot = slot_ref[i]

    def weight_copies(e, s):
        return [pltpu.make_async_copy(w1_hbm.at[e], w1_f32.at[s], sems.at[s, 0]),
                pltpu.make_async_copy(w3_hbm.at[e], w3_f32.at[s], sems.at[s, 1]),
                pltpu.make_async_copy(w2_hbm.at[e], w2_f32.at[s], sems.at[s, 2])]

    @pl.when(i == 0)
    def _():
        for s in range(2):
            for cp in weight_copies(fetch_ref[n_steps + s], s):
                cp.start(priority=1)

    def tile(cast):
        bits = x_ref[...]
        x = jnp.concatenate(
            [pltpu.bitcast(jnp.bitwise_and(bits, jnp.uint32(0xFFFF0000)), F32).astype(BF16),
             pltpu.bitcast(jnp.left_shift(bits, 16), F32).astype(BF16)], axis=1)
        if cast:
            w1_scr[...] = w1_f32[slot].astype(BF16)
        a = jnp.dot(x, w1_scr[...], preferred_element_type=F32)
        if cast:
            w3_scr[...] = w3_f32[slot].astype(BF16)
        b = jnp.dot(x, w3_scr[...], preferred_element_type=F32)
        if cast:
            w2_scr[...] = w2_f32[slot].astype(BF16)
            for cp in weight_copies(fetch_ref[i], slot):
                cp.start(priority=1)
        hid = (a / (1.0 + jnp.exp(-a))) * b
        o_ref[...] = jnp.dot(hid.astype(BF16), w2_scr[...], preferred_element_type=F32)

    @pl.when(first_of_expert)
    def _():
        for cp in weight_copies(texp_ref[i], slot):
            cp.wait()
        tile(cast=True)

    @pl.when(jnp.logical_not(first_of_expert) & (i < used_ref[0]))
    def _():
        tile(cast=False)

    @pl.when(i == n_steps - 1)
    def _():
        for s in range(2):
            for cp in weight_copies(texp_ref[i], s):
                cp.wait()

    @pl.when(i >= used_ref[0])
    def _():
        o_ref[...] = jnp.zeros_like(o_ref)


def _experts(texp, tidx, fetch, slot, used, xs, w1, w3, w2, tr):
    rows = xs.shape[0]
    nt = rows // tr
    d, f = w1.shape[1:]
    assert xs.shape[1] * 2 == d and xs.dtype == jnp.uint32
    grid_spec = pltpu.PrefetchScalarGridSpec(
        num_scalar_prefetch=5,
        grid=(nt,),
        in_specs=[pl.BlockSpec((tr, d // 2), lambda i, te, ti, *_: (ti[i], 0)),
                  pl.BlockSpec(memory_space=pl.ANY),
                  pl.BlockSpec(memory_space=pl.ANY),
                  pl.BlockSpec(memory_space=pl.ANY)],
        out_specs=pl.BlockSpec((tr, d), lambda i, *_: (i, 0)),
        scratch_shapes=[pltpu.VMEM((2, d, f), F32), pltpu.VMEM((2, d, f), F32), pltpu.VMEM((2, f, d), F32),
                        pltpu.VMEM((d, f), BF16), pltpu.VMEM((d, f), BF16), pltpu.VMEM((f, d), BF16),
                        pltpu.SemaphoreType.DMA((2, 3))],
    )
    return pl.pallas_call(
        _experts_kernel,
        grid_spec=grid_spec,
        out_shape=jax.ShapeDtypeStruct((rows, d), F32),
        compiler_params=_cparams(("arbitrary",)),
        name="experts",
    )(texp, tidx, fetch, slot, used, xs, w1, w3, w2)


def _combine_kernel(pa_ref, pb_ref, ys_ref, x1_ref, info_ref, mod_ref, g_ref, b_ref, o_ref,
                    buf_a, buf_b, sems, *, tm, alpha):
    i = pl.program_id(0)
    last = pl.num_programs(0) - 1

    def issue_row(tile, slot, t):
        _row_copy(ys_ref, pa_ref[tile * tm + t], buf_a.at[slot], t, sems.at[slot]).start()
        _row_copy(ys_ref, pb_ref[tile * tm + t], buf_b.at[slot], t, sems.at[slot]).start(priority=1)

    def combine_rows(slot, rows):
        info = info_ref[rows, :]
        y = info[:, 2:3] * buf_a[slot, rows, :] + info[:, 3:4] * buf_b[slot, rows, :]
        z = alpha * x1_ref[rows, :] + mod_ref[0, 5:6, :] * y
        o_ref[rows, :] = _ln(z) * g_ref[...] + b_ref[...]

    @pl.when(i == 0)
    def _():
        lax.fori_loop(0, tm, lambda t, _: (issue_row(0, 0, t), 0)[1], 0, unroll=8)

    slot = i % 2
    for buf in (buf_a, buf_b):
        pltpu.make_async_copy(ys_ref.at[pl.ds(0, tm)], buf.at[slot], sems.at[slot]).wait()

    chunk = 32

    @pl.when(i < last)
    def _():
        for r in range(0, tm, chunk):
            for t in range(r, r + chunk):
                issue_row(i + 1, 1 - slot, t)
            combine_rows(slot, slice(r, r + chunk))

    @pl.when(i == last)
    def _():
        combine_rows(slot, slice(0, tm))


def _combine(pos_a, pos_b, ys, x1, info, mod3, ln_g, ln_b, seq, alpha):
    t, d = x1.shape
    tm = 512
    per_b = seq // tm
    grid_spec = pltpu.PrefetchScalarGridSpec(
        num_scalar_prefetch=2,
        grid=(t // tm,),
        in_specs=[pl.BlockSpec(memory_space=pl.ANY),
                  pl.BlockSpec((tm, d), lambda i, pa, pb: (i, 0)),
                  pl.BlockSpec((tm, LANES), lambda i, pa, pb: (i, 0)),
                  pl.BlockSpec((1, 6, d), lambda i, pa, pb: (i // per_b, 0, 0)),
                  pl.BlockSpec((1, d), lambda i, pa, pb: (0, 0)),
                  pl.BlockSpec((1, d), lambda i, pa, pb: (0, 0))],
        out_specs=pl.BlockSpec((tm, d), lambda i, pa, pb: (i, 0)),
        scratch_shapes=[pltpu.VMEM((2, tm, d), F32), pltpu.VMEM((2, tm, d), F32), pltpu.SemaphoreType.DMA((2,))],
    )
    return pl.pallas_call(
        functools.partial(_combine_kernel, tm=tm, alpha=alpha),
        grid_spec=grid_spec,
        out_shape=jax.ShapeDtypeStruct((t, d), F32),
        compiler_params=_cparams(("arbitrary",), row_dma=True),
        name="combine",
    )(pos_a, pos_b, ys, x1, info, mod3, ln_g, ln_b)


def _prep_w_in_kernel(w_ref, o_ref):
    r0 = MLA_Q_RANK + MLA_KV_RANK
    r1 = r0 + MLA_ROPE_DIM
    half = MLA_ROPE_DIM // 2
    z = jnp.zeros((LANES - MLA_ROPE_DIM, o_ref.shape[1]), BF16)
    o_ref[:r1, :] = w_ref[:r1, :].astype(BF16)
    o_ref[r1:r0 + LANES, :] = z
    o_ref[r0 + LANES:r0 + LANES + half, :] = w_ref[r0 + half:r1, :].astype(BF16)
    o_ref[r0 + LANES + half:r0 + LANES + MLA_ROPE_DIM, :] = w_ref[r0:r0 + half, :].astype(BF16)
    o_ref[r0 + LANES + MLA_ROPE_DIM:PART_A, :] = z
    o_ref[PART_A:, :] = w_ref[r1:, :].astype(BF16)


def _prep_w_in(w_t):
    n, k = w_t.shape
    tk = 512
    n_out = PART_A + 3 * MOBA_WIDTH
    assert n == MLA_Q_RANK + MLA_KV_RANK + MLA_ROPE_DIM + 3 * MOBA_WIDTH
    return pl.pallas_call(
        _prep_w_in_kernel,
        grid=(k // tk,),
        in_specs=[pl.BlockSpec((n, tk), lambda i: (0, i))],
        out_specs=pl.BlockSpec((n_out, tk), lambda i: (0, i)),
        out_shape=jax.ShapeDtypeStruct((n_out, k), BF16),
        compiler_params=_cparams(("arbitrary",)),
        name="prep_w_in",
    )(w_t)


def _prep_w_uq(w):
    r = w.shape[0]
    w = w.reshape(r, MLA_HEADS, MLA_NOPE_DIM + MLA_ROPE_DIM)
    half = MLA_ROPE_DIM // 2
    nope = w[:, :, :MLA_NOPE_DIM]
    x1 = w[:, :, MLA_NOPE_DIM:MLA_NOPE_DIM + half]
    x2 = w[:, :, MLA_NOPE_DIM + half:]
    assert 2 * MLA_ROPE_DIM == LANES
    return jnp.concatenate([nope, x1, x2, x2, x1], axis=2).reshape(r, MLA_HEADS * QK_PAD).astype(BF16)


def _rope_lanes(seq):
    inv = 1.0 / (ROPE_THETA ** (jnp.arange(0, MLA_ROPE_DIM, 2, dtype=F32) / MLA_ROPE_DIM))
    hi = (jnp.arange(seq // LANES, dtype=F32) * LANES)[:, None] * inv[None, :]
    lo = jnp.arange(LANES, dtype=F32)[:, None] * inv[None, :]
    ch, sh, cl, sl = jnp.cos(hi)[:, None, :], jnp.sin(hi)[:, None, :], jnp.cos(lo)[None], jnp.sin(lo)[None]
    cos = (ch * cl - sh * sl).reshape(seq, -1)
    sin = (sh * cl + ch * sl).reshape(seq, -1)
    z = jnp.zeros((seq, LANES - MLA_ROPE_DIM), F32)
    return jnp.concatenate([cos, cos, z], axis=1), jnp.concatenate([-sin, sin, z], axis=1)


def _prep_router(w_rg, b_rg, w_re, b_re):
    d = w_rg.shape[0]
    w = jnp.concatenate([w_rg, w_re], axis=1)
    n = w.shape[1]
    hi = w.astype(BF16)
    lo = (w - hi.astype(F32)).astype(BF16)
    z = jnp.zeros((d, LANES // 2 - n), BF16)
    wr = jnp.concatenate([hi, z, lo, z], axis=1)
    br = jnp.zeros((1, LANES), F32).at[0, :n].set(jnp.concatenate([b_rg, b_re]))
    return wr, br


def _layer(xf, mod3, batch, seq, depth_alpha, w_in, q_norm_g, w_uq, kv_norm_g, w_ukv, w_out, bias_tiles,
           cos_t, sin_t, ln1_g, ln1_b, w_rg, b_rg, w_re, b_re, w1, w3, w2, ln2_g, ln2_b):
    t, d = xf.shape
    part_a, qkv_mo = _in_proj(xf, mod3, _prep_w_in(w_in.T), seq)
    q, k, v = _qkv(part_a, _prep_w_uq(w_uq), w_ukv.astype(BF16), q_norm_g.reshape(1, -1),
                   kv_norm_g.reshape(1, -1), cos_t, sin_t, seq)
    o_mla = _mla_attention(q, k, v, batch, seq)
    q_wide = _moba_select(qkv_mo, batch, seq)
    o_moba = _moba_attention(q_wide, qkv_mo, bias_tiles, batch, seq)
    wo = w_out.astype(BF16)
    wr, br = _prep_router(w_rg, b_rg, w_re, b_re)
    x1, h2, logits = _out_proj(o_mla, o_moba, wo, xf, mod3, ln1_g.reshape(1, d),
                               ln1_b.reshape(1, d), wr, seq, depth_alpha)
    info, counts = _route(logits, br)
    tr = 256
    nt = (2 * t) // tr + MOE_N_EXPERTS
    cnt = counts[0, MOE_GROUPS:MOE_GROUPS + MOE_N_EXPERTS].astype(jnp.int32)
    ntile = (cnt + tr - 1) // tr
    tile_end = jnp.cumsum(ntile)
    tile_start = tile_end - ntile
    used = tile_end[-1]
    start_lanes = jnp.zeros((1, LANES), F32).at[0, MOE_GROUPS:MOE_GROUPS + MOE_N_EXPERTS].set(
        (tile_start * tr).astype(F32))
    pos = _positions(info, start_lanes)
    pos_a, pos_b = pos[0], pos[1]
    tidx = jnp.minimum(jnp.arange(nt, dtype=jnp.int32), used - 1)
    texp = jnp.sum(tidx[:, None] >= tile_end[None, :], axis=1).astype(jnp.int32)
    eids = jnp.arange(MOE_N_EXPERTS, dtype=jnp.int32)
    owns = ntile > 0
    position = jnp.cumsum(owns) - 1
    two_later = owns[None, :] & (position[None, :] == position[:, None] + 2)
    fetch_of = jnp.where(two_later.any(axis=1), jnp.argmax(two_later, axis=1), eids).astype(jnp.int32)
    by_position = jnp.argmax(owns[None, :] & (position[None, :] == jnp.arange(2)[:, None]), axis=1)
    fetch = jnp.concatenate([fetch_of[texp], by_position.astype(jnp.int32)])
    slot = (position[texp] % 2).astype(jnp.int32)
    used1 = used.reshape(1).astype(jnp.int32)
    pad0 = (tile_start * tr + cnt).astype(jnp.int32)
    padn = (ntile * tr - cnt).astype(jnp.int32)
    xs = _dispatch(pos_a, pos_b, pad0, padn, used1, h2, tr, nt)
    ys = _experts(texp, tidx, fetch, slot, used1, xs, w1, w3, w2, tr)
    return _combine(pos_a, pos_b, ys, x1, info, mod3, ln2_g.reshape(1, d), ln2_b.reshape(1, d), seq, depth_alpha)


def kernel(x, c, w_ada, b_ada, w_in, q_norm_g, w_uq, kv_norm_g, w_ukv, w_out, t5_table, ln1_g, ln1_b,
           w_router_group, b_router_group, w_router_expert, b_router_expert, w1, w3, w2, ln2_g, ln2_b):
    batch, seq, d = x.shape
    depth = w_ada.shape[0]
    alpha = (2.0 * depth) ** 0.25
    cos_t, sin_t = _rope_lanes(seq)
    bias_tiles = _t5_tiles(t5_table)
    xf = x.reshape(batch * seq, d)
    for l in range(depth):
        mod3 = _ada_mod(c, w_ada[l], b_ada[l]).reshape(batch, 6, d)
        xf = _layer(xf, mod3, batch, seq, alpha, w_in[l], q_norm_g[l], w_uq[l], kv_norm_g[l], w_ukv[l],
                    w_out[l], bias_tiles, cos_t, sin_t, ln1_g[l], ln1_b[l], w_router_group[l],
                    b_router_group[l], w_router_expert[l], b_router_expert[l], w1[l], w3[l], w2[l],
                    ln2_g[l], ln2_b[l])
    return xf.reshape(batch, seq, d)
```
